```python
import jax, jax.numpy as jnp
from jax import lax
import numpy as np

D_MODEL = 1024
BATCH = 2
SEQ = 8192
DEPTH = 1

EPS = 1e-6
NEG_INF = -1e30
FORCE_SCORE = 1e9
ROPE_THETA = 500000.0
Q_BLOCK = 128

A_HEADS = 8
A_KV_HEADS = 2
A_HEAD_DIM = 64
A_ROPE_DIM = A_HEAD_DIM // 4
CMP_LEN = 32
CMP_STRIDE = 16
CMP_HIDDEN = 256
SLC_LEN = 64
SLC_TOPK = 16
WINDOW = 512
A_WIDTH = A_HEADS * A_HEAD_DIM
A_KV_WIDTH = A_KV_HEADS * A_HEAD_DIM

B_HEADS = 8
Q_LORA = 256
KV_LORA = 128
B_NOPE = 64
B_ROPE = 32
B_V = 64
B_QK = B_NOPE + B_ROPE
B_WIDTH = B_HEADS * B_V

MIX_WIDTH = A_WIDTH + B_WIDTH
IN_SIZES = (A_WIDTH, A_KV_WIDTH, A_KV_WIDTH, A_KV_WIDTH, A_KV_WIDTH, A_KV_WIDTH, A_KV_WIDTH,
            3 * A_HEADS, Q_LORA, KV_LORA, B_ROPE)
IN_WIDTH = sum(IN_SIZES)

P_HEADS = 8
N_KEYS = 128
N_EXPERTS = N_KEYS * N_KEYS
P_KEY_DIM = 256
P_TOPK = 16
P_CHUNK = 128

kernel_name = 'hybrid_nsa_mla_peer_layer'


def _rms(x):
    xf = x.astype(jnp.float32)
    return xf * lax.rsqrt(jnp.mean(xf * xf, axis=-1, keepdims=True) + EPS)


def rmsnorm(x, gain):
    return (_rms(x) * gain.astype(jnp.float32)).astype(x.dtype)


def rope_tables(pos, rot_dim):
    inv_freq = ROPE_THETA ** (-jnp.arange(0, rot_dim, 2, dtype=jnp.float32) / rot_dim)
    ang = pos.astype(jnp.float32)[..., None] * inv_freq
    return jnp.cos(ang), jnp.sin(ang)


def apply_rope(x, cos, sin):
    half = cos.shape[-1]
    xf = x.astype(jnp.float32)
    x1, x2, rest = xf[..., :half], xf[..., half:2 * half], xf[..., 2 * half:]
    out = jnp.concatenate([x1 * cos - x2 * sin, x2 * cos + x1 * sin, rest], axis=-1)
    return out.astype(x.dtype)


def masked_softmax(s, mask):
    s = jnp.where(mask, s.astype(jnp.float32), NEG_INF)
    p = jax.nn.softmax(s, axis=-1)
    return jnp.where(mask, p, 0.0)


def nsa_mixer(q, k_cmp, v_cmp, k_slc, v_slc, k_win, v_win, gates, positions,
              q_gain, kc_gain, ks_gain, kw_gain, cmp_pos, cmp_k_w1, cmp_k_w2, cmp_v_w1, cmp_v_w2):
    B, S = q.shape[:2]
    G, R, dk = A_KV_HEADS, A_HEADS // A_KV_HEADS, A_HEAD_DIM
    scale = dk ** -0.5
    cos, sin = rope_tables(positions, A_ROPE_DIM)
    cos_h, sin_h = cos[:, :, None, :], sin[:, :, None, :]
    q = apply_rope(rmsnorm(q, q_gain), cos_h, sin_h)
    k_slc = apply_rope(rmsnorm(k_slc, ks_gain), cos_h, sin_h)
    k_win = apply_rope(rmsnorm(k_win, kw_gain), cos_h, sin_h)

    n_cmp = (S - CMP_LEN) // CMP_STRIDE + 1
    cmp_idx = jnp.arange(n_cmp)[:, None] * CMP_STRIDE + jnp.arange(CMP_LEN)[None, :]
    cmp_start, cmp_end = cmp_idx[:, 0], cmp_idx[:, -1]

    def compress(t, w1, w2):
        blk = t[:, cmp_idx] + cmp_pos[:, None, :]
        blk = blk.transpose(0, 1, 3, 2, 4).reshape(B, n_cmp, G, CMP_LEN * dk)
        return jax.nn.gelu(blk @ w1) @ w2

    kc = compress(k_cmp, cmp_k_w1, cmp_k_w2)
    vc = compress(v_cmp, cmp_v_w1, cmp_v_w2)
    cos_c, sin_c = rope_tables(positions[:, cmp_end], A_ROPE_DIM)
    kc = apply_rope(rmsnorm(kc, kc_gain), cos_c[:, :, None, :], sin_c[:, :, None, :])

    n_slc = S // SLC_LEN
    topk = min(SLC_TOPK, n_slc)
    slc_start = jnp.arange(n_slc) * SLC_LEN
    overlap = ((cmp_start[:, None] < slc_start[None, :] + SLC_LEN)
               & (cmp_end[:, None] >= slc_start[None, :])).astype(jnp.float32)
    ks_blocks = k_slc.reshape(B, n_slc, SLC_LEN, G, dk).transpose(0, 3, 1, 2, 4)
    vs_blocks = v_slc.reshape(B, n_slc, SLC_LEN, G, dk).transpose(0, 3, 1, 2, 4)

    pad = ((0, 0), (WINDOW, 0), (0, 0), (0, 0))
    kw_pad, vw_pad = jnp.pad(k_win, pad), jnp.pad(v_win, pad)

    def block(qb):
        q0 = qb * Q_BLOCK
        t = q0 + jnp.arange(Q_BLOCK)
        qblk = lax.dynamic_slice_in_dim(q, q0, Q_BLOCK, axis=1).reshape(B, Q_BLOCK, G, R, dk)
        gblk = lax.dynamic_slice_in_dim(gates, q0, Q_BLOCK, axis=1).reshape(B, Q_BLOCK, G, R, 3)

        s_c = jnp.einsum('bqgrd,bngd->bgrqn', qblk, kc) * scale
        p_c = masked_softmax(s_c, cmp_end[None, :] <= t[:, None])
        o_c = jnp.einsum('bgrqn,bngd->bqgrd', p_c.astype(vc.dtype), vc)

        imp = jnp.einsum('bgrqn,nj->bgqj', p_c, overlap)
        j = jnp.arange(n_slc)
        forced = (j[None, :] == (t // SLC_LEN)[:, None]) | (j[None, :] == 0)
        causal_blk = slc_start[None, :] <= t[:, None]
        imp = jnp.where(forced, FORCE_SCORE, jnp.where(causal_blk, imp, NEG_INF))
        _, sel = lax.top_k(imp, topk)
        flat = sel.reshape(B, G, Q_BLOCK * topk)[..., None, None]
        ks = jnp.take_along_axis(ks_blocks, flat, axis=2).reshape(B, G, Q_BLOCK, topk * SLC_LEN, dk)
        vs = jnp.take_along_axis(vs_blocks, flat, axis=2).reshape(B, G, Q_BLOCK, topk * SLC_LEN, dk)
        tok = (sel[..., None] * SLC_LEN + jnp.arange(SLC_LEN)).reshape(B, G, Q_BLOCK, topk * SLC_LEN)
        mask_s = (tok <= t[:, None])[:, :, None]
        s_s = jnp.einsum('bqgrd,bgqsd->bgrqs', qblk, ks) * scale
        p_s = masked_softmax(s_s, mask_s)
        o_s = jnp.einsum('bgrqs,bgqsd->bqgrd', p_s.astype(vs.dtype), vs)

        kw = lax.dynamic_slice_in_dim(kw_pad, q0, Q_BLOCK + WINDOW, axis=1)
        vw = lax.dynamic_slice_in_dim(vw_pad, q0, Q_BLOCK + WINDOW, axis=1)
        s_pos = q0 - WINDOW + jnp.arange(Q_BLOCK + WINDOW)
        mask_w = ((s_pos[None, :] <= t[:, None]) & (s_pos[None, :] > t[:, None] - WINDOW)
                  & (s_pos[None, :] >= 0))
        s_w = jnp.einsum('bqgrd,bsgd->bgrqs', qblk, kw) * scale
        p_w = masked_softmax(s_w, mask_w)
        o_w = jnp.einsum('bgrqs,bsgd->bqgrd', p_w.astype(vw.dtype), vw)

        return gblk[..., 0:1] * o_c + gblk[..., 1:2] * o_s + gblk[..., 2:3] * o_w

    out = lax.map(block, jnp.arange(S // Q_BLOCK))
    return out.transpose(1, 0, 2, 3, 4, 5).reshape(B, S, A_WIDTH)


def mla_mixer(c_q, c_kv, k_rope, positions, q_lora_gain, w_uq, kv_lora_gain, w_ukv, q_gain, k_gain):
    B, S = c_q.shape[:2]
    H = B_HEADS
    cos, sin = rope_tables(positions, B_ROPE)
    cos_h, sin_h = cos[:, :, None, :], sin[:, :, None, :]
    q = (rmsnorm(c_q, q_lora_gain) @ w_uq).reshape(B, S, H, B_QK)
    kv = (rmsnorm(c_kv, kv_lora_gain) @ w_ukv).reshape(B, S, H, B_NOPE + B_V)
    k_nope, v = kv[..., :B_NOPE], kv[..., B_NOPE:]
    k = jnp.concatenate([k_nope, jnp.broadcast_to(k_rope[:, :, None, :], (B, S, H, B_ROPE))], axis=-1)
    q, k = rmsnorm(q, q_gain), rmsnorm(k, k_gain)
    q = jnp.concatenate([q[..., :B_NOPE], apply_rope(q[..., B_NOPE:], cos_h, sin_h)], axis=-1)
    k = jnp.concatenate([k[..., :B_NOPE], apply_rope(k[..., B_NOPE:], cos_h, sin_h)], axis=-1)
    scale = B_QK ** -0.5
    key_idx = jnp.arange(S)

    def block(qb):
        q0 = qb * Q_BLOCK
        t = q0 + jnp.arange(Q_BLOCK)
        qblk = lax.dynamic_slice_in_dim(q, q0, Q_BLOCK, axis=1)
        s = jnp.einsum('bqhd,bshd->bhqs', qblk, k) * scale
        p = masked_softmax(s, key_idx[None, :] <= t[:, None])
        return jnp.einsum('bhqs,bshd->bqhd', p.astype(v.dtype), v)

    out = lax.map(block, jnp.arange(S // Q_BLOCK))
    return out.transpose(1, 0, 2, 3, 4).reshape(B, S, B_WIDTH)


def peer_ffn(x, w_q, sub_keys, expert_u, expert_v):
    B, S, D = x.shape
    T = B * S
    xt = x.reshape(T, D)
    q = _rms((xt @ w_q).reshape(T, P_HEADS, P_KEY_DIM)).astype(x.dtype)
    half = P_KEY_DIM // 2
    s1 = jnp.einsum('thd,hnd->thn', q[..., :half], sub_keys[:, 0]).astype(jnp.float32)
    s2 = jnp.einsum('thd,hnd->thn', q[..., half:], sub_keys[:, 1]).astype(jnp.float32)
    v1, i1 = lax.top_k(s1, P_TOPK)
    v2, i2 = lax.top_k(s2, P_TOPK)
    cand = (v1[..., :, None] + v2[..., None, :]).reshape(T, P_HEADS, P_TOPK * P_TOPK)
    vals, ci = lax.top_k(cand, P_TOPK)
    e = (jnp.take_along_axis(i1, ci // P_TOPK, axis=-1) * N_KEYS
         + jnp.take_along_axis(i2, ci % P_TOPK, axis=-1))
    g = jax.nn.softmax(vals, axis=-1)
    n_chunks = T // P_CHUNK

    def chunk(args):
        xc, ec, gc = args
        a = jnp.einsum('cd,ckd->ck', xc, expert_u[ec])
        w = (gc * jax.nn.gelu(a.astype(jnp.float32))).astype(x.dtype)
        return jnp.einsum('ck,ckd->cd', w, expert_v[ec])

    y = lax.map(chunk, (xt.reshape(n_chunks, P_CHUNK, D),
                        e.reshape(n_chunks, P_CHUNK, P_HEADS * P_TOPK),
                        g.reshape(n_chunks, P_CHUNK, P_HEADS * P_TOPK)))
    return y.reshape(B, S, D)


def hybrid_layer(x, positions, norm1_gain, w_in,
                 nsa_q_gain, nsa_kc_gain, nsa_ks_gain, nsa_kw_gain,
                 cmp_pos, cmp_k_w1, cmp_k_w2, cmp_v_w1, cmp_v_w2,
                 mla_q_lora_gain, mla_w_uq, mla_kv_lora_gain, mla_w_ukv, mla_q_gain, mla_k_gain,
                 out_gain_a, out_gain_b, w_out,
                 norm2_gain, peer_w_q, peer_sub_keys, peer_u, peer_v):
    B, S, _ = x.shape
    proj = rmsnorm(x, norm1_gain) @ w_in
    pts = [int(p) for p in np.cumsum(IN_SIZES)[:-1]]
    (a_q, a_kc, a_vc, a_ks, a_vs, a_kw, a_vw, a_gate, b_cq, b_ckv, b_kr) = jnp.split(proj, pts, axis=-1)
    kvs = lambda t: t.reshape(B, S, A_KV_HEADS, A_HEAD_DIM)
    gates = jax.nn.sigmoid(a_gate).reshape(B, S, A_HEADS, 3)
    o_a = nsa_mixer(a_q.reshape(B, S, A_HEADS, A_HEAD_DIM), kvs(a_kc), kvs(a_vc), kvs(a_ks), kvs(a_vs),
                    kvs(a_kw), kvs(a_vw), gates, positions,
                    nsa_q_gain, nsa_kc_gain, nsa_ks_gain, nsa_kw_gain,
                    cmp_pos, cmp_k_w1, cmp_k_w2, cmp_v_w1, cmp_v_w2)
    o_b = mla_mixer(b_cq, b_ckv, b_kr, positions, mla_q_lora_gain, mla_w_uq,
                    mla_kv_lora_gain, mla_w_ukv, mla_q_gain, mla_k_gain)
    mixed = jnp.concatenate([rmsnorm(o_a, out_gain_a), rmsnorm(o_b, out_gain_b)], axis=-1) @ w_out
    h = x + mixed
    return h + peer_ffn(rmsnorm(h, norm2_gain), peer_w_q, peer_sub_keys, peer_u, peer_v)


def setup_inputs(seed: int = 0) -> dict:
    key = jax.random.key(seed)
    ks = jax.random.split(key, 32)
    L = DEPTH
    nrm = lambda k, shape, s: jax.random.normal(k, shape, jnp.float32) * s
    gain = lambda k, n: 1.0 + 0.02 * jax.random.normal(k, (L, n), jnp.float32)
    start = jax.random.randint(ks[1], (BATCH, 1), 0, 4096, dtype=jnp.int32)
    return {
        'x': nrm(ks[0], (BATCH, SEQ, D_MODEL), 1.0),
        'positions': (start + jnp.arange(SEQ, dtype=jnp.int32)[None, :]).astype(jnp.int32),
        'norm1_gain': gain(ks[2], D_MODEL),
        'w_in': nrm(ks[3], (L, D_MODEL, IN_WIDTH), D_MODEL ** -0.5),
        'nsa_q_gain': gain(ks[4], A_HEAD_DIM),
        'nsa_kc_gain': gain(ks[5], A_HEAD_DIM),
        'nsa_ks_gain': gain(ks[6], A_HEAD_DIM),
        'nsa_kw_gain': gain(ks[7], A_HEAD_DIM),
        'cmp_pos': nrm(ks[8], (L, CMP_LEN, A_HEAD_DIM), 0.1),
        'cmp_k_w1': nrm(ks[9], (L, CMP_LEN * A_HEAD_DIM, CMP_HIDDEN), (CMP_LEN * A_HEAD_DIM) ** -0.5),
        'cmp_k_w2': nrm(ks[10], (L, CMP_HIDDEN, A_HEAD_DIM), CMP_HIDDEN ** -0.5),
        'cmp_v_w1': nrm(ks[11], (L, CMP_LEN * A_HEAD_DIM, CMP_HIDDEN), (CMP_LEN * A_HEAD_DIM) ** -0.5),
        'cmp_v_w2': nrm(ks[12], (L, CMP_HIDDEN, A_HEAD_DIM), CMP_HIDDEN ** -0.5),
        'mla_q_lora_gain': gain(ks[13], Q_LORA),
        'mla_w_uq': nrm(ks[14], (L, Q_LORA, B_HEADS * B_QK), Q_LORA ** -0.5),
        'mla_kv_lora_gain': gain(ks[15], KV_LORA),
        'mla_w_ukv': nrm(ks[16], (L, KV_LORA, B_HEADS * (B_NOPE + B_V)), KV_LORA ** -0.5),
        'mla_q_gain': gain(ks[17], B_QK),
        'mla_k_gain': gain(ks[18], B_QK),
        'out_gain_a': gain(ks[19], A_WIDTH),
        'out_gain_b': gain(ks[20], B_WIDTH),
        'w_out': nrm(ks[21], (L, MIX_WIDTH, D_MODEL), MIX_WIDTH ** -0.5),
        'norm2_gain': gain(ks[22], D_MODEL),
        'peer_w_q': nrm(ks[23], (L, D_MODEL, P_HEADS * P_KEY_DIM), D_MODEL ** -0.5),
        'peer_sub_keys': nrm(ks[24], (L, P_HEADS, 2, N_KEYS, P_KEY_DIM // 2), (P_KEY_DIM // 2) ** -0.5),
        'peer_u': nrm(ks[25], (L, N_EXPERTS, D_MODEL), D_MODEL ** -0.5),
        'peer_v': nrm(ks[26], (L, N_EXPERTS, D_MODEL), D_MODEL ** -0.5),
    }


def reference(x, positions, norm1_gain, w_in,
              nsa_q_gain, nsa_kc_gain, nsa_ks_gain, nsa_kw_gain,
              cmp_pos, cmp_k_w1, cmp_k_w2, cmp_v_w1, cmp_v_w2,
              mla_q_lora_gain, mla_w_uq, mla_kv_lora_gain, mla_w_ukv, mla_q_gain, mla_k_gain,
              out_gain_a, out_gain_b, w_out,
              norm2_gain, peer_w_q, peer_sub_keys, peer_u, peer_v):
    h = x
    for l in range(DEPTH):
        h = hybrid_layer(h, positions, norm1_gain[l], w_in[l],
                         nsa_q_gain[l], nsa_kc_gain[l], nsa_ks_gain[l], nsa_kw_gain[l],
                         cmp_pos[l], cmp_k_w1[l], cmp_k_w2[l], cmp_v_w1[l], cmp_v_w2[l],
                         mla_q_lora_gain[l], mla_w_uq[l], mla_kv_lora_gain[l], mla_w_ukv[l],
                         mla_q_gain[l], mla_k_gain[l],
                         out_gain_a[l], out_gain_b[l], w_out[l],
                         norm2_gain[l], peer_w_q[l], peer_sub_keys[l], peer_u[l], peer_v[l])
    return h
```

```python
import functools

import jax
import jax.numpy as jnp
import numpy as np
from jax import lax
from jax.experimental import pallas as pl
from jax.experimental.pallas import tpu as pltpu

F32, BF16 = jnp.float32, jnp.bfloat16
EPS = 1e-6
NEG = -1e30
FORCE = 1e9
ROPE_THETA = 500000.0

D_MODEL = 1024
A_HEADS, A_GROUPS, A_DIM = 8, 2, 64
A_REP = A_HEADS // A_GROUPS
A_ROPE = A_DIM // 4
CMP_LEN, CMP_STRIDE, CMP_HIDDEN = 32, 16, 256
SLC_LEN, SLC_TOPK, WINDOW = 64, 16, 512
B_HEADS, Q_LORA, KV_LORA, B_NOPE, B_ROPE, B_V = 8, 256, 128, 64, 32, 64
B_QK = B_NOPE + B_ROPE
P_HEADS, N_KEYS, P_KEY_DIM, P_TOPK = 8, 128, 256, 16
N_EXPERTS = N_KEYS * N_KEYS

ROW_Q, ROW_KC, ROW_VC, ROW_KS, ROW_VS, ROW_KW, ROW_VW = 0, 512, 640, 768, 896, 1024, 1152
ROW_CQ, ROW_CKV, ROW_KR, ROW_GATE, PROJ_ROWS = 1280, 1536, 1664, 1696, 1728
GATE_ROWS = 32

VMEM_LIMIT = 56 * 1024 * 1024
NT_DIMS = (((1,), (1,)), ((), ()))


def _params(n_axes):
    return pltpu.CompilerParams(dimension_semantics=("arbitrary",) * n_axes, vmem_limit_bytes=VMEM_LIMIT)


def _dot(a, b):
    return jnp.dot(a, b, preferred_element_type=F32)


def _dot_nt(a, b):
    return lax.dot_general(a, b, NT_DIMS, preferred_element_type=F32)


def _row_sumsq(x):
    sq = x * x
    hi = sq.astype(BF16)
    lo = (sq - hi.astype(F32)).astype(BF16)
    ones = jnp.ones((8, x.shape[1]), BF16)
    return (_dot_nt(ones, hi) + _dot_nt(ones, lo))[0:1, :]


def _rms_rows(x):
    ss = jnp.sum(x * x, axis=0, keepdims=True)
    return x * lax.rsqrt(ss * (1.0 / x.shape[0]) + EPS)


def _rope_rows(y, cos, sin, off, half):
    x1, x2 = y[off:off + half], y[off + half:off + 2 * half]
    parts = [y[:off]] if off else []
    parts += [x1 * cos - x2 * sin, x2 * cos + x1 * sin]
    if off + 2 * half < y.shape[0]:
        parts.append(y[off + 2 * half:])
    return jnp.concatenate(parts, axis=0)


def _gelu_tanh(x):
    return 0.5 * x * (1.0 + jnp.tanh(0.7978845608028654 * (x + 0.044715 * (x * x * x))))


def _in_proj_kernel(x_ref, g_ref, w_ref, o_ref):
    x = x_ref[...]
    xg = (x * g_ref[...]).astype(BF16)
    p = _dot_nt(w_ref[...], xg)
    rinv = lax.rsqrt(_row_sumsq(x) * (1.0 / x.shape[1]) + EPS)
    o_ref[...] = p * rinv


def _in_proj(x2, gain, w_t, tt):
    t, d = x2.shape
    rows = w_t.shape[0]
    return pl.pallas_call(
        _in_proj_kernel,
        grid=(t // tt,),
        in_specs=[pl.BlockSpec((tt, d), lambda i: (i, 0)),
                  pl.BlockSpec((1, d), lambda i: (0, 0)),
                  pl.BlockSpec((rows, d), lambda i: (0, 0))],
        out_specs=pl.BlockSpec((rows, tt), lambda i: (0, i)),
        out_shape=jax.ShapeDtypeStruct((rows, t), F32),
        compiler_params=_params(1),
        name="in_proj",
    )(x2, gain, w_t)


def _nsa_prep_kernel(q_ref, kc_ref, vc_ref, ks_ref, vs_ref, kw_ref, vw_ref, gt_ref, cos_ref, sin_ref,
                     qg_ref, ksg_ref, kwg_ref,
                     qo_ref, kco_ref, vco_ref, kso_ref, vso_ref, kwo_ref, vwo_ref, gto_ref):
    cos, sin = cos_ref[...], sin_ref[...]
    tt = cos.shape[1]
    for h in range(A_HEADS):
        y = _rms_rows(q_ref[h * A_DIM:(h + 1) * A_DIM, :]) * qg_ref[...]
        y = _rope_rows(y, cos, sin, 0, A_ROPE // 2) * (A_DIM ** -0.5)
        qo_ref[h * A_DIM:(h + 1) * A_DIM, :] = y.astype(BF16)
    kco_ref[...] = kc_ref[...].T
    vco_ref[...] = vc_ref[...].T
    zeros = jnp.zeros((A_DIM, tt), F32)
    for g in range(A_GROUPS):
        for src, gain, dst in ((ks_ref, ksg_ref, kso_ref), (kw_ref, kwg_ref, kwo_ref)):
            y = _rms_rows(src[g * A_DIM:(g + 1) * A_DIM, :]) * gain[...]
            y = _rope_rows(y, cos, sin, 0, A_ROPE // 2)
            dst[g] = jnp.concatenate([y, zeros], axis=0).T.astype(BF16)
    vso_ref[...] = vs_ref[...].astype(BF16)
    vwo_ref[...] = vw_ref[...].astype(BF16)
    gto_ref[...] = 1.0 / (1.0 + jnp.exp(-gt_ref[...]))


def _nsa_prep(proj_t, cos_t, sin_t, q_gain, ks_gain, kw_gain, tt):
    t = proj_t.shape[1]
    gw = A_GROUPS * A_DIM
    row_spec = lambda rows, start: pl.BlockSpec((rows, tt), lambda i: (start // rows, i))
    col = lambda n: pl.BlockSpec((n, 1), lambda i: (0, 0))
    return pl.pallas_call(
        _nsa_prep_kernel,
        grid=(t // tt,),
        in_specs=[row_spec(512, ROW_Q), row_spec(gw, ROW_KC), row_spec(gw, ROW_VC), row_spec(gw, ROW_KS),
                  row_spec(gw, ROW_VS), row_spec(gw, ROW_KW), row_spec(gw, ROW_VW), row_spec(GATE_ROWS, ROW_GATE),
                  pl.BlockSpec((A_ROPE // 2, tt), lambda i: (0, i)), pl.BlockSpec((A_ROPE // 2, tt), lambda i: (0, i)),
                  col(A_DIM), col(A_DIM), col(A_DIM)],
        out_specs=[pl.BlockSpec((512, tt), lambda i: (0, i)),
                   pl.BlockSpec((tt, gw), lambda i: (i, 0)),
                   pl.BlockSpec((tt, gw), lambda i: (i, 0)),
                   pl.BlockSpec((A_GROUPS, tt, 128), lambda i: (0, i, 0)),
                   pl.BlockSpec((gw, tt), lambda i: (0, i)),
                   pl.BlockSpec((A_GROUPS, tt, 128), lambda i: (0, i, 0)),
                   pl.BlockSpec((gw, tt), lambda i: (0, i)),
                   pl.BlockSpec((GATE_ROWS, tt), lambda i: (0, i))],
        out_shape=[jax.ShapeDtypeStruct((512, t), BF16),
                   jax.ShapeDtypeStruct((t, gw), F32),
                   jax.ShapeDtypeStruct((t, gw), F32),
                   jax.ShapeDtypeStruct((A_GROUPS, t, 128), BF16),
                   jax.ShapeDtypeStruct((gw, t), BF16),
                   jax.ShapeDtypeStruct((A_GROUPS, t, 128), BF16),
                   jax.ShapeDtypeStruct((gw, t), BF16),
                   jax.ShapeDtypeStruct((GATE_ROWS, t), F32)],
        compiler_params=_params(1),
        name="nsa_prep",
    )(proj_t, proj_t, proj_t, proj_t, proj_t, proj_t, proj_t, proj_t, cos_t, sin_t, q_gain, ks_gain, kw_gain)


def _compress_kernel(kc_ref, vc_ref, w1ka_ref, w1kb_ref, w1va_ref, w1vb_ref, plo_ref, phi_ref,
                     w2k_ref, w2v_ref, kg_ref, cos_ref, sin_ref, ko_ref, vo_ref):
    nc = kc_ref.shape[1]
    zeros = jnp.zeros((A_DIM, nc), F32)
    for src, w1a, w1b, w2, is_k in ((kc_ref, w1ka_ref, w1kb_ref, w2k_ref, True),
                                    (vc_ref, w1va_ref, w1vb_ref, w2v_ref, False)):
        x = src[0]
        xlo = (x + plo_ref[...]).astype(BF16)
        xhi = (x + phi_ref[...]).astype(BF16)
        for g in range(A_GROUPS):
            first = _dot_nt(w1a[g], xlo)
            second = _dot_nt(w1b[g], xhi)
            hid = _gelu_tanh(first + pltpu.roll(second, nc - 1, axis=1)).astype(BF16)
            c = _dot(w2[...], hid)
            if is_k:
                y = _rope_rows(_rms_rows(c) * kg_ref[...], cos_ref[0], sin_ref[0], 0, A_ROPE // 2)
                ko_ref[0, g] = jnp.concatenate([y, zeros], axis=0).T.astype(BF16)
            else:
                vo_ref[0, g] = c.astype(BF16)


def _compress(kc_chunks, vc_chunks, w1ka, w1kb, w1va, w1vb, plo, phi, w2k_t, w2v_t, kc_gain, cos_c, sin_c):
    b, nc, cw = kc_chunks.shape
    full = lambda a: pl.BlockSpec(a.shape, lambda i: (0,) * a.ndim)
    return pl.pallas_call(
        _compress_kernel,
        grid=(b,),
        in_specs=[pl.BlockSpec((1, nc, cw), lambda i: (i, 0, 0)), pl.BlockSpec((1, nc, cw), lambda i: (i, 0, 0)),
                  full(w1ka), full(w1kb), full(w1va), full(w1vb), full(plo), full(phi), full(w2k_t), full(w2v_t),
                  full(kc_gain),
                  pl.BlockSpec((1, A_ROPE // 2, nc), lambda i: (i, 0, 0)),
                  pl.BlockSpec((1, A_ROPE // 2, nc), lambda i: (i, 0, 0))],
        out_specs=[pl.BlockSpec((1, A_GROUPS, nc, 128), lambda i: (i, 0, 0, 0)),
                   pl.BlockSpec((1, A_GROUPS, A_DIM, nc), lambda i: (i, 0, 0, 0))],
        out_shape=[jax.ShapeDtypeStruct((b, A_GROUPS, nc, 128), BF16),
                   jax.ShapeDtypeStruct((b, A_GROUPS, A_DIM, nc), BF16)],
        compiler_params=_params(1),
        name="nsa_compress",
    )(kc_chunks, vc_chunks, w1ka, w1kb, w1va, w1vb, plo, phi, w2k_t, w2v_t, kc_gain, cos_c, sin_c)


def _stack_heads(q, tq):
    qs = jnp.concatenate([q[r * A_DIM:(r + 1) * A_DIM, :] for r in range(A_REP)], axis=1)
    return jnp.concatenate([qs, jnp.zeros_like(qs)], axis=0)


def _nsa_cmp_kernel(q_ref, k_ref, v_ref, ov_ref, o_ref, sb_ref, *, tq):
    i = pl.program_id(2)
    n = A_REP * tq
    nc = k_ref.shape[2]
    nb = ov_ref.shape[0]
    qp = _stack_heads(q_ref[...], tq)
    s = _dot(k_ref[0, 0], qp)
    cmp_end = lax.broadcasted_iota(jnp.int32, (nc, n), 0) * CMP_STRIDE + (CMP_LEN - 1)
    tok = i * tq + (lax.broadcasted_iota(jnp.int32, (nc, n), 1) & (tq - 1))
    mask = cmp_end <= tok
    s = jnp.where(mask, s, NEG)
    m = jnp.max(s, axis=0, keepdims=True)
    p = jnp.where(mask, jnp.exp(s - m), 0.0)
    l = jnp.sum(p, axis=0, keepdims=True)
    inv = jnp.where(l > 0.0, 1.0 / l, 0.0)
    pn = (p * inv).astype(BF16)
    oc = _dot(v_ref[0, 0], pn)
    for r in range(A_REP):
        o_ref[r * A_DIM:(r + 1) * A_DIM, :] = oc[:, r * tq:(r + 1) * tq]
    imp4 = _dot(ov_ref[...], pn)
    imp = imp4[:, 0:tq]
    for r in range(1, A_REP):
        imp = imp + imp4[:, r * tq:(r + 1) * tq]

    blk = lax.broadcasted_iota(jnp.int32, (nb, tq), 0)
    t = i * tq + lax.broadcasted_iota(jnp.int32, (nb, tq), 1)
    forced = (blk == (t >> 6)) | (blk == 0)
    v = jnp.where(forced, FORCE, jnp.where(blk * SLC_LEN <= t, imp, NEG))
    blk_f = blk.astype(F32)
    sel = jnp.zeros((nb, tq), F32)
    for _ in range(min(SLC_TOPK, nb)):
        mx = jnp.max(v, axis=0, keepdims=True)
        first = jnp.min(jnp.where(v == mx, blk_f, float(nb)), axis=0, keepdims=True)
        hit = blk_f == first
        sel = jnp.where(hit, 1.0, sel)
        v = jnp.where(hit, -jnp.inf, v)
    sb_ref[0] = jnp.where(sel > 0.0, 0.0, NEG)


def _nsa_cmp(q_t, kcmp, vcmp_t, ov_t, batch, seq, tq):
    nq = seq // tq
    nc = kcmp.shape[2]
    nb = ov_t.shape[0]
    t = q_t.shape[1]
    gr = A_REP * A_DIM
    return pl.pallas_call(
        functools.partial(_nsa_cmp_kernel, tq=tq),
        grid=(batch, A_GROUPS, nq),
        in_specs=[pl.BlockSpec((gr, tq), lambda b, g, i: (g, b * nq + i)),
                  pl.BlockSpec((1, 1, nc, 128), lambda b, g, i: (b, g, 0, 0)),
                  pl.BlockSpec((1, 1, A_DIM, nc), lambda b, g, i: (b, g, 0, 0)),
                  pl.BlockSpec((nb, nc), lambda b, g, i: (0, 0))],
        out_specs=[pl.BlockSpec((gr, tq), lambda b, g, i: (g, b * nq + i)),
                   pl.BlockSpec((1, nb, tq), lambda b, g, i: (g, 0, b * nq + i))],
        out_shape=[jax.ShapeDtypeStruct((A_HEADS * A_DIM, t), F32),
                   jax.ShapeDtypeStruct((A_GROUPS, nb, t), F32)],
        compiler_params=_params(3),
        name="nsa_cmp",
    )(q_t, kcmp, vcmp_t, ov_t)


def _flash_init(m_scr, l_scr, acc_scr):
    m_scr[...] = jnp.full(m_scr.shape, -jnp.inf, F32)
    l_scr[...] = jnp.zeros(l_scr.shape, F32)
    acc_scr[...] = jnp.zeros(acc_scr.shape, F32)


def _flash_update(s, v_t, m_scr, l_scr, acc_scr):
    m_prev = m_scr[...]
    m_new = jnp.maximum(m_prev, jnp.max(s, axis=0, keepdims=True))
    alpha = jnp.exp(m_prev - m_new)
    p = jnp.exp(s - m_new)
    l_scr[...] = alpha * l_scr[...] + jnp.sum(p, axis=0, keepdims=True)
    acc_scr[...] = alpha * acc_scr[...] + _dot(v_t, p.astype(BF16))
    m_scr[...] = m_new


def _causal_pairs(seq, tq, tk):
    qi, kj = [], []
    for i in range(seq // tq):
        for j in range((i * tq + tq - 1) // tk + 1):
            qi.append(i)
            kj.append(j)
    return jnp.asarray(np.array(qi, np.int32)), jnp.asarray(np.array(kj, np.int32))


def _nsa_sel_kernel(qi_ref, kj_ref, q_ref, k_ref, v_ref, sb_ref, o_ref, qp_scr, sb4_scr, m_scr, l_scr, acc_scr, *, tq, tk):
    p_id = pl.program_id(2)
    i, j = qi_ref[p_id], kj_ref[p_id]
    j_last = (i * tq + tq - 1) // tk
    n = A_REP * tq

    @pl.when(j == 0)
    def _():
        qp_scr[...] = _stack_heads(q_ref[...], tq)
        sb = sb_ref[0]
        for r in range(A_REP):
            sb4_scr[:, r * tq:(r + 1) * tq] = sb
        _flash_init(m_scr, l_scr, acc_scr)

    def step(diagonal):
        s = _dot(k_ref[0], qp_scr[...])
        base = j * (tk // SLC_LEN)
        bias = jnp.concatenate(
            [jnp.broadcast_to(sb4_scr[pl.ds(base + c, 1), :], (SLC_LEN, n)) for c in range(tk // SLC_LEN)], axis=0)
        s = s + bias
        if diagonal:
            kpos = j * tk + lax.broadcasted_iota(jnp.int32, (tk, n), 0)
            tok = i * tq + (lax.broadcasted_iota(jnp.int32, (tk, n), 1) & (tq - 1))
            s = jnp.where(kpos <= tok, s, NEG)
        _flash_update(s, v_ref[...], m_scr, l_scr, acc_scr)

    pl.when(j < j_last)(lambda: step(False))

    @pl.when(j == j_last)
    def _():
        step(True)
        o = acc_scr[...] * (1.0 / l_scr[...])
        for r in range(A_REP):
            o_ref[r * A_DIM:(r + 1) * A_DIM, :] = o[:, r * tq:(r + 1) * tq]


def _nsa_sel(q_t, k_slc, v_slc_t, selb, batch, seq, tq, tk):
    nq, nk = seq // tq, seq // tk
    nb = selb.shape[1]
    t = q_t.shape[1]
    gr = A_REP * A_DIM
    n = A_REP * tq
    qi, kj = _causal_pairs(seq, tq, tk)
    grid_spec = pltpu.PrefetchScalarGridSpec(
        num_scalar_prefetch=2,
        grid=(batch, A_GROUPS, int(qi.shape[0])),
        in_specs=[pl.BlockSpec((gr, tq), lambda b, g, p, qi, kj: (g, b * nq + qi[p])),
                  pl.BlockSpec((1, tk, 128), lambda b, g, p, qi, kj: (g, b * nk + kj[p], 0)),
                  pl.BlockSpec((A_DIM, tk), lambda b, g, p, qi, kj: (g, b * nk + kj[p])),
                  pl.BlockSpec((1, nb, tq), lambda b, g, p, qi, kj: (g, 0, b * nq + qi[p]))],
        out_specs=pl.BlockSpec((gr, tq), lambda b, g, p, qi, kj: (g, b * nq + qi[p])),
        scratch_shapes=[pltpu.VMEM((128, n), BF16), pltpu.VMEM((nb, n), F32),
                        pltpu.VMEM((1, n), F32), pltpu.VMEM((1, n), F32), pltpu.VMEM((A_DIM, n), F32)],
    )
    return pl.pallas_call(
        functools.partial(_nsa_sel_kernel, tq=tq, tk=tk),
        grid_spec=grid_spec,
        out_shape=jax.ShapeDtypeStruct((A_HEADS * A_DIM, t), F32),
        compiler_params=_params(3),
        name="nsa_sel",
    )(qi, kj, q_t, k_slc, v_slc_t, selb)


def _nsa_win_kernel(q_ref, k_ref, v_ref, o_ref, qp_scr, m_scr, l_scr, acc_scr, *, tq):
    i, c = pl.program_id(2), pl.program_id(3)
    n_back = WINDOW // tq
    n = A_REP * tq
    kt = i - n_back + c

    @pl.when(c == 0)
    def _():
        qp_scr[...] = _stack_heads(q_ref[...], tq)
        _flash_init(m_scr, l_scr, acc_scr)

    @pl.when(kt >= 0)
    def _():
        s = _dot(k_ref[0], qp_scr[...])
        kpos = kt * tq + lax.broadcasted_iota(jnp.int32, (tq, n), 0)
        tok = i * tq + (lax.broadcasted_iota(jnp.int32, (tq, n), 1) & (tq - 1))
        s = jnp.where((kpos <= tok) & (kpos > tok - WINDOW), s, NEG)
        _flash_update(s, v_ref[...], m_scr, l_scr, acc_scr)

    @pl.when(c == n_back)
    def _():
        o = acc_scr[...] * (1.0 / l_scr[...])
        for r in range(A_REP):
            o_ref[r * A_DIM:(r + 1) * A_DIM, :] = o[:, r * tq:(r + 1) * tq]


def _nsa_win(q_t, k_win, v_win_t, batch, seq, tq):
    nq = seq // tq
    n_back = WINDOW // tq
    t = q_t.shape[1]
    gr = A_REP * A_DIM
    n = A_REP * tq
    kidx = lambda b, i, c: b * nq + jnp.maximum(i - n_back + c, 0)
    return pl.pallas_call(
        functools.partial(_nsa_win_kernel, tq=tq),
        grid=(batch, A_GROUPS, nq, n_back + 1),
        in_specs=[pl.BlockSpec((gr, tq), lambda b, g, i, c: (g, b * nq + i)),
                  pl.BlockSpec((1, tq, 128), lambda b, g, i, c: (g, kidx(b, i, c), 0)),
                  pl.BlockSpec((A_DIM, tq), lambda b, g, i, c: (g, kidx(b, i, c)))],
        out_specs=pl.BlockSpec((gr, tq), lambda b, g, i, c: (g, b * nq + i)),
        out_shape=jax.ShapeDtypeStruct((A_HEADS * A_DIM, t), F32),
        scratch_shapes=[pltpu.VMEM((128, n), BF16), pltpu.VMEM((1, n), F32), pltpu.VMEM((1, n), F32),
                        pltpu.VMEM((A_DIM, n), F32)],
        compiler_params=_params(4),
        name="nsa_win",
    )(q_t, k_win, v_win_t)


def _mla_prep_kernel(cq_ref, ckv_ref, kr_ref, cos_ref, sin_ref, qlg_ref, kvlg_ref, qg_ref, kg_ref, wuq_ref, wukv_ref,
                     qo_ref, ko_ref, vo_ref):
    cos, sin = cos_ref[...], sin_ref[...]
    tt = cos.shape[1]
    q_all = _dot(wuq_ref[...], (_rms_rows(cq_ref[...]) * qlg_ref[...]).astype(BF16))
    kv_all = _dot(wukv_ref[...], (_rms_rows(ckv_ref[...]) * kvlg_ref[...]).astype(BF16))
    kr = kr_ref[...]
    pad = jnp.zeros((128 - B_QK, tt), F32)
    for h in range(B_HEADS):
        y = _rms_rows(q_all[h * B_QK:(h + 1) * B_QK]) * qg_ref[...]
        y = _rope_rows(y, cos, sin, B_NOPE, B_ROPE // 2) * (B_QK ** -0.5)
        qo_ref[h] = jnp.concatenate([y, pad], axis=0).astype(BF16)
        base = h * (B_NOPE + B_V)
        k = jnp.concatenate([kv_all[base:base + B_NOPE], kr], axis=0)
        y = _rope_rows(_rms_rows(k) * kg_ref[...], cos, sin, B_NOPE, B_ROPE // 2)
        ko_ref[h] = jnp.concatenate([y, pad], axis=0).T.astype(BF16)
        vo_ref[h] = kv_all[base + B_NOPE:base + B_NOPE + B_V].astype(BF16)


def _mla_prep(proj_t, cos_t, sin_t, q_lora_gain, kv_lora_gain, q_gain, k_gain, wuq_t, wukv_t, tt):
    t = proj_t.shape[1]
    row_spec = lambda rows, start: pl.BlockSpec((rows, tt), lambda i: (start // rows, i))
    full = lambda a: pl.BlockSpec(a.shape, lambda i: (0,) * a.ndim)
    return pl.pallas_call(
        _mla_prep_kernel,
        grid=(t // tt,),
        in_specs=[row_spec(Q_LORA, ROW_CQ), row_spec(KV_LORA, ROW_CKV), row_spec(B_ROPE, ROW_KR),
                  pl.BlockSpec((B_ROPE // 2, tt), lambda i: (0, i)), pl.BlockSpec((B_ROPE // 2, tt), lambda i: (0, i)),
                  full(q_lora_gain), full(kv_lora_gain), full(q_gain), full(k_gain), full(wuq_t), full(wukv_t)],
        out_specs=[pl.BlockSpec((B_HEADS, 128, tt), lambda i: (0, 0, i)),
                   pl.BlockSpec((B_HEADS, tt, 128), lambda i: (0, i, 0)),
                   pl.BlockSpec((B_HEADS, B_V, tt), lambda i: (0, 0, i))],
        out_shape=[jax.ShapeDtypeStruct((B_HEADS, 128, t), BF16),
                   jax.ShapeDtypeStruct((B_HEADS, t, 128), BF16),
                   jax.ShapeDtypeStruct((B_HEADS, B_V, t), BF16)],
        compiler_params=_params(1),
        name="mla_prep",
    )(proj_t, proj_t, proj_t, cos_t, sin_t, q_lora_gain, kv_lora_gain, q_gain, k_gain, wuq_t, wukv_t)


def _mla_attn_kernel(qi_ref, kj_ref, q_ref, k_ref, v_ref, o_ref, m_scr, l_scr, acc_scr, *, tq, tk):
    p_id = pl.program_id(2)
    i, j = qi_ref[p_id], kj_ref[p_id]
    j_last = (i * tq + tq - 1) // tk

    pl.when(j == 0)(lambda: _flash_init(m_scr, l_scr, acc_scr))

    def step(diagonal):
        s = _dot(k_ref[0], q_ref[0])
        if diagonal:
            kpos = j * tk + lax.broadcasted_iota(jnp.int32, (tk, tq), 0)
            tok = i * tq + lax.broadcasted_iota(jnp.int32, (tk, tq), 1)
            s = jnp.where(kpos <= tok, s, NEG)
        _flash_update(s, v_ref[0], m_scr, l_scr, acc_scr)

    pl.when(j < j_last)(lambda: step(False))

    @pl.when(j == j_last)
    def _():
        step(True)
        o_ref[...] = acc_scr[...] * (1.0 / l_scr[...])


def _mla_attn(q_m, k_m, v_m_t, batch, seq, tq, tk):
    nq, nk = seq // tq, seq // tk
    t = q_m.shape[2]
    qi, kj = _causal_pairs(seq, tq, tk)
    grid_spec = pltpu.PrefetchScalarGridSpec(
        num_scalar_prefetch=2,
        grid=(batch, B_HEADS, int(qi.shape[0])),
        in_specs=[pl.BlockSpec((1, 128, tq), lambda b, h, p, qi, kj: (h, 0, b * nq + qi[p])),
                  pl.BlockSpec((1, tk, 128), lambda b, h, p, qi, kj: (h, b * nk + kj[p], 0)),
                  pl.BlockSpec((1, B_V, tk), lambda b, h, p, qi, kj: (h, 0, b * nk + kj[p]))],
        out_specs=pl.BlockSpec((B_V, tq), lambda b, h, p, qi, kj: (h, b * nq + qi[p])),
        scratch_shapes=[pltpu.VMEM((1, tq), F32), pltpu.VMEM((1, tq), F32), pltpu.VMEM((B_V, tq), F32)],
    )
    return pl.pallas_call(
        functools.partial(_mla_attn_kernel, tq=tq, tk=tk),
        grid_spec=grid_spec,
        out_shape=jax.ShapeDtypeStruct((B_HEADS * B_V, t), F32),
        compiler_params=_params(3),
        name="mla_attn",
    )(qi, kj, q_m, k_m, v_m_t)


def _out_proj_kernel(oc_ref, os_ref, ow_ref, gt_ref, ob_ref, x_ref, ga_ref, gb_ref, w_ref, g2_ref, h_ref, hn_ref):
    heads = []
    for h in range(A_HEADS):
        rows = slice(h * A_DIM, (h + 1) * A_DIM)
        heads.append(gt_ref[3 * h:3 * h + 1, :] * oc_ref[rows, :] + gt_ref[3 * h + 1:3 * h + 2, :] * os_ref[rows, :]
                     + gt_ref[3 * h + 2:3 * h + 3, :] * ow_ref[rows, :])
    oa = _rms_rows(jnp.concatenate(heads, axis=0)) * ga_ref[...]
    ob = _rms_rows(ob_ref[...]) * gb_ref[...]
    cat = jnp.concatenate([oa, ob], axis=0).astype(BF16)
    hid = x_ref[...].T + _dot(w_ref[...], cat)
    h_ref[...] = hid
    hn_ref[...] = (_rms_rows(hid) * g2_ref[...]).astype(BF16)


def _out_proj(oc_t, os_t, ow_t, gates_t, ob_t, x2, gain_a, gain_b, w_out_t, gain2, tt):
    t, d = x2.shape
    aw = oc_t.shape[0]
    bw = ob_t.shape[0]
    tok = lambda rows: pl.BlockSpec((rows, tt), lambda i: (0, i))
    full = lambda a: pl.BlockSpec(a.shape, lambda i: (0,) * a.ndim)
    return pl.pallas_call(
        _out_proj_kernel,
        grid=(t // tt,),
        in_specs=[tok(aw), tok(aw), tok(aw), tok(GATE_ROWS), tok(bw), pl.BlockSpec((tt, d), lambda i: (i, 0)),
                  full(gain_a), full(gain_b), full(w_out_t), full(gain2)],
        out_specs=[tok(d), tok(d)],
        out_shape=[jax.ShapeDtypeStruct((d, t), F32), jax.ShapeDtypeStruct((d, t), BF16)],
        compiler_params=_params(1),
        name="out_proj",
    )(oc_t, os_t, ow_t, gates_t, ob_t, x2, gain_a, gain_b, w_out_t, gain2)


def _top_ranked(s):
    n, tp = s.shape
    row = lax.broadcasted_iota(jnp.int32, (n, tp), 0).astype(F32)
    slot = lax.broadcasted_iota(jnp.int32, (P_TOPK, tp), 0)

    def body(a, carry):
        v, rank, vals = carry
        mx = jnp.max(v, axis=0, keepdims=True)
        first = jnp.min(jnp.where(v == mx, row, float(n)), axis=0, keepdims=True)
        hit = row == first
        rank = jnp.where(hit, a.astype(F32), rank)
        v = jnp.where(hit, -jnp.inf, v)
        vals = jnp.where(slot == a, mx, vals)
        return v, rank, vals

    _, rank, vals = lax.fori_loop(0, P_TOPK, body,
                                  (s, jnp.full((n, tp), float(P_TOPK), F32), jnp.zeros((P_TOPK, tp), F32)))
    return rank, vals


def _pair_counts(v1, v2):
    k, tp = v1.shape
    slot = lax.broadcasted_iota(jnp.int32, (k, tp), 0).astype(F32)
    top = v1[0:1] + v2[0:1]

    def body(_, carry):
        count, front, z = carry
        mx = jnp.max(front, axis=0, keepdims=True)
        a_star = jnp.min(jnp.where(front == mx, slot, float(k)), axis=0, keepdims=True)
        hit = slot == a_star
        count = count + jnp.where(hit, 1.0, 0.0)
        nxt = jnp.sum(jnp.where(hit, count, 0.0), axis=0, keepdims=True)
        v2_nxt = jnp.sum(jnp.where(slot == nxt, v2, 0.0), axis=0, keepdims=True)
        front = jnp.where(hit, jnp.where(nxt < float(k), v1 + v2_nxt, -jnp.inf), front)
        return count, front, z + jnp.exp(mx - top)

    count, _, z = lax.fori_loop(0, k, body, (jnp.zeros((k, tp), F32), v1 + v2[0:1], jnp.zeros((1, tp), F32)))
    return count, z


def _peer_route_kernel(hn_ref, wq_ref, keys_ref, cut_ref, g1_ref, r2_ref, g2_ref, q_scr):
    q_scr[...] = _dot(wq_ref[...], hn_ref[...])
    half = P_KEY_DIM // 2

    def head(h, _):
        q = _rms_rows(q_scr[pl.ds(pl.multiple_of(h * P_KEY_DIM, P_KEY_DIM), P_KEY_DIM), :]).astype(BF16)
        s1 = _dot(keys_ref[h, 0], q[:half])
        s2 = _dot(keys_ref[h, 1], q[half:])
        rank1, v1 = _top_ranked(s1)
        rank2, v2 = _top_ranked(s2)
        count, z = _pair_counts(v1, v2)
        cut = jnp.zeros_like(rank1)
        for a in range(P_TOPK):
            cut = jnp.where(rank1 == float(a), count[a:a + 1], cut)
        cut_ref[h] = cut
        g1_ref[h] = jnp.exp(s1 - v1[0:1])
        r2_ref[h] = rank2
        g2_ref[h] = jnp.exp(s2 - v2[0:1]) * (1.0 / z)
        return 0

    lax.fori_loop(0, P_HEADS, head, 0)


def _peer_route(hn_t, wq_t, sub_keys, tp):
    d, t = hn_t.shape
    out = jax.ShapeDtypeStruct((P_HEADS, N_KEYS, t), F32)
    ospec = pl.BlockSpec((P_HEADS, N_KEYS, tp), lambda i: (0, 0, i))
    return pl.pallas_call(
        _peer_route_kernel,
        grid=(t // tp,),
        in_specs=[pl.BlockSpec((d, tp), lambda i: (0, i)),
                  pl.BlockSpec(wq_t.shape, lambda i: (0, 0)),
                  pl.BlockSpec(sub_keys.shape, lambda i: (0, 0, 0, 0))],
        out_specs=[ospec, ospec, ospec, ospec],
        out_shape=[out, out, out, out],
        scratch_shapes=[pltpu.VMEM((P_HEADS * P_KEY_DIM, tp), F32)],
        compiler_params=_params(1),
        name="peer_route",
    )(hn_t, wq_t, sub_keys)


def _peer_ffn_kernel(hn_ref, h_ref, u_ref, v_ref, cut_ref, g1_ref, r2_ref, g2_ref, o_ref, a_scr, w_scr, acc_scr, *, te):
    e = pl.program_id(1)

    @pl.when(e == 0)
    def _():
        acc_scr[...] = jnp.zeros(acc_scr.shape, F32)

    a_scr[...] = _dot(u_ref[...], hn_ref[...])
    for ii in range(te // N_KEYS):
        rows = slice(ii * N_KEYS, (ii + 1) * N_KEYS)
        gate = jnp.zeros((N_KEYS, a_scr.shape[1]), F32)
        for h in range(P_HEADS):
            chosen = r2_ref[h] < cut_ref[ii, h:h + 1, :]
            gate = gate + jnp.where(chosen, g2_ref[h], 0.0) * g1_ref[ii, h:h + 1, :]
        w_scr[rows, :] = (gate * _gelu_tanh(a_scr[rows, :])).astype(BF16)
    acc_scr[...] += _dot(v_ref[...], w_scr[...])

    @pl.when(e == pl.num_programs(1) - 1)
    def _():
        o_ref[...] = (h_ref[...] + acc_scr[...]).T


def _peer_ffn(hn_t, h_t, u_bf, v_t_bf, cut_k, g1_k, r2, g2, tt, te):
    d, t = hn_t.shape
    n_exp = u_bf.shape[0]
    kpe = te // N_KEYS
    return pl.pallas_call(
        functools.partial(_peer_ffn_kernel, te=te),
        grid=(t // tt, n_exp // te),
        in_specs=[pl.BlockSpec((d, tt), lambda i, e: (0, i)),
                  pl.BlockSpec((d, tt), lambda i, e: (0, i)),
                  pl.BlockSpec((te, d), lambda i, e: (e, 0)),
                  pl.BlockSpec((d, te), lambda i, e: (0, e)),
                  pl.BlockSpec((kpe, P_HEADS, tt), lambda i, e: (e, 0, i)),
                  pl.BlockSpec((kpe, P_HEADS, tt), lambda i, e: (e, 0, i)),
                  pl.BlockSpec((P_HEADS, N_KEYS, tt), lambda i, e: (0, 0, i)),
                  pl.BlockSpec((P_HEADS, N_KEYS, tt), lambda i, e: (0, 0, i))],
        out_specs=pl.BlockSpec((tt, d), lambda i, e: (i, 0)),
        out_shape=jax.ShapeDtypeStruct((t, d), F32),
        scratch_shapes=[pltpu.VMEM((te, tt), F32), pltpu.VMEM((te, tt), BF16), pltpu.VMEM((d, tt), F32)],
        compiler_params=_params(2),
        name="peer_ffn",
    )(hn_t, h_t, u_bf, v_t_bf, cut_k, g1_k, r2, g2)


def _rope_tables_t(pos_flat, rot_dim):
    inv_freq = ROPE_THETA ** (-jnp.arange(0, rot_dim, 2, dtype=F32) / rot_dim)
    ang = pos_flat.astype(F32)[None, :] * inv_freq[:, None]
    return jnp.cos(ang), jnp.sin(ang)


def _expand_cmp_w1(w1):
    w = w1.reshape(CMP_LEN, A_DIM, CMP_HIDDEN)
    out = []
    for part in (w[:CMP_STRIDE], w[CMP_STRIDE:]):
        z = jnp.zeros_like(part)
        both = jnp.stack([jnp.concatenate([part, z], axis=1), jnp.concatenate([z, part], axis=1)])
        out.append(both.reshape(A_GROUPS, CMP_STRIDE * A_GROUPS * A_DIM, CMP_HIDDEN).transpose(0, 2, 1).astype(BF16))
    return out


TOKEN_TILE = 512
NSA_Q_TILE = 256
NSA_SEL_K_TILE = 512
MLA_TILE = 512
PEER_ROUTE_TILE = 256
PEER_EXPERT_TILE = 1024


def _col(v):
    return v.reshape(-1, 1).astype(F32)


def _mixers(x, positions, norm1_gain, w_in, nsa_q_gain, nsa_kc_gain, nsa_ks_gain, nsa_kw_gain,
            cmp_pos, cmp_k_w1, cmp_k_w2, cmp_v_w1, cmp_v_w2,
            mla_q_lora_gain, mla_w_uq, mla_kv_lora_gain, mla_w_ukv, mla_q_gain, mla_k_gain):
    batch, seq, d = x.shape
    t = batch * seq
    tt = TOKEN_TILE
    tq_nsa = NSA_Q_TILE
    tk_sel = NSA_SEL_K_TILE
    tq_mla = tk_mla = MLA_TILE
    assert d == D_MODEL and seq % 512 == 0 and seq // SLC_LEN >= SLC_TOPK and WINDOW % tq_nsa == 0
    col = _col
    x2 = x.reshape(t, d)
    pos = positions.reshape(t)

    w_in_t = w_in.T
    gate_lo = sum((512, 128, 128, 128, 128, 128, 128))
    gate_hi = gate_lo + 3 * A_HEADS
    w_in_t = jnp.concatenate([w_in_t[:gate_lo], w_in_t[gate_hi:], w_in_t[gate_lo:gate_hi],
                              jnp.zeros((PROJ_ROWS - w_in_t.shape[0], d), F32)], axis=0).astype(BF16)
    proj_t = _in_proj(x2, norm1_gain.reshape(1, d), w_in_t, tt)

    cos_a, sin_a = _rope_tables_t(pos, A_ROPE)
    q_t, kc_tm, vc_tm, k_slc, v_slc_t, k_win, v_win_t, gates_t = _nsa_prep(
        proj_t, cos_a, sin_a, col(nsa_q_gain), col(nsa_ks_gain), col(nsa_kw_gain), tt)

    nc = seq // CMP_STRIDE
    chunk_w = CMP_STRIDE * A_GROUPS * A_DIM
    w1ka, w1kb = _expand_cmp_w1(cmp_k_w1)
    w1va, w1vb = _expand_cmp_w1(cmp_v_w1)
    pos_rows = lambda p: jnp.broadcast_to(p[:, None, :], (CMP_STRIDE, A_GROUPS, A_DIM)).reshape(1, chunk_w)
    cmp_end = jnp.minimum(jnp.arange(nc) * CMP_STRIDE + CMP_LEN - 1, seq - 1)
    cos_c, sin_c = _rope_tables_t(positions[:, cmp_end].reshape(-1), A_ROPE)
    to_b = lambda a: a.reshape(A_ROPE // 2, batch, nc).transpose(1, 0, 2)
    kcmp, vcmp_t = _compress(kc_tm.reshape(batch, nc, chunk_w), vc_tm.reshape(batch, nc, chunk_w),
                             w1ka, w1kb, w1va, w1vb, pos_rows(cmp_pos[:CMP_STRIDE]), pos_rows(cmp_pos[CMP_STRIDE:]),
                             cmp_k_w2.T.astype(BF16), cmp_v_w2.T.astype(BF16), col(nsa_kc_gain), to_b(cos_c), to_b(sin_c))

    n_cmp = (seq - CMP_LEN) // CMP_STRIDE + 1
    nb = seq // SLC_LEN
    c_start = np.arange(nc)[None, :] * CMP_STRIDE
    s_start = np.arange(nb)[:, None] * SLC_LEN
    ov = (c_start < s_start + SLC_LEN) & (c_start + CMP_LEN - 1 >= s_start) & (np.arange(nc)[None, :] < n_cmp)
    ov_t = jnp.asarray(ov.astype(np.float32)).astype(BF16)

    oc_t, selb = _nsa_cmp(q_t, kcmp, vcmp_t, ov_t, batch, seq, tq_nsa)
    os_t = _nsa_sel(q_t, k_slc, v_slc_t, selb, batch, seq, tq_nsa, tk_sel)
    ow_t = _nsa_win(q_t, k_win, v_win_t, batch, seq, tq_nsa)

    cos_b, sin_b = _rope_tables_t(pos, B_ROPE)
    q_m, k_m, v_m_t = _mla_prep(proj_t, cos_b, sin_b, col(mla_q_lora_gain), col(mla_kv_lora_gain),
                                col(mla_q_gain), col(mla_k_gain), mla_w_uq.T.astype(BF16), mla_w_ukv.T.astype(BF16), tt)
    ob_t = _mla_attn(q_m, k_m, v_m_t, batch, seq, tq_mla, tk_mla)
    return oc_t, os_t, ow_t, gates_t, ob_t


def _peer(hn_t, h_t, peer_w_q, peer_sub_keys, peer_u, peer_v):
    cut, g1, r2, g2 = _peer_route(hn_t, peer_w_q.T.astype(BF16), peer_sub_keys.astype(BF16), PEER_ROUTE_TILE)
    return _peer_ffn(hn_t, h_t, peer_u.astype(BF16), peer_v.T.astype(BF16),
                     cut.transpose(1, 0, 2), g1.transpose(1, 0, 2), r2, g2, TOKEN_TILE, PEER_EXPERT_TILE)


def _layer(x, positions, norm1_gain, w_in, nsa_q_gain, nsa_kc_gain, nsa_ks_gain, nsa_kw_gain,
           cmp_pos, cmp_k_w1, cmp_k_w2, cmp_v_w1, cmp_v_w2,
           mla_q_lora_gain, mla_w_uq, mla_kv_lora_gain, mla_w_ukv, mla_q_gain, mla_k_gain,
           out_gain_a, out_gain_b, w_out, norm2_gain, peer_w_q, peer_sub_keys, peer_u, peer_v):
    batch, seq, d = x.shape
    oc_t, os_t, ow_t, gates_t, ob_t = _mixers(
        x, positions, norm1_gain, w_in, nsa_q_gain, nsa_kc_gain, nsa_ks_gain, nsa_kw_gain,
        cmp_pos, cmp_k_w1, cmp_k_w2, cmp_v_w1, cmp_v_w2,
        mla_q_lora_gain, mla_w_uq, mla_kv_lora_gain, mla_w_ukv, mla_q_gain, mla_k_gain)
    h_t, hn_t = _out_proj(oc_t, os_t, ow_t, gates_t, ob_t, x.reshape(batch * seq, d), _col(out_gain_a), _col(out_gain_b),
                          w_out.T.astype(BF16), _col(norm2_gain), TOKEN_TILE)
    return _peer(hn_t, h_t, peer_w_q, peer_sub_keys, peer_u, peer_v).reshape(batch, seq, d)


def kernel(x, positions, norm1_gain, w_in, nsa_q_gain, nsa_kc_gain, nsa_ks_gain, nsa_kw_gain, cmp_pos, cmp_k_w1, cmp_k_w2, cmp_v_w1, cmp_v_w2, mla_q_lora_gain, mla_w_uq, mla_kv_lora_gain, mla_w_ukv, mla_q_gain, mla_k_gain, out_gain_a, out_gain_b, w_out, norm2_gain, peer_w_q, peer_sub_keys, peer_u, peer_v):
    h = x
    for l in range(norm1_gain.shape[0]):
        h = _layer(h, positions, norm1_gain[l], w_in[l], nsa_q_gain[l], nsa_kc_gain[l], nsa_ks_gain[l], nsa_kw_gain[l],
                   cmp_pos[l], cmp_k_w1[l], cmp_k_w2[l], cmp_v_w1[l], cmp_v_w2[l],
                   mla_q_lora_gain[l], mla_w_uq[l], mla_kv_lora_gain[l], mla_w_ukv[l], mla_q_gain[l], mla_k_gain[l],
                   out_gain_a[l], out_gain_b[l], w_out[l], norm2_gain[l], peer_w_q[l], peer_sub_keys[l],
                   peer_u[l], peer_v[l])
    return h
```

```python
import functools

import jax
import jax.numpy as jnp
import numpy as np
from jax import lax
from jax.experimental import pallas as pl
from jax.experimental.pallas import tpu as pltpu

F32, BF16 = jnp.float32, jnp.bfloat16
EPS = 1e-6
NEG = -1e30
FORCE = 1e9
ROPE_THETA = 500000.0

D_MODEL = 1024
A_HEADS, A_GROUPS, A_DIM = 8, 2, 64
A_REP = A_HEADS // A_GROUPS
A_ROPE = A_DIM // 4
CMP_LEN, CMP_STRIDE, CMP_HIDDEN = 32, 16, 256
SLC_LEN, SLC_TOPK, WINDOW = 64, 16, 512
B_HEADS, Q_LORA, KV_LORA, B_NOPE, B_ROPE, B_V = 8, 256, 128, 64, 32, 64
B_QK = B_NOPE + B_ROPE
P_HEADS, N_KEYS, P_KEY_DIM, P_TOPK = 8, 128, 256, 16
N_EXPERTS = N_KEYS * N_KEYS

ROW_Q, ROW_KC, ROW_VC, ROW_KS, ROW_VS, ROW_KW, ROW_VW = 0, 512, 640, 768, 896, 1024, 1152
ROW_CQ, ROW_CKV, ROW_KR, ROW_GATE, PROJ_ROWS = 1280, 1536, 1664, 1696, 1728
GATE_ROWS = 32

VMEM_LIMIT = 56 * 1024 * 1024
NT_DIMS = (((1,), (1,)), ((), ()))


def _params(n_axes):
    return pltpu.CompilerParams(dimension_semantics=("arbitrary",) * n_axes, vmem_limit_bytes=VMEM_LIMIT)


def _dot(a, b):
    return jnp.dot(a, b, preferred_element_type=F32)


def _dot_nt(a, b):
    return lax.dot_general(a, b, NT_DIMS, preferred_element_type=F32)


def _row_sumsq(x):
    sq = x * x
    hi = sq.astype(BF16)
    lo = (sq - hi.astype(F32)).astype(BF16)
    ones = jnp.ones((8, x.shape[1]), BF16)
    return (_dot_nt(ones, hi) + _dot_nt(ones, lo))[0:1, :]


def _rms_rows(x):
    ss = jnp.sum(x * x, axis=0, keepdims=True)
    return x * lax.rsqrt(ss * (1.0 / x.shape[0]) + EPS)


def _rope_rows(y, cos, sin, off, half):
    x1, x2 = y[off:off + half], y[off + half:off + 2 * half]
    parts = [y[:off]] if off else []
    parts += [x1 * cos - x2 * sin, x2 * cos + x1 * sin]
    if off + 2 * half < y.shape[0]:
        parts.append(y[off + 2 * half:])
    return jnp.concatenate(parts, axis=0)


def _gelu_tanh(x):
    return 0.5 * x * (1.0 + jnp.tanh(0.7978845608028654 * (x + 0.044715 * (x * x * x))))


def _in_proj_kernel(x_ref, g_ref, w_ref, o_ref):
    x = x_ref[...]
    xg = (x * g_ref[...]).astype(BF16)
    p = _dot_nt(w_ref[...], xg)
    rinv = lax.rsqrt(_row_sumsq(x) * (1.0 / x.shape[1]) + EPS)
    o_ref[...] = p * rinv


def _in_proj(x2, gain, w_t, tt):
    t, d = x2.shape
    rows = w_t.shape[0]
    return pl.pallas_call(
        _in_proj_kernel,
        grid=(t // tt,),
        in_specs=[pl.BlockSpec((tt, d), lambda i: (i, 0)),
                  pl.BlockSpec((1, d), lambda i: (0, 0)),
                  pl.BlockSpec((rows, d), lambda i: (0, 0))],
        out_specs=pl.BlockSpec((rows, tt), lambda i: (0, i)),
        out_shape=jax.ShapeDtypeStruct((rows, t), F32),
        compiler_params=_params(1),
        name="in_proj",
    )(x2, gain, w_t)


def _nsa_prep_kernel(q_ref, kc_ref, vc_ref, ks_ref, vs_ref, kw_ref, vw_ref, gt_ref, cos_ref, sin_ref,
                     qg_ref, ksg_ref, kwg_ref,
                     qo_ref, kco_ref, vco_ref, kso_ref, vso_ref, kwo_ref, vwo_ref, gto_ref):
    cos, sin = cos_ref[...], sin_ref[...]
    tt = cos.shape[1]
    for h in range(A_HEADS):
        y = _rms_rows(q_ref[h * A_DIM:(h + 1) * A_DIM, :]) * qg_ref[...]
        y = _rope_rows(y, cos, sin, 0, A_ROPE // 2) * (A_DIM ** -0.5)
        qo_ref[h * A_DIM:(h + 1) * A_DIM, :] = y.astype(BF16)
    kco_ref[...] = kc_ref[...].T
    vco_ref[...] = vc_ref[...].T
    zeros = jnp.zeros((A_DIM, tt), F32)
    for g in range(A_GROUPS):
        for src, gain, dst in ((ks_ref, ksg_ref, kso_ref), (kw_ref, kwg_ref, kwo_ref)):
            y = _rms_rows(src[g * A_DIM:(g + 1) * A_DIM, :]) * gain[...]
            y = _rope_rows(y, cos, sin, 0, A_ROPE // 2)
            dst[g] = jnp.concatenate([y, zeros], axis=0).T.astype(BF16)
    vso_ref[...] = vs_ref[...].astype(BF16)
    vwo_ref[...] = vw_ref[...].astype(BF16)
    gto_ref[...] = 1.0 / (1.0 + jnp.exp(-gt_ref[...]))


def _nsa_prep(proj_t, cos_t, sin_t, q_gain, ks_gain, kw_gain, tt):
    t = proj_t.shape[1]
    gw = A_GROUPS * A_DIM
    row_spec = lambda rows, start: pl.BlockSpec((rows, tt), lambda i: (start // rows, i))
    col = lambda n: pl.BlockSpec((n, 1), lambda i: (0, 0))
    return pl.pallas_call(
        _nsa_prep_kernel,
        grid=(t // tt,),
        in_specs=[row_spec(512, ROW_Q), row_spec(gw, ROW_KC), row_spec(gw, ROW_VC), row_spec(gw, ROW_KS),
                  row_spec(gw, ROW_VS), row_spec(gw, ROW_KW), row_spec(gw, ROW_VW), row_spec(GATE_ROWS, ROW_GATE),
                  pl.BlockSpec((A_ROPE // 2, tt), lambda i: (0, i)), pl.BlockSpec((A_ROPE // 2, tt), lambda i: (0, i)),
                  col(A_DIM), col(A_DIM), col(A_DIM)],
        out_specs=[pl.BlockSpec((512, tt), lambda i: (0, i)),
                   pl.BlockSpec((tt, gw), lambda i: (i, 0)),
                   pl.BlockSpec((tt, gw), lambda i: (i, 0)),
                   pl.BlockSpec((A_GROUPS, tt, 128), lambda i: (0, i, 0)),
                   pl.BlockSpec((gw, tt), lambda i: (0, i)),
                   pl.BlockSpec((A_GROUPS, tt, 128), lambda i: (0, i, 0)),
                   pl.BlockSpec((gw, tt), lambda i: (0, i)),
                   pl.BlockSpec((GATE_ROWS, tt), lambda i: (0, i))],
        out_shape=[jax.ShapeDtypeStruct((512, t), BF16),
                   jax.ShapeDtypeStruct((t, gw), F32),
                   jax.ShapeDtypeStruct((t, gw), F32),
                   jax.ShapeDtypeStruct((A_GROUPS, t, 128), BF16),
                   jax.ShapeDtypeStruct((gw, t), BF16),
                   jax.ShapeDtypeStruct((A_GROUPS, t, 128), BF16),
                   jax.ShapeDtypeStruct((gw, t), BF16),
                   jax.ShapeDtypeStruct((GATE_ROWS, t), F32)],
        compiler_params=_params(1),
        name="nsa_prep",
    )(proj_t, proj_t, proj_t, proj_t, proj_t, proj_t, proj_t, proj_t, cos_t, sin_t, q_gain, ks_gain, kw_gain)


def _compress_kernel(kc_ref, vc_ref, w1ka_ref, w1kb_ref, w1va_ref, w1vb_ref, plo_ref, phi_ref,
                     w2k_ref, w2v_ref, kg_ref, cos_ref, sin_ref, ko_ref, vo_ref):
    nc = kc_ref.shape[1]
    zeros = jnp.zeros((A_DIM, nc), F32)
    for src, w1a, w1b, w2, is_k in ((kc_ref, w1ka_ref, w1kb_ref, w2k_ref, True),
                                    (vc_ref, w1va_ref, w1vb_ref, w2v_ref, False)):
        x = src[0]
        xlo = (x + plo_ref[...]).astype(BF16)
        xhi = (x + phi_ref[...]).astype(BF16)
        for g in range(A_GROUPS):
            first = _dot_nt(w1a[g], xlo)
            second = _dot_nt(w1b[g], xhi)
            hid = _gelu_tanh(first + pltpu.roll(second, nc - 1, axis=1)).astype(BF16)
            c = _dot(w2[...], hid)
            if is_k:
                y = _rope_rows(_rms_rows(c) * kg_ref[...], cos_ref[0], sin_ref[0], 0, A_ROPE // 2)
                ko_ref[0, g] = jnp.concatenate([y, zeros], axis=0).T.astype(BF16)
            else:
                vo_ref[0, g] = c.astype(BF16)


def _compress(kc_chunks, vc_chunks, w1ka, w1kb, w1va, w1vb, plo, phi, w2k_t, w2v_t, kc_gain, cos_c, sin_c):
    b, nc, cw = kc_chunks.shape
    full = lambda a: pl.BlockSpec(a.shape, lambda i: (0,) * a.ndim)
    return pl.pallas_call(
        _compress_kernel,
        grid=(b,),
        in_specs=[pl.BlockSpec((1, nc, cw), lambda i: (i, 0, 0)), pl.BlockSpec((1, nc, cw), lambda i: (i, 0, 0)),
                  full(w1ka), full(w1kb), full(w1va), full(w1vb), full(plo), full(phi), full(w2k_t), full(w2v_t),
                  full(kc_gain),
                  pl.BlockSpec((1, A_ROPE // 2, nc), lambda i: (i, 0, 0)),
                  pl.BlockSpec((1, A_ROPE // 2, nc), lambda i: (i, 0, 0))],
        out_specs=[pl.BlockSpec((1, A_GROUPS, nc, 128), lambda i: (i, 0, 0, 0)),
                   pl.BlockSpec((1, A_GROUPS, A_DIM, nc), lambda i: (i, 0, 0, 0))],
        out_shape=[jax.ShapeDtypeStruct((b, A_GROUPS, nc, 128), BF16),
                   jax.ShapeDtypeStruct((b, A_GROUPS, A_DIM, nc), BF16)],
        compiler_params=_params(1),
        name="nsa_compress",
    )(kc_chunks, vc_chunks, w1ka, w1kb, w1va, w1vb, plo, phi, w2k_t, w2v_t, kc_gain, cos_c, sin_c)


def _stack_heads(q, tq):
    qs = jnp.concatenate([q[r * A_DIM:(r + 1) * A_DIM, :] for r in range(A_REP)], axis=1)
    return jnp.concatenate([qs, jnp.zeros_like(qs)], axis=0)


def _nsa_cmp_kernel(q_ref, k_ref, v_ref, ov_ref, o_ref, sb_ref, *, tq):
    i = pl.program_id(2)
    n = A_REP * tq
    nc = k_ref.shape[2]
    nb = ov_ref.shape[0]
    qp = _stack_heads(q_ref[...], tq)
    s = _dot(k_ref[0, 0], qp)
    cmp_end = lax.broadcasted_iota(jnp.int32, (nc, n), 0) * CMP_STRIDE + (CMP_LEN - 1)
    tok = i * tq + (lax.broadcasted_iota(jnp.int32, (nc, n), 1) & (tq - 1))
    mask = cmp_end <= tok
    s = jnp.where(mask, s, NEG)
    m = jnp.max(s, axis=0, keepdims=True)
    p = jnp.where(mask, jnp.exp(s - m), 0.0)
    l = jnp.sum(p, axis=0, keepdims=True)
    inv = jnp.where(l > 0.0, 1.0 / l, 0.0)
    pn = (p * inv).astype(BF16)
    oc = _dot(v_ref[0, 0], pn)
    for r in range(A_REP):
        o_ref[r * A_DIM:(r + 1) * A_DIM, :] = oc[:, r * tq:(r + 1) * tq]
    imp4 = _dot(ov_ref[...], pn)
    imp = imp4[:, 0:tq]
    for r in range(1, A_REP):
        imp = imp + imp4[:, r * tq:(r + 1) * tq]

    blk = lax.broadcasted_iota(jnp.int32, (nb, tq), 0)
    t = i * tq + lax.broadcasted_iota(jnp.int32, (nb, tq), 1)
    forced = (blk == (t >> 6)) | (blk == 0)
    v = jnp.where(forced, FORCE, jnp.where(blk * SLC_LEN <= t, imp, NEG))
    blk_f = blk.astype(F32)
    sel = jnp.zeros((nb, tq), F32)
    for _ in range(min(SLC_TOPK, nb)):
        mx = jnp.max(v, axis=0, keepdims=True)
        first = jnp.min(jnp.where(v == mx, blk_f, float(nb)), axis=0, keepdims=True)
        hit = blk_f == first
        sel = jnp.where(hit, 1.0, sel)
        v = jnp.where(hit, -jnp.inf, v)
    sb_ref[0] = jnp.where(sel > 0.0, 0.0, NEG)


def _nsa_cmp(q_t, kcmp, vcmp_t, ov_t, batch, seq, tq):
    nq = seq // tq
    nc = kcmp.shape[2]
    nb = ov_t.shape[0]
    t = q_t.shape[1]
    gr = A_REP * A_DIM
    return pl.pallas_call(
        functools.partial(_nsa_cmp_kernel, tq=tq),
        grid=(batch, A_GROUPS, nq),
        in_specs=[pl.BlockSpec((gr, tq), lambda b, g, i: (g, b * nq + i)),
                  pl.BlockSpec((1, 1, nc, 128), lambda b, g, i: (b, g, 0, 0)),
                  pl.BlockSpec((1, 1, A_DIM, nc), lambda b, g, i: (b, g, 0, 0)),
                  pl.BlockSpec((nb, nc), lambda b, g, i: (0, 0))],
        out_specs=[pl.BlockSpec((gr, tq), lambda b, g, i: (g, b * nq + i)),
                   pl.BlockSpec((1, nb, tq), lambda b, g, i: (g, 0, b * nq + i))],
        out_shape=[jax.ShapeDtypeStruct((A_HEADS * A_DIM, t), F32),
                   jax.ShapeDtypeStruct((A_GROUPS, nb, t), F32)],
        compiler_params=_params(3),
        name="nsa_cmp",
    )(q_t, kcmp, vcmp_t, ov_t)


def _flash_init(m_scr, l_scr, acc_scr):
    m_scr[...] = jnp.full(m_scr.shape, -jnp.inf, F32)
    l_scr[...] = jnp.zeros(l_scr.shape, F32)
    acc_scr[...] = jnp.zeros(acc_scr.shape, F32)


def _flash_update(s, v_t, m_scr, l_scr, acc_scr):
    m_prev = m_scr[...]
    m_new = jnp.maximum(m_prev, jnp.max(s, axis=0, keepdims=True))
    alpha = jnp.exp(m_prev - m_new)
    p = jnp.exp(s - m_new)
    l_scr[...] = alpha * l_scr[...] + jnp.sum(p, axis=0, keepdims=True)
    acc_scr[...] = alpha * acc_scr[...] + _dot(v_t, p.astype(BF16))
    m_scr[...] = m_new


def _causal_pairs(seq, tq, tk):
    qi, kj = [], []
    for i in range(seq // tq):
        for j in range((i * tq + tq - 1) // tk + 1):
            qi.append(i)
            kj.append(j)
    return jnp.asarray(np.array(qi, np.int32)), jnp.asarray(np.array(kj, np.int32))


def _nsa_sel_kernel(qi_ref, kj_ref, q_ref, k_ref, v_ref, sb_ref, o_ref, qp_scr, sb4_scr, m_scr, l_scr, acc_scr, *, tq, tk):
    p_id = pl.program_id(2)
    i, j = qi_ref[p_id], kj_ref[p_id]
    j_last = (i * tq + tq - 1) // tk
    n = A_REP * tq

    @pl.when(j == 0)
    def _():
        qp_scr[...] = _stack_heads(q_ref[...], tq)
        sb = sb_ref[0]
        for r in range(A_REP):
            sb4_scr[:, r * tq:(r + 1) * tq] = sb
        _flash_init(m_scr, l_scr, acc_scr)

    def step(diagonal):
        s = _dot(k_ref[0], qp_scr[...])
        base = j * (tk // SLC_LEN)
        bias = jnp.concatenate(
            [jnp.broadcast_to(sb4_scr[pl.ds(base + c, 1), :], (SLC_LEN, n)) for c in range(tk // SLC_LEN)], axis=0)
        s = s + bias
        if diagonal:
            kpos = j * tk + lax.broadcasted_iota(jnp.int32, (tk, n), 0)
            tok = i * tq + (lax.broadcasted_iota(jnp.int32, (tk, n), 1) & (tq - 1))
            s = jnp.where(kpos <= tok, s, NEG)
        _flash_update(s, v_ref[...], m_scr, l_scr, acc_scr)

    pl.when(j < j_last)(lambda: step(False))

    @pl.when(j == j_last)
    def _():
        step(True)
        o = acc_scr[...] * (1.0 / l_scr[...])
        for r in range(A_REP):
            o_ref[r * A_DIM:(r + 1) * A_DIM, :] = o[:, r * tq:(r + 1) * tq]


def _nsa_sel(q_t, k_slc, v_slc_t, selb, batch, seq, tq, tk):
    nq, nk = seq // tq, seq // tk
    nb = selb.shape[1]
    t = q_t.shape[1]
    gr = A_REP * A_DIM
    n = A_REP * tq
    qi, kj = _causal_pairs(seq, tq, tk)
    grid_spec = pltpu.PrefetchScalarGridSpec(
        num_scalar_prefetch=2,
        grid=(batch, A_GROUPS, int(qi.shape[0])),
        in_specs=[pl.BlockSpec((gr, tq), lambda b, g, p, qi, kj: (g, b * nq + qi[p])),
                  pl.BlockSpec((1, tk, 128), lambda b, g, p, qi, kj: (g, b * nk + kj[p], 0)),
                  pl.BlockSpec((A_DIM, tk), lambda b, g, p, qi, kj: (g, b * nk + kj[p])),
                  pl.BlockSpec((1, nb, tq), lambda b, g, p, qi, kj: (g, 0, b * nq + qi[p]))],
        out_specs=pl.BlockSpec((gr, tq), lambda b, g, p, qi, kj: (g, b * nq + qi[p])),
        scratch_shapes=[pltpu.VMEM((128, n), BF16), pltpu.VMEM((nb, n), F32),
                        pltpu.VMEM((1, n), F32), pltpu.VMEM((1, n), F32), pltpu.VMEM((A_DIM, n), F32)],
    )
    return pl.pallas_call(
        functools.partial(_nsa_sel_kernel, tq=tq, tk=tk),
        grid_spec=grid_spec,
        out_shape=jax.ShapeDtypeStruct((A_HEADS * A_DIM, t), F32),
        compiler_params=_params(3),
        name="nsa_sel",
    )(qi, kj, q_t, k_slc, v_slc_t, selb)


def _nsa_win_kernel(q_ref, k_ref, v_ref, o_ref, qp_scr, m_scr, l_scr, acc_scr, *, tq):
    i, c = pl.program_id(2), pl.program_id(3)
    n_back = WINDOW // tq
    n = A_REP * tq
    kt = i - n_back + c

    @pl.when(c == 0)
    def _():
        qp_scr[...] = _stack_heads(q_ref[...], tq)
        _flash_init(m_scr, l_scr, acc_scr)

    @pl.when(kt >= 0)
    def _():
        s = _dot(k_ref[0], qp_scr[...])
        kpos = kt * tq + lax.broadcasted_iota(jnp.int32, (tq, n), 0)
        tok = i * tq + (lax.broadcasted_iota(jnp.int32, (tq, n), 1) & (tq - 1))
        s = jnp.where((kpos <= tok) & (kpos > tok - WINDOW), s, NEG)
        _flash_update(s, v_ref[...], m_scr, l_scr, acc_scr)

    @pl.when(c == n_back)
    def _():
        o = acc_scr[...] * (1.0 / l_scr[...])
        for r in range(A_REP):
            o_ref[r * A_DIM:(r + 1) * A_DIM, :] = o[:, r * tq:(r + 1) * tq]


def _nsa_win(q_t, k_win, v_win_t, batch, seq, tq):
    nq = seq // tq
    n_back = WINDOW // tq
    t = q_t.shape[1]
    gr = A_REP * A_DIM
    n = A_REP * tq
    kidx = lambda b, i, c: b * nq + jnp.maximum(i - n_back + c, 0)
    return pl.pallas_call(
        functools.partial(_nsa_win_kernel, tq=tq),
        grid=(batch, A_GROUPS, nq, n_back + 1),
        in_specs=[pl.BlockSpec((gr, tq), lambda b, g, i, c: (g, b * nq + i)),
                  pl.BlockSpec((1, tq, 128), lambda b, g, i, c: (g, kidx(b, i, c), 0)),
                  pl.BlockSpec((A_DIM, tq), lambda b, g, i, c: (g, kidx(b, i, c)))],
        out_specs=pl.BlockSpec((gr, tq), lambda b, g, i, c: (g, b * nq + i)),
        out_shape=jax.ShapeDtypeStruct((A_HEADS * A_DIM, t), F32),
        scratch_shapes=[pltpu.VMEM((128, n), BF16), pltpu.VMEM((1, n), F32), pltpu.VMEM((1, n), F32),
                        pltpu.VMEM((A_DIM, n), F32)],
        compiler_params=_params(4),
        name="nsa_win",
    )(q_t, k_win, v_win_t)


def _mla_prep_kernel(cq_ref, ckv_ref, kr_ref, cos_ref, sin_ref, qlg_ref, kvlg_ref, qg_ref, kg_ref, wuq_ref, wukv_ref,
                     qo_ref, ko_ref, vo_ref):
    cos, sin = cos_ref[...], sin_ref[...]
    tt = cos.shape[1]
    q_all = _dot(wuq_ref[...], (_rms_rows(cq_ref[...]) * qlg_ref[...]).astype(BF16))
    kv_all = _dot(wukv_ref[...], (_rms_rows(ckv_ref[...]) * kvlg_ref[...]).astype(BF16))
    kr = kr_ref[...]
    pad = jnp.zeros((128 - B_QK, tt), F32)
    for h in range(B_HEADS):
        y = _rms_rows(q_all[h * B_QK:(h + 1) * B_QK]) * qg_ref[...]
        y = _rope_rows(y, cos, sin, B_NOPE, B_ROPE // 2) * (B_QK ** -0.5)
        qo_ref[h] = jnp.concatenate([y, pad], axis=0).astype(BF16)
        base = h * (B_NOPE + B_V)
        k = jnp.concatenate([kv_all[base:base + B_NOPE], kr], axis=0)
        y = _rope_rows(_rms_rows(k) * kg_ref[...], cos, sin, B_NOPE, B_ROPE // 2)
        ko_ref[h] = jnp.concatenate([y, pad], axis=0).T.astype(BF16)
        vo_ref[h] = kv_all[base + B_NOPE:base + B_NOPE + B_V].astype(BF16)


def _mla_prep(proj_t, cos_t, sin_t, q_lora_gain, kv_lora_gain, q_gain, k_gain, wuq_t, wukv_t, tt):
    t = proj_t.shape[1]
    row_spec = lambda rows, start: pl.BlockSpec((rows, tt), lambda i: (start // rows, i))
    full = lambda a: pl.BlockSpec(a.shape, lambda i: (0,) * a.ndim)
    return pl.pallas_call(
        _mla_prep_kernel,
        grid=(t // tt,),
        in_specs=[row_spec(Q_LORA, ROW_CQ), row_spec(KV_LORA, ROW_CKV), row_spec(B_ROPE, ROW_KR),
                  pl.BlockSpec((B_ROPE // 2, tt), lambda i: (0, i)), pl.BlockSpec((B_ROPE // 2, tt), lambda i: (0, i)),
                  full(q_lora_gain), full(kv_lora_gain), full(q_gain), full(k_gain), full(wuq_t), full(wukv_t)],
        out_specs=[pl.BlockSpec((B_HEADS, 128, tt), lambda i: (0, 0, i)),
                   pl.BlockSpec((B_HEADS, tt, 128), lambda i: (0, i, 0)),
                   pl.BlockSpec((B_HEADS, B_V, tt), lambda i: (0, 0, i))],
        out_shape=[jax.ShapeDtypeStruct((B_HEADS, 128, t), BF16),
                   jax.ShapeDtypeStruct((B_HEADS, t, 128), BF16),
                   jax.ShapeDtypeStruct((B_HEADS, B_V, t), BF16)],
        compiler_params=_params(1),
        name="mla_prep",
    )(proj_t, proj_t, proj_t, cos_t, sin_t, q_lora_gain, kv_lora_gain, q_gain, k_gain, wuq_t, wukv_t)


def _mla_attn_kernel(qi_ref, kj_ref, q_ref, k_ref, v_ref, o_ref, m_scr, l_scr, acc_scr, *, tq, tk):
    p_id = pl.program_id(2)
    i, j = qi_ref[p_id], kj_ref[p_id]
    j_last = (i * tq + tq - 1) // tk

    pl.when(j == 0)(lambda: _flash_init(m_scr, l_scr, acc_scr))

    def step(diagonal):
        s = _dot(k_ref[0], q_ref[0])
        if diagonal:
            kpos = j * tk + lax.broadcasted_iota(jnp.int32, (tk, tq), 0)
            tok = i * tq + lax.broadcasted_iota(jnp.int32, (tk, tq), 1)
            s = jnp.where(kpos <= tok, s, NEG)
        _flash_update(s, v_ref[0], m_scr, l_scr, acc_scr)

    pl.when(j < j_last)(lambda: step(False))

    @pl.when(j == j_last)
    def _():
        step(True)
        o_ref[...] = acc_scr[...] * (1.0 / l_scr[...])


def _mla_attn(q_m, k_m, v_m_t, batch, seq, tq, tk):
    nq, nk = seq // tq, seq // tk
    t = q_m.shape[2]
    qi, kj = _causal_pairs(seq, tq, tk)
    grid_spec = pltpu.PrefetchScalarGridSpec(
        num_scalar_prefetch=2,
        grid=(batch, B_HEADS, int(qi.shape[0])),
        in_specs=[pl.BlockSpec((1, 128, tq), lambda b, h, p, qi, kj: (h, 0, b * nq + qi[p])),
                  pl.BlockSpec((1, tk, 128), lambda b, h, p, qi, kj: (h, b * nk + kj[p], 0)),
                  pl.BlockSpec((1, B_V, tk), lambda b, h, p, qi, kj: (h, 0, b * nk + kj[p]))],
        out_specs=pl.BlockSpec((B_V, tq), lambda b, h, p, qi, kj: (h, b * nq + qi[p])),
        scratch_shapes=[pltpu.VMEM((1, tq), F32), pltpu.VMEM((1, tq), F32), pltpu.VMEM((B_V, tq), F32)],
    )
    return pl.pallas_call(
        functools.partial(_mla_attn_kernel, tq=tq, tk=tk),
        grid_spec=grid_spec,
        out_shape=jax.ShapeDtypeStruct((B_HEADS * B_V, t), F32),
        compiler_params=_params(3),
        name="mla_attn",
    )(qi, kj, q_m, k_m, v_m_t)


def _out_proj_kernel(oc_ref, os_ref, ow_ref, gt_ref, ob_ref, x_ref, ga_ref, gb_ref, w_ref, g2_ref, h_ref, hn_ref):
    heads = []
    for h in range(A_HEADS):
        rows = slice(h * A_DIM, (h + 1) * A_DIM)
        heads.append(gt_ref[3 * h:3 * h + 1, :] * oc_ref[rows, :] + gt_ref[3 * h + 1:3 * h + 2, :] * os_ref[rows, :]
                     + gt_ref[3 * h + 2:3 * h + 3, :] * ow_ref[rows, :])
    oa = _rms_rows(jnp.concatenate(heads, axis=0)) * ga_ref[...]
    ob = _rms_rows(ob_ref[...]) * gb_ref[...]
    cat = jnp.concatenate([oa, ob], axis=0).astype(BF16)
    hid = x_ref[...].T + _dot(w_ref[...], cat)
    h_ref[...] = hid
    hn_ref[...] = (_rms_rows(hid) * g2_ref[...]).astype(BF16)


def _out_proj(oc_t, os_t, ow_t, gates_t, ob_t, x2, gain_a, gain_b, w_out_t, gain2, tt):
    t, d = x2.shape
    aw = oc_t.shape[0]
    bw = ob_t.shape[0]
    tok = lambda rows: pl.BlockSpec((rows, tt), lambda i: (0, i))
    full = lambda a: pl.BlockSpec(a.shape, lambda i: (0,) * a.ndim)
    return pl.pallas_call(
        _out_proj_kernel,
        grid=(t // tt,),
        in_specs=[tok(aw), tok(aw), tok(aw), tok(GATE_ROWS), tok(bw), pl.BlockSpec((tt, d), lambda i: (i, 0)),
                  full(gain_a), full(gain_b), full(w_out_t), full(gain2)],
        out_specs=[tok(d), tok(d)],
        out_shape=[jax.ShapeDtypeStruct((d, t), F32), jax.ShapeDtypeStruct((d, t), BF16)],
        compiler_params=_params(1),
        name="out_proj",
    )(oc_t, os_t, ow_t, gates_t, ob_t, x2, gain_a, gain_b, w_out_t, gain2)


def _top_ranked(s):
    n, tp = s.shape
    row = lax.broadcasted_iota(jnp.int32, (n, tp), 0).astype(F32)
    slot = lax.broadcasted_iota(jnp.int32, (P_TOPK, tp), 0)

    def body(a, carry):
        v, rank, vals = carry
        mx = jnp.max(v, axis=0, keepdims=True)
        first = jnp.min(jnp.where(v == mx, row, float(n)), axis=0, keepdims=True)
        hit = row == first
        rank = jnp.where(hit, a.astype(F32), rank)
        v = jnp.where(hit, -jnp.inf, v)
        vals = jnp.where(slot == a, mx, vals)
        return v, rank, vals

    _, rank, vals = lax.fori_loop(0, P_TOPK, body,
                                  (s, jnp.full((n, tp), float(P_TOPK), F32), jnp.zeros((P_TOPK, tp), F32)))
    return rank, vals


def _pair_counts(v1, v2):
    k = v1.shape[0]
    slot = lax.broadcasted_iota(jnp.int32, v1.shape, 0).astype(F32)
    top = v1[0:1] + v2[0:1]

    def body(_, carry):
        count, front, z = carry
        mx = jnp.max(front, axis=0, keepdims=True)
        a_star = jnp.min(jnp.where(front == mx, slot, float(k)), axis=0, keepdims=True)
        hit = slot == a_star
        count = count + jnp.where(hit, 1.0, 0.0)
        nxt = jnp.sum(jnp.where(hit, count, 0.0), axis=0, keepdims=True)
        v2_nxt = jnp.sum(jnp.where(slot == nxt, v2, 0.0), axis=0, keepdims=True)
        front = jnp.where(hit, jnp.where(nxt < float(k), v1 + v2_nxt, -jnp.inf), front)
        return count, front, z + jnp.exp(mx - top)

    count, _, z = lax.fori_loop(0, k, body, (jnp.zeros(v1.shape, F32), v1 + v2[0:1], jnp.zeros(top.shape, F32)))
    return count, z


INT32_MIN = -2 ** 31
LANES = 128


def _order_key(bits):
    return bits ^ ((bits >> 31) & 0x7FFFFFFF)


def _top_ranked_pair_fast(s1, s2):
    n, tp = s1.shape
    slot = lax.broadcasted_iota(jnp.int32, (P_TOPK, tp), 0)

    def body(a, carry):
        out = []
        for key, vals in (carry[0:2], carry[2:4]):
            mx = jnp.max(key, axis=0, keepdims=True)
            key = jnp.where(key == mx, jnp.int32(INT32_MIN) + a, key)
            out += [key, jnp.where(slot == a, mx, vals)]
        return tuple(out)

    zeros = jnp.zeros((P_TOPK, tp), jnp.int32)
    k1, t1, k2, t2 = lax.fori_loop(
        0, P_TOPK, body, (_order_key(pltpu.bitcast(s1, jnp.int32)), zeros, _order_key(pltpu.bitcast(s2, jnp.int32)), zeros))
    res, ok = [], None
    for key, vals in ((k1, t1), (k2, t2)):
        removed = key < jnp.int32(INT32_MIN + P_TOPK)
        rank = jnp.where(removed, key - jnp.int32(INT32_MIN), P_TOPK).astype(F32)
        n_removed = jnp.sum(jnp.where(removed, 1, 0), axis=0, keepdims=True)
        good = jnp.max(jnp.abs(n_removed - P_TOPK)) == 0
        ok = good if ok is None else ok & good
        res += [rank, pltpu.bitcast(_order_key(vals), F32)]
    return res[0], res[1], res[2], res[3], ok


def _pack_bf16_twice(x):
    hi = pltpu.bitcast(x.astype(BF16).astype(F32), jnp.int32)
    return hi | lax.shift_right_logical(hi, 16)


def _peer_route_kernel(hn_ref, wq_ref, keys_ref, cut_ref, g1_ref, r2_ref, g2_ref, q_scr, s_scr, rank_scr, vals_scr):
    tp = hn_ref.shape[1]
    half = P_KEY_DIM // 2
    q_scr[...] = _dot(wq_ref[...], hn_ref[...])
    for h in range(P_HEADS):
        q = _rms_rows(q_scr[h * P_KEY_DIM:(h + 1) * P_KEY_DIM, :]).astype(BF16)
        s_scr[0, h] = _dot(keys_ref[h, 0], q[:half])
        s_scr[1, h] = _dot(keys_ref[h, 1], q[half:])

    for h in range(P_HEADS):
        for lt in range(tp // LANES):
            lanes = slice(lt * LANES, (lt + 1) * LANES)
            s1, s2 = s_scr[0, h, :, lanes], s_scr[1, h, :, lanes]

            def put(rank1, vals1, rank2, vals2, h=h, lanes=lanes):
                rank_scr[0, h, :, lanes] = rank1
                rank_scr[1, h, :, lanes] = rank2
                for a in range(P_TOPK):
                    vals_scr[0, a, h:h + 1, lanes] = vals1[a:a + 1]
                    vals_scr[1, a, h:h + 1, lanes] = vals2[a:a + 1]

            rank1, vals1, rank2, vals2, ok = _top_ranked_pair_fast(s1, s2)
            put(rank1, vals1, rank2, vals2)

            @pl.when(jnp.logical_not(ok))
            def _(s1=s1, s2=s2, put=put):
                put(*_top_ranked(s1), *_top_ranked(s2))

    count, z = _pair_counts(vals_scr[0], vals_scr[1])
    for h in range(P_HEADS):
        rank1 = rank_scr[0, h]
        cut = jnp.zeros_like(rank1)
        for a in range(P_TOPK):
            cut = jnp.where(rank1 == float(a), count[a, h:h + 1, :], cut)
        cut_ref[h] = _pack_bf16_twice(cut)
        g1_ref[h] = _pack_bf16_twice(jnp.exp(s_scr[0, h] - vals_scr[0, 0, h:h + 1, :]))
        r2_ref[h] = rank_scr[1, h].astype(BF16)
        g2_ref[h] = (jnp.exp(s_scr[1, h] - vals_scr[1, 0, h:h + 1, :]) * (1.0 / z[0, h:h + 1, :])).astype(BF16)


def _peer_route(hn_t, wq_t, sub_keys, tp):
    d, t = hn_t.shape
    words = jax.ShapeDtypeStruct((P_HEADS, N_KEYS, t), jnp.int32)
    halfs = jax.ShapeDtypeStruct((P_HEADS, N_KEYS, t), BF16)
    ospec = pl.BlockSpec((P_HEADS, N_KEYS, tp), lambda i: (0, 0, i))
    return pl.pallas_call(
        _peer_route_kernel,
        grid=(t // tp,),
        in_specs=[pl.BlockSpec((d, tp), lambda i: (0, i)),
                  pl.BlockSpec(wq_t.shape, lambda i: (0, 0)),
                  pl.BlockSpec(sub_keys.shape, lambda i: (0, 0, 0, 0))],
        out_specs=[ospec, ospec, ospec, ospec],
        out_shape=[words, words, halfs, halfs],
        scratch_shapes=[pltpu.VMEM((P_HEADS * P_KEY_DIM, tp), F32), pltpu.VMEM((2, P_HEADS, N_KEYS, tp), F32),
                        pltpu.VMEM((2, P_HEADS, N_KEYS, tp), F32), pltpu.VMEM((2, P_TOPK, P_HEADS, tp), F32)],
        compiler_params=_params(1),
        name="peer_route",
    )(hn_t, wq_t, sub_keys)


def _peer_ffn_kernel(hn_ref, h_ref, u_ref, v_ref, cut_ref, g1_ref, r2_ref, g2_ref, o_ref, acc_scr, *, te):
    e = pl.program_id(1)

    @pl.when(e == 0)
    def _():
        acc_scr[...] = jnp.zeros(acc_scr.shape, F32)

    hn = hn_ref[...]
    tt = hn.shape[1]

    def rows_bf16(word_row):
        tile = pltpu.bitcast(jnp.broadcast_to(word_row, (8, tt)), BF16)
        return jnp.concatenate([tile] * (N_KEYS // 16), axis=0)

    chunk = 2 * N_KEYS
    weights = []
    for c in range(te // chunk):
        a = _dot(u_ref[c * chunk:(c + 1) * chunk, :], hn)
        for k in range(chunk // N_KEYS):
            ii = c * (chunk // N_KEYS) + k
            gate = jnp.zeros((N_KEYS, tt), BF16)
            for h in range(P_HEADS):
                chosen = r2_ref[h] < rows_bf16(cut_ref[ii, h:h + 1, :])
                gate = gate + jnp.where(chosen, g2_ref[h], jnp.zeros_like(gate)) * rows_bf16(g1_ref[ii, h:h + 1, :])
            weights.append(gate * _gelu_tanh(a[k * N_KEYS:(k + 1) * N_KEYS, :]).astype(BF16))
    acc_scr[...] += _dot(v_ref[...], jnp.concatenate(weights, axis=0))

    @pl.when(e == pl.num_programs(1) - 1)
    def _():
        o_ref[...] = (h_ref[...] + acc_scr[...]).T


def _peer_ffn(hn_t, h_t, u_bf, v_t_bf, cut_k, g1_k, r2, g2, tt, te):
    d, t = hn_t.shape
    n_exp = u_bf.shape[0]
    kpe = te // N_KEYS
    return pl.pallas_call(
        functools.partial(_peer_ffn_kernel, te=te),
        grid=(t // tt, n_exp // te),
        in_specs=[pl.BlockSpec((d, tt), lambda i, e: (0, i)),
                  pl.BlockSpec((d, tt), lambda i, e: (0, i)),
                  pl.BlockSpec((te, d), lambda i, e: (e, 0)),
                  pl.BlockSpec((d, te), lambda i, e: (0, e)),
                  pl.BlockSpec((kpe, P_HEADS, tt), lambda i, e: (e, 0, i)),
                  pl.BlockSpec((kpe, P_HEADS, tt), lambda i, e: (e, 0, i)),
                  pl.BlockSpec((P_HEADS, N_KEYS, tt), lambda i, e: (0, 0, i)),
                  pl.BlockSpec((P_HEADS, N_KEYS, tt), lambda i, e: (0, 0, i))],
        out_specs=pl.BlockSpec((tt, d), lambda i, e: (i, 0)),
        out_shape=jax.ShapeDtypeStruct((t, d), F32),
        scratch_shapes=[pltpu.VMEM((d, tt), F32)],
        compiler_params=_params(2),
        name="peer_ffn",
    )(hn_t, h_t, u_bf, v_t_bf, cut_k, g1_k, r2, g2)


def _rope_tables_t(pos_flat, rot_dim):
    inv_freq = ROPE_THETA ** (-jnp.arange(0, rot_dim, 2, dtype=F32) / rot_dim)
    ang = pos_flat.astype(F32)[None, :] * inv_freq[:, None]
    return jnp.cos(ang), jnp.sin(ang)


def _expand_cmp_w1(w1):
    w = w1.reshape(CMP_LEN, A_DIM, CMP_HIDDEN)
    out = []
    for part in (w[:CMP_STRIDE], w[CMP_STRIDE:]):
        z = jnp.zeros_like(part)
        both = jnp.stack([jnp.concatenate([part, z], axis=1), jnp.concatenate([z, part], axis=1)])
        out.append(both.reshape(A_GROUPS, CMP_STRIDE * A_GROUPS * A_DIM, CMP_HIDDEN).transpose(0, 2, 1).astype(BF16))
    return out


TOKEN_TILE = 512
NSA_Q_TILE = 256
NSA_SEL_K_TILE = 512
MLA_TILE = 512
PEER_ROUTE_TILE = 256
PEER_EXPERT_TILE = 1024


def _col(v):
    return v.reshape(-1, 1).astype(F32)


def _mixers(x, positions, norm1_gain, w_in, nsa_q_gain, nsa_kc_gain, nsa_ks_gain, nsa_kw_gain,
            cmp_pos, cmp_k_w1, cmp_k_w2, cmp_v_w1, cmp_v_w2,
            mla_q_lora_gain, mla_w_uq, mla_kv_lora_gain, mla_w_ukv, mla_q_gain, mla_k_gain):
    batch, seq, d = x.shape
    t = batch * seq
    tt = TOKEN_TILE
    tq_nsa = NSA_Q_TILE
    tk_sel = NSA_SEL_K_TILE
    tq_mla = tk_mla = MLA_TILE
    assert d == D_MODEL and seq % 512 == 0 and seq // SLC_LEN >= SLC_TOPK and WINDOW % tq_nsa == 0
    col = _col
    x2 = x.reshape(t, d)
    pos = positions.reshape(t)

    w_in_t = w_in.T
    gate_lo = sum((512, 128, 128, 128, 128, 128, 128))
    gate_hi = gate_lo + 3 * A_HEADS
    w_in_t = jnp.concatenate([w_in_t[:gate_lo], w_in_t[gate_hi:], w_in_t[gate_lo:gate_hi],
                              jnp.zeros((PROJ_ROWS - w_in_t.shape[0], d), F32)], axis=0).astype(BF16)
    proj_t = _in_proj(x2, norm1_gain.reshape(1, d), w_in_t, tt)

    cos_a, sin_a = _rope_tables_t(pos, A_ROPE)
    q_t, kc_tm, vc_tm, k_slc, v_slc_t, k_win, v_win_t, gates_t = _nsa_prep(
        proj_t, cos_a, sin_a, col(nsa_q_gain), col(nsa_ks_gain), col(nsa_kw_gain), tt)

    nc = seq // CMP_STRIDE
    chunk_w = CMP_STRIDE * A_GROUPS * A_DIM
    w1ka, w1kb = _expand_cmp_w1(cmp_k_w1)
    w1va, w1vb = _expand_cmp_w1(cmp_v_w1)
    pos_rows = lambda p: jnp.broadcast_to(p[:, None, :], (CMP_STRIDE, A_GROUPS, A_DIM)).reshape(1, chunk_w)
    cmp_end = jnp.minimum(jnp.arange(nc) * CMP_STRIDE + CMP_LEN - 1, seq - 1)
    cos_c, sin_c = _rope_tables_t(positions[:, cmp_end].reshape(-1), A_ROPE)
    to_b = lambda a: a.reshape(A_ROPE // 2, batch, nc).transpose(1, 0, 2)
    kcmp, vcmp_t = _compress(kc_tm.reshape(batch, nc, chunk_w), vc_tm.reshape(batch, nc, chunk_w),
                             w1ka, w1kb, w1va, w1vb, pos_rows(cmp_pos[:CMP_STRIDE]), pos_rows(cmp_pos[CMP_STRIDE:]),
                             cmp_k_w2.T.astype(BF16), cmp_v_w2.T.astype(BF16), col(nsa_kc_gain), to_b(cos_c), to_b(sin_c))

    n_cmp = (seq - CMP_LEN) // CMP_STRIDE + 1
    nb = seq // SLC_LEN
    c_start = np.arange(nc)[None, :] * CMP_STRIDE
    s_start = np.arange(nb)[:, None] * SLC_LEN
    ov = (c_start < s_start + SLC_LEN) & (c_start + CMP_LEN - 1 >= s_start) & (np.arange(nc)[None, :] < n_cmp)
    ov_t = jnp.asarray(ov.astype(np.float32)).astype(BF16)

    oc_t, selb = _nsa_cmp(q_t, kcmp, vcmp_t, ov_t, batch, seq, tq_nsa)
    os_t = _nsa_sel(q_t, k_slc, v_slc_t, selb, batch, seq, tq_nsa, tk_sel)
    ow_t = _nsa_win(q_t, k_win, v_win_t, batch, seq, tq_nsa)

    cos_b, sin_b = _rope_tables_t(pos, B_ROPE)
    q_m, k_m, v_m_t = _mla_prep(proj_t, cos_b, sin_b, col(mla_q_lora_gain), col(mla_kv_lora_gain),
                                col(mla_q_gain), col(mla_k_gain), mla_w_uq.T.astype(BF16), mla_w_ukv.T.astype(BF16), tt)
    ob_t = _mla_attn(q_m, k_m, v_m_t, batch, seq, tq_mla, tk_mla)
    return oc_t, os_t, ow_t, gates_t, ob_t


def _peer(hn_t, h_t, peer_w_q, peer_sub_keys, peer_u, peer_v):
    cut, g1, r2, g2 = _peer_route(hn_t, peer_w_q.T.astype(BF16), peer_sub_keys.astype(BF16), PEER_ROUTE_TILE)
    return _peer_ffn(hn_t, h_t, peer_u.astype(BF16), peer_v.T.astype(BF16),
                     cut.transpose(1, 0, 2), g1.transpose(1, 0, 2), r2, g2, TOKEN_TILE, PEER_EXPERT_TILE)


def _layer(x, positions, norm1_gain, w_in, nsa_q_gain, nsa_kc_gain, nsa_ks_gain, nsa_kw_gain,
           cmp_pos, cmp_k_w1, cmp_k_w2, cmp_v_w1, cmp_v_w2,
           mla_q_lora_gain, mla_w_uq, mla_kv_lora_gain, mla_w_ukv, mla_q_gain, mla_k_gain,
           out_gain_a, out_gain_b, w_out, norm2_gain, peer_w_q, peer_sub_keys, peer_u, peer_v):
    batch, seq, d = x.shape
    oc_t, os_t, ow_t, gates_t, ob_t = _mixers(
        x, positions, norm1_gain, w_in, nsa_q_gain, nsa_kc_gain, nsa_ks_gain, nsa_kw_gain,
        cmp_pos, cmp_k_w1, cmp_k_w2, cmp_v_w1, cmp_v_w2,
        mla_q_lora_gain, mla_w_uq, mla_kv_lora_gain, mla_w_ukv, mla_q_gain, mla_k_gain)
    h_t, hn_t = _out_proj(oc_t, os_t, ow_t, gates_t, ob_t, x.reshape(batch * seq, d), _col(out_gain_a), _col(out_gain_b),
                          w_out.T.astype(BF16), _col(norm2_gain), TOKEN_TILE)
    return _peer(hn_t, h_t, peer_w_q, peer_sub_keys, peer_u, peer_v).reshape(batch, seq, d)


def kernel(x, positions, norm1_gain, w_in, nsa_q_gain, nsa_kc_gain, nsa_ks_gain, nsa_kw_gain, cmp_pos, cmp_k_w1, cmp_k_w2, cmp_v_w1, cmp_v_w2, mla_q_lora_gain, mla_w_uq, mla_kv_lora_gain, mla_w_ukv, mla_q_gain, mla_k_gain, out_gain_a, out_gain_b, w_out, norm2_gain, peer_w_q, peer_sub_keys, peer_u, peer_v):
    h = x
    for l in range(norm1_gain.shape[0]):
        h = _layer(h, positions, norm1_gain[l], w_in[l], nsa_q_gain[l], nsa_kc_gain[l], nsa_ks_gain[l], nsa_kw_gain[l],
                   cmp_pos[l], cmp_k_w1[l], cmp_k_w2[l], cmp_v_w1[l], cmp_v_w2[l],
                   mla_q_lora_gain[l], mla_w_uq[l], mla_kv_lora_gain[l], mla_w_ukv[l], mla_q_gain[l], mla_k_gain[l],
                   out_gain_a[l], out_gain_b[l], w_out[l], norm2_gain[l], peer_w_q[l], peer_sub_keys[l],
                   peer_u[l], peer_v[l])
    return h
```

```python
import functools

import jax
import jax.numpy as jnp
import numpy as np
from jax import lax
from jax.experimental import pallas as pl
from jax.experimental.pallas import tpu as pltpu

F32, BF16 = jnp.float32, jnp.bfloat16
EPS = 1e-6
NEG = -1e30
FORCE = 1e9
ROPE_THETA = 500000.0
LOG2E = 1.4426950408889634

D_MODEL = 1024
A_HEADS, A_GROUPS, A_DIM = 8, 2, 64
A_REP = A_HEADS // A_GROUPS
A_ROPE = A_DIM // 4
CMP_LEN, CMP_STRIDE, CMP_HIDDEN = 32, 16, 256
SLC_LEN, SLC_TOPK, WINDOW = 64, 16, 512
B_HEADS, Q_LORA, KV_LORA, B_NOPE, B_ROPE, B_V = 8, 256, 128, 64, 32, 64
B_QK = B_NOPE + B_ROPE
P_HEADS, N_KEYS, P_KEY_DIM, P_TOPK = 8, 128, 256, 16
N_EXPERTS = N_KEYS * N_KEYS

ROW_Q, ROW_KC, ROW_VC, ROW_KS, ROW_VS, ROW_KW, ROW_VW = 0, 512, 640, 768, 896, 1024, 1152
ROW_CQ, ROW_CKV, ROW_KR, ROW_GATE, PROJ_ROWS = 1280, 1536, 1664, 1696, 1728
GATE_ROWS = 32

VMEM_LIMIT = 56 * 1024 * 1024
NT_DIMS = (((1,), (1,)), ((), ()))


def _params(n_axes):
    return pltpu.CompilerParams(dimension_semantics=("arbitrary",) * n_axes, vmem_limit_bytes=VMEM_LIMIT)


def _dot(a, b):
    return jnp.dot(a, b, preferred_element_type=F32)


def _dot_nt(a, b):
    return lax.dot_general(a, b, NT_DIMS, preferred_element_type=F32)


def _row_sumsq(x):
    sq = x * x
    hi = sq.astype(BF16)
    lo = (sq - hi.astype(F32)).astype(BF16)
    ones = jnp.ones((8, x.shape[1]), BF16)
    return (_dot_nt(ones, hi) + _dot_nt(ones, lo))[0:1, :]


def _rms_rows(x):
    ss = jnp.sum(x * x, axis=0, keepdims=True)
    return x * lax.rsqrt(ss * (1.0 / x.shape[0]) + EPS)


def _rope_rows(y, cos, sin, off, half):
    x1, x2 = y[off:off + half], y[off + half:off + 2 * half]
    parts = [y[:off]] if off else []
    parts += [x1 * cos - x2 * sin, x2 * cos + x1 * sin]
    if off + 2 * half < y.shape[0]:
        parts.append(y[off + 2 * half:])
    return jnp.concatenate(parts, axis=0)


def _gelu_tanh(x):
    return 0.5 * x * (1.0 + jnp.tanh(0.7978845608028654 * (x + 0.044715 * (x * x * x))))


def _in_proj_kernel(x_ref, g_ref, w_ref, o_ref):
    x = x_ref[...]
    xg = (x * g_ref[...]).astype(BF16)
    p = _dot_nt(w_ref[...], xg)
    rinv = lax.rsqrt(_row_sumsq(x) * (1.0 / x.shape[1]) + EPS)
    o_ref[...] = p * rinv


def _in_proj(x2, gain, w_t, tt):
    t, d = x2.shape
    rows = w_t.shape[0]
    return pl.pallas_call(
        _in_proj_kernel,
        grid=(t // tt,),
        in_specs=[pl.BlockSpec((tt, d), lambda i: (i, 0)),
                  pl.BlockSpec((1, d), lambda i: (0, 0)),
                  pl.BlockSpec((rows, d), lambda i: (0, 0))],
        out_specs=pl.BlockSpec((rows, tt), lambda i: (0, i)),
        out_shape=jax.ShapeDtypeStruct((rows, t), F32),
        compiler_params=_params(1),
        name="in_proj",
    )(x2, gain, w_t)


def _nsa_prep_kernel(q_ref, kc_ref, vc_ref, ks_ref, vs_ref, kw_ref, vw_ref, gt_ref, cos_ref, sin_ref,
                     qg_ref, ksg_ref, kwg_ref,
                     qo_ref, kco_ref, vco_ref, kso_ref, vso_ref, kwo_ref, vwo_ref, gto_ref, *, seq):
    cos, sin = cos_ref[...], sin_ref[...]
    tt = cos.shape[1]
    nb = seq // SLC_LEN
    for h in range(A_HEADS):
        y = _rms_rows(q_ref[h * A_DIM:(h + 1) * A_DIM, :]) * qg_ref[...]
        y = _rope_rows(y, cos, sin, 0, A_ROPE // 2) * (A_DIM ** -0.5 * LOG2E)
        qo_ref[h * A_DIM:(h + 1) * A_DIM, :] = y.astype(BF16)
    kco_ref[...] = kc_ref[...].T
    vco_ref[...] = vc_ref[...].T
    zeros = jnp.zeros((A_DIM, tt), F32)
    tok = pl.program_id(0) * tt + lax.broadcasted_iota(jnp.int32, (tt, nb), 0)
    block_hot = jnp.where(lax.broadcasted_iota(jnp.int32, (tt, nb), 1) == (tok % seq) // SLC_LEN, 1.0, 0.0)
    for g in range(A_GROUPS):
        for src, gain, dst in ((ks_ref, ksg_ref, kso_ref), (kw_ref, kwg_ref, kwo_ref)):
            y = _rms_rows(src[g * A_DIM:(g + 1) * A_DIM, :]) * gain[...]
            y = _rope_rows(y, cos, sin, 0, A_ROPE // 2)
            k_tm = jnp.concatenate([y, zeros], axis=0).T
            if dst is kso_ref:
                k_tm = jnp.concatenate([k_tm, block_hot], axis=1)
            dst[g] = k_tm.astype(BF16)
        vso_ref[g, 0] = _with_ones(vs_ref[g * A_DIM:(g + 1) * A_DIM, :])
        vwo_ref[g] = _with_ones(vw_ref[g * A_DIM:(g + 1) * A_DIM, :])
    gto_ref[...] = 1.0 / (1.0 + jnp.exp(-gt_ref[...]))


def _nsa_prep(proj_t, cos_t, sin_t, q_gain, ks_gain, kw_gain, tt, seq):
    t = proj_t.shape[1]
    gw = A_GROUPS * A_DIM
    kw = 128 + seq // SLC_LEN
    row_spec = lambda rows, start: pl.BlockSpec((rows, tt), lambda i: (start // rows, i))
    col = lambda n: pl.BlockSpec((n, 1), lambda i: (0, 0))
    return pl.pallas_call(
        functools.partial(_nsa_prep_kernel, seq=seq),
        grid=(t // tt,),
        in_specs=[row_spec(512, ROW_Q), row_spec(gw, ROW_KC), row_spec(gw, ROW_VC), row_spec(gw, ROW_KS),
                  row_spec(gw, ROW_VS), row_spec(gw, ROW_KW), row_spec(gw, ROW_VW), row_spec(GATE_ROWS, ROW_GATE),
                  pl.BlockSpec((A_ROPE // 2, tt), lambda i: (0, i)), pl.BlockSpec((A_ROPE // 2, tt), lambda i: (0, i)),
                  col(A_DIM), col(A_DIM), col(A_DIM)],
        out_specs=[pl.BlockSpec((512, tt), lambda i: (0, i)),
                   pl.BlockSpec((tt, gw), lambda i: (i, 0)),
                   pl.BlockSpec((tt, gw), lambda i: (i, 0)),
                   pl.BlockSpec((A_GROUPS, tt, kw), lambda i: (0, i, 0)),
                   pl.BlockSpec((A_GROUPS, 1, A_DIM + ONES_ROWS, tt), lambda i: (0, i, 0, 0)),
                   pl.BlockSpec((A_GROUPS, tt, 128), lambda i: (0, i, 0)),
                   pl.BlockSpec((A_GROUPS, A_DIM + ONES_ROWS, tt), lambda i: (0, 0, i)),
                   pl.BlockSpec((GATE_ROWS, tt), lambda i: (0, i))],
        out_shape=[jax.ShapeDtypeStruct((512, t), BF16),
                   jax.ShapeDtypeStruct((t, gw), F32),
                   jax.ShapeDtypeStruct((t, gw), F32),
                   jax.ShapeDtypeStruct((A_GROUPS, t, kw), BF16),
                   jax.ShapeDtypeStruct((A_GROUPS, t // tt, A_DIM + ONES_ROWS, tt), BF16),
                   jax.ShapeDtypeStruct((A_GROUPS, t, 128), BF16),
                   jax.ShapeDtypeStruct((A_GROUPS, A_DIM + ONES_ROWS, t), BF16),
                   jax.ShapeDtypeStruct((GATE_ROWS, t), F32)],
        compiler_params=_params(1),
        name="nsa_prep",
    )(proj_t, proj_t, proj_t, proj_t, proj_t, proj_t, proj_t, proj_t, cos_t, sin_t, q_gain, ks_gain, kw_gain)


def _compress_kernel(kc_ref, vc_ref, w1ka_ref, w1kb_ref, w1va_ref, w1vb_ref, plo_ref, phi_ref,
                     w2k_ref, w2v_ref, kg_ref, cos_ref, sin_ref, ko_ref, vo_ref):
    nc = kc_ref.shape[1]
    zeros = jnp.zeros((A_DIM, nc), F32)
    for src, w1a, w1b, w2, is_k in ((kc_ref, w1ka_ref, w1kb_ref, w2k_ref, True),
                                    (vc_ref, w1va_ref, w1vb_ref, w2v_ref, False)):
        x = src[0]
        xlo = (x + plo_ref[...]).astype(BF16)
        xhi = (x + phi_ref[...]).astype(BF16)
        for g in range(A_GROUPS):
            first = _dot_nt(w1a[g], xlo)
            second = _dot_nt(w1b[g], xhi)
            hid = _gelu_tanh(first + pltpu.roll(second, nc - 1, axis=1)).astype(BF16)
            c = _dot(w2[...], hid)
            if is_k:
                y = _rope_rows(_rms_rows(c) * kg_ref[...], cos_ref[0], sin_ref[0], 0, A_ROPE // 2)
                ko_ref[0, g] = jnp.concatenate([y, zeros], axis=0).T.astype(BF16)
            else:
                vo_ref[0, g] = c.astype(BF16)


def _compress(kc_chunks, vc_chunks, w1ka, w1kb, w1va, w1vb, plo, phi, w2k_t, w2v_t, kc_gain, cos_c, sin_c):
    b, nc, cw = kc_chunks.shape
    full = lambda a: pl.BlockSpec(a.shape, lambda i: (0,) * a.ndim)
    return pl.pallas_call(
        _compress_kernel,
        grid=(b,),
        in_specs=[pl.BlockSpec((1, nc, cw), lambda i: (i, 0, 0)), pl.BlockSpec((1, nc, cw), lambda i: (i, 0, 0)),
                  full(w1ka), full(w1kb), full(w1va), full(w1vb), full(plo), full(phi), full(w2k_t), full(w2v_t),
                  full(kc_gain),
                  pl.BlockSpec((1, A_ROPE // 2, nc), lambda i: (i, 0, 0)),
                  pl.BlockSpec((1, A_ROPE // 2, nc), lambda i: (i, 0, 0))],
        out_specs=[pl.BlockSpec((1, A_GROUPS, nc, 128), lambda i: (i, 0, 0, 0)),
                   pl.BlockSpec((1, A_GROUPS, A_DIM, nc), lambda i: (i, 0, 0, 0))],
        out_shape=[jax.ShapeDtypeStruct((b, A_GROUPS, nc, 128), BF16),
                   jax.ShapeDtypeStruct((b, A_GROUPS, A_DIM, nc), BF16)],
        compiler_params=_params(1),
        name="nsa_compress",
    )(kc_chunks, vc_chunks, w1ka, w1kb, w1va, w1vb, plo, phi, w2k_t, w2v_t, kc_gain, cos_c, sin_c)


def _stack_heads(q, tq):
    qs = jnp.concatenate([q[r * A_DIM:(r + 1) * A_DIM, :] for r in range(A_REP)], axis=1)
    return jnp.concatenate([qs, jnp.zeros_like(qs)], axis=0)


def _nsa_cmp_kernel(q_ref, k_ref, v_ref, ov_ref, o_ref, sb_ref, *, tq):
    i = pl.program_id(2)
    n = A_REP * tq
    nc = k_ref.shape[2]
    nb = ov_ref.shape[0]
    qp = _stack_heads(q_ref[...], tq)
    s = _dot(k_ref[0, 0], qp)
    cmp_end = lax.broadcasted_iota(jnp.int32, (nc, n), 0) * CMP_STRIDE + (CMP_LEN - 1)
    tok = i * tq + (lax.broadcasted_iota(jnp.int32, (nc, n), 1) & (tq - 1))
    mask = cmp_end <= tok
    s = jnp.where(mask, s, NEG)
    m = jnp.max(s, axis=0, keepdims=True)
    p = jnp.where(mask, jnp.exp2(s - m), 0.0)
    l = jnp.sum(p, axis=0, keepdims=True)
    inv = jnp.where(l > 0.0, 1.0 / l, 0.0)
    pn = (p * inv).astype(BF16)
    oc = _dot(v_ref[0, 0], pn)
    for r in range(A_REP):
        o_ref[r * A_DIM:(r + 1) * A_DIM, :] = oc[:, r * tq:(r + 1) * tq]
    imp4 = _dot(ov_ref[...], pn)
    imp = imp4[:, 0:tq]
    for r in range(1, A_REP):
        imp = imp + imp4[:, r * tq:(r + 1) * tq]

    blk = lax.broadcasted_iota(jnp.int32, (nb, tq), 0)
    t = i * tq + lax.broadcasted_iota(jnp.int32, (nb, tq), 1)
    forced = (blk == (t >> 6)) | (blk == 0)
    v = jnp.where(forced, FORCE, jnp.where(blk * SLC_LEN <= t, imp, NEG))
    blk_f = blk.astype(F32)
    sel = jnp.zeros((nb, tq), F32)
    for _ in range(min(SLC_TOPK, nb)):
        mx = jnp.max(v, axis=0, keepdims=True)
        first = jnp.min(jnp.where(v == mx, blk_f, float(nb)), axis=0, keepdims=True)
        hit = blk_f == first
        sel = jnp.where(hit, 1.0, sel)
        v = jnp.where(hit, -jnp.inf, v)
    sb_ref[0] = jnp.where(sel > 0.0, 0.0, NEG)


def _nsa_cmp(q_t, kcmp, vcmp_t, ov_t, batch, seq, tq):
    nq = seq // tq
    nc = kcmp.shape[2]
    nb = ov_t.shape[0]
    t = q_t.shape[1]
    gr = A_REP * A_DIM
    return pl.pallas_call(
        functools.partial(_nsa_cmp_kernel, tq=tq),
        grid=(batch, A_GROUPS, nq),
        in_specs=[pl.BlockSpec((gr, tq), lambda b, g, i: (g, b * nq + i)),
                  pl.BlockSpec((1, 1, nc, 128), lambda b, g, i: (b, g, 0, 0)),
                  pl.BlockSpec((1, 1, A_DIM, nc), lambda b, g, i: (b, g, 0, 0)),
                  pl.BlockSpec((nb, nc), lambda b, g, i: (0, 0))],
        out_specs=[pl.BlockSpec((gr, tq), lambda b, g, i: (g, b * nq + i)),
                   pl.BlockSpec((1, nb, tq), lambda b, g, i: (g, 0, b * nq + i))],
        out_shape=[jax.ShapeDtypeStruct((A_HEADS * A_DIM, t), F32),
                   jax.ShapeDtypeStruct((A_GROUPS, nb, t), F32)],
        compiler_params=_params(3),
        name="nsa_cmp",
    )(q_t, kcmp, vcmp_t, ov_t)


ONES_ROWS = 16


def _with_ones(v):
    return jnp.concatenate([v, jnp.ones((ONES_ROWS, v.shape[1]), F32)], axis=0).astype(BF16)


def _flash_init(m_scr, acc_scr):
    m_scr[...] = jnp.full(m_scr.shape, -jnp.inf, F32)
    acc_scr[...] = jnp.zeros(acc_scr.shape, F32)


def _flash_update(s, v_t, m_scr, acc_scr):
    m_prev = m_scr[...]
    m_new = jnp.maximum(m_prev, jnp.max(s, axis=0, keepdims=True))
    alpha = jnp.exp2(m_prev - m_new)
    p = jnp.exp2(s - m_new)
    acc_scr[...] = alpha * acc_scr[...] + _dot(v_t, p.astype(BF16))
    m_scr[...] = m_new


def _flash_result(acc_scr, dv):
    acc = acc_scr[...]
    return acc[:dv] * (1.0 / acc[dv:dv + 1])


def _flash_causal(scores, values, mask_last, n_full, sa_scr, sb_scr, m_scr, acc_scr):
    sa_scr[...] = scores(0)

    def pair(jj, carry):
        j = 2 * jj
        sb_scr[...] = scores(j + 1)
        _flash_update(sa_scr[...], values(j), m_scr, acc_scr)
        sa_scr[...] = scores(j + 2)
        _flash_update(sb_scr[...], values(j + 1), m_scr, acc_scr)
        return carry

    lax.fori_loop(0, n_full // 2, pair, 0)

    @pl.when(n_full % 2 == 0)
    def _():
        _flash_update(mask_last(sa_scr[...]), values(n_full), m_scr, acc_scr)

    @pl.when(n_full % 2 == 1)
    def _():
        sb_scr[...] = scores(n_full)
        _flash_update(sa_scr[...], values(n_full - 1), m_scr, acc_scr)
        _flash_update(mask_last(sb_scr[...]), values(n_full), m_scr, acc_scr)


def _nsa_sel_kernel(q_ref, k_ref, v_ref, sb_ref, o_ref, qa_scr, sa_scr, sb_scr, m_scr, acc_scr, *, tq, tk):
    i = pl.program_id(2)
    n = A_REP * tq
    q = q_ref[...]
    qs = jnp.concatenate([q[r * A_DIM:(r + 1) * A_DIM, :] for r in range(A_REP)], axis=1)
    sb = sb_ref[0].astype(BF16)
    qa_scr[...] = jnp.concatenate([qs, jnp.zeros_like(qs), jnp.concatenate([sb] * A_REP, axis=1)], axis=0)
    _flash_init(m_scr, acc_scr)
    j_diag = (i * tq + tq - 1) // tk

    def scores(j):
        return _dot(k_ref[0, pl.ds(pl.multiple_of(j * tk, tk), tk), :], qa_scr[...])

    def causal(s):
        kpos = j_diag * tk + lax.broadcasted_iota(jnp.int32, (tk, n), 0)
        tok = i * tq + (lax.broadcasted_iota(jnp.int32, (tk, n), 1) & (tq - 1))
        return jnp.where(kpos <= tok, s, NEG)

    _flash_causal(scores, lambda j: v_ref[0, j], causal, j_diag, sa_scr, sb_scr, m_scr, acc_scr)
    o = _flash_result(acc_scr, A_DIM)
    for r in range(A_REP):
        o_ref[r * A_DIM:(r + 1) * A_DIM, :] = o[:, r * tq:(r + 1) * tq]


def _nsa_sel(q_t, k_aug, v_slc_tiles, selb, batch, seq, tq, tk):
    nq, nk = seq // tq, seq // tk
    nb = selb.shape[1]
    t = q_t.shape[1]
    gr = A_REP * A_DIM
    n = A_REP * tq
    kw = k_aug.shape[2]
    return pl.pallas_call(
        functools.partial(_nsa_sel_kernel, tq=tq, tk=tk),
        grid=(batch, A_GROUPS, nq),
        in_specs=[pl.BlockSpec((gr, tq), lambda b, g, i: (g, b * nq + i)),
                  pl.BlockSpec((1, seq, kw), lambda b, g, i: (g, b, 0)),
                  pl.BlockSpec((1, nk, A_DIM + ONES_ROWS, tk), lambda b, g, i: (g, b, 0, 0)),
                  pl.BlockSpec((1, nb, tq), lambda b, g, i: (g, 0, b * nq + i))],
        out_specs=pl.BlockSpec((gr, tq), lambda b, g, i: (g, b * nq + i)),
        out_shape=jax.ShapeDtypeStruct((A_HEADS * A_DIM, t), F32),
        scratch_shapes=[pltpu.VMEM((kw, n), BF16), pltpu.VMEM((tk, n), F32), pltpu.VMEM((tk, n), F32),
                        pltpu.VMEM((1, n), F32), pltpu.VMEM((A_DIM + ONES_ROWS, n), F32)],
        compiler_params=_params(3),
        name="nsa_sel",
    )(q_t, k_aug, v_slc_tiles, selb)


def _nsa_win_kernel(q_ref, k_ref, v_ref, o_ref, qp_scr, m_scr, acc_scr, *, tq):
    i, c = pl.program_id(2), pl.program_id(3)
    n_back = WINDOW // tq
    n = A_REP * tq
    kt = i - n_back + c

    @pl.when(c == 0)
    def _():
        qp_scr[...] = _stack_heads(q_ref[...], tq)
        _flash_init(m_scr, acc_scr)

    @pl.when(kt >= 0)
    def _():
        s = _dot(k_ref[0], qp_scr[...])
        kpos = kt * tq + lax.broadcasted_iota(jnp.int32, (tq, n), 0)
        tok = i * tq + (lax.broadcasted_iota(jnp.int32, (tq, n), 1) & (tq - 1))
        s = jnp.where((kpos <= tok) & (kpos > tok - WINDOW), s, NEG)
        _flash_update(s, v_ref[0], m_scr, acc_scr)

    @pl.when(c == n_back)
    def _():
        o = _flash_result(acc_scr, A_DIM)
        for r in range(A_REP):
            o_ref[r * A_DIM:(r + 1) * A_DIM, :] = o[:, r * tq:(r + 1) * tq]


def _nsa_win(q_t, k_win, v_win_t, batch, seq, tq):
    nq = seq // tq
    n_back = WINDOW // tq
    t = q_t.shape[1]
    gr = A_REP * A_DIM
    n = A_REP * tq
    kidx = lambda b, i, c: b * nq + jnp.maximum(i - n_back + c, 0)
    return pl.pallas_call(
        functools.partial(_nsa_win_kernel, tq=tq),
        grid=(batch, A_GROUPS, nq, n_back + 1),
        in_specs=[pl.BlockSpec((gr, tq), lambda b, g, i, c: (g, b * nq + i)),
                  pl.BlockSpec((1, tq, 128), lambda b, g, i, c: (g, kidx(b, i, c), 0)),
                  pl.BlockSpec((1, A_DIM + ONES_ROWS, tq), lambda b, g, i, c: (g, 0, kidx(b, i, c)))],
        out_specs=pl.BlockSpec((gr, tq), lambda b, g, i, c: (g, b * nq + i)),
        out_shape=jax.ShapeDtypeStruct((A_HEADS * A_DIM, t), F32),
        scratch_shapes=[pltpu.VMEM((128, n), BF16), pltpu.VMEM((1, n), F32),
                        pltpu.VMEM((A_DIM + ONES_ROWS, n), F32)],
        compiler_params=_params(4),
        name="nsa_win",
    )(q_t, k_win, v_win_t)


def _mla_prep_kernel(cq_ref, ckv_ref, kr_ref, cos_ref, sin_ref, qlg_ref, kvlg_ref, qg_ref, kg_ref, wuq_ref, wukv_ref,
                     qo_ref, ko_ref, vo_ref):
    cos, sin = cos_ref[...], sin_ref[...]
    tt = cos.shape[1]
    q_all = _dot(wuq_ref[...], (_rms_rows(cq_ref[...]) * qlg_ref[...]).astype(BF16))
    kv_all = _dot(wukv_ref[...], (_rms_rows(ckv_ref[...]) * kvlg_ref[...]).astype(BF16))
    kr = kr_ref[...]
    pad = jnp.zeros((128 - B_QK, tt), F32)
    for h in range(B_HEADS):
        y = _rms_rows(q_all[h * B_QK:(h + 1) * B_QK]) * qg_ref[...]
        y = _rope_rows(y, cos, sin, B_NOPE, B_ROPE // 2) * (B_QK ** -0.5 * LOG2E)
        qo_ref[h] = jnp.concatenate([y, pad], axis=0).astype(BF16)
        base = h * (B_NOPE + B_V)
        k = jnp.concatenate([kv_all[base:base + B_NOPE], kr], axis=0)
        y = _rope_rows(_rms_rows(k) * kg_ref[...], cos, sin, B_NOPE, B_ROPE // 2)
        ko_ref[h] = jnp.concatenate([y, pad], axis=0).T.astype(BF16)
        vo_ref[h, 0] = _with_ones(kv_all[base + B_NOPE:base + B_NOPE + B_V])


def _mla_prep(proj_t, cos_t, sin_t, q_lora_gain, kv_lora_gain, q_gain, k_gain, wuq_t, wukv_t, tt):
    t = proj_t.shape[1]
    row_spec = lambda rows, start: pl.BlockSpec((rows, tt), lambda i: (start // rows, i))
    full = lambda a: pl.BlockSpec(a.shape, lambda i: (0,) * a.ndim)
    return pl.pallas_call(
        _mla_prep_kernel,
        grid=(t // tt,),
        in_specs=[row_spec(Q_LORA, ROW_CQ), row_spec(KV_LORA, ROW_CKV), row_spec(B_ROPE, ROW_KR),
                  pl.BlockSpec((B_ROPE // 2, tt), lambda i: (0, i)), pl.BlockSpec((B_ROPE // 2, tt), lambda i: (0, i)),
                  full(q_lora_gain), full(kv_lora_gain), full(q_gain), full(k_gain), full(wuq_t), full(wukv_t)],
        out_specs=[pl.BlockSpec((B_HEADS, 128, tt), lambda i: (0, 0, i)),
                   pl.BlockSpec((B_HEADS, tt, 128), lambda i: (0, i, 0)),
                   pl.BlockSpec((B_HEADS, 1, B_V + ONES_ROWS, tt), lambda i: (0, i, 0, 0))],
        out_shape=[jax.ShapeDtypeStruct((B_HEADS, 128, t), BF16),
                   jax.ShapeDtypeStruct((B_HEADS, t, 128), BF16),
                   jax.ShapeDtypeStruct((B_HEADS, t // tt, B_V + ONES_ROWS, tt), BF16)],
        compiler_params=_params(1),
        name="mla_prep",
    )(proj_t, proj_t, proj_t, cos_t, sin_t, q_lora_gain, kv_lora_gain, q_gain, k_gain, wuq_t, wukv_t)


def _mla_attn_kernel(q_ref, k_ref, v_ref, o_ref, sa_scr, sb_scr, m_scr, acc_scr, *, tq, tk):
    i = pl.program_id(2)
    _flash_init(m_scr, acc_scr)
    j_diag = (i * tq + tq - 1) // tk

    def scores(j):
        return _dot(k_ref[0, pl.ds(pl.multiple_of(j * tk, tk), tk), :], q_ref[0])

    def causal(s):
        kpos = j_diag * tk + lax.broadcasted_iota(jnp.int32, (tk, tq), 0)
        tok = i * tq + lax.broadcasted_iota(jnp.int32, (tk, tq), 1)
        return jnp.where(kpos <= tok, s, NEG)

    _flash_causal(scores, lambda j: v_ref[0, j], causal, j_diag, sa_scr, sb_scr, m_scr, acc_scr)
    o_ref[...] = _flash_result(acc_scr, B_V)


def _mla_attn(q_m, k_m, v_m_tiles, batch, seq, tq, tk):
    nq, nk = seq // tq, seq // tk
    t = q_m.shape[2]
    return pl.pallas_call(
        functools.partial(_mla_attn_kernel, tq=tq, tk=tk),
        grid=(batch, B_HEADS, nq),
        in_specs=[pl.BlockSpec((1, 128, tq), lambda b, h, i: (h, 0, b * nq + i)),
                  pl.BlockSpec((1, seq, 128), lambda b, h, i: (h, b, 0)),
                  pl.BlockSpec((1, nk, B_V + ONES_ROWS, tk), lambda b, h, i: (h, b, 0, 0))],
        out_specs=pl.BlockSpec((B_V, tq), lambda b, h, i: (h, b * nq + i)),
        out_shape=jax.ShapeDtypeStruct((B_HEADS * B_V, t), F32),
        scratch_shapes=[pltpu.VMEM((tk, tq), F32), pltpu.VMEM((tk, tq), F32),
                        pltpu.VMEM((1, tq), F32), pltpu.VMEM((B_V + ONES_ROWS, tq), F32)],
        compiler_params=_params(3),
        name="mla_attn",
    )(q_m, k_m, v_m_tiles)


def _out_proj_kernel(oc_ref, os_ref, ow_ref, gt_ref, ob_ref, x_ref, ga_ref, gb_ref, w_ref, g2_ref, h_ref, hn_ref):
    heads = []
    for h in range(A_HEADS):
        rows = slice(h * A_DIM, (h + 1) * A_DIM)
        heads.append(gt_ref[3 * h:3 * h + 1, :] * oc_ref[rows, :] + gt_ref[3 * h + 1:3 * h + 2, :] * os_ref[rows, :]
                     + gt_ref[3 * h + 2:3 * h + 3, :] * ow_ref[rows, :])
    oa = _rms_rows(jnp.concatenate(heads, axis=0)) * ga_ref[...]
    ob = _rms_rows(ob_ref[...]) * gb_ref[...]
    cat = jnp.concatenate([oa, ob], axis=0).astype(BF16)
    hid = x_ref[...].T + _dot(w_ref[...], cat)
    h_ref[...] = hid
    hn_ref[...] = (_rms_rows(hid) * g2_ref[...]).astype(BF16)


def _out_proj(oc_t, os_t, ow_t, gates_t, ob_t, x2, gain_a, gain_b, w_out_t, gain2, tt):
    t, d = x2.shape
    aw = oc_t.shape[0]
    bw = ob_t.shape[0]
    tok = lambda rows: pl.BlockSpec((rows, tt), lambda i: (0, i))
    full = lambda a: pl.BlockSpec(a.shape, lambda i: (0,) * a.ndim)
    return pl.pallas_call(
        _out_proj_kernel,
        grid=(t // tt,),
        in_specs=[tok(aw), tok(aw), tok(aw), tok(GATE_ROWS), tok(bw), pl.BlockSpec((tt, d), lambda i: (i, 0)),
                  full(gain_a), full(gain_b), full(w_out_t), full(gain2)],
        out_specs=[tok(d), tok(d)],
        out_shape=[jax.ShapeDtypeStruct((d, t), F32), jax.ShapeDtypeStruct((d, t), BF16)],
        compiler_params=_params(1),
        name="out_proj",
    )(oc_t, os_t, ow_t, gates_t, ob_t, x2, gain_a, gain_b, w_out_t, gain2)


def _top_ranked(s):
    n, tp = s.shape
    row = lax.broadcasted_iota(jnp.int32, (n, tp), 0).astype(F32)
    slot = lax.broadcasted_iota(jnp.int32, (P_TOPK, tp), 0)

    def body(a, carry):
        v, rank, vals = carry
        mx = jnp.max(v, axis=0, keepdims=True)
        first = jnp.min(jnp.where(v == mx, row, float(n)), axis=0, keepdims=True)
        hit = row == first
        rank = jnp.where(hit, jnp.asarray(a, F32), rank)
        v = jnp.where(hit, -jnp.inf, v)
        vals = jnp.where(slot == a, mx, vals)
        return v, rank, vals

    _, rank, vals = lax.fori_loop(0, P_TOPK, body,
                                  (s, jnp.full((n, tp), float(P_TOPK), F32), jnp.zeros((P_TOPK, tp), F32)))
    return rank, vals


def _pair_counts(v1, v2):
    k = v1.shape[0]
    slot = lax.broadcasted_iota(jnp.int32, v1.shape, 0).astype(F32)
    top = v1[0:1] + v2[0:1]

    def body(_, carry):
        count, front, z = carry
        mx = jnp.max(front, axis=0, keepdims=True)
        a_star = jnp.min(jnp.where(front == mx, slot, float(k)), axis=0, keepdims=True)
        hit = slot == a_star
        count = count + jnp.where(hit, 1.0, 0.0)
        nxt = jnp.sum(jnp.where(hit, count, 0.0), axis=0, keepdims=True)
        v2_nxt = jnp.sum(jnp.where(slot == nxt, v2, 0.0), axis=0, keepdims=True)
        front = jnp.where(hit, jnp.where(nxt < float(k), v1 + v2_nxt, -jnp.inf), front)
        return count, front, z + jnp.exp(mx - top)

    count, _, z = lax.fori_loop(0, k, body, (jnp.zeros(v1.shape, F32), v1 + v2[0:1], jnp.zeros(top.shape, F32)))
    return count, z


INT32_MIN = -2 ** 31
LANES = 128


def _order_key(bits):
    return bits ^ ((bits >> 31) & 0x7FFFFFFF)


def _top_ranked_pair_fast(s1, s2):
    n, tp = s1.shape
    slot = lax.broadcasted_iota(jnp.int32, (P_TOPK, tp), 0)

    def body(a, carry):
        out = []
        for key, vals in (carry[0:2], carry[2:4]):
            mx = jnp.max(key, axis=0, keepdims=True)
            key = jnp.where(key == mx, jnp.int32(INT32_MIN) + a, key)
            out += [key, jnp.where(slot == a, mx, vals)]
        return tuple(out)

    zeros = jnp.zeros((P_TOPK, tp), jnp.int32)
    k1, t1, k2, t2 = lax.fori_loop(
        0, P_TOPK, body, (_order_key(pltpu.bitcast(s1, jnp.int32)), zeros, _order_key(pltpu.bitcast(s2, jnp.int32)), zeros))
    res, ok = [], None
    for key, vals in ((k1, t1), (k2, t2)):
        removed = key < jnp.int32(INT32_MIN + P_TOPK)
        rank = jnp.where(removed, key - jnp.int32(INT32_MIN), P_TOPK).astype(F32)
        n_removed = jnp.sum(jnp.where(removed, 1, 0), axis=0, keepdims=True)
        good = jnp.max(jnp.abs(n_removed - P_TOPK)) == 0
        ok = good if ok is None else ok & good
        res += [rank, pltpu.bitcast(_order_key(vals), F32)]
    return res[0], res[1], res[2], res[3], ok


def _pack_bf16_twice(x):
    hi = pltpu.bitcast(x.astype(BF16).astype(F32), jnp.int32)
    return hi | lax.shift_right_logical(hi, 16)


def _peer_route_kernel(hn_ref, wq_ref, keys_ref, cut_ref, g1_ref, r2_ref, g2_ref, q_scr, s_scr, rank_scr, vals_scr):
    tp = hn_ref.shape[1]
    half = P_KEY_DIM // 2
    q_scr[...] = _dot(wq_ref[...], hn_ref[...])
    for h in range(P_HEADS):
        q = _rms_rows(q_scr[h * P_KEY_DIM:(h + 1) * P_KEY_DIM, :]).astype(BF16)
        s_scr[0, h] = _dot(keys_ref[h, 0], q[:half])
        s_scr[1, h] = _dot(keys_ref[h, 1], q[half:])

    for h in range(P_HEADS):
        for lt in range(tp // LANES):
            lanes = slice(lt * LANES, (lt + 1) * LANES)
            s1, s2 = s_scr[0, h, :, lanes], s_scr[1, h, :, lanes]

            def put(rank1, vals1, rank2, vals2, h=h, lanes=lanes):
                rank_scr[0, h, :, lanes] = rank1
                rank_scr[1, h, :, lanes] = rank2
                for a in range(P_TOPK):
                    vals_scr[0, a, h:h + 1, lanes] = vals1[a:a + 1]
                    vals_scr[1, a, h:h + 1, lanes] = vals2[a:a + 1]

            rank1, vals1, rank2, vals2, ok = _top_ranked_pair_fast(s1, s2)
            put(rank1, vals1, rank2, vals2)

            @pl.when(jnp.logical_not(ok))
            def _(s1=s1, s2=s2, put=put):
                put(*_top_ranked(s1), *_top_ranked(s2))

    count, z = _pair_counts(vals_scr[0], vals_scr[1])
    for h in range(P_HEADS):
        rank1 = rank_scr[0, h]
        cut = jnp.zeros_like(rank1)
        for a in range(P_TOPK):
            cut = jnp.where(rank1 == float(a), count[a, h:h + 1, :], cut)
        cut_ref[h] = _pack_bf16_twice(cut)
        g1_ref[h] = _pack_bf16_twice(jnp.exp(s_scr[0, h] - vals_scr[0, 0, h:h + 1, :]))
        r2_ref[h] = rank_scr[1, h].astype(BF16)
        g2_ref[h] = (jnp.exp(s_scr[1, h] - vals_scr[1, 0, h:h + 1, :]) * (1.0 / z[0, h:h + 1, :])).astype(BF16)


def _peer_route(hn_t, wq_t, sub_keys, tp):
    d, t = hn_t.shape
    words = jax.ShapeDtypeStruct((P_HEADS, N_KEYS, t), jnp.int32)
    halfs = jax.ShapeDtypeStruct((P_HEADS, N_KEYS, t), BF16)
    ospec = pl.BlockSpec((P_HEADS, N_KEYS, tp), lambda i: (0, 0, i))
    return pl.pallas_call(
        _peer_route_kernel,
        grid=(t // tp,),
        in_specs=[pl.BlockSpec((d, tp), lambda i: (0, i)),
                  pl.BlockSpec(wq_t.shape, lambda i: (0, 0)),
                  pl.BlockSpec(sub_keys.shape, lambda i: (0, 0, 0, 0))],
        out_specs=[ospec, ospec, ospec, ospec],
        out_shape=[words, words, halfs, halfs],
        scratch_shapes=[pltpu.VMEM((P_HEADS * P_KEY_DIM, tp), F32), pltpu.VMEM((2, P_HEADS, N_KEYS, tp), F32),
                        pltpu.VMEM((2, P_HEADS, N_KEYS, tp), F32), pltpu.VMEM((2, P_TOPK, P_HEADS, tp), F32)],
        compiler_params=_params(1),
        name="peer_route",
    )(hn_t, wq_t, sub_keys)


def _peer_ffn_kernel(hn_ref, h_ref, u_ref, v_ref, cut_ref, g1_ref, r2_ref, g2_ref, o_ref, acc_scr, *, te):
    e = pl.program_id(1)

    @pl.when(e == 0)
    def _():
        acc_scr[...] = jnp.zeros(acc_scr.shape, F32)

    hn = hn_ref[...]
    tt = hn.shape[1]

    def rows_bf16(word_row):
        tile = pltpu.bitcast(jnp.broadcast_to(word_row, (8, tt)), BF16)
        return jnp.concatenate([tile] * (N_KEYS // 16), axis=0)

    chunk = 2 * N_KEYS
    weights = []
    for c in range(te // chunk):
        a = _dot(u_ref[c * chunk:(c + 1) * chunk, :], hn)
        for k in range(chunk // N_KEYS):
            ii = c * (chunk // N_KEYS) + k
            gate = jnp.zeros((N_KEYS, tt), BF16)
            for h in range(P_HEADS):
                chosen = r2_ref[h] < rows_bf16(cut_ref[ii, h:h + 1, :])
                gate = gate + jnp.where(chosen, g2_ref[h], jnp.zeros_like(gate)) * rows_bf16(g1_ref[ii, h:h + 1, :])
            weights.append(gate * _gelu_tanh(a[k * N_KEYS:(k + 1) * N_KEYS, :]).astype(BF16))
    acc_scr[...] += _dot(v_ref[...], jnp.concatenate(weights, axis=0))

    @pl.when(e == pl.num_programs(1) - 1)
    def _():
        o_ref[...] = (h_ref[...] + acc_scr[...]).T


def _peer_ffn(hn_t, h_t, u_bf, v_t_bf, cut_k, g1_k, r2, g2, tt, te):
    d, t = hn_t.shape
    n_exp = u_bf.shape[0]
    kpe = te // N_KEYS
    return pl.pallas_call(
        functools.partial(_peer_ffn_kernel, te=te),
        grid=(t // tt, n_exp // te),
        in_specs=[pl.BlockSpec((d, tt), lambda i, e: (0, i)),
                  pl.BlockSpec((d, tt), lambda i, e: (0, i)),
                  pl.BlockSpec((te, d), lambda i, e: (e, 0)),
                  pl.BlockSpec((d, te), lambda i, e: (0, e)),
                  pl.BlockSpec((kpe, P_HEADS, tt), lambda i, e: (e, 0, i)),
                  pl.BlockSpec((kpe, P_HEADS, tt), lambda i, e: (e, 0, i)),
                  pl.BlockSpec((P_HEADS, N_KEYS, tt), lambda i, e: (0, 0, i)),
                  pl.BlockSpec((P_HEADS, N_KEYS, tt), lambda i, e: (0, 0, i))],
        out_specs=pl.BlockSpec((tt, d), lambda i, e: (i, 0)),
        out_shape=jax.ShapeDtypeStruct((t, d), F32),
        scratch_shapes=[pltpu.VMEM((d, tt), F32)],
        compiler_params=_params(2),
        name="peer_ffn",
    )(hn_t, h_t, u_bf, v_t_bf, cut_k, g1_k, r2, g2)


def _rope_tables_t(pos_flat, rot_dim):
    inv_freq = ROPE_THETA ** (-jnp.arange(0, rot_dim, 2, dtype=F32) / rot_dim)
    ang = pos_flat.astype(F32)[None, :] * inv_freq[:, None]
    return jnp.cos(ang), jnp.sin(ang)


def _expand_cmp_w1(w1):
    w = w1.reshape(CMP_LEN, A_DIM, CMP_HIDDEN)
    out = []
    for part in (w[:CMP_STRIDE], w[CMP_STRIDE:]):
        z = jnp.zeros_like(part)
        both = jnp.stack([jnp.concatenate([part, z], axis=1), jnp.concatenate([z, part], axis=1)])
        out.append(both.reshape(A_GROUPS, CMP_STRIDE * A_GROUPS * A_DIM, CMP_HIDDEN).transpose(0, 2, 1).astype(BF16))
    return out


TOKEN_TILE = 512
NSA_Q_TILE = 256
NSA_SEL_K_TILE = 512
MLA_TILE = 512
PEER_ROUTE_TILE = 256
PEER_EXPERT_TILE = 1024


def _col(v):
    return v.reshape(-1, 1).astype(F32)


def _mixers(x, positions, norm1_gain, w_in, nsa_q_gain, nsa_kc_gain, nsa_ks_gain, nsa_kw_gain,
            cmp_pos, cmp_k_w1, cmp_k_w2, cmp_v_w1, cmp_v_w2,
            mla_q_lora_gain, mla_w_uq, mla_kv_lora_gain, mla_w_ukv, mla_q_gain, mla_k_gain):
    batch, seq, d = x.shape
    t = batch * seq
    tt = TOKEN_TILE
    tq_nsa = NSA_Q_TILE
    tk_sel = NSA_SEL_K_TILE
    tq_mla = tk_mla = MLA_TILE
    assert tk_sel == tt and tk_mla == tt
    assert d == D_MODEL and seq % 512 == 0 and seq // SLC_LEN >= SLC_TOPK and WINDOW % tq_nsa == 0
    col = _col
    x2 = x.reshape(t, d)
    pos = positions.reshape(t)

    w_in_t = w_in.T
    gate_lo = sum((512, 128, 128, 128, 128, 128, 128))
    gate_hi = gate_lo + 3 * A_HEADS
    w_in_t = jnp.concatenate([w_in_t[:gate_lo], w_in_t[gate_hi:], w_in_t[gate_lo:gate_hi],
                              jnp.zeros((PROJ_ROWS - w_in_t.shape[0], d), F32)], axis=0).astype(BF16)
    proj_t = _in_proj(x2, norm1_gain.reshape(1, d), w_in_t, tt)

    cos_a, sin_a = _rope_tables_t(pos, A_ROPE)
    q_t, kc_tm, vc_tm, k_slc, v_slc_t, k_win, v_win_t, gates_t = _nsa_prep(
        proj_t, cos_a, sin_a, col(nsa_q_gain), col(nsa_ks_gain), col(nsa_kw_gain), tt, seq)

    nc = seq // CMP_STRIDE
    chunk_w = CMP_STRIDE * A_GROUPS * A_DIM
    w1ka, w1kb = _expand_cmp_w1(cmp_k_w1)
    w1va, w1vb = _expand_cmp_w1(cmp_v_w1)
    pos_rows = lambda p: jnp.broadcast_to(p[:, None, :], (CMP_STRIDE, A_GROUPS, A_DIM)).reshape(1, chunk_w)
    cmp_end = jnp.minimum(jnp.arange(nc) * CMP_STRIDE + CMP_LEN - 1, seq - 1)
    cos_c, sin_c = _rope_tables_t(positions[:, cmp_end].reshape(-1), A_ROPE)
    to_b = lambda a: a.reshape(A_ROPE // 2, batch, nc).transpose(1, 0, 2)
    kcmp, vcmp_t = _compress(kc_tm.reshape(batch, nc, chunk_w), vc_tm.reshape(batch, nc, chunk_w),
                             w1ka, w1kb, w1va, w1vb, pos_rows(cmp_pos[:CMP_STRIDE]), pos_rows(cmp_pos[CMP_STRIDE:]),
                             cmp_k_w2.T.astype(BF16), cmp_v_w2.T.astype(BF16), col(nsa_kc_gain), to_b(cos_c), to_b(sin_c))

    n_cmp = (seq - CMP_LEN) // CMP_STRIDE + 1
    nb = seq // SLC_LEN
    c_start = np.arange(nc)[None, :] * CMP_STRIDE
    s_start = np.arange(nb)[:, None] * SLC_LEN
    ov = (c_start < s_start + SLC_LEN) & (c_start + CMP_LEN - 1 >= s_start) & (np.arange(nc)[None, :] < n_cmp)
    ov_t = jnp.asarray(ov.astype(np.float32)).astype(BF16)

    oc_t, selb = _nsa_cmp(q_t, kcmp, vcmp_t, ov_t, batch, seq, tq_nsa)
    os_t = _nsa_sel(q_t, k_slc, v_slc_t, selb, batch, seq, tq_nsa, tk_sel)
    ow_t = _nsa_win(q_t, k_win, v_win_t, batch, seq, tq_nsa)

    cos_b, sin_b = _rope_tables_t(pos, B_ROPE)
    q_m, k_m, v_m_t = _mla_prep(proj_t, cos_b, sin_b, col(mla_q_lora_gain), col(mla_kv_lora_gain),
                                col(mla_q_gain), col(mla_k_gain), mla_w_uq.T.astype(BF16), mla_w_ukv.T.astype(BF16), tt)
    ob_t = _mla_attn(q_m, k_m, v_m_t, batch, seq, tq_mla, tk_mla)
    return oc_t, os_t, ow_t, gates_t, ob_t


def _peer(hn_t, h_t, peer_w_q, peer_sub_keys, peer_u, peer_v):
    cut, g1, r2, g2 = _peer_route(hn_t, peer_w_q.T.astype(BF16), peer_sub_keys.astype(BF16), PEER_ROUTE_TILE)
    return _peer_ffn(hn_t, h_t, peer_u.astype(BF16), peer_v.T.astype(BF16),
                     cut.transpose(1, 0, 2), g1.transpose(1, 0, 2), r2, g2, TOKEN_TILE, PEER_EXPERT_TILE)


def _layer(x, positions, norm1_gain, w_in, nsa_q_gain, nsa_kc_gain, nsa_ks_gain, nsa_kw_gain,
           cmp_pos, cmp_k_w1, cmp_k_w2, cmp_v_w1, cmp_v_w2,
           mla_q_lora_gain, mla_w_uq, mla_kv_lora_gain, mla_w_ukv, mla_q_gain, mla_k_gain,
           out_gain_a, out_gain_b, w_out, norm2_gain, peer_w_q, peer_sub_keys, peer_u, peer_v):
    batch, seq, d = x.shape
    oc_t, os_t, ow_t, gates_t, ob_t = _mixers(
        x, positions, norm1_gain, w_in, nsa_q_gain, nsa_kc_gain, nsa_ks_gain, nsa_kw_gain,
        cmp_pos, cmp_k_w1, cmp_k_w2, cmp_v_w1, cmp_v_w2,
        mla_q_lora_gain, mla_w_uq, mla_kv_lora_gain, mla_w_ukv, mla_q_gain, mla_k_gain)
    h_t, hn_t = _out_proj(oc_t, os_t, ow_t, gates_t, ob_t, x.reshape(batch * seq, d), _col(out_gain_a), _col(out_gain_b),
                          w_out.T.astype(BF16), _col(norm2_gain), TOKEN_TILE)
    return _peer(hn_t, h_t, peer_w_q, peer_sub_keys, peer_u, peer_v).reshape(batch, seq, d)


def kernel(x, positions, norm1_gain, w_in, nsa_q_gain, nsa_kc_gain, nsa_ks_gain, nsa_kw_gain, cmp_pos, cmp_k_w1, cmp_k_w2, cmp_v_w1, cmp_v_w2, mla_q_lora_gain, mla_w_uq, mla_kv_lora_gain, mla_w_ukv, mla_q_gain, mla_k_gain, out_gain_a, out_gain_b, w_out, norm2_gain, peer_w_q, peer_sub_keys, peer_u, peer_v):
    h = x
    for l in range(norm1_gain.shape[0]):
        h = _layer(h, positions, norm1_gain[l], w_in[l], nsa_q_gain[l], nsa_kc_gain[l], nsa_ks_gain[l], nsa_kw_gain[l],
                   cmp_pos[l], cmp_k_w1[l], cmp_k_w2[l], cmp_v_w1[l], cmp_v_w2[l],
                   mla_q_lora_gain[l], mla_w_uq[l], mla_kv_lora_gain[l], mla_w_ukv[l], mla_q_gain[l], mla_k_gain[l],
                   out_gain_a[l], out_gain_b[l], w_out[l], norm2_gain[l], peer_w_q[l], peer_sub_keys[l],
                   peer_u[l], peer_v[l])
    return h
```

```python
import functools

import jax
import jax.numpy as jnp
import numpy as np
from jax import lax
from jax.experimental import pallas as pl
from jax.experimental.pallas import tpu as pltpu

F32, BF16 = jnp.float32, jnp.bfloat16
EPS = 1e-6
NEG = -1e30
FORCE = 1e9
ROPE_THETA = 500000.0
LOG2E = 1.4426950408889634

D_MODEL = 1024
A_HEADS, A_GROUPS, A_DIM = 8, 2, 64
A_REP = A_HEADS // A_GROUPS
A_ROPE = A_DIM // 4
CMP_LEN, CMP_STRIDE, CMP_HIDDEN = 32, 16, 256
SLC_LEN, SLC_TOPK, WINDOW = 64, 16, 512
B_HEADS, Q_LORA, KV_LORA, B_NOPE, B_ROPE, B_V = 8, 256, 128, 64, 32, 64
B_QK = B_NOPE + B_ROPE
P_HEADS, N_KEYS, P_KEY_DIM, P_TOPK = 8, 128, 256, 16
N_EXPERTS = N_KEYS * N_KEYS

ROW_Q, ROW_KC, ROW_VC, ROW_KS, ROW_VS, ROW_KW, ROW_VW = 0, 512, 640, 768, 896, 1024, 1152
ROW_CQ, ROW_CKV, ROW_KR, ROW_GATE, PROJ_ROWS = 1280, 1536, 1664, 1696, 1728
GATE_ROWS = 32

VMEM_LIMIT = 56 * 1024 * 1024
NT_DIMS = (((1,), (1,)), ((), ()))


def _params(n_axes):
    return pltpu.CompilerParams(dimension_semantics=("arbitrary",) * n_axes, vmem_limit_bytes=VMEM_LIMIT)


def _dot(a, b):
    return jnp.dot(a, b, preferred_element_type=F32)


def _dot_nt(a, b):
    return lax.dot_general(a, b, NT_DIMS, preferred_element_type=F32)


def _row_sumsq(x):
    sq = x * x
    hi = sq.astype(BF16)
    lo = (sq - hi.astype(F32)).astype(BF16)
    ones = jnp.ones((8, x.shape[1]), BF16)
    return (_dot_nt(ones, hi) + _dot_nt(ones, lo))[0:1, :]


def _rms_rows(x):
    ss = jnp.sum(x * x, axis=0, keepdims=True)
    return x * lax.rsqrt(ss * (1.0 / x.shape[0]) + EPS)


def _rope_rows(y, cos, sin, off, half):
    x1, x2 = y[off:off + half], y[off + half:off + 2 * half]
    parts = [y[:off]] if off else []
    parts += [x1 * cos - x2 * sin, x2 * cos + x1 * sin]
    if off + 2 * half < y.shape[0]:
        parts.append(y[off + 2 * half:])
    return jnp.concatenate(parts, axis=0)


def _gelu_tanh(x):
    c = 0.7978845608028654
    half = 0.5 * x
    return half + half * jnp.tanh(x * (c + (c * 0.044715) * (x * x)))


def _in_proj_kernel(x_ref, g_ref, w_ref, o_ref):
    x = x_ref[...]
    xg = (x * g_ref[...]).astype(BF16)
    p = _dot_nt(w_ref[...], xg)
    rinv = lax.rsqrt(_row_sumsq(x) * (1.0 / x.shape[1]) + EPS)
    o_ref[...] = p * rinv


def _in_proj(x2, gain, w_t, tt):
    t, d = x2.shape
    rows = w_t.shape[0]
    return pl.pallas_call(
        _in_proj_kernel,
        grid=(t // tt,),
        in_specs=[pl.BlockSpec((tt, d), lambda i: (i, 0)),
                  pl.BlockSpec((1, d), lambda i: (0, 0)),
                  pl.BlockSpec((rows, d), lambda i: (0, 0))],
        out_specs=pl.BlockSpec((rows, tt), lambda i: (0, i)),
        out_shape=jax.ShapeDtypeStruct((rows, t), F32),
        compiler_params=_params(1),
        name="in_proj",
    )(x2, gain, w_t)


def _nsa_prep_kernel(q_ref, kc_ref, vc_ref, ks_ref, vs_ref, kw_ref, vw_ref, gt_ref, cos_ref, sin_ref,
                     qg_ref, ksg_ref, kwg_ref,
                     qo_ref, kco_ref, vco_ref, kso_ref, vso_ref, kwo_ref, vwo_ref, gto_ref, *, seq):
    cos, sin = cos_ref[...], sin_ref[...]
    tt = cos.shape[1]
    nb = seq // SLC_LEN
    for h in range(A_HEADS):
        y = _rms_rows(q_ref[h * A_DIM:(h + 1) * A_DIM, :]) * qg_ref[...]
        y = _rope_rows(y, cos, sin, 0, A_ROPE // 2) * (A_DIM ** -0.5 * LOG2E)
        qo_ref[h * A_DIM:(h + 1) * A_DIM, :] = y.astype(BF16)
    kco_ref[...] = kc_ref[...].T
    vco_ref[...] = vc_ref[...].T
    zeros = jnp.zeros((A_DIM, tt), F32)
    tok = pl.program_id(0) * tt + lax.broadcasted_iota(jnp.int32, (tt, nb), 0)
    block_hot = jnp.where(lax.broadcasted_iota(jnp.int32, (tt, nb), 1) == (tok % seq) // SLC_LEN, 1.0, 0.0)
    for g in range(A_GROUPS):
        for src, gain, dst in ((ks_ref, ksg_ref, kso_ref), (kw_ref, kwg_ref, kwo_ref)):
            y = _rms_rows(src[g * A_DIM:(g + 1) * A_DIM, :]) * gain[...]
            y = _rope_rows(y, cos, sin, 0, A_ROPE // 2)
            k_tm = jnp.concatenate([y, zeros], axis=0).T
            if dst is kso_ref:
                k_tm = jnp.concatenate([k_tm, block_hot], axis=1)
            dst[g] = k_tm.astype(BF16)
        vso_ref[g, 0] = _with_ones(vs_ref[g * A_DIM:(g + 1) * A_DIM, :])
        vwo_ref[g] = _with_ones(vw_ref[g * A_DIM:(g + 1) * A_DIM, :])
    gto_ref[...] = 1.0 / (1.0 + jnp.exp(-gt_ref[...]))


def _nsa_prep(proj_t, cos_t, sin_t, q_gain, ks_gain, kw_gain, tt, seq):
    t = proj_t.shape[1]
    gw = A_GROUPS * A_DIM
    kw = 128 + seq // SLC_LEN
    row_spec = lambda rows, start: pl.BlockSpec((rows, tt), lambda i: (start // rows, i))
    col = lambda n: pl.BlockSpec((n, 1), lambda i: (0, 0))
    return pl.pallas_call(
        functools.partial(_nsa_prep_kernel, seq=seq),
        grid=(t // tt,),
        in_specs=[row_spec(512, ROW_Q), row_spec(gw, ROW_KC), row_spec(gw, ROW_VC), row_spec(gw, ROW_KS),
                  row_spec(gw, ROW_VS), row_spec(gw, ROW_KW), row_spec(gw, ROW_VW), row_spec(GATE_ROWS, ROW_GATE),
                  pl.BlockSpec((A_ROPE // 2, tt), lambda i: (0, i)), pl.BlockSpec((A_ROPE // 2, tt), lambda i: (0, i)),
                  col(A_DIM), col(A_DIM), col(A_DIM)],
        out_specs=[pl.BlockSpec((512, tt), lambda i: (0, i)),
                   pl.BlockSpec((tt, gw), lambda i: (i, 0)),
                   pl.BlockSpec((tt, gw), lambda i: (i, 0)),
                   pl.BlockSpec((A_GROUPS, tt, kw), lambda i: (0, i, 0)),
                   pl.BlockSpec((A_GROUPS, 1, A_DIM + ONES_ROWS, tt), lambda i: (0, i, 0, 0)),
                   pl.BlockSpec((A_GROUPS, tt, 128), lambda i: (0, i, 0)),
                   pl.BlockSpec((A_GROUPS, A_DIM + ONES_ROWS, tt), lambda i: (0, 0, i)),
                   pl.BlockSpec((GATE_ROWS, tt), lambda i: (0, i))],
        out_shape=[jax.ShapeDtypeStruct((512, t), BF16),
                   jax.ShapeDtypeStruct((t, gw), F32),
                   jax.ShapeDtypeStruct((t, gw), F32),
                   jax.ShapeDtypeStruct((A_GROUPS, t, kw), BF16),
                   jax.ShapeDtypeStruct((A_GROUPS, t // tt, A_DIM + ONES_ROWS, tt), BF16),
                   jax.ShapeDtypeStruct((A_GROUPS, t, 128), BF16),
                   jax.ShapeDtypeStruct((A_GROUPS, A_DIM + ONES_ROWS, t), BF16),
                   jax.ShapeDtypeStruct((GATE_ROWS, t), F32)],
        compiler_params=_params(1),
        name="nsa_prep",
    )(proj_t, proj_t, proj_t, proj_t, proj_t, proj_t, proj_t, proj_t, cos_t, sin_t, q_gain, ks_gain, kw_gain)


def _compress_kernel(kc_ref, vc_ref, w1ka_ref, w1kb_ref, w1va_ref, w1vb_ref, plo_ref, phi_ref,
                     w2k_ref, w2v_ref, kg_ref, cos_ref, sin_ref, ko_ref, vo_ref):
    nc = kc_ref.shape[1]
    zeros = jnp.zeros((A_DIM, nc), F32)
    for src, w1a, w1b, w2, is_k in ((kc_ref, w1ka_ref, w1kb_ref, w2k_ref, True),
                                    (vc_ref, w1va_ref, w1vb_ref, w2v_ref, False)):
        x = src[0]
        xlo = (x + plo_ref[...]).astype(BF16)
        xhi = (x + phi_ref[...]).astype(BF16)
        for g in range(A_GROUPS):
            first = _dot_nt(w1a[g], xlo)
            second = _dot_nt(w1b[g], xhi)
            hid = _gelu_tanh(first + pltpu.roll(second, nc - 1, axis=1)).astype(BF16)
            c = _dot(w2[...], hid)
            if is_k:
                y = _rope_rows(_rms_rows(c) * kg_ref[...], cos_ref[0], sin_ref[0], 0, A_ROPE // 2)
                ko_ref[0, g] = jnp.concatenate([y, zeros], axis=0).T.astype(BF16)
            else:
                vo_ref[0, g] = c.astype(BF16)


def _compress(kc_chunks, vc_chunks, w1ka, w1kb, w1va, w1vb, plo, phi, w2k_t, w2v_t, kc_gain, cos_c, sin_c):
    b, nc, cw = kc_chunks.shape
    full = lambda a: pl.BlockSpec(a.shape, lambda i: (0,) * a.ndim)
    return pl.pallas_call(
        _compress_kernel,
        grid=(b,),
        in_specs=[pl.BlockSpec((1, nc, cw), lambda i: (i, 0, 0)), pl.BlockSpec((1, nc, cw), lambda i: (i, 0, 0)),
                  full(w1ka), full(w1kb), full(w1va), full(w1vb), full(plo), full(phi), full(w2k_t), full(w2v_t),
                  full(kc_gain),
                  pl.BlockSpec((1, A_ROPE // 2, nc), lambda i: (i, 0, 0)),
                  pl.BlockSpec((1, A_ROPE // 2, nc), lambda i: (i, 0, 0))],
        out_specs=[pl.BlockSpec((1, A_GROUPS, nc, 128), lambda i: (i, 0, 0, 0)),
                   pl.BlockSpec((1, A_GROUPS, A_DIM, nc), lambda i: (i, 0, 0, 0))],
        out_shape=[jax.ShapeDtypeStruct((b, A_GROUPS, nc, 128), BF16),
                   jax.ShapeDtypeStruct((b, A_GROUPS, A_DIM, nc), BF16)],
        compiler_params=_params(1),
        name="nsa_compress",
    )(kc_chunks, vc_chunks, w1ka, w1kb, w1va, w1vb, plo, phi, w2k_t, w2v_t, kc_gain, cos_c, sin_c)


def _stack_heads(q, tq):
    qs = jnp.concatenate([q[r * A_DIM:(r + 1) * A_DIM, :] for r in range(A_REP)], axis=1)
    return jnp.concatenate([qs, jnp.zeros_like(qs)], axis=0)


def _nsa_cmp_kernel(q_ref, k_ref, v_ref, ov_ref, o_ref, sb_ref, *, tq):
    i = pl.program_id(2)
    n = A_REP * tq
    nc = k_ref.shape[2]
    nb = ov_ref.shape[0]
    qp = _stack_heads(q_ref[...], tq)
    s = _dot(k_ref[0, 0], qp)
    cmp_end = lax.broadcasted_iota(jnp.int32, (nc, n), 0) * CMP_STRIDE + (CMP_LEN - 1)
    tok = i * tq + (lax.broadcasted_iota(jnp.int32, (nc, n), 1) & (tq - 1))
    mask = cmp_end <= tok
    s = jnp.where(mask, s, NEG)
    m = jnp.max(s, axis=0, keepdims=True)
    p = jnp.where(mask, jnp.exp2(s - m), 0.0)
    l = jnp.sum(p, axis=0, keepdims=True)
    inv = jnp.where(l > 0.0, 1.0 / l, 0.0)
    pn = (p * inv).astype(BF16)
    oc = _dot(v_ref[0, 0], pn)
    for r in range(A_REP):
        o_ref[r * A_DIM:(r + 1) * A_DIM, :] = oc[:, r * tq:(r + 1) * tq]
    imp4 = _dot(ov_ref[...], pn)
    imp = imp4[:, 0:tq]
    for r in range(1, A_REP):
        imp = imp + imp4[:, r * tq:(r + 1) * tq]

    blk = lax.broadcasted_iota(jnp.int32, (nb, tq), 0)
    t = i * tq + lax.broadcasted_iota(jnp.int32, (nb, tq), 1)
    forced = (blk == (t >> 6)) | (blk == 0)
    v = jnp.where(forced, FORCE, jnp.where(blk * SLC_LEN <= t, imp, NEG))
    blk_f = blk.astype(F32)
    sel = jnp.zeros((nb, tq), F32)
    for _ in range(min(SLC_TOPK, nb)):
        mx = jnp.max(v, axis=0, keepdims=True)
        first = jnp.min(jnp.where(v == mx, blk_f, float(nb)), axis=0, keepdims=True)
        hit = blk_f == first
        sel = jnp.where(hit, 1.0, sel)
        v = jnp.where(hit, -jnp.inf, v)
    sb_ref[0] = jnp.where(sel > 0.0, 0.0, NEG)


def _nsa_cmp(q_t, kcmp, vcmp_t, ov_t, batch, seq, tq):
    nq = seq // tq
    nc = kcmp.shape[2]
    nb = ov_t.shape[0]
    t = q_t.shape[1]
    gr = A_REP * A_DIM
    return pl.pallas_call(
        functools.partial(_nsa_cmp_kernel, tq=tq),
        grid=(batch, A_GROUPS, nq),
        in_specs=[pl.BlockSpec((gr, tq), lambda b, g, i: (g, b * nq + i)),
                  pl.BlockSpec((1, 1, nc, 128), lambda b, g, i: (b, g, 0, 0)),
                  pl.BlockSpec((1, 1, A_DIM, nc), lambda b, g, i: (b, g, 0, 0)),
                  pl.BlockSpec((nb, nc), lambda b, g, i: (0, 0))],
        out_specs=[pl.BlockSpec((gr, tq), lambda b, g, i: (g, b * nq + i)),
                   pl.BlockSpec((1, nb, tq), lambda b, g, i: (g, 0, b * nq + i))],
        out_shape=[jax.ShapeDtypeStruct((A_HEADS * A_DIM, t), F32),
                   jax.ShapeDtypeStruct((A_GROUPS, nb, t), F32)],
        compiler_params=_params(3),
        name="nsa_cmp",
    )(q_t, kcmp, vcmp_t, ov_t)


ONES_ROWS = 16


def _with_ones(v):
    return jnp.concatenate([v, jnp.ones((ONES_ROWS, v.shape[1]), F32)], axis=0).astype(BF16)


def _flash_init(m_scr, acc_scr):
    m_scr[...] = jnp.full(m_scr.shape, -jnp.inf, F32)
    acc_scr[...] = jnp.zeros(acc_scr.shape, F32)


def _flash_update(s, v_t, m_scr, acc_scr):
    m_prev = m_scr[...]
    m_new = jnp.maximum(m_prev, jnp.max(s, axis=0, keepdims=True))
    alpha = jnp.exp2(m_prev - m_new)
    p = jnp.exp2(s - m_new)
    acc_scr[...] = alpha * acc_scr[...] + _dot(v_t, p.astype(BF16))
    m_scr[...] = m_new


def _flash_result(acc_scr, dv):
    acc = acc_scr[...]
    return acc[:dv] * (1.0 / acc[dv:dv + 1])


def _flash_causal(scores, values, mask_last, n_full, sa_scr, sb_scr, m_scr, acc_scr):
    sa_scr[...] = scores(0)

    def pair(jj, carry):
        j = 2 * jj
        sb_scr[...] = scores(j + 1)
        _flash_update(sa_scr[...], values(j), m_scr, acc_scr)
        sa_scr[...] = scores(j + 2)
        _flash_update(sb_scr[...], values(j + 1), m_scr, acc_scr)
        return carry

    lax.fori_loop(0, n_full // 2, pair, 0)

    @pl.when(n_full % 2 == 0)
    def _():
        _flash_update(mask_last(sa_scr[...]), values(n_full), m_scr, acc_scr)

    @pl.when(n_full % 2 == 1)
    def _():
        sb_scr[...] = scores(n_full)
        _flash_update(sa_scr[...], values(n_full - 1), m_scr, acc_scr)
        _flash_update(mask_last(sb_scr[...]), values(n_full), m_scr, acc_scr)


def _nsa_sel_kernel(q_ref, k_ref, v_ref, sb_ref, o_ref, qa_scr, sa_scr, sb_scr, m_scr, acc_scr, *, tq, tk):
    i = pl.program_id(2)
    n = A_REP * tq
    q = q_ref[...]
    qs = jnp.concatenate([q[r * A_DIM:(r + 1) * A_DIM, :] for r in range(A_REP)], axis=1)
    sb = sb_ref[0].astype(BF16)
    qa_scr[...] = jnp.concatenate([qs, jnp.zeros_like(qs), jnp.concatenate([sb] * A_REP, axis=1)], axis=0)
    _flash_init(m_scr, acc_scr)
    j_diag = (i * tq + tq - 1) // tk

    def scores(j):
        return _dot(k_ref[0, pl.ds(pl.multiple_of(j * tk, tk), tk), :], qa_scr[...])

    def causal(s):
        kpos = j_diag * tk + lax.broadcasted_iota(jnp.int32, (tk, n), 0)
        tok = i * tq + (lax.broadcasted_iota(jnp.int32, (tk, n), 1) & (tq - 1))
        return jnp.where(kpos <= tok, s, NEG)

    _flash_causal(scores, lambda j: v_ref[0, j], causal, j_diag, sa_scr, sb_scr, m_scr, acc_scr)
    o = _flash_result(acc_scr, A_DIM)
    for r in range(A_REP):
        o_ref[r * A_DIM:(r + 1) * A_DIM, :] = o[:, r * tq:(r + 1) * tq]


def _nsa_sel(q_t, k_aug, v_slc_tiles, selb, batch, seq, tq, tk):
    nq, nk = seq // tq, seq // tk
    nb = selb.shape[1]
    t = q_t.shape[1]
    gr = A_REP * A_DIM
    n = A_REP * tq
    kw = k_aug.shape[2]
    return pl.pallas_call(
        functools.partial(_nsa_sel_kernel, tq=tq, tk=tk),
        grid=(batch, A_GROUPS, nq),
        in_specs=[pl.BlockSpec((gr, tq), lambda b, g, i: (g, b * nq + i)),
                  pl.BlockSpec((1, seq, kw), lambda b, g, i: (g, b, 0)),
                  pl.BlockSpec((1, nk, A_DIM + ONES_ROWS, tk), lambda b, g, i: (g, b, 0, 0)),
                  pl.BlockSpec((1, nb, tq), lambda b, g, i: (g, 0, b * nq + i))],
        out_specs=pl.BlockSpec((gr, tq), lambda b, g, i: (g, b * nq + i)),
        out_shape=jax.ShapeDtypeStruct((A_HEADS * A_DIM, t), F32),
        scratch_shapes=[pltpu.VMEM((kw, n), BF16), pltpu.VMEM((tk, n), F32), pltpu.VMEM((tk, n), F32),
                        pltpu.VMEM((1, n), F32), pltpu.VMEM((A_DIM + ONES_ROWS, n), F32)],
        compiler_params=_params(3),
        name="nsa_sel",
    )(q_t, k_aug, v_slc_tiles, selb)


def _nsa_win_kernel(q_ref, k_ref, v_ref, o_ref, qp_scr, m_scr, acc_scr, *, tq):
    i, c = pl.program_id(2), pl.program_id(3)
    n_back = WINDOW // tq
    n = A_REP * tq
    kt = i - n_back + c

    @pl.when(c == 0)
    def _():
        qp_scr[...] = _stack_heads(q_ref[...], tq)
        _flash_init(m_scr, acc_scr)

    @pl.when(kt >= 0)
    def _():
        s = _dot(k_ref[0], qp_scr[...])
        kpos = kt * tq + lax.broadcasted_iota(jnp.int32, (tq, n), 0)
        tok = i * tq + (lax.broadcasted_iota(jnp.int32, (tq, n), 1) & (tq - 1))
        s = jnp.where((kpos <= tok) & (kpos > tok - WINDOW), s, NEG)
        _flash_update(s, v_ref[0], m_scr, acc_scr)

    @pl.when(c == n_back)
    def _():
        o = _flash_result(acc_scr, A_DIM)
        for r in range(A_REP):
            o_ref[r * A_DIM:(r + 1) * A_DIM, :] = o[:, r * tq:(r + 1) * tq]


def _nsa_win(q_t, k_win, v_win_t, batch, seq, tq):
    nq = seq // tq
    n_back = WINDOW // tq
    t = q_t.shape[1]
    gr = A_REP * A_DIM
    n = A_REP * tq
    kidx = lambda b, i, c: b * nq + jnp.maximum(i - n_back + c, 0)
    return pl.pallas_call(
        functools.partial(_nsa_win_kernel, tq=tq),
        grid=(batch, A_GROUPS, nq, n_back + 1),
        in_specs=[pl.BlockSpec((gr, tq), lambda b, g, i, c: (g, b * nq + i)),
                  pl.BlockSpec((1, tq, 128), lambda b, g, i, c: (g, kidx(b, i, c), 0)),
                  pl.BlockSpec((1, A_DIM + ONES_ROWS, tq), lambda b, g, i, c: (g, 0, kidx(b, i, c)))],
        out_specs=pl.BlockSpec((gr, tq), lambda b, g, i, c: (g, b * nq + i)),
        out_shape=jax.ShapeDtypeStruct((A_HEADS * A_DIM, t), F32),
        scratch_shapes=[pltpu.VMEM((128, n), BF16), pltpu.VMEM((1, n), F32),
                        pltpu.VMEM((A_DIM + ONES_ROWS, n), F32)],
        compiler_params=_params(4),
        name="nsa_win",
    )(q_t, k_win, v_win_t)


def _mla_prep_kernel(cq_ref, ckv_ref, kr_ref, cos_ref, sin_ref, qlg_ref, kvlg_ref, qg_ref, kg_ref, wuq_ref, wukv_ref,
                     qo_ref, ko_ref, vo_ref):
    cos, sin = cos_ref[...], sin_ref[...]
    tt = cos.shape[1]
    q_all = _dot(wuq_ref[...], (_rms_rows(cq_ref[...]) * qlg_ref[...]).astype(BF16))
    kv_all = _dot(wukv_ref[...], (_rms_rows(ckv_ref[...]) * kvlg_ref[...]).astype(BF16))
    kr = kr_ref[...]
    pad = jnp.zeros((128 - B_QK, tt), F32)
    for h in range(B_HEADS):
        y = _rms_rows(q_all[h * B_QK:(h + 1) * B_QK]) * qg_ref[...]
        y = _rope_rows(y, cos, sin, B_NOPE, B_ROPE // 2) * (B_QK ** -0.5 * LOG2E)
        qo_ref[h] = jnp.concatenate([y, pad], axis=0).astype(BF16)
        base = h * (B_NOPE + B_V)
        k = jnp.concatenate([kv_all[base:base + B_NOPE], kr], axis=0)
        y = _rope_rows(_rms_rows(k) * kg_ref[...], cos, sin, B_NOPE, B_ROPE // 2)
        ko_ref[h] = jnp.concatenate([y, pad], axis=0).T.astype(BF16)
        vo_ref[h, 0] = _with_ones(kv_all[base + B_NOPE:base + B_NOPE + B_V])


def _mla_prep(proj_t, cos_t, sin_t, q_lora_gain, kv_lora_gain, q_gain, k_gain, wuq_t, wukv_t, tt):
    t = proj_t.shape[1]
    row_spec = lambda rows, start: pl.BlockSpec((rows, tt), lambda i: (start // rows, i))
    full = lambda a: pl.BlockSpec(a.shape, lambda i: (0,) * a.ndim)
    return pl.pallas_call(
        _mla_prep_kernel,
        grid=(t // tt,),
        in_specs=[row_spec(Q_LORA, ROW_CQ), row_spec(KV_LORA, ROW_CKV), row_spec(B_ROPE, ROW_KR),
                  pl.BlockSpec((B_ROPE // 2, tt), lambda i: (0, i)), pl.BlockSpec((B_ROPE // 2, tt), lambda i: (0, i)),
                  full(q_lora_gain), full(kv_lora_gain), full(q_gain), full(k_gain), full(wuq_t), full(wukv_t)],
        out_specs=[pl.BlockSpec((B_HEADS, 128, tt), lambda i: (0, 0, i)),
                   pl.BlockSpec((B_HEADS, tt, 128), lambda i: (0, i, 0)),
                   pl.BlockSpec((B_HEADS, 1, B_V + ONES_ROWS, tt), lambda i: (0, i, 0, 0))],
        out_shape=[jax.ShapeDtypeStruct((B_HEADS, 128, t), BF16),
                   jax.ShapeDtypeStruct((B_HEADS, t, 128), BF16),
                   jax.ShapeDtypeStruct((B_HEADS, t // tt, B_V + ONES_ROWS, tt), BF16)],
        compiler_params=_params(1),
        name="mla_prep",
    )(proj_t, proj_t, proj_t, cos_t, sin_t, q_lora_gain, kv_lora_gain, q_gain, k_gain, wuq_t, wukv_t)


def _mla_attn_kernel(q_ref, k_ref, v_ref, o_ref, sa_scr, sb_scr, m_scr, acc_scr, *, tq, tk):
    i = pl.program_id(2)
    _flash_init(m_scr, acc_scr)
    j_diag = (i * tq + tq - 1) // tk

    def scores(j):
        return _dot(k_ref[0, pl.ds(pl.multiple_of(j * tk, tk), tk), :], q_ref[0])

    def causal(s):
        kpos = j_diag * tk + lax.broadcasted_iota(jnp.int32, (tk, tq), 0)
        tok = i * tq + lax.broadcasted_iota(jnp.int32, (tk, tq), 1)
        return jnp.where(kpos <= tok, s, NEG)

    _flash_causal(scores, lambda j: v_ref[0, j], causal, j_diag, sa_scr, sb_scr, m_scr, acc_scr)
    o_ref[...] = _flash_result(acc_scr, B_V)


def _mla_attn(q_m, k_m, v_m_tiles, batch, seq, tq, tk):
    nq, nk = seq // tq, seq // tk
    t = q_m.shape[2]
    return pl.pallas_call(
        functools.partial(_mla_attn_kernel, tq=tq, tk=tk),
        grid=(batch, B_HEADS, nq),
        in_specs=[pl.BlockSpec((1, 128, tq), lambda b, h, i: (h, 0, b * nq + i)),
                  pl.BlockSpec((1, seq, 128), lambda b, h, i: (h, b, 0)),
                  pl.BlockSpec((1, nk, B_V + ONES_ROWS, tk), lambda b, h, i: (h, b, 0, 0))],
        out_specs=pl.BlockSpec((B_V, tq), lambda b, h, i: (h, b * nq + i)),
        out_shape=jax.ShapeDtypeStruct((B_HEADS * B_V, t), F32),
        scratch_shapes=[pltpu.VMEM((tk, tq), F32), pltpu.VMEM((tk, tq), F32),
                        pltpu.VMEM((1, tq), F32), pltpu.VMEM((B_V + ONES_ROWS, tq), F32)],
        compiler_params=_params(3),
        name="mla_attn",
    )(q_m, k_m, v_m_tiles)


def _out_proj_kernel(oc_ref, os_ref, ow_ref, gt_ref, ob_ref, x_ref, ga_ref, gb_ref, w_ref, g2_ref, h_ref, hn_ref):
    heads = []
    for h in range(A_HEADS):
        rows = slice(h * A_DIM, (h + 1) * A_DIM)
        heads.append(gt_ref[3 * h:3 * h + 1, :] * oc_ref[rows, :] + gt_ref[3 * h + 1:3 * h + 2, :] * os_ref[rows, :]
                     + gt_ref[3 * h + 2:3 * h + 3, :] * ow_ref[rows, :])
    oa = _rms_rows(jnp.concatenate(heads, axis=0)) * ga_ref[...]
    ob = _rms_rows(ob_ref[...]) * gb_ref[...]
    cat = jnp.concatenate([oa, ob], axis=0).astype(BF16)
    hid = x_ref[...].T + _dot(w_ref[...], cat)
    h_ref[...] = hid
    hn_ref[...] = (_rms_rows(hid) * g2_ref[...]).astype(BF16)


def _out_proj(oc_t, os_t, ow_t, gates_t, ob_t, x2, gain_a, gain_b, w_out_t, gain2, tt):
    t, d = x2.shape
    aw = oc_t.shape[0]
    bw = ob_t.shape[0]
    tok = lambda rows: pl.BlockSpec((rows, tt), lambda i: (0, i))
    full = lambda a: pl.BlockSpec(a.shape, lambda i: (0,) * a.ndim)
    return pl.pallas_call(
        _out_proj_kernel,
        grid=(t // tt,),
        in_specs=[tok(aw), tok(aw), tok(aw), tok(GATE_ROWS), tok(bw), pl.BlockSpec((tt, d), lambda i: (i, 0)),
                  full(gain_a), full(gain_b), full(w_out_t), full(gain2)],
        out_specs=[tok(d), tok(d)],
        out_shape=[jax.ShapeDtypeStruct((d, t), F32), jax.ShapeDtypeStruct((d, t), BF16)],
        compiler_params=_params(1),
        name="out_proj",
    )(oc_t, os_t, ow_t, gates_t, ob_t, x2, gain_a, gain_b, w_out_t, gain2)


def _top_ranked(s):
    n, tp = s.shape
    row = lax.broadcasted_iota(jnp.int32, (n, tp), 0).astype(F32)
    slot = lax.broadcasted_iota(jnp.int32, (P_TOPK, tp), 0)

    def body(a, carry):
        v, rank, vals = carry
        mx = jnp.max(v, axis=0, keepdims=True)
        first = jnp.min(jnp.where(v == mx, row, float(n)), axis=0, keepdims=True)
        hit = row == first
        rank = jnp.where(hit, jnp.asarray(a, F32), rank)
        v = jnp.where(hit, -jnp.inf, v)
        vals = jnp.where(slot == a, mx, vals)
        return v, rank, vals

    _, rank, vals = lax.fori_loop(0, P_TOPK, body,
                                  (s, jnp.full((n, tp), float(P_TOPK), F32), jnp.zeros((P_TOPK, tp), F32)))
    return rank, vals


def _pair_counts(v1, v2):
    k = v1.shape[0]
    slot = lax.broadcasted_iota(jnp.int32, v1.shape, 0).astype(F32)
    top = v1[0:1] + v2[0:1]

    def body(_, carry):
        count, front, z = carry
        mx = jnp.max(front, axis=0, keepdims=True)
        a_star = jnp.min(jnp.where(front == mx, slot, float(k)), axis=0, keepdims=True)
        hit = slot == a_star
        count = count + jnp.where(hit, 1.0, 0.0)
        nxt = jnp.sum(jnp.where(hit, count, 0.0), axis=0, keepdims=True)
        v2_nxt = jnp.sum(jnp.where(slot == nxt, v2, 0.0), axis=0, keepdims=True)
        front = jnp.where(hit, jnp.where(nxt < float(k), v1 + v2_nxt, -jnp.inf), front)
        return count, front, z + jnp.exp(mx - top)

    count, _, z = lax.fori_loop(0, k, body, (jnp.zeros(v1.shape, F32), v1 + v2[0:1], jnp.zeros(top.shape, F32)))
    return count, z


REMOVED = -2.0 ** 126
LANES = 128


def _top_ranked_pair_fast(s1, s2):
    n, tp = s1.shape
    slot = lax.broadcasted_iota(jnp.int32, (P_TOPK, tp), 0)

    def body(a, carry):
        code = REMOVED * (1.0 + jnp.asarray(a, F32) * (1.0 / 32.0))
        out = []
        for key, vals in (carry[0:2], carry[2:4]):
            mx = jnp.max(key, axis=0, keepdims=True)
            key = jnp.where(key == mx, code, key)
            out += [key, jnp.where(slot == a, mx, vals)]
        return tuple(out)

    zeros = jnp.zeros((P_TOPK, tp), F32)
    k1, t1, k2, t2 = lax.fori_loop(0, P_TOPK, body, (s1, zeros, s2, zeros))
    res, bad = [], jnp.zeros((1, tp), F32)
    for key, vals in ((k1, t1), (k2, t2)):
        removed = key <= REMOVED
        rank = jnp.where(removed, (key * (1.0 / REMOVED) - 1.0) * 32.0, float(P_TOPK))
        n_removed = jnp.sum(jnp.where(removed, 1.0, 0.0), axis=0, keepdims=True)
        bad = jnp.maximum(bad, jnp.abs(n_removed - float(P_TOPK)))
        res += [rank, vals]
    return res[0], res[1], res[2], res[3], bad


def _pack_bf16_twice(x):
    hi = pltpu.bitcast(x.astype(BF16).astype(F32), jnp.int32)
    return hi | lax.shift_right_logical(hi, 16)


def _peer_route_kernel(hn_ref, wq_ref, keys_ref, cut_ref, g1_ref, r2_ref, g2_ref, q_scr, s_scr, rank_scr, vals_scr):
    tp = hn_ref.shape[1]
    half = P_KEY_DIM // 2
    q_scr[...] = _dot(wq_ref[...], hn_ref[...])
    for h in range(P_HEADS):
        q = _rms_rows(q_scr[h * P_KEY_DIM:(h + 1) * P_KEY_DIM, :]).astype(BF16)
        s_scr[0, h] = _dot(keys_ref[h, 0], q[:half])
        s_scr[1, h] = _dot(keys_ref[h, 1], q[half:])

    def first_level(rank_pair, flagged):
        for h in range(P_HEADS):
            for lt in range(tp // LANES):
                lanes = slice(lt * LANES, (lt + 1) * LANES)
                rank1, vals1, rank2, vals2, bad = rank_pair(s_scr[0, h, :, lanes], s_scr[1, h, :, lanes])
                flagged = jnp.maximum(flagged, bad)
                rank_scr[0, h, :, lanes] = rank1
                rank_scr[1, h, :, lanes] = rank2
                for a in range(P_TOPK):
                    vals_scr[0, a, h:h + 1, lanes] = vals1[a:a + 1]
                    vals_scr[1, a, h:h + 1, lanes] = vals2[a:a + 1]
        return flagged

    flagged = first_level(_top_ranked_pair_fast, jnp.zeros((1, LANES), F32))

    @pl.when(jnp.max(flagged) > 0.0)
    def _():
        first_level(lambda s1, s2: (*_top_ranked(s1), *_top_ranked(s2), jnp.zeros((1, LANES), F32)), flagged)

    count, z = _pair_counts(vals_scr[0], vals_scr[1])
    for h in range(P_HEADS):
        rank1 = rank_scr[0, h]
        cut = jnp.zeros_like(rank1)
        for a in range(P_TOPK):
            cut = jnp.where(rank1 == float(a), count[a, h:h + 1, :], cut)
        cut_ref[h] = _pack_bf16_twice(cut)
        g1_ref[h] = _pack_bf16_twice(jnp.exp(s_scr[0, h] - vals_scr[0, 0, h:h + 1, :]))
        r2_ref[h] = rank_scr[1, h].astype(BF16)
        g2_ref[h] = (jnp.exp(s_scr[1, h] - vals_scr[1, 0, h:h + 1, :]) * (1.0 / z[0, h:h + 1, :])).astype(BF16)


def _peer_route(hn_t, wq_t, sub_keys, tp):
    d, t = hn_t.shape
    words = jax.ShapeDtypeStruct((P_HEADS, N_KEYS, t), jnp.int32)
    halfs = jax.ShapeDtypeStruct((P_HEADS, N_KEYS, t), BF16)
    ospec = pl.BlockSpec((P_HEADS, N_KEYS, tp), lambda i: (0, 0, i))
    return pl.pallas_call(
        _peer_route_kernel,
        grid=(t // tp,),
        in_specs=[pl.BlockSpec((d, tp), lambda i: (0, i)),
                  pl.BlockSpec(wq_t.shape, lambda i: (0, 0)),
                  pl.BlockSpec(sub_keys.shape, lambda i: (0, 0, 0, 0))],
        out_specs=[ospec, ospec, ospec, ospec],
        out_shape=[words, words, halfs, halfs],
        scratch_shapes=[pltpu.VMEM((P_HEADS * P_KEY_DIM, tp), F32), pltpu.VMEM((2, P_HEADS, N_KEYS, tp), F32),
                        pltpu.VMEM((2, P_HEADS, N_KEYS, tp), F32), pltpu.VMEM((2, P_TOPK, P_HEADS, tp), F32)],
        compiler_params=_params(1),
        name="peer_route",
    )(hn_t, wq_t, sub_keys)


def _peer_ffn_kernel(hn_ref, h_ref, u_ref, v_ref, cut_ref, g1_ref, r2_ref, g2_ref, o_ref, acc_scr, *, te):
    e = pl.program_id(1)

    @pl.when(e == 0)
    def _():
        acc_scr[...] = jnp.zeros(acc_scr.shape, F32)

    hn = hn_ref[...]
    tt = hn.shape[1]

    def rows_bf16(word_row):
        tile = pltpu.bitcast(jnp.broadcast_to(word_row, (8, tt)), BF16)
        return jnp.concatenate([tile] * (N_KEYS // 16), axis=0)

    chunk = 2 * N_KEYS
    weights = []
    for c in range(te // chunk):
        a = _dot(u_ref[c * chunk:(c + 1) * chunk, :], hn)
        for k in range(chunk // N_KEYS):
            ii = c * (chunk // N_KEYS) + k
            gate = jnp.zeros((N_KEYS, tt), BF16)
            for h in range(P_HEADS):
                chosen = r2_ref[h] < rows_bf16(cut_ref[h, ii:ii + 1, :])
                gate = gate + jnp.where(chosen, g2_ref[h], jnp.zeros_like(gate)) * rows_bf16(g1_ref[h, ii:ii + 1, :])
            weights.append(gate * _gelu_tanh(a[k * N_KEYS:(k + 1) * N_KEYS, :]).astype(BF16))
    acc_scr[...] += _dot(v_ref[...], jnp.concatenate(weights, axis=0))

    @pl.when(e == pl.num_programs(1) - 1)
    def _():
        o_ref[...] = (h_ref[...] + acc_scr[...]).T


def _peer_ffn(hn_t, h_t, u_bf, v_t_bf, cut_k, g1_k, r2, g2, tt, te):
    d, t = hn_t.shape
    n_exp = u_bf.shape[0]
    kpe = te // N_KEYS
    return pl.pallas_call(
        functools.partial(_peer_ffn_kernel, te=te),
        grid=(t // tt, n_exp // te),
        in_specs=[pl.BlockSpec((d, tt), lambda i, e: (0, i)),
                  pl.BlockSpec((d, tt), lambda i, e: (0, i)),
                  pl.BlockSpec((te, d), lambda i, e: (e, 0)),
                  pl.BlockSpec((d, te), lambda i, e: (0, e)),
                  pl.BlockSpec((P_HEADS, kpe, tt), lambda i, e: (0, e, i)),
                  pl.BlockSpec((P_HEADS, kpe, tt), lambda i, e: (0, e, i)),
                  pl.BlockSpec((P_HEADS, N_KEYS, tt), lambda i, e: (0, 0, i)),
                  pl.BlockSpec((P_HEADS, N_KEYS, tt), lambda i, e: (0, 0, i))],
        out_specs=pl.BlockSpec((tt, d), lambda i, e: (i, 0)),
        out_shape=jax.ShapeDtypeStruct((t, d), F32),
        scratch_shapes=[pltpu.VMEM((d, tt), F32)],
        compiler_params=_params(2),
        name="peer_ffn",
    )(hn_t, h_t, u_bf, v_t_bf, cut_k, g1_k, r2, g2)


def _rope_tables_t(pos_flat, rot_dim):
    inv_freq = ROPE_THETA ** (-jnp.arange(0, rot_dim, 2, dtype=F32) / rot_dim)
    ang = pos_flat.astype(F32)[None, :] * inv_freq[:, None]
    return jnp.cos(ang), jnp.sin(ang)


def _expand_cmp_w1(w1):
    w = w1.reshape(CMP_LEN, A_DIM, CMP_HIDDEN)
    out = []
    for part in (w[:CMP_STRIDE], w[CMP_STRIDE:]):
        z = jnp.zeros_like(part)
        both = jnp.stack([jnp.concatenate([part, z], axis=1), jnp.concatenate([z, part], axis=1)])
        out.append(both.reshape(A_GROUPS, CMP_STRIDE * A_GROUPS * A_DIM, CMP_HIDDEN).transpose(0, 2, 1).astype(BF16))
    return out


TOKEN_TILE = 512
NSA_Q_TILE = 256
NSA_SEL_K_TILE = 512
MLA_TILE = 512
PEER_ROUTE_TILE = 256
PEER_EXPERT_TILE = 1024


def _col(v):
    return v.reshape(-1, 1).astype(F32)


def _mixers(x, positions, norm1_gain, w_in, nsa_q_gain, nsa_kc_gain, nsa_ks_gain, nsa_kw_gain,
            cmp_pos, cmp_k_w1, cmp_k_w2, cmp_v_w1, cmp_v_w2,
            mla_q_lora_gain, mla_w_uq, mla_kv_lora_gain, mla_w_ukv, mla_q_gain, mla_k_gain):
    batch, seq, d = x.shape
    t = batch * seq
    tt = TOKEN_TILE
    tq_nsa = NSA_Q_TILE
    tk_sel = NSA_SEL_K_TILE
    tq_mla = tk_mla = MLA_TILE
    assert tk_sel == tt and tk_mla == tt
    assert d == D_MODEL and seq % 512 == 0 and seq // SLC_LEN >= SLC_TOPK and WINDOW % tq_nsa == 0
    col = _col
    x2 = x.reshape(t, d)
    pos = positions.reshape(t)

    w_in_t = w_in.T
    gate_lo = sum((512, 128, 128, 128, 128, 128, 128))
    gate_hi = gate_lo + 3 * A_HEADS
    w_in_t = jnp.concatenate([w_in_t[:gate_lo], w_in_t[gate_hi:], w_in_t[gate_lo:gate_hi],
                              jnp.zeros((PROJ_ROWS - w_in_t.shape[0], d), F32)], axis=0).astype(BF16)
    proj_t = _in_proj(x2, norm1_gain.reshape(1, d), w_in_t, tt)

    cos_a, sin_a = _rope_tables_t(pos, A_ROPE)
    q_t, kc_tm, vc_tm, k_slc, v_slc_t, k_win, v_win_t, gates_t = _nsa_prep(
        proj_t, cos_a, sin_a, col(nsa_q_gain), col(nsa_ks_gain), col(nsa_kw_gain), tt, seq)

    nc = seq // CMP_STRIDE
    chunk_w = CMP_STRIDE * A_GROUPS * A_DIM
    w1ka, w1kb = _expand_cmp_w1(cmp_k_w1)
    w1va, w1vb = _expand_cmp_w1(cmp_v_w1)
    pos_rows = lambda p: jnp.broadcast_to(p[:, None, :], (CMP_STRIDE, A_GROUPS, A_DIM)).reshape(1, chunk_w)
    cmp_end = jnp.minimum(jnp.arange(nc) * CMP_STRIDE + CMP_LEN - 1, seq - 1)
    cos_c, sin_c = _rope_tables_t(positions[:, cmp_end].reshape(-1), A_ROPE)
    to_b = lambda a: a.reshape(A_ROPE // 2, batch, nc).transpose(1, 0, 2)
    kcmp, vcmp_t = _compress(kc_tm.reshape(batch, nc, chunk_w), vc_tm.reshape(batch, nc, chunk_w),
                             w1ka, w1kb, w1va, w1vb, pos_rows(cmp_pos[:CMP_STRIDE]), pos_rows(cmp_pos[CMP_STRIDE:]),
                             cmp_k_w2.T.astype(BF16), cmp_v_w2.T.astype(BF16), col(nsa_kc_gain), to_b(cos_c), to_b(sin_c))

    n_cmp = (seq - CMP_LEN) // CMP_STRIDE + 1
    nb = seq // SLC_LEN
    c_start = np.arange(nc)[None, :] * CMP_STRIDE
    s_start = np.arange(nb)[:, None] * SLC_LEN
    ov = (c_start < s_start + SLC_LEN) & (c_start + CMP_LEN - 1 >= s_start) & (np.arange(nc)[None, :] < n_cmp)
    ov_t = jnp.asarray(ov.astype(np.float32)).astype(BF16)

    oc_t, selb = _nsa_cmp(q_t, kcmp, vcmp_t, ov_t, batch, seq, tq_nsa)
    os_t = _nsa_sel(q_t, k_slc, v_slc_t, selb, batch, seq, tq_nsa, tk_sel)
    ow_t = _nsa_win(q_t, k_win, v_win_t, batch, seq, tq_nsa)

    cos_b, sin_b = _rope_tables_t(pos, B_ROPE)
    q_m, k_m, v_m_t = _mla_prep(proj_t, cos_b, sin_b, col(mla_q_lora_gain), col(mla_kv_lora_gain),
                                col(mla_q_gain), col(mla_k_gain), mla_w_uq.T.astype(BF16), mla_w_ukv.T.astype(BF16), tt)
    ob_t = _mla_attn(q_m, k_m, v_m_t, batch, seq, tq_mla, tk_mla)
    return oc_t, os_t, ow_t, gates_t, ob_t


def _peer(hn_t, h_t, peer_w_q, peer_sub_keys, peer_u, peer_v):
    cut, g1, r2, g2 = _peer_route(hn_t, peer_w_q.T.astype(BF16), peer_sub_keys.astype(BF16), PEER_ROUTE_TILE)
    return _peer_ffn(hn_t, h_t, peer_u.astype(BF16), peer_v.T.astype(BF16),
                     cut, g1, r2, g2, TOKEN_TILE, PEER_EXPERT_TILE)


def _layer(x, positions, norm1_gain, w_in, nsa_q_gain, nsa_kc_gain, nsa_ks_gain, nsa_kw_gain,
           cmp_pos, cmp_k_w1, cmp_k_w2, cmp_v_w1, cmp_v_w2,
           mla_q_lora_gain, mla_w_uq, mla_kv_lora_gain, mla_w_ukv, mla_q_gain, mla_k_gain,
           out_gain_a, out_gain_b, w_out, norm2_gain, peer_w_q, peer_sub_keys, peer_u, peer_v):
    batch, seq, d = x.shape
    oc_t, os_t, ow_t, gates_t, ob_t = _mixers(
        x, positions, norm1_gain, w_in, nsa_q_gain, nsa_kc_gain, nsa_ks_gain, nsa_kw_gain,
        cmp_pos, cmp_k_w1, cmp_k_w2, cmp_v_w1, cmp_v_w2,
        mla_q_lora_gain, mla_w_uq, mla_kv_lora_gain, mla_w_ukv, mla_q_gain, mla_k_gain)
    h_t, hn_t = _out_proj(oc_t, os_t, ow_t, gates_t, ob_t, x.reshape(batch * seq, d), _col(out_gain_a), _col(out_gain_b),
                          w_out.T.astype(BF16), _col(norm2_gain), TOKEN_TILE)
    return _peer(hn_t, h_t, peer_w_q, peer_sub_keys, peer_u, peer_v).reshape(batch, seq, d)


def kernel(x, positions, norm1_gain, w_in, nsa_q_gain, nsa_kc_gain, nsa_ks_gain, nsa_kw_gain, cmp_pos, cmp_k_w1, cmp_k_w2, cmp_v_w1, cmp_v_w2, mla_q_lora_gain, mla_w_uq, mla_kv_lora_gain, mla_w_ukv, mla_q_gain, mla_k_gain, out_gain_a, out_gain_b, w_out, norm2_gain, peer_w_q, peer_sub_keys, peer_u, peer_v):
    h = x
    for l in range(norm1_gain.shape[0]):
        h = _layer(h, positions, norm1_gain[l], w_in[l], nsa_q_gain[l], nsa_kc_gain[l], nsa_ks_gain[l], nsa_kw_gain[l],
                   cmp_pos[l], cmp_k_w1[l], cmp_k_w2[l], cmp_v_w1[l], cmp_v_w2[l],
                   mla_q_lora_gain[l], mla_w_uq[l], mla_kv_lora_gain[l], mla_w_ukv[l], mla_q_gain[l], mla_k_gain[l],
                   out_gain_a[l], out_gain_b[l], w_out[l], norm2_gain[l], peer_w_q[l], peer_sub_keys[l],
                   peer_u[l], peer_v[l])
    return h
```

```python
import functools

import jax
import jax.numpy as jnp
import numpy as np
from jax import lax
from jax.experimental import pallas as pl
from jax.experimental.pallas import tpu as pltpu

F32, BF16 = jnp.float32, jnp.bfloat16
EPS = 1e-6
NEG = -1e30
FORCE = 1e9
ROPE_THETA = 500000.0
LOG2E = 1.4426950408889634

D_MODEL = 1024
A_HEADS, A_GROUPS, A_DIM = 8, 2, 64
A_REP = A_HEADS // A_GROUPS
A_ROPE = A_DIM // 4
CMP_LEN, CMP_STRIDE, CMP_HIDDEN = 32, 16, 256
SLC_LEN, SLC_TOPK, WINDOW = 64, 16, 512
B_HEADS, Q_LORA, KV_LORA, B_NOPE, B_ROPE, B_V = 8, 256, 128, 64, 32, 64
B_QK = B_NOPE + B_ROPE
P_HEADS, N_KEYS, P_KEY_DIM, P_TOPK = 8, 128, 256, 16
N_EXPERTS = N_KEYS * N_KEYS

ROW_Q, ROW_KC, ROW_VC, ROW_KS, ROW_VS, ROW_KW, ROW_VW = 0, 512, 640, 768, 896, 1024, 1152
ROW_CQ, ROW_CKV, ROW_KR, ROW_GATE, PROJ_ROWS = 1280, 1536, 1664, 1696, 1728
GATE_ROWS = 32

VMEM_LIMIT = 56 * 1024 * 1024
NT_DIMS = (((1,), (1,)), ((), ()))


def _params(n_axes):
    return pltpu.CompilerParams(dimension_semantics=("arbitrary",) * n_axes, vmem_limit_bytes=VMEM_LIMIT)


def _dot(a, b):
    return jnp.dot(a, b, preferred_element_type=F32)


def _dot_nt(a, b):
    return lax.dot_general(a, b, NT_DIMS, preferred_element_type=F32)


def _row_sumsq(x):
    sq = x * x
    hi = sq.astype(BF16)
    lo = (sq - hi.astype(F32)).astype(BF16)
    ones = jnp.ones((8, x.shape[1]), BF16)
    return (_dot_nt(ones, hi) + _dot_nt(ones, lo))[0:1, :]


def _rms_rows(x):
    ss = jnp.sum(x * x, axis=0, keepdims=True)
    return x * lax.rsqrt(ss * (1.0 / x.shape[0]) + EPS)


def _rope_rows(y, cos, sin, off, half):
    x1, x2 = y[off:off + half], y[off + half:off + 2 * half]
    parts = [y[:off]] if off else []
    parts += [x1 * cos - x2 * sin, x2 * cos + x1 * sin]
    if off + 2 * half < y.shape[0]:
        parts.append(y[off + 2 * half:])
    return jnp.concatenate(parts, axis=0)


def _gelu_tanh(x):
    c = 0.7978845608028654
    half = 0.5 * x
    return half + half * jnp.tanh(x * (c + (c * 0.044715) * (x * x)))


def _in_proj_kernel(x_ref, g_ref, w_ref, o_ref):
    x = x_ref[...]
    xg = (x * g_ref[...]).astype(BF16)
    p = _dot_nt(w_ref[...], xg)
    rinv = lax.rsqrt(_row_sumsq(x) * (1.0 / x.shape[1]) + EPS)
    o_ref[...] = p * rinv


def _in_proj(x2, gain, w_t, tt):
    t, d = x2.shape
    rows = w_t.shape[0]
    return pl.pallas_call(
        _in_proj_kernel,
        grid=(t // tt,),
        in_specs=[pl.BlockSpec((tt, d), lambda i: (i, 0)),
                  pl.BlockSpec((1, d), lambda i: (0, 0)),
                  pl.BlockSpec((rows, d), lambda i: (0, 0))],
        out_specs=pl.BlockSpec((rows, tt), lambda i: (0, i)),
        out_shape=jax.ShapeDtypeStruct((rows, t), F32),
        compiler_params=_params(1),
        name="in_proj",
    )(x2, gain, w_t)


def _nsa_prep_kernel(q_ref, kc_ref, vc_ref, ks_ref, vs_ref, kw_ref, vw_ref, gt_ref, cos_ref, sin_ref,
                     qg_ref, ksg_ref, kwg_ref,
                     qo_ref, kco_ref, vco_ref, kso_ref, vso_ref, kwo_ref, vwo_ref, gto_ref, *, seq, tw):
    cos, sin = cos_ref[...], sin_ref[...]
    tt = cos.shape[1]
    nb = seq // SLC_LEN
    for h in range(A_HEADS):
        y = _rms_rows(q_ref[h * A_DIM:(h + 1) * A_DIM, :]) * qg_ref[...]
        y = _rope_rows(y, cos, sin, 0, A_ROPE // 2) * (A_DIM ** -0.5 * LOG2E)
        qo_ref[h * A_DIM:(h + 1) * A_DIM, :] = y.astype(BF16)
    kco_ref[...] = kc_ref[...].T
    vco_ref[...] = vc_ref[...].T
    zeros = jnp.zeros((A_DIM, tt), F32)
    tok = pl.program_id(0) * tt + lax.broadcasted_iota(jnp.int32, (tt, nb), 0)
    block_hot = jnp.where(lax.broadcasted_iota(jnp.int32, (tt, nb), 1) == (tok % seq) // SLC_LEN, 1.0, 0.0)
    for g in range(A_GROUPS):
        for src, gain, dst in ((ks_ref, ksg_ref, kso_ref), (kw_ref, kwg_ref, kwo_ref)):
            y = _rms_rows(src[g * A_DIM:(g + 1) * A_DIM, :]) * gain[...]
            y = _rope_rows(y, cos, sin, 0, A_ROPE // 2)
            k_tm = jnp.concatenate([y, zeros], axis=0).T
            if dst is kso_ref:
                k_tm = jnp.concatenate([k_tm, block_hot], axis=1)
            dst[g] = k_tm.astype(BF16)
        vso_ref[g, 0] = _with_ones(vs_ref[g * A_DIM:(g + 1) * A_DIM, :])
        v_win = _with_ones(vw_ref[g * A_DIM:(g + 1) * A_DIM, :])
        for c in range(tt // tw):
            vwo_ref[g, c] = v_win[:, c * tw:(c + 1) * tw]
    gto_ref[...] = 1.0 / (1.0 + jnp.exp(-gt_ref[...]))


def _nsa_prep(proj_t, cos_t, sin_t, q_gain, ks_gain, kw_gain, tt, seq, tw):
    t = proj_t.shape[1]
    gw = A_GROUPS * A_DIM
    kw = 128 + seq // SLC_LEN
    row_spec = lambda rows, start: pl.BlockSpec((rows, tt), lambda i: (start // rows, i))
    col = lambda n: pl.BlockSpec((n, 1), lambda i: (0, 0))
    return pl.pallas_call(
        functools.partial(_nsa_prep_kernel, seq=seq, tw=tw),
        grid=(t // tt,),
        in_specs=[row_spec(512, ROW_Q), row_spec(gw, ROW_KC), row_spec(gw, ROW_VC), row_spec(gw, ROW_KS),
                  row_spec(gw, ROW_VS), row_spec(gw, ROW_KW), row_spec(gw, ROW_VW), row_spec(GATE_ROWS, ROW_GATE),
                  pl.BlockSpec((A_ROPE // 2, tt), lambda i: (0, i)), pl.BlockSpec((A_ROPE // 2, tt), lambda i: (0, i)),
                  col(A_DIM), col(A_DIM), col(A_DIM)],
        out_specs=[pl.BlockSpec((512, tt), lambda i: (0, i)),
                   pl.BlockSpec((tt, gw), lambda i: (i, 0)),
                   pl.BlockSpec((tt, gw), lambda i: (i, 0)),
                   pl.BlockSpec((A_GROUPS, tt, kw), lambda i: (0, i, 0)),
                   pl.BlockSpec((A_GROUPS, 1, A_DIM + ONES_ROWS, tt), lambda i: (0, i, 0, 0)),
                   pl.BlockSpec((A_GROUPS, tt, 128), lambda i: (0, i, 0)),
                   pl.BlockSpec((A_GROUPS, tt // tw, A_DIM + ONES_ROWS, tw), lambda i: (0, i, 0, 0)),
                   pl.BlockSpec((GATE_ROWS, tt), lambda i: (0, i))],
        out_shape=[jax.ShapeDtypeStruct((512, t), BF16),
                   jax.ShapeDtypeStruct((t, gw), F32),
                   jax.ShapeDtypeStruct((t, gw), F32),
                   jax.ShapeDtypeStruct((A_GROUPS, t, kw), BF16),
                   jax.ShapeDtypeStruct((A_GROUPS, t // tt, A_DIM + ONES_ROWS, tt), BF16),
                   jax.ShapeDtypeStruct((A_GROUPS, t, 128), BF16),
                   jax.ShapeDtypeStruct((A_GROUPS, t // tw, A_DIM + ONES_ROWS, tw), BF16),
                   jax.ShapeDtypeStruct((GATE_ROWS, t), F32)],
        compiler_params=_params(1),
        name="nsa_prep",
    )(proj_t, proj_t, proj_t, proj_t, proj_t, proj_t, proj_t, proj_t, cos_t, sin_t, q_gain, ks_gain, kw_gain)


def _compress_kernel(kc_ref, vc_ref, w1ka_ref, w1kb_ref, w1va_ref, w1vb_ref, plo_ref, phi_ref,
                     w2k_ref, w2v_ref, kg_ref, cos_ref, sin_ref, ko_ref, vo_ref):
    nc = kc_ref.shape[1]
    zeros = jnp.zeros((A_DIM, nc), F32)
    for src, w1a, w1b, w2, is_k in ((kc_ref, w1ka_ref, w1kb_ref, w2k_ref, True),
                                    (vc_ref, w1va_ref, w1vb_ref, w2v_ref, False)):
        x = src[0]
        xlo = (x + plo_ref[...]).astype(BF16)
        xhi = (x + phi_ref[...]).astype(BF16)
        for g in range(A_GROUPS):
            first = _dot_nt(w1a[g], xlo)
            second = _dot_nt(w1b[g], xhi)
            hid = _gelu_tanh(first + pltpu.roll(second, nc - 1, axis=1)).astype(BF16)
            c = _dot(w2[...], hid)
            if is_k:
                y = _rope_rows(_rms_rows(c) * kg_ref[...], cos_ref[0], sin_ref[0], 0, A_ROPE // 2)
                ko_ref[0, g] = jnp.concatenate([y, zeros], axis=0).T.astype(BF16)
            else:
                vo_ref[0, g] = c.astype(BF16)


def _compress(kc_chunks, vc_chunks, w1ka, w1kb, w1va, w1vb, plo, phi, w2k_t, w2v_t, kc_gain, cos_c, sin_c):
    b, nc, cw = kc_chunks.shape
    full = lambda a: pl.BlockSpec(a.shape, lambda i: (0,) * a.ndim)
    return pl.pallas_call(
        _compress_kernel,
        grid=(b,),
        in_specs=[pl.BlockSpec((1, nc, cw), lambda i: (i, 0, 0)), pl.BlockSpec((1, nc, cw), lambda i: (i, 0, 0)),
                  full(w1ka), full(w1kb), full(w1va), full(w1vb), full(plo), full(phi), full(w2k_t), full(w2v_t),
                  full(kc_gain),
                  pl.BlockSpec((1, A_ROPE // 2, nc), lambda i: (i, 0, 0)),
                  pl.BlockSpec((1, A_ROPE // 2, nc), lambda i: (i, 0, 0))],
        out_specs=[pl.BlockSpec((1, A_GROUPS, nc, 128), lambda i: (i, 0, 0, 0)),
                   pl.BlockSpec((1, A_GROUPS, A_DIM, nc), lambda i: (i, 0, 0, 0))],
        out_shape=[jax.ShapeDtypeStruct((b, A_GROUPS, nc, 128), BF16),
                   jax.ShapeDtypeStruct((b, A_GROUPS, A_DIM, nc), BF16)],
        compiler_params=_params(1),
        name="nsa_compress",
    )(kc_chunks, vc_chunks, w1ka, w1kb, w1va, w1vb, plo, phi, w2k_t, w2v_t, kc_gain, cos_c, sin_c)


def _stack_heads(q, tq):
    qs = jnp.concatenate([q[r * A_DIM:(r + 1) * A_DIM, :] for r in range(A_REP)], axis=1)
    return jnp.concatenate([qs, jnp.zeros_like(qs)], axis=0)


def _nsa_cmp_kernel(q_ref, k_ref, v_ref, ov_ref, o_ref, sb_ref, *, tq):
    i = pl.program_id(2)
    n = A_REP * tq
    nc = k_ref.shape[2]
    nb = ov_ref.shape[0]
    qp = _stack_heads(q_ref[...], tq)
    s = _dot(k_ref[0, 0], qp)
    cmp_end = lax.broadcasted_iota(jnp.int32, (nc, n), 0) * CMP_STRIDE + (CMP_LEN - 1)
    tok = i * tq + (lax.broadcasted_iota(jnp.int32, (nc, n), 1) & (tq - 1))
    mask = cmp_end <= tok
    s = jnp.where(mask, s, NEG)
    m = jnp.max(s, axis=0, keepdims=True)
    p = jnp.where(mask, jnp.exp2(s - m), 0.0)
    l = jnp.sum(p, axis=0, keepdims=True)
    inv = jnp.where(l > 0.0, 1.0 / l, 0.0)
    pn = (p * inv).astype(BF16)
    oc = _dot(v_ref[0, 0], pn)
    for r in range(A_REP):
        o_ref[r * A_DIM:(r + 1) * A_DIM, :] = oc[:, r * tq:(r + 1) * tq]
    imp4 = _dot(ov_ref[...], pn)
    imp = imp4[:, 0:tq]
    for r in range(1, A_REP):
        imp = imp + imp4[:, r * tq:(r + 1) * tq]

    blk = lax.broadcasted_iota(jnp.int32, (nb, tq), 0)
    t = i * tq + lax.broadcasted_iota(jnp.int32, (nb, tq), 1)
    forced = (blk == (t >> 6)) | (blk == 0)
    v = jnp.where(forced, FORCE, jnp.where(blk * SLC_LEN <= t, imp, NEG))
    blk_f = blk.astype(F32)
    sel = jnp.zeros((nb, tq), F32)
    for _ in range(min(SLC_TOPK, nb)):
        mx = jnp.max(v, axis=0, keepdims=True)
        first = jnp.min(jnp.where(v == mx, blk_f, float(nb)), axis=0, keepdims=True)
        hit = blk_f == first
        sel = jnp.where(hit, 1.0, sel)
        v = jnp.where(hit, -jnp.inf, v)
    sb_ref[0] = jnp.where(sel > 0.0, 0.0, NEG)


def _nsa_cmp(q_t, kcmp, vcmp_t, ov_t, batch, seq, tq):
    nq = seq // tq
    nc = kcmp.shape[2]
    nb = ov_t.shape[0]
    t = q_t.shape[1]
    gr = A_REP * A_DIM
    return pl.pallas_call(
        functools.partial(_nsa_cmp_kernel, tq=tq),
        grid=(batch, A_GROUPS, nq),
        in_specs=[pl.BlockSpec((gr, tq), lambda b, g, i: (g, b * nq + i)),
                  pl.BlockSpec((1, 1, nc, 128), lambda b, g, i: (b, g, 0, 0)),
                  pl.BlockSpec((1, 1, A_DIM, nc), lambda b, g, i: (b, g, 0, 0)),
                  pl.BlockSpec((nb, nc), lambda b, g, i: (0, 0))],
        out_specs=[pl.BlockSpec((gr, tq), lambda b, g, i: (g, b * nq + i)),
                   pl.BlockSpec((1, nb, tq), lambda b, g, i: (g, 0, b * nq + i))],
        out_shape=[jax.ShapeDtypeStruct((A_HEADS * A_DIM, t), F32),
                   jax.ShapeDtypeStruct((A_GROUPS, nb, t), F32)],
        compiler_params=_params(3),
        name="nsa_cmp",
    )(q_t, kcmp, vcmp_t, ov_t)


ONES_ROWS = 16
FLASH_UNROLL = 4


def _with_ones(v):
    return jnp.concatenate([v, jnp.ones((ONES_ROWS, v.shape[1]), F32)], axis=0).astype(BF16)


def _flash_init(m_scr, acc_scr):
    m_scr[...] = jnp.full(m_scr.shape, -jnp.inf, F32)
    acc_scr[...] = jnp.zeros(acc_scr.shape, F32)


def _flash_update(s, v_t, m_scr, acc_scr):
    m_prev = m_scr[...]
    m_new = jnp.maximum(m_prev, jnp.max(s, axis=0, keepdims=True))
    alpha = jnp.exp2(m_prev - m_new)
    p = jnp.exp2(s - m_new)
    acc_scr[...] = alpha * acc_scr[...] + _dot(v_t, p.astype(BF16))
    m_scr[...] = m_new


def _flash_result(acc_scr, dv):
    acc = acc_scr[...]
    return acc[:dv] * (1.0 / acc[dv:dv + 1])


def _flash_causal(scores, values, mask_last, n_full, s_scr, m_scr, acc_scr):
    unroll = s_scr.shape[0]
    s_scr[0] = scores(0)

    def trip(t, carry):
        j = unroll * t
        for u in range(unroll):
            s_scr[(u + 1) % unroll] = scores(j + u + 1)
            _flash_update(s_scr[u], values(j + u), m_scr, acc_scr)
        return carry

    lax.fori_loop(0, n_full // unroll, trip, 0)
    first = (n_full // unroll) * unroll
    for rest in range(unroll):

        @pl.when(n_full - first == rest)
        def _(rest=rest):
            for u in range(rest + 1):
                if u < rest:
                    s_scr[u + 1] = scores(first + u + 1)
                    _flash_update(s_scr[u], values(first + u), m_scr, acc_scr)
                else:
                    _flash_update(mask_last(s_scr[u]), values(first + u), m_scr, acc_scr)


def _nsa_sel_kernel(q_ref, k_ref, v_ref, sb_ref, kw_ref, vw_ref, o_ref, ow_ref, qa_scr, s_scr, m_scr, acc_scr,
                    *, tq, tk):
    i = pl.program_id(2)
    n = A_REP * tq
    q = q_ref[...]
    qs = jnp.concatenate([q[r * A_DIM:(r + 1) * A_DIM, :] for r in range(A_REP)], axis=1)
    sb = sb_ref[0].astype(BF16)
    qa_scr[...] = jnp.concatenate([qs, jnp.zeros_like(qs), jnp.concatenate([sb] * A_REP, axis=1)], axis=0)
    _flash_init(m_scr, acc_scr)
    j_diag = (i * tq + tq - 1) // tk

    def scores(j):
        return _dot(k_ref[0, pl.ds(pl.multiple_of(j * tk, tk), tk), :], qa_scr[...])

    def causal(s):
        kpos = j_diag * tk + lax.broadcasted_iota(jnp.int32, (tk, n), 0)
        tok = i * tq + (lax.broadcasted_iota(jnp.int32, (tk, n), 1) & (tq - 1))
        return jnp.where(kpos <= tok, s, NEG)

    _flash_causal(scores, lambda j: v_ref[0, j], causal, j_diag, s_scr, m_scr, acc_scr)
    o = _flash_result(acc_scr, A_DIM)
    for r in range(A_REP):
        o_ref[r * A_DIM:(r + 1) * A_DIM, :] = o[:, r * tq:(r + 1) * tq]

    _flash_init(m_scr, acc_scr)
    n_back = WINDOW // tq
    for c in range(n_back + 1):
        kt = i - n_back + c

        @pl.when(kt >= 0)
        def _(kt=kt):
            s = _dot(kw_ref[0, pl.ds(pl.multiple_of(kt * tq, tq), tq), :], qa_scr[0:128, :])
            kpos = kt * tq + lax.broadcasted_iota(jnp.int32, (tq, n), 0)
            tok = i * tq + (lax.broadcasted_iota(jnp.int32, (tq, n), 1) & (tq - 1))
            s = jnp.where((kpos <= tok) & (kpos > tok - WINDOW), s, NEG)
            _flash_update(s, vw_ref[0, kt], m_scr, acc_scr)

    o = _flash_result(acc_scr, A_DIM)
    for r in range(A_REP):
        ow_ref[r * A_DIM:(r + 1) * A_DIM, :] = o[:, r * tq:(r + 1) * tq]


def _nsa_sel(q_t, k_aug, v_slc_tiles, selb, k_win, v_win_tiles, batch, seq, tq, tk):
    nq, nk = seq // tq, seq // tk
    nb = selb.shape[1]
    t = q_t.shape[1]
    gr = A_REP * A_DIM
    n = A_REP * tq
    kw = k_aug.shape[2]
    out = jax.ShapeDtypeStruct((A_HEADS * A_DIM, t), F32)
    return pl.pallas_call(
        functools.partial(_nsa_sel_kernel, tq=tq, tk=tk),
        grid=(batch, A_GROUPS, nq),
        in_specs=[pl.BlockSpec((gr, tq), lambda b, g, i: (g, b * nq + i)),
                  pl.BlockSpec((1, seq, kw), lambda b, g, i: (g, b, 0)),
                  pl.BlockSpec((1, nk, A_DIM + ONES_ROWS, tk), lambda b, g, i: (g, b, 0, 0)),
                  pl.BlockSpec((1, nb, tq), lambda b, g, i: (g, 0, b * nq + i)),
                  pl.BlockSpec((1, seq, 128), lambda b, g, i: (g, b, 0)),
                  pl.BlockSpec((1, nq, A_DIM + ONES_ROWS, tq), lambda b, g, i: (g, b, 0, 0))],
        out_specs=[pl.BlockSpec((gr, tq), lambda b, g, i: (g, b * nq + i)),
                   pl.BlockSpec((gr, tq), lambda b, g, i: (g, b * nq + i))],
        out_shape=[out, out],
        scratch_shapes=[pltpu.VMEM((kw, n), BF16), pltpu.VMEM((FLASH_UNROLL, tk, n), F32),
                        pltpu.VMEM((1, n), F32), pltpu.VMEM((A_DIM + ONES_ROWS, n), F32)],
        compiler_params=_params(3),
        name="nsa_sel",
    )(q_t, k_aug, v_slc_tiles, selb, k_win, v_win_tiles)


def _mla_prep_kernel(cq_ref, ckv_ref, kr_ref, cos_ref, sin_ref, qlg_ref, kvlg_ref, qg_ref, kg_ref, wuq_ref, wukv_ref,
                     qo_ref, ko_ref, vo_ref):
    cos, sin = cos_ref[...], sin_ref[...]
    tt = cos.shape[1]
    q_all = _dot(wuq_ref[...], (_rms_rows(cq_ref[...]) * qlg_ref[...]).astype(BF16))
    kv_all = _dot(wukv_ref[...], (_rms_rows(ckv_ref[...]) * kvlg_ref[...]).astype(BF16))
    kr = kr_ref[...]
    pad = jnp.zeros((128 - B_QK, tt), F32)
    for h in range(B_HEADS):
        y = _rms_rows(q_all[h * B_QK:(h + 1) * B_QK]) * qg_ref[...]
        y = _rope_rows(y, cos, sin, B_NOPE, B_ROPE // 2) * (B_QK ** -0.5 * LOG2E)
        qo_ref[h] = jnp.concatenate([y, pad], axis=0).astype(BF16)
        base = h * (B_NOPE + B_V)
        k = jnp.concatenate([kv_all[base:base + B_NOPE], kr], axis=0)
        y = _rope_rows(_rms_rows(k) * kg_ref[...], cos, sin, B_NOPE, B_ROPE // 2)
        ko_ref[h] = jnp.concatenate([y, pad], axis=0).T.astype(BF16)
        vo_ref[h, 0] = _with_ones(kv_all[base + B_NOPE:base + B_NOPE + B_V])


def _mla_prep(proj_t, cos_t, sin_t, q_lora_gain, kv_lora_gain, q_gain, k_gain, wuq_t, wukv_t, tt):
    t = proj_t.shape[1]
    row_spec = lambda rows, start: pl.BlockSpec((rows, tt), lambda i: (start // rows, i))
    full = lambda a: pl.BlockSpec(a.shape, lambda i: (0,) * a.ndim)
    return pl.pallas_call(
        _mla_prep_kernel,
        grid=(t // tt,),
        in_specs=[row_spec(Q_LORA, ROW_CQ), row_spec(KV_LORA, ROW_CKV), row_spec(B_ROPE, ROW_KR),
                  pl.BlockSpec((B_ROPE // 2, tt), lambda i: (0, i)), pl.BlockSpec((B_ROPE // 2, tt), lambda i: (0, i)),
                  full(q_lora_gain), full(kv_lora_gain), full(q_gain), full(k_gain), full(wuq_t), full(wukv_t)],
        out_specs=[pl.BlockSpec((B_HEADS, 128, tt), lambda i: (0, 0, i)),
                   pl.BlockSpec((B_HEADS, tt, 128), lambda i: (0, i, 0)),
                   pl.BlockSpec((B_HEADS, 1, B_V + ONES_ROWS, tt), lambda i: (0, i, 0, 0))],
        out_shape=[jax.ShapeDtypeStruct((B_HEADS, 128, t), BF16),
                   jax.ShapeDtypeStruct((B_HEADS, t, 128), BF16),
                   jax.ShapeDtypeStruct((B_HEADS, t // tt, B_V + ONES_ROWS, tt), BF16)],
        compiler_params=_params(1),
        name="mla_prep",
    )(proj_t, proj_t, proj_t, cos_t, sin_t, q_lora_gain, kv_lora_gain, q_gain, k_gain, wuq_t, wukv_t)


def _mla_attn_kernel(q_ref, k_ref, v_ref, o_ref, s_scr, m_scr, acc_scr, *, tq, tk):
    i = pl.program_id(2)
    _flash_init(m_scr, acc_scr)
    j_diag = (i * tq + tq - 1) // tk

    def scores(j):
        return _dot(k_ref[0, pl.ds(pl.multiple_of(j * tk, tk), tk), :], q_ref[0])

    def causal(s):
        kpos = j_diag * tk + lax.broadcasted_iota(jnp.int32, (tk, tq), 0)
        tok = i * tq + lax.broadcasted_iota(jnp.int32, (tk, tq), 1)
        return jnp.where(kpos <= tok, s, NEG)

    _flash_causal(scores, lambda j: v_ref[0, j], causal, j_diag, s_scr, m_scr, acc_scr)
    o_ref[...] = _flash_result(acc_scr, B_V)


def _mla_attn(q_m, k_m, v_m_tiles, batch, seq, tq, tk):
    nq, nk = seq // tq, seq // tk
    t = q_m.shape[2]
    return pl.pallas_call(
        functools.partial(_mla_attn_kernel, tq=tq, tk=tk),
        grid=(batch, B_HEADS, nq),
        in_specs=[pl.BlockSpec((1, 128, tq), lambda b, h, i: (h, 0, b * nq + i)),
                  pl.BlockSpec((1, seq, 128), lambda b, h, i: (h, b, 0)),
                  pl.BlockSpec((1, nk, B_V + ONES_ROWS, tk), lambda b, h, i: (h, b, 0, 0))],
        out_specs=pl.BlockSpec((B_V, tq), lambda b, h, i: (h, b * nq + i)),
        out_shape=jax.ShapeDtypeStruct((B_HEADS * B_V, t), F32),
        scratch_shapes=[pltpu.VMEM((FLASH_UNROLL, tk, tq), F32),
                        pltpu.VMEM((1, tq), F32), pltpu.VMEM((B_V + ONES_ROWS, tq), F32)],
        compiler_params=_params(3),
        name="mla_attn",
    )(q_m, k_m, v_m_tiles)


def _out_proj_kernel(oc_ref, os_ref, ow_ref, gt_ref, ob_ref, x_ref, ga_ref, gb_ref, w_ref, g2_ref, h_ref, hn_ref):
    heads = []
    for h in range(A_HEADS):
        rows = slice(h * A_DIM, (h + 1) * A_DIM)
        heads.append(gt_ref[3 * h:3 * h + 1, :] * oc_ref[rows, :] + gt_ref[3 * h + 1:3 * h + 2, :] * os_ref[rows, :]
                     + gt_ref[3 * h + 2:3 * h + 3, :] * ow_ref[rows, :])
    oa = _rms_rows(jnp.concatenate(heads, axis=0)) * ga_ref[...]
    ob = _rms_rows(ob_ref[...]) * gb_ref[...]
    cat = jnp.concatenate([oa, ob], axis=0).astype(BF16)
    hid = x_ref[...].T + _dot(w_ref[...], cat)
    h_ref[...] = hid
    hn_ref[...] = (_rms_rows(hid) * g2_ref[...]).astype(BF16)


def _out_proj(oc_t, os_t, ow_t, gates_t, ob_t, x2, gain_a, gain_b, w_out_t, gain2, tt):
    t, d = x2.shape
    aw = oc_t.shape[0]
    bw = ob_t.shape[0]
    tok = lambda rows: pl.BlockSpec((rows, tt), lambda i: (0, i))
    full = lambda a: pl.BlockSpec(a.shape, lambda i: (0,) * a.ndim)
    return pl.pallas_call(
        _out_proj_kernel,
        grid=(t // tt,),
        in_specs=[tok(aw), tok(aw), tok(aw), tok(GATE_ROWS), tok(bw), pl.BlockSpec((tt, d), lambda i: (i, 0)),
                  full(gain_a), full(gain_b), full(w_out_t), full(gain2)],
        out_specs=[tok(d), tok(d)],
        out_shape=[jax.ShapeDtypeStruct((d, t), F32), jax.ShapeDtypeStruct((d, t), BF16)],
        compiler_params=_params(1),
        name="out_proj",
    )(oc_t, os_t, ow_t, gates_t, ob_t, x2, gain_a, gain_b, w_out_t, gain2)


def _top_ranked(s):
    n, tp = s.shape
    row = lax.broadcasted_iota(jnp.int32, (n, tp), 0).astype(F32)
    slot = lax.broadcasted_iota(jnp.int32, (P_TOPK, tp), 0)

    def body(a, carry):
        v, rank, vals = carry
        mx = jnp.max(v, axis=0, keepdims=True)
        first = jnp.min(jnp.where(v == mx, row, float(n)), axis=0, keepdims=True)
        hit = row == first
        rank = jnp.where(hit, jnp.asarray(a, F32), rank)
        v = jnp.where(hit, -jnp.inf, v)
        vals = jnp.where(slot == a, mx, vals)
        return v, rank, vals

    _, rank, vals = lax.fori_loop(0, P_TOPK, body,
                                  (s, jnp.full((n, tp), float(P_TOPK), F32), jnp.zeros((P_TOPK, tp), F32)))
    return rank, vals


def _pair_counts(v1, v2):
    k = v1.shape[0]
    slot = lax.broadcasted_iota(jnp.int32, v1.shape, 0).astype(F32)
    top = v1[0:1] + v2[0:1]

    def body(_, carry):
        count, front, z = carry
        mx = jnp.max(front, axis=0, keepdims=True)
        a_star = jnp.min(jnp.where(front == mx, slot, float(k)), axis=0, keepdims=True)
        hit = slot == a_star
        count = count + jnp.where(hit, 1.0, 0.0)
        nxt = jnp.sum(jnp.where(hit, count, 0.0), axis=0, keepdims=True)
        v2_nxt = jnp.sum(jnp.where(slot == nxt, v2, 0.0), axis=0, keepdims=True)
        front = jnp.where(hit, jnp.where(nxt < float(k), v1 + v2_nxt, -jnp.inf), front)
        return count, front, z + jnp.exp(mx - top)

    count, _, z = lax.fori_loop(0, k, body, (jnp.zeros(v1.shape, F32), v1 + v2[0:1], jnp.zeros(top.shape, F32)))
    return count, z


REMOVED = -2.0 ** 126
LANES = 128


def _top_ranked_pair_fast(s1, s2):
    n, tp = s1.shape
    slot = lax.broadcasted_iota(jnp.int32, (P_TOPK, tp), 0)

    def body(a, carry):
        code = REMOVED * (1.0 + jnp.asarray(a, F32) * (1.0 / 32.0))
        out = []
        for key, vals in (carry[0:2], carry[2:4]):
            mx = jnp.max(key, axis=0, keepdims=True)
            key = jnp.where(key == mx, code, key)
            out += [key, jnp.where(slot == a, mx, vals)]
        return tuple(out)

    zeros = jnp.zeros((P_TOPK, tp), F32)
    k1, t1, k2, t2 = lax.fori_loop(0, P_TOPK, body, (s1, zeros, s2, zeros))
    res, bad = [], jnp.zeros((1, tp), F32)
    for key, vals in ((k1, t1), (k2, t2)):
        removed = key <= REMOVED
        rank = jnp.where(removed, (key * (1.0 / REMOVED) - 1.0) * 32.0, float(P_TOPK))
        n_removed = jnp.sum(jnp.where(removed, 1.0, 0.0), axis=0, keepdims=True)
        bad = jnp.maximum(bad, jnp.abs(n_removed - float(P_TOPK)))
        res += [rank, vals]
    return res[0], res[1], res[2], res[3], bad


def _pack_bf16_twice(x):
    hi = pltpu.bitcast(x.astype(BF16).astype(F32), jnp.int32)
    return hi | lax.shift_right_logical(hi, 16)


def _peer_route_kernel(hn_ref, wq_ref, keys_ref, cut_ref, g1_ref, r2_ref, g2_ref, q_scr, s_scr, rank_scr, vals_scr):
    tp = hn_ref.shape[1]
    half = P_KEY_DIM // 2
    q_scr[...] = _dot(wq_ref[...], hn_ref[...])
    for h in range(P_HEADS):
        q = _rms_rows(q_scr[h * P_KEY_DIM:(h + 1) * P_KEY_DIM, :]).astype(BF16)
        s_scr[0, h] = _dot(keys_ref[h, 0], q[:half])
        s_scr[1, h] = _dot(keys_ref[h, 1], q[half:])

    def first_level(rank_pair, flagged):
        for h in range(P_HEADS):
            for lt in range(tp // LANES):
                lanes = slice(lt * LANES, (lt + 1) * LANES)
                rank1, vals1, rank2, vals2, bad = rank_pair(s_scr[0, h, :, lanes], s_scr[1, h, :, lanes])
                flagged = jnp.maximum(flagged, bad)
                rank_scr[0, h, :, lanes] = rank1
                rank_scr[1, h, :, lanes] = rank2
                for a in range(P_TOPK):
                    vals_scr[0, a, h:h + 1, lanes] = vals1[a:a + 1]
                    vals_scr[1, a, h:h + 1, lanes] = vals2[a:a + 1]
        return flagged

    flagged = first_level(_top_ranked_pair_fast, jnp.zeros((1, LANES), F32))

    @pl.when(jnp.max(flagged) > 0.0)
    def _():
        first_level(lambda s1, s2: (*_top_ranked(s1), *_top_ranked(s2), jnp.zeros((1, LANES), F32)), flagged)

    count, z = _pair_counts(vals_scr[0], vals_scr[1])
    for h in range(P_HEADS):
        rank1 = rank_scr[0, h]
        cut = jnp.zeros_like(rank1)
        for a in range(P_TOPK):
            cut = jnp.where(rank1 == float(a), count[a, h:h + 1, :], cut)
        cut_ref[h] = _pack_bf16_twice(cut)
        g1_ref[h] = _pack_bf16_twice(jnp.exp(s_scr[0, h] - vals_scr[0, 0, h:h + 1, :]))
        r2_ref[h] = rank_scr[1, h].astype(BF16)
        g2_ref[h] = (jnp.exp(s_scr[1, h] - vals_scr[1, 0, h:h + 1, :]) * (1.0 / z[0, h:h + 1, :])).astype(BF16)


def _peer_route(hn_t, wq_t, sub_keys, tp):
    d, t = hn_t.shape
    words = jax.ShapeDtypeStruct((P_HEADS, N_KEYS, t), jnp.int32)
    halfs = jax.ShapeDtypeStruct((P_HEADS, N_KEYS, t), BF16)
    ospec = pl.BlockSpec((P_HEADS, N_KEYS, tp), lambda i: (0, 0, i))
    return pl.pallas_call(
        _peer_route_kernel,
        grid=(t // tp,),
        in_specs=[pl.BlockSpec((d, tp), lambda i: (0, i)),
                  pl.BlockSpec(wq_t.shape, lambda i: (0, 0)),
                  pl.BlockSpec(sub_keys.shape, lambda i: (0, 0, 0, 0))],
        out_specs=[ospec, ospec, ospec, ospec],
        out_shape=[words, words, halfs, halfs],
        scratch_shapes=[pltpu.VMEM((P_HEADS * P_KEY_DIM, tp), F32), pltpu.VMEM((2, P_HEADS, N_KEYS, tp), F32),
                        pltpu.VMEM((2, P_HEADS, N_KEYS, tp), F32), pltpu.VMEM((2, P_TOPK, P_HEADS, tp), F32)],
        compiler_params=_params(1),
        name="peer_route",
    )(hn_t, wq_t, sub_keys)


def _peer_ffn_kernel(hn_ref, h_ref, u_ref, v_ref, cut_ref, g1_ref, r2_ref, g2_ref, o_ref, acc_scr, *, te):
    e = pl.program_id(1)

    @pl.when(e == 0)
    def _():
        acc_scr[...] = jnp.zeros(acc_scr.shape, F32)

    hn = hn_ref[...]
    tt = hn.shape[1]

    def rows_bf16(word_row):
        tile = pltpu.bitcast(jnp.broadcast_to(word_row, (8, tt)), BF16)
        return jnp.concatenate([tile] * (N_KEYS // 16), axis=0)

    chunk = 2 * N_KEYS
    weights = []
    for c in range(te // chunk):
        a = _dot(u_ref[c * chunk:(c + 1) * chunk, :], hn)
        for k in range(chunk // N_KEYS):
            ii = c * (chunk // N_KEYS) + k
            gate = jnp.zeros((N_KEYS, tt), BF16)
            for h in range(P_HEADS):
                chosen = r2_ref[h] < rows_bf16(cut_ref[h, ii:ii + 1, :])
                gate = gate + jnp.where(chosen, g2_ref[h], jnp.zeros_like(gate)) * rows_bf16(g1_ref[h, ii:ii + 1, :])
            weights.append(gate * _gelu_tanh(a[k * N_KEYS:(k + 1) * N_KEYS, :]).astype(BF16))
    acc_scr[...] += _dot(v_ref[...], jnp.concatenate(weights, axis=0))

    @pl.when(e == pl.num_programs(1) - 1)
    def _():
        o_ref[...] = (h_ref[...] + acc_scr[...]).T


def _peer_ffn(hn_t, h_t, u_bf, v_t_bf, cut_k, g1_k, r2, g2, tt, te):
    d, t = hn_t.shape
    n_exp = u_bf.shape[0]
    kpe = te // N_KEYS
    return pl.pallas_call(
        functools.partial(_peer_ffn_kernel, te=te),
        grid=(t // tt, n_exp // te),
        in_specs=[pl.BlockSpec((d, tt), lambda i, e: (0, i)),
                  pl.BlockSpec((d, tt), lambda i, e: (0, i)),
                  pl.BlockSpec((te, d), lambda i, e: (e, 0)),
                  pl.BlockSpec((d, te), lambda i, e: (0, e)),
                  pl.BlockSpec((P_HEADS, kpe, tt), lambda i, e: (0, e, i)),
                  pl.BlockSpec((P_HEADS, kpe, tt), lambda i, e: (0, e, i)),
                  pl.BlockSpec((P_HEADS, N_KEYS, tt), lambda i, e: (0, 0, i)),
                  pl.BlockSpec((P_HEADS, N_KEYS, tt), lambda i, e: (0, 0, i))],
        out_specs=pl.BlockSpec((tt, d), lambda i, e: (i, 0)),
        out_shape=jax.ShapeDtypeStruct((t, d), F32),
        scratch_shapes=[pltpu.VMEM((d, tt), F32)],
        compiler_params=_params(2),
        name="peer_ffn",
    )(hn_t, h_t, u_bf, v_t_bf, cut_k, g1_k, r2, g2)


def _rope_tables_t(pos_flat, rot_dim):
    inv_freq = ROPE_THETA ** (-jnp.arange(0, rot_dim, 2, dtype=F32) / rot_dim)
    ang = pos_flat.astype(F32)[None, :] * inv_freq[:, None]
    return jnp.cos(ang), jnp.sin(ang)


def _expand_cmp_w1(w1):
    w = w1.reshape(CMP_LEN, A_DIM, CMP_HIDDEN)
    out = []
    for part in (w[:CMP_STRIDE], w[CMP_STRIDE:]):
        z = jnp.zeros_like(part)
        both = jnp.stack([jnp.concatenate([part, z], axis=1), jnp.concatenate([z, part], axis=1)])
        out.append(both.reshape(A_GROUPS, CMP_STRIDE * A_GROUPS * A_DIM, CMP_HIDDEN).transpose(0, 2, 1).astype(BF16))
    return out


TOKEN_TILE = 512
NSA_Q_TILE = 256
NSA_SEL_K_TILE = 512
MLA_TILE = 512
PEER_ROUTE_TILE = 256
PEER_EXPERT_TILE = 2048


def _col(v):
    return v.reshape(-1, 1).astype(F32)


def _mixers(x, positions, norm1_gain, w_in, nsa_q_gain, nsa_kc_gain, nsa_ks_gain, nsa_kw_gain,
            cmp_pos, cmp_k_w1, cmp_k_w2, cmp_v_w1, cmp_v_w2,
            mla_q_lora_gain, mla_w_uq, mla_kv_lora_gain, mla_w_ukv, mla_q_gain, mla_k_gain):
    batch, seq, d = x.shape
    t = batch * seq
    tt = TOKEN_TILE
    tq_nsa = NSA_Q_TILE
    tk_sel = NSA_SEL_K_TILE
    tq_mla = tk_mla = MLA_TILE
    assert tk_sel == tt and tk_mla == tt
    assert d == D_MODEL and seq % 512 == 0 and seq // SLC_LEN >= SLC_TOPK and WINDOW % tq_nsa == 0
    col = _col
    x2 = x.reshape(t, d)
    pos = positions.reshape(t)

    w_in_t = w_in.T
    gate_lo = sum((512, 128, 128, 128, 128, 128, 128))
    gate_hi = gate_lo + 3 * A_HEADS
    w_in_t = jnp.concatenate([w_in_t[:gate_lo], w_in_t[gate_hi:], w_in_t[gate_lo:gate_hi],
                              jnp.zeros((PROJ_ROWS - w_in_t.shape[0], d), F32)], axis=0).astype(BF16)
    proj_t = _in_proj(x2, norm1_gain.reshape(1, d), w_in_t, tt)

    cos_a, sin_a = _rope_tables_t(pos, A_ROPE)
    q_t, kc_tm, vc_tm, k_slc, v_slc_t, k_win, v_win_t, gates_t = _nsa_prep(
        proj_t, cos_a, sin_a, col(nsa_q_gain), col(nsa_ks_gain), col(nsa_kw_gain), tt, seq, tq_nsa)

    nc = seq // CMP_STRIDE
    chunk_w = CMP_STRIDE * A_GROUPS * A_DIM
    w1ka, w1kb = _expand_cmp_w1(cmp_k_w1)
    w1va, w1vb = _expand_cmp_w1(cmp_v_w1)
    pos_rows = lambda p: jnp.broadcast_to(p[:, None, :], (CMP_STRIDE, A_GROUPS, A_DIM)).reshape(1, chunk_w)
    cmp_end = jnp.minimum(jnp.arange(nc) * CMP_STRIDE + CMP_LEN - 1, seq - 1)
    cos_c, sin_c = _rope_tables_t(positions[:, cmp_end].reshape(-1), A_ROPE)
    to_b = lambda a: a.reshape(A_ROPE // 2, batch, nc).transpose(1, 0, 2)
    kcmp, vcmp_t = _compress(kc_tm.reshape(batch, nc, chunk_w), vc_tm.reshape(batch, nc, chunk_w),
                             w1ka, w1kb, w1va, w1vb, pos_rows(cmp_pos[:CMP_STRIDE]), pos_rows(cmp_pos[CMP_STRIDE:]),
                             cmp_k_w2.T.astype(BF16), cmp_v_w2.T.astype(BF16), col(nsa_kc_gain), to_b(cos_c), to_b(sin_c))

    n_cmp = (seq - CMP_LEN) // CMP_STRIDE + 1
    nb = seq // SLC_LEN
    c_start = np.arange(nc)[None, :] * CMP_STRIDE
    s_start = np.arange(nb)[:, None] * SLC_LEN
    ov = (c_start < s_start + SLC_LEN) & (c_start + CMP_LEN - 1 >= s_start) & (np.arange(nc)[None, :] < n_cmp)
    ov_t = jnp.asarray(ov.astype(np.float32)).astype(BF16)

    oc_t, selb = _nsa_cmp(q_t, kcmp, vcmp_t, ov_t, batch, seq, tq_nsa)
    os_t, ow_t = _nsa_sel(q_t, k_slc, v_slc_t, selb, k_win, v_win_t, batch, seq, tq_nsa, tk_sel)

    cos_b, sin_b = _rope_tables_t(pos, B_ROPE)
    q_m, k_m, v_m_t = _mla_prep(proj_t, cos_b, sin_b, col(mla_q_lora_gain), col(mla_kv_lora_gain),
                                col(mla_q_gain), col(mla_k_gain), mla_w_uq.T.astype(BF16), mla_w_ukv.T.astype(BF16), tt)
    ob_t = _mla_attn(q_m, k_m, v_m_t, batch, seq, tq_mla, tk_mla)
    return oc_t, os_t, ow_t, gates_t, ob_t


def _peer(hn_t, h_t, peer_w_q, peer_sub_keys, peer_u, peer_v):
    cut, g1, r2, g2 = _peer_route(hn_t, peer_w_q.T.astype(BF16), peer_sub_keys.astype(BF16), PEER_ROUTE_TILE)
    return _peer_ffn(hn_t, h_t, peer_u.astype(BF16), peer_v.T.astype(BF16),
                     cut, g1, r2, g2, TOKEN_TILE, PEER_EXPERT_TILE)


def _layer(x, positions, norm1_gain, w_in, nsa_q_gain, nsa_kc_gain, nsa_ks_gain, nsa_kw_gain,
           cmp_pos, cmp_k_w1, cmp_k_w2, cmp_v_w1, cmp_v_w2,
           mla_q_lora_gain, mla_w_uq, mla_kv_lora_gain, mla_w_ukv, mla_q_gain, mla_k_gain,
           out_gain_a, out_gain_b, w_out, norm2_gain, peer_w_q, peer_sub_keys, peer_u, peer_v):
    batch, seq, d = x.shape
    oc_t, os_t, ow_t, gates_t, ob_t = _mixers(
        x, positions, norm1_gain, w_in, nsa_q_gain, nsa_kc_gain, nsa_ks_gain, nsa_kw_gain,
        cmp_pos, cmp_k_w1, cmp_k_w2, cmp_v_w1, cmp_v_w2,
        mla_q_lora_gain, mla_w_uq, mla_kv_lora_gain, mla_w_ukv, mla_q_gain, mla_k_gain)
    h_t, hn_t = _out_proj(oc_t, os_t, ow_t, gates_t, ob_t, x.reshape(batch * seq, d), _col(out_gain_a), _col(out_gain_b),
                          w_out.T.astype(BF16), _col(norm2_gain), TOKEN_TILE)
    return _peer(hn_t, h_t, peer_w_q, peer_sub_keys, peer_u, peer_v).reshape(batch, seq, d)


def kernel(x, positions, norm1_gain, w_in, nsa_q_gain, nsa_kc_gain, nsa_ks_gain, nsa_kw_gain, cmp_pos, cmp_k_w1, cmp_k_w2, cmp_v_w1, cmp_v_w2, mla_q_lora_gain, mla_w_uq, mla_kv_lora_gain, mla_w_ukv, mla_q_gain, mla_k_gain, out_gain_a, out_gain_b, w_out, norm2_gain, peer_w_q, peer_sub_keys, peer_u, peer_v):
    h = x
    for l in range(norm1_gain.shape[0]):
        h = _layer(h, positions, norm1_gain[l], w_in[l], nsa_q_gain[l], nsa_kc_gain[l], nsa_ks_gain[l], nsa_kw_gain[l],
                   cmp_pos[l], cmp_k_w1[l], cmp_k_w2[l], cmp_v_w1[l], cmp_v_w2[l],
                   mla_q_lora_gain[l], mla_w_uq[l], mla_kv_lora_gain[l], mla_w_ukv[l], mla_q_gain[l], mla_k_gain[l],
                   out_gain_a[l], out_gain_b[l], w_out[l], norm2_gain[l], peer_w_q[l], peer_sub_keys[l],
                   peer_u[l], peer_v[l])
    return h
```

```python
import functools

import jax
import jax.numpy as jnp
import numpy as np
from jax import lax
from jax.experimental import pallas as pl
from jax.experimental.pallas import tpu as pltpu

F32, BF16 = jnp.float32, jnp.bfloat16
EPS = 1e-6
NEG = -1e30
FORCE = 1e9
ROPE_THETA = 500000.0
LOG2E = 1.4426950408889634

D_MODEL = 1024
A_HEADS, A_GROUPS, A_DIM = 8, 2, 64
A_REP = A_HEADS // A_GROUPS
A_ROPE = A_DIM // 4
CMP_LEN, CMP_STRIDE, CMP_HIDDEN = 32, 16, 256
SLC_LEN, SLC_TOPK, WINDOW = 64, 16, 512
B_HEADS, Q_LORA, KV_LORA, B_NOPE, B_ROPE, B_V = 8, 256, 128, 64, 32, 64
B_QK = B_NOPE + B_ROPE
P_HEADS, N_KEYS, P_KEY_DIM, P_TOPK = 8, 128, 256, 16
N_EXPERTS = N_KEYS * N_KEYS

ROW_Q, ROW_KC, ROW_VC, ROW_KS, ROW_VS, ROW_KW, ROW_VW = 0, 512, 640, 768, 896, 1024, 1152
ROW_CQ, ROW_CKV, ROW_KR, ROW_GATE, PROJ_ROWS = 1280, 1536, 1664, 1696, 1728
GATE_ROWS = 32

VMEM_LIMIT = 56 * 1024 * 1024
NT_DIMS = (((1,), (1,)), ((), ()))


def _params(n_axes):
    return pltpu.CompilerParams(dimension_semantics=("arbitrary",) * n_axes, vmem_limit_bytes=VMEM_LIMIT)


def _dot(a, b):
    return jnp.dot(a, b, preferred_element_type=F32)


def _dot_nt(a, b):
    return lax.dot_general(a, b, NT_DIMS, preferred_element_type=F32)


def _row_sumsq(x):
    sq = x * x
    hi = sq.astype(BF16)
    lo = (sq - hi.astype(F32)).astype(BF16)
    ones = jnp.ones((8, x.shape[1]), BF16)
    return (_dot_nt(ones, hi) + _dot_nt(ones, lo))[0:1, :]


def _rms_rows(x):
    ss = jnp.sum(x * x, axis=0, keepdims=True)
    return x * lax.rsqrt(ss * (1.0 / x.shape[0]) + EPS)


def _rope_rows(y, cos, sin, off, half):
    x1, x2 = y[off:off + half], y[off + half:off + 2 * half]
    parts = [y[:off]] if off else []
    parts += [x1 * cos - x2 * sin, x2 * cos + x1 * sin]
    if off + 2 * half < y.shape[0]:
        parts.append(y[off + 2 * half:])
    return jnp.concatenate(parts, axis=0)


def _gelu_tanh(x):
    c = 0.7978845608028654
    half = 0.5 * x
    return half + half * jnp.tanh(x * (c + (c * 0.044715) * (x * x)))


def _in_proj_kernel(x_ref, g_ref, w_ref, o_ref):
    x = x_ref[...]
    xg = (x * g_ref[...]).astype(BF16)
    p = _dot_nt(w_ref[...], xg)
    rinv = lax.rsqrt(_row_sumsq(x) * (1.0 / x.shape[1]) + EPS)
    o_ref[...] = p * rinv


def _in_proj(x2, gain, w_t, tt):
    t, d = x2.shape
    rows = w_t.shape[0]
    return pl.pallas_call(
        _in_proj_kernel,
        grid=(t // tt,),
        in_specs=[pl.BlockSpec((tt, d), lambda i: (i, 0)),
                  pl.BlockSpec((1, d), lambda i: (0, 0)),
                  pl.BlockSpec((rows, d), lambda i: (0, 0))],
        out_specs=pl.BlockSpec((rows, tt), lambda i: (0, i)),
        out_shape=jax.ShapeDtypeStruct((rows, t), F32),
        compiler_params=_params(1),
        name="in_proj",
    )(x2, gain, w_t)


def _nsa_prep_kernel(q_ref, kc_ref, vc_ref, ks_ref, vs_ref, kw_ref, vw_ref, gt_ref, cos_ref, sin_ref,
                     qg_ref, ksg_ref, kwg_ref,
                     qo_ref, kco_ref, vco_ref, kso_ref, vso_ref, kwo_ref, vwo_ref, gto_ref, *, seq, tw):
    cos, sin = cos_ref[...], sin_ref[...]
    tt = cos.shape[1]
    nb = seq // SLC_LEN
    for h in range(A_HEADS):
        y = _rms_rows(q_ref[h * A_DIM:(h + 1) * A_DIM, :]) * qg_ref[...]
        y = _rope_rows(y, cos, sin, 0, A_ROPE // 2) * (A_DIM ** -0.5 * LOG2E)
        qo_ref[h * A_DIM:(h + 1) * A_DIM, :] = y.astype(BF16)
    kco_ref[...] = kc_ref[...].T
    vco_ref[...] = vc_ref[...].T
    zeros = jnp.zeros((A_DIM, tt), F32)
    tok = pl.program_id(0) * tt + lax.broadcasted_iota(jnp.int32, (tt, nb), 0)
    block_hot = jnp.where(lax.broadcasted_iota(jnp.int32, (tt, nb), 1) == (tok % seq) // SLC_LEN, 1.0, 0.0)
    for g in range(A_GROUPS):
        for src, gain, dst in ((ks_ref, ksg_ref, kso_ref), (kw_ref, kwg_ref, kwo_ref)):
            y = _rms_rows(src[g * A_DIM:(g + 1) * A_DIM, :]) * gain[...]
            y = _rope_rows(y, cos, sin, 0, A_ROPE // 2)
            k_tm = jnp.concatenate([y, zeros], axis=0).T
            if dst is kso_ref:
                k_tm = jnp.concatenate([k_tm, block_hot], axis=1)
            dst[g] = k_tm.astype(BF16)
        vso_ref[g, 0] = _with_ones(vs_ref[g * A_DIM:(g + 1) * A_DIM, :])
        v_win = _with_ones(vw_ref[g * A_DIM:(g + 1) * A_DIM, :])
        for c in range(tt // tw):
            vwo_ref[g, c] = v_win[:, c * tw:(c + 1) * tw]
    gto_ref[...] = 1.0 / (1.0 + jnp.exp(-gt_ref[...]))


def _nsa_prep(proj_t, cos_t, sin_t, q_gain, ks_gain, kw_gain, tt, seq, tw):
    t = proj_t.shape[1]
    gw = A_GROUPS * A_DIM
    kw = 128 + seq // SLC_LEN
    row_spec = lambda rows, start: pl.BlockSpec((rows, tt), lambda i: (start // rows, i))
    col = lambda n: pl.BlockSpec((n, 1), lambda i: (0, 0))
    return pl.pallas_call(
        functools.partial(_nsa_prep_kernel, seq=seq, tw=tw),
        grid=(t // tt,),
        in_specs=[row_spec(512, ROW_Q), row_spec(gw, ROW_KC), row_spec(gw, ROW_VC), row_spec(gw, ROW_KS),
                  row_spec(gw, ROW_VS), row_spec(gw, ROW_KW), row_spec(gw, ROW_VW), row_spec(GATE_ROWS, ROW_GATE),
                  pl.BlockSpec((A_ROPE // 2, tt), lambda i: (0, i)), pl.BlockSpec((A_ROPE // 2, tt), lambda i: (0, i)),
                  col(A_DIM), col(A_DIM), col(A_DIM)],
        out_specs=[pl.BlockSpec((512, tt), lambda i: (0, i)),
                   pl.BlockSpec((tt, gw), lambda i: (i, 0)),
                   pl.BlockSpec((tt, gw), lambda i: (i, 0)),
                   pl.BlockSpec((A_GROUPS, tt, kw), lambda i: (0, i, 0)),
                   pl.BlockSpec((A_GROUPS, 1, A_DIM + ONES_ROWS, tt), lambda i: (0, i, 0, 0)),
                   pl.BlockSpec((A_GROUPS, tt, 128), lambda i: (0, i, 0)),
                   pl.BlockSpec((A_GROUPS, tt // tw, A_DIM + ONES_ROWS, tw), lambda i: (0, i, 0, 0)),
                   pl.BlockSpec((GATE_ROWS, tt), lambda i: (0, i))],
        out_shape=[jax.ShapeDtypeStruct((512, t), BF16),
                   jax.ShapeDtypeStruct((t, gw), F32),
                   jax.ShapeDtypeStruct((t, gw), F32),
                   jax.ShapeDtypeStruct((A_GROUPS, t, kw), BF16),
                   jax.ShapeDtypeStruct((A_GROUPS, t // tt, A_DIM + ONES_ROWS, tt), BF16),
                   jax.ShapeDtypeStruct((A_GROUPS, t, 128), BF16),
                   jax.ShapeDtypeStruct((A_GROUPS, t // tw, A_DIM + ONES_ROWS, tw), BF16),
                   jax.ShapeDtypeStruct((GATE_ROWS, t), F32)],
        compiler_params=_params(1),
        name="nsa_prep",
    )(proj_t, proj_t, proj_t, proj_t, proj_t, proj_t, proj_t, proj_t, cos_t, sin_t, q_gain, ks_gain, kw_gain)


def _compress_kernel(kc_ref, vc_ref, w1ka_ref, w1kb_ref, w1va_ref, w1vb_ref, plo_ref, phi_ref,
                     w2k_ref, w2v_ref, kg_ref, cos_ref, sin_ref, ko_ref, vo_ref):
    nc = kc_ref.shape[1]
    zeros = jnp.zeros((A_DIM, nc), F32)
    for src, w1a, w1b, w2, is_k in ((kc_ref, w1ka_ref, w1kb_ref, w2k_ref, True),
                                    (vc_ref, w1va_ref, w1vb_ref, w2v_ref, False)):
        x = src[0]
        xlo = (x + plo_ref[...]).astype(BF16)
        xhi = (x + phi_ref[...]).astype(BF16)
        for g in range(A_GROUPS):
            first = _dot_nt(w1a[g], xlo)
            second = _dot_nt(w1b[g], xhi)
            hid = _gelu_tanh(first + pltpu.roll(second, nc - 1, axis=1)).astype(BF16)
            c = _dot(w2[...], hid)
            if is_k:
                y = _rope_rows(_rms_rows(c) * kg_ref[...], cos_ref[0], sin_ref[0], 0, A_ROPE // 2)
                ko_ref[0, g] = jnp.concatenate([y, zeros], axis=0).T.astype(BF16)
            else:
                vo_ref[0, g] = c.astype(BF16)


def _compress(kc_chunks, vc_chunks, w1ka, w1kb, w1va, w1vb, plo, phi, w2k_t, w2v_t, kc_gain, cos_c, sin_c):
    b, nc, cw = kc_chunks.shape
    full = lambda a: pl.BlockSpec(a.shape, lambda i: (0,) * a.ndim)
    return pl.pallas_call(
        _compress_kernel,
        grid=(b,),
        in_specs=[pl.BlockSpec((1, nc, cw), lambda i: (i, 0, 0)), pl.BlockSpec((1, nc, cw), lambda i: (i, 0, 0)),
                  full(w1ka), full(w1kb), full(w1va), full(w1vb), full(plo), full(phi), full(w2k_t), full(w2v_t),
                  full(kc_gain),
                  pl.BlockSpec((1, A_ROPE // 2, nc), lambda i: (i, 0, 0)),
                  pl.BlockSpec((1, A_ROPE // 2, nc), lambda i: (i, 0, 0))],
        out_specs=[pl.BlockSpec((1, A_GROUPS, nc, 128), lambda i: (i, 0, 0, 0)),
                   pl.BlockSpec((1, A_GROUPS, A_DIM, nc), lambda i: (i, 0, 0, 0))],
        out_shape=[jax.ShapeDtypeStruct((b, A_GROUPS, nc, 128), BF16),
                   jax.ShapeDtypeStruct((b, A_GROUPS, A_DIM, nc), BF16)],
        compiler_params=_params(1),
        name="nsa_compress",
    )(kc_chunks, vc_chunks, w1ka, w1kb, w1va, w1vb, plo, phi, w2k_t, w2v_t, kc_gain, cos_c, sin_c)


def _stack_heads(q, tq):
    qs = jnp.concatenate([q[r * A_DIM:(r + 1) * A_DIM, :] for r in range(A_REP)], axis=1)
    return jnp.concatenate([qs, jnp.zeros_like(qs)], axis=0)


def _nsa_cmp_kernel(q_ref, k_ref, v_ref, ov_ref, o_ref, sb_ref, *, tq):
    i = pl.program_id(2)
    n = A_REP * tq
    nc = k_ref.shape[2]
    nb = ov_ref.shape[0]
    qp = _stack_heads(q_ref[...], tq)
    s = _dot(k_ref[0, 0], qp)
    cmp_end = lax.broadcasted_iota(jnp.int32, (nc, n), 0) * CMP_STRIDE + (CMP_LEN - 1)
    tok = i * tq + (lax.broadcasted_iota(jnp.int32, (nc, n), 1) & (tq - 1))
    mask = cmp_end <= tok
    s = jnp.where(mask, s, NEG)
    m = jnp.max(s, axis=0, keepdims=True)
    p = jnp.where(mask, jnp.exp2(s - m), 0.0)
    l = jnp.sum(p, axis=0, keepdims=True)
    inv = jnp.where(l > 0.0, 1.0 / l, 0.0)
    pn = (p * inv).astype(BF16)
    oc = _dot(v_ref[0, 0], pn)
    for r in range(A_REP):
        o_ref[r * A_DIM:(r + 1) * A_DIM, :] = oc[:, r * tq:(r + 1) * tq]
    imp4 = _dot(ov_ref[...], pn)
    imp = imp4[:, 0:tq]
    for r in range(1, A_REP):
        imp = imp + imp4[:, r * tq:(r + 1) * tq]

    blk = lax.broadcasted_iota(jnp.int32, (nb, tq), 0)
    t = i * tq + lax.broadcasted_iota(jnp.int32, (nb, tq), 1)
    forced = (blk == (t >> 6)) | (blk == 0)
    v = jnp.where(forced, FORCE, jnp.where(blk * SLC_LEN <= t, imp, NEG))
    blk_f = blk.astype(F32)
    sel = jnp.zeros((nb, tq), F32)
    for _ in range(min(SLC_TOPK, nb)):
        mx = jnp.max(v, axis=0, keepdims=True)
        first = jnp.min(jnp.where(v == mx, blk_f, float(nb)), axis=0, keepdims=True)
        hit = blk_f == first
        sel = jnp.where(hit, 1.0, sel)
        v = jnp.where(hit, -jnp.inf, v)
    sb_ref[0] = jnp.where(sel > 0.0, 0.0, NEG)


def _nsa_cmp(q_t, kcmp, vcmp_t, ov_t, batch, seq, tq):
    nq = seq // tq
    nc = kcmp.shape[2]
    nb = ov_t.shape[0]
    t = q_t.shape[1]
    gr = A_REP * A_DIM
    return pl.pallas_call(
        functools.partial(_nsa_cmp_kernel, tq=tq),
        grid=(batch, A_GROUPS, nq),
        in_specs=[pl.BlockSpec((gr, tq), lambda b, g, i: (g, b * nq + i)),
                  pl.BlockSpec((1, 1, nc, 128), lambda b, g, i: (b, g, 0, 0)),
                  pl.BlockSpec((1, 1, A_DIM, nc), lambda b, g, i: (b, g, 0, 0)),
                  pl.BlockSpec((nb, nc), lambda b, g, i: (0, 0))],
        out_specs=[pl.BlockSpec((gr, tq), lambda b, g, i: (g, b * nq + i)),
                   pl.BlockSpec((1, nb, tq), lambda b, g, i: (g, 0, b * nq + i))],
        out_shape=[jax.ShapeDtypeStruct((A_HEADS * A_DIM, t), F32),
                   jax.ShapeDtypeStruct((A_GROUPS, nb, t), F32)],
        compiler_params=_params(3),
        name="nsa_cmp",
    )(q_t, kcmp, vcmp_t, ov_t)


ONES_ROWS = 16
FLASH_UNROLL = 4


def _with_ones(v):
    return jnp.concatenate([v, jnp.ones((ONES_ROWS, v.shape[1]), F32)], axis=0).astype(BF16)


def _flash_init(m_scr, acc_scr):
    m_scr[...] = jnp.full(m_scr.shape, -jnp.inf, F32)
    acc_scr[...] = jnp.zeros(acc_scr.shape, F32)


def _flash_update(s, v_t, m_scr, acc_scr):
    m_prev = m_scr[...]
    m_new = jnp.maximum(m_prev, jnp.max(s, axis=0, keepdims=True))
    alpha = jnp.exp2(m_prev - m_new)
    p = jnp.exp2(s - m_new)
    acc_scr[...] = alpha * acc_scr[...] + _dot(v_t, p.astype(BF16))
    m_scr[...] = m_new


def _flash_result(acc_scr, dv):
    acc = acc_scr[...]
    return acc[:dv] * (1.0 / acc[dv:dv + 1])


def _flash_causal(scores, values, mask_last, n_full, s_scr, m_scr, acc_scr):
    unroll = s_scr.shape[0]
    s_scr[0] = scores(0)

    def trip(t, carry):
        j = unroll * t
        for u in range(unroll):
            s_scr[(u + 1) % unroll] = scores(j + u + 1)
            _flash_update(s_scr[u], values(j + u), m_scr, acc_scr)
        return carry

    lax.fori_loop(0, n_full // unroll, trip, 0)
    first = (n_full // unroll) * unroll
    for rest in range(unroll):

        @pl.when(n_full - first == rest)
        def _(rest=rest):
            for u in range(rest + 1):
                if u < rest:
                    s_scr[u + 1] = scores(first + u + 1)
                    _flash_update(s_scr[u], values(first + u), m_scr, acc_scr)
                else:
                    _flash_update(mask_last(s_scr[u]), values(first + u), m_scr, acc_scr)


def _nsa_sel_kernel(q_ref, k_ref, v_ref, sb_ref, kw_ref, vw_ref, o_ref, ow_ref, qa_scr, s_scr, m_scr, acc_scr,
                    *, tq, tk):
    i = pl.program_id(2)
    n = A_REP * tq
    q = q_ref[...]
    qs = jnp.concatenate([q[r * A_DIM:(r + 1) * A_DIM, :] for r in range(A_REP)], axis=1)
    sb = sb_ref[0].astype(BF16)
    qa_scr[...] = jnp.concatenate([qs, jnp.zeros_like(qs), jnp.concatenate([sb] * A_REP, axis=1)], axis=0)
    _flash_init(m_scr, acc_scr)
    j_diag = (i * tq + tq - 1) // tk

    def scores(j):
        return _dot(k_ref[0, pl.ds(pl.multiple_of(j * tk, tk), tk), :], qa_scr[...])

    def causal(s):
        kpos = j_diag * tk + lax.broadcasted_iota(jnp.int32, (tk, n), 0)
        tok = i * tq + (lax.broadcasted_iota(jnp.int32, (tk, n), 1) & (tq - 1))
        return jnp.where(kpos <= tok, s, NEG)

    _flash_causal(scores, lambda j: v_ref[0, j], causal, j_diag, s_scr, m_scr, acc_scr)
    o = _flash_result(acc_scr, A_DIM)
    for r in range(A_REP):
        o_ref[r * A_DIM:(r + 1) * A_DIM, :] = o[:, r * tq:(r + 1) * tq]

    _flash_init(m_scr, acc_scr)
    n_back = WINDOW // tq
    for c in range(n_back + 1):
        kt = i - n_back + c

        @pl.when(kt >= 0)
        def _(kt=kt):
            s = _dot(kw_ref[0, pl.ds(pl.multiple_of(kt * tq, tq), tq), :], qa_scr[0:128, :])
            kpos = kt * tq + lax.broadcasted_iota(jnp.int32, (tq, n), 0)
            tok = i * tq + (lax.broadcasted_iota(jnp.int32, (tq, n), 1) & (tq - 1))
            s = jnp.where((kpos <= tok) & (kpos > tok - WINDOW), s, NEG)
            _flash_update(s, vw_ref[0, kt], m_scr, acc_scr)

    o = _flash_result(acc_scr, A_DIM)
    for r in range(A_REP):
        ow_ref[r * A_DIM:(r + 1) * A_DIM, :] = o[:, r * tq:(r + 1) * tq]


def _nsa_sel(q_t, k_aug, v_slc_tiles, selb, k_win, v_win_tiles, batch, seq, tq, tk):
    nq, nk = seq // tq, seq // tk
    nb = selb.shape[1]
    t = q_t.shape[1]
    gr = A_REP * A_DIM
    n = A_REP * tq
    kw = k_aug.shape[2]
    out = jax.ShapeDtypeStruct((A_HEADS * A_DIM, t), F32)
    return pl.pallas_call(
        functools.partial(_nsa_sel_kernel, tq=tq, tk=tk),
        grid=(batch, A_GROUPS, nq),
        in_specs=[pl.BlockSpec((gr, tq), lambda b, g, i: (g, b * nq + i)),
                  pl.BlockSpec((1, seq, kw), lambda b, g, i: (g, b, 0)),
                  pl.BlockSpec((1, nk, A_DIM + ONES_ROWS, tk), lambda b, g, i: (g, b, 0, 0)),
                  pl.BlockSpec((1, nb, tq), lambda b, g, i: (g, 0, b * nq + i)),
                  pl.BlockSpec((1, seq, 128), lambda b, g, i: (g, b, 0)),
                  pl.BlockSpec((1, nq, A_DIM + ONES_ROWS, tq), lambda b, g, i: (g, b, 0, 0))],
        out_specs=[pl.BlockSpec((gr, tq), lambda b, g, i: (g, b * nq + i)),
                   pl.BlockSpec((gr, tq), lambda b, g, i: (g, b * nq + i))],
        out_shape=[out, out],
        scratch_shapes=[pltpu.VMEM((kw, n), BF16), pltpu.VMEM((FLASH_UNROLL, tk, n), F32),
                        pltpu.VMEM((1, n), F32), pltpu.VMEM((A_DIM + ONES_ROWS, n), F32)],
        compiler_params=_params(3),
        name="nsa_sel",
    )(q_t, k_aug, v_slc_tiles, selb, k_win, v_win_tiles)


def _mla_prep_kernel(cq_ref, ckv_ref, kr_ref, cos_ref, sin_ref, qlg_ref, kvlg_ref, qg_ref, kg_ref, wuq_ref, wukv_ref,
                     qo_ref, ko_ref, vo_ref):
    cos, sin = cos_ref[...], sin_ref[...]
    tt = cos.shape[1]
    q_all = _dot(wuq_ref[...], (_rms_rows(cq_ref[...]) * qlg_ref[...]).astype(BF16))
    kv_all = _dot(wukv_ref[...], (_rms_rows(ckv_ref[...]) * kvlg_ref[...]).astype(BF16))
    kr = kr_ref[...]
    pad = jnp.zeros((128 - B_QK, tt), F32)
    for h in range(B_HEADS):
        y = _rms_rows(q_all[h * B_QK:(h + 1) * B_QK]) * qg_ref[...]
        y = _rope_rows(y, cos, sin, B_NOPE, B_ROPE // 2) * (B_QK ** -0.5 * LOG2E)
        qo_ref[h] = jnp.concatenate([y, pad], axis=0).astype(BF16)
        base = h * (B_NOPE + B_V)
        k = jnp.concatenate([kv_all[base:base + B_NOPE], kr], axis=0)
        y = _rope_rows(_rms_rows(k) * kg_ref[...], cos, sin, B_NOPE, B_ROPE // 2)
        ko_ref[h] = jnp.concatenate([y, pad], axis=0).T.astype(BF16)
        vo_ref[h, 0] = _with_ones(kv_all[base + B_NOPE:base + B_NOPE + B_V])


def _mla_prep(proj_t, cos_t, sin_t, q_lora_gain, kv_lora_gain, q_gain, k_gain, wuq_t, wukv_t, tt):
    t = proj_t.shape[1]
    row_spec = lambda rows, start: pl.BlockSpec((rows, tt), lambda i: (start // rows, i))
    full = lambda a: pl.BlockSpec(a.shape, lambda i: (0,) * a.ndim)
    return pl.pallas_call(
        _mla_prep_kernel,
        grid=(t // tt,),
        in_specs=[row_spec(Q_LORA, ROW_CQ), row_spec(KV_LORA, ROW_CKV), row_spec(B_ROPE, ROW_KR),
                  pl.BlockSpec((B_ROPE // 2, tt), lambda i: (0, i)), pl.BlockSpec((B_ROPE // 2, tt), lambda i: (0, i)),
                  full(q_lora_gain), full(kv_lora_gain), full(q_gain), full(k_gain), full(wuq_t), full(wukv_t)],
        out_specs=[pl.BlockSpec((B_HEADS, 128, tt), lambda i: (0, 0, i)),
                   pl.BlockSpec((B_HEADS, tt, 128), lambda i: (0, i, 0)),
                   pl.BlockSpec((B_HEADS, 1, B_V + ONES_ROWS, tt), lambda i: (0, i, 0, 0))],
        out_shape=[jax.ShapeDtypeStruct((B_HEADS, 128, t), BF16),
                   jax.ShapeDtypeStruct((B_HEADS, t, 128), BF16),
                   jax.ShapeDtypeStruct((B_HEADS, t // tt, B_V + ONES_ROWS, tt), BF16)],
        compiler_params=_params(1),
        name="mla_prep",
    )(proj_t, proj_t, proj_t, cos_t, sin_t, q_lora_gain, kv_lora_gain, q_gain, k_gain, wuq_t, wukv_t)


def _mla_attn_kernel(q_ref, k_ref, v_ref, o_ref, s_scr, m_scr, acc_scr, *, tq, tk):
    i = pl.program_id(2)
    _flash_init(m_scr, acc_scr)
    j_diag = (i * tq + tq - 1) // tk

    def scores(j):
        return _dot(k_ref[0, pl.ds(pl.multiple_of(j * tk, tk), tk), :], q_ref[0])

    def causal(s):
        kpos = j_diag * tk + lax.broadcasted_iota(jnp.int32, (tk, tq), 0)
        tok = i * tq + lax.broadcasted_iota(jnp.int32, (tk, tq), 1)
        return jnp.where(kpos <= tok, s, NEG)

    _flash_causal(scores, lambda j: v_ref[0, j], causal, j_diag, s_scr, m_scr, acc_scr)
    o_ref[...] = _flash_result(acc_scr, B_V)


def _mla_attn(q_m, k_m, v_m_tiles, batch, seq, tq, tk):
    nq, nk = seq // tq, seq // tk
    t = q_m.shape[2]
    return pl.pallas_call(
        functools.partial(_mla_attn_kernel, tq=tq, tk=tk),
        grid=(batch, B_HEADS, nq),
        in_specs=[pl.BlockSpec((1, 128, tq), lambda b, h, i: (h, 0, b * nq + i)),
                  pl.BlockSpec((1, seq, 128), lambda b, h, i: (h, b, 0)),
                  pl.BlockSpec((1, nk, B_V + ONES_ROWS, tk), lambda b, h, i: (h, b, 0, 0))],
        out_specs=pl.BlockSpec((B_V, tq), lambda b, h, i: (h, b * nq + i)),
        out_shape=jax.ShapeDtypeStruct((B_HEADS * B_V, t), F32),
        scratch_shapes=[pltpu.VMEM((FLASH_UNROLL, tk, tq), F32),
                        pltpu.VMEM((1, tq), F32), pltpu.VMEM((B_V + ONES_ROWS, tq), F32)],
        compiler_params=_params(3),
        name="mla_attn",
    )(q_m, k_m, v_m_tiles)


def _out_proj_kernel(oc_ref, os_ref, ow_ref, gt_ref, ob_ref, x_ref, ga_ref, gb_ref, w_ref, g2_ref, h_ref, hn_ref):
    heads = []
    for h in range(A_HEADS):
        rows = slice(h * A_DIM, (h + 1) * A_DIM)
        heads.append(gt_ref[3 * h:3 * h + 1, :] * oc_ref[rows, :] + gt_ref[3 * h + 1:3 * h + 2, :] * os_ref[rows, :]
                     + gt_ref[3 * h + 2:3 * h + 3, :] * ow_ref[rows, :])
    oa = _rms_rows(jnp.concatenate(heads, axis=0)) * ga_ref[...]
    ob = _rms_rows(ob_ref[...]) * gb_ref[...]
    cat = jnp.concatenate([oa, ob], axis=0).astype(BF16)
    hid = x_ref[...].T + _dot(w_ref[...], cat)
    h_ref[...] = hid
    hn_ref[...] = (_rms_rows(hid) * g2_ref[...]).astype(BF16)


def _out_proj(oc_t, os_t, ow_t, gates_t, ob_t, x2, gain_a, gain_b, w_out_t, gain2, tt):
    t, d = x2.shape
    aw = oc_t.shape[0]
    bw = ob_t.shape[0]
    tok = lambda rows: pl.BlockSpec((rows, tt), lambda i: (0, i))
    full = lambda a: pl.BlockSpec(a.shape, lambda i: (0,) * a.ndim)
    return pl.pallas_call(
        _out_proj_kernel,
        grid=(t // tt,),
        in_specs=[tok(aw), tok(aw), tok(aw), tok(GATE_ROWS), tok(bw), pl.BlockSpec((tt, d), lambda i: (i, 0)),
                  full(gain_a), full(gain_b), full(w_out_t), full(gain2)],
        out_specs=[tok(d), tok(d)],
        out_shape=[jax.ShapeDtypeStruct((d, t), F32), jax.ShapeDtypeStruct((d, t), BF16)],
        compiler_params=_params(1),
        name="out_proj",
    )(oc_t, os_t, ow_t, gates_t, ob_t, x2, gain_a, gain_b, w_out_t, gain2)


def _top_ranked(s):
    n, tp = s.shape
    row = lax.broadcasted_iota(jnp.int32, (n, tp), 0).astype(F32)
    slot = lax.broadcasted_iota(jnp.int32, (P_TOPK, tp), 0)

    def body(a, carry):
        v, rank, vals = carry
        mx = jnp.max(v, axis=0, keepdims=True)
        first = jnp.min(jnp.where(v == mx, row, float(n)), axis=0, keepdims=True)
        hit = row == first
        rank = jnp.where(hit, jnp.asarray(a, F32), rank)
        v = jnp.where(hit, -jnp.inf, v)
        vals = jnp.where(slot == a, mx, vals)
        return v, rank, vals

    _, rank, vals = lax.fori_loop(0, P_TOPK, body,
                                  (s, jnp.full((n, tp), float(P_TOPK), F32), jnp.zeros((P_TOPK, tp), F32)))
    return rank, vals


def _pair_counts(v1, v2):
    k = v1.shape[0]
    slot = lax.broadcasted_iota(jnp.int32, v1.shape, 0).astype(F32)
    top = v1[0:1] + v2[0:1]

    def body(_, carry):
        count, front, z = carry
        mx = jnp.max(front, axis=0, keepdims=True)
        a_star = jnp.min(jnp.where(front == mx, slot, float(k)), axis=0, keepdims=True)
        hit = slot == a_star
        count = count + jnp.where(hit, 1.0, 0.0)
        nxt = jnp.sum(jnp.where(hit, count, 0.0), axis=0, keepdims=True)
        v2_nxt = jnp.sum(jnp.where(slot == nxt, v2, 0.0), axis=0, keepdims=True)
        front = jnp.where(hit, jnp.where(nxt < float(k), v1 + v2_nxt, -jnp.inf), front)
        return count, front, z + jnp.exp(mx - top)

    count, _, z = lax.fori_loop(0, k, body, (jnp.zeros(v1.shape, F32), v1 + v2[0:1], jnp.zeros(top.shape, F32)))
    return count, z


REMOVED = -2.0 ** 126
LANES = 128


def _top_ranked_pair_fast(s1, s2):
    n, tp = s1.shape
    slot = lax.broadcasted_iota(jnp.int32, (P_TOPK, tp), 0)

    def body(a, carry):
        code = REMOVED * (1.0 + jnp.asarray(a, F32) * (1.0 / 32.0))
        out = []
        for key, vals in (carry[0:2], carry[2:4]):
            mx = jnp.max(key, axis=0, keepdims=True)
            key = jnp.where(key == mx, code, key)
            out += [key, jnp.where(slot == a, mx, vals)]
        return tuple(out)

    zeros = jnp.zeros((P_TOPK, tp), F32)
    k1, t1, k2, t2 = lax.fori_loop(0, P_TOPK, body, (s1, zeros, s2, zeros))
    res, bad = [], jnp.zeros((1, tp), F32)
    for key, vals in ((k1, t1), (k2, t2)):
        removed = key <= REMOVED
        rank = jnp.where(removed, (key * (1.0 / REMOVED) - 1.0) * 32.0, float(P_TOPK))
        n_removed = jnp.sum(jnp.where(removed, 1.0, 0.0), axis=0, keepdims=True)
        bad = jnp.maximum(bad, jnp.abs(n_removed - float(P_TOPK)))
        res += [rank, vals]
    return res[0], res[1], res[2], res[3], bad


def _pack_bf16_twice(x):
    hi = pltpu.bitcast(x.astype(BF16).astype(F32), jnp.int32)
    return hi | lax.shift_right_logical(hi, 16)


def _peer_route_kernel(hn_ref, wq_ref, keys_ref, cut_ref, g1_ref, r2_ref, g2_ref, q_scr, s_scr, rank_scr, vals_scr):
    tp = hn_ref.shape[1]
    half = P_KEY_DIM // 2
    q_scr[...] = _dot(wq_ref[...], hn_ref[...])
    for h in range(P_HEADS):
        q = _rms_rows(q_scr[h * P_KEY_DIM:(h + 1) * P_KEY_DIM, :]).astype(BF16)
        s_scr[0, h] = _dot(keys_ref[h, 0], q[:half])
        s_scr[1, h] = _dot(keys_ref[h, 1], q[half:])

    def first_level(rank_pair, flagged):
        for h in range(P_HEADS):
            for lt in range(tp // LANES):
                lanes = slice(lt * LANES, (lt + 1) * LANES)
                rank1, vals1, rank2, vals2, bad = rank_pair(s_scr[0, h, :, lanes], s_scr[1, h, :, lanes])
                flagged = jnp.maximum(flagged, bad)
                rank_scr[0, h, :, lanes] = rank1
                rank_scr[1, h, :, lanes] = rank2
                for a in range(P_TOPK):
                    vals_scr[0, a, h:h + 1, lanes] = vals1[a:a + 1]
                    vals_scr[1, a, h:h + 1, lanes] = vals2[a:a + 1]
        return flagged

    flagged = first_level(_top_ranked_pair_fast, jnp.zeros((1, LANES), F32))

    @pl.when(jnp.max(flagged) > 0.0)
    def _():
        first_level(lambda s1, s2: (*_top_ranked(s1), *_top_ranked(s2), jnp.zeros((1, LANES), F32)), flagged)

    count, z = _pair_counts(vals_scr[0], vals_scr[1])
    for h in range(P_HEADS):
        rank1 = rank_scr[0, h]
        cut = jnp.zeros_like(rank1)
        for a in range(P_TOPK):
            cut = jnp.where(rank1 == float(a), count[a, h:h + 1, :], cut)
        cut_ref[h] = _pack_bf16_twice(cut)
        g1_ref[h] = _pack_bf16_twice(jnp.exp(s_scr[0, h] - vals_scr[0, 0, h:h + 1, :]))
        r2_ref[h] = rank_scr[1, h].astype(BF16)
        g2_ref[h] = (jnp.exp(s_scr[1, h] - vals_scr[1, 0, h:h + 1, :]) * (1.0 / z[0, h:h + 1, :])).astype(BF16)


def _peer_route(hn_t, wq_t, sub_keys, tp):
    d, t = hn_t.shape
    words = jax.ShapeDtypeStruct((P_HEADS, N_KEYS, t), jnp.int32)
    halfs = jax.ShapeDtypeStruct((P_HEADS, N_KEYS, t), BF16)
    ospec = pl.BlockSpec((P_HEADS, N_KEYS, tp), lambda i: (0, 0, i))
    return pl.pallas_call(
        _peer_route_kernel,
        grid=(t // tp,),
        in_specs=[pl.BlockSpec((d, tp), lambda i: (0, i)),
                  pl.BlockSpec(wq_t.shape, lambda i: (0, 0)),
                  pl.BlockSpec(sub_keys.shape, lambda i: (0, 0, 0, 0))],
        out_specs=[ospec, ospec, ospec, ospec],
        out_shape=[words, words, halfs, halfs],
        scratch_shapes=[pltpu.VMEM((P_HEADS * P_KEY_DIM, tp), F32), pltpu.VMEM((2, P_HEADS, N_KEYS, tp), F32),
                        pltpu.VMEM((2, P_HEADS, N_KEYS, tp), F32), pltpu.VMEM((2, P_TOPK, P_HEADS, tp), F32)],
        compiler_params=_params(1),
        name="peer_route",
    )(hn_t, wq_t, sub_keys)


def _peer_ffn_kernel(hn_ref, h_ref, u_ref, v_ref, cut_ref, g1_ref, r2_ref, g2_ref, o_ref, acc_scr, w_scr, *, te):
    e = pl.program_id(1)

    @pl.when(e == 0)
    def _():
        acc_scr[...] = jnp.zeros(acc_scr.shape, F32)
        w_scr[...] = jnp.zeros(w_scr.shape, BF16)

    hn = hn_ref[...]
    tt = hn.shape[1]

    def rows_bf16(word_row):
        tile = pltpu.bitcast(jnp.broadcast_to(word_row, (8, tt)), BF16)
        return jnp.concatenate([tile] * (N_KEYS // 16), axis=0)

    last = pl.num_programs(1) - 1

    @pl.when(e < last)
    def _():
        weights = []
        for ii in range(te // N_KEYS):
            a = _dot(u_ref[ii * N_KEYS:(ii + 1) * N_KEYS, :], hn)
            gate = jnp.zeros((N_KEYS, tt), BF16)
            for h in range(P_HEADS):
                chosen = r2_ref[h] < rows_bf16(cut_ref[h, ii:ii + 1, :])
                gate = gate + (jnp.where(chosen, g2_ref[h], jnp.zeros_like(gate))
                               * rows_bf16(g1_ref[h, ii:ii + 1, :]))
            weights.append(gate * _gelu_tanh(a).astype(BF16))
        acc_scr[...] += _dot(v_ref[...], w_scr[...])
        w_scr[...] = jnp.concatenate(weights, axis=0)

    @pl.when(e == last)
    def _():
        o_ref[...] = (h_ref[...] + acc_scr[...] + _dot(v_ref[...], w_scr[...])).T


def _peer_ffn(hn_t, h_t, u_bf, v_t_bf, cut_k, g1_k, r2, g2, tt, te):
    d, t = hn_t.shape
    n_tiles = u_bf.shape[0] // te
    kpe = te // N_KEYS
    this = lambda e: jnp.minimum(e, n_tiles - 1)
    prev = lambda e: jnp.maximum(e - 1, 0)
    return pl.pallas_call(
        functools.partial(_peer_ffn_kernel, te=te),
        grid=(t // tt, n_tiles + 1),
        in_specs=[pl.BlockSpec((d, tt), lambda i, e: (0, i)),
                  pl.BlockSpec((d, tt), lambda i, e: (0, i)),
                  pl.BlockSpec((te, d), lambda i, e: (this(e), 0)),
                  pl.BlockSpec((d, te), lambda i, e: (0, prev(e))),
                  pl.BlockSpec((P_HEADS, kpe, tt), lambda i, e: (0, this(e), i)),
                  pl.BlockSpec((P_HEADS, kpe, tt), lambda i, e: (0, this(e), i)),
                  pl.BlockSpec((P_HEADS, N_KEYS, tt), lambda i, e: (0, 0, i)),
                  pl.BlockSpec((P_HEADS, N_KEYS, tt), lambda i, e: (0, 0, i))],
        out_specs=pl.BlockSpec((tt, d), lambda i, e: (i, 0)),
        out_shape=jax.ShapeDtypeStruct((t, d), F32),
        scratch_shapes=[pltpu.VMEM((d, tt), F32), pltpu.VMEM((te, tt), BF16)],
        compiler_params=_params(2),
        name="peer_ffn",
    )(hn_t, h_t, u_bf, v_t_bf, cut_k, g1_k, r2, g2)


def _rope_tables_t(pos_flat, rot_dim):
    inv_freq = ROPE_THETA ** (-jnp.arange(0, rot_dim, 2, dtype=F32) / rot_dim)
    ang = pos_flat.astype(F32)[None, :] * inv_freq[:, None]
    return jnp.cos(ang), jnp.sin(ang)


def _expand_cmp_w1(w1):
    w = w1.reshape(CMP_LEN, A_DIM, CMP_HIDDEN)
    out = []
    for part in (w[:CMP_STRIDE], w[CMP_STRIDE:]):
        z = jnp.zeros_like(part)
        both = jnp.stack([jnp.concatenate([part, z], axis=1), jnp.concatenate([z, part], axis=1)])
        out.append(both.reshape(A_GROUPS, CMP_STRIDE * A_GROUPS * A_DIM, CMP_HIDDEN).transpose(0, 2, 1).astype(BF16))
    return out


TOKEN_TILE = 512
NSA_Q_TILE = 256
NSA_SEL_K_TILE = 512
MLA_TILE = 512
PEER_ROUTE_TILE = 256
PEER_EXPERT_TILE = 2048
PEER_OUT_GROUP = 2048


def _col(v):
    return v.reshape(-1, 1).astype(F32)


def _mixers(x, positions, norm1_gain, w_in, nsa_q_gain, nsa_kc_gain, nsa_ks_gain, nsa_kw_gain,
            cmp_pos, cmp_k_w1, cmp_k_w2, cmp_v_w1, cmp_v_w2,
            mla_q_lora_gain, mla_w_uq, mla_kv_lora_gain, mla_w_ukv, mla_q_gain, mla_k_gain):
    batch, seq, d = x.shape
    t = batch * seq
    tt = TOKEN_TILE
    tq_nsa = NSA_Q_TILE
    tk_sel = NSA_SEL_K_TILE
    tq_mla = tk_mla = MLA_TILE
    assert tk_sel == tt and tk_mla == tt
    assert d == D_MODEL and seq % 512 == 0 and seq // SLC_LEN >= SLC_TOPK and WINDOW % tq_nsa == 0
    col = _col
    x2 = x.reshape(t, d)
    pos = positions.reshape(t)

    w_in_t = w_in.T
    gate_lo = sum((512, 128, 128, 128, 128, 128, 128))
    gate_hi = gate_lo + 3 * A_HEADS
    w_in_t = jnp.concatenate([w_in_t[:gate_lo], w_in_t[gate_hi:], w_in_t[gate_lo:gate_hi],
                              jnp.zeros((PROJ_ROWS - w_in_t.shape[0], d), F32)], axis=0).astype(BF16)
    proj_t = _in_proj(x2, norm1_gain.reshape(1, d), w_in_t, tt)

    cos_a, sin_a = _rope_tables_t(pos, A_ROPE)
    q_t, kc_tm, vc_tm, k_slc, v_slc_t, k_win, v_win_t, gates_t = _nsa_prep(
        proj_t, cos_a, sin_a, col(nsa_q_gain), col(nsa_ks_gain), col(nsa_kw_gain), tt, seq, tq_nsa)

    nc = seq // CMP_STRIDE
    chunk_w = CMP_STRIDE * A_GROUPS * A_DIM
    w1ka, w1kb = _expand_cmp_w1(cmp_k_w1)
    w1va, w1vb = _expand_cmp_w1(cmp_v_w1)
    pos_rows = lambda p: jnp.broadcast_to(p[:, None, :], (CMP_STRIDE, A_GROUPS, A_DIM)).reshape(1, chunk_w)
    cmp_end = jnp.minimum(jnp.arange(nc) * CMP_STRIDE + CMP_LEN - 1, seq - 1)
    cos_c, sin_c = _rope_tables_t(positions[:, cmp_end].reshape(-1), A_ROPE)
    to_b = lambda a: a.reshape(A_ROPE // 2, batch, nc).transpose(1, 0, 2)
    kcmp, vcmp_t = _compress(kc_tm.reshape(batch, nc, chunk_w), vc_tm.reshape(batch, nc, chunk_w),
                             w1ka, w1kb, w1va, w1vb, pos_rows(cmp_pos[:CMP_STRIDE]), pos_rows(cmp_pos[CMP_STRIDE:]),
                             cmp_k_w2.T.astype(BF16), cmp_v_w2.T.astype(BF16), col(nsa_kc_gain), to_b(cos_c), to_b(sin_c))

    n_cmp = (seq - CMP_LEN) // CMP_STRIDE + 1
    nb = seq // SLC_LEN
    c_start = np.arange(nc)[None, :] * CMP_STRIDE
    s_start = np.arange(nb)[:, None] * SLC_LEN
    ov = (c_start < s_start + SLC_LEN) & (c_start + CMP_LEN - 1 >= s_start) & (np.arange(nc)[None, :] < n_cmp)
    ov_t = jnp.asarray(ov.astype(np.float32)).astype(BF16)

    oc_t, selb = _nsa_cmp(q_t, kcmp, vcmp_t, ov_t, batch, seq, tq_nsa)
    os_t, ow_t = _nsa_sel(q_t, k_slc, v_slc_t, selb, k_win, v_win_t, batch, seq, tq_nsa, tk_sel)

    cos_b, sin_b = _rope_tables_t(pos, B_ROPE)
    q_m, k_m, v_m_t = _mla_prep(proj_t, cos_b, sin_b, col(mla_q_lora_gain), col(mla_kv_lora_gain),
                                col(mla_q_gain), col(mla_k_gain), mla_w_uq.T.astype(BF16), mla_w_ukv.T.astype(BF16), tt)
    ob_t = _mla_attn(q_m, k_m, v_m_t, batch, seq, tq_mla, tk_mla)
    return oc_t, os_t, ow_t, gates_t, ob_t


def _peer(hn_t, h_t, peer_w_q, peer_sub_keys, peer_u, peer_v):
    cut, g1, r2, g2 = _peer_route(hn_t, peer_w_q.T.astype(BF16), peer_sub_keys.astype(BF16), PEER_ROUTE_TILE)
    return _peer_ffn(hn_t, h_t, peer_u.astype(BF16), peer_v.T.astype(BF16),
                     cut, g1, r2, g2, TOKEN_TILE, PEER_EXPERT_TILE)


def _layer(x, positions, norm1_gain, w_in, nsa_q_gain, nsa_kc_gain, nsa_ks_gain, nsa_kw_gain,
           cmp_pos, cmp_k_w1, cmp_k_w2, cmp_v_w1, cmp_v_w2,
           mla_q_lora_gain, mla_w_uq, mla_kv_lora_gain, mla_w_ukv, mla_q_gain, mla_k_gain,
           out_gain_a, out_gain_b, w_out, norm2_gain, peer_w_q, peer_sub_keys, peer_u, peer_v):
    batch, seq, d = x.shape
    oc_t, os_t, ow_t, gates_t, ob_t = _mixers(
        x, positions, norm1_gain, w_in, nsa_q_gain, nsa_kc_gain, nsa_ks_gain, nsa_kw_gain,
        cmp_pos, cmp_k_w1, cmp_k_w2, cmp_v_w1, cmp_v_w2,
        mla_q_lora_gain, mla_w_uq, mla_kv_lora_gain, mla_w_ukv, mla_q_gain, mla_k_gain)
    h_t, hn_t = _out_proj(oc_t, os_t, ow_t, gates_t, ob_t, x.reshape(batch * seq, d), _col(out_gain_a), _col(out_gain_b),
                          w_out.T.astype(BF16), _col(norm2_gain), TOKEN_TILE)
    return _peer(hn_t, h_t, peer_w_q, peer_sub_keys, peer_u, peer_v).reshape(batch, seq, d)


def kernel(x, positions, norm1_gain, w_in, nsa_q_gain, nsa_kc_gain, nsa_ks_gain, nsa_kw_gain, cmp_pos, cmp_k_w1, cmp_k_w2, cmp_v_w1, cmp_v_w2, mla_q_lora_gain, mla_w_uq, mla_kv_lora_gain, mla_w_ukv, mla_q_gain, mla_k_gain, out_gain_a, out_gain_b, w_out, norm2_gain, peer_w_q, peer_sub_keys, peer_u, peer_v):
    h = x
    for l in range(norm1_gain.shape[0]):
        h = _layer(h, positions, norm1_gain[l], w_in[l], nsa_q_gain[l], nsa_kc_gain[l], nsa_ks_gain[l], nsa_kw_gain[l],
                   cmp_pos[l], cmp_k_w1[l], cmp_k_w2[l], cmp_v_w1[l], cmp_v_w2[l],
                   mla_q_lora_gain[l], mla_w_uq[l], mla_kv_lora_gain[l], mla_w_ukv[l], mla_q_gain[l], mla_k_gain[l],
                   out_gain_a[l], out_gain_b[l], w_out[l], norm2_gain[l], peer_w_q[l], peer_sub_keys[l],
                   peer_u[l], peer_v[l])
    return h
```

```python
import functools

import jax
import jax.numpy as jnp
import numpy as np
from jax import lax
from jax.experimental import pallas as pl
from jax.experimental.pallas import tpu as pltpu

F32, BF16 = jnp.float32, jnp.bfloat16
EPS = 1e-6
NEG = -1e30
FORCE = 1e9
ROPE_THETA = 500000.0
LOG2E = 1.4426950408889634

D_MODEL = 1024
A_HEADS, A_GROUPS, A_DIM = 8, 2, 64
A_REP = A_HEADS // A_GROUPS
A_ROPE = A_DIM // 4
CMP_LEN, CMP_STRIDE, CMP_HIDDEN = 32, 16, 256
SLC_LEN, SLC_TOPK, WINDOW = 64, 16, 512
B_HEADS, Q_LORA, KV_LORA, B_NOPE, B_ROPE, B_V = 8, 256, 128, 64, 32, 64
B_QK = B_NOPE + B_ROPE
P_HEADS, N_KEYS, P_KEY_DIM, P_TOPK = 8, 128, 256, 16
N_EXPERTS = N_KEYS * N_KEYS

ROW_Q, ROW_KC, ROW_VC, ROW_KS, ROW_VS, ROW_KW, ROW_VW = 0, 512, 640, 768, 896, 1024, 1152
ROW_CQ, ROW_CKV, ROW_KR, ROW_GATE, PROJ_ROWS = 1280, 1536, 1664, 1696, 1728
GATE_ROWS = 32

VMEM_LIMIT = 56 * 1024 * 1024
NT_DIMS = (((1,), (1,)), ((), ()))


def _params(n_axes):
    return pltpu.CompilerParams(dimension_semantics=("arbitrary",) * n_axes, vmem_limit_bytes=VMEM_LIMIT)


def _dot(a, b):
    return jnp.dot(a, b, preferred_element_type=F32)


def _dot_nt(a, b):
    return lax.dot_general(a, b, NT_DIMS, preferred_element_type=F32)


def _row_sumsq(x):
    sq = x * x
    hi = sq.astype(BF16)
    lo = (sq - hi.astype(F32)).astype(BF16)
    ones = jnp.ones((8, x.shape[1]), BF16)
    return (_dot_nt(ones, hi) + _dot_nt(ones, lo))[0:1, :]


def _rms_rows(x):
    ss = jnp.sum(x * x, axis=0, keepdims=True)
    return x * lax.rsqrt(ss * (1.0 / x.shape[0]) + EPS)


def _rope_rows(y, cos, sin, off, half):
    x1, x2 = y[off:off + half], y[off + half:off + 2 * half]
    parts = [y[:off]] if off else []
    parts += [x1 * cos - x2 * sin, x2 * cos + x1 * sin]
    if off + 2 * half < y.shape[0]:
        parts.append(y[off + 2 * half:])
    return jnp.concatenate(parts, axis=0)


def _gelu_tanh(x):
    c = 0.7978845608028654
    half = 0.5 * x
    return half + half * jnp.tanh(x * (c + (c * 0.044715) * (x * x)))


def _in_proj_kernel(x_ref, g_ref, w_ref, o_ref):
    x = x_ref[...]
    xg = (x * g_ref[...]).astype(BF16)
    p = _dot_nt(w_ref[...], xg)
    rinv = lax.rsqrt(_row_sumsq(x) * (1.0 / x.shape[1]) + EPS)
    o_ref[...] = p * rinv


def _in_proj(x2, gain, w_t, tt):
    t, d = x2.shape
    rows = w_t.shape[0]
    return pl.pallas_call(
        _in_proj_kernel,
        grid=(t // tt,),
        in_specs=[pl.BlockSpec((tt, d), lambda i: (i, 0)),
                  pl.BlockSpec((1, d), lambda i: (0, 0)),
                  pl.BlockSpec((rows, d), lambda i: (0, 0))],
        out_specs=pl.BlockSpec((rows, tt), lambda i: (0, i)),
        out_shape=jax.ShapeDtypeStruct((rows, t), F32),
        compiler_params=_params(1),
        name="in_proj",
    )(x2, gain, w_t)


def _nsa_prep_kernel(q_ref, kc_ref, vc_ref, ks_ref, vs_ref, kw_ref, vw_ref, gt_ref, cos_ref, sin_ref,
                     qg_ref, ksg_ref, kwg_ref,
                     qo_ref, kco_ref, vco_ref, kso_ref, vso_ref, kwo_ref, vwo_ref, gto_ref, *, seq, tw):
    cos, sin = cos_ref[...], sin_ref[...]
    tt = cos.shape[1]
    nb = seq // SLC_LEN
    for h in range(A_HEADS):
        y = _rms_rows(q_ref[h * A_DIM:(h + 1) * A_DIM, :]) * qg_ref[...]
        y = _rope_rows(y, cos, sin, 0, A_ROPE // 2) * (A_DIM ** -0.5 * LOG2E)
        qo_ref[h * A_DIM:(h + 1) * A_DIM, :] = y.astype(BF16)
    kco_ref[...] = kc_ref[...].T
    vco_ref[...] = vc_ref[...].T
    zeros = jnp.zeros((A_DIM, tt), F32)
    tok = pl.program_id(0) * tt + lax.broadcasted_iota(jnp.int32, (tt, nb), 0)
    block_hot = jnp.where(lax.broadcasted_iota(jnp.int32, (tt, nb), 1) == (tok % seq) // SLC_LEN, 1.0, 0.0)
    for g in range(A_GROUPS):
        for src, gain, dst in ((ks_ref, ksg_ref, kso_ref), (kw_ref, kwg_ref, kwo_ref)):
            y = _rms_rows(src[g * A_DIM:(g + 1) * A_DIM, :]) * gain[...]
            y = _rope_rows(y, cos, sin, 0, A_ROPE // 2)
            k_tm = jnp.concatenate([y, zeros], axis=0).T
            if dst is kso_ref:
                k_tm = jnp.concatenate([k_tm, block_hot], axis=1)
            dst[g] = k_tm.astype(BF16)
        vso_ref[g, 0] = _with_ones(vs_ref[g * A_DIM:(g + 1) * A_DIM, :])
        v_win = _with_ones(vw_ref[g * A_DIM:(g + 1) * A_DIM, :])
        for c in range(tt // tw):
            vwo_ref[g, c] = v_win[:, c * tw:(c + 1) * tw]
    gto_ref[...] = 1.0 / (1.0 + jnp.exp(-gt_ref[...]))


def _nsa_prep(proj_t, cos_t, sin_t, q_gain, ks_gain, kw_gain, tt, seq, tw):
    t = proj_t.shape[1]
    gw = A_GROUPS * A_DIM
    kw = 128 + seq // SLC_LEN
    row_spec = lambda rows, start: pl.BlockSpec((rows, tt), lambda i: (start // rows, i))
    col = lambda n: pl.BlockSpec((n, 1), lambda i: (0, 0))
    return pl.pallas_call(
        functools.partial(_nsa_prep_kernel, seq=seq, tw=tw),
        grid=(t // tt,),
        in_specs=[row_spec(512, ROW_Q), row_spec(gw, ROW_KC), row_spec(gw, ROW_VC), row_spec(gw, ROW_KS),
                  row_spec(gw, ROW_VS), row_spec(gw, ROW_KW), row_spec(gw, ROW_VW), row_spec(GATE_ROWS, ROW_GATE),
                  pl.BlockSpec((A_ROPE // 2, tt), lambda i: (0, i)), pl.BlockSpec((A_ROPE // 2, tt), lambda i: (0, i)),
                  col(A_DIM), col(A_DIM), col(A_DIM)],
        out_specs=[pl.BlockSpec((512, tt), lambda i: (0, i)),
                   pl.BlockSpec((tt, gw), lambda i: (i, 0)),
                   pl.BlockSpec((tt, gw), lambda i: (i, 0)),
                   pl.BlockSpec((A_GROUPS, tt, kw), lambda i: (0, i, 0)),
                   pl.BlockSpec((A_GROUPS, 1, A_DIM + ONES_ROWS, tt), lambda i: (0, i, 0, 0)),
                   pl.BlockSpec((A_GROUPS, tt, 128), lambda i: (0, i, 0)),
                   pl.BlockSpec((A_GROUPS, tt // tw, A_DIM + ONES_ROWS, tw), lambda i: (0, i, 0, 0)),
                   pl.BlockSpec((GATE_ROWS, tt), lambda i: (0, i))],
        out_shape=[jax.ShapeDtypeStruct((512, t), BF16),
                   jax.ShapeDtypeStruct((t, gw), F32),
                   jax.ShapeDtypeStruct((t, gw), F32),
                   jax.ShapeDtypeStruct((A_GROUPS, t, kw), BF16),
                   jax.ShapeDtypeStruct((A_GROUPS, t // tt, A_DIM + ONES_ROWS, tt), BF16),
                   jax.ShapeDtypeStruct((A_GROUPS, t, 128), BF16),
                   jax.ShapeDtypeStruct((A_GROUPS, t // tw, A_DIM + ONES_ROWS, tw), BF16),
                   jax.ShapeDtypeStruct((GATE_ROWS, t), F32)],
        compiler_params=_params(1),
        name="nsa_prep",
    )(proj_t, proj_t, proj_t, proj_t, proj_t, proj_t, proj_t, proj_t, cos_t, sin_t, q_gain, ks_gain, kw_gain)


def _compress_kernel(kc_ref, vc_ref, w1ka_ref, w1kb_ref, w1va_ref, w1vb_ref, plo_ref, phi_ref,
                     w2k_ref, w2v_ref, kg_ref, cos_ref, sin_ref, ko_ref, vo_ref):
    nc = kc_ref.shape[1]
    zeros = jnp.zeros((A_DIM, nc), F32)
    for src, w1a, w1b, w2, is_k in ((kc_ref, w1ka_ref, w1kb_ref, w2k_ref, True),
                                    (vc_ref, w1va_ref, w1vb_ref, w2v_ref, False)):
        x = src[0]
        xlo = (x + plo_ref[...]).astype(BF16)
        xhi = (x + phi_ref[...]).astype(BF16)
        for g in range(A_GROUPS):
            first = _dot_nt(w1a[g], xlo)
            second = _dot_nt(w1b[g], xhi)
            hid = _gelu_tanh(first + pltpu.roll(second, nc - 1, axis=1)).astype(BF16)
            c = _dot(w2[...], hid)
            if is_k:
                y = _rope_rows(_rms_rows(c) * kg_ref[...], cos_ref[0], sin_ref[0], 0, A_ROPE // 2)
                ko_ref[0, g] = jnp.concatenate([y, zeros], axis=0).T.astype(BF16)
            else:
                vo_ref[0, g] = c.astype(BF16)


def _compress(kc_chunks, vc_chunks, w1ka, w1kb, w1va, w1vb, plo, phi, w2k_t, w2v_t, kc_gain, cos_c, sin_c):
    b, nc, cw = kc_chunks.shape
    full = lambda a: pl.BlockSpec(a.shape, lambda i: (0,) * a.ndim)
    return pl.pallas_call(
        _compress_kernel,
        grid=(b,),
        in_specs=[pl.BlockSpec((1, nc, cw), lambda i: (i, 0, 0)), pl.BlockSpec((1, nc, cw), lambda i: (i, 0, 0)),
                  full(w1ka), full(w1kb), full(w1va), full(w1vb), full(plo), full(phi), full(w2k_t), full(w2v_t),
                  full(kc_gain),
                  pl.BlockSpec((1, A_ROPE // 2, nc), lambda i: (i, 0, 0)),
                  pl.BlockSpec((1, A_ROPE // 2, nc), lambda i: (i, 0, 0))],
        out_specs=[pl.BlockSpec((1, A_GROUPS, nc, 128), lambda i: (i, 0, 0, 0)),
                   pl.BlockSpec((1, A_GROUPS, A_DIM, nc), lambda i: (i, 0, 0, 0))],
        out_shape=[jax.ShapeDtypeStruct((b, A_GROUPS, nc, 128), BF16),
                   jax.ShapeDtypeStruct((b, A_GROUPS, A_DIM, nc), BF16)],
        compiler_params=_params(1),
        name="nsa_compress",
    )(kc_chunks, vc_chunks, w1ka, w1kb, w1va, w1vb, plo, phi, w2k_t, w2v_t, kc_gain, cos_c, sin_c)


def _stack_heads(q, tq):
    qs = jnp.concatenate([q[r * A_DIM:(r + 1) * A_DIM, :] for r in range(A_REP)], axis=1)
    return jnp.concatenate([qs, jnp.zeros_like(qs)], axis=0)


def _nsa_cmp_kernel(q_ref, k_ref, v_ref, ov_ref, o_ref, sb_ref, *, tq):
    i = pl.program_id(2)
    n = A_REP * tq
    nc = k_ref.shape[2]
    nb = ov_ref.shape[0]
    qp = _stack_heads(q_ref[...], tq)
    s = _dot(k_ref[0, 0], qp)
    cmp_end = lax.broadcasted_iota(jnp.int32, (nc, n), 0) * CMP_STRIDE + (CMP_LEN - 1)
    tok = i * tq + (lax.broadcasted_iota(jnp.int32, (nc, n), 1) & (tq - 1))
    mask = cmp_end <= tok
    s = jnp.where(mask, s, NEG)
    m = jnp.max(s, axis=0, keepdims=True)
    p = jnp.where(mask, jnp.exp2(s - m), 0.0)
    l = jnp.sum(p, axis=0, keepdims=True)
    inv = jnp.where(l > 0.0, 1.0 / l, 0.0)
    pn = (p * inv).astype(BF16)
    oc = _dot(v_ref[0, 0], pn)
    for r in range(A_REP):
        o_ref[r * A_DIM:(r + 1) * A_DIM, :] = oc[:, r * tq:(r + 1) * tq]
    imp4 = _dot(ov_ref[...], pn)
    imp = imp4[:, 0:tq]
    for r in range(1, A_REP):
        imp = imp + imp4[:, r * tq:(r + 1) * tq]

    blk = lax.broadcasted_iota(jnp.int32, (nb, tq), 0)
    t = i * tq + lax.broadcasted_iota(jnp.int32, (nb, tq), 1)
    forced = (blk == (t >> 6)) | (blk == 0)
    v = jnp.where(forced, FORCE, jnp.where(blk * SLC_LEN <= t, imp, NEG))
    blk_f = blk.astype(F32)
    sel = jnp.zeros((nb, tq), F32)
    for _ in range(min(SLC_TOPK, nb)):
        mx = jnp.max(v, axis=0, keepdims=True)
        first = jnp.min(jnp.where(v == mx, blk_f, float(nb)), axis=0, keepdims=True)
        hit = blk_f == first
        sel = jnp.where(hit, 1.0, sel)
        v = jnp.where(hit, -jnp.inf, v)
    sb_ref[0] = jnp.where(sel > 0.0, 0.0, NEG)


def _nsa_cmp(q_t, kcmp, vcmp_t, ov_t, batch, seq, tq):
    nq = seq // tq
    nc = kcmp.shape[2]
    nb = ov_t.shape[0]
    t = q_t.shape[1]
    gr = A_REP * A_DIM
    return pl.pallas_call(
        functools.partial(_nsa_cmp_kernel, tq=tq),
        grid=(batch, A_GROUPS, nq),
        in_specs=[pl.BlockSpec((gr, tq), lambda b, g, i: (g, b * nq + i)),
                  pl.BlockSpec((1, 1, nc, 128), lambda b, g, i: (b, g, 0, 0)),
                  pl.BlockSpec((1, 1, A_DIM, nc), lambda b, g, i: (b, g, 0, 0)),
                  pl.BlockSpec((nb, nc), lambda b, g, i: (0, 0))],
        out_specs=[pl.BlockSpec((gr, tq), lambda b, g, i: (g, b * nq + i)),
                   pl.BlockSpec((1, nb, tq), lambda b, g, i: (g, 0, b * nq + i))],
        out_shape=[jax.ShapeDtypeStruct((A_HEADS * A_DIM, t), F32),
                   jax.ShapeDtypeStruct((A_GROUPS, nb, t), F32)],
        compiler_params=_params(3),
        name="nsa_cmp",
    )(q_t, kcmp, vcmp_t, ov_t)


ONES_ROWS = 16
FLASH_UNROLL = 4


def _with_ones(v):
    return jnp.concatenate([v, jnp.ones((ONES_ROWS, v.shape[1]), F32)], axis=0).astype(BF16)


def _flash_init(m_scr, acc_scr):
    m_scr[...] = jnp.full(m_scr.shape, -jnp.inf, F32)
    acc_scr[...] = jnp.zeros(acc_scr.shape, F32)


def _flash_update(s, v_t, m_scr, acc_scr):
    m_prev = m_scr[...]
    m_new = jnp.maximum(m_prev, jnp.max(s, axis=0, keepdims=True))
    alpha = jnp.exp2(m_prev - m_new)
    p = jnp.exp2(s - m_new)
    acc_scr[...] = alpha * acc_scr[...] + _dot(v_t, p.astype(BF16))
    m_scr[...] = m_new


def _flash_result(acc_scr, dv):
    acc = acc_scr[...]
    return acc[:dv] * (1.0 / acc[dv:dv + 1])


def _flash_causal(scores, values, mask_last, n_full, s_scr, m_scr, acc_scr):
    unroll = s_scr.shape[0]
    s_scr[0] = scores(0)

    def trip(t, carry):
        j = unroll * t
        for u in range(unroll):
            s_scr[(u + 1) % unroll] = scores(j + u + 1)
            _flash_update(s_scr[u], values(j + u), m_scr, acc_scr)
        return carry

    lax.fori_loop(0, n_full // unroll, trip, 0)
    first = (n_full // unroll) * unroll
    for rest in range(unroll):

        @pl.when(n_full - first == rest)
        def _(rest=rest):
            for u in range(rest + 1):
                if u < rest:
                    s_scr[u + 1] = scores(first + u + 1)
                    _flash_update(s_scr[u], values(first + u), m_scr, acc_scr)
                else:
                    _flash_update(mask_last(s_scr[u]), values(first + u), m_scr, acc_scr)


def _nsa_sel_kernel(q_ref, k_ref, v_ref, sb_ref, kw_ref, vw_ref, o_ref, ow_ref, qa_scr, s_scr, m_scr, acc_scr,
                    *, tq, tk):
    i = pl.program_id(2)
    n = A_REP * tq
    q = q_ref[...]
    qs = jnp.concatenate([q[r * A_DIM:(r + 1) * A_DIM, :] for r in range(A_REP)], axis=1)
    sb = sb_ref[0].astype(BF16)
    qa_scr[...] = jnp.concatenate([qs, jnp.zeros_like(qs), jnp.concatenate([sb] * A_REP, axis=1)], axis=0)
    _flash_init(m_scr, acc_scr)
    j_diag = (i * tq + tq - 1) // tk

    def scores(j):
        return _dot(k_ref[0, pl.ds(pl.multiple_of(j * tk, tk), tk), :], qa_scr[...])

    def causal(s):
        kpos = j_diag * tk + lax.broadcasted_iota(jnp.int32, (tk, n), 0)
        tok = i * tq + (lax.broadcasted_iota(jnp.int32, (tk, n), 1) & (tq - 1))
        return jnp.where(kpos <= tok, s, NEG)

    _flash_causal(scores, lambda j: v_ref[0, j], causal, j_diag, s_scr, m_scr, acc_scr)
    o = _flash_result(acc_scr, A_DIM)
    for r in range(A_REP):
        o_ref[r * A_DIM:(r + 1) * A_DIM, :] = o[:, r * tq:(r + 1) * tq]

    _flash_init(m_scr, acc_scr)
    n_back = WINDOW // tq
    for c in range(n_back + 1):
        kt = i - n_back + c

        @pl.when(kt >= 0)
        def _(kt=kt):
            s = _dot(kw_ref[0, pl.ds(pl.multiple_of(kt * tq, tq), tq), :], qa_scr[0:128, :])
            kpos = kt * tq + lax.broadcasted_iota(jnp.int32, (tq, n), 0)
            tok = i * tq + (lax.broadcasted_iota(jnp.int32, (tq, n), 1) & (tq - 1))
            s = jnp.where((kpos <= tok) & (kpos > tok - WINDOW), s, NEG)
            _flash_update(s, vw_ref[0, kt], m_scr, acc_scr)

    o = _flash_result(acc_scr, A_DIM)
    for r in range(A_REP):
        ow_ref[r * A_DIM:(r + 1) * A_DIM, :] = o[:, r * tq:(r + 1) * tq]


def _nsa_sel(q_t, k_aug, v_slc_tiles, selb, k_win, v_win_tiles, batch, seq, tq, tk):
    nq, nk = seq // tq, seq // tk
    nb = selb.shape[1]
    t = q_t.shape[1]
    gr = A_REP * A_DIM
    n = A_REP * tq
    kw = k_aug.shape[2]
    out = jax.ShapeDtypeStruct((A_HEADS * A_DIM, t), F32)
    return pl.pallas_call(
        functools.partial(_nsa_sel_kernel, tq=tq, tk=tk),
        grid=(batch, A_GROUPS, nq),
        in_specs=[pl.BlockSpec((gr, tq), lambda b, g, i: (g, b * nq + i)),
                  pl.BlockSpec((1, seq, kw), lambda b, g, i: (g, b, 0)),
                  pl.BlockSpec((1, nk, A_DIM + ONES_ROWS, tk), lambda b, g, i: (g, b, 0, 0)),
                  pl.BlockSpec((1, nb, tq), lambda b, g, i: (g, 0, b * nq + i)),
                  pl.BlockSpec((1, seq, 128), lambda b, g, i: (g, b, 0)),
                  pl.BlockSpec((1, nq, A_DIM + ONES_ROWS, tq), lambda b, g, i: (g, b, 0, 0))],
        out_specs=[pl.BlockSpec((gr, tq), lambda b, g, i: (g, b * nq + i)),
                   pl.BlockSpec((gr, tq), lambda b, g, i: (g, b * nq + i))],
        out_shape=[out, out],
        scratch_shapes=[pltpu.VMEM((kw, n), BF16), pltpu.VMEM((FLASH_UNROLL, tk, n), F32),
                        pltpu.VMEM((1, n), F32), pltpu.VMEM((A_DIM + ONES_ROWS, n), F32)],
        compiler_params=_params(3),
        name="nsa_sel",
    )(q_t, k_aug, v_slc_tiles, selb, k_win, v_win_tiles)


def _mla_prep_kernel(cq_ref, ckv_ref, kr_ref, cos_ref, sin_ref, qlg_ref, kvlg_ref, qg_ref, kg_ref, wuq_ref, wukv_ref,
                     qo_ref, ko_ref, vo_ref):
    cos, sin = cos_ref[...], sin_ref[...]
    tt = cos.shape[1]
    q_all = _dot(wuq_ref[...], (_rms_rows(cq_ref[...]) * qlg_ref[...]).astype(BF16))
    kv_all = _dot(wukv_ref[...], (_rms_rows(ckv_ref[...]) * kvlg_ref[...]).astype(BF16))
    kr = kr_ref[...]
    pad = jnp.zeros((128 - B_QK, tt), F32)
    for h in range(B_HEADS):
        y = _rms_rows(q_all[h * B_QK:(h + 1) * B_QK]) * qg_ref[...]
        y = _rope_rows(y, cos, sin, B_NOPE, B_ROPE // 2) * (B_QK ** -0.5 * LOG2E)
        qo_ref[h] = jnp.concatenate([y, pad], axis=0).astype(BF16)
        base = h * (B_NOPE + B_V)
        k = jnp.concatenate([kv_all[base:base + B_NOPE], kr], axis=0)
        y = _rope_rows(_rms_rows(k) * kg_ref[...], cos, sin, B_NOPE, B_ROPE // 2)
        ko_ref[h] = jnp.concatenate([y, pad], axis=0).T.astype(BF16)
        vo_ref[h, 0] = _with_ones(kv_all[base + B_NOPE:base + B_NOPE + B_V])


def _mla_prep(proj_t, cos_t, sin_t, q_lora_gain, kv_lora_gain, q_gain, k_gain, wuq_t, wukv_t, tt):
    t = proj_t.shape[1]
    row_spec = lambda rows, start: pl.BlockSpec((rows, tt), lambda i: (start // rows, i))
    full = lambda a: pl.BlockSpec(a.shape, lambda i: (0,) * a.ndim)
    return pl.pallas_call(
        _mla_prep_kernel,
        grid=(t // tt,),
        in_specs=[row_spec(Q_LORA, ROW_CQ), row_spec(KV_LORA, ROW_CKV), row_spec(B_ROPE, ROW_KR),
                  pl.BlockSpec((B_ROPE // 2, tt), lambda i: (0, i)), pl.BlockSpec((B_ROPE // 2, tt), lambda i: (0, i)),
                  full(q_lora_gain), full(kv_lora_gain), full(q_gain), full(k_gain), full(wuq_t), full(wukv_t)],
        out_specs=[pl.BlockSpec((B_HEADS, 128, tt), lambda i: (0, 0, i)),
                   pl.BlockSpec((B_HEADS, tt, 128), lambda i: (0, i, 0)),
                   pl.BlockSpec((B_HEADS, 1, B_V + ONES_ROWS, tt), lambda i: (0, i, 0, 0))],
        out_shape=[jax.ShapeDtypeStruct((B_HEADS, 128, t), BF16),
                   jax.ShapeDtypeStruct((B_HEADS, t, 128), BF16),
                   jax.ShapeDtypeStruct((B_HEADS, t // tt, B_V + ONES_ROWS, tt), BF16)],
        compiler_params=_params(1),
        name="mla_prep",
    )(proj_t, proj_t, proj_t, cos_t, sin_t, q_lora_gain, kv_lora_gain, q_gain, k_gain, wuq_t, wukv_t)


def _mla_attn_kernel(q_ref, k_ref, v_ref, o_ref, s_scr, m_scr, acc_scr, *, tq, tk):
    i = pl.program_id(2)
    _flash_init(m_scr, acc_scr)
    j_diag = (i * tq + tq - 1) // tk

    def scores(j):
        return _dot(k_ref[0, pl.ds(pl.multiple_of(j * tk, tk), tk), :], q_ref[0])

    def causal(s):
        kpos = j_diag * tk + lax.broadcasted_iota(jnp.int32, (tk, tq), 0)
        tok = i * tq + lax.broadcasted_iota(jnp.int32, (tk, tq), 1)
        return jnp.where(kpos <= tok, s, NEG)

    _flash_causal(scores, lambda j: v_ref[0, j], causal, j_diag, s_scr, m_scr, acc_scr)
    o_ref[...] = _flash_result(acc_scr, B_V)


def _mla_attn(q_m, k_m, v_m_tiles, batch, seq, tq, tk):
    nq, nk = seq // tq, seq // tk
    t = q_m.shape[2]
    return pl.pallas_call(
        functools.partial(_mla_attn_kernel, tq=tq, tk=tk),
        grid=(batch, B_HEADS, nq),
        in_specs=[pl.BlockSpec((1, 128, tq), lambda b, h, i: (h, 0, b * nq + i)),
                  pl.BlockSpec((1, seq, 128), lambda b, h, i: (h, b, 0)),
                  pl.BlockSpec((1, nk, B_V + ONES_ROWS, tk), lambda b, h, i: (h, b, 0, 0))],
        out_specs=pl.BlockSpec((B_V, tq), lambda b, h, i: (h, b * nq + i)),
        out_shape=jax.ShapeDtypeStruct((B_HEADS * B_V, t), F32),
        scratch_shapes=[pltpu.VMEM((FLASH_UNROLL, tk, tq), F32),
                        pltpu.VMEM((1, tq), F32), pltpu.VMEM((B_V + ONES_ROWS, tq), F32)],
        compiler_params=_params(3),
        name="mla_attn",
    )(q_m, k_m, v_m_tiles)


def _out_proj_kernel(oc_ref, os_ref, ow_ref, gt_ref, ob_ref, x_ref, ga_ref, gb_ref, w_ref, g2_ref, h_ref, hn_ref):
    heads = []
    for h in range(A_HEADS):
        rows = slice(h * A_DIM, (h + 1) * A_DIM)
        heads.append(gt_ref[3 * h:3 * h + 1, :] * oc_ref[rows, :] + gt_ref[3 * h + 1:3 * h + 2, :] * os_ref[rows, :]
                     + gt_ref[3 * h + 2:3 * h + 3, :] * ow_ref[rows, :])
    oa = _rms_rows(jnp.concatenate(heads, axis=0)) * ga_ref[...]
    ob = _rms_rows(ob_ref[...]) * gb_ref[...]
    cat = jnp.concatenate([oa, ob], axis=0).astype(BF16)
    hid = x_ref[...].T + _dot(w_ref[...], cat)
    h_ref[...] = hid
    hn_ref[...] = (_rms_rows(hid) * g2_ref[...]).astype(BF16)


def _out_proj(oc_t, os_t, ow_t, gates_t, ob_t, x2, gain_a, gain_b, w_out_t, gain2, tt):
    t, d = x2.shape
    aw = oc_t.shape[0]
    bw = ob_t.shape[0]
    tok = lambda rows: pl.BlockSpec((rows, tt), lambda i: (0, i))
    full = lambda a: pl.BlockSpec(a.shape, lambda i: (0,) * a.ndim)
    return pl.pallas_call(
        _out_proj_kernel,
        grid=(t // tt,),
        in_specs=[tok(aw), tok(aw), tok(aw), tok(GATE_ROWS), tok(bw), pl.BlockSpec((tt, d), lambda i: (i, 0)),
                  full(gain_a), full(gain_b), full(w_out_t), full(gain2)],
        out_specs=[tok(d), tok(d)],
        out_shape=[jax.ShapeDtypeStruct((d, t), F32), jax.ShapeDtypeStruct((d, t), BF16)],
        compiler_params=_params(1),
        name="out_proj",
    )(oc_t, os_t, ow_t, gates_t, ob_t, x2, gain_a, gain_b, w_out_t, gain2)


def _top_ranked(s):
    n, tp = s.shape
    row = lax.broadcasted_iota(jnp.int32, (n, tp), 0).astype(F32)
    slot = lax.broadcasted_iota(jnp.int32, (P_TOPK, tp), 0)

    def body(a, carry):
        v, rank, vals = carry
        mx = jnp.max(v, axis=0, keepdims=True)
        first = jnp.min(jnp.where(v == mx, row, float(n)), axis=0, keepdims=True)
        hit = row == first
        rank = jnp.where(hit, jnp.asarray(a, F32), rank)
        v = jnp.where(hit, -jnp.inf, v)
        vals = jnp.where(slot == a, mx, vals)
        return v, rank, vals

    _, rank, vals = lax.fori_loop(0, P_TOPK, body,
                                  (s, jnp.full((n, tp), float(P_TOPK), F32), jnp.zeros((P_TOPK, tp), F32)))
    return rank, vals


def _pair_counts(v1, v2):
    k = v1.shape[0]
    slot = lax.broadcasted_iota(jnp.int32, v1.shape, 0).astype(F32)
    top = v1[0:1] + v2[0:1]

    def body(_, carry):
        count, front, z = carry
        mx = jnp.max(front, axis=0, keepdims=True)
        a_star = jnp.min(jnp.where(front == mx, slot, float(k)), axis=0, keepdims=True)
        hit = slot == a_star
        count = count + jnp.where(hit, 1.0, 0.0)
        nxt = jnp.sum(jnp.where(hit, count, 0.0), axis=0, keepdims=True)
        v2_nxt = jnp.sum(jnp.where(slot == nxt, v2, 0.0), axis=0, keepdims=True)
        front = jnp.where(hit, jnp.where(nxt < float(k), v1 + v2_nxt, -jnp.inf), front)
        return count, front, z + jnp.exp(mx - top)

    count, _, z = lax.fori_loop(0, k, body, (jnp.zeros(v1.shape, F32), v1 + v2[0:1], jnp.zeros(top.shape, F32)))
    return count, z


REMOVED = -2.0 ** 126
LANES = 128


def _top_ranked_pair_fast(s1, s2):
    n, tp = s1.shape
    slot = lax.broadcasted_iota(jnp.int32, (P_TOPK, tp), 0)

    def body(a, carry):
        code = REMOVED * (1.0 + jnp.asarray(a, F32) * (1.0 / 32.0))
        out = []
        for key, vals in (carry[0:2], carry[2:4]):
            mx = jnp.max(key, axis=0, keepdims=True)
            key = jnp.where(key == mx, code, key)
            out += [key, jnp.where(slot == a, mx, vals)]
        return tuple(out)

    zeros = jnp.zeros((P_TOPK, tp), F32)
    k1, t1, k2, t2 = lax.fori_loop(0, P_TOPK, body, (s1, zeros, s2, zeros))
    res, bad = [], jnp.zeros((1, tp), F32)
    for key, vals in ((k1, t1), (k2, t2)):
        removed = key <= REMOVED
        rank = jnp.where(removed, (key * (1.0 / REMOVED) - 1.0) * 32.0, float(P_TOPK))
        n_removed = jnp.sum(jnp.where(removed, 1.0, 0.0), axis=0, keepdims=True)
        bad = jnp.maximum(bad, jnp.abs(n_removed - float(P_TOPK)))
        res += [rank, vals]
    return res[0], res[1], res[2], res[3], bad


def _pack_bf16_twice(x):
    hi = pltpu.bitcast(x.astype(BF16).astype(F32), jnp.int32)
    return hi | lax.shift_right_logical(hi, 16)


def _peer_route_kernel(hn_ref, wq_ref, keys_ref, cut_ref, g1_ref, r2_ref, g2_ref, q_scr, s_scr, rank_scr, vals_scr):
    tp = hn_ref.shape[1]
    half = P_KEY_DIM // 2
    q_scr[...] = _dot(wq_ref[...], hn_ref[...])
    for h in range(P_HEADS):
        q = _rms_rows(q_scr[h * P_KEY_DIM:(h + 1) * P_KEY_DIM, :]).astype(BF16)
        s_scr[0, h] = _dot(keys_ref[h, 0], q[:half])
        s_scr[1, h] = _dot(keys_ref[h, 1], q[half:])

    def first_level(rank_pair, flagged):
        for h in range(P_HEADS):
            for lt in range(tp // LANES):
                lanes = slice(lt * LANES, (lt + 1) * LANES)
                rank1, vals1, rank2, vals2, bad = rank_pair(s_scr[0, h, :, lanes], s_scr[1, h, :, lanes])
                flagged = jnp.maximum(flagged, bad)
                rank_scr[0, h, :, lanes] = rank1
                rank_scr[1, h, :, lanes] = rank2
                for a in range(P_TOPK):
                    vals_scr[0, a, h:h + 1, lanes] = vals1[a:a + 1]
                    vals_scr[1, a, h:h + 1, lanes] = vals2[a:a + 1]
        return flagged

    flagged = first_level(_top_ranked_pair_fast, jnp.zeros((1, LANES), F32))

    @pl.when(jnp.max(flagged) > 0.0)
    def _():
        first_level(lambda s1, s2: (*_top_ranked(s1), *_top_ranked(s2), jnp.zeros((1, LANES), F32)), flagged)

    count, z = _pair_counts(vals_scr[0], vals_scr[1])
    for h in range(P_HEADS):
        rank1 = rank_scr[0, h]
        cut = jnp.zeros_like(rank1)
        for a in range(P_TOPK):
            cut = jnp.where(rank1 == float(a), count[a, h:h + 1, :], cut)
        cut_ref[h] = _pack_bf16_twice(cut)
        g1_ref[h] = _pack_bf16_twice(jnp.exp(s_scr[0, h] - vals_scr[0, 0, h:h + 1, :]))
        r2_ref[h] = rank_scr[1, h].astype(BF16)
        g2_ref[h] = (jnp.exp(s_scr[1, h] - vals_scr[1, 0, h:h + 1, :]) * (1.0 / z[0, h:h + 1, :])).astype(BF16)


def _peer_route(hn_t, wq_t, sub_keys, tp):
    d, t = hn_t.shape
    words = jax.ShapeDtypeStruct((P_HEADS, N_KEYS, t), jnp.int32)
    halfs = jax.ShapeDtypeStruct((P_HEADS, N_KEYS, t), BF16)
    ospec = pl.BlockSpec((P_HEADS, N_KEYS, tp), lambda i: (0, 0, i))
    return pl.pallas_call(
        _peer_route_kernel,
        grid=(t // tp,),
        in_specs=[pl.BlockSpec((d, tp), lambda i: (0, i)),
                  pl.BlockSpec(wq_t.shape, lambda i: (0, 0)),
                  pl.BlockSpec(sub_keys.shape, lambda i: (0, 0, 0, 0))],
        out_specs=[ospec, ospec, ospec, ospec],
        out_shape=[words, words, halfs, halfs],
        scratch_shapes=[pltpu.VMEM((P_HEADS * P_KEY_DIM, tp), F32), pltpu.VMEM((2, P_HEADS, N_KEYS, tp), F32),
                        pltpu.VMEM((2, P_HEADS, N_KEYS, tp), F32), pltpu.VMEM((2, P_TOPK, P_HEADS, tp), F32)],
        compiler_params=_params(1),
        name="peer_route",
    )(hn_t, wq_t, sub_keys)


def _peer_ffn_kernel(hn_ref, h_ref, u_ref, v_ref, cut_ref, g1_ref, r2_ref, g2_ref, o_ref, acc_scr, *, te):
    e = pl.program_id(1)

    @pl.when(e == 0)
    def _():
        acc_scr[...] = jnp.zeros(acc_scr.shape, F32)

    hn = hn_ref[...]
    tt = hn.shape[1]

    def rows_bf16(word_row):
        tile = pltpu.bitcast(jnp.broadcast_to(word_row, (8, tt)), BF16)
        return jnp.concatenate([tile] * (N_KEYS // 16), axis=0)

    chunk = 2 * N_KEYS
    weights = []
    for c in range(te // chunk):
        a = _dot(u_ref[c * chunk:(c + 1) * chunk, :], hn)
        for k in range(chunk // N_KEYS):
            ii = c * (chunk // N_KEYS) + k
            gate = jnp.zeros((N_KEYS, tt), BF16)
            for h in range(P_HEADS):
                chosen = r2_ref[h] < rows_bf16(cut_ref[h, ii:ii + 1, :])
                gate = gate + jnp.where(chosen, g2_ref[h], jnp.zeros_like(gate)) * rows_bf16(g1_ref[h, ii:ii + 1, :])
            weights.append(gate * _gelu_tanh(a[k * N_KEYS:(k + 1) * N_KEYS, :]).astype(BF16))
    acc_scr[...] += _dot(v_ref[...], jnp.concatenate(weights, axis=0))

    @pl.when(e == pl.num_programs(1) - 1)
    def _():
        o_ref[...] = (h_ref[...] + acc_scr[...]).T


def _peer_ffn(hn_t, h_t, u_bf, v_t_bf, cut_k, g1_k, r2, g2, tt, te):
    d, t = hn_t.shape
    n_exp = u_bf.shape[0]
    kpe = te // N_KEYS
    return pl.pallas_call(
        functools.partial(_peer_ffn_kernel, te=te),
        grid=(t // tt, n_exp // te),
        in_specs=[pl.BlockSpec((d, tt), lambda i, e: (0, i)),
                  pl.BlockSpec((d, tt), lambda i, e: (0, i)),
                  pl.BlockSpec((te, d), lambda i, e: (e, 0)),
                  pl.BlockSpec((d, te), lambda i, e: (0, e)),
                  pl.BlockSpec((P_HEADS, kpe, tt), lambda i, e: (0, e, i)),
                  pl.BlockSpec((P_HEADS, kpe, tt), lambda i, e: (0, e, i)),
                  pl.BlockSpec((P_HEADS, N_KEYS, tt), lambda i, e: (0, 0, i)),
                  pl.BlockSpec((P_HEADS, N_KEYS, tt), lambda i, e: (0, 0, i))],
        out_specs=pl.BlockSpec((tt, d), lambda i, e: (i, 0)),
        out_shape=jax.ShapeDtypeStruct((t, d), F32),
        scratch_shapes=[pltpu.VMEM((d, tt), F32)],
        compiler_params=_params(2),
        name="peer_ffn",
    )(hn_t, h_t, u_bf, v_t_bf, cut_k, g1_k, r2, g2)


def _rope_tables_t(pos_flat, rot_dim):
    inv_freq = ROPE_THETA ** (-jnp.arange(0, rot_dim, 2, dtype=F32) / rot_dim)
    ang = pos_flat.astype(F32)[None, :] * inv_freq[:, None]
    return jnp.cos(ang), jnp.sin(ang)


def _expand_cmp_w1(w1):
    w = w1.reshape(CMP_LEN, A_DIM, CMP_HIDDEN)
    out = []
    for part in (w[:CMP_STRIDE], w[CMP_STRIDE:]):
        z = jnp.zeros_like(part)
        both = jnp.stack([jnp.concatenate([part, z], axis=1), jnp.concatenate([z, part], axis=1)])
        out.append(both.reshape(A_GROUPS, CMP_STRIDE * A_GROUPS * A_DIM, CMP_HIDDEN).transpose(0, 2, 1).astype(BF16))
    return out


TOKEN_TILE = 512
NSA_Q_TILE = 256
NSA_SEL_K_TILE = 512
MLA_TILE = 512
PEER_ROUTE_TILE = 512
PEER_EXPERT_TILE = 2048


def _col(v):
    return v.reshape(-1, 1).astype(F32)


def _mixers(x, positions, norm1_gain, w_in, nsa_q_gain, nsa_kc_gain, nsa_ks_gain, nsa_kw_gain,
            cmp_pos, cmp_k_w1, cmp_k_w2, cmp_v_w1, cmp_v_w2,
            mla_q_lora_gain, mla_w_uq, mla_kv_lora_gain, mla_w_ukv, mla_q_gain, mla_k_gain):
    batch, seq, d = x.shape
    t = batch * seq
    tt = TOKEN_TILE
    tq_nsa = NSA_Q_TILE
    tk_sel = NSA_SEL_K_TILE
    tq_mla = tk_mla = MLA_TILE
    assert tk_sel == tt and tk_mla == tt
    assert d == D_MODEL and seq % 512 == 0 and seq // SLC_LEN >= SLC_TOPK and WINDOW % tq_nsa == 0
    col = _col
    x2 = x.reshape(t, d)
    pos = positions.reshape(t)

    w_in_t = w_in.T
    gate_lo = sum((512, 128, 128, 128, 128, 128, 128))
    gate_hi = gate_lo + 3 * A_HEADS
    w_in_t = jnp.concatenate([w_in_t[:gate_lo], w_in_t[gate_hi:], w_in_t[gate_lo:gate_hi],
                              jnp.zeros((PROJ_ROWS - w_in_t.shape[0], d), F32)], axis=0).astype(BF16)
    proj_t = _in_proj(x2, norm1_gain.reshape(1, d), w_in_t, tt)

    cos_a, sin_a = _rope_tables_t(pos, A_ROPE)
    q_t, kc_tm, vc_tm, k_slc, v_slc_t, k_win, v_win_t, gates_t = _nsa_prep(
        proj_t, cos_a, sin_a, col(nsa_q_gain), col(nsa_ks_gain), col(nsa_kw_gain), tt, seq, tq_nsa)

    nc = seq // CMP_STRIDE
    chunk_w = CMP_STRIDE * A_GROUPS * A_DIM
    w1ka, w1kb = _expand_cmp_w1(cmp_k_w1)
    w1va, w1vb = _expand_cmp_w1(cmp_v_w1)
    pos_rows = lambda p: jnp.broadcast_to(p[:, None, :], (CMP_STRIDE, A_GROUPS, A_DIM)).reshape(1, chunk_w)
    cmp_end = jnp.minimum(jnp.arange(nc) * CMP_STRIDE + CMP_LEN - 1, seq - 1)
    cos_c, sin_c = _rope_tables_t(positions[:, cmp_end].reshape(-1), A_ROPE)
    to_b = lambda a: a.reshape(A_ROPE // 2, batch, nc).transpose(1, 0, 2)
    kcmp, vcmp_t = _compress(kc_tm.reshape(batch, nc, chunk_w), vc_tm.reshape(batch, nc, chunk_w),
                             w1ka, w1kb, w1va, w1vb, pos_rows(cmp_pos[:CMP_STRIDE]), pos_rows(cmp_pos[CMP_STRIDE:]),
                             cmp_k_w2.T.astype(BF16), cmp_v_w2.T.astype(BF16), col(nsa_kc_gain), to_b(cos_c), to_b(sin_c))

    n_cmp = (seq - CMP_LEN) // CMP_STRIDE + 1
    nb = seq // SLC_LEN
    c_start = np.arange(nc)[None, :] * CMP_STRIDE
    s_start = np.arange(nb)[:, None] * SLC_LEN
    ov = (c_start < s_start + SLC_LEN) & (c_start + CMP_LEN - 1 >= s_start) & (np.arange(nc)[None, :] < n_cmp)
    ov_t = jnp.asarray(ov.astype(np.float32)).astype(BF16)

    oc_t, selb = _nsa_cmp(q_t, kcmp, vcmp_t, ov_t, batch, seq, tq_nsa)
    os_t, ow_t = _nsa_sel(q_t, k_slc, v_slc_t, selb, k_win, v_win_t, batch, seq, tq_nsa, tk_sel)

    cos_b, sin_b = _rope_tables_t(pos, B_ROPE)
    q_m, k_m, v_m_t = _mla_prep(proj_t, cos_b, sin_b, col(mla_q_lora_gain), col(mla_kv_lora_gain),
                                col(mla_q_gain), col(mla_k_gain), mla_w_uq.T.astype(BF16), mla_w_ukv.T.astype(BF16), tt)
    ob_t = _mla_attn(q_m, k_m, v_m_t, batch, seq, tq_mla, tk_mla)
    return oc_t, os_t, ow_t, gates_t, ob_t


def _peer(hn_t, h_t, peer_w_q, peer_sub_keys, peer_u, peer_v):
    cut, g1, r2, g2 = _peer_route(hn_t, peer_w_q.T.astype(BF16), peer_sub_keys.astype(BF16), PEER_ROUTE_TILE)
    return _peer_ffn(hn_t, h_t, peer_u.astype(BF16), peer_v.T.astype(BF16),
                     cut, g1, r2, g2, TOKEN_TILE, PEER_EXPERT_TILE)


def _layer(x, positions, norm1_gain, w_in, nsa_q_gain, nsa_kc_gain, nsa_ks_gain, nsa_kw_gain,
           cmp_pos, cmp_k_w1, cmp_k_w2, cmp_v_w1, cmp_v_w2,
           mla_q_lora_gain, mla_w_uq, mla_kv_lora_gain, mla_w_ukv, mla_q_gain, mla_k_gain,
           out_gain_a, out_gain_b, w_out, norm2_gain, peer_w_q, peer_sub_keys, peer_u, peer_v):
    batch, seq, d = x.shape
    oc_t, os_t, ow_t, gates_t, ob_t = _mixers(
        x, positions, norm1_gain, w_in, nsa_q_gain, nsa_kc_gain, nsa_ks_gain, nsa_kw_gain,
        cmp_pos, cmp_k_w1, cmp_k_w2, cmp_v_w1, cmp_v_w2,
        mla_q_lora_gain, mla_w_uq, mla_kv_lora_gain, mla_w_ukv, mla_q_gain, mla_k_gain)
    h_t, hn_t = _out_proj(oc_t, os_t, ow_t, gates_t, ob_t, x.reshape(batch * seq, d), _col(out_gain_a), _col(out_gain_b),
                          w_out.T.astype(BF16), _col(norm2_gain), TOKEN_TILE)
    return _peer(hn_t, h_t, peer_w_q, peer_sub_keys, peer_u, peer_v).reshape(batch, seq, d)


def kernel(x, positions, norm1_gain, w_in, nsa_q_gain, nsa_kc_gain, nsa_ks_gain, nsa_kw_gain, cmp_pos, cmp_k_w1, cmp_k_w2, cmp_v_w1, cmp_v_w2, mla_q_lora_gain, mla_w_uq, mla_kv_lora_gain, mla_w_ukv, mla_q_gain, mla_k_gain, out_gain_a, out_gain_b, w_out, norm2_gain, peer_w_q, peer_sub_keys, peer_u, peer_v):
    h = x
    for l in range(norm1_gain.shape[0]):
        h = _layer(h, positions, norm1_gain[l], w_in[l], nsa_q_gain[l], nsa_kc_gain[l], nsa_ks_gain[l], nsa_kw_gain[l],
                   cmp_pos[l], cmp_k_w1[l], cmp_k_w2[l], cmp_v_w1[l], cmp_v_w2[l],
                   mla_q_lora_gain[l], mla_w_uq[l], mla_kv_lora_gain[l], mla_w_ukv[l], mla_q_gain[l], mla_k_gain[l],
                   out_gain_a[l], out_gain_b[l], w_out[l], norm2_gain[l], peer_w_q[l], peer_sub_keys[l],
                   peer_u[l], peer_v[l])
    return h
```

```python
import functools

import jax
import jax.numpy as jnp
import numpy as np
from jax import lax
from jax.experimental import pallas as pl
from jax.experimental.pallas import tpu as pltpu

F32, BF16 = jnp.float32, jnp.bfloat16
EPS = 1e-6
NEG = -1e30
FORCE = 1e9
ROPE_THETA = 500000.0
LOG2E = 1.4426950408889634

D_MODEL = 1024
A_HEADS, A_GROUPS, A_DIM = 8, 2, 64
A_REP = A_HEADS // A_GROUPS
A_ROPE = A_DIM // 4
CMP_LEN, CMP_STRIDE, CMP_HIDDEN = 32, 16, 256
SLC_LEN, SLC_TOPK, WINDOW = 64, 16, 512
B_HEADS, Q_LORA, KV_LORA, B_NOPE, B_ROPE, B_V = 8, 256, 128, 64, 32, 64
B_QK = B_NOPE + B_ROPE
P_HEADS, N_KEYS, P_KEY_DIM, P_TOPK = 8, 128, 256, 16
N_EXPERTS = N_KEYS * N_KEYS

ROW_Q, ROW_KC, ROW_VC, ROW_KS, ROW_VS, ROW_KW, ROW_VW = 0, 512, 640, 768, 896, 1024, 1152
ROW_CQ, ROW_CKV, ROW_KR, ROW_GATE, PROJ_ROWS = 1280, 1536, 1664, 1696, 1728
GATE_ROWS = 32

VMEM_LIMIT = 56 * 1024 * 1024
NT_DIMS = (((1,), (1,)), ((), ()))


def _params(n_axes):
    return pltpu.CompilerParams(dimension_semantics=("arbitrary",) * n_axes, vmem_limit_bytes=VMEM_LIMIT)


def _dot(a, b):
    return jnp.dot(a, b, preferred_element_type=F32)


def _dot_nt(a, b):
    return lax.dot_general(a, b, NT_DIMS, preferred_element_type=F32)


def _row_sumsq(x):
    sq = x * x
    hi = sq.astype(BF16)
    lo = (sq - hi.astype(F32)).astype(BF16)
    ones = jnp.ones((8, x.shape[1]), BF16)
    return (_dot_nt(ones, hi) + _dot_nt(ones, lo))[0:1, :]


def _rms_rows(x):
    ss = jnp.sum(x * x, axis=0, keepdims=True)
    return x * lax.rsqrt(ss * (1.0 / x.shape[0]) + EPS)


def _rope_rows(y, cos, sin, off, half):
    x1, x2 = y[off:off + half], y[off + half:off + 2 * half]
    parts = [y[:off]] if off else []
    parts += [x1 * cos - x2 * sin, x2 * cos + x1 * sin]
    if off + 2 * half < y.shape[0]:
        parts.append(y[off + 2 * half:])
    return jnp.concatenate(parts, axis=0)


def _gelu_tanh(x):
    c = 0.7978845608028654
    half = 0.5 * x
    return half + half * jnp.tanh(x * (c + (c * 0.044715) * (x * x)))


def _in_proj_kernel(x_ref, g_ref, w_ref, cos_a_ref, sin_a_ref, qg_ref, ksg_ref, kwg_ref,
                    cos_b_ref, sin_b_ref, qlg_ref, kvlg_ref, mqg_ref, mkg_ref, wuq_ref, wukv_ref, *out_refs, seq, tw):
    x = x_ref[...]
    xg = (x * g_ref[...]).astype(BF16)
    p = _dot_nt(w_ref[...], xg)
    p = p * lax.rsqrt(_row_sumsq(x) * (1.0 / x.shape[1]) + EPS)
    gw = A_GROUPS * A_DIM
    rows = lambda start, n: p[start:start + n]
    _nsa_prep_kernel(rows(ROW_Q, A_HEADS * A_DIM), rows(ROW_KC, gw), rows(ROW_VC, gw), rows(ROW_KS, gw), rows(ROW_VS, gw),
                     rows(ROW_KW, gw), rows(ROW_VW, gw), rows(ROW_GATE, GATE_ROWS), cos_a_ref, sin_a_ref,
                     qg_ref, ksg_ref, kwg_ref, *out_refs[:8], seq=seq, tw=tw)
    _mla_prep_kernel(rows(ROW_CQ, Q_LORA), rows(ROW_CKV, KV_LORA), rows(ROW_KR, B_ROPE), cos_b_ref, sin_b_ref,
                     qlg_ref, kvlg_ref, mqg_ref, mkg_ref, wuq_ref, wukv_ref, *out_refs[8:])


def _in_proj(x2, gain, w_t, cos_a, sin_a, q_gain, ks_gain, kw_gain, cos_b, sin_b, q_lora_gain, kv_lora_gain,
             mq_gain, mk_gain, wuq_t, wukv_t, tt, seq, tw):
    t, d = x2.shape
    gw = A_GROUPS * A_DIM
    kw = 128 + seq // SLC_LEN
    va, vb = A_DIM + ONES_ROWS, B_V + ONES_ROWS
    full = lambda a: pl.BlockSpec(a.shape, lambda i: (0,) * a.ndim)
    lanes = lambda n: pl.BlockSpec((n, tt), lambda i: (0, i))
    return pl.pallas_call(
        functools.partial(_in_proj_kernel, seq=seq, tw=tw),
        grid=(t // tt,),
        in_specs=[pl.BlockSpec((tt, d), lambda i: (i, 0)), full(gain), full(w_t),
                  lanes(A_ROPE // 2), lanes(A_ROPE // 2), full(q_gain), full(ks_gain), full(kw_gain),
                  lanes(B_ROPE // 2), lanes(B_ROPE // 2), full(q_lora_gain), full(kv_lora_gain), full(mq_gain),
                  full(mk_gain), full(wuq_t), full(wukv_t)],
        out_specs=[lanes(A_HEADS * A_DIM),
                   pl.BlockSpec((tt, gw), lambda i: (i, 0)),
                   pl.BlockSpec((tt, gw), lambda i: (i, 0)),
                   pl.BlockSpec((A_GROUPS, tt, kw), lambda i: (0, i, 0)),
                   pl.BlockSpec((A_GROUPS, 1, va, tt), lambda i: (0, i, 0, 0)),
                   pl.BlockSpec((A_GROUPS, tt, 128), lambda i: (0, i, 0)),
                   pl.BlockSpec((A_GROUPS, tt // tw, va, tw), lambda i: (0, i, 0, 0)),
                   lanes(GATE_ROWS),
                   pl.BlockSpec((B_HEADS, 128, tt), lambda i: (0, 0, i)),
                   pl.BlockSpec((B_HEADS, tt, 128), lambda i: (0, i, 0)),
                   pl.BlockSpec((B_HEADS, 1, vb, tt), lambda i: (0, i, 0, 0))],
        out_shape=[jax.ShapeDtypeStruct((A_HEADS * A_DIM, t), BF16),
                   jax.ShapeDtypeStruct((t, gw), F32),
                   jax.ShapeDtypeStruct((t, gw), F32),
                   jax.ShapeDtypeStruct((A_GROUPS, t, kw), BF16),
                   jax.ShapeDtypeStruct((A_GROUPS, t // tt, va, tt), BF16),
                   jax.ShapeDtypeStruct((A_GROUPS, t, 128), BF16),
                   jax.ShapeDtypeStruct((A_GROUPS, t // tw, va, tw), BF16),
                   jax.ShapeDtypeStruct((GATE_ROWS, t), F32),
                   jax.ShapeDtypeStruct((B_HEADS, 128, t), BF16),
                   jax.ShapeDtypeStruct((B_HEADS, t, 128), BF16),
                   jax.ShapeDtypeStruct((B_HEADS, t // tt, vb, tt), BF16)],
        compiler_params=_params(1),
        name="in_proj",
    )(x2, gain, w_t, cos_a, sin_a, q_gain, ks_gain, kw_gain, cos_b, sin_b, q_lora_gain, kv_lora_gain,
      mq_gain, mk_gain, wuq_t, wukv_t)


def _nsa_prep_kernel(q_ref, kc_ref, vc_ref, ks_ref, vs_ref, kw_ref, vw_ref, gt_ref, cos_ref, sin_ref,
                     qg_ref, ksg_ref, kwg_ref,
                     qo_ref, kco_ref, vco_ref, kso_ref, vso_ref, kwo_ref, vwo_ref, gto_ref, *, seq, tw):
    cos, sin = cos_ref[...], sin_ref[...]
    tt = cos.shape[1]
    nb = seq // SLC_LEN
    for h in range(A_HEADS):
        y = _rms_rows(q_ref[h * A_DIM:(h + 1) * A_DIM, :]) * qg_ref[...]
        y = _rope_rows(y, cos, sin, 0, A_ROPE // 2) * (A_DIM ** -0.5 * LOG2E)
        qo_ref[h * A_DIM:(h + 1) * A_DIM, :] = y.astype(BF16)
    kco_ref[...] = kc_ref[...].T
    vco_ref[...] = vc_ref[...].T
    zeros = jnp.zeros((A_DIM, tt), F32)
    tok = pl.program_id(0) * tt + lax.broadcasted_iota(jnp.int32, (tt, nb), 0)
    block_hot = jnp.where(lax.broadcasted_iota(jnp.int32, (tt, nb), 1) == (tok % seq) // SLC_LEN, 1.0, 0.0)
    for g in range(A_GROUPS):
        for src, gain, dst in ((ks_ref, ksg_ref, kso_ref), (kw_ref, kwg_ref, kwo_ref)):
            y = _rms_rows(src[g * A_DIM:(g + 1) * A_DIM, :]) * gain[...]
            y = _rope_rows(y, cos, sin, 0, A_ROPE // 2)
            k_tm = jnp.concatenate([y, zeros], axis=0).T
            if dst is kso_ref:
                k_tm = jnp.concatenate([k_tm, block_hot], axis=1)
            dst[g] = k_tm.astype(BF16)
        vso_ref[g, 0] = _with_ones(vs_ref[g * A_DIM:(g + 1) * A_DIM, :])
        v_win = _with_ones(vw_ref[g * A_DIM:(g + 1) * A_DIM, :])
        for c in range(tt // tw):
            vwo_ref[g, c] = v_win[:, c * tw:(c + 1) * tw]
    gto_ref[...] = 1.0 / (1.0 + jnp.exp(-gt_ref[...]))


def _compress_kernel(kc_ref, vc_ref, w1ka_ref, w1kb_ref, w1va_ref, w1vb_ref, plo_ref, phi_ref,
                     w2k_ref, w2v_ref, kg_ref, cos_ref, sin_ref, ko_ref, vo_ref):
    nc = kc_ref.shape[1]
    zeros = jnp.zeros((A_DIM, nc), F32)
    for src, w1a, w1b, w2, is_k in ((kc_ref, w1ka_ref, w1kb_ref, w2k_ref, True),
                                    (vc_ref, w1va_ref, w1vb_ref, w2v_ref, False)):
        x = src[0]
        xlo = (x + plo_ref[...]).astype(BF16)
        xhi = (x + phi_ref[...]).astype(BF16)
        for g in range(A_GROUPS):
            first = _dot_nt(w1a[g], xlo)
            second = _dot_nt(w1b[g], xhi)
            hid = _gelu_tanh(first + pltpu.roll(second, nc - 1, axis=1)).astype(BF16)
            c = _dot(w2[...], hid)
            if is_k:
                y = _rope_rows(_rms_rows(c) * kg_ref[...], cos_ref[0], sin_ref[0], 0, A_ROPE // 2)
                ko_ref[0, g] = jnp.concatenate([y, zeros], axis=0).T.astype(BF16)
            else:
                vo_ref[0, g] = c.astype(BF16)


def _compress(kc_chunks, vc_chunks, w1ka, w1kb, w1va, w1vb, plo, phi, w2k_t, w2v_t, kc_gain, cos_c, sin_c):
    b, nc, cw = kc_chunks.shape
    full = lambda a: pl.BlockSpec(a.shape, lambda i: (0,) * a.ndim)
    return pl.pallas_call(
        _compress_kernel,
        grid=(b,),
        in_specs=[pl.BlockSpec((1, nc, cw), lambda i: (i, 0, 0)), pl.BlockSpec((1, nc, cw), lambda i: (i, 0, 0)),
                  full(w1ka), full(w1kb), full(w1va), full(w1vb), full(plo), full(phi), full(w2k_t), full(w2v_t),
                  full(kc_gain),
                  pl.BlockSpec((1, A_ROPE // 2, nc), lambda i: (i, 0, 0)),
                  pl.BlockSpec((1, A_ROPE // 2, nc), lambda i: (i, 0, 0))],
        out_specs=[pl.BlockSpec((1, A_GROUPS, nc, 128), lambda i: (i, 0, 0, 0)),
                   pl.BlockSpec((1, A_GROUPS, A_DIM, nc), lambda i: (i, 0, 0, 0))],
        out_shape=[jax.ShapeDtypeStruct((b, A_GROUPS, nc, 128), BF16),
                   jax.ShapeDtypeStruct((b, A_GROUPS, A_DIM, nc), BF16)],
        compiler_params=_params(1),
        name="nsa_compress",
    )(kc_chunks, vc_chunks, w1ka, w1kb, w1va, w1vb, plo, phi, w2k_t, w2v_t, kc_gain, cos_c, sin_c)


def _stack_heads(q, tq):
    qs = jnp.concatenate([q[r * A_DIM:(r + 1) * A_DIM, :] for r in range(A_REP)], axis=1)
    return jnp.concatenate([qs, jnp.zeros_like(qs)], axis=0)


def _nsa_cmp_kernel(q_ref, k_ref, v_ref, ov_ref, o_ref, sb_ref, *, tq):
    i = pl.program_id(2)
    n = A_REP * tq
    nc = k_ref.shape[2]
    nb = ov_ref.shape[0]
    qp = _stack_heads(q_ref[...], tq)
    s = _dot(k_ref[0, 0], qp)
    cmp_end = lax.broadcasted_iota(jnp.int32, (nc, n), 0) * CMP_STRIDE + (CMP_LEN - 1)
    tok = i * tq + (lax.broadcasted_iota(jnp.int32, (nc, n), 1) & (tq - 1))
    mask = cmp_end <= tok
    s = jnp.where(mask, s, NEG)
    m = jnp.max(s, axis=0, keepdims=True)
    p = jnp.where(mask, jnp.exp2(s - m), 0.0)
    l = jnp.sum(p, axis=0, keepdims=True)
    inv = jnp.where(l > 0.0, 1.0 / l, 0.0)
    pn = (p * inv).astype(BF16)
    oc = _dot(v_ref[0, 0], pn)
    for r in range(A_REP):
        o_ref[r * A_DIM:(r + 1) * A_DIM, :] = oc[:, r * tq:(r + 1) * tq]
    imp4 = _dot(ov_ref[...], pn)
    imp = imp4[:, 0:tq]
    for r in range(1, A_REP):
        imp = imp + imp4[:, r * tq:(r + 1) * tq]

    blk = lax.broadcasted_iota(jnp.int32, (nb, tq), 0)
    t = i * tq + lax.broadcasted_iota(jnp.int32, (nb, tq), 1)
    forced = (blk == (t >> 6)) | (blk == 0)
    v = jnp.where(forced, FORCE, jnp.where(blk * SLC_LEN <= t, imp, NEG))
    blk_f = blk.astype(F32)
    sel = jnp.zeros((nb, tq), F32)
    for _ in range(min(SLC_TOPK, nb)):
        mx = jnp.max(v, axis=0, keepdims=True)
        first = jnp.min(jnp.where(v == mx, blk_f, float(nb)), axis=0, keepdims=True)
        hit = blk_f == first
        sel = jnp.where(hit, 1.0, sel)
        v = jnp.where(hit, -jnp.inf, v)
    sb_ref[0] = jnp.where(sel > 0.0, 0.0, NEG)


def _nsa_cmp(q_t, kcmp, vcmp_t, ov_t, batch, seq, tq):
    nq = seq // tq
    nc = kcmp.shape[2]
    nb = ov_t.shape[0]
    t = q_t.shape[1]
    gr = A_REP * A_DIM
    return pl.pallas_call(
        functools.partial(_nsa_cmp_kernel, tq=tq),
        grid=(batch, A_GROUPS, nq),
        in_specs=[pl.BlockSpec((gr, tq), lambda b, g, i: (g, b * nq + i)),
                  pl.BlockSpec((1, 1, nc, 128), lambda b, g, i: (b, g, 0, 0)),
                  pl.BlockSpec((1, 1, A_DIM, nc), lambda b, g, i: (b, g, 0, 0)),
                  pl.BlockSpec((nb, nc), lambda b, g, i: (0, 0))],
        out_specs=[pl.BlockSpec((gr, tq), lambda b, g, i: (g, b * nq + i)),
                   pl.BlockSpec((1, nb, tq), lambda b, g, i: (g, 0, b * nq + i))],
        out_shape=[jax.ShapeDtypeStruct((A_HEADS * A_DIM, t), F32),
                   jax.ShapeDtypeStruct((A_GROUPS, nb, t), F32)],
        compiler_params=_params(3),
        name="nsa_cmp",
    )(q_t, kcmp, vcmp_t, ov_t)


ONES_ROWS = 16
FLASH_UNROLL = 4


def _with_ones(v):
    return jnp.concatenate([v, jnp.ones((ONES_ROWS, v.shape[1]), F32)], axis=0).astype(BF16)


def _flash_init(m_scr, acc_scr):
    m_scr[...] = jnp.full(m_scr.shape, -jnp.inf, F32)
    acc_scr[...] = jnp.zeros(acc_scr.shape, F32)


def _flash_update(s, v_t, m_scr, acc_scr):
    m_prev = m_scr[...]
    m_new = jnp.maximum(m_prev, jnp.max(s, axis=0, keepdims=True))
    alpha = jnp.exp2(m_prev - m_new)
    p = jnp.exp2(s - m_new)
    acc_scr[...] = alpha * acc_scr[...] + _dot(v_t, p.astype(BF16))
    m_scr[...] = m_new


def _flash_result(acc_scr, dv):
    acc = acc_scr[...]
    return acc[:dv] * (1.0 / acc[dv:dv + 1])


def _flash_causal(scores, values, mask, n_full, n_masked, s_scr, m_scr, acc_scr):
    unroll = s_scr.shape[0]
    s_scr[0] = scores(0)

    def trip(t, carry):
        j = unroll * t
        for u in range(unroll):
            s_scr[(u + 1) % unroll] = scores(j + u + 1)
            _flash_update(s_scr[u], values(j + u), m_scr, acc_scr)
        return carry

    lax.fori_loop(0, n_full // unroll, trip, 0)
    first = (n_full // unroll) * unroll
    for rest in range(unroll):

        @pl.when(n_full - first == rest)
        def _(rest=rest):
            for u in range(rest + n_masked):
                if u + 1 < rest + n_masked:
                    s_scr[(u + 1) % unroll] = scores(first + u + 1)
                s = s_scr[u % unroll]
                _flash_update(s if u < rest else mask(s, first + u), values(first + u), m_scr, acc_scr)


def _nsa_sel_kernel(q_ref, k_ref, v_ref, sb_ref, kw_ref, vw_ref, o_ref, ow_ref, qa_scr, s_scr, m_scr, acc_scr,
                    *, tq, tk):
    i = pl.program_id(2)
    n = A_REP * tq
    q = q_ref[...]
    qs = jnp.concatenate([q[r * A_DIM:(r + 1) * A_DIM, :] for r in range(A_REP)], axis=1)
    sb = sb_ref[0].astype(BF16)
    qa_scr[...] = jnp.concatenate([qs, jnp.zeros_like(qs), jnp.concatenate([sb] * A_REP, axis=1)], axis=0)
    _flash_init(m_scr, acc_scr)

    def scores(j):
        return _dot(k_ref[0, pl.ds(pl.multiple_of(j * tk, tk), tk), :], qa_scr[...])

    def causal(s, j):
        kpos = j * tk + lax.broadcasted_iota(jnp.int32, (tk, n), 0)
        tok = i * tq + (lax.broadcasted_iota(jnp.int32, (tk, n), 1) & (tq - 1))
        return jnp.where(kpos <= tok, s, NEG)

    _flash_causal(scores, lambda j: v_ref[0, j], causal, (i * tq) // tk, max(1, tq // tk), s_scr, m_scr, acc_scr)
    o = _flash_result(acc_scr, A_DIM)
    for r in range(A_REP):
        o_ref[r * A_DIM:(r + 1) * A_DIM, :] = o[:, r * tq:(r + 1) * tq]

    _flash_init(m_scr, acc_scr)
    n_back = WINDOW // tq
    for c in range(n_back + 1):
        kt = i - n_back + c

        @pl.when(kt >= 0)
        def _(kt=kt):
            s = _dot(kw_ref[0, pl.ds(pl.multiple_of(kt * tq, tq), tq), :], qa_scr[0:128, :])
            kpos = kt * tq + lax.broadcasted_iota(jnp.int32, (tq, n), 0)
            tok = i * tq + (lax.broadcasted_iota(jnp.int32, (tq, n), 1) & (tq - 1))
            s = jnp.where((kpos <= tok) & (kpos > tok - WINDOW), s, NEG)
            _flash_update(s, vw_ref[0, kt], m_scr, acc_scr)

    o = _flash_result(acc_scr, A_DIM)
    for r in range(A_REP):
        ow_ref[r * A_DIM:(r + 1) * A_DIM, :] = o[:, r * tq:(r + 1) * tq]


def _nsa_sel(q_t, k_aug, v_slc_tiles, selb, k_win, v_win_tiles, batch, seq, tq, tk):
    nq, nk = seq // tq, seq // tk
    nb = selb.shape[1]
    t = q_t.shape[1]
    gr = A_REP * A_DIM
    n = A_REP * tq
    kw = k_aug.shape[2]
    out = jax.ShapeDtypeStruct((A_HEADS * A_DIM, t), F32)
    return pl.pallas_call(
        functools.partial(_nsa_sel_kernel, tq=tq, tk=tk),
        grid=(batch, A_GROUPS, nq),
        in_specs=[pl.BlockSpec((gr, tq), lambda b, g, i: (g, b * nq + i)),
                  pl.BlockSpec((1, seq, kw), lambda b, g, i: (g, b, 0)),
                  pl.BlockSpec((1, nk, A_DIM + ONES_ROWS, tk), lambda b, g, i: (g, b, 0, 0)),
                  pl.BlockSpec((1, nb, tq), lambda b, g, i: (g, 0, b * nq + i)),
                  pl.BlockSpec((1, seq, 128), lambda b, g, i: (g, b, 0)),
                  pl.BlockSpec((1, nq, A_DIM + ONES_ROWS, tq), lambda b, g, i: (g, b, 0, 0))],
        out_specs=[pl.BlockSpec((gr, tq), lambda b, g, i: (g, b * nq + i)),
                   pl.BlockSpec((gr, tq), lambda b, g, i: (g, b * nq + i))],
        out_shape=[out, out],
        scratch_shapes=[pltpu.VMEM((kw, n), BF16), pltpu.VMEM((FLASH_UNROLL, tk, n), F32),
                        pltpu.VMEM((1, n), F32), pltpu.VMEM((A_DIM + ONES_ROWS, n), F32)],
        compiler_params=_params(3),
        name="nsa_sel",
    )(q_t, k_aug, v_slc_tiles, selb, k_win, v_win_tiles)


def _mla_prep_kernel(cq_ref, ckv_ref, kr_ref, cos_ref, sin_ref, qlg_ref, kvlg_ref, qg_ref, kg_ref, wuq_ref, wukv_ref,
                     qo_ref, ko_ref, vo_ref):
    cos, sin = cos_ref[...], sin_ref[...]
    tt = cos.shape[1]
    q_all = _dot(wuq_ref[...], (_rms_rows(cq_ref[...]) * qlg_ref[...]).astype(BF16))
    kv_all = _dot(wukv_ref[...], (_rms_rows(ckv_ref[...]) * kvlg_ref[...]).astype(BF16))
    kr = kr_ref[...]
    pad = jnp.zeros((128 - B_QK, tt), F32)
    for h in range(B_HEADS):
        y = _rms_rows(q_all[h * B_QK:(h + 1) * B_QK]) * qg_ref[...]
        y = _rope_rows(y, cos, sin, B_NOPE, B_ROPE // 2) * (B_QK ** -0.5 * LOG2E)
        qo_ref[h] = jnp.concatenate([y, pad], axis=0).astype(BF16)
        base = h * (B_NOPE + B_V)
        k = jnp.concatenate([kv_all[base:base + B_NOPE], kr], axis=0)
        y = _rope_rows(_rms_rows(k) * kg_ref[...], cos, sin, B_NOPE, B_ROPE // 2)
        ko_ref[h] = jnp.concatenate([y, pad], axis=0).T.astype(BF16)
        vo_ref[h, 0] = _with_ones(kv_all[base + B_NOPE:base + B_NOPE + B_V])


def _mla_attn_kernel(q_ref, k_ref, v_ref, o_ref, s_scr, m_scr, acc_scr, *, tq, tk):
    i = pl.program_id(2)
    _flash_init(m_scr, acc_scr)

    def scores(j):
        return _dot(k_ref[0, pl.ds(pl.multiple_of(j * tk, tk), tk), :], q_ref[0])

    def causal(s, j):
        kpos = j * tk + lax.broadcasted_iota(jnp.int32, (tk, tq), 0)
        tok = i * tq + lax.broadcasted_iota(jnp.int32, (tk, tq), 1)
        return jnp.where(kpos <= tok, s, NEG)

    _flash_causal(scores, lambda j: v_ref[0, j], causal, (i * tq) // tk, max(1, tq // tk), s_scr, m_scr, acc_scr)
    o_ref[...] = _flash_result(acc_scr, B_V)


def _mla_attn(q_m, k_m, v_m_tiles, batch, seq, tq, tk):
    nq, nk = seq // tq, seq // tk
    t = q_m.shape[2]
    return pl.pallas_call(
        functools.partial(_mla_attn_kernel, tq=tq, tk=tk),
        grid=(batch, B_HEADS, nq),
        in_specs=[pl.BlockSpec((1, 128, tq), lambda b, h, i: (h, 0, b * nq + i)),
                  pl.BlockSpec((1, seq, 128), lambda b, h, i: (h, b, 0)),
                  pl.BlockSpec((1, nk, B_V + ONES_ROWS, tk), lambda b, h, i: (h, b, 0, 0))],
        out_specs=pl.BlockSpec((B_V, tq), lambda b, h, i: (h, b * nq + i)),
        out_shape=jax.ShapeDtypeStruct((B_HEADS * B_V, t), F32),
        scratch_shapes=[pltpu.VMEM((FLASH_UNROLL, tk, tq), F32),
                        pltpu.VMEM((1, tq), F32), pltpu.VMEM((B_V + ONES_ROWS, tq), F32)],
        compiler_params=_params(3),
        name="mla_attn",
    )(q_m, k_m, v_m_tiles)


def _out_proj_kernel(oc_ref, os_ref, ow_ref, gt_ref, ob_ref, x_ref, ga_ref, gb_ref, w_ref, g2_ref, h_ref, hn_ref):
    heads = []
    for h in range(A_HEADS):
        rows = slice(h * A_DIM, (h + 1) * A_DIM)
        heads.append(gt_ref[3 * h:3 * h + 1, :] * oc_ref[rows, :] + gt_ref[3 * h + 1:3 * h + 2, :] * os_ref[rows, :]
                     + gt_ref[3 * h + 2:3 * h + 3, :] * ow_ref[rows, :])
    oa = _rms_rows(jnp.concatenate(heads, axis=0)) * ga_ref[...]
    ob = _rms_rows(ob_ref[...]) * gb_ref[...]
    cat = jnp.concatenate([oa, ob], axis=0).astype(BF16)
    hid = x_ref[...].T + _dot(w_ref[...], cat)
    h_ref[...] = hid
    hn_ref[...] = (_rms_rows(hid) * g2_ref[...]).astype(BF16)


def _out_proj(oc_t, os_t, ow_t, gates_t, ob_t, x2, gain_a, gain_b, w_out_t, gain2, tt):
    t, d = x2.shape
    aw = oc_t.shape[0]
    bw = ob_t.shape[0]
    tok = lambda rows: pl.BlockSpec((rows, tt), lambda i: (0, i))
    full = lambda a: pl.BlockSpec(a.shape, lambda i: (0,) * a.ndim)
    return pl.pallas_call(
        _out_proj_kernel,
        grid=(t // tt,),
        in_specs=[tok(aw), tok(aw), tok(aw), tok(GATE_ROWS), tok(bw), pl.BlockSpec((tt, d), lambda i: (i, 0)),
                  full(gain_a), full(gain_b), full(w_out_t), full(gain2)],
        out_specs=[tok(d), tok(d)],
        out_shape=[jax.ShapeDtypeStruct((d, t), F32), jax.ShapeDtypeStruct((d, t), BF16)],
        compiler_params=_params(1),
        name="out_proj",
    )(oc_t, os_t, ow_t, gates_t, ob_t, x2, gain_a, gain_b, w_out_t, gain2)


def _top_ranked(s):
    n, tp = s.shape
    row = lax.broadcasted_iota(jnp.int32, (n, tp), 0).astype(F32)
    slot = lax.broadcasted_iota(jnp.int32, (P_TOPK, tp), 0)

    def body(a, carry):
        v, rank, vals = carry
        mx = jnp.max(v, axis=0, keepdims=True)
        first = jnp.min(jnp.where(v == mx, row, float(n)), axis=0, keepdims=True)
        hit = row == first
        rank = jnp.where(hit, jnp.asarray(a, F32), rank)
        v = jnp.where(hit, -jnp.inf, v)
        vals = jnp.where(slot == a, mx, vals)
        return v, rank, vals

    _, rank, vals = lax.fori_loop(0, P_TOPK, body,
                                  (s, jnp.full((n, tp), float(P_TOPK), F32), jnp.zeros((P_TOPK, tp), F32)))
    return rank, vals


def _pair_counts(v1, v2):
    k = v1.shape[0]
    slot = lax.broadcasted_iota(jnp.int32, v1.shape, 0).astype(F32)
    top = v1[0:1] + v2[0:1]

    def body(_, carry):
        count, front, z = carry
        mx = jnp.max(front, axis=0, keepdims=True)
        a_star = jnp.min(jnp.where(front == mx, slot, float(k)), axis=0, keepdims=True)
        hit = slot == a_star
        count = count + jnp.where(hit, 1.0, 0.0)
        nxt = jnp.sum(jnp.where(hit, count, 0.0), axis=0, keepdims=True)
        v2_nxt = jnp.sum(jnp.where(slot == nxt, v2, 0.0), axis=0, keepdims=True)
        front = jnp.where(hit, jnp.where(nxt < float(k), v1 + v2_nxt, -jnp.inf), front)
        return count, front, z + jnp.exp(mx - top)

    count, _, z = lax.fori_loop(0, k, body, (jnp.zeros(v1.shape, F32), v1 + v2[0:1], jnp.zeros(top.shape, F32)))
    return count, z


REMOVED = -2.0 ** 126
LANES = 128


def _top_ranked_pair_fast(s1, s2):
    n, tp = s1.shape
    slot = lax.broadcasted_iota(jnp.int32, (P_TOPK, tp), 0)

    def body(a, carry):
        code = REMOVED * (1.0 + jnp.asarray(a, F32) * (1.0 / 32.0))
        out = []
        for key, vals in (carry[0:2], carry[2:4]):
            mx = jnp.max(key, axis=0, keepdims=True)
            key = jnp.where(key == mx, code, key)
            out += [key, jnp.where(slot == a, mx, vals)]
        return tuple(out)

    zeros = jnp.zeros((P_TOPK, tp), F32)
    k1, t1, k2, t2 = lax.fori_loop(0, P_TOPK, body, (s1, zeros, s2, zeros))
    res, bad = [], jnp.zeros((1, tp), F32)
    for key, vals in ((k1, t1), (k2, t2)):
        removed = key <= REMOVED
        rank = jnp.where(removed, (key * (1.0 / REMOVED) - 1.0) * 32.0, float(P_TOPK))
        n_removed = jnp.sum(jnp.where(removed, 1.0, 0.0), axis=0, keepdims=True)
        bad = jnp.maximum(bad, jnp.abs(n_removed - float(P_TOPK)))
        res += [rank, vals]
    return res[0], res[1], res[2], res[3], bad


def _pack_bf16_twice(x):
    hi = pltpu.bitcast(x.astype(BF16).astype(F32), jnp.int32)
    return hi | lax.shift_right_logical(hi, 16)


def _peer_route_kernel(hn_ref, wq_ref, keys_ref, cut_ref, g1_ref, r2_ref, g2_ref, q_scr, s_scr, rank_scr, vals_scr):
    tp = hn_ref.shape[1]
    half = P_KEY_DIM // 2
    q_scr[...] = _dot(wq_ref[...], hn_ref[...])
    for h in range(P_HEADS):
        q = _rms_rows(q_scr[h * P_KEY_DIM:(h + 1) * P_KEY_DIM, :]).astype(BF16)
        s_scr[0, h] = _dot(keys_ref[h, 0], q[:half])
        s_scr[1, h] = _dot(keys_ref[h, 1], q[half:])

    def first_level(rank_pair, flagged):
        for h in range(P_HEADS):
            for lt in range(tp // LANES):
                lanes = slice(lt * LANES, (lt + 1) * LANES)
                rank1, vals1, rank2, vals2, bad = rank_pair(s_scr[0, h, :, lanes], s_scr[1, h, :, lanes])
                flagged = jnp.maximum(flagged, bad)
                rank_scr[0, h, :, lanes] = rank1
                rank_scr[1, h, :, lanes] = rank2
                for a in range(P_TOPK):
                    vals_scr[0, a, h:h + 1, lanes] = vals1[a:a + 1]
                    vals_scr[1, a, h:h + 1, lanes] = vals2[a:a + 1]
        return flagged

    flagged = first_level(_top_ranked_pair_fast, jnp.zeros((1, LANES), F32))

    @pl.when(jnp.max(flagged) > 0.0)
    def _():
        first_level(lambda s1, s2: (*_top_ranked(s1), *_top_ranked(s2), jnp.zeros((1, LANES), F32)), flagged)

    count, z = _pair_counts(vals_scr[0], vals_scr[1])
    for h in range(P_HEADS):
        rank1 = rank_scr[0, h]
        cut = jnp.zeros_like(rank1)
        for a in range(P_TOPK):
            cut = jnp.where(rank1 == float(a), count[a, h:h + 1, :], cut)
        cut_ref[h] = _pack_bf16_twice(cut)
        g1_ref[h] = _pack_bf16_twice(jnp.exp(s_scr[0, h] - vals_scr[0, 0, h:h + 1, :]))
        r2_ref[h] = rank_scr[1, h].astype(BF16)
        g2_ref[h] = (jnp.exp(s_scr[1, h] - vals_scr[1, 0, h:h + 1, :]) * (1.0 / z[0, h:h + 1, :])).astype(BF16)


def _peer_route(hn_t, wq_t, sub_keys, tp):
    d, t = hn_t.shape
    words = jax.ShapeDtypeStruct((P_HEADS, N_KEYS, t), jnp.int32)
    halfs = jax.ShapeDtypeStruct((P_HEADS, N_KEYS, t), BF16)
    ospec = pl.BlockSpec((P_HEADS, N_KEYS, tp), lambda i: (0, 0, i))
    return pl.pallas_call(
        _peer_route_kernel,
        grid=(t // tp,),
        in_specs=[pl.BlockSpec((d, tp), lambda i: (0, i)),
                  pl.BlockSpec(wq_t.shape, lambda i: (0, 0)),
                  pl.BlockSpec(sub_keys.shape, lambda i: (0, 0, 0, 0))],
        out_specs=[ospec, ospec, ospec, ospec],
        out_shape=[words, words, halfs, halfs],
        scratch_shapes=[pltpu.VMEM((P_HEADS * P_KEY_DIM, tp), F32), pltpu.VMEM((2, P_HEADS, N_KEYS, tp), F32),
                        pltpu.VMEM((2, P_HEADS, N_KEYS, tp), F32), pltpu.VMEM((2, P_TOPK, P_HEADS, tp), F32)],
        compiler_params=_params(1),
        name="peer_route",
    )(hn_t, wq_t, sub_keys)


def _peer_ffn_kernel(hn_ref, h_ref, u_ref, v_ref, cut_ref, g1_ref, r2_ref, g2_ref, o_ref, acc_scr, *, te):
    e = pl.program_id(1)

    @pl.when(e == 0)
    def _():
        acc_scr[...] = jnp.zeros(acc_scr.shape, F32)

    hn = hn_ref[...]
    tt = hn.shape[1]

    def rows_bf16(word_row):
        tile = pltpu.bitcast(jnp.broadcast_to(word_row, (8, tt)), BF16)
        return jnp.concatenate([tile] * (N_KEYS // 16), axis=0)

    chunk = 2 * N_KEYS
    weights = []
    for c in range(te // chunk):
        a = _dot(u_ref[c * chunk:(c + 1) * chunk, :], hn)
        for k in range(chunk // N_KEYS):
            ii = c * (chunk // N_KEYS) + k
            gate = jnp.zeros((N_KEYS, tt), BF16)
            for h in range(P_HEADS):
                chosen = r2_ref[h] < rows_bf16(cut_ref[h, ii:ii + 1, :])
                gate = gate + jnp.where(chosen, g2_ref[h], jnp.zeros_like(gate)) * rows_bf16(g1_ref[h, ii:ii + 1, :])
            weights.append(gate * _gelu_tanh(a[k * N_KEYS:(k + 1) * N_KEYS, :]).astype(BF16))
    acc_scr[...] += _dot(v_ref[...], jnp.concatenate(weights, axis=0))

    @pl.when(e == pl.num_programs(1) - 1)
    def _():
        o_ref[...] = (h_ref[...] + acc_scr[...]).T


def _peer_ffn(hn_t, h_t, u_bf, v_t_bf, cut_k, g1_k, r2, g2, tt, te):
    d, t = hn_t.shape
    n_exp = u_bf.shape[0]
    kpe = te // N_KEYS
    return pl.pallas_call(
        functools.partial(_peer_ffn_kernel, te=te),
        grid=(t // tt, n_exp // te),
        in_specs=[pl.BlockSpec((d, tt), lambda i, e: (0, i)),
                  pl.BlockSpec((d, tt), lambda i, e: (0, i)),
                  pl.BlockSpec((te, d), lambda i, e: (e, 0)),
                  pl.BlockSpec((d, te), lambda i, e: (0, e)),
                  pl.BlockSpec((P_HEADS, kpe, tt), lambda i, e: (0, e, i)),
                  pl.BlockSpec((P_HEADS, kpe, tt), lambda i, e: (0, e, i)),
                  pl.BlockSpec((P_HEADS, N_KEYS, tt), lambda i, e: (0, 0, i)),
                  pl.BlockSpec((P_HEADS, N_KEYS, tt), lambda i, e: (0, 0, i))],
        out_specs=pl.BlockSpec((tt, d), lambda i, e: (i, 0)),
        out_shape=jax.ShapeDtypeStruct((t, d), F32),
        scratch_shapes=[pltpu.VMEM((d, tt), F32)],
        compiler_params=_params(2),
        name="peer_ffn",
    )(hn_t, h_t, u_bf, v_t_bf, cut_k, g1_k, r2, g2)


def _rope_tables_t(pos_flat, rot_dim):
    inv_freq = ROPE_THETA ** (-jnp.arange(0, rot_dim, 2, dtype=F32) / rot_dim)
    ang = pos_flat.astype(F32)[None, :] * inv_freq[:, None]
    return jnp.cos(ang), jnp.sin(ang)


def _expand_cmp_w1(w1):
    w = w1.reshape(CMP_LEN, A_DIM, CMP_HIDDEN)
    out = []
    for part in (w[:CMP_STRIDE], w[CMP_STRIDE:]):
        z = jnp.zeros_like(part)
        both = jnp.stack([jnp.concatenate([part, z], axis=1), jnp.concatenate([z, part], axis=1)])
        out.append(both.reshape(A_GROUPS, CMP_STRIDE * A_GROUPS * A_DIM, CMP_HIDDEN).transpose(0, 2, 1).astype(BF16))
    return out


TOKEN_TILE = 512
NSA_Q_TILE = 256
NSA_SEL_K_TILE = 512
MLA_Q_TILE = 1024
MLA_K_TILE = 512
PEER_ROUTE_TILE = 256
PEER_EXPERT_TILE = 2048


def _col(v):
    return v.reshape(-1, 1).astype(F32)


def _mixers(x, positions, norm1_gain, w_in, nsa_q_gain, nsa_kc_gain, nsa_ks_gain, nsa_kw_gain,
            cmp_pos, cmp_k_w1, cmp_k_w2, cmp_v_w1, cmp_v_w2,
            mla_q_lora_gain, mla_w_uq, mla_kv_lora_gain, mla_w_ukv, mla_q_gain, mla_k_gain):
    batch, seq, d = x.shape
    t = batch * seq
    tt = TOKEN_TILE
    tq_nsa = NSA_Q_TILE
    tk_sel = NSA_SEL_K_TILE
    tq_mla, tk_mla = MLA_Q_TILE, MLA_K_TILE
    assert tk_sel == tt and tk_mla == tt
    assert d == D_MODEL and seq % 512 == 0 and seq // SLC_LEN >= SLC_TOPK and WINDOW % tq_nsa == 0
    col = _col
    x2 = x.reshape(t, d)
    pos = positions.reshape(t)

    w_in_t = w_in.T
    gate_lo = sum((512, 128, 128, 128, 128, 128, 128))
    gate_hi = gate_lo + 3 * A_HEADS
    w_in_t = jnp.concatenate([w_in_t[:gate_lo], w_in_t[gate_hi:], w_in_t[gate_lo:gate_hi],
                              jnp.zeros((PROJ_ROWS - w_in_t.shape[0], d), F32)], axis=0).astype(BF16)
    cos_a, sin_a = _rope_tables_t(pos, A_ROPE)
    cos_b, sin_b = _rope_tables_t(pos, B_ROPE)
    q_t, kc_tm, vc_tm, k_slc, v_slc_t, k_win, v_win_t, gates_t, q_m, k_m, v_m_t = _in_proj(
        x2, norm1_gain.reshape(1, d), w_in_t, cos_a, sin_a, col(nsa_q_gain), col(nsa_ks_gain), col(nsa_kw_gain),
        cos_b, sin_b, col(mla_q_lora_gain), col(mla_kv_lora_gain), col(mla_q_gain), col(mla_k_gain),
        mla_w_uq.T.astype(BF16), mla_w_ukv.T.astype(BF16), tt, seq, tq_nsa)

    nc = seq // CMP_STRIDE
    chunk_w = CMP_STRIDE * A_GROUPS * A_DIM
    w1ka, w1kb = _expand_cmp_w1(cmp_k_w1)
    w1va, w1vb = _expand_cmp_w1(cmp_v_w1)
    pos_rows = lambda p: jnp.broadcast_to(p[:, None, :], (CMP_STRIDE, A_GROUPS, A_DIM)).reshape(1, chunk_w)
    cmp_end = jnp.minimum(jnp.arange(nc) * CMP_STRIDE + CMP_LEN - 1, seq - 1)
    cos_c, sin_c = _rope_tables_t(positions[:, cmp_end].reshape(-1), A_ROPE)
    to_b = lambda a: a.reshape(A_ROPE // 2, batch, nc).transpose(1, 0, 2)
    kcmp, vcmp_t = _compress(kc_tm.reshape(batch, nc, chunk_w), vc_tm.reshape(batch, nc, chunk_w),
                             w1ka, w1kb, w1va, w1vb, pos_rows(cmp_pos[:CMP_STRIDE]), pos_rows(cmp_pos[CMP_STRIDE:]),
                             cmp_k_w2.T.astype(BF16), cmp_v_w2.T.astype(BF16), col(nsa_kc_gain), to_b(cos_c), to_b(sin_c))

    n_cmp = (seq - CMP_LEN) // CMP_STRIDE + 1
    nb = seq // SLC_LEN
    c_start = np.arange(nc)[None, :] * CMP_STRIDE
    s_start = np.arange(nb)[:, None] * SLC_LEN
    ov = (c_start < s_start + SLC_LEN) & (c_start + CMP_LEN - 1 >= s_start) & (np.arange(nc)[None, :] < n_cmp)
    ov_t = jnp.asarray(ov.astype(np.float32)).astype(BF16)

    oc_t, selb = _nsa_cmp(q_t, kcmp, vcmp_t, ov_t, batch, seq, tq_nsa)
    os_t, ow_t = _nsa_sel(q_t, k_slc, v_slc_t, selb, k_win, v_win_t, batch, seq, tq_nsa, tk_sel)

    ob_t = _mla_attn(q_m, k_m, v_m_t, batch, seq, tq_mla, tk_mla)
    return oc_t, os_t, ow_t, gates_t, ob_t


def _peer(hn_t, h_t, peer_w_q, peer_sub_keys, peer_u, peer_v):
    cut, g1, r2, g2 = _peer_route(hn_t, peer_w_q.T.astype(BF16), peer_sub_keys.astype(BF16), PEER_ROUTE_TILE)
    return _peer_ffn(hn_t, h_t, peer_u.astype(BF16), peer_v.T.astype(BF16),
                     cut, g1, r2, g2, TOKEN_TILE, PEER_EXPERT_TILE)


def _layer(x, positions, norm1_gain, w_in, nsa_q_gain, nsa_kc_gain, nsa_ks_gain, nsa_kw_gain,
           cmp_pos, cmp_k_w1, cmp_k_w2, cmp_v_w1, cmp_v_w2,
           mla_q_lora_gain, mla_w_uq, mla_kv_lora_gain, mla_w_ukv, mla_q_gain, mla_k_gain,
           out_gain_a, out_gain_b, w_out, norm2_gain, peer_w_q, peer_sub_keys, peer_u, peer_v):
    batch, seq, d = x.shape
    oc_t, os_t, ow_t, gates_t, ob_t = _mixers(
        x, positions, norm1_gain, w_in, nsa_q_gain, nsa_kc_gain, nsa_ks_gain, nsa_kw_gain,
        cmp_pos, cmp_k_w1, cmp_k_w2, cmp_v_w1, cmp_v_w2,
        mla_q_lora_gain, mla_w_uq, mla_kv_lora_gain, mla_w_ukv, mla_q_gain, mla_k_gain)
    h_t, hn_t = _out_proj(oc_t, os_t, ow_t, gates_t, ob_t, x.reshape(batch * seq, d), _col(out_gain_a), _col(out_gain_b),
                          w_out.T.astype(BF16), _col(norm2_gain), TOKEN_TILE)
    return _peer(hn_t, h_t, peer_w_q, peer_sub_keys, peer_u, peer_v).reshape(batch, seq, d)


def kernel(x, positions, norm1_gain, w_in, nsa_q_gain, nsa_kc_gain, nsa_ks_gain, nsa_kw_gain, cmp_pos, cmp_k_w1, cmp_k_w2, cmp_v_w1, cmp_v_w2, mla_q_lora_gain, mla_w_uq, mla_kv_lora_gain, mla_w_ukv, mla_q_gain, mla_k_gain, out_gain_a, out_gain_b, w_out, norm2_gain, peer_w_q, peer_sub_keys, peer_u, peer_v):
    h = x
    for l in range(norm1_gain.shape[0]):
        h = _layer(h, positions, norm1_gain[l], w_in[l], nsa_q_gain[l], nsa_kc_gain[l], nsa_ks_gain[l], nsa_kw_gain[l],
                   cmp_pos[l], cmp_k_w1[l], cmp_k_w2[l], cmp_v_w1[l], cmp_v_w2[l],
                   mla_q_lora_gain[l], mla_w_uq[l], mla_kv_lora_gain[l], mla_w_ukv[l], mla_q_gain[l], mla_k_gain[l],
                   out_gain_a[l], out_gain_b[l], w_out[l], norm2_gain[l], peer_w_q[l], peer_sub_keys[l],
                   peer_u[l], peer_v[l])
    return h
```

```python
import functools

import jax
import jax.numpy as jnp
import numpy as np
from jax import lax
from jax.experimental import pallas as pl
from jax.experimental.pallas import tpu as pltpu

F32, BF16 = jnp.float32, jnp.bfloat16
EPS = 1e-6
NEG = -1e30
FORCE = 1e9
ROPE_THETA = 500000.0
LOG2E = 1.4426950408889634

D_MODEL = 1024
A_HEADS, A_GROUPS, A_DIM = 8, 2, 64
A_REP = A_HEADS // A_GROUPS
A_ROPE = A_DIM // 4
CMP_LEN, CMP_STRIDE, CMP_HIDDEN = 32, 16, 256
SLC_LEN, SLC_TOPK, WINDOW = 64, 16, 512
B_HEADS, Q_LORA, KV_LORA, B_NOPE, B_ROPE, B_V = 8, 256, 128, 64, 32, 64
B_QK = B_NOPE + B_ROPE
P_HEADS, N_KEYS, P_KEY_DIM, P_TOPK = 8, 128, 256, 16
N_EXPERTS = N_KEYS * N_KEYS

ROW_Q, ROW_KC, ROW_VC, ROW_KS, ROW_VS, ROW_KW, ROW_VW = 0, 512, 640, 768, 896, 1024, 1152
ROW_CQ, ROW_CKV, ROW_KR, ROW_GATE, PROJ_ROWS = 1280, 1536, 1664, 1696, 1728
GATE_ROWS = 32

VMEM_LIMIT = 56 * 1024 * 1024
NT_DIMS = (((1,), (1,)), ((), ()))


def _params(n_axes):
    return pltpu.CompilerParams(dimension_semantics=("arbitrary",) * n_axes, vmem_limit_bytes=VMEM_LIMIT)


def _dot(a, b):
    return jnp.dot(a, b, preferred_element_type=F32)


def _dot_nt(a, b):
    return lax.dot_general(a, b, NT_DIMS, preferred_element_type=F32)


def _row_sumsq(x):
    sq = x * x
    hi = sq.astype(BF16)
    lo = (sq - hi.astype(F32)).astype(BF16)
    ones = jnp.ones((8, x.shape[1]), BF16)
    return (_dot_nt(ones, hi) + _dot_nt(ones, lo))[0:1, :]


def _rms_rows(x):
    ss = jnp.sum(x * x, axis=0, keepdims=True)
    return x * lax.rsqrt(ss * (1.0 / x.shape[0]) + EPS)


def _rope_rows(y, cos, sin, off, half):
    x1, x2 = y[off:off + half], y[off + half:off + 2 * half]
    parts = [y[:off]] if off else []
    parts += [x1 * cos - x2 * sin, x2 * cos + x1 * sin]
    if off + 2 * half < y.shape[0]:
        parts.append(y[off + 2 * half:])
    return jnp.concatenate(parts, axis=0)


def _gelu_tanh(x):
    c = 0.7978845608028654
    half = 0.5 * x
    return half + half * jnp.tanh(x * (c + (c * 0.044715) * (x * x)))


def _in_proj_kernel(x_ref, g_ref, w_ref, cos_a_ref, sin_a_ref, qg_ref, ksg_ref, kwg_ref,
                    cos_b_ref, sin_b_ref, qlg_ref, kvlg_ref, mqg_ref, mkg_ref, wuq_ref, wukv_ref, *out_refs, seq, tw):
    x = x_ref[...]
    xg = (x * g_ref[...]).astype(BF16)
    p = _dot_nt(w_ref[...], xg)
    p = p * lax.rsqrt(_row_sumsq(x) * (1.0 / x.shape[1]) + EPS)
    gw = A_GROUPS * A_DIM
    rows = lambda start, n: p[start:start + n]
    _nsa_prep_kernel(rows(ROW_Q, A_HEADS * A_DIM), rows(ROW_KC, gw), rows(ROW_VC, gw), rows(ROW_KS, gw), rows(ROW_VS, gw),
                     rows(ROW_KW, gw), rows(ROW_VW, gw), rows(ROW_GATE, GATE_ROWS), cos_a_ref, sin_a_ref,
                     qg_ref, ksg_ref, kwg_ref, *out_refs[:8], seq=seq, tw=tw)
    _mla_prep_kernel(rows(ROW_CQ, Q_LORA), rows(ROW_CKV, KV_LORA), rows(ROW_KR, B_ROPE), cos_b_ref, sin_b_ref,
                     qlg_ref, kvlg_ref, mqg_ref, mkg_ref, wuq_ref, wukv_ref, *out_refs[8:])


def _in_proj(x2, gain, w_t, cos_a, sin_a, q_gain, ks_gain, kw_gain, cos_b, sin_b, q_lora_gain, kv_lora_gain,
             mq_gain, mk_gain, wuq_t, wukv_t, tt, seq, tw):
    t, d = x2.shape
    gw = A_GROUPS * A_DIM
    kw = 128 + seq // SLC_LEN
    va, vb = A_DIM + ONES_ROWS, B_V + ONES_ROWS
    full = lambda a: pl.BlockSpec(a.shape, lambda i: (0,) * a.ndim)
    lanes = lambda n: pl.BlockSpec((n, tt), lambda i: (0, i))
    return pl.pallas_call(
        functools.partial(_in_proj_kernel, seq=seq, tw=tw),
        grid=(t // tt,),
        in_specs=[pl.BlockSpec((tt, d), lambda i: (i, 0)), full(gain), full(w_t),
                  lanes(A_ROPE // 2), lanes(A_ROPE // 2), full(q_gain), full(ks_gain), full(kw_gain),
                  lanes(B_ROPE // 2), lanes(B_ROPE // 2), full(q_lora_gain), full(kv_lora_gain), full(mq_gain),
                  full(mk_gain), full(wuq_t), full(wukv_t)],
        out_specs=[lanes(A_HEADS * A_DIM),
                   pl.BlockSpec((tt, gw), lambda i: (i, 0)),
                   pl.BlockSpec((tt, gw), lambda i: (i, 0)),
                   pl.BlockSpec((A_GROUPS, tt, kw), lambda i: (0, i, 0)),
                   pl.BlockSpec((A_GROUPS, 1, va, tt), lambda i: (0, i, 0, 0)),
                   pl.BlockSpec((A_GROUPS, tt, 128), lambda i: (0, i, 0)),
                   pl.BlockSpec((A_GROUPS, tt // tw, va, tw), lambda i: (0, i, 0, 0)),
                   lanes(GATE_ROWS),
                   pl.BlockSpec((B_HEADS, 128, tt), lambda i: (0, 0, i)),
                   pl.BlockSpec((B_HEADS, tt, 128), lambda i: (0, i, 0)),
                   pl.BlockSpec((B_HEADS, 1, vb, tt), lambda i: (0, i, 0, 0))],
        out_shape=[jax.ShapeDtypeStruct((A_HEADS * A_DIM, t), BF16),
                   jax.ShapeDtypeStruct((t, gw), F32),
                   jax.ShapeDtypeStruct((t, gw), F32),
                   jax.ShapeDtypeStruct((A_GROUPS, t, kw), BF16),
                   jax.ShapeDtypeStruct((A_GROUPS, t // tt, va, tt), BF16),
                   jax.ShapeDtypeStruct((A_GROUPS, t, 128), BF16),
                   jax.ShapeDtypeStruct((A_GROUPS, t // tw, va, tw), BF16),
                   jax.ShapeDtypeStruct((GATE_ROWS, t), F32),
                   jax.ShapeDtypeStruct((B_HEADS, 128, t), BF16),
                   jax.ShapeDtypeStruct((B_HEADS, t, 128), BF16),
                   jax.ShapeDtypeStruct((B_HEADS, t // tt, vb, tt), BF16)],
        compiler_params=_params(1),
        name="in_proj",
    )(x2, gain, w_t, cos_a, sin_a, q_gain, ks_gain, kw_gain, cos_b, sin_b, q_lora_gain, kv_lora_gain,
      mq_gain, mk_gain, wuq_t, wukv_t)


def _nsa_prep_kernel(q_ref, kc_ref, vc_ref, ks_ref, vs_ref, kw_ref, vw_ref, gt_ref, cos_ref, sin_ref,
                     qg_ref, ksg_ref, kwg_ref,
                     qo_ref, kco_ref, vco_ref, kso_ref, vso_ref, kwo_ref, vwo_ref, gto_ref, *, seq, tw):
    cos, sin = cos_ref[...], sin_ref[...]
    tt = cos.shape[1]
    nb = seq // SLC_LEN
    for h in range(A_HEADS):
        y = _rms_rows(q_ref[h * A_DIM:(h + 1) * A_DIM, :]) * qg_ref[...]
        y = _rope_rows(y, cos, sin, 0, A_ROPE // 2) * (A_DIM ** -0.5 * LOG2E)
        qo_ref[h * A_DIM:(h + 1) * A_DIM, :] = y.astype(BF16)
    kco_ref[...] = kc_ref[...].T
    vco_ref[...] = vc_ref[...].T
    zeros = jnp.zeros((A_DIM, tt), F32)
    tok = pl.program_id(0) * tt + lax.broadcasted_iota(jnp.int32, (tt, nb), 0)
    block_hot = jnp.where(lax.broadcasted_iota(jnp.int32, (tt, nb), 1) == (tok % seq) // SLC_LEN, 1.0, 0.0)
    for g in range(A_GROUPS):
        for src, gain, dst in ((ks_ref, ksg_ref, kso_ref), (kw_ref, kwg_ref, kwo_ref)):
            y = _rms_rows(src[g * A_DIM:(g + 1) * A_DIM, :]) * gain[...]
            y = _rope_rows(y, cos, sin, 0, A_ROPE // 2)
            k_tm = jnp.concatenate([y, zeros], axis=0).T
            if dst is kso_ref:
                k_tm = jnp.concatenate([k_tm, block_hot], axis=1)
            dst[g] = k_tm.astype(BF16)
        vso_ref[g, 0] = _with_ones(vs_ref[g * A_DIM:(g + 1) * A_DIM, :])
        v_win = _with_ones(vw_ref[g * A_DIM:(g + 1) * A_DIM, :])
        for c in range(tt // tw):
            vwo_ref[g, c] = v_win[:, c * tw:(c + 1) * tw]
    gto_ref[...] = 1.0 / (1.0 + jnp.exp(-gt_ref[...]))


def _compress_kernel(kc_ref, vc_ref, w1ka_ref, w1kb_ref, w1va_ref, w1vb_ref, plo_ref, phi_ref,
                     w2k_ref, w2v_ref, kg_ref, cos_ref, sin_ref, ko_ref, vo_ref):
    nc = kc_ref.shape[1]
    zeros = jnp.zeros((A_DIM, nc), F32)
    for src, w1a, w1b, w2, is_k in ((kc_ref, w1ka_ref, w1kb_ref, w2k_ref, True),
                                    (vc_ref, w1va_ref, w1vb_ref, w2v_ref, False)):
        x = src[0]
        xlo = (x + plo_ref[...]).astype(BF16)
        xhi = (x + phi_ref[...]).astype(BF16)
        for g in range(A_GROUPS):
            first = _dot_nt(w1a[g], xlo)
            second = _dot_nt(w1b[g], xhi)
            hid = _gelu_tanh(first + pltpu.roll(second, nc - 1, axis=1)).astype(BF16)
            c = _dot(w2[...], hid)
            if is_k:
                y = _rope_rows(_rms_rows(c) * kg_ref[...], cos_ref[0], sin_ref[0], 0, A_ROPE // 2)
                ko_ref[0, g] = jnp.concatenate([y, zeros], axis=0).T.astype(BF16)
            else:
                vo_ref[0, g] = c.astype(BF16)


def _compress(kc_chunks, vc_chunks, w1ka, w1kb, w1va, w1vb, plo, phi, w2k_t, w2v_t, kc_gain, cos_c, sin_c):
    b, nc, cw = kc_chunks.shape
    full = lambda a: pl.BlockSpec(a.shape, lambda i: (0,) * a.ndim)
    return pl.pallas_call(
        _compress_kernel,
        grid=(b,),
        in_specs=[pl.BlockSpec((1, nc, cw), lambda i: (i, 0, 0)), pl.BlockSpec((1, nc, cw), lambda i: (i, 0, 0)),
                  full(w1ka), full(w1kb), full(w1va), full(w1vb), full(plo), full(phi), full(w2k_t), full(w2v_t),
                  full(kc_gain),
                  pl.BlockSpec((1, A_ROPE // 2, nc), lambda i: (i, 0, 0)),
                  pl.BlockSpec((1, A_ROPE // 2, nc), lambda i: (i, 0, 0))],
        out_specs=[pl.BlockSpec((1, A_GROUPS, nc, 128), lambda i: (i, 0, 0, 0)),
                   pl.BlockSpec((1, A_GROUPS, A_DIM, nc), lambda i: (i, 0, 0, 0))],
        out_shape=[jax.ShapeDtypeStruct((b, A_GROUPS, nc, 128), BF16),
                   jax.ShapeDtypeStruct((b, A_GROUPS, A_DIM, nc), BF16)],
        compiler_params=_params(1),
        name="nsa_compress",
    )(kc_chunks, vc_chunks, w1ka, w1kb, w1va, w1vb, plo, phi, w2k_t, w2v_t, kc_gain, cos_c, sin_c)


def _stack_heads(q, tq):
    qs = jnp.concatenate([q[r * A_DIM:(r + 1) * A_DIM, :] for r in range(A_REP)], axis=1)
    return jnp.concatenate([qs, jnp.zeros_like(qs)], axis=0)


def _nsa_cmp_kernel(q_ref, k_ref, v_ref, ov_ref, o_ref, sb_ref, *, tq):
    i = pl.program_id(2)
    n = A_REP * tq
    nc = k_ref.shape[2]
    nb = ov_ref.shape[0]
    qp = _stack_heads(q_ref[...], tq)
    s = _dot(k_ref[0, 0], qp)
    cmp_end = lax.broadcasted_iota(jnp.int32, (nc, n), 0) * CMP_STRIDE + (CMP_LEN - 1)
    tok = i * tq + (lax.broadcasted_iota(jnp.int32, (nc, n), 1) & (tq - 1))
    mask = cmp_end <= tok
    s = jnp.where(mask, s, NEG)
    m = jnp.max(s, axis=0, keepdims=True)
    p = jnp.where(mask, jnp.exp2(s - m), 0.0)
    l = jnp.sum(p, axis=0, keepdims=True)
    inv = jnp.where(l > 0.0, 1.0 / l, 0.0)
    pn = (p * inv).astype(BF16)
    oc = _dot(v_ref[0, 0], pn)
    for r in range(A_REP):
        o_ref[r * A_DIM:(r + 1) * A_DIM, :] = oc[:, r * tq:(r + 1) * tq]
    imp4 = _dot(ov_ref[...], pn)
    imp = imp4[:, 0:tq]
    for r in range(1, A_REP):
        imp = imp + imp4[:, r * tq:(r + 1) * tq]

    blk = lax.broadcasted_iota(jnp.int32, (nb, tq), 0)
    t = i * tq + lax.broadcasted_iota(jnp.int32, (nb, tq), 1)
    forced = (blk == (t >> 6)) | (blk == 0)
    v = jnp.where(forced, FORCE, jnp.where(blk * SLC_LEN <= t, imp, NEG))
    blk_f = blk.astype(F32)
    sel = jnp.zeros((nb, tq), F32)
    for _ in range(min(SLC_TOPK, nb)):
        mx = jnp.max(v, axis=0, keepdims=True)
        first = jnp.min(jnp.where(v == mx, blk_f, float(nb)), axis=0, keepdims=True)
        hit = blk_f == first
        sel = jnp.where(hit, 1.0, sel)
        v = jnp.where(hit, -jnp.inf, v)
    sb_ref[0] = jnp.where(sel > 0.0, 0.0, NEG)


def _nsa_cmp(q_t, kcmp, vcmp_t, ov_t, batch, seq, tq):
    nq = seq // tq
    nc = kcmp.shape[2]
    nb = ov_t.shape[0]
    t = q_t.shape[1]
    gr = A_REP * A_DIM
    return pl.pallas_call(
        functools.partial(_nsa_cmp_kernel, tq=tq),
        grid=(batch, A_GROUPS, nq),
        in_specs=[pl.BlockSpec((gr, tq), lambda b, g, i: (g, b * nq + i)),
                  pl.BlockSpec((1, 1, nc, 128), lambda b, g, i: (b, g, 0, 0)),
                  pl.BlockSpec((1, 1, A_DIM, nc), lambda b, g, i: (b, g, 0, 0)),
                  pl.BlockSpec((nb, nc), lambda b, g, i: (0, 0))],
        out_specs=[pl.BlockSpec((gr, tq), lambda b, g, i: (g, b * nq + i)),
                   pl.BlockSpec((1, nb, tq), lambda b, g, i: (g, 0, b * nq + i))],
        out_shape=[jax.ShapeDtypeStruct((A_HEADS * A_DIM, t), F32),
                   jax.ShapeDtypeStruct((A_GROUPS, nb, t), F32)],
        compiler_params=_params(3),
        name="nsa_cmp",
    )(q_t, kcmp, vcmp_t, ov_t)


ONES_ROWS = 16
FLASH_UNROLL = 4


def _with_ones(v):
    return jnp.concatenate([v, jnp.ones((ONES_ROWS, v.shape[1]), F32)], axis=0).astype(BF16)


def _flash_init(m_scr, acc_scr):
    m_scr[...] = jnp.full(m_scr.shape, -jnp.inf, F32)
    acc_scr[...] = jnp.zeros(acc_scr.shape, F32)


def _flash_update(s, v_t, m_scr, acc_scr):
    m_prev = m_scr[...]
    m_new = jnp.maximum(m_prev, jnp.max(s, axis=0, keepdims=True))
    alpha = jnp.exp2(m_prev - m_new)
    p = jnp.exp2(s - m_new)
    acc_scr[...] = alpha * acc_scr[...] + _dot(v_t, p.astype(BF16))
    m_scr[...] = m_new


def _flash_result(acc_scr, dv):
    acc = acc_scr[...]
    return acc[:dv] * (1.0 / acc[dv:dv + 1])


def _flash_causal(scores, values, mask, n_full, n_masked, s_scr, m_scr, acc_scr):
    unroll = s_scr.shape[0]
    s_scr[0] = scores(0)

    def trip(t, carry):
        j = unroll * t
        for u in range(unroll):
            s_scr[(u + 1) % unroll] = scores(j + u + 1)
            _flash_update(s_scr[u], values(j + u), m_scr, acc_scr)
        return carry

    lax.fori_loop(0, n_full // unroll, trip, 0)
    first = (n_full // unroll) * unroll
    for rest in range(unroll):

        @pl.when(n_full - first == rest)
        def _(rest=rest):
            for u in range(rest + n_masked):
                if u + 1 < rest + n_masked:
                    s_scr[(u + 1) % unroll] = scores(first + u + 1)
                s = s_scr[u % unroll]
                _flash_update(s if u < rest else mask(s, first + u), values(first + u), m_scr, acc_scr)


def _nsa_sel_kernel(q_ref, k_ref, v_ref, sb_ref, kw_ref, vw_ref, o_ref, ow_ref, qa_scr, s_scr, m_scr, acc_scr,
                    *, tq, tk):
    i = pl.program_id(2)
    n = A_REP * tq
    q = q_ref[...]
    qs = jnp.concatenate([q[r * A_DIM:(r + 1) * A_DIM, :] for r in range(A_REP)], axis=1)
    sb = sb_ref[0].astype(BF16)
    qa_scr[...] = jnp.concatenate([qs, jnp.zeros_like(qs), jnp.concatenate([sb] * A_REP, axis=1)], axis=0)
    _flash_init(m_scr, acc_scr)

    def scores(j):
        return _dot(k_ref[0, pl.ds(pl.multiple_of(j * tk, tk), tk), :], qa_scr[...])

    def causal(s, j):
        kpos = j * tk + lax.broadcasted_iota(jnp.int32, (tk, n), 0)
        tok = i * tq + (lax.broadcasted_iota(jnp.int32, (tk, n), 1) & (tq - 1))
        return jnp.where(kpos <= tok, s, NEG)

    _flash_causal(scores, lambda j: v_ref[0, j], causal, (i * tq) // tk, max(1, tq // tk), s_scr, m_scr, acc_scr)
    o = _flash_result(acc_scr, A_DIM)
    for r in range(A_REP):
        o_ref[r * A_DIM:(r + 1) * A_DIM, :] = o[:, r * tq:(r + 1) * tq]

    _flash_init(m_scr, acc_scr)
    n_back = WINDOW // tq
    for c in range(n_back + 1):
        kt = i - n_back + c

        @pl.when(kt >= 0)
        def _(kt=kt):
            s = _dot(kw_ref[0, pl.ds(pl.multiple_of(kt * tq, tq), tq), :], qa_scr[0:128, :])
            kpos = kt * tq + lax.broadcasted_iota(jnp.int32, (tq, n), 0)
            tok = i * tq + (lax.broadcasted_iota(jnp.int32, (tq, n), 1) & (tq - 1))
            s = jnp.where((kpos <= tok) & (kpos > tok - WINDOW), s, NEG)
            _flash_update(s, vw_ref[0, kt], m_scr, acc_scr)

    o = _flash_result(acc_scr, A_DIM)
    for r in range(A_REP):
        ow_ref[r * A_DIM:(r + 1) * A_DIM, :] = o[:, r * tq:(r + 1) * tq]


def _nsa_sel(q_t, k_aug, v_slc_tiles, selb, k_win, v_win_tiles, batch, seq, tq, tk):
    nq, nk = seq // tq, seq // tk
    nb = selb.shape[1]
    t = q_t.shape[1]
    gr = A_REP * A_DIM
    n = A_REP * tq
    kw = k_aug.shape[2]
    out = jax.ShapeDtypeStruct((A_HEADS * A_DIM, t), F32)
    return pl.pallas_call(
        functools.partial(_nsa_sel_kernel, tq=tq, tk=tk),
        grid=(batch, A_GROUPS, nq),
        in_specs=[pl.BlockSpec((gr, tq), lambda b, g, i: (g, b * nq + i)),
                  pl.BlockSpec((1, seq, kw), lambda b, g, i: (g, b, 0)),
                  pl.BlockSpec((1, nk, A_DIM + ONES_ROWS, tk), lambda b, g, i: (g, b, 0, 0)),
                  pl.BlockSpec((1, nb, tq), lambda b, g, i: (g, 0, b * nq + i)),
                  pl.BlockSpec((1, seq, 128), lambda b, g, i: (g, b, 0)),
                  pl.BlockSpec((1, nq, A_DIM + ONES_ROWS, tq), lambda b, g, i: (g, b, 0, 0))],
        out_specs=[pl.BlockSpec((gr, tq), lambda b, g, i: (g, b * nq + i)),
                   pl.BlockSpec((gr, tq), lambda b, g, i: (g, b * nq + i))],
        out_shape=[out, out],
        scratch_shapes=[pltpu.VMEM((kw, n), BF16), pltpu.VMEM((FLASH_UNROLL, tk, n), F32),
                        pltpu.VMEM((1, n), F32), pltpu.VMEM((A_DIM + ONES_ROWS, n), F32)],
        compiler_params=_params(3),
        name="nsa_sel",
    )(q_t, k_aug, v_slc_tiles, selb, k_win, v_win_tiles)


def _mla_prep_kernel(cq_ref, ckv_ref, kr_ref, cos_ref, sin_ref, qlg_ref, kvlg_ref, qg_ref, kg_ref, wuq_ref, wukv_ref,
                     qo_ref, ko_ref, vo_ref):
    cos, sin = cos_ref[...], sin_ref[...]
    tt = cos.shape[1]
    q_all = _dot(wuq_ref[...], (_rms_rows(cq_ref[...]) * qlg_ref[...]).astype(BF16))
    kv_all = _dot(wukv_ref[...], (_rms_rows(ckv_ref[...]) * kvlg_ref[...]).astype(BF16))
    kr = kr_ref[...]
    pad = jnp.zeros((128 - B_QK, tt), F32)
    for h in range(B_HEADS):
        y = _rms_rows(q_all[h * B_QK:(h + 1) * B_QK]) * qg_ref[...]
        y = _rope_rows(y, cos, sin, B_NOPE, B_ROPE // 2) * (B_QK ** -0.5 * LOG2E)
        qo_ref[h] = jnp.concatenate([y, pad], axis=0).astype(BF16)
        base = h * (B_NOPE + B_V)
        k = jnp.concatenate([kv_all[base:base + B_NOPE], kr], axis=0)
        y = _rope_rows(_rms_rows(k) * kg_ref[...], cos, sin, B_NOPE, B_ROPE // 2)
        ko_ref[h] = jnp.concatenate([y, pad], axis=0).T.astype(BF16)
        vo_ref[h, 0] = _with_ones(kv_all[base + B_NOPE:base + B_NOPE + B_V])


def _mla_attn_kernel(q_ref, k_ref, v_ref, o_ref, s_scr, m_scr, acc_scr, *, tq, tk):
    i = pl.program_id(2)
    _flash_init(m_scr, acc_scr)

    def scores(j):
        return _dot(k_ref[0, pl.ds(pl.multiple_of(j * tk, tk), tk), :], q_ref[0])

    def causal(s, j):
        kpos = j * tk + lax.broadcasted_iota(jnp.int32, (tk, tq), 0)
        tok = i * tq + lax.broadcasted_iota(jnp.int32, (tk, tq), 1)
        return jnp.where(kpos <= tok, s, NEG)

    _flash_causal(scores, lambda j: v_ref[0, j], causal, (i * tq) // tk, max(1, tq // tk), s_scr, m_scr, acc_scr)
    o_ref[...] = _flash_result(acc_scr, B_V)


def _mla_attn(q_m, k_m, v_m_tiles, batch, seq, tq, tk):
    nq, nk = seq // tq, seq // tk
    t = q_m.shape[2]
    return pl.pallas_call(
        functools.partial(_mla_attn_kernel, tq=tq, tk=tk),
        grid=(batch, B_HEADS, nq),
        in_specs=[pl.BlockSpec((1, 128, tq), lambda b, h, i: (h, 0, b * nq + i)),
                  pl.BlockSpec((1, seq, 128), lambda b, h, i: (h, b, 0)),
                  pl.BlockSpec((1, nk, B_V + ONES_ROWS, tk), lambda b, h, i: (h, b, 0, 0))],
        out_specs=pl.BlockSpec((B_V, tq), lambda b, h, i: (h, b * nq + i)),
        out_shape=jax.ShapeDtypeStruct((B_HEADS * B_V, t), F32),
        scratch_shapes=[pltpu.VMEM((FLASH_UNROLL, tk, tq), F32),
                        pltpu.VMEM((1, tq), F32), pltpu.VMEM((B_V + ONES_ROWS, tq), F32)],
        compiler_params=_params(3),
        name="mla_attn",
    )(q_m, k_m, v_m_tiles)


def _out_proj_kernel(oc_ref, os_ref, ow_ref, gt_ref, ob_ref, x_ref, ga_ref, gb_ref, w_ref, g2_ref, h_ref, hn_ref):
    heads = []
    for h in range(A_HEADS):
        rows = slice(h * A_DIM, (h + 1) * A_DIM)
        heads.append(gt_ref[3 * h:3 * h + 1, :] * oc_ref[rows, :] + gt_ref[3 * h + 1:3 * h + 2, :] * os_ref[rows, :]
                     + gt_ref[3 * h + 2:3 * h + 3, :] * ow_ref[rows, :])
    oa = _rms_rows(jnp.concatenate(heads, axis=0)) * ga_ref[...]
    ob = _rms_rows(ob_ref[...]) * gb_ref[...]
    cat = jnp.concatenate([oa, ob], axis=0).astype(BF16)
    hid = x_ref[...].T + _dot(w_ref[...], cat)
    h_ref[...] = hid
    hn_ref[...] = (_rms_rows(hid) * g2_ref[...]).astype(BF16)


def _out_proj(oc_t, os_t, ow_t, gates_t, ob_t, x2, gain_a, gain_b, w_out_t, gain2, tt):
    t, d = x2.shape
    aw = oc_t.shape[0]
    bw = ob_t.shape[0]
    tok = lambda rows: pl.BlockSpec((rows, tt), lambda i: (0, i))
    full = lambda a: pl.BlockSpec(a.shape, lambda i: (0,) * a.ndim)
    return pl.pallas_call(
        _out_proj_kernel,
        grid=(t // tt,),
        in_specs=[tok(aw), tok(aw), tok(aw), tok(GATE_ROWS), tok(bw), pl.BlockSpec((tt, d), lambda i: (i, 0)),
                  full(gain_a), full(gain_b), full(w_out_t), full(gain2)],
        out_specs=[tok(d), tok(d)],
        out_shape=[jax.ShapeDtypeStruct((d, t), F32), jax.ShapeDtypeStruct((d, t), BF16)],
        compiler_params=_params(1),
        name="out_proj",
    )(oc_t, os_t, ow_t, gates_t, ob_t, x2, gain_a, gain_b, w_out_t, gain2)


def _top_ranked(s):
    n, tp = s.shape
    row = lax.broadcasted_iota(jnp.int32, (n, tp), 0).astype(F32)
    slot = lax.broadcasted_iota(jnp.int32, (P_TOPK, tp), 0)

    def body(a, carry):
        v, rank, vals = carry
        mx = jnp.max(v, axis=0, keepdims=True)
        first = jnp.min(jnp.where(v == mx, row, float(n)), axis=0, keepdims=True)
        hit = row == first
        rank = jnp.where(hit, jnp.asarray(a, F32), rank)
        v = jnp.where(hit, -jnp.inf, v)
        vals = jnp.where(slot == a, mx, vals)
        return v, rank, vals

    _, rank, vals = lax.fori_loop(0, P_TOPK, body,
                                  (s, jnp.full((n, tp), float(P_TOPK), F32), jnp.zeros((P_TOPK, tp), F32)))
    return rank, vals


def _pair_counts(v1, v2):
    k = v1.shape[0]
    slot = lax.broadcasted_iota(jnp.int32, v1.shape, 0).astype(F32)
    top = v1[0:1] + v2[0:1]

    def body(_, carry):
        count, front, z = carry
        mx = jnp.max(front, axis=0, keepdims=True)
        a_star = jnp.min(jnp.where(front == mx, slot, float(k)), axis=0, keepdims=True)
        hit = slot == a_star
        count = count + jnp.where(hit, 1.0, 0.0)
        nxt = jnp.sum(jnp.where(hit, count, 0.0), axis=0, keepdims=True)
        v2_nxt = jnp.sum(jnp.where(slot == nxt, v2, 0.0), axis=0, keepdims=True)
        front = jnp.where(hit, jnp.where(nxt < float(k), v1 + v2_nxt, -jnp.inf), front)
        return count, front, z + jnp.exp(mx - top)

    count, _, z = lax.fori_loop(0, k, body, (jnp.zeros(v1.shape, F32), v1 + v2[0:1], jnp.zeros(top.shape, F32)))
    return count, z


REMOVED = -2.0 ** 126
LANES = 128


def _top_ranked_pair_fast(s1, s2):
    n, tp = s1.shape
    slot = lax.broadcasted_iota(jnp.int32, (P_TOPK, tp), 0)

    def body(a, carry):
        code = REMOVED * (1.0 + jnp.asarray(a, F32) * (1.0 / 32.0))
        out = []
        for key, vals in (carry[0:2], carry[2:4]):
            mx = jnp.max(key, axis=0, keepdims=True)
            key = jnp.where(key == mx, code, key)
            out += [key, jnp.where(slot == a, mx, vals)]
        return tuple(out)

    zeros = jnp.zeros((P_TOPK, tp), F32)
    k1, t1, k2, t2 = lax.fori_loop(0, P_TOPK, body, (s1, zeros, s2, zeros))
    res, bad = [], jnp.zeros((1, tp), F32)
    for key, vals in ((k1, t1), (k2, t2)):
        removed = key <= REMOVED
        rank = jnp.where(removed, (key * (1.0 / REMOVED) - 1.0) * 32.0, float(P_TOPK))
        n_removed = jnp.sum(jnp.where(removed, 1.0, 0.0), axis=0, keepdims=True)
        bad = jnp.maximum(bad, jnp.abs(n_removed - float(P_TOPK)))
        res += [rank, vals]
    return res[0], res[1], res[2], res[3], bad


def _peer_route_kernel(hn_ref, wq_ref, keys_ref, cut_ref, g1_ref, r2_ref, g2_ref, q_scr, s_scr, rank_scr, vals_scr):
    tp = hn_ref.shape[1]
    half = P_KEY_DIM // 2
    q_scr[...] = _dot(wq_ref[...], hn_ref[...])
    for h in range(P_HEADS):
        q = _rms_rows(q_scr[h * P_KEY_DIM:(h + 1) * P_KEY_DIM, :]).astype(BF16)
        s_scr[0, h] = _dot(keys_ref[h, 0], q[:half])
        s_scr[1, h] = _dot(keys_ref[h, 1], q[half:])

    def first_level(rank_pair, flagged):
        for h in range(P_HEADS):
            for lt in range(tp // LANES):
                lanes = slice(lt * LANES, (lt + 1) * LANES)
                rank1, vals1, rank2, vals2, bad = rank_pair(s_scr[0, h, :, lanes], s_scr[1, h, :, lanes])
                flagged = jnp.maximum(flagged, bad)
                rank_scr[0, h, :, lanes] = rank1
                rank_scr[1, h, :, lanes] = rank2
                for a in range(P_TOPK):
                    vals_scr[0, a, h:h + 1, lanes] = vals1[a:a + 1]
                    vals_scr[1, a, h:h + 1, lanes] = vals2[a:a + 1]
        return flagged

    flagged = first_level(_top_ranked_pair_fast, jnp.zeros((1, LANES), F32))

    @pl.when(jnp.max(flagged) > 0.0)
    def _():
        first_level(lambda s1, s2: (*_top_ranked(s1), *_top_ranked(s2), jnp.zeros((1, LANES), F32)), flagged)

    count, z = _pair_counts(vals_scr[0], vals_scr[1])
    for h in range(P_HEADS):
        rank1 = rank_scr[0, h]
        cut = jnp.zeros_like(rank1)
        for a in range(P_TOPK):
            cut = jnp.where(rank1 == float(a), count[a, h:h + 1, :], cut)
        cut_ref[h] = cut.astype(BF16)
        g1_ref[h] = jnp.exp(s_scr[0, h] - vals_scr[0, 0, h:h + 1, :]).astype(BF16)
        r2_ref[h] = rank_scr[1, h].astype(BF16)
        g2_ref[h] = (jnp.exp(s_scr[1, h] - vals_scr[1, 0, h:h + 1, :]) * (1.0 / z[0, h:h + 1, :])).astype(BF16)


def _peer_route(hn_t, wq_t, sub_keys, tp):
    d, t = hn_t.shape
    halfs = jax.ShapeDtypeStruct((P_HEADS, N_KEYS, t), BF16)
    ospec = pl.BlockSpec((P_HEADS, N_KEYS, tp), lambda i: (0, 0, i))
    return pl.pallas_call(
        _peer_route_kernel,
        grid=(t // tp,),
        in_specs=[pl.BlockSpec((d, tp), lambda i: (0, i)),
                  pl.BlockSpec(wq_t.shape, lambda i: (0, 0)),
                  pl.BlockSpec(sub_keys.shape, lambda i: (0, 0, 0, 0))],
        out_specs=[ospec, ospec, ospec, ospec],
        out_shape=[halfs, halfs, halfs, halfs],
        scratch_shapes=[pltpu.VMEM((P_HEADS * P_KEY_DIM, tp), F32), pltpu.VMEM((2, P_HEADS, N_KEYS, tp), F32),
                        pltpu.VMEM((2, P_HEADS, N_KEYS, tp), F32), pltpu.VMEM((2, P_TOPK, P_HEADS, tp), F32)],
        compiler_params=_params(1),
        name="peer_route",
    )(hn_t, wq_t, sub_keys)


def _peer_ffn_kernel(hn_ref, h_ref, u_ref, v_ref, cut_ref, g1_ref, r2_ref, g2_ref, o_ref, acc_scr, *, te):
    e = pl.program_id(1)

    @pl.when(e == 0)
    def _():
        acc_scr[...] = jnp.zeros(acc_scr.shape, F32)

    hn = hn_ref[...]
    tt = hn.shape[1]

    chunk = 2 * N_KEYS
    tl = PEER_LANE_GROUP
    for l0 in range(0, tt, tl):
        lanes = slice(l0, l0 + tl)
        weights = []
        for c in range(te // chunk):
            a = _dot(u_ref[c * chunk:(c + 1) * chunk, :], hn[:, lanes])
            for k in range(chunk // N_KEYS):
                ii = c * (chunk // N_KEYS) + k
                gate = jnp.zeros((N_KEYS, tl), BF16)
                for h in range(P_HEADS):
                    chosen = r2_ref[h, :, lanes] < jnp.broadcast_to(cut_ref[h, ii:ii + 1, lanes], (N_KEYS, tl))
                    gate = gate + (jnp.where(chosen, g2_ref[h, :, lanes], jnp.zeros_like(gate))
                                   * jnp.broadcast_to(g1_ref[h, ii:ii + 1, lanes], (N_KEYS, tl)))
                weights.append(gate * _gelu_tanh(a[k * N_KEYS:(k + 1) * N_KEYS, :]).astype(BF16))
        acc_scr[:, lanes] += _dot(v_ref[...], jnp.concatenate(weights, axis=0))

    @pl.when(e == pl.num_programs(1) - 1)
    def _():
        o_ref[...] = (h_ref[...] + acc_scr[...]).T


def _peer_ffn(hn_t, h_t, u_bf, v_t_bf, cut_k, g1_k, r2, g2, tt, te):
    d, t = hn_t.shape
    n_exp = u_bf.shape[0]
    kpe = te // N_KEYS
    return pl.pallas_call(
        functools.partial(_peer_ffn_kernel, te=te),
        grid=(t // tt, n_exp // te),
        in_specs=[pl.BlockSpec((d, tt), lambda i, e: (0, i)),
                  pl.BlockSpec((d, tt), lambda i, e: (0, i)),
                  pl.BlockSpec((te, d), lambda i, e: (e, 0)),
                  pl.BlockSpec((d, te), lambda i, e: (0, e)),
                  pl.BlockSpec((P_HEADS, kpe, tt), lambda i, e: (0, e, i)),
                  pl.BlockSpec((P_HEADS, kpe, tt), lambda i, e: (0, e, i)),
                  pl.BlockSpec((P_HEADS, N_KEYS, tt), lambda i, e: (0, 0, i)),
                  pl.BlockSpec((P_HEADS, N_KEYS, tt), lambda i, e: (0, 0, i))],
        out_specs=pl.BlockSpec((tt, d), lambda i, e: (i, 0)),
        out_shape=jax.ShapeDtypeStruct((t, d), F32),
        scratch_shapes=[pltpu.VMEM((d, tt), F32)],
        compiler_params=_params(2),
        name="peer_ffn",
    )(hn_t, h_t, u_bf, v_t_bf, cut_k, g1_k, r2, g2)


def _rope_tables_t(pos_flat, rot_dim):
    inv_freq = ROPE_THETA ** (-jnp.arange(0, rot_dim, 2, dtype=F32) / rot_dim)
    ang = pos_flat.astype(F32)[None, :] * inv_freq[:, None]
    return jnp.cos(ang), jnp.sin(ang)


def _expand_cmp_w1(w1):
    w = w1.reshape(CMP_LEN, A_DIM, CMP_HIDDEN)
    out = []
    for part in (w[:CMP_STRIDE], w[CMP_STRIDE:]):
        z = jnp.zeros_like(part)
        both = jnp.stack([jnp.concatenate([part, z], axis=1), jnp.concatenate([z, part], axis=1)])
        out.append(both.reshape(A_GROUPS, CMP_STRIDE * A_GROUPS * A_DIM, CMP_HIDDEN).transpose(0, 2, 1).astype(BF16))
    return out


TOKEN_TILE = 512
NSA_Q_TILE = 256
NSA_SEL_K_TILE = 512
MLA_Q_TILE = 1024
MLA_K_TILE = 512
PEER_ROUTE_TILE = 256
PEER_EXPERT_TILE = 2048
PEER_LANE_GROUP = 512


def _col(v):
    return v.reshape(-1, 1).astype(F32)


def _mixers(x, positions, norm1_gain, w_in, nsa_q_gain, nsa_kc_gain, nsa_ks_gain, nsa_kw_gain,
            cmp_pos, cmp_k_w1, cmp_k_w2, cmp_v_w1, cmp_v_w2,
            mla_q_lora_gain, mla_w_uq, mla_kv_lora_gain, mla_w_ukv, mla_q_gain, mla_k_gain):
    batch, seq, d = x.shape
    t = batch * seq
    tt = TOKEN_TILE
    tq_nsa = NSA_Q_TILE
    tk_sel = NSA_SEL_K_TILE
    tq_mla, tk_mla = MLA_Q_TILE, MLA_K_TILE
    assert tk_sel == tt and tk_mla == tt
    assert d == D_MODEL and seq % 512 == 0 and seq // SLC_LEN >= SLC_TOPK and WINDOW % tq_nsa == 0
    col = _col
    x2 = x.reshape(t, d)
    pos = positions.reshape(t)

    w_in_t = w_in.T
    gate_lo = sum((512, 128, 128, 128, 128, 128, 128))
    gate_hi = gate_lo + 3 * A_HEADS
    w_in_t = jnp.concatenate([w_in_t[:gate_lo], w_in_t[gate_hi:], w_in_t[gate_lo:gate_hi],
                              jnp.zeros((PROJ_ROWS - w_in_t.shape[0], d), F32)], axis=0).astype(BF16)
    cos_a, sin_a = _rope_tables_t(pos, A_ROPE)
    cos_b, sin_b = _rope_tables_t(pos, B_ROPE)
    q_t, kc_tm, vc_tm, k_slc, v_slc_t, k_win, v_win_t, gates_t, q_m, k_m, v_m_t = _in_proj(
        x2, norm1_gain.reshape(1, d), w_in_t, cos_a, sin_a, col(nsa_q_gain), col(nsa_ks_gain), col(nsa_kw_gain),
        cos_b, sin_b, col(mla_q_lora_gain), col(mla_kv_lora_gain), col(mla_q_gain), col(mla_k_gain),
        mla_w_uq.T.astype(BF16), mla_w_ukv.T.astype(BF16), tt, seq, tq_nsa)

    nc = seq // CMP_STRIDE
    chunk_w = CMP_STRIDE * A_GROUPS * A_DIM
    w1ka, w1kb = _expand_cmp_w1(cmp_k_w1)
    w1va, w1vb = _expand_cmp_w1(cmp_v_w1)
    pos_rows = lambda p: jnp.broadcast_to(p[:, None, :], (CMP_STRIDE, A_GROUPS, A_DIM)).reshape(1, chunk_w)
    cmp_end = jnp.minimum(jnp.arange(nc) * CMP_STRIDE + CMP_LEN - 1, seq - 1)
    cos_c, sin_c = _rope_tables_t(positions[:, cmp_end].reshape(-1), A_ROPE)
    to_b = lambda a: a.reshape(A_ROPE // 2, batch, nc).transpose(1, 0, 2)
    kcmp, vcmp_t = _compress(kc_tm.reshape(batch, nc, chunk_w), vc_tm.reshape(batch, nc, chunk_w),
                             w1ka, w1kb, w1va, w1vb, pos_rows(cmp_pos[:CMP_STRIDE]), pos_rows(cmp_pos[CMP_STRIDE:]),
                             cmp_k_w2.T.astype(BF16), cmp_v_w2.T.astype(BF16), col(nsa_kc_gain), to_b(cos_c), to_b(sin_c))

    n_cmp = (seq - CMP_LEN) // CMP_STRIDE + 1
    nb = seq // SLC_LEN
    c_start = np.arange(nc)[None, :] * CMP_STRIDE
    s_start = np.arange(nb)[:, None] * SLC_LEN
    ov = (c_start < s_start + SLC_LEN) & (c_start + CMP_LEN - 1 >= s_start) & (np.arange(nc)[None, :] < n_cmp)
    ov_t = jnp.asarray(ov.astype(np.float32)).astype(BF16)

    oc_t, selb = _nsa_cmp(q_t, kcmp, vcmp_t, ov_t, batch, seq, tq_nsa)
    os_t, ow_t = _nsa_sel(q_t, k_slc, v_slc_t, selb, k_win, v_win_t, batch, seq, tq_nsa, tk_sel)

    ob_t = _mla_attn(q_m, k_m, v_m_t, batch, seq, tq_mla, tk_mla)
    return oc_t, os_t, ow_t, gates_t, ob_t


def _peer(hn_t, h_t, peer_w_q, peer_sub_keys, peer_u, peer_v):
    cut, g1, r2, g2 = _peer_route(hn_t, peer_w_q.T.astype(BF16), peer_sub_keys.astype(BF16), PEER_ROUTE_TILE)
    return _peer_ffn(hn_t, h_t, peer_u.astype(BF16), peer_v.T.astype(BF16),
                     cut, g1, r2, g2, TOKEN_TILE, PEER_EXPERT_TILE)


def _layer(x, positions, norm1_gain, w_in, nsa_q_gain, nsa_kc_gain, nsa_ks_gain, nsa_kw_gain,
           cmp_pos, cmp_k_w1, cmp_k_w2, cmp_v_w1, cmp_v_w2,
           mla_q_lora_gain, mla_w_uq, mla_kv_lora_gain, mla_w_ukv, mla_q_gain, mla_k_gain,
           out_gain_a, out_gain_b, w_out, norm2_gain, peer_w_q, peer_sub_keys, peer_u, peer_v):
    batch, seq, d = x.shape
    oc_t, os_t, ow_t, gates_t, ob_t = _mixers(
        x, positions, norm1_gain, w_in, nsa_q_gain, nsa_kc_gain, nsa_ks_gain, nsa_kw_gain,
        cmp_pos, cmp_k_w1, cmp_k_w2, cmp_v_w1, cmp_v_w2,
        mla_q_lora_gain, mla_w_uq, mla_kv_lora_gain, mla_w_ukv, mla_q_gain, mla_k_gain)
    h_t, hn_t = _out_proj(oc_t, os_t, ow_t, gates_t, ob_t, x.reshape(batch * seq, d), _col(out_gain_a), _col(out_gain_b),
                          w_out.T.astype(BF16), _col(norm2_gain), TOKEN_TILE)
    return _peer(hn_t, h_t, peer_w_q, peer_sub_keys, peer_u, peer_v).reshape(batch, seq, d)


def kernel(x, positions, norm1_gain, w_in, nsa_q_gain, nsa_kc_gain, nsa_ks_gain, nsa_kw_gain, cmp_pos, cmp_k_w1, cmp_k_w2, cmp_v_w1, cmp_v_w2, mla_q_lora_gain, mla_w_uq, mla_kv_lora_gain, mla_w_ukv, mla_q_gain, mla_k_gain, out_gain_a, out_gain_b, w_out, norm2_gain, peer_w_q, peer_sub_keys, peer_u, peer_v):
    h = x
    for l in range(norm1_gain.shape[0]):
        h = _layer(h, positions, norm1_gain[l], w_in[l], nsa_q_gain[l], nsa_kc_gain[l], nsa_ks_gain[l], nsa_kw_gain[l],
                   cmp_pos[l], cmp_k_w1[l], cmp_k_w2[l], cmp_v_w1[l], cmp_v_w2[l],
                   mla_q_lora_gain[l], mla_w_uq[l], mla_kv_lora_gain[l], mla_w_ukv[l], mla_q_gain[l], mla_k_gain[l],
                   out_gain_a[l], out_gain_b[l], w_out[l], norm2_gain[l], peer_w_q[l], peer_sub_keys[l],
                   peer_u[l], peer_v[l])
    return h
```

```python
import functools

import jax
import jax.numpy as jnp
import numpy as np
from jax import lax
from jax.experimental import pallas as pl
from jax.experimental.pallas import tpu as pltpu

F32, BF16 = jnp.float32, jnp.bfloat16
EPS = 1e-6
NEG = -1e30
FORCE = 1e9
ROPE_THETA = 500000.0
LOG2E = 1.4426950408889634

D_MODEL = 1024
A_HEADS, A_GROUPS, A_DIM = 8, 2, 64
A_REP = A_HEADS // A_GROUPS
A_ROPE = A_DIM // 4
CMP_LEN, CMP_STRIDE, CMP_HIDDEN = 32, 16, 256
SLC_LEN, SLC_TOPK, WINDOW = 64, 16, 512
B_HEADS, Q_LORA, KV_LORA, B_NOPE, B_ROPE, B_V = 8, 256, 128, 64, 32, 64
B_QK = B_NOPE + B_ROPE
P_HEADS, N_KEYS, P_KEY_DIM, P_TOPK = 8, 128, 256, 16
N_EXPERTS = N_KEYS * N_KEYS

ROW_Q, ROW_KC, ROW_VC, ROW_KS, ROW_VS, ROW_KW, ROW_VW = 0, 512, 640, 768, 896, 1024, 1152
ROW_CQ, ROW_CKV, ROW_KR, ROW_GATE, PROJ_ROWS = 1280, 1536, 1664, 1696, 1728
GATE_ROWS = 32

VMEM_LIMIT = 56 * 1024 * 1024
NT_DIMS = (((1,), (1,)), ((), ()))


def _params(n_axes):
    return pltpu.CompilerParams(dimension_semantics=("arbitrary",) * n_axes, vmem_limit_bytes=VMEM_LIMIT)


def _dot(a, b):
    return jnp.dot(a, b, preferred_element_type=F32)


def _dot_nt(a, b):
    return lax.dot_general(a, b, NT_DIMS, preferred_element_type=F32)


def _row_sumsq(x):
    sq = x * x
    hi = sq.astype(BF16)
    lo = (sq - hi.astype(F32)).astype(BF16)
    ones = jnp.ones((8, x.shape[1]), BF16)
    return (_dot_nt(ones, hi) + _dot_nt(ones, lo))[0:1, :]


def _rms_rows(x):
    ss = jnp.sum(x * x, axis=0, keepdims=True)
    return x * lax.rsqrt(ss * (1.0 / x.shape[0]) + EPS)


def _rope_rows(y, cos, sin, off, half):
    x1, x2 = y[off:off + half], y[off + half:off + 2 * half]
    parts = [y[:off]] if off else []
    parts += [x1 * cos - x2 * sin, x2 * cos + x1 * sin]
    if off + 2 * half < y.shape[0]:
        parts.append(y[off + 2 * half:])
    return jnp.concatenate(parts, axis=0)


def _gelu_tanh(x):
    c = 0.7978845608028654
    half = 0.5 * x
    return half + half * jnp.tanh(x * (c + (c * 0.044715) * (x * x)))


def _in_proj_kernel(x_ref, g_ref, w_ref, cos_a_ref, sin_a_ref, qg_ref, ksg_ref, kwg_ref,
                    cos_b_ref, sin_b_ref, qlg_ref, kvlg_ref, mqg_ref, mkg_ref, wuq_ref, wukv_ref, *out_refs, seq, tw):
    x = x_ref[...]
    xg = (x * g_ref[...]).astype(BF16)
    p = _dot_nt(w_ref[...], xg)
    p = p * lax.rsqrt(_row_sumsq(x) * (1.0 / x.shape[1]) + EPS)
    gw = A_GROUPS * A_DIM
    rows = lambda start, n: p[start:start + n]
    _nsa_prep_kernel(rows(ROW_Q, A_HEADS * A_DIM), rows(ROW_KC, gw), rows(ROW_VC, gw), rows(ROW_KS, gw), rows(ROW_VS, gw),
                     rows(ROW_KW, gw), rows(ROW_VW, gw), rows(ROW_GATE, GATE_ROWS), cos_a_ref, sin_a_ref,
                     qg_ref, ksg_ref, kwg_ref, *out_refs[:8], seq=seq, tw=tw)
    _mla_prep_kernel(rows(ROW_CQ, Q_LORA), rows(ROW_CKV, KV_LORA), rows(ROW_KR, B_ROPE), cos_b_ref, sin_b_ref,
                     qlg_ref, kvlg_ref, mqg_ref, mkg_ref, wuq_ref, wukv_ref, *out_refs[8:])


def _in_proj(x2, gain, w_t, cos_a, sin_a, q_gain, ks_gain, kw_gain, cos_b, sin_b, q_lora_gain, kv_lora_gain,
             mq_gain, mk_gain, wuq_t, wukv_t, tt, seq, tw):
    t, d = x2.shape
    gw = A_GROUPS * A_DIM
    kw = 128 + seq // SLC_LEN
    va, vb = A_DIM + ONES_ROWS, B_V + ONES_ROWS
    full = lambda a: pl.BlockSpec(a.shape, lambda i: (0,) * a.ndim)
    lanes = lambda n: pl.BlockSpec((n, tt), lambda i: (0, i))
    return pl.pallas_call(
        functools.partial(_in_proj_kernel, seq=seq, tw=tw),
        grid=(t // tt,),
        in_specs=[pl.BlockSpec((tt, d), lambda i: (i, 0)), full(gain), full(w_t),
                  lanes(A_ROPE // 2), lanes(A_ROPE // 2), full(q_gain), full(ks_gain), full(kw_gain),
                  lanes(B_ROPE // 2), lanes(B_ROPE // 2), full(q_lora_gain), full(kv_lora_gain), full(mq_gain),
                  full(mk_gain), full(wuq_t), full(wukv_t)],
        out_specs=[lanes(A_HEADS * A_DIM),
                   pl.BlockSpec((tt, gw), lambda i: (i, 0)),
                   pl.BlockSpec((tt, gw), lambda i: (i, 0)),
                   pl.BlockSpec((A_GROUPS, tt, kw), lambda i: (0, i, 0)),
                   pl.BlockSpec((A_GROUPS, 1, va, tt), lambda i: (0, i, 0, 0)),
                   pl.BlockSpec((A_GROUPS, tt, 128), lambda i: (0, i, 0)),
                   pl.BlockSpec((A_GROUPS, tt // tw, va, tw), lambda i: (0, i, 0, 0)),
                   lanes(GATE_ROWS),
                   pl.BlockSpec((B_HEADS, 128, tt), lambda i: (0, 0, i)),
                   pl.BlockSpec((B_HEADS, tt, 128), lambda i: (0, i, 0)),
                   pl.BlockSpec((B_HEADS, 1, vb, tt), lambda i: (0, i, 0, 0))],
        out_shape=[jax.ShapeDtypeStruct((A_HEADS * A_DIM, t), BF16),
                   jax.ShapeDtypeStruct((t, gw), F32),
                   jax.ShapeDtypeStruct((t, gw), F32),
                   jax.ShapeDtypeStruct((A_GROUPS, t, kw), BF16),
                   jax.ShapeDtypeStruct((A_GROUPS, t // tt, va, tt), BF16),
                   jax.ShapeDtypeStruct((A_GROUPS, t, 128), BF16),
                   jax.ShapeDtypeStruct((A_GROUPS, t // tw, va, tw), BF16),
                   jax.ShapeDtypeStruct((GATE_ROWS, t), F32),
                   jax.ShapeDtypeStruct((B_HEADS, 128, t), BF16),
                   jax.ShapeDtypeStruct((B_HEADS, t, 128), BF16),
                   jax.ShapeDtypeStruct((B_HEADS, t // tt, vb, tt), BF16)],
        compiler_params=_params(1),
        name="in_proj",
    )(x2, gain, w_t, cos_a, sin_a, q_gain, ks_gain, kw_gain, cos_b, sin_b, q_lora_gain, kv_lora_gain,
      mq_gain, mk_gain, wuq_t, wukv_t)


def _nsa_prep_kernel(q_ref, kc_ref, vc_ref, ks_ref, vs_ref, kw_ref, vw_ref, gt_ref, cos_ref, sin_ref,
                     qg_ref, ksg_ref, kwg_ref,
                     qo_ref, kco_ref, vco_ref, kso_ref, vso_ref, kwo_ref, vwo_ref, gto_ref, *, seq, tw):
    cos, sin = cos_ref[...], sin_ref[...]
    tt = cos.shape[1]
    nb = seq // SLC_LEN
    for h in range(A_HEADS):
        y = _rms_rows(q_ref[h * A_DIM:(h + 1) * A_DIM, :]) * qg_ref[...]
        y = _rope_rows(y, cos, sin, 0, A_ROPE // 2) * (A_DIM ** -0.5 * LOG2E)
        qo_ref[h * A_DIM:(h + 1) * A_DIM, :] = y.astype(BF16)
    kco_ref[...] = kc_ref[...].T
    vco_ref[...] = vc_ref[...].T
    zeros = jnp.zeros((A_DIM, tt), F32)
    tok = pl.program_id(0) * tt + lax.broadcasted_iota(jnp.int32, (tt, nb), 0)
    block_hot = jnp.where(lax.broadcasted_iota(jnp.int32, (tt, nb), 1) == (tok % seq) // SLC_LEN, 1.0, 0.0)
    for g in range(A_GROUPS):
        for src, gain, dst in ((ks_ref, ksg_ref, kso_ref), (kw_ref, kwg_ref, kwo_ref)):
            y = _rms_rows(src[g * A_DIM:(g + 1) * A_DIM, :]) * gain[...]
            y = _rope_rows(y, cos, sin, 0, A_ROPE // 2)
            k_tm = jnp.concatenate([y, zeros], axis=0).T
            if dst is kso_ref:
                k_tm = jnp.concatenate([k_tm, block_hot], axis=1)
            dst[g] = k_tm.astype(BF16)
        vso_ref[g, 0] = _with_ones(vs_ref[g * A_DIM:(g + 1) * A_DIM, :])
        v_win = _with_ones(vw_ref[g * A_DIM:(g + 1) * A_DIM, :])
        for c in range(tt // tw):
            vwo_ref[g, c] = v_win[:, c * tw:(c + 1) * tw]
    gto_ref[...] = 1.0 / (1.0 + jnp.exp(-gt_ref[...]))


def _compress_kernel(kc_ref, vc_ref, w1ka_ref, w1kb_ref, w1va_ref, w1vb_ref, plo_ref, phi_ref,
                     w2k_ref, w2v_ref, kg_ref, cos_ref, sin_ref, ko_ref, vo_ref):
    nc = kc_ref.shape[1]
    zeros = jnp.zeros((A_DIM, nc), F32)
    for src, w1a, w1b, w2, is_k in ((kc_ref, w1ka_ref, w1kb_ref, w2k_ref, True),
                                    (vc_ref, w1va_ref, w1vb_ref, w2v_ref, False)):
        x = src[0]
        xlo = (x + plo_ref[...]).astype(BF16)
        xhi = (x + phi_ref[...]).astype(BF16)
        for g in range(A_GROUPS):
            first = _dot_nt(w1a[g], xlo)
            second = _dot_nt(w1b[g], xhi)
            hid = _gelu_tanh(first + pltpu.roll(second, nc - 1, axis=1)).astype(BF16)
            c = _dot(w2[...], hid)
            if is_k:
                y = _rope_rows(_rms_rows(c) * kg_ref[...], cos_ref[0], sin_ref[0], 0, A_ROPE // 2)
                ko_ref[0, g] = jnp.concatenate([y, zeros], axis=0).T.astype(BF16)
            else:
                vo_ref[0, g] = c.astype(BF16)


def _compress(kc_chunks, vc_chunks, w1ka, w1kb, w1va, w1vb, plo, phi, w2k_t, w2v_t, kc_gain, cos_c, sin_c):
    b, nc, cw = kc_chunks.shape
    full = lambda a: pl.BlockSpec(a.shape, lambda i: (0,) * a.ndim)
    return pl.pallas_call(
        _compress_kernel,
        grid=(b,),
        in_specs=[pl.BlockSpec((1, nc, cw), lambda i: (i, 0, 0)), pl.BlockSpec((1, nc, cw), lambda i: (i, 0, 0)),
                  full(w1ka), full(w1kb), full(w1va), full(w1vb), full(plo), full(phi), full(w2k_t), full(w2v_t),
                  full(kc_gain),
                  pl.BlockSpec((1, A_ROPE // 2, nc), lambda i: (i, 0, 0)),
                  pl.BlockSpec((1, A_ROPE // 2, nc), lambda i: (i, 0, 0))],
        out_specs=[pl.BlockSpec((1, A_GROUPS, nc, 128), lambda i: (i, 0, 0, 0)),
                   pl.BlockSpec((1, A_GROUPS, A_DIM, nc), lambda i: (i, 0, 0, 0))],
        out_shape=[jax.ShapeDtypeStruct((b, A_GROUPS, nc, 128), BF16),
                   jax.ShapeDtypeStruct((b, A_GROUPS, A_DIM, nc), BF16)],
        compiler_params=_params(1),
        name="nsa_compress",
    )(kc_chunks, vc_chunks, w1ka, w1kb, w1va, w1vb, plo, phi, w2k_t, w2v_t, kc_gain, cos_c, sin_c)


def _stack_heads(q, tq):
    qs = jnp.concatenate([q[r * A_DIM:(r + 1) * A_DIM, :] for r in range(A_REP)], axis=1)
    return jnp.concatenate([qs, jnp.zeros_like(qs)], axis=0)


def _nsa_cmp_kernel(q_ref, k_ref, v_ref, ov_ref, o_ref, sb_ref, *, tq):
    i = pl.program_id(2)
    n = A_REP * tq
    nc = k_ref.shape[2]
    nb = ov_ref.shape[0]
    qp = _stack_heads(q_ref[...], tq)
    s = _dot(k_ref[0, 0], qp)
    cmp_end = lax.broadcasted_iota(jnp.int32, (nc, n), 0) * CMP_STRIDE + (CMP_LEN - 1)
    tok = i * tq + (lax.broadcasted_iota(jnp.int32, (nc, n), 1) & (tq - 1))
    mask = cmp_end <= tok
    s = jnp.where(mask, s, NEG)
    m = jnp.max(s, axis=0, keepdims=True)
    p = jnp.where(mask, jnp.exp2(s - m), 0.0)
    l = jnp.sum(p, axis=0, keepdims=True)
    inv = jnp.where(l > 0.0, 1.0 / l, 0.0)
    pn = (p * inv).astype(BF16)
    oc = _dot(v_ref[0, 0], pn)
    for r in range(A_REP):
        o_ref[r * A_DIM:(r + 1) * A_DIM, :] = oc[:, r * tq:(r + 1) * tq]
    imp4 = _dot(ov_ref[...], pn)
    imp = imp4[:, 0:tq]
    for r in range(1, A_REP):
        imp = imp + imp4[:, r * tq:(r + 1) * tq]

    blk = lax.broadcasted_iota(jnp.int32, (nb, tq), 0)
    t = i * tq + lax.broadcasted_iota(jnp.int32, (nb, tq), 1)
    forced = (blk == t // SLC_LEN) | (blk == 0)
    v = jnp.where(forced, FORCE, jnp.where(blk * SLC_LEN <= t, imp, NEG))
    blk_f = blk.astype(F32)
    sel = jnp.zeros((nb, tq), F32)
    for _ in range(min(SLC_TOPK, nb)):
        mx = jnp.max(v, axis=0, keepdims=True)
        first = jnp.min(jnp.where(v == mx, blk_f, float(nb)), axis=0, keepdims=True)
        hit = blk_f == first
        sel = jnp.where(hit, 1.0, sel)
        v = jnp.where(hit, -jnp.inf, v)
    sb_ref[0] = jnp.where(sel > 0.0, 0.0, NEG)


def _nsa_cmp(q_t, kcmp, vcmp_t, ov_t, batch, seq, tq):
    nq = seq // tq
    nc = kcmp.shape[2]
    nb = ov_t.shape[0]
    t = q_t.shape[1]
    gr = A_REP * A_DIM
    return pl.pallas_call(
        functools.partial(_nsa_cmp_kernel, tq=tq),
        grid=(batch, A_GROUPS, nq),
        in_specs=[pl.BlockSpec((gr, tq), lambda b, g, i: (g, b * nq + i)),
                  pl.BlockSpec((1, 1, nc, 128), lambda b, g, i: (b, g, 0, 0)),
                  pl.BlockSpec((1, 1, A_DIM, nc), lambda b, g, i: (b, g, 0, 0)),
                  pl.BlockSpec((nb, nc), lambda b, g, i: (0, 0))],
        out_specs=[pl.BlockSpec((gr, tq), lambda b, g, i: (g, b * nq + i)),
                   pl.BlockSpec((1, nb, tq), lambda b, g, i: (g, 0, b * nq + i))],
        out_shape=[jax.ShapeDtypeStruct((A_HEADS * A_DIM, t), F32),
                   jax.ShapeDtypeStruct((A_GROUPS, nb, t), F32)],
        compiler_params=_params(3),
        name="nsa_cmp",
    )(q_t, kcmp, vcmp_t, ov_t)


ONES_ROWS = 16
FLASH_UNROLL = 4


def _with_ones(v):
    return jnp.concatenate([v, jnp.ones((ONES_ROWS, v.shape[1]), F32)], axis=0).astype(BF16)


def _flash_init(m_scr, acc_scr):
    m_scr[...] = jnp.full(m_scr.shape, -jnp.inf, F32)
    acc_scr[...] = jnp.zeros(acc_scr.shape, F32)


def _flash_update(s, v_t, m_scr, acc_scr):
    m_prev = m_scr[...]
    m_new = jnp.maximum(m_prev, jnp.max(s, axis=0, keepdims=True))
    alpha = jnp.exp2(m_prev - m_new)
    p = jnp.exp2(s - m_new)
    acc_scr[...] = alpha * acc_scr[...] + _dot(v_t, p.astype(BF16))
    m_scr[...] = m_new


def _flash_result(acc_scr, dv):
    acc = acc_scr[...]
    return acc[:dv] * (1.0 / acc[dv:dv + 1])


def _flash_causal(scores, values, mask, n_full, n_masked, s_scr, m_scr, acc_scr):
    unroll = s_scr.shape[0]
    s_scr[0] = scores(0)

    def trip(t, carry):
        j = unroll * t
        for u in range(unroll):
            s_scr[(u + 1) % unroll] = scores(j + u + 1)
            _flash_update(s_scr[u], values(j + u), m_scr, acc_scr)
        return carry

    lax.fori_loop(0, n_full // unroll, trip, 0)
    first = (n_full // unroll) * unroll
    for rest in range(unroll):

        @pl.when(n_full - first == rest)
        def _(rest=rest):
            for u in range(rest + n_masked):
                if u + 1 < rest + n_masked:
                    s_scr[(u + 1) % unroll] = scores(first + u + 1)
                s = s_scr[u % unroll]
                _flash_update(s if u < rest else mask(s, first + u), values(first + u), m_scr, acc_scr)


def _nsa_sel_kernel(q_ref, k_ref, v_ref, sb_ref, kw_ref, vw_ref, o_ref, ow_ref, qa_scr, s_scr, m_scr, acc_scr,
                    *, tq, tk):
    i = pl.program_id(2)
    n = A_REP * tq
    q = q_ref[...]
    qs = jnp.concatenate([q[r * A_DIM:(r + 1) * A_DIM, :] for r in range(A_REP)], axis=1)
    sb = sb_ref[0].astype(BF16)
    qa_scr[...] = jnp.concatenate([qs, jnp.zeros_like(qs), jnp.concatenate([sb] * A_REP, axis=1)], axis=0)
    _flash_init(m_scr, acc_scr)

    def scores(j):
        return _dot(k_ref[0, pl.ds(pl.multiple_of(j * tk, tk), tk), :], qa_scr[...])

    def causal(s, j):
        kpos = j * tk + lax.broadcasted_iota(jnp.int32, (tk, n), 0)
        tok = i * tq + (lax.broadcasted_iota(jnp.int32, (tk, n), 1) & (tq - 1))
        return jnp.where(kpos <= tok, s, NEG)

    _flash_causal(scores, lambda j: v_ref[0, j], causal, (i * tq) // tk, max(1, tq // tk), s_scr, m_scr, acc_scr)
    o = _flash_result(acc_scr, A_DIM)
    for r in range(A_REP):
        o_ref[r * A_DIM:(r + 1) * A_DIM, :] = o[:, r * tq:(r + 1) * tq]

    _flash_init(m_scr, acc_scr)
    n_back = WINDOW // tq
    for c in range(n_back + 1):
        kt = i - n_back + c

        @pl.when(kt >= 0)
        def _(kt=kt):
            s = _dot(kw_ref[0, pl.ds(pl.multiple_of(kt * tq, tq), tq), :], qa_scr[0:128, :])
            kpos = kt * tq + lax.broadcasted_iota(jnp.int32, (tq, n), 0)
            tok = i * tq + (lax.broadcasted_iota(jnp.int32, (tq, n), 1) & (tq - 1))
            s = jnp.where((kpos <= tok) & (kpos > tok - WINDOW), s, NEG)
            _flash_update(s, vw_ref[0, kt], m_scr, acc_scr)

    o = _flash_result(acc_scr, A_DIM)
    for r in range(A_REP):
        ow_ref[r * A_DIM:(r + 1) * A_DIM, :] = o[:, r * tq:(r + 1) * tq]


def _nsa_sel(q_t, k_aug, v_slc_tiles, selb, k_win, v_win_tiles, batch, seq, tq, tk):
    nq, nk = seq // tq, seq // tk
    nb = selb.shape[1]
    t = q_t.shape[1]
    gr = A_REP * A_DIM
    n = A_REP * tq
    kw = k_aug.shape[2]
    out = jax.ShapeDtypeStruct((A_HEADS * A_DIM, t), F32)
    return pl.pallas_call(
        functools.partial(_nsa_sel_kernel, tq=tq, tk=tk),
        grid=(batch, A_GROUPS, nq),
        in_specs=[pl.BlockSpec((gr, tq), lambda b, g, i: (g, b * nq + i)),
                  pl.BlockSpec((1, seq, kw), lambda b, g, i: (g, b, 0)),
                  pl.BlockSpec((1, nk, A_DIM + ONES_ROWS, tk), lambda b, g, i: (g, b, 0, 0)),
                  pl.BlockSpec((1, nb, tq), lambda b, g, i: (g, 0, b * nq + i)),
                  pl.BlockSpec((1, seq, 128), lambda b, g, i: (g, b, 0)),
                  pl.BlockSpec((1, nq, A_DIM + ONES_ROWS, tq), lambda b, g, i: (g, b, 0, 0))],
        out_specs=[pl.BlockSpec((gr, tq), lambda b, g, i: (g, b * nq + i)),
                   pl.BlockSpec((gr, tq), lambda b, g, i: (g, b * nq + i))],
        out_shape=[out, out],
        scratch_shapes=[pltpu.VMEM((kw, n), BF16), pltpu.VMEM((FLASH_UNROLL, tk, n), F32),
                        pltpu.VMEM((1, n), F32), pltpu.VMEM((A_DIM + ONES_ROWS, n), F32)],
        compiler_params=_params(3),
        name="nsa_sel",
    )(q_t, k_aug, v_slc_tiles, selb, k_win, v_win_tiles)


def _mla_prep_kernel(cq_ref, ckv_ref, kr_ref, cos_ref, sin_ref, qlg_ref, kvlg_ref, qg_ref, kg_ref, wuq_ref, wukv_ref,
                     qo_ref, ko_ref, vo_ref):
    cos, sin = cos_ref[...], sin_ref[...]
    tt = cos.shape[1]
    q_all = _dot(wuq_ref[...], (_rms_rows(cq_ref[...]) * qlg_ref[...]).astype(BF16))
    kv_all = _dot(wukv_ref[...], (_rms_rows(ckv_ref[...]) * kvlg_ref[...]).astype(BF16))
    kr = kr_ref[...]
    pad = jnp.zeros((128 - B_QK, tt), F32)
    for h in range(B_HEADS):
        y = _rms_rows(q_all[h * B_QK:(h + 1) * B_QK]) * qg_ref[...]
        y = _rope_rows(y, cos, sin, B_NOPE, B_ROPE // 2) * (B_QK ** -0.5 * LOG2E)
        qo_ref[h] = jnp.concatenate([y, pad], axis=0).astype(BF16)
        base = h * (B_NOPE + B_V)
        k = jnp.concatenate([kv_all[base:base + B_NOPE], kr], axis=0)
        y = _rope_rows(_rms_rows(k) * kg_ref[...], cos, sin, B_NOPE, B_ROPE // 2)
        ko_ref[h] = jnp.concatenate([y, pad], axis=0).T.astype(BF16)
        vo_ref[h, 0] = _with_ones(kv_all[base + B_NOPE:base + B_NOPE + B_V])


def _mla_attn_kernel(q_ref, k_ref, v_ref, o_ref, s_scr, m_scr, acc_scr, *, tq, tk):
    i = pl.program_id(2)
    _flash_init(m_scr, acc_scr)

    def scores(j):
        return _dot(k_ref[0, pl.ds(pl.multiple_of(j * tk, tk), tk), :], q_ref[0])

    def causal(s, j):
        kpos = j * tk + lax.broadcasted_iota(jnp.int32, (tk, tq), 0)
        tok = i * tq + lax.broadcasted_iota(jnp.int32, (tk, tq), 1)
        return jnp.where(kpos <= tok, s, NEG)

    _flash_causal(scores, lambda j: v_ref[0, j], causal, (i * tq) // tk, max(1, tq // tk), s_scr, m_scr, acc_scr)
    o_ref[...] = _flash_result(acc_scr, B_V)


def _mla_attn(q_m, k_m, v_m_tiles, batch, seq, tq, tk):
    nq, nk = seq // tq, seq // tk
    t = q_m.shape[2]
    return pl.pallas_call(
        functools.partial(_mla_attn_kernel, tq=tq, tk=tk),
        grid=(batch, B_HEADS, nq),
        in_specs=[pl.BlockSpec((1, 128, tq), lambda b, h, i: (h, 0, b * nq + i)),
                  pl.BlockSpec((1, seq, 128), lambda b, h, i: (h, b, 0)),
                  pl.BlockSpec((1, nk, B_V + ONES_ROWS, tk), lambda b, h, i: (h, b, 0, 0))],
        out_specs=pl.BlockSpec((B_V, tq), lambda b, h, i: (h, b * nq + i)),
        out_shape=jax.ShapeDtypeStruct((B_HEADS * B_V, t), F32),
        scratch_shapes=[pltpu.VMEM((FLASH_UNROLL, tk, tq), F32),
                        pltpu.VMEM((1, tq), F32), pltpu.VMEM((B_V + ONES_ROWS, tq), F32)],
        compiler_params=_params(3),
        name="mla_attn",
    )(q_m, k_m, v_m_tiles)


def _out_proj_kernel(oc_ref, os_ref, ow_ref, gt_ref, ob_ref, x_ref, ga_ref, gb_ref, w_ref, g2_ref, h_ref, hn_ref):
    heads = []
    for h in range(A_HEADS):
        rows = slice(h * A_DIM, (h + 1) * A_DIM)
        heads.append(gt_ref[3 * h:3 * h + 1, :] * oc_ref[rows, :] + gt_ref[3 * h + 1:3 * h + 2, :] * os_ref[rows, :]
                     + gt_ref[3 * h + 2:3 * h + 3, :] * ow_ref[rows, :])
    oa = _rms_rows(jnp.concatenate(heads, axis=0)) * ga_ref[...]
    ob = _rms_rows(ob_ref[...]) * gb_ref[...]
    cat = jnp.concatenate([oa, ob], axis=0).astype(BF16)
    hid = x_ref[...].T + _dot(w_ref[...], cat)
    h_ref[...] = hid
    hn_ref[...] = (_rms_rows(hid) * g2_ref[...]).astype(BF16)


def _out_proj(oc_t, os_t, ow_t, gates_t, ob_t, x2, gain_a, gain_b, w_out_t, gain2, tt):
    t, d = x2.shape
    aw = oc_t.shape[0]
    bw = ob_t.shape[0]
    tok = lambda rows: pl.BlockSpec((rows, tt), lambda i: (0, i))
    full = lambda a: pl.BlockSpec(a.shape, lambda i: (0,) * a.ndim)
    return pl.pallas_call(
        _out_proj_kernel,
        grid=(t // tt,),
        in_specs=[tok(aw), tok(aw), tok(aw), tok(GATE_ROWS), tok(bw), pl.BlockSpec((tt, d), lambda i: (i, 0)),
                  full(gain_a), full(gain_b), full(w_out_t), full(gain2)],
        out_specs=[tok(d), tok(d)],
        out_shape=[jax.ShapeDtypeStruct((d, t), F32), jax.ShapeDtypeStruct((d, t), BF16)],
        compiler_params=_params(1),
        name="out_proj",
    )(oc_t, os_t, ow_t, gates_t, ob_t, x2, gain_a, gain_b, w_out_t, gain2)


def _top_ranked(s):
    n, rest = s.shape[0], s.shape[1:]
    row = lax.broadcasted_iota(jnp.int32, s.shape, 0).astype(F32)
    slot = lax.broadcasted_iota(jnp.int32, (P_TOPK,) + rest, 0)

    def body(a, carry):
        v, rank, vals = carry
        mx = jnp.max(v, axis=0, keepdims=True)
        first = jnp.min(jnp.where(v == mx, row, float(n)), axis=0, keepdims=True)
        hit = row == first
        rank = jnp.where(hit, jnp.asarray(a, F32), rank)
        v = jnp.where(hit, -jnp.inf, v)
        vals = jnp.where(slot == a, mx, vals)
        return v, rank, vals

    _, rank, vals = lax.fori_loop(0, P_TOPK, body,
                                  (s, jnp.full(s.shape, float(P_TOPK), F32), jnp.zeros((P_TOPK,) + rest, F32)))
    return rank, vals


def _pair_counts(v1, v2):
    k = v1.shape[0]
    slot = lax.broadcasted_iota(jnp.int32, v1.shape, 0).astype(F32)
    top = v1[0:1] + v2[0:1]

    def body(_, carry):
        count, front, z = carry
        mx = jnp.max(front, axis=0, keepdims=True)
        a_star = jnp.min(jnp.where(front == mx, slot, float(k)), axis=0, keepdims=True)
        hit = slot == a_star
        count = count + jnp.where(hit, 1.0, 0.0)
        nxt = jnp.sum(jnp.where(hit, count, 0.0), axis=0, keepdims=True)
        v2_nxt = jnp.sum(jnp.where(slot == nxt, v2, 0.0), axis=0, keepdims=True)
        front = jnp.where(hit, jnp.where(nxt < float(k), v1 + v2_nxt, -jnp.inf), front)
        return count, front, z + jnp.exp(mx - top)

    count, _, z = lax.fori_loop(0, k, body, (jnp.zeros(v1.shape, F32), v1 + v2[0:1], jnp.zeros(top.shape, F32)))
    return count, z


REMOVED = -2.0 ** 126


def _rank_fast(v_ref, vals_ref):
    sides = v_ref.shape[0]

    def body(a, carry):
        code = REMOVED * (1.0 + jnp.asarray(a, F32) * (1.0 / 32.0))
        for side in range(sides):
            v = v_ref[side]
            mx = jnp.max(v, axis=0, keepdims=True)
            v_ref[side] = jnp.where(v == mx, code, v)
            vals_ref[side, pl.ds(a, 1)] = mx
        return carry

    lax.fori_loop(0, P_TOPK, body, 0)
    bad = 0.0
    for side in range(sides):
        v = v_ref[side]
        removed = v <= REMOVED
        v_ref[side] = jnp.where(removed, (v * (1.0 / REMOVED) - 1.0) * 32.0, float(P_TOPK))
        n_removed = jnp.sum(jnp.where(removed, 1.0, 0.0), axis=0)
        bad = jnp.maximum(bad, jnp.max(jnp.abs(n_removed - float(P_TOPK))))
    return bad


def _peer_route_kernel(hn_ref, wq_ref, keys_ref, perm_ref, cut_ref, g1_ref, r2_ref, g2_ref,
                       q_scr, s_scr, v_scr, vals_scr):
    half = P_KEY_DIM // 2
    q_scr[...] = _dot(wq_ref[...], hn_ref[...])
    halves = ([], [])
    for h in range(P_HEADS):
        q = _rms_rows(q_scr[h * P_KEY_DIM:(h + 1) * P_KEY_DIM, :]).astype(BF16)
        halves[0].append(q[:half])
        halves[1].append(q[half:])
    for side in range(2):
        s = _dot(keys_ref[side], jnp.concatenate(halves[side], axis=0))
        for n in range(N_KEYS):
            s_scr[side, n] = s[n * P_HEADS:(n + 1) * P_HEADS]
    v_scr[...] = s_scr[...]

    bad = _rank_fast(v_scr, vals_scr)

    @pl.when(bad > 0.0)
    def _():
        for side in range(2):
            v_scr[side], vals_scr[side] = _top_ranked(s_scr[side])

    count, z = _pair_counts(vals_scr[0], vals_scr[1])
    rank1 = v_scr[0]
    cut = jnp.zeros_like(rank1)
    for a in range(P_TOPK):
        cut = jnp.where(rank1 == float(a), count[a:a + 1], cut)
    cut_ref[...] = cut
    g1_ref[...] = jnp.exp(s_scr[0] - vals_scr[0, 0:1])
    for dst, x in ((r2_ref, v_scr[1]), (g2_ref, jnp.exp(s_scr[1] - vals_scr[1, 0:1]) * (1.0 / z))):
        rows = jnp.concatenate([x[n] for n in range(N_KEYS)], axis=0).astype(BF16)
        by_head = _dot(perm_ref[...], rows)
        for h in range(P_HEADS):
            dst[h] = by_head[h * N_KEYS:(h + 1) * N_KEYS].astype(BF16)


def _peer_route(hn_t, wq_t, keys_big, tp):
    d, t = hn_t.shape
    rows = N_KEYS * P_HEADS
    perm = np.zeros((rows, rows), np.float32)
    perm[np.arange(rows), (np.arange(rows) % N_KEYS) * P_HEADS + np.arange(rows) // N_KEYS] = 1.0
    perm = jnp.asarray(perm).astype(BF16)
    by_key = jax.ShapeDtypeStruct((N_KEYS, P_HEADS, t), F32)
    by_head = jax.ShapeDtypeStruct((P_HEADS, N_KEYS, t), BF16)
    key_spec = pl.BlockSpec((N_KEYS, P_HEADS, tp), lambda i: (0, 0, i))
    head_spec = pl.BlockSpec((P_HEADS, N_KEYS, tp), lambda i: (0, 0, i))
    return pl.pallas_call(
        _peer_route_kernel,
        grid=(t // tp,),
        in_specs=[pl.BlockSpec((d, tp), lambda i: (0, i)),
                  pl.BlockSpec(wq_t.shape, lambda i: (0, 0)),
                  pl.BlockSpec(keys_big.shape, lambda i: (0, 0, 0)),
                  pl.BlockSpec((rows, rows), lambda i: (0, 0))],
        out_specs=[key_spec, key_spec, head_spec, head_spec],
        out_shape=[by_key, by_key, by_head, by_head],
        scratch_shapes=[pltpu.VMEM((P_HEADS * P_KEY_DIM, tp), F32), pltpu.VMEM((2, N_KEYS, P_HEADS, tp), F32),
                        pltpu.VMEM((2, N_KEYS, P_HEADS, tp), F32), pltpu.VMEM((2, P_TOPK, P_HEADS, tp), F32)],
        compiler_params=_params(1),
        name="peer_route",
    )(hn_t, wq_t, keys_big, perm)


def _peer_ffn_kernel(hn_ref, h_ref, u_ref, v_ref, cut_ref, g1_ref, r2_ref, g2_ref, o_ref, acc_scr, *, te):
    e = pl.program_id(1)

    @pl.when(e == 0)
    def _():
        acc_scr[...] = jnp.zeros(acc_scr.shape, F32)

    hn = hn_ref[...]
    tt = hn.shape[1]

    def rows_bf16(ref, ii, h):
        r = ii * P_HEADS + h
        return jnp.broadcast_to(ref[r:r + 1, :].astype(BF16), (N_KEYS, tt))

    chunk = 2 * N_KEYS
    weights = []
    for c in range(te // chunk):
        a = _dot(u_ref[c * chunk:(c + 1) * chunk, :], hn)
        for k in range(chunk // N_KEYS):
            ii = c * (chunk // N_KEYS) + k
            gate = jnp.zeros((N_KEYS, tt), BF16)
            for h in range(P_HEADS):
                chosen = r2_ref[h] < rows_bf16(cut_ref, ii, h)
                gate = gate + jnp.where(chosen, g2_ref[h], jnp.zeros_like(gate)) * rows_bf16(g1_ref, ii, h)
            weights.append(gate * _gelu_tanh(a[k * N_KEYS:(k + 1) * N_KEYS, :]).astype(BF16))
    acc_scr[...] += _dot(v_ref[...], jnp.concatenate(weights, axis=0))

    @pl.when(e == pl.num_programs(1) - 1)
    def _():
        o_ref[...] = (h_ref[...] + acc_scr[...]).T


def _peer_ffn(hn_t, h_t, u_bf, v_t_bf, cut, g1, r2, g2, tt, te):
    d, t = hn_t.shape
    n_exp = u_bf.shape[0]
    step_rows = te // N_KEYS * P_HEADS
    return pl.pallas_call(
        functools.partial(_peer_ffn_kernel, te=te),
        grid=(t // tt, n_exp // te),
        in_specs=[pl.BlockSpec((d, tt), lambda i, e: (0, i)),
                  pl.BlockSpec((d, tt), lambda i, e: (0, i)),
                  pl.BlockSpec((te, d), lambda i, e: (e, 0)),
                  pl.BlockSpec((d, te), lambda i, e: (0, e)),
                  pl.BlockSpec((step_rows, tt), lambda i, e: (e, i)),
                  pl.BlockSpec((step_rows, tt), lambda i, e: (e, i)),
                  pl.BlockSpec((P_HEADS, N_KEYS, tt), lambda i, e: (0, 0, i)),
                  pl.BlockSpec((P_HEADS, N_KEYS, tt), lambda i, e: (0, 0, i))],
        out_specs=pl.BlockSpec((tt, d), lambda i, e: (i, 0)),
        out_shape=jax.ShapeDtypeStruct((t, d), F32),
        scratch_shapes=[pltpu.VMEM((d, tt), F32)],
        compiler_params=_params(2),
        name="peer_ffn",
    )(hn_t, h_t, u_bf, v_t_bf, cut, g1, r2, g2)


def _rope_tables_t(pos_flat, rot_dim):
    inv_freq = ROPE_THETA ** (-jnp.arange(0, rot_dim, 2, dtype=F32) / rot_dim)
    ang = pos_flat.astype(F32)[None, :] * inv_freq[:, None]
    return jnp.cos(ang), jnp.sin(ang)


def _expand_cmp_w1(w1):
    w = w1.reshape(CMP_LEN, A_DIM, CMP_HIDDEN)
    out = []
    for part in (w[:CMP_STRIDE], w[CMP_STRIDE:]):
        z = jnp.zeros_like(part)
        both = jnp.stack([jnp.concatenate([part, z], axis=1), jnp.concatenate([z, part], axis=1)])
        out.append(both.reshape(A_GROUPS, CMP_STRIDE * A_GROUPS * A_DIM, CMP_HIDDEN).transpose(0, 2, 1).astype(BF16))
    return out


TOKEN_TILE = 512
NSA_Q_TILE = 256
NSA_SEL_K_TILE = 512
MLA_Q_TILE = 1024
MLA_K_TILE = 512
PEER_ROUTE_TILE = 256
PEER_EXPERT_TILE = 2048


def _col(v):
    return v.reshape(-1, 1).astype(F32)


def _mixers(x, positions, norm1_gain, w_in, nsa_q_gain, nsa_kc_gain, nsa_ks_gain, nsa_kw_gain,
            cmp_pos, cmp_k_w1, cmp_k_w2, cmp_v_w1, cmp_v_w2,
            mla_q_lora_gain, mla_w_uq, mla_kv_lora_gain, mla_w_ukv, mla_q_gain, mla_k_gain):
    batch, seq, d = x.shape
    t = batch * seq
    tt = TOKEN_TILE
    tq_nsa = NSA_Q_TILE
    tk_sel = NSA_SEL_K_TILE
    tq_mla, tk_mla = MLA_Q_TILE, MLA_K_TILE
    assert tk_sel == tt and tk_mla == tt
    assert d == D_MODEL and seq % 512 == 0 and seq // SLC_LEN >= SLC_TOPK and WINDOW % tq_nsa == 0
    col = _col
    x2 = x.reshape(t, d)
    pos = positions.reshape(t)

    w_in_t = w_in.T
    gate_lo = sum((512, 128, 128, 128, 128, 128, 128))
    gate_hi = gate_lo + 3 * A_HEADS
    w_in_t = jnp.concatenate([w_in_t[:gate_lo], w_in_t[gate_hi:], w_in_t[gate_lo:gate_hi],
                              jnp.zeros((PROJ_ROWS - w_in_t.shape[0], d), F32)], axis=0).astype(BF16)
    cos_a, sin_a = _rope_tables_t(pos, A_ROPE)
    cos_b, sin_b = _rope_tables_t(pos, B_ROPE)
    q_t, kc_tm, vc_tm, k_slc, v_slc_t, k_win, v_win_t, gates_t, q_m, k_m, v_m_t = _in_proj(
        x2, norm1_gain.reshape(1, d), w_in_t, cos_a, sin_a, col(nsa_q_gain), col(nsa_ks_gain), col(nsa_kw_gain),
        cos_b, sin_b, col(mla_q_lora_gain), col(mla_kv_lora_gain), col(mla_q_gain), col(mla_k_gain),
        mla_w_uq.T.astype(BF16), mla_w_ukv.T.astype(BF16), tt, seq, tq_nsa)

    nc = seq // CMP_STRIDE
    chunk_w = CMP_STRIDE * A_GROUPS * A_DIM
    w1ka, w1kb = _expand_cmp_w1(cmp_k_w1)
    w1va, w1vb = _expand_cmp_w1(cmp_v_w1)
    pos_rows = lambda p: jnp.broadcast_to(p[:, None, :], (CMP_STRIDE, A_GROUPS, A_DIM)).reshape(1, chunk_w)
    cmp_end = jnp.minimum(jnp.arange(nc) * CMP_STRIDE + CMP_LEN - 1, seq - 1)
    cos_c, sin_c = _rope_tables_t(positions[:, cmp_end].reshape(-1), A_ROPE)
    to_b = lambda a: a.reshape(A_ROPE // 2, batch, nc).transpose(1, 0, 2)
    kcmp, vcmp_t = _compress(kc_tm.reshape(batch, nc, chunk_w), vc_tm.reshape(batch, nc, chunk_w),
                             w1ka, w1kb, w1va, w1vb, pos_rows(cmp_pos[:CMP_STRIDE]), pos_rows(cmp_pos[CMP_STRIDE:]),
                             cmp_k_w2.T.astype(BF16), cmp_v_w2.T.astype(BF16), col(nsa_kc_gain), to_b(cos_c), to_b(sin_c))

    n_cmp = (seq - CMP_LEN) // CMP_STRIDE + 1
    nb = seq // SLC_LEN
    c_start = np.arange(nc)[None, :] * CMP_STRIDE
    s_start = np.arange(nb)[:, None] * SLC_LEN
    ov = (c_start < s_start + SLC_LEN) & (c_start + CMP_LEN - 1 >= s_start) & (np.arange(nc)[None, :] < n_cmp)
    ov_t = jnp.asarray(ov.astype(np.float32)).astype(BF16)

    oc_t, selb = _nsa_cmp(q_t, kcmp, vcmp_t, ov_t, batch, seq, tq_nsa)
    os_t, ow_t = _nsa_sel(q_t, k_slc, v_slc_t, selb, k_win, v_win_t, batch, seq, tq_nsa, tk_sel)

    ob_t = _mla_attn(q_m, k_m, v_m_t, batch, seq, tq_mla, tk_mla)
    return oc_t, os_t, ow_t, gates_t, ob_t


def _peer(hn_t, h_t, peer_w_q, peer_sub_keys, peer_u, peer_v):
    keys_big = jnp.einsum("hsnd,hg->snhgd", peer_sub_keys, jnp.eye(P_HEADS, dtype=F32))
    keys_big = keys_big.reshape(2, N_KEYS * P_HEADS, P_HEADS * (P_KEY_DIM // 2)).astype(BF16)
    cut, g1, r2, g2 = _peer_route(hn_t, peer_w_q.T.astype(BF16), keys_big, PEER_ROUTE_TILE)
    by_row = lambda a: a.reshape(N_KEYS * P_HEADS, -1)
    return _peer_ffn(hn_t, h_t, peer_u.astype(BF16), peer_v.T.astype(BF16),
                     by_row(cut), by_row(g1), r2, g2, TOKEN_TILE, PEER_EXPERT_TILE)


def _layer(x, positions, norm1_gain, w_in, nsa_q_gain, nsa_kc_gain, nsa_ks_gain, nsa_kw_gain,
           cmp_pos, cmp_k_w1, cmp_k_w2, cmp_v_w1, cmp_v_w2,
           mla_q_lora_gain, mla_w_uq, mla_kv_lora_gain, mla_w_ukv, mla_q_gain, mla_k_gain,
           out_gain_a, out_gain_b, w_out, norm2_gain, peer_w_q, peer_sub_keys, peer_u, peer_v):
    batch, seq, d = x.shape
    oc_t, os_t, ow_t, gates_t, ob_t = _mixers(
        x, positions, norm1_gain, w_in, nsa_q_gain, nsa_kc_gain, nsa_ks_gain, nsa_kw_gain,
        cmp_pos, cmp_k_w1, cmp_k_w2, cmp_v_w1, cmp_v_w2,
        mla_q_lora_gain, mla_w_uq, mla_kv_lora_gain, mla_w_ukv, mla_q_gain, mla_k_gain)
    h_t, hn_t = _out_proj(oc_t, os_t, ow_t, gates_t, ob_t, x.reshape(batch * seq, d), _col(out_gain_a), _col(out_gain_b),
                          w_out.T.astype(BF16), _col(norm2_gain), TOKEN_TILE)
    return _peer(hn_t, h_t, peer_w_q, peer_sub_keys, peer_u, peer_v).reshape(batch, seq, d)


def kernel(x, positions, norm1_gain, w_in, nsa_q_gain, nsa_kc_gain, nsa_ks_gain, nsa_kw_gain, cmp_pos, cmp_k_w1, cmp_k_w2, cmp_v_w1, cmp_v_w2, mla_q_lora_gain, mla_w_uq, mla_kv_lora_gain, mla_w_ukv, mla_q_gain, mla_k_gain, out_gain_a, out_gain_b, w_out, norm2_gain, peer_w_q, peer_sub_keys, peer_u, peer_v):
    h = x
    for l in range(norm1_gain.shape[0]):
        h = _layer(h, positions, norm1_gain[l], w_in[l], nsa_q_gain[l], nsa_kc_gain[l], nsa_ks_gain[l], nsa_kw_gain[l],
                   cmp_pos[l], cmp_k_w1[l], cmp_k_w2[l], cmp_v_w1[l], cmp_v_w2[l],
                   mla_q_lora_gain[l], mla_w_uq[l], mla_kv_lora_gain[l], mla_w_ukv[l], mla_q_gain[l], mla_k_gain[l],
                   out_gain_a[l], out_gain_b[l], w_out[l], norm2_gain[l], peer_w_q[l], peer_sub_keys[l],
                   peer_u[l], peer_v[l])
    return h
```

```python
import functools

import jax
import jax.numpy as jnp
import numpy as np
from jax import lax
from jax.experimental import pallas as pl
from jax.experimental.pallas import tpu as pltpu

F32, BF16 = jnp.float32, jnp.bfloat16
EPS = 1e-6
NEG = -1e30
FORCE = 1e9
ROPE_THETA = 500000.0
LOG2E = 1.4426950408889634

D_MODEL = 1024
A_HEADS, A_GROUPS, A_DIM = 8, 2, 64
A_REP = A_HEADS // A_GROUPS
A_ROPE = A_DIM // 4
CMP_LEN, CMP_STRIDE, CMP_HIDDEN = 32, 16, 256
SLC_LEN, SLC_TOPK, WINDOW = 64, 16, 512
B_HEADS, Q_LORA, KV_LORA, B_NOPE, B_ROPE, B_V = 8, 256, 128, 64, 32, 64
B_QK = B_NOPE + B_ROPE
P_HEADS, N_KEYS, P_KEY_DIM, P_TOPK = 8, 128, 256, 16
N_EXPERTS = N_KEYS * N_KEYS

ROW_Q, ROW_KC, ROW_VC, ROW_KS, ROW_VS, ROW_KW, ROW_VW = 0, 512, 640, 768, 896, 1024, 1152
ROW_CQ, ROW_CKV, ROW_KR, ROW_GATE, PROJ_ROWS = 1280, 1536, 1664, 1696, 1728
GATE_ROWS = 32

VMEM_LIMIT = 56 * 1024 * 1024
NT_DIMS = (((1,), (1,)), ((), ()))


def _params(n_axes):
    return pltpu.CompilerParams(dimension_semantics=("arbitrary",) * n_axes, vmem_limit_bytes=VMEM_LIMIT)


def _dot(a, b):
    return jnp.dot(a, b, preferred_element_type=F32)


def _dot_nt(a, b):
    return lax.dot_general(a, b, NT_DIMS, preferred_element_type=F32)


def _row_sumsq(x):
    sq = x * x
    hi = sq.astype(BF16)
    lo = (sq - hi.astype(F32)).astype(BF16)
    ones = jnp.ones((8, x.shape[1]), BF16)
    return (_dot_nt(ones, hi) + _dot_nt(ones, lo))[0:1, :]


def _rms_rows(x):
    ss = jnp.sum(x * x, axis=0, keepdims=True)
    return x * lax.rsqrt(ss * (1.0 / x.shape[0]) + EPS)


def _rope_rows(y, cos, sin, off, half):
    x1, x2 = y[off:off + half], y[off + half:off + 2 * half]
    parts = [y[:off]] if off else []
    parts += [x1 * cos - x2 * sin, x2 * cos + x1 * sin]
    if off + 2 * half < y.shape[0]:
        parts.append(y[off + 2 * half:])
    return jnp.concatenate(parts, axis=0)


def _gelu_tanh(x):
    c = 0.7978845608028654
    half = 0.5 * x
    return half + half * jnp.tanh(x * (c + (c * 0.044715) * (x * x)))


def _in_proj_kernel(x_ref, g_ref, w_ref, cos_a_ref, sin_a_ref, qg_ref, ksg_ref, kwg_ref,
                    cos_b_ref, sin_b_ref, qlg_ref, kvlg_ref, mqg_ref, mkg_ref, wuq_ref, wukv_ref, *out_refs, seq, tw):
    x = x_ref[...]
    xg = (x * g_ref[...]).astype(BF16)
    p = _dot_nt(w_ref[...], xg)
    p = p * lax.rsqrt(_row_sumsq(x) * (1.0 / x.shape[1]) + EPS)
    gw = A_GROUPS * A_DIM
    rows = lambda start, n: p[start:start + n]
    _nsa_prep_kernel(rows(ROW_Q, A_HEADS * A_DIM), rows(ROW_KC, gw), rows(ROW_VC, gw), rows(ROW_KS, gw), rows(ROW_VS, gw),
                     rows(ROW_KW, gw), rows(ROW_VW, gw), rows(ROW_GATE, GATE_ROWS), cos_a_ref, sin_a_ref,
                     qg_ref, ksg_ref, kwg_ref, *out_refs[:8], seq=seq, tw=tw)
    _mla_prep_kernel(rows(ROW_CQ, Q_LORA), rows(ROW_CKV, KV_LORA), rows(ROW_KR, B_ROPE), cos_b_ref, sin_b_ref,
                     qlg_ref, kvlg_ref, mqg_ref, mkg_ref, wuq_ref, wukv_ref, *out_refs[8:])


def _in_proj(x2, gain, w_t, cos_a, sin_a, q_gain, ks_gain, kw_gain, cos_b, sin_b, q_lora_gain, kv_lora_gain,
             mq_gain, mk_gain, wuq_t, wukv_t, tt, seq, tw):
    t, d = x2.shape
    gw = A_GROUPS * A_DIM
    kw = 128 + seq // SLC_LEN
    va, vb = A_DIM + ONES_ROWS, B_V + ONES_ROWS
    full = lambda a: pl.BlockSpec(a.shape, lambda i: (0,) * a.ndim)
    lanes = lambda n: pl.BlockSpec((n, tt), lambda i: (0, i))
    return pl.pallas_call(
        functools.partial(_in_proj_kernel, seq=seq, tw=tw),
        grid=(t // tt,),
        in_specs=[pl.BlockSpec((tt, d), lambda i: (i, 0)), full(gain), full(w_t),
                  lanes(A_ROPE // 2), lanes(A_ROPE // 2), full(q_gain), full(ks_gain), full(kw_gain),
                  lanes(B_ROPE // 2), lanes(B_ROPE // 2), full(q_lora_gain), full(kv_lora_gain), full(mq_gain),
                  full(mk_gain), full(wuq_t), full(wukv_t)],
        out_specs=[lanes(A_HEADS * A_DIM),
                   pl.BlockSpec((tt, gw), lambda i: (i, 0)),
                   pl.BlockSpec((tt, gw), lambda i: (i, 0)),
                   pl.BlockSpec((A_GROUPS, tt, kw), lambda i: (0, i, 0)),
                   pl.BlockSpec((A_GROUPS, 1, va, tt), lambda i: (0, i, 0, 0)),
                   pl.BlockSpec((A_GROUPS, tt, 128), lambda i: (0, i, 0)),
                   pl.BlockSpec((A_GROUPS, tt // tw, va, tw), lambda i: (0, i, 0, 0)),
                   lanes(GATE_ROWS),
                   pl.BlockSpec((B_HEADS, 128, tt), lambda i: (0, 0, i)),
                   pl.BlockSpec((B_HEADS, tt, 128), lambda i: (0, i, 0)),
                   pl.BlockSpec((B_HEADS, 1, vb, tt), lambda i: (0, i, 0, 0))],
        out_shape=[jax.ShapeDtypeStruct((A_HEADS * A_DIM, t), BF16),
                   jax.ShapeDtypeStruct((t, gw), F32),
                   jax.ShapeDtypeStruct((t, gw), F32),
                   jax.ShapeDtypeStruct((A_GROUPS, t, kw), BF16),
                   jax.ShapeDtypeStruct((A_GROUPS, t // tt, va, tt), BF16),
                   jax.ShapeDtypeStruct((A_GROUPS, t, 128), BF16),
                   jax.ShapeDtypeStruct((A_GROUPS, t // tw, va, tw), BF16),
                   jax.ShapeDtypeStruct((GATE_ROWS, t), F32),
                   jax.ShapeDtypeStruct((B_HEADS, 128, t), BF16),
                   jax.ShapeDtypeStruct((B_HEADS, t, 128), BF16),
                   jax.ShapeDtypeStruct((B_HEADS, t // tt, vb, tt), BF16)],
        compiler_params=_params(1),
        name="in_proj",
    )(x2, gain, w_t, cos_a, sin_a, q_gain, ks_gain, kw_gain, cos_b, sin_b, q_lora_gain, kv_lora_gain,
      mq_gain, mk_gain, wuq_t, wukv_t)


def _nsa_prep_kernel(q_ref, kc_ref, vc_ref, ks_ref, vs_ref, kw_ref, vw_ref, gt_ref, cos_ref, sin_ref,
                     qg_ref, ksg_ref, kwg_ref,
                     qo_ref, kco_ref, vco_ref, kso_ref, vso_ref, kwo_ref, vwo_ref, gto_ref, *, seq, tw):
    cos, sin = cos_ref[...], sin_ref[...]
    tt = cos.shape[1]
    nb = seq // SLC_LEN
    for h in range(A_HEADS):
        y = _rms_rows(q_ref[h * A_DIM:(h + 1) * A_DIM, :]) * qg_ref[...]
        y = _rope_rows(y, cos, sin, 0, A_ROPE // 2) * (A_DIM ** -0.5 * LOG2E)
        qo_ref[h * A_DIM:(h + 1) * A_DIM, :] = y.astype(BF16)
    kco_ref[...] = kc_ref[...].T
    vco_ref[...] = vc_ref[...].T
    zeros = jnp.zeros((A_DIM, tt), F32)
    tok = pl.program_id(0) * tt + lax.broadcasted_iota(jnp.int32, (tt, nb), 0)
    block_hot = jnp.where(lax.broadcasted_iota(jnp.int32, (tt, nb), 1) == (tok % seq) // SLC_LEN, 1.0, 0.0)
    for g in range(A_GROUPS):
        for src, gain, dst in ((ks_ref, ksg_ref, kso_ref), (kw_ref, kwg_ref, kwo_ref)):
            y = _rms_rows(src[g * A_DIM:(g + 1) * A_DIM, :]) * gain[...]
            y = _rope_rows(y, cos, sin, 0, A_ROPE // 2)
            k_tm = jnp.concatenate([y, zeros], axis=0).T
            if dst is kso_ref:
                k_tm = jnp.concatenate([k_tm, block_hot], axis=1)
            dst[g] = k_tm.astype(BF16)
        vso_ref[g, 0] = _with_ones(vs_ref[g * A_DIM:(g + 1) * A_DIM, :])
        v_win = _with_ones(vw_ref[g * A_DIM:(g + 1) * A_DIM, :])
        for c in range(tt // tw):
            vwo_ref[g, c] = v_win[:, c * tw:(c + 1) * tw]
    gto_ref[...] = 1.0 / (1.0 + jnp.exp(-gt_ref[...]))


def _compress_kernel(kc_ref, vc_ref, w1ka_ref, w1kb_ref, w1va_ref, w1vb_ref, plo_ref, phi_ref,
                     w2k_ref, w2v_ref, kg_ref, cos_ref, sin_ref, ko_ref, vo_ref):
    nc = kc_ref.shape[1]
    zeros = jnp.zeros((A_DIM, nc), F32)
    for src, w1a, w1b, w2, is_k in ((kc_ref, w1ka_ref, w1kb_ref, w2k_ref, True),
                                    (vc_ref, w1va_ref, w1vb_ref, w2v_ref, False)):
        x = src[0]
        xlo = (x + plo_ref[...]).astype(BF16)
        xhi = (x + phi_ref[...]).astype(BF16)
        for g in range(A_GROUPS):
            first = _dot_nt(w1a[g], xlo)
            second = _dot_nt(w1b[g], xhi)
            hid = _gelu_tanh(first + pltpu.roll(second, nc - 1, axis=1)).astype(BF16)
            c = _dot(w2[...], hid)
            if is_k:
                y = _rope_rows(_rms_rows(c) * kg_ref[...], cos_ref[0], sin_ref[0], 0, A_ROPE // 2)
                ko_ref[0, g] = jnp.concatenate([y, zeros], axis=0).T.astype(BF16)
            else:
                vo_ref[0, g] = c.astype(BF16)


def _compress(kc_chunks, vc_chunks, w1ka, w1kb, w1va, w1vb, plo, phi, w2k_t, w2v_t, kc_gain, cos_c, sin_c):
    b, nc, cw = kc_chunks.shape
    full = lambda a: pl.BlockSpec(a.shape, lambda i: (0,) * a.ndim)
    return pl.pallas_call(
        _compress_kernel,
        grid=(b,),
        in_specs=[pl.BlockSpec((1, nc, cw), lambda i: (i, 0, 0)), pl.BlockSpec((1, nc, cw), lambda i: (i, 0, 0)),
                  full(w1ka), full(w1kb), full(w1va), full(w1vb), full(plo), full(phi), full(w2k_t), full(w2v_t),
                  full(kc_gain),
                  pl.BlockSpec((1, A_ROPE // 2, nc), lambda i: (i, 0, 0)),
                  pl.BlockSpec((1, A_ROPE // 2, nc), lambda i: (i, 0, 0))],
        out_specs=[pl.BlockSpec((1, A_GROUPS, nc, 128), lambda i: (i, 0, 0, 0)),
                   pl.BlockSpec((1, A_GROUPS, A_DIM, nc), lambda i: (i, 0, 0, 0))],
        out_shape=[jax.ShapeDtypeStruct((b, A_GROUPS, nc, 128), BF16),
                   jax.ShapeDtypeStruct((b, A_GROUPS, A_DIM, nc), BF16)],
        compiler_params=_params(1),
        name="nsa_compress",
    )(kc_chunks, vc_chunks, w1ka, w1kb, w1va, w1vb, plo, phi, w2k_t, w2v_t, kc_gain, cos_c, sin_c)


def _stack_heads(q, tq):
    qs = jnp.concatenate([q[r * A_DIM:(r + 1) * A_DIM, :] for r in range(A_REP)], axis=1)
    return jnp.concatenate([qs, jnp.zeros_like(qs)], axis=0)


def _nsa_cmp_kernel(q_ref, k_ref, v_ref, ov_ref, o_ref, sb_ref, *, tq):
    i = pl.program_id(2)
    n = A_REP * tq
    nc = k_ref.shape[2]
    nb = ov_ref.shape[0]
    qp = _stack_heads(q_ref[...], tq)
    s = _dot(k_ref[0, 0], qp)
    cmp_end = lax.broadcasted_iota(jnp.int32, (nc, n), 0) * CMP_STRIDE + (CMP_LEN - 1)
    tok = i * tq + (lax.broadcasted_iota(jnp.int32, (nc, n), 1) & (tq - 1))
    mask = cmp_end <= tok
    s = jnp.where(mask, s, NEG)
    m = jnp.max(s, axis=0, keepdims=True)
    p = jnp.where(mask, jnp.exp2(s - m), 0.0)
    l = jnp.sum(p, axis=0, keepdims=True)
    inv = jnp.where(l > 0.0, 1.0 / l, 0.0)
    pn = (p * inv).astype(BF16)
    oc = _dot(v_ref[0, 0], pn)
    for r in range(A_REP):
        o_ref[r * A_DIM:(r + 1) * A_DIM, :] = oc[:, r * tq:(r + 1) * tq]
    imp4 = _dot(ov_ref[...], pn)
    imp = imp4[:, 0:tq]
    for r in range(1, A_REP):
        imp = imp + imp4[:, r * tq:(r + 1) * tq]

    blk = lax.broadcasted_iota(jnp.int32, (nb, tq), 0)
    t = i * tq + lax.broadcasted_iota(jnp.int32, (nb, tq), 1)
    forced = (blk == t // SLC_LEN) | (blk == 0)
    v = jnp.where(forced, FORCE, jnp.where(blk * SLC_LEN <= t, imp, NEG))
    blk_f = blk.astype(F32)
    sel = jnp.zeros((nb, tq), F32)
    for _ in range(min(SLC_TOPK, nb)):
        mx = jnp.max(v, axis=0, keepdims=True)
        first = jnp.min(jnp.where(v == mx, blk_f, float(nb)), axis=0, keepdims=True)
        hit = blk_f == first
        sel = jnp.where(hit, 1.0, sel)
        v = jnp.where(hit, -jnp.inf, v)
    sb_ref[0] = jnp.where(sel > 0.0, 0.0, NEG)


def _nsa_cmp(q_t, kcmp, vcmp_t, ov_t, batch, seq, tq):
    nq = seq // tq
    nc = kcmp.shape[2]
    nb = ov_t.shape[0]
    t = q_t.shape[1]
    gr = A_REP * A_DIM
    return pl.pallas_call(
        functools.partial(_nsa_cmp_kernel, tq=tq),
        grid=(batch, A_GROUPS, nq),
        in_specs=[pl.BlockSpec((gr, tq), lambda b, g, i: (g, b * nq + i)),
                  pl.BlockSpec((1, 1, nc, 128), lambda b, g, i: (b, g, 0, 0)),
                  pl.BlockSpec((1, 1, A_DIM, nc), lambda b, g, i: (b, g, 0, 0)),
                  pl.BlockSpec((nb, nc), lambda b, g, i: (0, 0))],
        out_specs=[pl.BlockSpec((gr, tq), lambda b, g, i: (g, b * nq + i)),
                   pl.BlockSpec((1, nb, tq), lambda b, g, i: (g, 0, b * nq + i))],
        out_shape=[jax.ShapeDtypeStruct((A_HEADS * A_DIM, t), F32),
                   jax.ShapeDtypeStruct((A_GROUPS, nb, t), F32)],
        compiler_params=_params(3),
        name="nsa_cmp",
    )(q_t, kcmp, vcmp_t, ov_t)


ONES_ROWS = 16
FLASH_UNROLL = 4


def _with_ones(v):
    return jnp.concatenate([v, jnp.ones((ONES_ROWS, v.shape[1]), F32)], axis=0).astype(BF16)


def _flash_init(m_scr, acc_scr):
    m_scr[...] = jnp.full(m_scr.shape, -jnp.inf, F32)
    acc_scr[...] = jnp.zeros(acc_scr.shape, F32)


def _flash_update(s, v_t, m_scr, acc_scr):
    m_prev = m_scr[...]
    m_new = jnp.maximum(m_prev, jnp.max(s, axis=0, keepdims=True))
    alpha = jnp.exp2(m_prev - m_new)
    p = jnp.exp2(s - m_new)
    acc_scr[...] = alpha * acc_scr[...] + _dot(v_t, p.astype(BF16))
    m_scr[...] = m_new


def _flash_result(acc_scr, dv):
    acc = acc_scr[...]
    return acc[:dv] * (1.0 / acc[dv:dv + 1])


def _flash_causal(scores, values, mask, n_full, n_masked, s_scr, m_scr, acc_scr):
    unroll = s_scr.shape[0]
    s_scr[0] = scores(0)

    def trip(t, carry):
        j = unroll * t
        for u in range(unroll):
            s_scr[(u + 1) % unroll] = scores(j + u + 1)
            _flash_update(s_scr[u], values(j + u), m_scr, acc_scr)
        return carry

    lax.fori_loop(0, n_full // unroll, trip, 0)
    first = (n_full // unroll) * unroll
    for rest in range(unroll):

        @pl.when(n_full - first == rest)
        def _(rest=rest):
            for u in range(rest + n_masked):
                if u + 1 < rest + n_masked:
                    s_scr[(u + 1) % unroll] = scores(first + u + 1)
                s = s_scr[u % unroll]
                _flash_update(s if u < rest else mask(s, first + u), values(first + u), m_scr, acc_scr)


def _nsa_sel_kernel(q_ref, k_ref, v_ref, sb_ref, kw_ref, vw_ref, o_ref, ow_ref, qa_scr, s_scr, m_scr, acc_scr,
                    *, tq, tk):
    i = pl.program_id(2)
    n = A_REP * tq
    q = q_ref[...]
    qs = jnp.concatenate([q[r * A_DIM:(r + 1) * A_DIM, :] for r in range(A_REP)], axis=1)
    sb = sb_ref[0].astype(BF16)
    qa_scr[...] = jnp.concatenate([qs, jnp.zeros_like(qs), jnp.concatenate([sb] * A_REP, axis=1)], axis=0)
    _flash_init(m_scr, acc_scr)

    def scores(j):
        return _dot(k_ref[0, pl.ds(pl.multiple_of(j * tk, tk), tk), :], qa_scr[...])

    def causal(s, j):
        kpos = j * tk + lax.broadcasted_iota(jnp.int32, (tk, n), 0)
        tok = i * tq + (lax.broadcasted_iota(jnp.int32, (tk, n), 1) & (tq - 1))
        return jnp.where(kpos <= tok, s, NEG)

    _flash_causal(scores, lambda j: v_ref[0, j], causal, (i * tq) // tk, max(1, tq // tk), s_scr, m_scr, acc_scr)
    o = _flash_result(acc_scr, A_DIM)
    for r in range(A_REP):
        o_ref[r * A_DIM:(r + 1) * A_DIM, :] = o[:, r * tq:(r + 1) * tq]

    _flash_init(m_scr, acc_scr)
    n_back = WINDOW // tq
    for c in range(n_back + 1):
        kt = i - n_back + c

        @pl.when(kt >= 0)
        def _(kt=kt):
            s = _dot(kw_ref[0, pl.ds(pl.multiple_of(kt * tq, tq), tq), :], qa_scr[0:128, :])
            kpos = kt * tq + lax.broadcasted_iota(jnp.int32, (tq, n), 0)
            tok = i * tq + (lax.broadcasted_iota(jnp.int32, (tq, n), 1) & (tq - 1))
            s = jnp.where((kpos <= tok) & (kpos > tok - WINDOW), s, NEG)
            _flash_update(s, vw_ref[0, kt], m_scr, acc_scr)

    o = _flash_result(acc_scr, A_DIM)
    for r in range(A_REP):
        ow_ref[r * A_DIM:(r + 1) * A_DIM, :] = o[:, r * tq:(r + 1) * tq]


def _nsa_sel(q_t, k_aug, v_slc_tiles, selb, k_win, v_win_tiles, batch, seq, tq, tk):
    nq, nk = seq // tq, seq // tk
    nb = selb.shape[1]
    t = q_t.shape[1]
    gr = A_REP * A_DIM
    n = A_REP * tq
    kw = k_aug.shape[2]
    out = jax.ShapeDtypeStruct((A_HEADS * A_DIM, t), F32)
    return pl.pallas_call(
        functools.partial(_nsa_sel_kernel, tq=tq, tk=tk),
        grid=(batch, A_GROUPS, nq),
        in_specs=[pl.BlockSpec((gr, tq), lambda b, g, i: (g, b * nq + i)),
                  pl.BlockSpec((1, seq, kw), lambda b, g, i: (g, b, 0)),
                  pl.BlockSpec((1, nk, A_DIM + ONES_ROWS, tk), lambda b, g, i: (g, b, 0, 0)),
                  pl.BlockSpec((1, nb, tq), lambda b, g, i: (g, 0, b * nq + i)),
                  pl.BlockSpec((1, seq, 128), lambda b, g, i: (g, b, 0)),
                  pl.BlockSpec((1, nq, A_DIM + ONES_ROWS, tq), lambda b, g, i: (g, b, 0, 0))],
        out_specs=[pl.BlockSpec((gr, tq), lambda b, g, i: (g, b * nq + i)),
                   pl.BlockSpec((gr, tq), lambda b, g, i: (g, b * nq + i))],
        out_shape=[out, out],
        scratch_shapes=[pltpu.VMEM((kw, n), BF16), pltpu.VMEM((FLASH_UNROLL, tk, n), F32),
                        pltpu.VMEM((1, n), F32), pltpu.VMEM((A_DIM + ONES_ROWS, n), F32)],
        compiler_params=_params(3),
        name="nsa_sel",
    )(q_t, k_aug, v_slc_tiles, selb, k_win, v_win_tiles)


def _mla_prep_kernel(cq_ref, ckv_ref, kr_ref, cos_ref, sin_ref, qlg_ref, kvlg_ref, qg_ref, kg_ref, wuq_ref, wukv_ref,
                     qo_ref, ko_ref, vo_ref):
    cos, sin = cos_ref[...], sin_ref[...]
    tt = cos.shape[1]
    q_all = _dot(wuq_ref[...], (_rms_rows(cq_ref[...]) * qlg_ref[...]).astype(BF16))
    kv_all = _dot(wukv_ref[...], (_rms_rows(ckv_ref[...]) * kvlg_ref[...]).astype(BF16))
    kr = kr_ref[...]
    pad = jnp.zeros((128 - B_QK, tt), F32)
    for h in range(B_HEADS):
        y = _rms_rows(q_all[h * B_QK:(h + 1) * B_QK]) * qg_ref[...]
        y = _rope_rows(y, cos, sin, B_NOPE, B_ROPE // 2) * (B_QK ** -0.5 * LOG2E)
        qo_ref[h] = jnp.concatenate([y, pad], axis=0).astype(BF16)
        base = h * (B_NOPE + B_V)
        k = jnp.concatenate([kv_all[base:base + B_NOPE], kr], axis=0)
        y = _rope_rows(_rms_rows(k) * kg_ref[...], cos, sin, B_NOPE, B_ROPE // 2)
        ko_ref[h] = jnp.concatenate([y, pad], axis=0).T.astype(BF16)
        vo_ref[h, 0] = _with_ones(kv_all[base + B_NOPE:base + B_NOPE + B_V])


def _mla_attn_kernel(q_ref, k_ref, v_ref, o_ref, s_scr, m_scr, acc_scr, *, tq, tk):
    i = pl.program_id(2)
    _flash_init(m_scr, acc_scr)

    def scores(j):
        return _dot(k_ref[0, pl.ds(pl.multiple_of(j * tk, tk), tk), :], q_ref[0])

    def causal(s, j):
        kpos = j * tk + lax.broadcasted_iota(jnp.int32, (tk, tq), 0)
        tok = i * tq + lax.broadcasted_iota(jnp.int32, (tk, tq), 1)
        return jnp.where(kpos <= tok, s, NEG)

    _flash_causal(scores, lambda j: v_ref[0, j], causal, (i * tq) // tk, max(1, tq // tk), s_scr, m_scr, acc_scr)
    o_ref[...] = _flash_result(acc_scr, B_V)


def _mla_attn(q_m, k_m, v_m_tiles, batch, seq, tq, tk):
    nq, nk = seq // tq, seq // tk
    t = q_m.shape[2]
    return pl.pallas_call(
        functools.partial(_mla_attn_kernel, tq=tq, tk=tk),
        grid=(batch, B_HEADS, nq),
        in_specs=[pl.BlockSpec((1, 128, tq), lambda b, h, i: (h, 0, b * nq + i)),
                  pl.BlockSpec((1, seq, 128), lambda b, h, i: (h, b, 0)),
                  pl.BlockSpec((1, nk, B_V + ONES_ROWS, tk), lambda b, h, i: (h, b, 0, 0))],
        out_specs=pl.BlockSpec((B_V, tq), lambda b, h, i: (h, b * nq + i)),
        out_shape=jax.ShapeDtypeStruct((B_HEADS * B_V, t), F32),
        scratch_shapes=[pltpu.VMEM((FLASH_UNROLL, tk, tq), F32),
                        pltpu.VMEM((1, tq), F32), pltpu.VMEM((B_V + ONES_ROWS, tq), F32)],
        compiler_params=_params(3),
        name="mla_attn",
    )(q_m, k_m, v_m_tiles)


def _out_proj_kernel(oc_ref, os_ref, ow_ref, gt_ref, ob_ref, x_ref, ga_ref, gb_ref, w_ref, g2_ref, h_ref, hn_ref):
    heads = []
    for h in range(A_HEADS):
        rows = slice(h * A_DIM, (h + 1) * A_DIM)
        heads.append(gt_ref[3 * h:3 * h + 1, :] * oc_ref[rows, :] + gt_ref[3 * h + 1:3 * h + 2, :] * os_ref[rows, :]
                     + gt_ref[3 * h + 2:3 * h + 3, :] * ow_ref[rows, :])
    oa = _rms_rows(jnp.concatenate(heads, axis=0)) * ga_ref[...]
    ob = _rms_rows(ob_ref[...]) * gb_ref[...]
    cat = jnp.concatenate([oa, ob], axis=0).astype(BF16)
    hid = x_ref[...].T + _dot(w_ref[...], cat)
    h_ref[...] = hid
    hn_ref[...] = (_rms_rows(hid) * g2_ref[...]).astype(BF16)


def _out_proj(oc_t, os_t, ow_t, gates_t, ob_t, x2, gain_a, gain_b, w_out_t, gain2, tt):
    t, d = x2.shape
    aw = oc_t.shape[0]
    bw = ob_t.shape[0]
    tok = lambda rows: pl.BlockSpec((rows, tt), lambda i: (0, i))
    full = lambda a: pl.BlockSpec(a.shape, lambda i: (0,) * a.ndim)
    return pl.pallas_call(
        _out_proj_kernel,
        grid=(t // tt,),
        in_specs=[tok(aw), tok(aw), tok(aw), tok(GATE_ROWS), tok(bw), pl.BlockSpec((tt, d), lambda i: (i, 0)),
                  full(gain_a), full(gain_b), full(w_out_t), full(gain2)],
        out_specs=[tok(d), tok(d)],
        out_shape=[jax.ShapeDtypeStruct((d, t), F32), jax.ShapeDtypeStruct((d, t), BF16)],
        compiler_params=_params(1),
        name="out_proj",
    )(oc_t, os_t, ow_t, gates_t, ob_t, x2, gain_a, gain_b, w_out_t, gain2)


def _top_ranked(s):
    n, rest = s.shape[0], s.shape[1:]
    row = lax.broadcasted_iota(jnp.int32, s.shape, 0).astype(F32)
    slot = lax.broadcasted_iota(jnp.int32, (P_TOPK,) + rest, 0)

    def body(a, carry):
        v, rank, vals = carry
        mx = jnp.max(v, axis=0, keepdims=True)
        first = jnp.min(jnp.where(v == mx, row, float(n)), axis=0, keepdims=True)
        hit = row == first
        rank = jnp.where(hit, jnp.asarray(a, F32), rank)
        v = jnp.where(hit, -jnp.inf, v)
        vals = jnp.where(slot == a, mx, vals)
        return v, rank, vals

    _, rank, vals = lax.fori_loop(0, P_TOPK, body,
                                  (s, jnp.full(s.shape, float(P_TOPK), F32), jnp.zeros((P_TOPK,) + rest, F32)))
    return rank, vals


def _pair_counts(v1, v2):
    k = v1.shape[0]
    slot = lax.broadcasted_iota(jnp.int32, v1.shape, 0).astype(F32)
    top = v1[0:1] + v2[0:1]

    def body(_, carry):
        count, front, z = carry
        mx = jnp.max(front, axis=0, keepdims=True)
        a_star = jnp.min(jnp.where(front == mx, slot, float(k)), axis=0, keepdims=True)
        hit = slot == a_star
        count = count + jnp.where(hit, 1.0, 0.0)
        nxt = jnp.sum(jnp.where(hit, count, 0.0), axis=0, keepdims=True)
        v2_nxt = jnp.sum(jnp.where(slot == nxt, v2, 0.0), axis=0, keepdims=True)
        front = jnp.where(hit, jnp.where(nxt < float(k), v1 + v2_nxt, -jnp.inf), front)
        return count, front, z + jnp.exp(mx - top)

    count, _, z = lax.fori_loop(0, k, body, (jnp.zeros(v1.shape, F32), v1 + v2[0:1], jnp.zeros(top.shape, F32)))
    return count, z


REMOVED = -2.0 ** 126
LANES = 128


def _top_ranked_pair_fast(s1, s2):
    n, tp = s1.shape
    slot = lax.broadcasted_iota(jnp.int32, (P_TOPK, tp), 0)

    def body(a, carry):
        code = REMOVED * (1.0 + jnp.asarray(a, F32) * (1.0 / 32.0))
        out = []
        for key, vals in (carry[0:2], carry[2:4]):
            mx = jnp.max(key, axis=0, keepdims=True)
            key = jnp.where(key == mx, code, key)
            out += [key, jnp.where(slot == a, mx, vals)]
        return tuple(out)

    zeros = jnp.zeros((P_TOPK, tp), F32)
    k1, t1, k2, t2 = lax.fori_loop(0, P_TOPK, body, (s1, zeros, s2, zeros))
    res, bad = [], jnp.zeros((1, tp), F32)
    for key, vals in ((k1, t1), (k2, t2)):
        removed = key <= REMOVED
        rank = jnp.where(removed, (key * (1.0 / REMOVED) - 1.0) * 32.0, float(P_TOPK))
        n_removed = jnp.sum(jnp.where(removed, 1.0, 0.0), axis=0, keepdims=True)
        bad = jnp.maximum(bad, jnp.abs(n_removed - float(P_TOPK)))
        res += [rank, vals]
    return res[0], res[1], res[2], res[3], bad


def _peer_route_kernel(hn_ref, wq_ref, keys_ref, cut_ref, g1_ref, r2_ref, g2_ref, q_scr, s_scr, rank_scr, vals_scr):
    tp = hn_ref.shape[1]
    half = P_KEY_DIM // 2
    q_scr[...] = _dot(wq_ref[...], hn_ref[...])
    for h in range(P_HEADS):
        q = _rms_rows(q_scr[h * P_KEY_DIM:(h + 1) * P_KEY_DIM, :]).astype(BF16)
        s_scr[0, h] = _dot(keys_ref[h, 0], q[:half])
        s_scr[1, h] = _dot(keys_ref[h, 1], q[half:])

    def first_level(rank_pair, flagged):
        for h in range(P_HEADS):
            for lt in range(tp // LANES):
                lanes = slice(lt * LANES, (lt + 1) * LANES)
                rank1, vals1, rank2, vals2, bad = rank_pair(s_scr[0, h, :, lanes], s_scr[1, h, :, lanes])
                flagged = jnp.maximum(flagged, bad)
                rank_scr[0, h, :, lanes] = rank1
                rank_scr[1, h, :, lanes] = rank2
                for a in range(P_TOPK):
                    vals_scr[0, a, h:h + 1, lanes] = vals1[a:a + 1]
                    vals_scr[1, a, h:h + 1, lanes] = vals2[a:a + 1]
        return flagged

    flagged = first_level(_top_ranked_pair_fast, jnp.zeros((1, LANES), F32))

    @pl.when(jnp.max(flagged) > 0.0)
    def _():
        first_level(lambda s1, s2: (*_top_ranked(s1), *_top_ranked(s2), jnp.zeros((1, LANES), F32)), flagged)

    count, z = _pair_counts(vals_scr[0], vals_scr[1])
    for h in range(P_HEADS):
        rank1 = rank_scr[0, h]
        cut = jnp.zeros_like(rank1)
        for a in range(P_TOPK):
            cut = jnp.where(rank1 == float(a), count[a, h:h + 1, :], cut)
        cut_ref[h] = cut.astype(BF16)
        g1_ref[h] = jnp.exp(s_scr[0, h] - vals_scr[0, 0, h:h + 1, :]).astype(BF16)
        r2_ref[h] = rank_scr[1, h].astype(BF16)
        g2_ref[h] = (jnp.exp(s_scr[1, h] - vals_scr[1, 0, h:h + 1, :]) * (1.0 / z[0, h:h + 1, :])).astype(BF16)


def _peer_route(hn_t, wq_t, sub_keys, tp):
    d, t = hn_t.shape
    halfs = jax.ShapeDtypeStruct((P_HEADS, N_KEYS, t), BF16)
    ospec = pl.BlockSpec((P_HEADS, N_KEYS, tp), lambda i: (0, 0, i))
    return pl.pallas_call(
        _peer_route_kernel,
        grid=(t // tp,),
        in_specs=[pl.BlockSpec((d, tp), lambda i: (0, i)),
                  pl.BlockSpec(wq_t.shape, lambda i: (0, 0)),
                  pl.BlockSpec(sub_keys.shape, lambda i: (0, 0, 0, 0))],
        out_specs=[ospec, ospec, ospec, ospec],
        out_shape=[halfs, halfs, halfs, halfs],
        scratch_shapes=[pltpu.VMEM((P_HEADS * P_KEY_DIM, tp), F32), pltpu.VMEM((2, P_HEADS, N_KEYS, tp), F32),
                        pltpu.VMEM((2, P_HEADS, N_KEYS, tp), F32), pltpu.VMEM((2, P_TOPK, P_HEADS, tp), F32)],
        compiler_params=_params(1),
        name="peer_route",
    )(hn_t, wq_t, sub_keys)


def _peer_ffn_kernel(hn_ref, h_ref, u_ref, v_ref, cut_ref, g1_ref, r2_ref, g2_ref, o_ref, acc_scr, *, te):
    e = pl.program_id(1)

    @pl.when(e == 0)
    def _():
        acc_scr[...] = jnp.zeros(acc_scr.shape, F32)

    hn = hn_ref[...]
    tt = hn.shape[1]

    def rows_bf16(ref, h, ii):
        return jnp.broadcast_to(ref[h, ii:ii + 1, :], (N_KEYS, tt))

    chunk = 2 * N_KEYS
    weights = []
    for c in range(te // chunk):
        a = _dot(u_ref[c * chunk:(c + 1) * chunk, :], hn)
        for k in range(chunk // N_KEYS):
            ii = c * (chunk // N_KEYS) + k
            gate = jnp.zeros((N_KEYS, tt), BF16)
            for h in range(P_HEADS):
                chosen = r2_ref[h] < rows_bf16(cut_ref, h, ii)
                gate = gate + jnp.where(chosen, g2_ref[h], jnp.zeros_like(gate)) * rows_bf16(g1_ref, h, ii)
            weights.append(gate * _gelu_tanh(a[k * N_KEYS:(k + 1) * N_KEYS, :]).astype(BF16))
    acc_scr[...] += _dot(v_ref[...], jnp.concatenate(weights, axis=0))

    @pl.when(e == pl.num_programs(1) - 1)
    def _():
        o_ref[...] = (h_ref[...] + acc_scr[...]).T


def _peer_ffn(hn_t, h_t, u_bf, v_t_bf, cut_k, g1_k, r2, g2, tt, te):
    d, t = hn_t.shape
    n_exp = u_bf.shape[0]
    kpe = te // N_KEYS
    return pl.pallas_call(
        functools.partial(_peer_ffn_kernel, te=te),
        grid=(t // tt, n_exp // te),
        in_specs=[pl.BlockSpec((d, tt), lambda i, e: (0, i)),
                  pl.BlockSpec((d, tt), lambda i, e: (0, i)),
                  pl.BlockSpec((te, d), lambda i, e: (e, 0)),
                  pl.BlockSpec((d, te), lambda i, e: (0, e)),
                  pl.BlockSpec((P_HEADS, kpe, tt), lambda i, e: (0, e, i)),
                  pl.BlockSpec((P_HEADS, kpe, tt), lambda i, e: (0, e, i)),
                  pl.BlockSpec((P_HEADS, N_KEYS, tt), lambda i, e: (0, 0, i)),
                  pl.BlockSpec((P_HEADS, N_KEYS, tt), lambda i, e: (0, 0, i))],
        out_specs=pl.BlockSpec((tt, d), lambda i, e: (i, 0)),
        out_shape=jax.ShapeDtypeStruct((t, d), F32),
        scratch_shapes=[pltpu.VMEM((d, tt), F32)],
        compiler_params=_params(2),
        name="peer_ffn",
    )(hn_t, h_t, u_bf, v_t_bf, cut_k, g1_k, r2, g2)


def _rope_tables_t(pos_flat, rot_dim):
    inv_freq = ROPE_THETA ** (-jnp.arange(0, rot_dim, 2, dtype=F32) / rot_dim)
    ang = pos_flat.astype(F32)[None, :] * inv_freq[:, None]
    return jnp.cos(ang), jnp.sin(ang)


def _expand_cmp_w1(w1):
    w = w1.reshape(CMP_LEN, A_DIM, CMP_HIDDEN)
    out = []
    for part in (w[:CMP_STRIDE], w[CMP_STRIDE:]):
        z = jnp.zeros_like(part)
        both = jnp.stack([jnp.concatenate([part, z], axis=1), jnp.concatenate([z, part], axis=1)])
        out.append(both.reshape(A_GROUPS, CMP_STRIDE * A_GROUPS * A_DIM, CMP_HIDDEN).transpose(0, 2, 1).astype(BF16))
    return out


TOKEN_TILE = 512
NSA_Q_TILE = 256
NSA_SEL_K_TILE = 512
MLA_Q_TILE = 1024
MLA_K_TILE = 512
PEER_ROUTE_TILE = 256
PEER_EXPERT_TILE = 2048


def _col(v):
    return v.reshape(-1, 1).astype(F32)


def _mixers(x, positions, norm1_gain, w_in, nsa_q_gain, nsa_kc_gain, nsa_ks_gain, nsa_kw_gain,
            cmp_pos, cmp_k_w1, cmp_k_w2, cmp_v_w1, cmp_v_w2,
            mla_q_lora_gain, mla_w_uq, mla_kv_lora_gain, mla_w_ukv, mla_q_gain, mla_k_gain):
    batch, seq, d = x.shape
    t = batch * seq
    tt = TOKEN_TILE
    tq_nsa = NSA_Q_TILE
    tk_sel = NSA_SEL_K_TILE
    tq_mla, tk_mla = MLA_Q_TILE, MLA_K_TILE
    assert tk_sel == tt and tk_mla == tt
    assert d == D_MODEL and seq % 512 == 0 and seq // SLC_LEN >= SLC_TOPK and WINDOW % tq_nsa == 0
    col = _col
    x2 = x.reshape(t, d)
    pos = positions.reshape(t)

    w_in_t = w_in.T
    gate_lo = sum((512, 128, 128, 128, 128, 128, 128))
    gate_hi = gate_lo + 3 * A_HEADS
    w_in_t = jnp.concatenate([w_in_t[:gate_lo], w_in_t[gate_hi:], w_in_t[gate_lo:gate_hi],
                              jnp.zeros((PROJ_ROWS - w_in_t.shape[0], d), F32)], axis=0).astype(BF16)
    cos_a, sin_a = _rope_tables_t(pos, A_ROPE)
    cos_b, sin_b = _rope_tables_t(pos, B_ROPE)
    q_t, kc_tm, vc_tm, k_slc, v_slc_t, k_win, v_win_t, gates_t, q_m, k_m, v_m_t = _in_proj(
        x2, norm1_gain.reshape(1, d), w_in_t, cos_a, sin_a, col(nsa_q_gain), col(nsa_ks_gain), col(nsa_kw_gain),
        cos_b, sin_b, col(mla_q_lora_gain), col(mla_kv_lora_gain), col(mla_q_gain), col(mla_k_gain),
        mla_w_uq.T.astype(BF16), mla_w_ukv.T.astype(BF16), tt, seq, tq_nsa)

    nc = seq // CMP_STRIDE
    chunk_w = CMP_STRIDE * A_GROUPS * A_DIM
    w1ka, w1kb = _expand_cmp_w1(cmp_k_w1)
    w1va, w1vb = _expand_cmp_w1(cmp_v_w1)
    pos_rows = lambda p: jnp.broadcast_to(p[:, None, :], (CMP_STRIDE, A_GROUPS, A_DIM)).reshape(1, chunk_w)
    cmp_end = jnp.minimum(jnp.arange(nc) * CMP_STRIDE + CMP_LEN - 1, seq - 1)
    cos_c, sin_c = _rope_tables_t(positions[:, cmp_end].reshape(-1), A_ROPE)
    to_b = lambda a: a.reshape(A_ROPE // 2, batch, nc).transpose(1, 0, 2)
    kcmp, vcmp_t = _compress(kc_tm.reshape(batch, nc, chunk_w), vc_tm.reshape(batch, nc, chunk_w),
                             w1ka, w1kb, w1va, w1vb, pos_rows(cmp_pos[:CMP_STRIDE]), pos_rows(cmp_pos[CMP_STRIDE:]),
                             cmp_k_w2.T.astype(BF16), cmp_v_w2.T.astype(BF16), col(nsa_kc_gain), to_b(cos_c), to_b(sin_c))

    n_cmp = (seq - CMP_LEN) // CMP_STRIDE + 1
    nb = seq // SLC_LEN
    c_start = np.arange(nc)[None, :] * CMP_STRIDE
    s_start = np.arange(nb)[:, None] * SLC_LEN
    ov = (c_start < s_start + SLC_LEN) & (c_start + CMP_LEN - 1 >= s_start) & (np.arange(nc)[None, :] < n_cmp)
    ov_t = jnp.asarray(ov.astype(np.float32)).astype(BF16)

    oc_t, selb = _nsa_cmp(q_t, kcmp, vcmp_t, ov_t, batch, seq, tq_nsa)
    os_t, ow_t = _nsa_sel(q_t, k_slc, v_slc_t, selb, k_win, v_win_t, batch, seq, tq_nsa, tk_sel)

    ob_t = _mla_attn(q_m, k_m, v_m_t, batch, seq, tq_mla, tk_mla)
    return oc_t, os_t, ow_t, gates_t, ob_t


def _peer(hn_t, h_t, peer_w_q, peer_sub_keys, peer_u, peer_v):
    cut, g1, r2, g2 = _peer_route(hn_t, peer_w_q.T.astype(BF16), peer_sub_keys.astype(BF16), PEER_ROUTE_TILE)
    return _peer_ffn(hn_t, h_t, peer_u.astype(BF16), peer_v.T.astype(BF16),
                     cut, g1, r2, g2, TOKEN_TILE, PEER_EXPERT_TILE)


def _layer(x, positions, norm1_gain, w_in, nsa_q_gain, nsa_kc_gain, nsa_ks_gain, nsa_kw_gain,
           cmp_pos, cmp_k_w1, cmp_k_w2, cmp_v_w1, cmp_v_w2,
           mla_q_lora_gain, mla_w_uq, mla_kv_lora_gain, mla_w_ukv, mla_q_gain, mla_k_gain,
           out_gain_a, out_gain_b, w_out, norm2_gain, peer_w_q, peer_sub_keys, peer_u, peer_v):
    batch, seq, d = x.shape
    oc_t, os_t, ow_t, gates_t, ob_t = _mixers(
        x, positions, norm1_gain, w_in, nsa_q_gain, nsa_kc_gain, nsa_ks_gain, nsa_kw_gain,
        cmp_pos, cmp_k_w1, cmp_k_w2, cmp_v_w1, cmp_v_w2,
        mla_q_lora_gain, mla_w_uq, mla_kv_lora_gain, mla_w_ukv, mla_q_gain, mla_k_gain)
    h_t, hn_t = _out_proj(oc_t, os_t, ow_t, gates_t, ob_t, x.reshape(batch * seq, d), _col(out_gain_a), _col(out_gain_b),
                          w_out.T.astype(BF16), _col(norm2_gain), TOKEN_TILE)
    return _peer(hn_t, h_t, peer_w_q, peer_sub_keys, peer_u, peer_v).reshape(batch, seq, d)


def kernel(x, positions, norm1_gain, w_in, nsa_q_gain, nsa_kc_gain, nsa_ks_gain, nsa_kw_gain, cmp_pos, cmp_k_w1, cmp_k_w2, cmp_v_w1, cmp_v_w2, mla_q_lora_gain, mla_w_uq, mla_kv_lora_gain, mla_w_ukv, mla_q_gain, mla_k_gain, out_gain_a, out_gain_b, w_out, norm2_gain, peer_w_q, peer_sub_keys, peer_u, peer_v):
    h = x
    for l in range(norm1_gain.shape[0]):
        h = _layer(h, positions, norm1_gain[l], w_in[l], nsa_q_gain[l], nsa_kc_gain[l], nsa_ks_gain[l], nsa_kw_gain[l],
                   cmp_pos[l], cmp_k_w1[l], cmp_k_w2[l], cmp_v_w1[l], cmp_v_w2[l],
                   mla_q_lora_gain[l], mla_w_uq[l], mla_kv_lora_gain[l], mla_w_ukv[l], mla_q_gain[l], mla_k_gain[l],
                   out_gain_a[l], out_gain_b[l], w_out[l], norm2_gain[l], peer_w_q[l], peer_sub_keys[l],
                   peer_u[l], peer_v[l])
    return h
```

```python
import functools

import jax
import jax.numpy as jnp
import numpy as np
from jax import lax
from jax.experimental import pallas as pl
from jax.experimental.pallas import tpu as pltpu

F32, BF16 = jnp.float32, jnp.bfloat16
EPS = 1e-6
NEG = -1e30
FORCE = 1e9
ROPE_THETA = 500000.0
LOG2E = 1.4426950408889634

D_MODEL = 1024
A_HEADS, A_GROUPS, A_DIM = 8, 2, 64
A_REP = A_HEADS // A_GROUPS
A_ROPE = A_DIM // 4
CMP_LEN, CMP_STRIDE, CMP_HIDDEN = 32, 16, 256
SLC_LEN, SLC_TOPK, WINDOW = 64, 16, 512
B_HEADS, Q_LORA, KV_LORA, B_NOPE, B_ROPE, B_V = 8, 256, 128, 64, 32, 64
B_QK = B_NOPE + B_ROPE
P_HEADS, N_KEYS, P_KEY_DIM, P_TOPK = 8, 128, 256, 16
N_EXPERTS = N_KEYS * N_KEYS

ROW_Q, ROW_KC, ROW_VC, ROW_KS, ROW_VS, ROW_KW, ROW_VW = 0, 512, 640, 768, 896, 1024, 1152
ROW_CQ, ROW_CKV, ROW_KR, ROW_GATE, PROJ_ROWS = 1280, 1536, 1664, 1696, 1728
GATE_ROWS = 32
KEY_NORM_ROWS = 8

VMEM_LIMIT = 56 * 1024 * 1024
NT_DIMS = (((1,), (1,)), ((), ()))


def _params(n_axes):
    return pltpu.CompilerParams(dimension_semantics=("arbitrary",) * n_axes, vmem_limit_bytes=VMEM_LIMIT)


def _dot(a, b):
    return jnp.dot(a, b, preferred_element_type=F32)


def _dot_nt(a, b):
    return lax.dot_general(a, b, NT_DIMS, preferred_element_type=F32)


def _row_sumsq(x):
    sq = x * x
    hi = sq.astype(BF16)
    lo = (sq - hi.astype(F32)).astype(BF16)
    ones = jnp.ones((8, x.shape[1]), BF16)
    return (_dot_nt(ones, hi) + _dot_nt(ones, lo))[0:1, :]


def _rms_rows(x):
    ss = jnp.sum(x * x, axis=0, keepdims=True)
    return x * lax.rsqrt(ss * (1.0 / x.shape[0]) + EPS)


def _rope_rows(y, cos, sin, off, half):
    x1, x2 = y[off:off + half], y[off + half:off + 2 * half]
    parts = [y[:off]] if off else []
    parts += [x1 * cos - x2 * sin, x2 * cos + x1 * sin]
    if off + 2 * half < y.shape[0]:
        parts.append(y[off + 2 * half:])
    return jnp.concatenate(parts, axis=0)


def _gelu_tanh(x):
    c = 0.7978845608028654
    half = 0.5 * x
    return half + half * jnp.tanh(x * (c + (c * 0.044715) * (x * x)))


def _in_proj_kernel(x_ref, g_ref, w_ref, cos_a_ref, sin_a_ref, qg_ref, ksg_ref, kwg_ref,
                    cos_b_ref, sin_b_ref, qlg_ref, kvlg_ref, mqg_ref, mkg_ref, wuq_ref, wukv_ref, *out_refs, seq, tw):
    x = x_ref[...]
    xg = (x * g_ref[...]).astype(BF16)
    p = _dot_nt(w_ref[...], xg)
    p = p * lax.rsqrt(_row_sumsq(x) * (1.0 / x.shape[1]) + EPS)
    gw = A_GROUPS * A_DIM
    rows = lambda start, n: p[start:start + n]
    _nsa_prep_kernel(rows(ROW_Q, A_HEADS * A_DIM), rows(ROW_KC, gw), rows(ROW_VC, gw), rows(ROW_KS, gw), rows(ROW_VS, gw),
                     rows(ROW_KW, gw), rows(ROW_VW, gw), rows(ROW_GATE, GATE_ROWS), cos_a_ref, sin_a_ref,
                     qg_ref, ksg_ref, kwg_ref, *out_refs[:8], out_refs[11], seq=seq, tw=tw)
    _mla_prep_kernel(rows(ROW_CQ, Q_LORA), rows(ROW_CKV, KV_LORA), rows(ROW_KR, B_ROPE), cos_b_ref, sin_b_ref,
                     qlg_ref, kvlg_ref, mqg_ref, mkg_ref, wuq_ref, wukv_ref, *out_refs[8:11], out_refs[12])


def _in_proj(x2, gain, w_t, cos_a, sin_a, q_gain, ks_gain, kw_gain, cos_b, sin_b, q_lora_gain, kv_lora_gain,
             mq_gain, mk_gain, wuq_t, wukv_t, tt, seq, tw):
    t, d = x2.shape
    gw = A_GROUPS * A_DIM
    kw = 128 + seq // SLC_LEN
    va, vb = A_DIM + ONES_ROWS, B_V + ONES_ROWS
    full = lambda a: pl.BlockSpec(a.shape, lambda i: (0,) * a.ndim)
    lanes = lambda n: pl.BlockSpec((n, tt), lambda i: (0, i))
    return pl.pallas_call(
        functools.partial(_in_proj_kernel, seq=seq, tw=tw),
        grid=(t // tt,),
        in_specs=[pl.BlockSpec((tt, d), lambda i: (i, 0)), full(gain), full(w_t),
                  lanes(A_ROPE // 2), lanes(A_ROPE // 2), full(q_gain), full(ks_gain), full(kw_gain),
                  lanes(B_ROPE // 2), lanes(B_ROPE // 2), full(q_lora_gain), full(kv_lora_gain), full(mq_gain),
                  full(mk_gain), full(wuq_t), full(wukv_t)],
        out_specs=[lanes(A_HEADS * A_DIM),
                   pl.BlockSpec((tt, gw), lambda i: (i, 0)),
                   pl.BlockSpec((tt, gw), lambda i: (i, 0)),
                   pl.BlockSpec((A_GROUPS, tt, kw), lambda i: (0, i, 0)),
                   pl.BlockSpec((A_GROUPS, 1, va, tt), lambda i: (0, i, 0, 0)),
                   pl.BlockSpec((A_GROUPS, tt, 128), lambda i: (0, i, 0)),
                   pl.BlockSpec((A_GROUPS, tt // tw, va, tw), lambda i: (0, i, 0, 0)),
                   lanes(GATE_ROWS),
                   pl.BlockSpec((B_HEADS, 128, tt), lambda i: (0, 0, i)),
                   pl.BlockSpec((B_HEADS, tt, 128), lambda i: (0, i, 0)),
                   pl.BlockSpec((B_HEADS, 1, vb, tt), lambda i: (0, i, 0, 0)),
                   lanes(KEY_NORM_ROWS), lanes(KEY_NORM_ROWS)],
        out_shape=[jax.ShapeDtypeStruct((A_HEADS * A_DIM, t), BF16),
                   jax.ShapeDtypeStruct((t, gw), F32),
                   jax.ShapeDtypeStruct((t, gw), F32),
                   jax.ShapeDtypeStruct((A_GROUPS, t, kw), BF16),
                   jax.ShapeDtypeStruct((A_GROUPS, t // tt, va, tt), BF16),
                   jax.ShapeDtypeStruct((A_GROUPS, t, 128), BF16),
                   jax.ShapeDtypeStruct((A_GROUPS, t // tw, va, tw), BF16),
                   jax.ShapeDtypeStruct((GATE_ROWS, t), F32),
                   jax.ShapeDtypeStruct((B_HEADS, 128, t), BF16),
                   jax.ShapeDtypeStruct((B_HEADS, t, 128), BF16),
                   jax.ShapeDtypeStruct((B_HEADS, t // tt, vb, tt), BF16),
                   jax.ShapeDtypeStruct((KEY_NORM_ROWS, t), F32),
                   jax.ShapeDtypeStruct((KEY_NORM_ROWS, t), F32)],
        compiler_params=_params(1),
        name="in_proj",
    )(x2, gain, w_t, cos_a, sin_a, q_gain, ks_gain, kw_gain, cos_b, sin_b, q_lora_gain, kv_lora_gain,
      mq_gain, mk_gain, wuq_t, wukv_t)


def _nsa_prep_kernel(q_ref, kc_ref, vc_ref, ks_ref, vs_ref, kw_ref, vw_ref, gt_ref, cos_ref, sin_ref,
                     qg_ref, ksg_ref, kwg_ref,
                     qo_ref, kco_ref, vco_ref, kso_ref, vso_ref, kwo_ref, vwo_ref, gto_ref, kn_ref, *, seq, tw):
    cos, sin = cos_ref[...], sin_ref[...]
    tt = cos.shape[1]
    nb = seq // SLC_LEN
    for h in range(A_HEADS):
        y = _rms_rows(q_ref[h * A_DIM:(h + 1) * A_DIM, :]) * qg_ref[...]
        y = _rope_rows(y, cos, sin, 0, A_ROPE // 2) * (A_DIM ** -0.5 * LOG2E)
        qo_ref[h * A_DIM:(h + 1) * A_DIM, :] = y.astype(BF16)
    kco_ref[...] = kc_ref[...].T
    vco_ref[...] = vc_ref[...].T
    zeros = jnp.zeros((A_DIM, tt), F32)
    tok = pl.program_id(0) * tt + lax.broadcasted_iota(jnp.int32, (tt, nb), 0)
    block_hot = jnp.where(lax.broadcasted_iota(jnp.int32, (tt, nb), 1) == (tok % seq) // SLC_LEN, 1.0, 0.0)
    kn_ref[...] = jnp.zeros(kn_ref.shape, F32)
    for g in range(A_GROUPS):
        for branch, (src, gain, dst) in enumerate(((ks_ref, ksg_ref, kso_ref), (kw_ref, kwg_ref, kwo_ref))):
            y = _rms_rows(src[g * A_DIM:(g + 1) * A_DIM, :]) * gain[...]
            y = _rope_rows(y, cos, sin, 0, A_ROPE // 2)
            row = branch * A_GROUPS + g
            kn_ref[row:row + 1, :] = jnp.sum(y * y, axis=0, keepdims=True)
            k_tm = jnp.concatenate([y, zeros], axis=0).T
            if dst is kso_ref:
                k_tm = jnp.concatenate([k_tm, block_hot], axis=1)
            dst[g] = k_tm.astype(BF16)
        vso_ref[g, 0] = _with_ones(vs_ref[g * A_DIM:(g + 1) * A_DIM, :])
        v_win = _with_ones(vw_ref[g * A_DIM:(g + 1) * A_DIM, :])
        for c in range(tt // tw):
            vwo_ref[g, c] = v_win[:, c * tw:(c + 1) * tw]
    gto_ref[...] = 1.0 / (1.0 + jnp.exp(-gt_ref[...]))


def _compress_kernel(kc_ref, vc_ref, w1ka_ref, w1kb_ref, w1va_ref, w1vb_ref, plo_ref, phi_ref,
                     w2k_ref, w2v_ref, kg_ref, cos_ref, sin_ref, ko_ref, vo_ref):
    nc = kc_ref.shape[1]
    zeros = jnp.zeros((A_DIM, nc), F32)
    for src, w1a, w1b, w2, is_k in ((kc_ref, w1ka_ref, w1kb_ref, w2k_ref, True),
                                    (vc_ref, w1va_ref, w1vb_ref, w2v_ref, False)):
        x = src[0]
        xlo = (x + plo_ref[...]).astype(BF16)
        xhi = (x + phi_ref[...]).astype(BF16)
        for g in range(A_GROUPS):
            first = _dot_nt(w1a[g], xlo)
            second = _dot_nt(w1b[g], xhi)
            hid = _gelu_tanh(first + pltpu.roll(second, nc - 1, axis=1)).astype(BF16)
            c = _dot(w2[...], hid)
            if is_k:
                y = _rope_rows(_rms_rows(c) * kg_ref[...], cos_ref[0], sin_ref[0], 0, A_ROPE // 2)
                ko_ref[0, g] = jnp.concatenate([y, zeros], axis=0).T.astype(BF16)
            else:
                vo_ref[0, g] = c.astype(BF16)


def _compress(kc_chunks, vc_chunks, w1ka, w1kb, w1va, w1vb, plo, phi, w2k_t, w2v_t, kc_gain, cos_c, sin_c):
    b, nc, cw = kc_chunks.shape
    full = lambda a: pl.BlockSpec(a.shape, lambda i: (0,) * a.ndim)
    return pl.pallas_call(
        _compress_kernel,
        grid=(b,),
        in_specs=[pl.BlockSpec((1, nc, cw), lambda i: (i, 0, 0)), pl.BlockSpec((1, nc, cw), lambda i: (i, 0, 0)),
                  full(w1ka), full(w1kb), full(w1va), full(w1vb), full(plo), full(phi), full(w2k_t), full(w2v_t),
                  full(kc_gain),
                  pl.BlockSpec((1, A_ROPE // 2, nc), lambda i: (i, 0, 0)),
                  pl.BlockSpec((1, A_ROPE // 2, nc), lambda i: (i, 0, 0))],
        out_specs=[pl.BlockSpec((1, A_GROUPS, nc, 128), lambda i: (i, 0, 0, 0)),
                   pl.BlockSpec((1, A_GROUPS, A_DIM, nc), lambda i: (i, 0, 0, 0))],
        out_shape=[jax.ShapeDtypeStruct((b, A_GROUPS, nc, 128), BF16),
                   jax.ShapeDtypeStruct((b, A_GROUPS, A_DIM, nc), BF16)],
        compiler_params=_params(1),
        name="nsa_compress",
    )(kc_chunks, vc_chunks, w1ka, w1kb, w1va, w1vb, plo, phi, w2k_t, w2v_t, kc_gain, cos_c, sin_c)


def _stack_heads(q, tq):
    qs = jnp.concatenate([q[r * A_DIM:(r + 1) * A_DIM, :] for r in range(A_REP)], axis=1)
    return jnp.concatenate([qs, jnp.zeros_like(qs)], axis=0)


def _nsa_cmp_kernel(q_ref, k_ref, v_ref, ov_ref, o_ref, sb_ref, *, tq):
    i = pl.program_id(2)
    n = A_REP * tq
    nc = k_ref.shape[2]
    nb = ov_ref.shape[0]
    qp = _stack_heads(q_ref[...], tq)
    s = _dot(k_ref[0, 0], qp)
    cmp_end = lax.broadcasted_iota(jnp.int32, (nc, n), 0) * CMP_STRIDE + (CMP_LEN - 1)
    tok = i * tq + (lax.broadcasted_iota(jnp.int32, (nc, n), 1) & (tq - 1))
    mask = cmp_end <= tok
    s = jnp.where(mask, s, NEG)
    m = jnp.max(s, axis=0, keepdims=True)
    p = jnp.where(mask, jnp.exp2(s - m), 0.0)
    l = jnp.sum(p, axis=0, keepdims=True)
    inv = jnp.where(l > 0.0, 1.0 / l, 0.0)
    pn = (p * inv).astype(BF16)
    oc = _dot(v_ref[0, 0], pn)
    for r in range(A_REP):
        o_ref[r * A_DIM:(r + 1) * A_DIM, :] = oc[:, r * tq:(r + 1) * tq]
    imp4 = _dot(ov_ref[...], pn)
    imp = imp4[:, 0:tq]
    for r in range(1, A_REP):
        imp = imp + imp4[:, r * tq:(r + 1) * tq]

    blk = lax.broadcasted_iota(jnp.int32, (nb, tq), 0)
    t = i * tq + lax.broadcasted_iota(jnp.int32, (nb, tq), 1)
    forced = (blk == t // SLC_LEN) | (blk == 0)
    v = jnp.where(forced, FORCE, jnp.where(blk * SLC_LEN <= t, imp, NEG))
    blk_f = blk.astype(F32)
    sel = jnp.zeros((nb, tq), F32)
    for _ in range(min(SLC_TOPK, nb)):
        mx = jnp.max(v, axis=0, keepdims=True)
        first = jnp.min(jnp.where(v == mx, blk_f, float(nb)), axis=0, keepdims=True)
        hit = blk_f == first
        sel = jnp.where(hit, 1.0, sel)
        v = jnp.where(hit, -jnp.inf, v)
    sb_ref[0] = jnp.where(sel > 0.0, 0.0, NEG)


def _nsa_cmp(q_t, kcmp, vcmp_t, ov_t, batch, seq, tq):
    nq = seq // tq
    nc = kcmp.shape[2]
    nb = ov_t.shape[0]
    t = q_t.shape[1]
    gr = A_REP * A_DIM
    return pl.pallas_call(
        functools.partial(_nsa_cmp_kernel, tq=tq),
        grid=(batch, A_GROUPS, nq),
        in_specs=[pl.BlockSpec((gr, tq), lambda b, g, i: (g, b * nq + i)),
                  pl.BlockSpec((1, 1, nc, 128), lambda b, g, i: (b, g, 0, 0)),
                  pl.BlockSpec((1, 1, A_DIM, nc), lambda b, g, i: (b, g, 0, 0)),
                  pl.BlockSpec((nb, nc), lambda b, g, i: (0, 0))],
        out_specs=[pl.BlockSpec((gr, tq), lambda b, g, i: (g, b * nq + i)),
                   pl.BlockSpec((1, nb, tq), lambda b, g, i: (g, 0, b * nq + i))],
        out_shape=[jax.ShapeDtypeStruct((A_HEADS * A_DIM, t), F32),
                   jax.ShapeDtypeStruct((A_GROUPS, nb, t), F32)],
        compiler_params=_params(3),
        name="nsa_cmp",
    )(q_t, kcmp, vcmp_t, ov_t)


ONES_ROWS = 16
FLASH_UNROLL = 4


def _with_ones(v):
    return jnp.concatenate([v, jnp.ones((ONES_ROWS, v.shape[1]), F32)], axis=0).astype(BF16)


def _flash_update(s, v_t, m_scr, acc_scr):
    m_prev = m_scr[...]
    m_new = jnp.maximum(m_prev, jnp.max(s, axis=0, keepdims=True))
    alpha = jnp.exp2(m_prev - m_new)
    p = jnp.exp2(s - m_new)
    acc_scr[...] = alpha * acc_scr[...] + _dot(v_t, p.astype(BF16))
    m_scr[...] = m_new


def _flash_update_bounded(s, v_t, m_scr, acc_scr):
    acc_scr[...] += _dot(v_t, jnp.exp2(s - m_scr[...]).astype(BF16))


def _flash_result(acc_scr, dv):
    acc = acc_scr[...]
    return acc[:dv] * (1.0 / acc[dv:dv + 1])


BOUND_LIMIT = 56.0


def _logit_bound(q, k_norm_max):
    qf = q.astype(F32)
    return jnp.sqrt(jnp.sum(qf * qf, axis=0, keepdims=True)) * (k_norm_max * 1.02)


def _flash_bounded_or_online(bound, run, m_scr, acc_scr):
    acc_scr[...] = jnp.zeros(acc_scr.shape, F32)
    small = jnp.max(bound) <= BOUND_LIMIT

    @pl.when(small)
    def _():
        m_scr[...] = bound
        run(_flash_update_bounded)

    @pl.when(jnp.logical_not(small))
    def _():
        m_scr[...] = jnp.full(m_scr.shape, -jnp.inf, F32)
        run(_flash_update)


def _flash_causal(scores, values, mask, n_full, n_masked, s_scr, m_scr, acc_scr, update):
    unroll = s_scr.shape[0]
    s_scr[0] = scores(0)

    def trip(t, carry):
        j = unroll * t
        for u in range(unroll):
            s_scr[(u + 1) % unroll] = scores(j + u + 1)
            update(s_scr[u], values(j + u), m_scr, acc_scr)
        return carry

    lax.fori_loop(0, n_full // unroll, trip, 0)
    first = (n_full // unroll) * unroll
    for rest in range(unroll):

        @pl.when(n_full - first == rest)
        def _(rest=rest):
            for u in range(rest + n_masked):
                if u + 1 < rest + n_masked:
                    s_scr[(u + 1) % unroll] = scores(first + u + 1)
                s = s_scr[u % unroll]
                update(s if u < rest else mask(s, first + u), values(first + u), m_scr, acc_scr)


def _nsa_sel_kernel(kmax_ref, q_ref, k_ref, v_ref, sb_ref, kw_ref, vw_ref, o_ref, ow_ref, qa_scr, s_scr, m_scr, acc_scr,
                    *, tq, tk):
    b, g, i = pl.program_id(0), pl.program_id(1), pl.program_id(2)
    n = A_REP * tq
    q = q_ref[...]
    qs = jnp.concatenate([q[r * A_DIM:(r + 1) * A_DIM, :] for r in range(A_REP)], axis=1)
    sb = sb_ref[0].astype(BF16)
    qa_scr[...] = jnp.concatenate([qs, jnp.zeros_like(qs), jnp.concatenate([sb] * A_REP, axis=1)], axis=0)
    n_batch = pl.num_programs(0)

    def scores(j):
        return _dot(k_ref[0, pl.ds(pl.multiple_of(j * tk, tk), tk), :], qa_scr[...])

    def causal(s, j):
        kpos = j * tk + lax.broadcasted_iota(jnp.int32, (tk, n), 0)
        tok = i * tq + (lax.broadcasted_iota(jnp.int32, (tk, n), 1) & (tq - 1))
        return jnp.where(kpos <= tok, s, NEG)

    _flash_bounded_or_online(
        _logit_bound(qs, kmax_ref[g * n_batch + b]),
        lambda update: _flash_causal(scores, lambda j: v_ref[0, j], causal, (i * tq) // tk, max(1, tq // tk),
                                     s_scr, m_scr, acc_scr, update),
        m_scr, acc_scr)
    o = _flash_result(acc_scr, A_DIM)
    for r in range(A_REP):
        o_ref[r * A_DIM:(r + 1) * A_DIM, :] = o[:, r * tq:(r + 1) * tq]

    n_back = WINDOW // tq

    def window(update):
        for c in range(n_back + 1):
            kt = i - n_back + c

            @pl.when(kt >= 0)
            def _(kt=kt):
                s = _dot(kw_ref[0, pl.ds(pl.multiple_of(kt * tq, tq), tq), :], qa_scr[0:128, :])
                kpos = kt * tq + lax.broadcasted_iota(jnp.int32, (tq, n), 0)
                tok = i * tq + (lax.broadcasted_iota(jnp.int32, (tq, n), 1) & (tq - 1))
                s = jnp.where((kpos <= tok) & (kpos > tok - WINDOW), s, NEG)
                update(s, vw_ref[0, kt], m_scr, acc_scr)

    _flash_bounded_or_online(_logit_bound(qs, kmax_ref[(A_GROUPS + g) * n_batch + b]), window, m_scr, acc_scr)
    o = _flash_result(acc_scr, A_DIM)
    for r in range(A_REP):
        ow_ref[r * A_DIM:(r + 1) * A_DIM, :] = o[:, r * tq:(r + 1) * tq]


def _nsa_sel(k_norm_max, q_t, k_aug, v_slc_tiles, selb, k_win, v_win_tiles, batch, seq, tq, tk):
    nq, nk = seq // tq, seq // tk
    nb = selb.shape[1]
    t = q_t.shape[1]
    gr = A_REP * A_DIM
    n = A_REP * tq
    kw = k_aug.shape[2]
    out = jax.ShapeDtypeStruct((A_HEADS * A_DIM, t), F32)
    grid_spec = pltpu.PrefetchScalarGridSpec(
        num_scalar_prefetch=1,
        grid=(batch, A_GROUPS, nq),
        in_specs=[pl.BlockSpec((gr, tq), lambda b, g, i, km: (g, b * nq + i)),
                  pl.BlockSpec((1, seq, kw), lambda b, g, i, km: (g, b, 0)),
                  pl.BlockSpec((1, nk, A_DIM + ONES_ROWS, tk), lambda b, g, i, km: (g, b, 0, 0)),
                  pl.BlockSpec((1, nb, tq), lambda b, g, i, km: (g, 0, b * nq + i)),
                  pl.BlockSpec((1, seq, 128), lambda b, g, i, km: (g, b, 0)),
                  pl.BlockSpec((1, nq, A_DIM + ONES_ROWS, tq), lambda b, g, i, km: (g, b, 0, 0))],
        out_specs=[pl.BlockSpec((gr, tq), lambda b, g, i, km: (g, b * nq + i)),
                   pl.BlockSpec((gr, tq), lambda b, g, i, km: (g, b * nq + i))],
        scratch_shapes=[pltpu.VMEM((kw, n), BF16), pltpu.VMEM((FLASH_UNROLL, tk, n), F32),
                        pltpu.VMEM((1, n), F32), pltpu.VMEM((A_DIM + ONES_ROWS, n), F32)],
    )
    return pl.pallas_call(
        functools.partial(_nsa_sel_kernel, tq=tq, tk=tk),
        grid_spec=grid_spec,
        out_shape=[out, out],
        compiler_params=_params(3),
        name="nsa_sel",
    )(k_norm_max, q_t, k_aug, v_slc_tiles, selb, k_win, v_win_tiles)


def _mla_prep_kernel(cq_ref, ckv_ref, kr_ref, cos_ref, sin_ref, qlg_ref, kvlg_ref, qg_ref, kg_ref, wuq_ref, wukv_ref,
                     qo_ref, ko_ref, vo_ref, kn_ref):
    cos, sin = cos_ref[...], sin_ref[...]
    tt = cos.shape[1]
    q_all = _dot(wuq_ref[...], (_rms_rows(cq_ref[...]) * qlg_ref[...]).astype(BF16))
    kv_all = _dot(wukv_ref[...], (_rms_rows(ckv_ref[...]) * kvlg_ref[...]).astype(BF16))
    kr = kr_ref[...]
    pad = jnp.zeros((128 - B_QK, tt), F32)
    for h in range(B_HEADS):
        y = _rms_rows(q_all[h * B_QK:(h + 1) * B_QK]) * qg_ref[...]
        y = _rope_rows(y, cos, sin, B_NOPE, B_ROPE // 2) * (B_QK ** -0.5 * LOG2E)
        qo_ref[h] = jnp.concatenate([y, pad], axis=0).astype(BF16)
        base = h * (B_NOPE + B_V)
        k = jnp.concatenate([kv_all[base:base + B_NOPE], kr], axis=0)
        y = _rope_rows(_rms_rows(k) * kg_ref[...], cos, sin, B_NOPE, B_ROPE // 2)
        kn_ref[h:h + 1, :] = jnp.sum(y * y, axis=0, keepdims=True)
        ko_ref[h] = jnp.concatenate([y, pad], axis=0).T.astype(BF16)
        vo_ref[h, 0] = _with_ones(kv_all[base + B_NOPE:base + B_NOPE + B_V])


def _mla_attn_kernel(kmax_ref, q_ref, k_ref, v_ref, o_ref, s_scr, m_scr, acc_scr, *, tq, tk):
    b, h, i = pl.program_id(0), pl.program_id(1), pl.program_id(2)

    def scores(j):
        return _dot(k_ref[0, pl.ds(pl.multiple_of(j * tk, tk), tk), :], q_ref[0])

    def causal(s, j):
        kpos = j * tk + lax.broadcasted_iota(jnp.int32, (tk, tq), 0)
        tok = i * tq + lax.broadcasted_iota(jnp.int32, (tk, tq), 1)
        return jnp.where(kpos <= tok, s, NEG)

    _flash_bounded_or_online(
        _logit_bound(q_ref[0], kmax_ref[h * pl.num_programs(0) + b]),
        lambda update: _flash_causal(scores, lambda j: v_ref[0, j], causal, (i * tq) // tk, max(1, tq // tk),
                                     s_scr, m_scr, acc_scr, update),
        m_scr, acc_scr)
    o_ref[...] = _flash_result(acc_scr, B_V)


def _mla_attn(k_norm_max, q_m, k_m, v_m_tiles, batch, seq, tq, tk):
    nq, nk = seq // tq, seq // tk
    t = q_m.shape[2]
    grid_spec = pltpu.PrefetchScalarGridSpec(
        num_scalar_prefetch=1,
        grid=(batch, B_HEADS, nq),
        in_specs=[pl.BlockSpec((1, 128, tq), lambda b, h, i, km: (h, 0, b * nq + i)),
                  pl.BlockSpec((1, seq, 128), lambda b, h, i, km: (h, b, 0)),
                  pl.BlockSpec((1, nk, B_V + ONES_ROWS, tk), lambda b, h, i, km: (h, b, 0, 0))],
        out_specs=pl.BlockSpec((B_V, tq), lambda b, h, i, km: (h, b * nq + i)),
        scratch_shapes=[pltpu.VMEM((FLASH_UNROLL, tk, tq), F32),
                        pltpu.VMEM((1, tq), F32), pltpu.VMEM((B_V + ONES_ROWS, tq), F32)],
    )
    return pl.pallas_call(
        functools.partial(_mla_attn_kernel, tq=tq, tk=tk),
        grid_spec=grid_spec,
        out_shape=jax.ShapeDtypeStruct((B_HEADS * B_V, t), F32),
        compiler_params=_params(3),
        name="mla_attn",
    )(k_norm_max, q_m, k_m, v_m_tiles)


def _out_proj_kernel(oc_ref, os_ref, ow_ref, gt_ref, ob_ref, x_ref, ga_ref, gb_ref, w_ref, g2_ref, h_ref, hn_ref):
    heads = []
    for h in range(A_HEADS):
        rows = slice(h * A_DIM, (h + 1) * A_DIM)
        heads.append(gt_ref[3 * h:3 * h + 1, :] * oc_ref[rows, :] + gt_ref[3 * h + 1:3 * h + 2, :] * os_ref[rows, :]
                     + gt_ref[3 * h + 2:3 * h + 3, :] * ow_ref[rows, :])
    oa = _rms_rows(jnp.concatenate(heads, axis=0)) * ga_ref[...]
    ob = _rms_rows(ob_ref[...]) * gb_ref[...]
    cat = jnp.concatenate([oa, ob], axis=0).astype(BF16)
    hid = x_ref[...].T + _dot(w_ref[...], cat)
    h_ref[...] = hid
    hn_ref[...] = (_rms_rows(hid) * g2_ref[...]).astype(BF16)


def _out_proj(oc_t, os_t, ow_t, gates_t, ob_t, x2, gain_a, gain_b, w_out_t, gain2, tt):
    t, d = x2.shape
    aw = oc_t.shape[0]
    bw = ob_t.shape[0]
    tok = lambda rows: pl.BlockSpec((rows, tt), lambda i: (0, i))
    full = lambda a: pl.BlockSpec(a.shape, lambda i: (0,) * a.ndim)
    return pl.pallas_call(
        _out_proj_kernel,
        grid=(t // tt,),
        in_specs=[tok(aw), tok(aw), tok(aw), tok(GATE_ROWS), tok(bw), pl.BlockSpec((tt, d), lambda i: (i, 0)),
                  full(gain_a), full(gain_b), full(w_out_t), full(gain2)],
        out_specs=[tok(d), tok(d)],
        out_shape=[jax.ShapeDtypeStruct((d, t), F32), jax.ShapeDtypeStruct((d, t), BF16)],
        compiler_params=_params(1),
        name="out_proj",
    )(oc_t, os_t, ow_t, gates_t, ob_t, x2, gain_a, gain_b, w_out_t, gain2)


def _top_ranked(s):
    n, rest = s.shape[0], s.shape[1:]
    row = lax.broadcasted_iota(jnp.int32, s.shape, 0).astype(F32)
    slot = lax.broadcasted_iota(jnp.int32, (P_TOPK,) + rest, 0)

    def body(a, carry):
        v, rank, vals = carry
        mx = jnp.max(v, axis=0, keepdims=True)
        first = jnp.min(jnp.where(v == mx, row, float(n)), axis=0, keepdims=True)
        hit = row == first
        rank = jnp.where(hit, jnp.asarray(a, F32), rank)
        v = jnp.where(hit, -jnp.inf, v)
        vals = jnp.where(slot == a, mx, vals)
        return v, rank, vals

    _, rank, vals = lax.fori_loop(0, P_TOPK, body,
                                  (s, jnp.full(s.shape, float(P_TOPK), F32), jnp.zeros((P_TOPK,) + rest, F32)))
    return rank, vals


def _pair_counts(v1, v2):
    k = v1.shape[0]
    slot = lax.broadcasted_iota(jnp.int32, v1.shape, 0).astype(F32)
    top = v1[0:1] + v2[0:1]

    def body(_, carry):
        count, front, z = carry
        mx = jnp.max(front, axis=0, keepdims=True)
        a_star = jnp.min(jnp.where(front == mx, slot, float(k)), axis=0, keepdims=True)
        hit = slot == a_star
        count = count + jnp.where(hit, 1.0, 0.0)
        nxt = jnp.sum(jnp.where(hit, count, 0.0), axis=0, keepdims=True)
        v2_nxt = jnp.sum(jnp.where(slot == nxt, v2, 0.0), axis=0, keepdims=True)
        front = jnp.where(hit, jnp.where(nxt < float(k), v1 + v2_nxt, -jnp.inf), front)
        return count, front, z + jnp.exp(mx - top)

    count, _, z = lax.fori_loop(0, k, body, (jnp.zeros(v1.shape, F32), v1 + v2[0:1], jnp.zeros(top.shape, F32)))
    return count, z


REMOVED = -2.0 ** 126
LANES = 128


def _top_ranked_pair_fast(s1, s2):
    n, tp = s1.shape
    slot = lax.broadcasted_iota(jnp.int32, (P_TOPK, tp), 0)

    def body(a, carry):
        code = REMOVED * (1.0 + jnp.asarray(a, F32) * (1.0 / 32.0))
        out = []
        for key, vals in (carry[0:2], carry[2:4]):
            mx = jnp.max(key, axis=0, keepdims=True)
            key = jnp.where(key == mx, code, key)
            out += [key, jnp.where(slot == a, mx, vals)]
        return tuple(out)

    zeros = jnp.zeros((P_TOPK, tp), F32)
    k1, t1, k2, t2 = lax.fori_loop(0, P_TOPK, body, (s1, zeros, s2, zeros))
    res, bad = [], jnp.zeros((1, tp), F32)
    for key, vals in ((k1, t1), (k2, t2)):
        removed = key <= REMOVED
        rank = jnp.where(removed, (key * (1.0 / REMOVED) - 1.0) * 32.0, float(P_TOPK))
        n_removed = jnp.sum(jnp.where(removed, 1.0, 0.0), axis=0, keepdims=True)
        bad = jnp.maximum(bad, jnp.abs(n_removed - float(P_TOPK)))
        res += [rank, vals]
    return res[0], res[1], res[2], res[3], bad


def _peer_route_kernel(hn_ref, wq_ref, keys_ref, cut_ref, g1_ref, r2_ref, g2_ref, q_scr, s_scr, rank_scr, vals_scr):
    tp = hn_ref.shape[1]
    half = P_KEY_DIM // 2
    q_scr[...] = _dot(wq_ref[...], hn_ref[...])
    for h in range(P_HEADS):
        q = _rms_rows(q_scr[h * P_KEY_DIM:(h + 1) * P_KEY_DIM, :]).astype(BF16)
        s_scr[0, h] = _dot(keys_ref[h, 0], q[:half])
        s_scr[1, h] = _dot(keys_ref[h, 1], q[half:])

    def first_level(rank_pair, flagged):
        for h in range(P_HEADS):
            for lt in range(tp // LANES):
                lanes = slice(lt * LANES, (lt + 1) * LANES)
                rank1, vals1, rank2, vals2, bad = rank_pair(s_scr[0, h, :, lanes], s_scr[1, h, :, lanes])
                flagged = jnp.maximum(flagged, bad)
                rank_scr[0, h, :, lanes] = rank1
                rank_scr[1, h, :, lanes] = rank2
                for a in range(P_TOPK):
                    vals_scr[0, a, h:h + 1, lanes] = vals1[a:a + 1]
                    vals_scr[1, a, h:h + 1, lanes] = vals2[a:a + 1]
        return flagged

    flagged = first_level(_top_ranked_pair_fast, jnp.zeros((1, LANES), F32))

    @pl.when(jnp.max(flagged) > 0.0)
    def _():
        first_level(lambda s1, s2: (*_top_ranked(s1), *_top_ranked(s2), jnp.zeros((1, LANES), F32)), flagged)

    count, z = _pair_counts(vals_scr[0], vals_scr[1])
    for h in range(P_HEADS):
        rank1 = rank_scr[0, h]
        cut = jnp.zeros_like(rank1)
        for a in range(P_TOPK):
            cut = jnp.where(rank1 == float(a), count[a, h:h + 1, :], cut)
        cut_ref[h] = cut.astype(BF16)
        g1_ref[h] = jnp.exp(s_scr[0, h] - vals_scr[0, 0, h:h + 1, :]).astype(BF16)
        r2_ref[h] = rank_scr[1, h].astype(BF16)
        g2_ref[h] = (jnp.exp(s_scr[1, h] - vals_scr[1, 0, h:h + 1, :]) * (1.0 / z[0, h:h + 1, :])).astype(BF16)


def _peer_route(hn_t, wq_t, sub_keys, tp):
    d, t = hn_t.shape
    halfs = jax.ShapeDtypeStruct((P_HEADS, N_KEYS, t), BF16)
    ospec = pl.BlockSpec((P_HEADS, N_KEYS, tp), lambda i: (0, 0, i))
    return pl.pallas_call(
        _peer_route_kernel,
        grid=(t // tp,),
        in_specs=[pl.BlockSpec((d, tp), lambda i: (0, i)),
                  pl.BlockSpec(wq_t.shape, lambda i: (0, 0)),
                  pl.BlockSpec(sub_keys.shape, lambda i: (0, 0, 0, 0))],
        out_specs=[ospec, ospec, ospec, ospec],
        out_shape=[halfs, halfs, halfs, halfs],
        scratch_shapes=[pltpu.VMEM((P_HEADS * P_KEY_DIM, tp), F32), pltpu.VMEM((2, P_HEADS, N_KEYS, tp), F32),
                        pltpu.VMEM((2, P_HEADS, N_KEYS, tp), F32), pltpu.VMEM((2, P_TOPK, P_HEADS, tp), F32)],
        compiler_params=_params(1),
        name="peer_route",
    )(hn_t, wq_t, sub_keys)


def _peer_ffn_kernel(hn_ref, h_ref, u_ref, v_ref, cut_ref, g1_ref, r2_ref, g2_ref, o_ref, acc_scr, *, te):
    e = pl.program_id(1)

    @pl.when(e == 0)
    def _():
        acc_scr[...] = jnp.zeros(acc_scr.shape, F32)

    hn = hn_ref[...]
    tt = hn.shape[1]

    def rows_bf16(ref, h, ii):
        return jnp.broadcast_to(ref[h, ii:ii + 1, :], (N_KEYS, tt))

    chunk = N_KEYS
    weights = []
    for c in range(te // chunk):
        a = _dot(u_ref[c * chunk:(c + 1) * chunk, :], hn)
        for k in range(chunk // N_KEYS):
            ii = c * (chunk // N_KEYS) + k
            gate = jnp.zeros((N_KEYS, tt), BF16)
            for h in range(P_HEADS):
                chosen = r2_ref[h] < rows_bf16(cut_ref, h, ii)
                gate = gate + jnp.where(chosen, g2_ref[h], jnp.zeros_like(gate)) * rows_bf16(g1_ref, h, ii)
            weights.append(gate * _gelu_tanh(a[k * N_KEYS:(k + 1) * N_KEYS, :]).astype(BF16))
    acc_scr[...] += _dot(v_ref[...], jnp.concatenate(weights, axis=0))

    @pl.when(e == pl.num_programs(1) - 1)
    def _():
        o_ref[...] = (h_ref[...] + acc_scr[...]).T


def _peer_ffn(hn_t, h_t, u_bf, v_t_bf, cut_k, g1_k, r2, g2, tt, te):
    d, t = hn_t.shape
    n_exp = u_bf.shape[0]
    kpe = te // N_KEYS
    return pl.pallas_call(
        functools.partial(_peer_ffn_kernel, te=te),
        grid=(t // tt, n_exp // te),
        in_specs=[pl.BlockSpec((d, tt), lambda i, e: (0, i)),
                  pl.BlockSpec((d, tt), lambda i, e: (0, i)),
                  pl.BlockSpec((te, d), lambda i, e: (e, 0)),
                  pl.BlockSpec((d, te), lambda i, e: (0, e)),
                  pl.BlockSpec((P_HEADS, kpe, tt), lambda i, e: (0, e, i)),
                  pl.BlockSpec((P_HEADS, kpe, tt), lambda i, e: (0, e, i)),
                  pl.BlockSpec((P_HEADS, N_KEYS, tt), lambda i, e: (0, 0, i)),
                  pl.BlockSpec((P_HEADS, N_KEYS, tt), lambda i, e: (0, 0, i))],
        out_specs=pl.BlockSpec((tt, d), lambda i, e: (i, 0)),
        out_shape=jax.ShapeDtypeStruct((t, d), F32),
        scratch_shapes=[pltpu.VMEM((d, tt), F32)],
        compiler_params=_params(2),
        name="peer_ffn",
    )(hn_t, h_t, u_bf, v_t_bf, cut_k, g1_k, r2, g2)


def _rope_tables_t(pos_flat, rot_dim):
    inv_freq = ROPE_THETA ** (-jnp.arange(0, rot_dim, 2, dtype=F32) / rot_dim)
    ang = pos_flat.astype(F32)[None, :] * inv_freq[:, None]
    return jnp.cos(ang), jnp.sin(ang)


def _expand_cmp_w1(w1):
    w = w1.reshape(CMP_LEN, A_DIM, CMP_HIDDEN)
    out = []
    for part in (w[:CMP_STRIDE], w[CMP_STRIDE:]):
        z = jnp.zeros_like(part)
        both = jnp.stack([jnp.concatenate([part, z], axis=1), jnp.concatenate([z, part], axis=1)])
        out.append(both.reshape(A_GROUPS, CMP_STRIDE * A_GROUPS * A_DIM, CMP_HIDDEN).transpose(0, 2, 1).astype(BF16))
    return out


TOKEN_TILE = 512
NSA_Q_TILE = 256
NSA_SEL_K_TILE = 512
MLA_Q_TILE = 1024
MLA_K_TILE = 512
PEER_ROUTE_TILE = 256
PEER_EXPERT_TILE = 2048


def _col(v):
    return v.reshape(-1, 1).astype(F32)


def _mixers(x, positions, norm1_gain, w_in, nsa_q_gain, nsa_kc_gain, nsa_ks_gain, nsa_kw_gain,
            cmp_pos, cmp_k_w1, cmp_k_w2, cmp_v_w1, cmp_v_w2,
            mla_q_lora_gain, mla_w_uq, mla_kv_lora_gain, mla_w_ukv, mla_q_gain, mla_k_gain):
    batch, seq, d = x.shape
    t = batch * seq
    tt = TOKEN_TILE
    tq_nsa = NSA_Q_TILE
    tk_sel = NSA_SEL_K_TILE
    tq_mla, tk_mla = MLA_Q_TILE, MLA_K_TILE
    assert tk_sel == tt and tk_mla == tt
    assert d == D_MODEL and seq % 512 == 0 and seq // SLC_LEN >= SLC_TOPK and WINDOW % tq_nsa == 0
    col = _col
    x2 = x.reshape(t, d)
    pos = positions.reshape(t)

    w_in_t = w_in.T
    gate_lo = sum((512, 128, 128, 128, 128, 128, 128))
    gate_hi = gate_lo + 3 * A_HEADS
    w_in_t = jnp.concatenate([w_in_t[:gate_lo], w_in_t[gate_hi:], w_in_t[gate_lo:gate_hi],
                              jnp.zeros((PROJ_ROWS - w_in_t.shape[0], d), F32)], axis=0).astype(BF16)
    cos_a, sin_a = _rope_tables_t(pos, A_ROPE)
    cos_b, sin_b = _rope_tables_t(pos, B_ROPE)
    q_t, kc_tm, vc_tm, k_slc, v_slc_t, k_win, v_win_t, gates_t, q_m, k_m, v_m_t, kn_a, kn_b = _in_proj(
        x2, norm1_gain.reshape(1, d), w_in_t, cos_a, sin_a, col(nsa_q_gain), col(nsa_ks_gain), col(nsa_kw_gain),
        cos_b, sin_b, col(mla_q_lora_gain), col(mla_kv_lora_gain), col(mla_q_gain), col(mla_k_gain),
        mla_w_uq.T.astype(BF16), mla_w_ukv.T.astype(BF16), tt, seq, tq_nsa)

    nc = seq // CMP_STRIDE
    chunk_w = CMP_STRIDE * A_GROUPS * A_DIM
    w1ka, w1kb = _expand_cmp_w1(cmp_k_w1)
    w1va, w1vb = _expand_cmp_w1(cmp_v_w1)
    pos_rows = lambda p: jnp.broadcast_to(p[:, None, :], (CMP_STRIDE, A_GROUPS, A_DIM)).reshape(1, chunk_w)
    cmp_end = jnp.minimum(jnp.arange(nc) * CMP_STRIDE + CMP_LEN - 1, seq - 1)
    cos_c, sin_c = _rope_tables_t(positions[:, cmp_end].reshape(-1), A_ROPE)
    to_b = lambda a: a.reshape(A_ROPE // 2, batch, nc).transpose(1, 0, 2)
    kcmp, vcmp_t = _compress(kc_tm.reshape(batch, nc, chunk_w), vc_tm.reshape(batch, nc, chunk_w),
                             w1ka, w1kb, w1va, w1vb, pos_rows(cmp_pos[:CMP_STRIDE]), pos_rows(cmp_pos[CMP_STRIDE:]),
                             cmp_k_w2.T.astype(BF16), cmp_v_w2.T.astype(BF16), col(nsa_kc_gain), to_b(cos_c), to_b(sin_c))

    n_cmp = (seq - CMP_LEN) // CMP_STRIDE + 1
    nb = seq // SLC_LEN
    c_start = np.arange(nc)[None, :] * CMP_STRIDE
    s_start = np.arange(nb)[:, None] * SLC_LEN
    ov = (c_start < s_start + SLC_LEN) & (c_start + CMP_LEN - 1 >= s_start) & (np.arange(nc)[None, :] < n_cmp)
    ov_t = jnp.asarray(ov.astype(np.float32)).astype(BF16)

    oc_t, selb = _nsa_cmp(q_t, kcmp, vcmp_t, ov_t, batch, seq, tq_nsa)
    norm_max = lambda kn: jnp.sqrt(jnp.max(kn.reshape(KEY_NORM_ROWS, batch, seq), axis=2)).reshape(-1)
    os_t, ow_t = _nsa_sel(norm_max(kn_a), q_t, k_slc, v_slc_t, selb, k_win, v_win_t, batch, seq, tq_nsa, tk_sel)

    ob_t = _mla_attn(norm_max(kn_b), q_m, k_m, v_m_t, batch, seq, tq_mla, tk_mla)
    return oc_t, os_t, ow_t, gates_t, ob_t


def _peer(hn_t, h_t, peer_w_q, peer_sub_keys, peer_u, peer_v):
    cut, g1, r2, g2 = _peer_route(hn_t, peer_w_q.T.astype(BF16), peer_sub_keys.astype(BF16), PEER_ROUTE_TILE)
    return _peer_ffn(hn_t, h_t, peer_u.astype(BF16), peer_v.T.astype(BF16),
                     cut, g1, r2, g2, TOKEN_TILE, PEER_EXPERT_TILE)


def _layer(x, positions, norm1_gain, w_in, nsa_q_gain, nsa_kc_gain, nsa_ks_gain, nsa_kw_gain,
           cmp_pos, cmp_k_w1, cmp_k_w2, cmp_v_w1, cmp_v_w2,
           mla_q_lora_gain, mla_w_uq, mla_kv_lora_gain, mla_w_ukv, mla_q_gain, mla_k_gain,
           out_gain_a, out_gain_b, w_out, norm2_gain, peer_w_q, peer_sub_keys, peer_u, peer_v):
    batch, seq, d = x.shape
    oc_t, os_t, ow_t, gates_t, ob_t = _mixers(
        x, positions, norm1_gain, w_in, nsa_q_gain, nsa_kc_gain, nsa_ks_gain, nsa_kw_gain,
        cmp_pos, cmp_k_w1, cmp_k_w2, cmp_v_w1, cmp_v_w2,
        mla_q_lora_gain, mla_w_uq, mla_kv_lora_gain, mla_w_ukv, mla_q_gain, mla_k_gain)
    h_t, hn_t = _out_proj(oc_t, os_t, ow_t, gates_t, ob_t, x.reshape(batch * seq, d), _col(out_gain_a), _col(out_gain_b),
                          w_out.T.astype(BF16), _col(norm2_gain), TOKEN_TILE)
    return _peer(hn_t, h_t, peer_w_q, peer_sub_keys, peer_u, peer_v).reshape(batch, seq, d)


def kernel(x, positions, norm1_gain, w_in, nsa_q_gain, nsa_kc_gain, nsa_ks_gain, nsa_kw_gain, cmp_pos, cmp_k_w1, cmp_k_w2, cmp_v_w1, cmp_v_w2, mla_q_lora_gain, mla_w_uq, mla_kv_lora_gain, mla_w_ukv, mla_q_gain, mla_k_gain, out_gain_a, out_gain_b, w_out, norm2_gain, peer_w_q, peer_sub_keys, peer_u, peer_v):
    h = x
    for l in range(norm1_gain.shape[0]):
        h = _layer(h, positions, norm1_gain[l], w_in[l], nsa_q_gain[l], nsa_kc_gain[l], nsa_ks_gain[l], nsa_kw_gain[l],
                   cmp_pos[l], cmp_k_w1[l], cmp_k_w2[l], cmp_v_w1[l], cmp_v_w2[l],
                   mla_q_lora_gain[l], mla_w_uq[l], mla_kv_lora_gain[l], mla_w_ukv[l], mla_q_gain[l], mla_k_gain[l],
                   out_gain_a[l], out_gain_b[l], w_out[l], norm2_gain[l], peer_w_q[l], peer_sub_keys[l],
                   peer_u[l], peer_v[l])
    return h
```

```python
import functools

import jax
import jax.numpy as jnp
import numpy as np
from jax import lax
from jax.experimental import pallas as pl
from jax.experimental.pallas import tpu as pltpu

F32, BF16 = jnp.float32, jnp.bfloat16
EPS = 1e-6
NEG = -1e30
FORCE = 1e9
ROPE_THETA = 500000.0
LOG2E = 1.4426950408889634

D_MODEL = 1024
A_HEADS, A_GROUPS, A_DIM = 8, 2, 64
A_REP = A_HEADS // A_GROUPS
A_ROPE = A_DIM // 4
CMP_LEN, CMP_STRIDE, CMP_HIDDEN = 32, 16, 256
SLC_LEN, SLC_TOPK, WINDOW = 64, 16, 512
B_HEADS, Q_LORA, KV_LORA, B_NOPE, B_ROPE, B_V = 8, 256, 128, 64, 32, 64
B_QK = B_NOPE + B_ROPE
P_HEADS, N_KEYS, P_KEY_DIM, P_TOPK = 8, 128, 256, 16
N_EXPERTS = N_KEYS * N_KEYS

ROW_Q, ROW_KC, ROW_VC, ROW_KS, ROW_VS, ROW_KW, ROW_VW = 0, 512, 640, 768, 896, 1024, 1152
ROW_CQ, ROW_CKV, ROW_KR, ROW_GATE, PROJ_ROWS = 1280, 1536, 1664, 1696, 1728
GATE_ROWS = 32
KEY_NORM_ROWS = 8

VMEM_LIMIT = 56 * 1024 * 1024
NT_DIMS = (((1,), (1,)), ((), ()))


def _params(n_axes):
    return pltpu.CompilerParams(dimension_semantics=("arbitrary",) * n_axes, vmem_limit_bytes=VMEM_LIMIT)


def _dot(a, b):
    return jnp.dot(a, b, preferred_element_type=F32)


def _dot_nt(a, b):
    return lax.dot_general(a, b, NT_DIMS, preferred_element_type=F32)


def _row_sumsq(x):
    sq = x * x
    hi = sq.astype(BF16)
    lo = (sq - hi.astype(F32)).astype(BF16)
    ones = jnp.ones((8, x.shape[1]), BF16)
    return (_dot_nt(ones, hi) + _dot_nt(ones, lo))[0:1, :]


def _rms_rows(x):
    ss = jnp.sum(x * x, axis=0, keepdims=True)
    return x * lax.rsqrt(ss * (1.0 / x.shape[0]) + EPS)


def _rope_rows(y, cos, sin, off, half):
    x1, x2 = y[off:off + half], y[off + half:off + 2 * half]
    parts = [y[:off]] if off else []
    parts += [x1 * cos - x2 * sin, x2 * cos + x1 * sin]
    if off + 2 * half < y.shape[0]:
        parts.append(y[off + 2 * half:])
    return jnp.concatenate(parts, axis=0)


def _gelu_tanh(x):
    c = 0.7978845608028654
    half = 0.5 * x
    return half + half * jnp.tanh(x * (c + (c * 0.044715) * (x * x)))


def _in_proj_kernel(x_ref, g_ref, w_ref, cos_a_ref, sin_a_ref, qg_ref, ksg_ref, kwg_ref,
                    cos_b_ref, sin_b_ref, qlg_ref, kvlg_ref, mqg_ref, mkg_ref, wuq_ref, wukv_ref, *out_refs, seq, tw):
    x = x_ref[...]
    xg = (x * g_ref[...]).astype(BF16)
    p = _dot_nt(w_ref[...], xg)
    p = p * lax.rsqrt(_row_sumsq(x) * (1.0 / x.shape[1]) + EPS)
    gw = A_GROUPS * A_DIM
    rows = lambda start, n: p[start:start + n]
    _nsa_prep_kernel(rows(ROW_Q, A_HEADS * A_DIM), rows(ROW_KC, gw), rows(ROW_VC, gw), rows(ROW_KS, gw), rows(ROW_VS, gw),
                     rows(ROW_KW, gw), rows(ROW_VW, gw), rows(ROW_GATE, GATE_ROWS), cos_a_ref, sin_a_ref,
                     qg_ref, ksg_ref, kwg_ref, *out_refs[:8], out_refs[11], seq=seq, tw=tw)
    _mla_prep_kernel(rows(ROW_CQ, Q_LORA), rows(ROW_CKV, KV_LORA), rows(ROW_KR, B_ROPE), cos_b_ref, sin_b_ref,
                     qlg_ref, kvlg_ref, mqg_ref, mkg_ref, wuq_ref, wukv_ref, *out_refs[8:11], out_refs[12])


def _in_proj(x2, gain, w_t, cos_a, sin_a, q_gain, ks_gain, kw_gain, cos_b, sin_b, q_lora_gain, kv_lora_gain,
             mq_gain, mk_gain, wuq_t, wukv_t, tt, seq, tw):
    t, d = x2.shape
    gw = A_GROUPS * A_DIM
    kw = 128 + seq // SLC_LEN
    va, vb = A_DIM + ONES_ROWS, B_V + ONES_ROWS
    full = lambda a: pl.BlockSpec(a.shape, lambda i: (0,) * a.ndim)
    lanes = lambda n: pl.BlockSpec((n, tt), lambda i: (0, i))
    return pl.pallas_call(
        functools.partial(_in_proj_kernel, seq=seq, tw=tw),
        grid=(t // tt,),
        in_specs=[pl.BlockSpec((tt, d), lambda i: (i, 0)), full(gain), full(w_t),
                  lanes(A_ROPE // 2), lanes(A_ROPE // 2), full(q_gain), full(ks_gain), full(kw_gain),
                  lanes(B_ROPE // 2), lanes(B_ROPE // 2), full(q_lora_gain), full(kv_lora_gain), full(mq_gain),
                  full(mk_gain), full(wuq_t), full(wukv_t)],
        out_specs=[lanes(A_HEADS * A_DIM),
                   pl.BlockSpec((tt, gw), lambda i: (i, 0)),
                   pl.BlockSpec((tt, gw), lambda i: (i, 0)),
                   pl.BlockSpec((A_GROUPS, tt, kw), lambda i: (0, i, 0)),
                   pl.BlockSpec((A_GROUPS, 1, va, tt), lambda i: (0, i, 0, 0)),
                   pl.BlockSpec((A_GROUPS, tt, 128), lambda i: (0, i, 0)),
                   pl.BlockSpec((A_GROUPS, tt // tw, va, tw), lambda i: (0, i, 0, 0)),
                   lanes(GATE_ROWS),
                   pl.BlockSpec((B_HEADS, 128, tt), lambda i: (0, 0, i)),
                   pl.BlockSpec((B_HEADS, tt, 128), lambda i: (0, i, 0)),
                   pl.BlockSpec((B_HEADS, 1, vb, tt), lambda i: (0, i, 0, 0)),
                   lanes(KEY_NORM_ROWS), lanes(KEY_NORM_ROWS)],
        out_shape=[jax.ShapeDtypeStruct((A_HEADS * A_DIM, t), BF16),
                   jax.ShapeDtypeStruct((t, gw), F32),
                   jax.ShapeDtypeStruct((t, gw), F32),
                   jax.ShapeDtypeStruct((A_GROUPS, t, kw), BF16),
                   jax.ShapeDtypeStruct((A_GROUPS, t // tt, va, tt), BF16),
                   jax.ShapeDtypeStruct((A_GROUPS, t, 128), BF16),
                   jax.ShapeDtypeStruct((A_GROUPS, t // tw, va, tw), BF16),
                   jax.ShapeDtypeStruct((GATE_ROWS, t), F32),
                   jax.ShapeDtypeStruct((B_HEADS, 128, t), BF16),
                   jax.ShapeDtypeStruct((B_HEADS, t, 128), BF16),
                   jax.ShapeDtypeStruct((B_HEADS, t // tt, vb, tt), BF16),
                   jax.ShapeDtypeStruct((KEY_NORM_ROWS, t), F32),
                   jax.ShapeDtypeStruct((KEY_NORM_ROWS, t), F32)],
        compiler_params=_params(1),
        name="in_proj",
    )(x2, gain, w_t, cos_a, sin_a, q_gain, ks_gain, kw_gain, cos_b, sin_b, q_lora_gain, kv_lora_gain,
      mq_gain, mk_gain, wuq_t, wukv_t)


def _nsa_prep_kernel(q_ref, kc_ref, vc_ref, ks_ref, vs_ref, kw_ref, vw_ref, gt_ref, cos_ref, sin_ref,
                     qg_ref, ksg_ref, kwg_ref,
                     qo_ref, kco_ref, vco_ref, kso_ref, vso_ref, kwo_ref, vwo_ref, gto_ref, kn_ref, *, seq, tw):
    cos, sin = cos_ref[...], sin_ref[...]
    tt = cos.shape[1]
    nb = seq // SLC_LEN
    for h in range(A_HEADS):
        y = _rms_rows(q_ref[h * A_DIM:(h + 1) * A_DIM, :]) * qg_ref[...]
        y = _rope_rows(y, cos, sin, 0, A_ROPE // 2) * (A_DIM ** -0.5 * LOG2E)
        qo_ref[h * A_DIM:(h + 1) * A_DIM, :] = y.astype(BF16)
    kco_ref[...] = kc_ref[...].T
    vco_ref[...] = vc_ref[...].T
    zeros = jnp.zeros((A_DIM, tt), F32)
    tok = pl.program_id(0) * tt + lax.broadcasted_iota(jnp.int32, (tt, nb), 0)
    block_hot = jnp.where(lax.broadcasted_iota(jnp.int32, (tt, nb), 1) == (tok % seq) // SLC_LEN, 1.0, 0.0)
    kn_ref[...] = jnp.zeros(kn_ref.shape, F32)
    for g in range(A_GROUPS):
        for branch, (src, gain, dst) in enumerate(((ks_ref, ksg_ref, kso_ref), (kw_ref, kwg_ref, kwo_ref))):
            y = _rms_rows(src[g * A_DIM:(g + 1) * A_DIM, :]) * gain[...]
            y = _rope_rows(y, cos, sin, 0, A_ROPE // 2)
            row = branch * A_GROUPS + g
            kn_ref[row:row + 1, :] = jnp.sum(y * y, axis=0, keepdims=True)
            k_tm = jnp.concatenate([y, zeros], axis=0).T
            if dst is kso_ref:
                k_tm = jnp.concatenate([k_tm, block_hot], axis=1)
            dst[g] = k_tm.astype(BF16)
        vso_ref[g, 0] = _with_ones(vs_ref[g * A_DIM:(g + 1) * A_DIM, :])
        v_win = _with_ones(vw_ref[g * A_DIM:(g + 1) * A_DIM, :])
        for c in range(tt // tw):
            vwo_ref[g, c] = v_win[:, c * tw:(c + 1) * tw]
    gto_ref[...] = 1.0 / (1.0 + jnp.exp(-gt_ref[...]))


def _compress_kernel(kc_ref, vc_ref, w1ka_ref, w1kb_ref, w1va_ref, w1vb_ref, plo_ref, phi_ref,
                     w2k_ref, w2v_ref, kg_ref, cos_ref, sin_ref, ko_ref, vo_ref):
    nc = kc_ref.shape[1]
    zeros = jnp.zeros((A_DIM, nc), F32)
    for src, w1a, w1b, w2, is_k in ((kc_ref, w1ka_ref, w1kb_ref, w2k_ref, True),
                                    (vc_ref, w1va_ref, w1vb_ref, w2v_ref, False)):
        x = src[0]
        xlo = (x + plo_ref[...]).astype(BF16)
        xhi = (x + phi_ref[...]).astype(BF16)
        for g in range(A_GROUPS):
            first = _dot_nt(w1a[g], xlo)
            second = _dot_nt(w1b[g], xhi)
            hid = _gelu_tanh(first + pltpu.roll(second, nc - 1, axis=1)).astype(BF16)
            c = _dot(w2[...], hid)
            if is_k:
                y = _rope_rows(_rms_rows(c) * kg_ref[...], cos_ref[0], sin_ref[0], 0, A_ROPE // 2)
                ko_ref[0, g] = jnp.concatenate([y, zeros], axis=0).T.astype(BF16)
            else:
                vo_ref[0, g] = _with_ones(c)


def _compress(kc_chunks, vc_chunks, w1ka, w1kb, w1va, w1vb, plo, phi, w2k_t, w2v_t, kc_gain, cos_c, sin_c):
    b, nc, cw = kc_chunks.shape
    full = lambda a: pl.BlockSpec(a.shape, lambda i: (0,) * a.ndim)
    return pl.pallas_call(
        _compress_kernel,
        grid=(b,),
        in_specs=[pl.BlockSpec((1, nc, cw), lambda i: (i, 0, 0)), pl.BlockSpec((1, nc, cw), lambda i: (i, 0, 0)),
                  full(w1ka), full(w1kb), full(w1va), full(w1vb), full(plo), full(phi), full(w2k_t), full(w2v_t),
                  full(kc_gain),
                  pl.BlockSpec((1, A_ROPE // 2, nc), lambda i: (i, 0, 0)),
                  pl.BlockSpec((1, A_ROPE // 2, nc), lambda i: (i, 0, 0))],
        out_specs=[pl.BlockSpec((1, A_GROUPS, nc, 128), lambda i: (i, 0, 0, 0)),
                   pl.BlockSpec((1, A_GROUPS, A_DIM + ONES_ROWS, nc), lambda i: (i, 0, 0, 0))],
        out_shape=[jax.ShapeDtypeStruct((b, A_GROUPS, nc, 128), BF16),
                   jax.ShapeDtypeStruct((b, A_GROUPS, A_DIM + ONES_ROWS, nc), BF16)],
        compiler_params=_params(1),
        name="nsa_compress",
    )(kc_chunks, vc_chunks, w1ka, w1kb, w1va, w1vb, plo, phi, w2k_t, w2v_t, kc_gain, cos_c, sin_c)


def _stack_heads(q, tq):
    qs = jnp.concatenate([q[r * A_DIM:(r + 1) * A_DIM, :] for r in range(A_REP)], axis=1)
    return jnp.concatenate([qs, jnp.zeros_like(qs)], axis=0)


def _nsa_cmp_kernel(q_ref, k_ref, v_ref, ov_ref, o_ref, sb_ref, *, tq):
    i = pl.program_id(2)
    n = A_REP * tq
    nc = k_ref.shape[2]
    nb = ov_ref.shape[0]
    qp = _stack_heads(q_ref[...], tq)
    s = _dot(k_ref[0, 0], qp)
    cmp_end = lax.broadcasted_iota(jnp.int32, (nc, n), 0) * CMP_STRIDE + (CMP_LEN - 1)
    tok = i * tq + (lax.broadcasted_iota(jnp.int32, (nc, n), 1) & (tq - 1))
    s = jnp.where(cmp_end <= tok, s, NEG)
    m = jnp.max(s, axis=0, keepdims=True)
    p = jnp.exp2(s - jnp.where(m > 0.5 * NEG, m, 0.0)).astype(BF16)
    ocl = _dot(v_ref[0, 0], p)
    l = ocl[A_DIM:A_DIM + 1]
    inv = jnp.where(l > 0.0, 1.0 / l, 0.0)
    oc = ocl[:A_DIM] * inv
    for r in range(A_REP):
        o_ref[r * A_DIM:(r + 1) * A_DIM, :] = oc[:, r * tq:(r + 1) * tq]
    imp4 = _dot(ov_ref[...], p) * inv
    imp = imp4[:, 0:tq]
    for r in range(1, A_REP):
        imp = imp + imp4[:, r * tq:(r + 1) * tq]

    blk = lax.broadcasted_iota(jnp.int32, (nb, tq), 0)
    t = i * tq + lax.broadcasted_iota(jnp.int32, (nb, tq), 1)
    forced = (blk == t // SLC_LEN) | (blk == 0)
    v0 = jnp.where(forced, FORCE, jnp.where(blk * SLC_LEN <= t, imp, NEG))
    topk = min(SLC_TOPK, nb)

    v = jnp.where(blk == t // SLC_LEN, 2.0 * FORCE, v0)
    for _ in range(topk):
        v = jnp.where(v == jnp.max(v, axis=0, keepdims=True), -jnp.inf, v)
    taken = v == -jnp.inf
    sb_ref[0] = jnp.where(taken, 0.0, NEG)
    n_taken = jnp.sum(jnp.where(taken, 1.0, 0.0), axis=0, keepdims=True)

    @pl.when(jnp.max(jnp.abs(n_taken - float(topk))) > 0.0)
    def _():
        blk_f = blk.astype(F32)
        w = v0
        sel = jnp.zeros((nb, tq), F32)
        for _ in range(topk):
            mx = jnp.max(w, axis=0, keepdims=True)
            first = jnp.min(jnp.where(w == mx, blk_f, float(nb)), axis=0, keepdims=True)
            hit = blk_f == first
            sel = jnp.where(hit, 1.0, sel)
            w = jnp.where(hit, -jnp.inf, w)
        sb_ref[0] = jnp.where(sel > 0.0, 0.0, NEG)


def _nsa_cmp(q_t, kcmp, vcmp_t, ov_t, batch, seq, tq):
    nq = seq // tq
    nc = kcmp.shape[2]
    nb = ov_t.shape[0]
    t = q_t.shape[1]
    gr = A_REP * A_DIM
    return pl.pallas_call(
        functools.partial(_nsa_cmp_kernel, tq=tq),
        grid=(batch, A_GROUPS, nq),
        in_specs=[pl.BlockSpec((gr, tq), lambda b, g, i: (g, b * nq + i)),
                  pl.BlockSpec((1, 1, nc, 128), lambda b, g, i: (b, g, 0, 0)),
                  pl.BlockSpec((1, 1, A_DIM + ONES_ROWS, nc), lambda b, g, i: (b, g, 0, 0)),
                  pl.BlockSpec((nb, nc), lambda b, g, i: (0, 0))],
        out_specs=[pl.BlockSpec((gr, tq), lambda b, g, i: (g, b * nq + i)),
                   pl.BlockSpec((1, nb, tq), lambda b, g, i: (g, 0, b * nq + i))],
        out_shape=[jax.ShapeDtypeStruct((A_HEADS * A_DIM, t), F32),
                   jax.ShapeDtypeStruct((A_GROUPS, nb, t), F32)],
        compiler_params=_params(3),
        name="nsa_cmp",
    )(q_t, kcmp, vcmp_t, ov_t)


ONES_ROWS = 16
FLASH_UNROLL = 4


def _with_ones(v):
    return jnp.concatenate([v, jnp.ones((ONES_ROWS, v.shape[1]), F32)], axis=0).astype(BF16)


def _flash_update(s, v_t, m_scr, acc_scr):
    m_prev = m_scr[...]
    m_new = jnp.maximum(m_prev, jnp.max(s, axis=0, keepdims=True))
    alpha = jnp.exp2(m_prev - m_new)
    p = jnp.exp2(s - m_new)
    acc_scr[...] = alpha * acc_scr[...] + _dot(v_t, p.astype(BF16))
    m_scr[...] = m_new


def _flash_update_bounded(s, v_t, m_scr, acc_scr):
    acc_scr[...] += _dot(v_t, jnp.exp2(s - m_scr[...]).astype(BF16))


def _flash_result(acc_scr, dv):
    acc = acc_scr[...]
    return acc[:dv] * (1.0 / acc[dv:dv + 1])


BOUND_LIMIT = 56.0


def _logit_bound(q, k_norm_max):
    qf = q.astype(F32)
    return jnp.sqrt(jnp.sum(qf * qf, axis=0, keepdims=True)) * (k_norm_max * 1.02)


def _flash_bounded_or_online(bound, run, m_scr, acc_scr):
    acc_scr[...] = jnp.zeros(acc_scr.shape, F32)
    small = jnp.max(bound) <= BOUND_LIMIT

    @pl.when(small)
    def _():
        m_scr[...] = bound
        run(_flash_update_bounded)

    @pl.when(jnp.logical_not(small))
    def _():
        m_scr[...] = jnp.full(m_scr.shape, -jnp.inf, F32)
        run(_flash_update)


def _flash_causal(scores, values, mask, n_full, n_masked, s_scr, m_scr, acc_scr, update):
    unroll = s_scr.shape[0]
    s_scr[0] = scores(0)

    def trip(t, carry):
        j = unroll * t
        for u in range(unroll):
            s_scr[(u + 1) % unroll] = scores(j + u + 1)
            update(s_scr[u], values(j + u), m_scr, acc_scr)
        return carry

    lax.fori_loop(0, n_full // unroll, trip, 0)
    first = (n_full // unroll) * unroll
    for rest in range(unroll):

        @pl.when(n_full - first == rest)
        def _(rest=rest):
            for u in range(rest + n_masked):
                if u + 1 < rest + n_masked:
                    s_scr[(u + 1) % unroll] = scores(first + u + 1)
                s = s_scr[u % unroll]
                update(s if u < rest else mask(s, first + u), values(first + u), m_scr, acc_scr)


def _nsa_sel_kernel(kmax_ref, q_ref, k_ref, v_ref, sb_ref, kw_ref, vw_ref, o_ref, ow_ref, qa_scr, s_scr, m_scr, acc_scr,
                    *, tq, tk):
    b, g, i = pl.program_id(0), pl.program_id(1), pl.program_id(2)
    n = A_REP * tq
    q = q_ref[...]
    qs = jnp.concatenate([q[r * A_DIM:(r + 1) * A_DIM, :] for r in range(A_REP)], axis=1)
    sb = sb_ref[0].astype(BF16)
    qa_scr[...] = jnp.concatenate([qs, jnp.zeros_like(qs), jnp.concatenate([sb] * A_REP, axis=1)], axis=0)
    n_batch = pl.num_programs(0)

    def scores(j):
        return _dot(k_ref[0, pl.ds(pl.multiple_of(j * tk, tk), tk), :], qa_scr[...])

    def causal(s, j):
        kpos = j * tk + lax.broadcasted_iota(jnp.int32, (tk, n), 0)
        tok = i * tq + (lax.broadcasted_iota(jnp.int32, (tk, n), 1) & (tq - 1))
        return jnp.where(kpos <= tok, s, NEG)

    _flash_bounded_or_online(
        _logit_bound(qs, kmax_ref[g * n_batch + b]),
        lambda update: _flash_causal(scores, lambda j: v_ref[0, j], causal, (i * tq) // tk, max(1, tq // tk),
                                     s_scr, m_scr, acc_scr, update),
        m_scr, acc_scr)
    o = _flash_result(acc_scr, A_DIM)
    for r in range(A_REP):
        o_ref[r * A_DIM:(r + 1) * A_DIM, :] = o[:, r * tq:(r + 1) * tq]

    n_back = WINDOW // tq

    def window(update):
        for c in range(n_back + 1):
            kt = i - n_back + c

            @pl.when(kt >= 0)
            def _(kt=kt):
                s = _dot(kw_ref[0, pl.ds(pl.multiple_of(kt * tq, tq), tq), :], qa_scr[0:128, :])
                kpos = kt * tq + lax.broadcasted_iota(jnp.int32, (tq, n), 0)
                tok = i * tq + (lax.broadcasted_iota(jnp.int32, (tq, n), 1) & (tq - 1))
                s = jnp.where((kpos <= tok) & (kpos > tok - WINDOW), s, NEG)
                update(s, vw_ref[0, kt], m_scr, acc_scr)

    _flash_bounded_or_online(_logit_bound(qs, kmax_ref[(A_GROUPS + g) * n_batch + b]), window, m_scr, acc_scr)
    o = _flash_result(acc_scr, A_DIM)
    for r in range(A_REP):
        ow_ref[r * A_DIM:(r + 1) * A_DIM, :] = o[:, r * tq:(r + 1) * tq]


def _nsa_sel(k_norm_max, q_t, k_aug, v_slc_tiles, selb, k_win, v_win_tiles, batch, seq, tq, tk):
    nq, nk = seq // tq, seq // tk
    nb = selb.shape[1]
    t = q_t.shape[1]
    gr = A_REP * A_DIM
    n = A_REP * tq
    kw = k_aug.shape[2]
    out = jax.ShapeDtypeStruct((A_HEADS * A_DIM, t), F32)
    grid_spec = pltpu.PrefetchScalarGridSpec(
        num_scalar_prefetch=1,
        grid=(batch, A_GROUPS, nq),
        in_specs=[pl.BlockSpec((gr, tq), lambda b, g, i, km: (g, b * nq + i)),
                  pl.BlockSpec((1, seq, kw), lambda b, g, i, km: (g, b, 0)),
                  pl.BlockSpec((1, nk, A_DIM + ONES_ROWS, tk), lambda b, g, i, km: (g, b, 0, 0)),
                  pl.BlockSpec((1, nb, tq), lambda b, g, i, km: (g, 0, b * nq + i)),
                  pl.BlockSpec((1, seq, 128), lambda b, g, i, km: (g, b, 0)),
                  pl.BlockSpec((1, nq, A_DIM + ONES_ROWS, tq), lambda b, g, i, km: (g, b, 0, 0))],
        out_specs=[pl.BlockSpec((gr, tq), lambda b, g, i, km: (g, b * nq + i)),
                   pl.BlockSpec((gr, tq), lambda b, g, i, km: (g, b * nq + i))],
        scratch_shapes=[pltpu.VMEM((kw, n), BF16), pltpu.VMEM((FLASH_UNROLL, tk, n), F32),
                        pltpu.VMEM((1, n), F32), pltpu.VMEM((A_DIM + ONES_ROWS, n), F32)],
    )
    return pl.pallas_call(
        functools.partial(_nsa_sel_kernel, tq=tq, tk=tk),
        grid_spec=grid_spec,
        out_shape=[out, out],
        compiler_params=_params(3),
        name="nsa_sel",
    )(k_norm_max, q_t, k_aug, v_slc_tiles, selb, k_win, v_win_tiles)


def _mla_prep_kernel(cq_ref, ckv_ref, kr_ref, cos_ref, sin_ref, qlg_ref, kvlg_ref, qg_ref, kg_ref, wuq_ref, wukv_ref,
                     qo_ref, ko_ref, vo_ref, kn_ref):
    cos, sin = cos_ref[...], sin_ref[...]
    tt = cos.shape[1]
    q_all = _dot(wuq_ref[...], (_rms_rows(cq_ref[...]) * qlg_ref[...]).astype(BF16))
    kv_all = _dot(wukv_ref[...], (_rms_rows(ckv_ref[...]) * kvlg_ref[...]).astype(BF16))
    kr = kr_ref[...]
    pad = jnp.zeros((128 - B_QK, tt), F32)
    for h in range(B_HEADS):
        y = _rms_rows(q_all[h * B_QK:(h + 1) * B_QK]) * qg_ref[...]
        y = _rope_rows(y, cos, sin, B_NOPE, B_ROPE // 2) * (B_QK ** -0.5 * LOG2E)
        qo_ref[h] = jnp.concatenate([y, pad], axis=0).astype(BF16)
        base = h * (B_NOPE + B_V)
        k = jnp.concatenate([kv_all[base:base + B_NOPE], kr], axis=0)
        y = _rope_rows(_rms_rows(k) * kg_ref[...], cos, sin, B_NOPE, B_ROPE // 2)
        kn_ref[h:h + 1, :] = jnp.sum(y * y, axis=0, keepdims=True)
        ko_ref[h] = jnp.concatenate([y, pad], axis=0).T.astype(BF16)
        vo_ref[h, 0] = _with_ones(kv_all[base + B_NOPE:base + B_NOPE + B_V])


def _mla_attn_kernel(kmax_ref, q_ref, k_ref, v_ref, o_ref, s_scr, m_scr, acc_scr, *, tq, tk):
    b, h, i = pl.program_id(0), pl.program_id(1), pl.program_id(2)

    def scores(j):
        return _dot(k_ref[0, pl.ds(pl.multiple_of(j * tk, tk), tk), :], q_ref[0])

    def causal(s, j):
        kpos = j * tk + lax.broadcasted_iota(jnp.int32, (tk, tq), 0)
        tok = i * tq + lax.broadcasted_iota(jnp.int32, (tk, tq), 1)
        return jnp.where(kpos <= tok, s, NEG)

    _flash_bounded_or_online(
        _logit_bound(q_ref[0], kmax_ref[h * pl.num_programs(0) + b]),
        lambda update: _flash_causal(scores, lambda j: v_ref[0, j], causal, (i * tq) // tk, max(1, tq // tk),
                                     s_scr, m_scr, acc_scr, update),
        m_scr, acc_scr)
    o_ref[...] = _flash_result(acc_scr, B_V)


def _mla_attn(k_norm_max, q_m, k_m, v_m_tiles, batch, seq, tq, tk):
    nq, nk = seq // tq, seq // tk
    t = q_m.shape[2]
    grid_spec = pltpu.PrefetchScalarGridSpec(
        num_scalar_prefetch=1,
        grid=(batch, B_HEADS, nq),
        in_specs=[pl.BlockSpec((1, 128, tq), lambda b, h, i, km: (h, 0, b * nq + i)),
                  pl.BlockSpec((1, seq, 128), lambda b, h, i, km: (h, b, 0)),
                  pl.BlockSpec((1, nk, B_V + ONES_ROWS, tk), lambda b, h, i, km: (h, b, 0, 0))],
        out_specs=pl.BlockSpec((B_V, tq), lambda b, h, i, km: (h, b * nq + i)),
        scratch_shapes=[pltpu.VMEM((FLASH_UNROLL, tk, tq), F32),
                        pltpu.VMEM((1, tq), F32), pltpu.VMEM((B_V + ONES_ROWS, tq), F32)],
    )
    return pl.pallas_call(
        functools.partial(_mla_attn_kernel, tq=tq, tk=tk),
        grid_spec=grid_spec,
        out_shape=jax.ShapeDtypeStruct((B_HEADS * B_V, t), F32),
        compiler_params=_params(3),
        name="mla_attn",
    )(k_norm_max, q_m, k_m, v_m_tiles)


def _out_proj_kernel(oc_ref, os_ref, ow_ref, gt_ref, ob_ref, x_ref, ga_ref, gb_ref, w_ref, g2_ref, h_ref, hn_ref):
    heads = []
    for h in range(A_HEADS):
        rows = slice(h * A_DIM, (h + 1) * A_DIM)
        heads.append(gt_ref[3 * h:3 * h + 1, :] * oc_ref[rows, :] + gt_ref[3 * h + 1:3 * h + 2, :] * os_ref[rows, :]
                     + gt_ref[3 * h + 2:3 * h + 3, :] * ow_ref[rows, :])
    oa = _rms_rows(jnp.concatenate(heads, axis=0)) * ga_ref[...]
    ob = _rms_rows(ob_ref[...]) * gb_ref[...]
    cat = jnp.concatenate([oa, ob], axis=0).astype(BF16)
    hid = x_ref[...].T + _dot(w_ref[...], cat)
    h_ref[...] = hid
    hn_ref[...] = (_rms_rows(hid) * g2_ref[...]).astype(BF16)


def _out_proj(oc_t, os_t, ow_t, gates_t, ob_t, x2, gain_a, gain_b, w_out_t, gain2, tt):
    t, d = x2.shape
    aw = oc_t.shape[0]
    bw = ob_t.shape[0]
    tok = lambda rows: pl.BlockSpec((rows, tt), lambda i: (0, i))
    full = lambda a: pl.BlockSpec(a.shape, lambda i: (0,) * a.ndim)
    return pl.pallas_call(
        _out_proj_kernel,
        grid=(t // tt,),
        in_specs=[tok(aw), tok(aw), tok(aw), tok(GATE_ROWS), tok(bw), pl.BlockSpec((tt, d), lambda i: (i, 0)),
                  full(gain_a), full(gain_b), full(w_out_t), full(gain2)],
        out_specs=[tok(d), tok(d)],
        out_shape=[jax.ShapeDtypeStruct((d, t), F32), jax.ShapeDtypeStruct((d, t), BF16)],
        compiler_params=_params(1),
        name="out_proj",
    )(oc_t, os_t, ow_t, gates_t, ob_t, x2, gain_a, gain_b, w_out_t, gain2)


def _top_ranked(s):
    n, rest = s.shape[0], s.shape[1:]
    row = lax.broadcasted_iota(jnp.int32, s.shape, 0).astype(F32)
    slot = lax.broadcasted_iota(jnp.int32, (P_TOPK,) + rest, 0)

    def body(a, carry):
        v, rank, vals = carry
        mx = jnp.max(v, axis=0, keepdims=True)
        first = jnp.min(jnp.where(v == mx, row, float(n)), axis=0, keepdims=True)
        hit = row == first
        rank = jnp.where(hit, jnp.asarray(a, F32), rank)
        v = jnp.where(hit, -jnp.inf, v)
        vals = jnp.where(slot == a, mx, vals)
        return v, rank, vals

    _, rank, vals = lax.fori_loop(0, P_TOPK, body,
                                  (s, jnp.full(s.shape, float(P_TOPK), F32), jnp.zeros((P_TOPK,) + rest, F32)))
    return rank, vals


def _pair_counts(v1, v2):
    k = v1.shape[0]
    slot = lax.broadcasted_iota(jnp.int32, v1.shape, 0).astype(F32)
    top = v1[0:1] + v2[0:1]

    def body(_, carry):
        count, front, z = carry
        mx = jnp.max(front, axis=0, keepdims=True)
        a_star = jnp.min(jnp.where(front == mx, slot, float(k)), axis=0, keepdims=True)
        hit = slot == a_star
        count = count + jnp.where(hit, 1.0, 0.0)
        nxt = jnp.sum(jnp.where(hit, count, 0.0), axis=0, keepdims=True)
        v2_nxt = jnp.sum(jnp.where(slot == nxt, v2, 0.0), axis=0, keepdims=True)
        front = jnp.where(hit, jnp.where(nxt < float(k), v1 + v2_nxt, -jnp.inf), front)
        return count, front, z + jnp.exp(mx - top)

    count, _, z = lax.fori_loop(0, k, body, (jnp.zeros(v1.shape, F32), v1 + v2[0:1], jnp.zeros(top.shape, F32)))
    return count, z


REMOVED = -2.0 ** 126
LANES = 128


def _top_ranked_pair_fast(s1, s2):
    n, tp = s1.shape
    slot = lax.broadcasted_iota(jnp.int32, (P_TOPK, tp), 0)

    def body(a, carry):
        code = REMOVED * (1.0 + jnp.asarray(a, F32) * (1.0 / 32.0))
        out = []
        for key, vals in (carry[0:2], carry[2:4]):
            mx = jnp.max(key, axis=0, keepdims=True)
            key = jnp.where(key == mx, code, key)
            out += [key, jnp.where(slot == a, mx, vals)]
        return tuple(out)

    zeros = jnp.zeros((P_TOPK, tp), F32)
    k1, t1, k2, t2 = lax.fori_loop(0, P_TOPK, body, (s1, zeros, s2, zeros))
    res, bad = [], jnp.zeros((1, tp), F32)
    for key, vals in ((k1, t1), (k2, t2)):
        removed = key <= REMOVED
        rank = jnp.where(removed, (key * (1.0 / REMOVED) - 1.0) * 32.0, float(P_TOPK))
        n_removed = jnp.sum(jnp.where(removed, 1.0, 0.0), axis=0, keepdims=True)
        bad = jnp.maximum(bad, jnp.abs(n_removed - float(P_TOPK)))
        res += [rank, vals]
    return res[0], res[1], res[2], res[3], bad


def _peer_route_kernel(hn_ref, wq_ref, keys_ref, cut_ref, g1_ref, r2_ref, g2_ref, q_scr, s_scr, rank_scr, vals_scr):
    tp = hn_ref.shape[1]
    half = P_KEY_DIM // 2
    q_scr[...] = _dot(wq_ref[...], hn_ref[...])
    for h in range(P_HEADS):
        q = _rms_rows(q_scr[h * P_KEY_DIM:(h + 1) * P_KEY_DIM, :]).astype(BF16)
        s_scr[0, h] = _dot(keys_ref[h, 0], q[:half])
        s_scr[1, h] = _dot(keys_ref[h, 1], q[half:])

    def first_level(rank_pair, flagged):
        for h in range(P_HEADS):
            for lt in range(tp // LANES):
                lanes = slice(lt * LANES, (lt + 1) * LANES)
                rank1, vals1, rank2, vals2, bad = rank_pair(s_scr[0, h, :, lanes], s_scr[1, h, :, lanes])
                flagged = jnp.maximum(flagged, bad)
                rank_scr[0, h, :, lanes] = rank1
                rank_scr[1, h, :, lanes] = rank2
                for a in range(P_TOPK):
                    vals_scr[0, a, h:h + 1, lanes] = vals1[a:a + 1]
                    vals_scr[1, a, h:h + 1, lanes] = vals2[a:a + 1]
        return flagged

    flagged = first_level(_top_ranked_pair_fast, jnp.zeros((1, LANES), F32))

    @pl.when(jnp.max(flagged) > 0.0)
    def _():
        first_level(lambda s1, s2: (*_top_ranked(s1), *_top_ranked(s2), jnp.zeros((1, LANES), F32)), flagged)

    count, z = _pair_counts(vals_scr[0], vals_scr[1])
    for h in range(P_HEADS):
        rank1 = rank_scr[0, h]
        cut = jnp.zeros_like(rank1)
        for a in range(P_TOPK):
            cut = jnp.where(rank1 == float(a), count[a, h:h + 1, :], cut)
        cut_ref[h] = cut.astype(BF16)
        g1_ref[h] = jnp.exp(s_scr[0, h] - vals_scr[0, 0, h:h + 1, :]).astype(BF16)
        r2_ref[h] = rank_scr[1, h].astype(BF16)
        g2_ref[h] = (jnp.exp(s_scr[1, h] - vals_scr[1, 0, h:h + 1, :]) * (1.0 / z[0, h:h + 1, :])).astype(BF16)


def _peer_route(hn_t, wq_t, sub_keys, tp):
    d, t = hn_t.shape
    halfs = jax.ShapeDtypeStruct((P_HEADS, N_KEYS, t), BF16)
    ospec = pl.BlockSpec((P_HEADS, N_KEYS, tp), lambda i: (0, 0, i))
    return pl.pallas_call(
        _peer_route_kernel,
        grid=(t // tp,),
        in_specs=[pl.BlockSpec((d, tp), lambda i: (0, i)),
                  pl.BlockSpec(wq_t.shape, lambda i: (0, 0)),
                  pl.BlockSpec(sub_keys.shape, lambda i: (0, 0, 0, 0))],
        out_specs=[ospec, ospec, ospec, ospec],
        out_shape=[halfs, halfs, halfs, halfs],
        scratch_shapes=[pltpu.VMEM((P_HEADS * P_KEY_DIM, tp), F32), pltpu.VMEM((2, P_HEADS, N_KEYS, tp), F32),
                        pltpu.VMEM((2, P_HEADS, N_KEYS, tp), F32), pltpu.VMEM((2, P_TOPK, P_HEADS, tp), F32)],
        compiler_params=_params(1),
        name="peer_route",
    )(hn_t, wq_t, sub_keys)


def _peer_ffn_kernel(hn_ref, h_ref, u_ref, v_ref, cut_ref, g1_ref, r2_ref, g2_ref, o_ref, acc_scr, *, te):
    e = pl.program_id(1)

    @pl.when(e == 0)
    def _():
        acc_scr[...] = jnp.zeros(acc_scr.shape, F32)

    hn = hn_ref[...]
    tt = hn.shape[1]

    def rows_bf16(ref, h, ii):
        return jnp.broadcast_to(ref[h, ii:ii + 1, :], (N_KEYS, tt))

    chunk = N_KEYS
    weights = []
    for c in range(te // chunk):
        a = _dot(u_ref[c * chunk:(c + 1) * chunk, :], hn)
        for k in range(chunk // N_KEYS):
            ii = c * (chunk // N_KEYS) + k
            gate = jnp.zeros((N_KEYS, tt), BF16)
            for h in range(P_HEADS):
                chosen = r2_ref[h] < rows_bf16(cut_ref, h, ii)
                gate = gate + jnp.where(chosen, g2_ref[h], jnp.zeros_like(gate)) * rows_bf16(g1_ref, h, ii)
            weights.append(gate * _gelu_tanh(a[k * N_KEYS:(k + 1) * N_KEYS, :]).astype(BF16))
    acc_scr[...] += _dot(v_ref[...], jnp.concatenate(weights, axis=0))

    @pl.when(e == pl.num_programs(1) - 1)
    def _():
        o_ref[...] = (h_ref[...] + acc_scr[...]).T


def _peer_ffn(hn_t, h_t, u_bf, v_t_bf, cut_k, g1_k, r2, g2, tt, te):
    d, t = hn_t.shape
    n_exp = u_bf.shape[0]
    kpe = te // N_KEYS
    return pl.pallas_call(
        functools.partial(_peer_ffn_kernel, te=te),
        grid=(t // tt, n_exp // te),
        in_specs=[pl.BlockSpec((d, tt), lambda i, e: (0, i)),
                  pl.BlockSpec((d, tt), lambda i, e: (0, i)),
                  pl.BlockSpec((te, d), lambda i, e: (e, 0)),
                  pl.BlockSpec((d, te), lambda i, e: (0, e)),
                  pl.BlockSpec((P_HEADS, kpe, tt), lambda i, e: (0, e, i)),
                  pl.BlockSpec((P_HEADS, kpe, tt), lambda i, e: (0, e, i)),
                  pl.BlockSpec((P_HEADS, N_KEYS, tt), lambda i, e: (0, 0, i)),
                  pl.BlockSpec((P_HEADS, N_KEYS, tt), lambda i, e: (0, 0, i))],
        out_specs=pl.BlockSpec((tt, d), lambda i, e: (i, 0)),
        out_shape=jax.ShapeDtypeStruct((t, d), F32),
        scratch_shapes=[pltpu.VMEM((d, tt), F32)],
        compiler_params=_params(2),
        name="peer_ffn",
    )(hn_t, h_t, u_bf, v_t_bf, cut_k, g1_k, r2, g2)


def _rope_tables_t(pos_flat, rot_dim):
    inv_freq = ROPE_THETA ** (-jnp.arange(0, rot_dim, 2, dtype=F32) / rot_dim)
    ang = pos_flat.astype(F32)[None, :] * inv_freq[:, None]
    return jnp.cos(ang), jnp.sin(ang)


def _expand_cmp_w1(w1):
    w = w1.reshape(CMP_LEN, A_DIM, CMP_HIDDEN)
    out = []
    for part in (w[:CMP_STRIDE], w[CMP_STRIDE:]):
        z = jnp.zeros_like(part)
        both = jnp.stack([jnp.concatenate([part, z], axis=1), jnp.concatenate([z, part], axis=1)])
        out.append(both.reshape(A_GROUPS, CMP_STRIDE * A_GROUPS * A_DIM, CMP_HIDDEN).transpose(0, 2, 1).astype(BF16))
    return out


TOKEN_TILE = 512
NSA_Q_TILE = 256
NSA_SEL_K_TILE = 512
MLA_Q_TILE = 1024
MLA_K_TILE = 512
PEER_ROUTE_TILE = 256
PEER_EXPERT_TILE = 2048


def _col(v):
    return v.reshape(-1, 1).astype(F32)


def _mixers(x, positions, norm1_gain, w_in, nsa_q_gain, nsa_kc_gain, nsa_ks_gain, nsa_kw_gain,
            cmp_pos, cmp_k_w1, cmp_k_w2, cmp_v_w1, cmp_v_w2,
            mla_q_lora_gain, mla_w_uq, mla_kv_lora_gain, mla_w_ukv, mla_q_gain, mla_k_gain):
    batch, seq, d = x.shape
    t = batch * seq
    tt = TOKEN_TILE
    tq_nsa = NSA_Q_TILE
    tk_sel = NSA_SEL_K_TILE
    tq_mla, tk_mla = MLA_Q_TILE, MLA_K_TILE
    assert tk_sel == tt and tk_mla == tt
    assert d == D_MODEL and seq % 512 == 0 and seq // SLC_LEN >= SLC_TOPK and WINDOW % tq_nsa == 0
    col = _col
    x2 = x.reshape(t, d)
    pos = positions.reshape(t)

    w_in_t = w_in.T
    gate_lo = sum((512, 128, 128, 128, 128, 128, 128))
    gate_hi = gate_lo + 3 * A_HEADS
    w_in_t = jnp.concatenate([w_in_t[:gate_lo], w_in_t[gate_hi:], w_in_t[gate_lo:gate_hi],
                              jnp.zeros((PROJ_ROWS - w_in_t.shape[0], d), F32)], axis=0).astype(BF16)
    cos_a, sin_a = _rope_tables_t(pos, A_ROPE)
    cos_b, sin_b = _rope_tables_t(pos, B_ROPE)
    q_t, kc_tm, vc_tm, k_slc, v_slc_t, k_win, v_win_t, gates_t, q_m, k_m, v_m_t, kn_a, kn_b = _in_proj(
        x2, norm1_gain.reshape(1, d), w_in_t, cos_a, sin_a, col(nsa_q_gain), col(nsa_ks_gain), col(nsa_kw_gain),
        cos_b, sin_b, col(mla_q_lora_gain), col(mla_kv_lora_gain), col(mla_q_gain), col(mla_k_gain),
        mla_w_uq.T.astype(BF16), mla_w_ukv.T.astype(BF16), tt, seq, tq_nsa)

    nc = seq // CMP_STRIDE
    chunk_w = CMP_STRIDE * A_GROUPS * A_DIM
    w1ka, w1kb = _expand_cmp_w1(cmp_k_w1)
    w1va, w1vb = _expand_cmp_w1(cmp_v_w1)
    pos_rows = lambda p: jnp.broadcast_to(p[:, None, :], (CMP_STRIDE, A_GROUPS, A_DIM)).reshape(1, chunk_w)
    cmp_end = jnp.minimum(jnp.arange(nc) * CMP_STRIDE + CMP_LEN - 1, seq - 1)
    cos_c, sin_c = _rope_tables_t(positions[:, cmp_end].reshape(-1), A_ROPE)
    to_b = lambda a: a.reshape(A_ROPE // 2, batch, nc).transpose(1, 0, 2)
    kcmp, vcmp_t = _compress(kc_tm.reshape(batch, nc, chunk_w), vc_tm.reshape(batch, nc, chunk_w),
                             w1ka, w1kb, w1va, w1vb, pos_rows(cmp_pos[:CMP_STRIDE]), pos_rows(cmp_pos[CMP_STRIDE:]),
                             cmp_k_w2.T.astype(BF16), cmp_v_w2.T.astype(BF16), col(nsa_kc_gain), to_b(cos_c), to_b(sin_c))

    n_cmp = (seq - CMP_LEN) // CMP_STRIDE + 1
    nb = seq // SLC_LEN
    c_start = np.arange(nc)[None, :] * CMP_STRIDE
    s_start = np.arange(nb)[:, None] * SLC_LEN
    ov = (c_start < s_start + SLC_LEN) & (c_start + CMP_LEN - 1 >= s_start) & (np.arange(nc)[None, :] < n_cmp)
    ov_t = jnp.asarray(ov.astype(np.float32)).astype(BF16)

    oc_t, selb = _nsa_cmp(q_t, kcmp, vcmp_t, ov_t, batch, seq, tq_nsa)
    norm_max = lambda kn: jnp.sqrt(jnp.max(kn.reshape(KEY_NORM_ROWS, batch, seq), axis=2)).reshape(-1)
    os_t, ow_t = _nsa_sel(norm_max(kn_a), q_t, k_slc, v_slc_t, selb, k_win, v_win_t, batch, seq, tq_nsa, tk_sel)

    ob_t = _mla_attn(norm_max(kn_b), q_m, k_m, v_m_t, batch, seq, tq_mla, tk_mla)
    return oc_t, os_t, ow_t, gates_t, ob_t


def _peer(hn_t, h_t, peer_w_q, peer_sub_keys, peer_u, peer_v):
    cut, g1, r2, g2 = _peer_route(hn_t, peer_w_q.T.astype(BF16), peer_sub_keys.astype(BF16), PEER_ROUTE_TILE)
    return _peer_ffn(hn_t, h_t, peer_u.astype(BF16), peer_v.T.astype(BF16),
                     cut, g1, r2, g2, TOKEN_TILE, PEER_EXPERT_TILE)


def _layer(x, positions, norm1_gain, w_in, nsa_q_gain, nsa_kc_gain, nsa_ks_gain, nsa_kw_gain,
           cmp_pos, cmp_k_w1, cmp_k_w2, cmp_v_w1, cmp_v_w2,
           mla_q_lora_gain, mla_w_uq, mla_kv_lora_gain, mla_w_ukv, mla_q_gain, mla_k_gain,
           out_gain_a, out_gain_b, w_out, norm2_gain, peer_w_q, peer_sub_keys, peer_u, peer_v):
    batch, seq, d = x.shape
    oc_t, os_t, ow_t, gates_t, ob_t = _mixers(
        x, positions, norm1_gain, w_in, nsa_q_gain, nsa_kc_gain, nsa_ks_gain, nsa_kw_gain,
        cmp_pos, cmp_k_w1, cmp_k_w2, cmp_v_w1, cmp_v_w2,
        mla_q_lora_gain, mla_w_uq, mla_kv_lora_gain, mla_w_ukv, mla_q_gain, mla_k_gain)
    h_t, hn_t = _out_proj(oc_t, os_t, ow_t, gates_t, ob_t, x.reshape(batch * seq, d), _col(out_gain_a), _col(out_gain_b),
                          w_out.T.astype(BF16), _col(norm2_gain), TOKEN_TILE)
    return _peer(hn_t, h_t, peer_w_q, peer_sub_keys, peer_u, peer_v).reshape(batch, seq, d)


def kernel(x, positions, norm1_gain, w_in, nsa_q_gain, nsa_kc_gain, nsa_ks_gain, nsa_kw_gain, cmp_pos, cmp_k_w1, cmp_k_w2, cmp_v_w1, cmp_v_w2, mla_q_lora_gain, mla_w_uq, mla_kv_lora_gain, mla_w_ukv, mla_q_gain, mla_k_gain, out_gain_a, out_gain_b, w_out, norm2_gain, peer_w_q, peer_sub_keys, peer_u, peer_v):
    h = x
    for l in range(norm1_gain.shape[0]):
        h = _layer(h, positions, norm1_gain[l], w_in[l], nsa_q_gain[l], nsa_kc_gain[l], nsa_ks_gain[l], nsa_kw_gain[l],
                   cmp_pos[l], cmp_k_w1[l], cmp_k_w2[l], cmp_v_w1[l], cmp_v_w2[l],
                   mla_q_lora_gain[l], mla_w_uq[l], mla_kv_lora_gain[l], mla_w_ukv[l], mla_q_gain[l], mla_k_gain[l],
                   out_gain_a[l], out_gain_b[l], w_out[l], norm2_gain[l], peer_w_q[l], peer_sub_keys[l],
                   peer_u[l], peer_v[l])
    return h
```

```python
import functools

import jax
import jax.numpy as jnp
import numpy as np
from jax import lax
from jax.experimental import pallas as pl
from jax.experimental.pallas import tpu as pltpu

F32, BF16 = jnp.float32, jnp.bfloat16
EPS = 1e-6
NEG = -1e30
FORCE = 1e9
ROPE_THETA = 500000.0
LOG2E = 1.4426950408889634

D_MODEL = 1024
A_HEADS, A_GROUPS, A_DIM = 8, 2, 64
A_REP = A_HEADS // A_GROUPS
A_ROPE = A_DIM // 4
CMP_LEN, CMP_STRIDE, CMP_HIDDEN = 32, 16, 256
SLC_LEN, SLC_TOPK, WINDOW = 64, 16, 512
B_HEADS, Q_LORA, KV_LORA, B_NOPE, B_ROPE, B_V = 8, 256, 128, 64, 32, 64
B_QK = B_NOPE + B_ROPE
P_HEADS, N_KEYS, P_KEY_DIM, P_TOPK = 8, 128, 256, 16
N_EXPERTS = N_KEYS * N_KEYS

ROW_Q, ROW_KC, ROW_VC, ROW_KS, ROW_VS, ROW_KW, ROW_VW = 0, 512, 640, 768, 896, 1024, 1152
ROW_CQ, ROW_CKV, ROW_KR, ROW_GATE, PROJ_ROWS = 1280, 1536, 1664, 1696, 1728
GATE_ROWS = 32
KEY_NORM_ROWS = 8

VMEM_LIMIT = 56 * 1024 * 1024
NT_DIMS = (((1,), (1,)), ((), ()))


def _params(n_axes):
    return pltpu.CompilerParams(dimension_semantics=("arbitrary",) * n_axes, vmem_limit_bytes=VMEM_LIMIT)


def _dot(a, b):
    return jnp.dot(a, b, preferred_element_type=F32)


def _dot_nt(a, b):
    return lax.dot_general(a, b, NT_DIMS, preferred_element_type=F32)


def _row_sumsq(x):
    sq = x * x
    hi = sq.astype(BF16)
    lo = (sq - hi.astype(F32)).astype(BF16)
    ones = jnp.ones((8, x.shape[1]), BF16)
    return (_dot_nt(ones, hi) + _dot_nt(ones, lo))[0:1, :]


def _rms_rows(x):
    ss = jnp.sum(x * x, axis=0, keepdims=True)
    return x * lax.rsqrt(ss * (1.0 / x.shape[0]) + EPS)


def _rope_rows(y, cos, sin, off, half):
    x1, x2 = y[off:off + half], y[off + half:off + 2 * half]
    parts = [y[:off]] if off else []
    parts += [x1 * cos - x2 * sin, x2 * cos + x1 * sin]
    if off + 2 * half < y.shape[0]:
        parts.append(y[off + 2 * half:])
    return jnp.concatenate(parts, axis=0)


def _gelu_tanh(x):
    c = 0.7978845608028654
    half = 0.5 * x
    return half + half * jnp.tanh(x * (c + (c * 0.044715) * (x * x)))


def _in_proj_kernel(x_ref, g_ref, w_ref, cos_a_ref, sin_a_ref, qg_ref, ksg_ref, kwg_ref,
                    cos_b_ref, sin_b_ref, qlg_ref, kvlg_ref, mqg_ref, mkg_ref, wuq_ref, wukv_ref, *out_refs, seq, tw):
    x = x_ref[...]
    xg = (x * g_ref[...]).astype(BF16)
    p = _dot_nt(w_ref[...], xg)
    p = p * lax.rsqrt(_row_sumsq(x) * (1.0 / x.shape[1]) + EPS)
    gw = A_GROUPS * A_DIM
    rows = lambda start, n: p[start:start + n]
    _nsa_prep_kernel(rows(ROW_Q, A_HEADS * A_DIM), rows(ROW_KC, gw), rows(ROW_VC, gw), rows(ROW_KS, gw), rows(ROW_VS, gw),
                     rows(ROW_KW, gw), rows(ROW_VW, gw), rows(ROW_GATE, GATE_ROWS), cos_a_ref, sin_a_ref,
                     qg_ref, ksg_ref, kwg_ref, *out_refs[:8], out_refs[11], seq=seq, tw=tw)
    _mla_prep_kernel(rows(ROW_CQ, Q_LORA), rows(ROW_CKV, KV_LORA), rows(ROW_KR, B_ROPE), cos_b_ref, sin_b_ref,
                     qlg_ref, kvlg_ref, mqg_ref, mkg_ref, wuq_ref, wukv_ref, *out_refs[8:11], out_refs[12])


def _in_proj(x2, gain, w_t, cos_a, sin_a, q_gain, ks_gain, kw_gain, cos_b, sin_b, q_lora_gain, kv_lora_gain,
             mq_gain, mk_gain, wuq_t, wukv_t, tt, seq, tw):
    t, d = x2.shape
    gw = A_GROUPS * A_DIM
    kw = 128 + seq // SLC_LEN
    va, vb = A_DIM + ONES_ROWS, B_V + ONES_ROWS
    full = lambda a: pl.BlockSpec(a.shape, lambda i: (0,) * a.ndim)
    lanes = lambda n: pl.BlockSpec((n, tt), lambda i: (0, i))
    return pl.pallas_call(
        functools.partial(_in_proj_kernel, seq=seq, tw=tw),
        grid=(t // tt,),
        in_specs=[pl.BlockSpec((tt, d), lambda i: (i, 0)), full(gain), full(w_t),
                  lanes(A_ROPE // 2), lanes(A_ROPE // 2), full(q_gain), full(ks_gain), full(kw_gain),
                  lanes(B_ROPE // 2), lanes(B_ROPE // 2), full(q_lora_gain), full(kv_lora_gain), full(mq_gain),
                  full(mk_gain), full(wuq_t), full(wukv_t)],
        out_specs=[lanes(A_HEADS * A_DIM),
                   pl.BlockSpec((tt, gw), lambda i: (i, 0)),
                   pl.BlockSpec((tt, gw), lambda i: (i, 0)),
                   pl.BlockSpec((A_GROUPS, tt, kw), lambda i: (0, i, 0)),
                   pl.BlockSpec((A_GROUPS, 1, va, tt), lambda i: (0, i, 0, 0)),
                   pl.BlockSpec((A_GROUPS, tt, 128), lambda i: (0, i, 0)),
                   pl.BlockSpec((A_GROUPS, tt // tw, va, tw), lambda i: (0, i, 0, 0)),
                   lanes(GATE_ROWS),
                   pl.BlockSpec((B_HEADS, 128, tt), lambda i: (0, 0, i)),
                   pl.BlockSpec((B_HEADS, tt, 128), lambda i: (0, i, 0)),
                   pl.BlockSpec((B_HEADS, 1, vb, tt), lambda i: (0, i, 0, 0)),
                   lanes(KEY_NORM_ROWS), lanes(KEY_NORM_ROWS)],
        out_shape=[jax.ShapeDtypeStruct((A_HEADS * A_DIM, t), BF16),
                   jax.ShapeDtypeStruct((t, gw), F32),
                   jax.ShapeDtypeStruct((t, gw), F32),
                   jax.ShapeDtypeStruct((A_GROUPS, t, kw), BF16),
                   jax.ShapeDtypeStruct((A_GROUPS, t // tt, va, tt), BF16),
                   jax.ShapeDtypeStruct((A_GROUPS, t, 128), BF16),
                   jax.ShapeDtypeStruct((A_GROUPS, t // tw, va, tw), BF16),
                   jax.ShapeDtypeStruct((GATE_ROWS, t), F32),
                   jax.ShapeDtypeStruct((B_HEADS, 128, t), BF16),
                   jax.ShapeDtypeStruct((B_HEADS, t, 128), BF16),
                   jax.ShapeDtypeStruct((B_HEADS, t // tt, vb, tt), BF16),
                   jax.ShapeDtypeStruct((KEY_NORM_ROWS, t), F32),
                   jax.ShapeDtypeStruct((KEY_NORM_ROWS, t), F32)],
        compiler_params=_params(1),
        name="in_proj",
    )(x2, gain, w_t, cos_a, sin_a, q_gain, ks_gain, kw_gain, cos_b, sin_b, q_lora_gain, kv_lora_gain,
      mq_gain, mk_gain, wuq_t, wukv_t)


def _nsa_prep_kernel(q_ref, kc_ref, vc_ref, ks_ref, vs_ref, kw_ref, vw_ref, gt_ref, cos_ref, sin_ref,
                     qg_ref, ksg_ref, kwg_ref,
                     qo_ref, kco_ref, vco_ref, kso_ref, vso_ref, kwo_ref, vwo_ref, gto_ref, kn_ref, *, seq, tw):
    cos, sin = cos_ref[...], sin_ref[...]
    tt = cos.shape[1]
    nb = seq // SLC_LEN
    for h in range(A_HEADS):
        y = _rms_rows(q_ref[h * A_DIM:(h + 1) * A_DIM, :]) * qg_ref[...]
        y = _rope_rows(y, cos, sin, 0, A_ROPE // 2) * (A_DIM ** -0.5 * LOG2E)
        qo_ref[h * A_DIM:(h + 1) * A_DIM, :] = y.astype(BF16)
    kco_ref[...] = kc_ref[...].T
    vco_ref[...] = vc_ref[...].T
    zeros = jnp.zeros((A_DIM, tt), F32)
    tok = pl.program_id(0) * tt + lax.broadcasted_iota(jnp.int32, (tt, nb), 0)
    block_hot = jnp.where(lax.broadcasted_iota(jnp.int32, (tt, nb), 1) == (tok % seq) // SLC_LEN, 1.0, 0.0)
    kn_ref[...] = jnp.zeros(kn_ref.shape, F32)
    for g in range(A_GROUPS):
        for branch, (src, gain, dst) in enumerate(((ks_ref, ksg_ref, kso_ref), (kw_ref, kwg_ref, kwo_ref))):
            y = _rms_rows(src[g * A_DIM:(g + 1) * A_DIM, :]) * gain[...]
            y = _rope_rows(y, cos, sin, 0, A_ROPE // 2)
            row = branch * A_GROUPS + g
            kn_ref[row:row + 1, :] = jnp.sum(y * y, axis=0, keepdims=True)
            k_tm = jnp.concatenate([y, zeros], axis=0).T
            if dst is kso_ref:
                k_tm = jnp.concatenate([k_tm, block_hot], axis=1)
            dst[g] = k_tm.astype(BF16)
        vso_ref[g, 0] = _with_ones(vs_ref[g * A_DIM:(g + 1) * A_DIM, :])
        v_win = _with_ones(vw_ref[g * A_DIM:(g + 1) * A_DIM, :])
        for c in range(tt // tw):
            vwo_ref[g, c] = v_win[:, c * tw:(c + 1) * tw]
    gto_ref[...] = 1.0 / (1.0 + jnp.exp(-gt_ref[...]))


def _compress_kernel(kc_ref, vc_ref, w1ka_ref, w1kb_ref, w1va_ref, w1vb_ref, plo_ref, phi_ref,
                     w2k_ref, w2v_ref, kg_ref, cos_ref, sin_ref, ko_ref, vo_ref):
    nc = kc_ref.shape[1]
    zeros = jnp.zeros((A_DIM, nc), F32)
    for src, w1a, w1b, w2, is_k in ((kc_ref, w1ka_ref, w1kb_ref, w2k_ref, True),
                                    (vc_ref, w1va_ref, w1vb_ref, w2v_ref, False)):
        x = src[0]
        xlo = (x + plo_ref[...]).astype(BF16)
        xhi = (x + phi_ref[...]).astype(BF16)
        for g in range(A_GROUPS):
            first = _dot_nt(w1a[g], xlo)
            second = _dot_nt(w1b[g], xhi)
            hid = _gelu_tanh(first + pltpu.roll(second, nc - 1, axis=1)).astype(BF16)
            c = _dot(w2[...], hid)
            if is_k:
                y = _rope_rows(_rms_rows(c) * kg_ref[...], cos_ref[0], sin_ref[0], 0, A_ROPE // 2)
                ko_ref[0, g] = jnp.concatenate([y, zeros], axis=0).T.astype(BF16)
            else:
                vo_ref[0, g] = _with_ones(c)


def _compress(kc_chunks, vc_chunks, w1ka, w1kb, w1va, w1vb, plo, phi, w2k_t, w2v_t, kc_gain, cos_c, sin_c):
    b, nc, cw = kc_chunks.shape
    full = lambda a: pl.BlockSpec(a.shape, lambda i: (0,) * a.ndim)
    return pl.pallas_call(
        _compress_kernel,
        grid=(b,),
        in_specs=[pl.BlockSpec((1, nc, cw), lambda i: (i, 0, 0)), pl.BlockSpec((1, nc, cw), lambda i: (i, 0, 0)),
                  full(w1ka), full(w1kb), full(w1va), full(w1vb), full(plo), full(phi), full(w2k_t), full(w2v_t),
                  full(kc_gain),
                  pl.BlockSpec((1, A_ROPE // 2, nc), lambda i: (i, 0, 0)),
                  pl.BlockSpec((1, A_ROPE // 2, nc), lambda i: (i, 0, 0))],
        out_specs=[pl.BlockSpec((1, A_GROUPS, nc, 128), lambda i: (i, 0, 0, 0)),
                   pl.BlockSpec((1, A_GROUPS, A_DIM + ONES_ROWS, nc), lambda i: (i, 0, 0, 0))],
        out_shape=[jax.ShapeDtypeStruct((b, A_GROUPS, nc, 128), BF16),
                   jax.ShapeDtypeStruct((b, A_GROUPS, A_DIM + ONES_ROWS, nc), BF16)],
        compiler_params=_params(1),
        name="nsa_compress",
    )(kc_chunks, vc_chunks, w1ka, w1kb, w1va, w1vb, plo, phi, w2k_t, w2v_t, kc_gain, cos_c, sin_c)


def _stack_heads(q, tq):
    qs = jnp.concatenate([q[r * A_DIM:(r + 1) * A_DIM, :] for r in range(A_REP)], axis=1)
    return jnp.concatenate([qs, jnp.zeros_like(qs)], axis=0)


def _nsa_cmp_kernel(q_ref, k_ref, v_ref, ov_ref, o_ref, sb_ref, *, tq):
    i = pl.program_id(2)
    n = A_REP * tq
    nc = k_ref.shape[2]
    nb = ov_ref.shape[0]
    qp = _stack_heads(q_ref[...], tq)
    s = _dot(k_ref[0, 0], qp)
    cmp_end = lax.broadcasted_iota(jnp.int32, (nc, n), 0) * CMP_STRIDE + (CMP_LEN - 1)
    tok = i * tq + (lax.broadcasted_iota(jnp.int32, (nc, n), 1) & (tq - 1))
    s = jnp.where(cmp_end <= tok, s, NEG)
    m = jnp.max(s, axis=0, keepdims=True)
    p = jnp.exp2(s - jnp.where(m > 0.5 * NEG, m, 0.0)).astype(BF16)
    ocl = _dot(v_ref[0, 0], p)
    l = ocl[A_DIM:A_DIM + 1]
    inv = jnp.where(l > 0.0, 1.0 / l, 0.0)
    oc = ocl[:A_DIM] * inv
    for r in range(A_REP):
        o_ref[r * A_DIM:(r + 1) * A_DIM, :] = oc[:, r * tq:(r + 1) * tq]
    imp4 = _dot(ov_ref[...], p) * inv
    imp = imp4[:, 0:tq]
    for r in range(1, A_REP):
        imp = imp + imp4[:, r * tq:(r + 1) * tq]

    blk = lax.broadcasted_iota(jnp.int32, (nb, tq), 0)
    t = i * tq + lax.broadcasted_iota(jnp.int32, (nb, tq), 1)
    forced = (blk == t // SLC_LEN) | (blk == 0)
    v0 = jnp.where(forced, FORCE, jnp.where(blk * SLC_LEN <= t, imp, NEG))
    topk = min(SLC_TOPK, nb)

    v = jnp.where(blk == t // SLC_LEN, 2.0 * FORCE, v0)
    for _ in range(topk):
        v = jnp.where(v == jnp.max(v, axis=0, keepdims=True), -jnp.inf, v)
    taken = v == -jnp.inf
    sb_ref[0] = jnp.where(taken, 0.0, NEG)
    n_taken = jnp.sum(jnp.where(taken, 1.0, 0.0), axis=0, keepdims=True)

    @pl.when(jnp.max(jnp.abs(n_taken - float(topk))) > 0.0)
    def _():
        blk_f = blk.astype(F32)
        w = v0
        sel = jnp.zeros((nb, tq), F32)
        for _ in range(topk):
            mx = jnp.max(w, axis=0, keepdims=True)
            first = jnp.min(jnp.where(w == mx, blk_f, float(nb)), axis=0, keepdims=True)
            hit = blk_f == first
            sel = jnp.where(hit, 1.0, sel)
            w = jnp.where(hit, -jnp.inf, w)
        sb_ref[0] = jnp.where(sel > 0.0, 0.0, NEG)


def _nsa_cmp(q_t, kcmp, vcmp_t, ov_t, batch, seq, tq):
    nq = seq // tq
    nc = kcmp.shape[2]
    nb = ov_t.shape[0]
    t = q_t.shape[1]
    gr = A_REP * A_DIM
    return pl.pallas_call(
        functools.partial(_nsa_cmp_kernel, tq=tq),
        grid=(batch, A_GROUPS, nq),
        in_specs=[pl.BlockSpec((gr, tq), lambda b, g, i: (g, b * nq + i)),
                  pl.BlockSpec((1, 1, nc, 128), lambda b, g, i: (b, g, 0, 0)),
                  pl.BlockSpec((1, 1, A_DIM + ONES_ROWS, nc), lambda b, g, i: (b, g, 0, 0)),
                  pl.BlockSpec((nb, nc), lambda b, g, i: (0, 0))],
        out_specs=[pl.BlockSpec((gr, tq), lambda b, g, i: (g, b * nq + i)),
                   pl.BlockSpec((1, nb, tq), lambda b, g, i: (g, 0, b * nq + i))],
        out_shape=[jax.ShapeDtypeStruct((A_HEADS * A_DIM, t), F32),
                   jax.ShapeDtypeStruct((A_GROUPS, nb, t), F32)],
        compiler_params=_params(3),
        name="nsa_cmp",
    )(q_t, kcmp, vcmp_t, ov_t)


ONES_ROWS = 16
FLASH_UNROLL = 4


def _with_ones(v):
    return jnp.concatenate([v, jnp.ones((ONES_ROWS, v.shape[1]), F32)], axis=0).astype(BF16)


def _flash_update(s, v_t, m_scr, acc_scr):
    m_prev = m_scr[...]
    m_new = jnp.maximum(m_prev, jnp.max(s, axis=0, keepdims=True))
    alpha = jnp.exp2(m_prev - m_new)
    p = jnp.exp2(s - m_new)
    acc_scr[...] = alpha * acc_scr[...] + _dot(v_t, p.astype(BF16))
    m_scr[...] = m_new


def _flash_update_bounded(s, v_t, m_scr, acc_scr):
    acc_scr[...] += _dot(v_t, jnp.exp2(s - m_scr[...]).astype(BF16))


def _flash_result(acc_scr, dv):
    acc = acc_scr[...]
    return acc[:dv] * (1.0 / acc[dv:dv + 1])


BOUND_LIMIT = 56.0


def _logit_bound(q, k_norm_max):
    qf = q.astype(F32)
    return jnp.sqrt(jnp.sum(qf * qf, axis=0, keepdims=True)) * (k_norm_max * 1.02)


def _flash_bounded_or_online(bound, run, m_scr, acc_scr):
    acc_scr[...] = jnp.zeros(acc_scr.shape, F32)
    small = jnp.max(bound) <= BOUND_LIMIT

    @pl.when(small)
    def _():
        m_scr[...] = bound
        run(_flash_update_bounded)

    @pl.when(jnp.logical_not(small))
    def _():
        m_scr[...] = jnp.full(m_scr.shape, -jnp.inf, F32)
        run(_flash_update)


def _flash_causal(scores, values, mask, n_full, n_masked, s_scr, m_scr, acc_scr, update):
    unroll = s_scr.shape[0]
    s_scr[0] = scores(0)

    def trip(t, carry):
        j = unroll * t
        for u in range(unroll):
            s_scr[(u + 1) % unroll] = scores(j + u + 1)
            update(s_scr[u], values(j + u), m_scr, acc_scr)
        return carry

    lax.fori_loop(0, n_full // unroll, trip, 0)
    first = (n_full // unroll) * unroll
    for rest in range(unroll):

        @pl.when(n_full - first == rest)
        def _(rest=rest):
            for u in range(rest + n_masked):
                if u + 1 < rest + n_masked:
                    s_scr[(u + 1) % unroll] = scores(first + u + 1)
                s = s_scr[u % unroll]
                update(s if u < rest else mask(s, first + u), values(first + u), m_scr, acc_scr)


def _nsa_sel_kernel(kmax_ref, q_ref, k_ref, v_ref, sb_ref, kw_ref, vw_ref, o_ref, ow_ref, qa_scr, s_scr, m_scr, acc_scr,
                    *, tq, tk):
    b, g, i = pl.program_id(0), pl.program_id(1), pl.program_id(2)
    n = A_REP * tq
    q = q_ref[...]
    qs = jnp.concatenate([q[r * A_DIM:(r + 1) * A_DIM, :] for r in range(A_REP)], axis=1)
    sb = sb_ref[0].astype(BF16)
    qa_scr[...] = jnp.concatenate([qs, jnp.zeros_like(qs), jnp.concatenate([sb] * A_REP, axis=1)], axis=0)
    n_batch = pl.num_programs(0)

    def scores(j):
        return _dot(k_ref[0, pl.ds(pl.multiple_of(j * tk, tk), tk), :], qa_scr[...])

    def causal(s, j):
        kpos = j * tk + lax.broadcasted_iota(jnp.int32, (tk, n), 0)
        tok = i * tq + (lax.broadcasted_iota(jnp.int32, (tk, n), 1) & (tq - 1))
        return jnp.where(kpos <= tok, s, NEG)

    _flash_bounded_or_online(
        _logit_bound(qs, kmax_ref[g * n_batch + b]),
        lambda update: _flash_causal(scores, lambda j: v_ref[0, j], causal, (i * tq) // tk, max(1, tq // tk),
                                     s_scr, m_scr, acc_scr, update),
        m_scr, acc_scr)
    o = _flash_result(acc_scr, A_DIM)
    for r in range(A_REP):
        o_ref[r * A_DIM:(r + 1) * A_DIM, :] = o[:, r * tq:(r + 1) * tq]

    n_back = WINDOW // tq

    def window_tile(update, c):
        kt = i - n_back + c
        s = _dot(kw_ref[0, pl.ds(pl.multiple_of(kt * tq, tq), tq), :], qa_scr[0:128, :])
        if c in (0, n_back):
            kpos = kt * tq + lax.broadcasted_iota(jnp.int32, (tq, n), 0)
            tok = i * tq + (lax.broadcasted_iota(jnp.int32, (tq, n), 1) & (tq - 1))
            s = jnp.where(kpos > tok - WINDOW if c == 0 else kpos <= tok, s, NEG)
        update(s, vw_ref[0, kt], m_scr, acc_scr)

    def window(update):
        @pl.when(i >= n_back)
        def _():
            for c in range(n_back + 1):
                window_tile(update, c)

        @pl.when(i < n_back)
        def _():
            for c in range(n_back + 1):
                pl.when(i - n_back + c >= 0)(functools.partial(window_tile, update, c))

    _flash_bounded_or_online(_logit_bound(qs, kmax_ref[(A_GROUPS + g) * n_batch + b]), window, m_scr, acc_scr)
    o = _flash_result(acc_scr, A_DIM)
    for r in range(A_REP):
        ow_ref[r * A_DIM:(r + 1) * A_DIM, :] = o[:, r * tq:(r + 1) * tq]


def _nsa_sel(k_norm_max, q_t, k_aug, v_slc_tiles, selb, k_win, v_win_tiles, batch, seq, tq, tk):
    nq, nk = seq // tq, seq // tk
    nb = selb.shape[1]
    t = q_t.shape[1]
    gr = A_REP * A_DIM
    n = A_REP * tq
    kw = k_aug.shape[2]
    out = jax.ShapeDtypeStruct((A_HEADS * A_DIM, t), F32)
    grid_spec = pltpu.PrefetchScalarGridSpec(
        num_scalar_prefetch=1,
        grid=(batch, A_GROUPS, nq),
        in_specs=[pl.BlockSpec((gr, tq), lambda b, g, i, km: (g, b * nq + i)),
                  pl.BlockSpec((1, seq, kw), lambda b, g, i, km: (g, b, 0)),
                  pl.BlockSpec((1, nk, A_DIM + ONES_ROWS, tk), lambda b, g, i, km: (g, b, 0, 0)),
                  pl.BlockSpec((1, nb, tq), lambda b, g, i, km: (g, 0, b * nq + i)),
                  pl.BlockSpec((1, seq, 128), lambda b, g, i, km: (g, b, 0)),
                  pl.BlockSpec((1, nq, A_DIM + ONES_ROWS, tq), lambda b, g, i, km: (g, b, 0, 0))],
        out_specs=[pl.BlockSpec((gr, tq), lambda b, g, i, km: (g, b * nq + i)),
                   pl.BlockSpec((gr, tq), lambda b, g, i, km: (g, b * nq + i))],
        scratch_shapes=[pltpu.VMEM((kw, n), BF16), pltpu.VMEM((FLASH_UNROLL, tk, n), F32),
                        pltpu.VMEM((1, n), F32), pltpu.VMEM((A_DIM + ONES_ROWS, n), F32)],
    )
    return pl.pallas_call(
        functools.partial(_nsa_sel_kernel, tq=tq, tk=tk),
        grid_spec=grid_spec,
        out_shape=[out, out],
        compiler_params=_params(3),
        name="nsa_sel",
    )(k_norm_max, q_t, k_aug, v_slc_tiles, selb, k_win, v_win_tiles)


def _mla_prep_kernel(cq_ref, ckv_ref, kr_ref, cos_ref, sin_ref, qlg_ref, kvlg_ref, qg_ref, kg_ref, wuq_ref, wukv_ref,
                     qo_ref, ko_ref, vo_ref, kn_ref):
    cos, sin = cos_ref[...], sin_ref[...]
    tt = cos.shape[1]
    q_all = _dot(wuq_ref[...], (_rms_rows(cq_ref[...]) * qlg_ref[...]).astype(BF16))
    kv_all = _dot(wukv_ref[...], (_rms_rows(ckv_ref[...]) * kvlg_ref[...]).astype(BF16))
    kr = kr_ref[...]
    pad = jnp.zeros((128 - B_QK, tt), F32)
    for h in range(B_HEADS):
        y = _rms_rows(q_all[h * B_QK:(h + 1) * B_QK]) * qg_ref[...]
        y = _rope_rows(y, cos, sin, B_NOPE, B_ROPE // 2) * (B_QK ** -0.5 * LOG2E)
        qo_ref[h] = jnp.concatenate([y, pad], axis=0).astype(BF16)
        base = h * (B_NOPE + B_V)
        k = jnp.concatenate([kv_all[base:base + B_NOPE], kr], axis=0)
        y = _rope_rows(_rms_rows(k) * kg_ref[...], cos, sin, B_NOPE, B_ROPE // 2)
        kn_ref[h:h + 1, :] = jnp.sum(y * y, axis=0, keepdims=True)
        ko_ref[h] = jnp.concatenate([y, pad], axis=0).T.astype(BF16)
        vo_ref[h, 0] = _with_ones(kv_all[base + B_NOPE:base + B_NOPE + B_V])


def _mla_attn_kernel(kmax_ref, q_ref, k_ref, v_ref, o_ref, s_scr, m_scr, acc_scr, *, tq, tk):
    b, h, i = pl.program_id(0), pl.program_id(1), pl.program_id(2)

    def scores(j):
        return _dot(k_ref[0, pl.ds(pl.multiple_of(j * tk, tk), tk), :], q_ref[0])

    def causal(s, j):
        kpos = j * tk + lax.broadcasted_iota(jnp.int32, (tk, tq), 0)
        tok = i * tq + lax.broadcasted_iota(jnp.int32, (tk, tq), 1)
        return jnp.where(kpos <= tok, s, NEG)

    _flash_bounded_or_online(
        _logit_bound(q_ref[0], kmax_ref[h * pl.num_programs(0) + b]),
        lambda update: _flash_causal(scores, lambda j: v_ref[0, j], causal, (i * tq) // tk, max(1, tq // tk),
                                     s_scr, m_scr, acc_scr, update),
        m_scr, acc_scr)
    o_ref[...] = _flash_result(acc_scr, B_V)


def _mla_attn(k_norm_max, q_m, k_m, v_m_tiles, batch, seq, tq, tk):
    nq, nk = seq // tq, seq // tk
    t = q_m.shape[2]
    grid_spec = pltpu.PrefetchScalarGridSpec(
        num_scalar_prefetch=1,
        grid=(batch, B_HEADS, nq),
        in_specs=[pl.BlockSpec((1, 128, tq), lambda b, h, i, km: (h, 0, b * nq + i)),
                  pl.BlockSpec((1, seq, 128), lambda b, h, i, km: (h, b, 0)),
                  pl.BlockSpec((1, nk, B_V + ONES_ROWS, tk), lambda b, h, i, km: (h, b, 0, 0))],
        out_specs=pl.BlockSpec((B_V, tq), lambda b, h, i, km: (h, b * nq + i)),
        scratch_shapes=[pltpu.VMEM((FLASH_UNROLL, tk, tq), F32),
                        pltpu.VMEM((1, tq), F32), pltpu.VMEM((B_V + ONES_ROWS, tq), F32)],
    )
    return pl.pallas_call(
        functools.partial(_mla_attn_kernel, tq=tq, tk=tk),
        grid_spec=grid_spec,
        out_shape=jax.ShapeDtypeStruct((B_HEADS * B_V, t), F32),
        compiler_params=_params(3),
        name="mla_attn",
    )(k_norm_max, q_m, k_m, v_m_tiles)


def _out_proj_kernel(oc_ref, os_ref, ow_ref, gt_ref, ob_ref, x_ref, ga_ref, gb_ref, w_ref, g2_ref, h_ref, hn_ref):
    heads = []
    for h in range(A_HEADS):
        rows = slice(h * A_DIM, (h + 1) * A_DIM)
        heads.append(gt_ref[3 * h:3 * h + 1, :] * oc_ref[rows, :] + gt_ref[3 * h + 1:3 * h + 2, :] * os_ref[rows, :]
                     + gt_ref[3 * h + 2:3 * h + 3, :] * ow_ref[rows, :])
    oa = _rms_rows(jnp.concatenate(heads, axis=0)) * ga_ref[...]
    ob = _rms_rows(ob_ref[...]) * gb_ref[...]
    cat = jnp.concatenate([oa, ob], axis=0).astype(BF16)
    hid = x_ref[...].T + _dot(w_ref[...], cat)
    h_ref[...] = hid
    hn_ref[...] = (_rms_rows(hid) * g2_ref[...]).astype(BF16)


def _out_proj(oc_t, os_t, ow_t, gates_t, ob_t, x2, gain_a, gain_b, w_out_t, gain2, tt):
    t, d = x2.shape
    aw = oc_t.shape[0]
    bw = ob_t.shape[0]
    tok = lambda rows: pl.BlockSpec((rows, tt), lambda i: (0, i))
    full = lambda a: pl.BlockSpec(a.shape, lambda i: (0,) * a.ndim)
    return pl.pallas_call(
        _out_proj_kernel,
        grid=(t // tt,),
        in_specs=[tok(aw), tok(aw), tok(aw), tok(GATE_ROWS), tok(bw), pl.BlockSpec((tt, d), lambda i: (i, 0)),
                  full(gain_a), full(gain_b), full(w_out_t), full(gain2)],
        out_specs=[tok(d), tok(d)],
        out_shape=[jax.ShapeDtypeStruct((d, t), F32), jax.ShapeDtypeStruct((d, t), BF16)],
        compiler_params=_params(1),
        name="out_proj",
    )(oc_t, os_t, ow_t, gates_t, ob_t, x2, gain_a, gain_b, w_out_t, gain2)


def _top_ranked(s):
    n, rest = s.shape[0], s.shape[1:]
    row = lax.broadcasted_iota(jnp.int32, s.shape, 0).astype(F32)
    slot = lax.broadcasted_iota(jnp.int32, (P_TOPK,) + rest, 0)

    def body(a, carry):
        v, rank, vals = carry
        mx = jnp.max(v, axis=0, keepdims=True)
        first = jnp.min(jnp.where(v == mx, row, float(n)), axis=0, keepdims=True)
        hit = row == first
        rank = jnp.where(hit, jnp.asarray(a, F32), rank)
        v = jnp.where(hit, -jnp.inf, v)
        vals = jnp.where(slot == a, mx, vals)
        return v, rank, vals

    _, rank, vals = lax.fori_loop(0, P_TOPK, body,
                                  (s, jnp.full(s.shape, float(P_TOPK), F32), jnp.zeros((P_TOPK,) + rest, F32)))
    return rank, vals


def _pair_counts(v1, v2):
    k = v1.shape[0]
    slot = lax.broadcasted_iota(jnp.int32, v1.shape, 0).astype(F32)
    top = v1[0:1] + v2[0:1]

    def body(_, carry):
        count, front, z = carry
        mx = jnp.max(front, axis=0, keepdims=True)
        a_star = jnp.min(jnp.where(front == mx, slot, float(k)), axis=0, keepdims=True)
        hit = slot == a_star
        count = count + jnp.where(hit, 1.0, 0.0)
        nxt = jnp.sum(jnp.where(hit, count, 0.0), axis=0, keepdims=True)
        v2_nxt = jnp.sum(jnp.where(slot == nxt, v2, 0.0), axis=0, keepdims=True)
        front = jnp.where(hit, jnp.where(nxt < float(k), v1 + v2_nxt, -jnp.inf), front)
        return count, front, z + jnp.exp(mx - top)

    count, _, z = lax.fori_loop(0, k, body, (jnp.zeros(v1.shape, F32), v1 + v2[0:1], jnp.zeros(top.shape, F32)))
    return count, z


REMOVED = -2.0 ** 126
LANES = 128


def _top_ranked_pair_fast(s1, s2):
    n, tp = s1.shape
    slot = lax.broadcasted_iota(jnp.int32, (P_TOPK, tp), 0)

    def body(a, carry):
        code = REMOVED * (1.0 + jnp.asarray(a, F32) * (1.0 / 32.0))
        out = []
        for key, vals in (carry[0:2], carry[2:4]):
            mx = jnp.max(key, axis=0, keepdims=True)
            key = jnp.where(key == mx, code, key)
            out += [key, jnp.where(slot == a, mx, vals)]
        return tuple(out)

    zeros = jnp.zeros((P_TOPK, tp), F32)
    k1, t1, k2, t2 = lax.fori_loop(0, P_TOPK, body, (s1, zeros, s2, zeros))
    res, bad = [], jnp.zeros((1, tp), F32)
    for key, vals in ((k1, t1), (k2, t2)):
        removed = key <= REMOVED
        rank = jnp.where(removed, (key * (1.0 / REMOVED) - 1.0) * 32.0, float(P_TOPK))
        n_removed = jnp.sum(jnp.where(removed, 1.0, 0.0), axis=0, keepdims=True)
        bad = jnp.maximum(bad, jnp.abs(n_removed - float(P_TOPK)))
        res += [rank, vals]
    return res[0], res[1], res[2], res[3], bad


def _peer_route_kernel(hn_ref, wq_ref, keys_ref, cut_ref, g1_ref, r2_ref, g2_ref, q_scr, s_scr, rank_scr, vals_scr):
    tp = hn_ref.shape[1]
    half = P_KEY_DIM // 2
    q_scr[...] = _dot(wq_ref[...], hn_ref[...])
    for h in range(P_HEADS):
        q = _rms_rows(q_scr[h * P_KEY_DIM:(h + 1) * P_KEY_DIM, :]).astype(BF16)
        s_scr[0, h] = _dot(keys_ref[h, 0], q[:half])
        s_scr[1, h] = _dot(keys_ref[h, 1], q[half:])

    def first_level(rank_pair, flagged):
        for h in range(P_HEADS):
            for lt in range(tp // LANES):
                lanes = slice(lt * LANES, (lt + 1) * LANES)
                rank1, vals1, rank2, vals2, bad = rank_pair(s_scr[0, h, :, lanes], s_scr[1, h, :, lanes])
                flagged = jnp.maximum(flagged, bad)
                rank_scr[0, h, :, lanes] = rank1
                rank_scr[1, h, :, lanes] = rank2
                for a in range(P_TOPK):
                    vals_scr[0, a, h:h + 1, lanes] = vals1[a:a + 1]
                    vals_scr[1, a, h:h + 1, lanes] = vals2[a:a + 1]
        return flagged

    flagged = first_level(_top_ranked_pair_fast, jnp.zeros((1, LANES), F32))

    @pl.when(jnp.max(flagged) > 0.0)
    def _():
        first_level(lambda s1, s2: (*_top_ranked(s1), *_top_ranked(s2), jnp.zeros((1, LANES), F32)), flagged)

    count, z = _pair_counts(vals_scr[0], vals_scr[1])
    for h in range(P_HEADS):
        rank1 = rank_scr[0, h]
        cut = jnp.zeros_like(rank1)
        for a in range(P_TOPK):
            cut = jnp.where(rank1 == float(a), count[a, h:h + 1, :], cut)
        cut_ref[h] = cut.astype(BF16)
        g1_ref[h] = jnp.exp(s_scr[0, h] - vals_scr[0, 0, h:h + 1, :]).astype(BF16)
        r2_ref[h] = rank_scr[1, h].astype(BF16)
        g2_ref[h] = (jnp.exp(s_scr[1, h] - vals_scr[1, 0, h:h + 1, :]) * (1.0 / z[0, h:h + 1, :])).astype(BF16)


def _peer_route(hn_t, wq_t, sub_keys, tp):
    d, t = hn_t.shape
    halfs = jax.ShapeDtypeStruct((P_HEADS, N_KEYS, t), BF16)
    ospec = pl.BlockSpec((P_HEADS, N_KEYS, tp), lambda i: (0, 0, i))
    return pl.pallas_call(
        _peer_route_kernel,
        grid=(t // tp,),
        in_specs=[pl.BlockSpec((d, tp), lambda i: (0, i)),
                  pl.BlockSpec(wq_t.shape, lambda i: (0, 0)),
                  pl.BlockSpec(sub_keys.shape, lambda i: (0, 0, 0, 0))],
        out_specs=[ospec, ospec, ospec, ospec],
        out_shape=[halfs, halfs, halfs, halfs],
        scratch_shapes=[pltpu.VMEM((P_HEADS * P_KEY_DIM, tp), F32), pltpu.VMEM((2, P_HEADS, N_KEYS, tp), F32),
                        pltpu.VMEM((2, P_HEADS, N_KEYS, tp), F32), pltpu.VMEM((2, P_TOPK, P_HEADS, tp), F32)],
        compiler_params=_params(1),
        name="peer_route",
    )(hn_t, wq_t, sub_keys)


def _peer_ffn_kernel(hn_ref, h_ref, u_ref, v_ref, cut_ref, g1_ref, r2_ref, g2_ref, o_ref, acc_scr, *, te):
    e = pl.program_id(1)

    @pl.when(e == 0)
    def _():
        acc_scr[...] = jnp.zeros(acc_scr.shape, F32)

    hn = hn_ref[...]
    tt = hn.shape[1]

    def rows_bf16(ref, h, ii):
        return jnp.broadcast_to(ref[h, ii:ii + 1, :], (N_KEYS, tt))

    chunk = N_KEYS
    weights = []
    for c in range(te // chunk):
        a = _dot(u_ref[c * chunk:(c + 1) * chunk, :], hn)
        for k in range(chunk // N_KEYS):
            ii = c * (chunk // N_KEYS) + k
            gate = jnp.zeros((N_KEYS, tt), BF16)
            for h in range(P_HEADS):
                chosen = r2_ref[h] < rows_bf16(cut_ref, h, ii)
                gate = gate + jnp.where(chosen, g2_ref[h], jnp.zeros_like(gate)) * rows_bf16(g1_ref, h, ii)
            weights.append(gate * _gelu_tanh(a[k * N_KEYS:(k + 1) * N_KEYS, :]).astype(BF16))
    acc_scr[...] += _dot(v_ref[...], jnp.concatenate(weights, axis=0))

    @pl.when(e == pl.num_programs(1) - 1)
    def _():
        o_ref[...] = (h_ref[...] + acc_scr[...]).T


def _peer_ffn(hn_t, h_t, u_bf, v_t_bf, cut_k, g1_k, r2, g2, tt, te):
    d, t = hn_t.shape
    n_exp = u_bf.shape[0]
    kpe = te // N_KEYS
    return pl.pallas_call(
        functools.partial(_peer_ffn_kernel, te=te),
        grid=(t // tt, n_exp // te),
        in_specs=[pl.BlockSpec((d, tt), lambda i, e: (0, i)),
                  pl.BlockSpec((d, tt), lambda i, e: (0, i)),
                  pl.BlockSpec((te, d), lambda i, e: (e, 0)),
                  pl.BlockSpec((d, te), lambda i, e: (0, e)),
                  pl.BlockSpec((P_HEADS, kpe, tt), lambda i, e: (0, e, i)),
                  pl.BlockSpec((P_HEADS, kpe, tt), lambda i, e: (0, e, i)),
                  pl.BlockSpec((P_HEADS, N_KEYS, tt), lambda i, e: (0, 0, i)),
                  pl.BlockSpec((P_HEADS, N_KEYS, tt), lambda i, e: (0, 0, i))],
        out_specs=pl.BlockSpec((tt, d), lambda i, e: (i, 0)),
        out_shape=jax.ShapeDtypeStruct((t, d), F32),
        scratch_shapes=[pltpu.VMEM((d, tt), F32)],
        compiler_params=_params(2),
        name="peer_ffn",
    )(hn_t, h_t, u_bf, v_t_bf, cut_k, g1_k, r2, g2)


def _rope_tables_t(pos_flat, rot_dim):
    inv_freq = ROPE_THETA ** (-jnp.arange(0, rot_dim, 2, dtype=F32) / rot_dim)
    ang = pos_flat.astype(F32)[None, :] * inv_freq[:, None]
    return jnp.cos(ang), jnp.sin(ang)


def _expand_cmp_w1(w1):
    w = w1.reshape(CMP_LEN, A_DIM, CMP_HIDDEN)
    out = []
    for part in (w[:CMP_STRIDE], w[CMP_STRIDE:]):
        z = jnp.zeros_like(part)
        both = jnp.stack([jnp.concatenate([part, z], axis=1), jnp.concatenate([z, part], axis=1)])
        out.append(both.reshape(A_GROUPS, CMP_STRIDE * A_GROUPS * A_DIM, CMP_HIDDEN).transpose(0, 2, 1).astype(BF16))
    return out


TOKEN_TILE = 512
NSA_Q_TILE = 256
NSA_SEL_K_TILE = 512
MLA_Q_TILE = 1024
MLA_K_TILE = 512
PEER_ROUTE_TILE = 256
PEER_EXPERT_TILE = 2048


def _col(v):
    return v.reshape(-1, 1).astype(F32)


def _mixers(x, positions, norm1_gain, w_in, nsa_q_gain, nsa_kc_gain, nsa_ks_gain, nsa_kw_gain,
            cmp_pos, cmp_k_w1, cmp_k_w2, cmp_v_w1, cmp_v_w2,
            mla_q_lora_gain, mla_w_uq, mla_kv_lora_gain, mla_w_ukv, mla_q_gain, mla_k_gain):
    batch, seq, d = x.shape
    t = batch * seq
    tt = TOKEN_TILE
    tq_nsa = NSA_Q_TILE
    tk_sel = NSA_SEL_K_TILE
    tq_mla, tk_mla = MLA_Q_TILE, MLA_K_TILE
    assert tk_sel == tt and tk_mla == tt
    assert d == D_MODEL and seq % 512 == 0 and seq // SLC_LEN >= SLC_TOPK and WINDOW % tq_nsa == 0
    col = _col
    x2 = x.reshape(t, d)
    pos = positions.reshape(t)

    w_in_t = w_in.T
    gate_lo = sum((512, 128, 128, 128, 128, 128, 128))
    gate_hi = gate_lo + 3 * A_HEADS
    w_in_t = jnp.concatenate([w_in_t[:gate_lo], w_in_t[gate_hi:], w_in_t[gate_lo:gate_hi],
                              jnp.zeros((PROJ_ROWS - w_in_t.shape[0], d), F32)], axis=0).astype(BF16)
    cos_a, sin_a = _rope_tables_t(pos, A_ROPE)
    cos_b, sin_b = _rope_tables_t(pos, B_ROPE)
    q_t, kc_tm, vc_tm, k_slc, v_slc_t, k_win, v_win_t, gates_t, q_m, k_m, v_m_t, kn_a, kn_b = _in_proj(
        x2, norm1_gain.reshape(1, d), w_in_t, cos_a, sin_a, col(nsa_q_gain), col(nsa_ks_gain), col(nsa_kw_gain),
        cos_b, sin_b, col(mla_q_lora_gain), col(mla_kv_lora_gain), col(mla_q_gain), col(mla_k_gain),
        mla_w_uq.T.astype(BF16), mla_w_ukv.T.astype(BF16), tt, seq, tq_nsa)

    nc = seq // CMP_STRIDE
    chunk_w = CMP_STRIDE * A_GROUPS * A_DIM
    w1ka, w1kb = _expand_cmp_w1(cmp_k_w1)
    w1va, w1vb = _expand_cmp_w1(cmp_v_w1)
    pos_rows = lambda p: jnp.broadcast_to(p[:, None, :], (CMP_STRIDE, A_GROUPS, A_DIM)).reshape(1, chunk_w)
    cmp_end = jnp.minimum(jnp.arange(nc) * CMP_STRIDE + CMP_LEN - 1, seq - 1)
    cos_c, sin_c = _rope_tables_t(positions[:, cmp_end].reshape(-1), A_ROPE)
    to_b = lambda a: a.reshape(A_ROPE // 2, batch, nc).transpose(1, 0, 2)
    kcmp, vcmp_t = _compress(kc_tm.reshape(batch, nc, chunk_w), vc_tm.reshape(batch, nc, chunk_w),
                             w1ka, w1kb, w1va, w1vb, pos_rows(cmp_pos[:CMP_STRIDE]), pos_rows(cmp_pos[CMP_STRIDE:]),
                             cmp_k_w2.T.astype(BF16), cmp_v_w2.T.astype(BF16), col(nsa_kc_gain), to_b(cos_c), to_b(sin_c))

    n_cmp = (seq - CMP_LEN) // CMP_STRIDE + 1
    nb = seq // SLC_LEN
    c_start = np.arange(nc)[None, :] * CMP_STRIDE
    s_start = np.arange(nb)[:, None] * SLC_LEN
    ov = (c_start < s_start + SLC_LEN) & (c_start + CMP_LEN - 1 >= s_start) & (np.arange(nc)[None, :] < n_cmp)
    ov_t = jnp.asarray(ov.astype(np.float32)).astype(BF16)

    oc_t, selb = _nsa_cmp(q_t, kcmp, vcmp_t, ov_t, batch, seq, tq_nsa)
    norm_max = lambda kn: jnp.sqrt(jnp.max(kn.reshape(KEY_NORM_ROWS, batch, seq), axis=2)).reshape(-1)
    os_t, ow_t = _nsa_sel(norm_max(kn_a), q_t, k_slc, v_slc_t, selb, k_win, v_win_t, batch, seq, tq_nsa, tk_sel)

    ob_t = _mla_attn(norm_max(kn_b), q_m, k_m, v_m_t, batch, seq, tq_mla, tk_mla)
    return oc_t, os_t, ow_t, gates_t, ob_t


def _peer(hn_t, h_t, peer_w_q, peer_sub_keys, peer_u, peer_v):
    cut, g1, r2, g2 = _peer_route(hn_t, peer_w_q.T.astype(BF16), peer_sub_keys.astype(BF16), PEER_ROUTE_TILE)
    return _peer_ffn(hn_t, h_t, peer_u.astype(BF16), peer_v.T.astype(BF16),
                     cut, g1, r2, g2, TOKEN_TILE, PEER_EXPERT_TILE)


def _layer(x, positions, norm1_gain, w_in, nsa_q_gain, nsa_kc_gain, nsa_ks_gain, nsa_kw_gain,
           cmp_pos, cmp_k_w1, cmp_k_w2, cmp_v_w1, cmp_v_w2,
           mla_q_lora_gain, mla_w_uq, mla_kv_lora_gain, mla_w_ukv, mla_q_gain, mla_k_gain,
           out_gain_a, out_gain_b, w_out, norm2_gain, peer_w_q, peer_sub_keys, peer_u, peer_v):
    batch, seq, d = x.shape
    oc_t, os_t, ow_t, gates_t, ob_t = _mixers(
        x, positions, norm1_gain, w_in, nsa_q_gain, nsa_kc_gain, nsa_ks_gain, nsa_kw_gain,
        cmp_pos, cmp_k_w1, cmp_k_w2, cmp_v_w1, cmp_v_w2,
        mla_q_lora_gain, mla_w_uq, mla_kv_lora_gain, mla_w_ukv, mla_q_gain, mla_k_gain)
    h_t, hn_t = _out_proj(oc_t, os_t, ow_t, gates_t, ob_t, x.reshape(batch * seq, d), _col(out_gain_a), _col(out_gain_b),
                          w_out.T.astype(BF16), _col(norm2_gain), TOKEN_TILE)
    return _peer(hn_t, h_t, peer_w_q, peer_sub_keys, peer_u, peer_v).reshape(batch, seq, d)


def kernel(x, positions, norm1_gain, w_in, nsa_q_gain, nsa_kc_gain, nsa_ks_gain, nsa_kw_gain, cmp_pos, cmp_k_w1, cmp_k_w2, cmp_v_w1, cmp_v_w2, mla_q_lora_gain, mla_w_uq, mla_kv_lora_gain, mla_w_ukv, mla_q_gain, mla_k_gain, out_gain_a, out_gain_b, w_out, norm2_gain, peer_w_q, peer_sub_keys, peer_u, peer_v):
    h = x
    for l in range(norm1_gain.shape[0]):
        h = _layer(h, positions, norm1_gain[l], w_in[l], nsa_q_gain[l], nsa_kc_gain[l], nsa_ks_gain[l], nsa_kw_gain[l],
                   cmp_pos[l], cmp_k_w1[l], cmp_k_w2[l], cmp_v_w1[l], cmp_v_w2[l],
                   mla_q_lora_gain[l], mla_w_uq[l], mla_kv_lora_gain[l], mla_w_ukv[l], mla_q_gain[l], mla_k_gain[l],
                   out_gain_a[l], out_gain_b[l], w_out[l], norm2_gain[l], peer_w_q[l], peer_sub_keys[l],
                   peer_u[l], peer_v[l])
    return h
```

```python
import functools

import jax
import jax.numpy as jnp
import numpy as np
from jax import lax
from jax.experimental import pallas as pl
from jax.experimental.pallas import tpu as pltpu

F32, BF16 = jnp.float32, jnp.bfloat16
EPS = 1e-6
NEG = -1e30
FORCE = 1e9
ROPE_THETA = 500000.0
LOG2E = 1.4426950408889634

D_MODEL = 1024
A_HEADS, A_GROUPS, A_DIM = 8, 2, 64
A_REP = A_HEADS // A_GROUPS
A_ROPE = A_DIM // 4
CMP_LEN, CMP_STRIDE, CMP_HIDDEN = 32, 16, 256
SLC_LEN, SLC_TOPK, WINDOW = 64, 16, 512
B_HEADS, Q_LORA, KV_LORA, B_NOPE, B_ROPE, B_V = 8, 256, 128, 64, 32, 64
B_QK = B_NOPE + B_ROPE
P_HEADS, N_KEYS, P_KEY_DIM, P_TOPK = 8, 128, 256, 16
N_EXPERTS = N_KEYS * N_KEYS

ROW_Q, ROW_KC, ROW_VC, ROW_KS, ROW_VS, ROW_KW, ROW_VW = 0, 512, 640, 768, 896, 1024, 1152
ROW_CQ, ROW_CKV, ROW_KR, ROW_GATE, PROJ_ROWS = 1280, 1536, 1664, 1696, 1728
GATE_ROWS = 32
KEY_NORM_ROWS = 8

VMEM_LIMIT = 56 * 1024 * 1024
NT_DIMS = (((1,), (1,)), ((), ()))


def _params(n_axes):
    return pltpu.CompilerParams(dimension_semantics=("arbitrary",) * n_axes, vmem_limit_bytes=VMEM_LIMIT)


def _dot(a, b):
    return jnp.dot(a, b, preferred_element_type=F32)


def _dot_nt(a, b):
    return lax.dot_general(a, b, NT_DIMS, preferred_element_type=F32)


def _row_sumsq(x):
    sq = x * x
    hi = sq.astype(BF16)
    lo = (sq - hi.astype(F32)).astype(BF16)
    ones = jnp.ones((8, x.shape[1]), BF16)
    return (_dot_nt(ones, hi) + _dot_nt(ones, lo))[0:1, :]


def _rms_rows(x):
    ss = jnp.sum(x * x, axis=0, keepdims=True)
    return x * lax.rsqrt(ss * (1.0 / x.shape[0]) + EPS)


def _rope_rows(y, cos, sin, off, half):
    x1, x2 = y[off:off + half], y[off + half:off + 2 * half]
    parts = [y[:off]] if off else []
    parts += [x1 * cos - x2 * sin, x2 * cos + x1 * sin]
    if off + 2 * half < y.shape[0]:
        parts.append(y[off + 2 * half:])
    return jnp.concatenate(parts, axis=0)


def _gelu_tanh(x):
    k = -2.0 * 0.7978845608028654 * LOG2E
    return x / (1.0 + jnp.exp2(x * (k + (k * 0.044715) * (x * x))))


def _in_proj_kernel(x_ref, g_ref, w_ref, cos_a_ref, sin_a_ref, qg_ref, ksg_ref, kwg_ref,
                    cos_b_ref, sin_b_ref, qlg_ref, kvlg_ref, mqg_ref, mkg_ref, wuq_ref, wukv_ref, *out_refs, seq, tw):
    x = x_ref[...]
    xg = (x * g_ref[...]).astype(BF16)
    p = _dot_nt(w_ref[...], xg)
    p = p * lax.rsqrt(_row_sumsq(x) * (1.0 / x.shape[1]) + EPS)
    gw = A_GROUPS * A_DIM
    rows = lambda start, n: p[start:start + n]
    _nsa_prep_kernel(rows(ROW_Q, A_HEADS * A_DIM), rows(ROW_KC, gw), rows(ROW_VC, gw), rows(ROW_KS, gw), rows(ROW_VS, gw),
                     rows(ROW_KW, gw), rows(ROW_VW, gw), rows(ROW_GATE, GATE_ROWS), cos_a_ref, sin_a_ref,
                     qg_ref, ksg_ref, kwg_ref, *out_refs[:8], out_refs[11], seq=seq, tw=tw)
    _mla_prep_kernel(rows(ROW_CQ, Q_LORA), rows(ROW_CKV, KV_LORA), rows(ROW_KR, B_ROPE), cos_b_ref, sin_b_ref,
                     qlg_ref, kvlg_ref, mqg_ref, mkg_ref, wuq_ref, wukv_ref, *out_refs[8:11], out_refs[12])


def _in_proj(x2, gain, w_t, cos_a, sin_a, q_gain, ks_gain, kw_gain, cos_b, sin_b, q_lora_gain, kv_lora_gain,
             mq_gain, mk_gain, wuq_t, wukv_t, tt, seq, tw):
    t, d = x2.shape
    gw = A_GROUPS * A_DIM
    kw = 128 + seq // SLC_LEN
    va, vb = A_DIM + ONES_ROWS, B_V + ONES_ROWS
    full = lambda a: pl.BlockSpec(a.shape, lambda i: (0,) * a.ndim)
    lanes = lambda n: pl.BlockSpec((n, tt), lambda i: (0, i))
    return pl.pallas_call(
        functools.partial(_in_proj_kernel, seq=seq, tw=tw),
        grid=(t // tt,),
        in_specs=[pl.BlockSpec((tt, d), lambda i: (i, 0)), full(gain), full(w_t),
                  lanes(A_ROPE // 2), lanes(A_ROPE // 2), full(q_gain), full(ks_gain), full(kw_gain),
                  lanes(B_ROPE // 2), lanes(B_ROPE // 2), full(q_lora_gain), full(kv_lora_gain), full(mq_gain),
                  full(mk_gain), full(wuq_t), full(wukv_t)],
        out_specs=[lanes(A_HEADS * A_DIM),
                   pl.BlockSpec((tt, gw), lambda i: (i, 0)),
                   pl.BlockSpec((tt, gw), lambda i: (i, 0)),
                   pl.BlockSpec((A_GROUPS, tt, kw), lambda i: (0, i, 0)),
                   pl.BlockSpec((A_GROUPS, 1, va, tt), lambda i: (0, i, 0, 0)),
                   pl.BlockSpec((A_GROUPS, tt, 128), lambda i: (0, i, 0)),
                   pl.BlockSpec((A_GROUPS, tt // tw, va, tw), lambda i: (0, i, 0, 0)),
                   lanes(GATE_ROWS),
                   pl.BlockSpec((B_HEADS, 128, tt), lambda i: (0, 0, i)),
                   pl.BlockSpec((B_HEADS, tt, 128), lambda i: (0, i, 0)),
                   pl.BlockSpec((B_HEADS, 1, vb, tt), lambda i: (0, i, 0, 0)),
                   lanes(KEY_NORM_ROWS), lanes(KEY_NORM_ROWS)],
        out_shape=[jax.ShapeDtypeStruct((A_HEADS * A_DIM, t), BF16),
                   jax.ShapeDtypeStruct((t, gw), F32),
                   jax.ShapeDtypeStruct((t, gw), F32),
                   jax.ShapeDtypeStruct((A_GROUPS, t, kw), BF16),
                   jax.ShapeDtypeStruct((A_GROUPS, t // tt, va, tt), BF16),
                   jax.ShapeDtypeStruct((A_GROUPS, t, 128), BF16),
                   jax.ShapeDtypeStruct((A_GROUPS, t // tw, va, tw), BF16),
                   jax.ShapeDtypeStruct((GATE_ROWS, t), F32),
                   jax.ShapeDtypeStruct((B_HEADS, 128, t), BF16),
                   jax.ShapeDtypeStruct((B_HEADS, t, 128), BF16),
                   jax.ShapeDtypeStruct((B_HEADS, t // tt, vb, tt), BF16),
                   jax.ShapeDtypeStruct((KEY_NORM_ROWS, t), F32),
                   jax.ShapeDtypeStruct((KEY_NORM_ROWS, t), F32)],
        compiler_params=_params(1),
        name="in_proj",
    )(x2, gain, w_t, cos_a, sin_a, q_gain, ks_gain, kw_gain, cos_b, sin_b, q_lora_gain, kv_lora_gain,
      mq_gain, mk_gain, wuq_t, wukv_t)


def _nsa_prep_kernel(q_ref, kc_ref, vc_ref, ks_ref, vs_ref, kw_ref, vw_ref, gt_ref, cos_ref, sin_ref,
                     qg_ref, ksg_ref, kwg_ref,
                     qo_ref, kco_ref, vco_ref, kso_ref, vso_ref, kwo_ref, vwo_ref, gto_ref, kn_ref, *, seq, tw):
    cos, sin = cos_ref[...], sin_ref[...]
    tt = cos.shape[1]
    nb = seq // SLC_LEN
    for h in range(A_HEADS):
        y = _rms_rows(q_ref[h * A_DIM:(h + 1) * A_DIM, :]) * qg_ref[...]
        y = _rope_rows(y, cos, sin, 0, A_ROPE // 2) * (A_DIM ** -0.5 * LOG2E)
        qo_ref[h * A_DIM:(h + 1) * A_DIM, :] = y.astype(BF16)
    kco_ref[...] = kc_ref[...].T
    vco_ref[...] = vc_ref[...].T
    zeros = jnp.zeros((A_DIM, tt), F32)
    tok = pl.program_id(0) * tt + lax.broadcasted_iota(jnp.int32, (tt, nb), 0)
    block_hot = jnp.where(lax.broadcasted_iota(jnp.int32, (tt, nb), 1) == (tok % seq) // SLC_LEN, 1.0, 0.0)
    kn_ref[...] = jnp.zeros(kn_ref.shape, F32)
    for g in range(A_GROUPS):
        for branch, (src, gain, dst) in enumerate(((ks_ref, ksg_ref, kso_ref), (kw_ref, kwg_ref, kwo_ref))):
            y = _rms_rows(src[g * A_DIM:(g + 1) * A_DIM, :]) * gain[...]
            y = _rope_rows(y, cos, sin, 0, A_ROPE // 2)
            row = branch * A_GROUPS + g
            kn_ref[row:row + 1, :] = jnp.sum(y * y, axis=0, keepdims=True)
            k_tm = jnp.concatenate([y, zeros], axis=0).T
            if dst is kso_ref:
                k_tm = jnp.concatenate([k_tm, block_hot], axis=1)
            dst[g] = k_tm.astype(BF16)
        vso_ref[g, 0] = _with_ones(vs_ref[g * A_DIM:(g + 1) * A_DIM, :])
        v_win = _with_ones(vw_ref[g * A_DIM:(g + 1) * A_DIM, :])
        for c in range(tt // tw):
            vwo_ref[g, c] = v_win[:, c * tw:(c + 1) * tw]
    gto_ref[...] = 1.0 / (1.0 + jnp.exp(-gt_ref[...]))


def _compress_kernel(kc_ref, vc_ref, w1ka_ref, w1kb_ref, w1va_ref, w1vb_ref, plo_ref, phi_ref,
                     w2k_ref, w2v_ref, kg_ref, cos_ref, sin_ref, ko_ref, vo_ref):
    nc = kc_ref.shape[1]
    zeros = jnp.zeros((A_DIM, nc), F32)
    for src, w1a, w1b, w2, is_k in ((kc_ref, w1ka_ref, w1kb_ref, w2k_ref, True),
                                    (vc_ref, w1va_ref, w1vb_ref, w2v_ref, False)):
        x = src[0]
        xlo = (x + plo_ref[...]).astype(BF16)
        xhi = (x + phi_ref[...]).astype(BF16)
        for g in range(A_GROUPS):
            first = _dot_nt(w1a[g], xlo)
            second = _dot_nt(w1b[g], xhi)
            hid = _gelu_tanh(first + pltpu.roll(second, nc - 1, axis=1)).astype(BF16)
            c = _dot(w2[...], hid)
            if is_k:
                y = _rope_rows(_rms_rows(c) * kg_ref[...], cos_ref[0], sin_ref[0], 0, A_ROPE // 2)
                ko_ref[0, g] = jnp.concatenate([y, zeros], axis=0).T.astype(BF16)
            else:
                vo_ref[0, g] = _with_ones(c)


def _compress(kc_chunks, vc_chunks, w1ka, w1kb, w1va, w1vb, plo, phi, w2k_t, w2v_t, kc_gain, cos_c, sin_c):
    b, nc, cw = kc_chunks.shape
    full = lambda a: pl.BlockSpec(a.shape, lambda i: (0,) * a.ndim)
    return pl.pallas_call(
        _compress_kernel,
        grid=(b,),
        in_specs=[pl.BlockSpec((1, nc, cw), lambda i: (i, 0, 0)), pl.BlockSpec((1, nc, cw), lambda i: (i, 0, 0)),
                  full(w1ka), full(w1kb), full(w1va), full(w1vb), full(plo), full(phi), full(w2k_t), full(w2v_t),
                  full(kc_gain),
                  pl.BlockSpec((1, A_ROPE // 2, nc), lambda i: (i, 0, 0)),
                  pl.BlockSpec((1, A_ROPE // 2, nc), lambda i: (i, 0, 0))],
        out_specs=[pl.BlockSpec((1, A_GROUPS, nc, 128), lambda i: (i, 0, 0, 0)),
                   pl.BlockSpec((1, A_GROUPS, A_DIM + ONES_ROWS, nc), lambda i: (i, 0, 0, 0))],
        out_shape=[jax.ShapeDtypeStruct((b, A_GROUPS, nc, 128), BF16),
                   jax.ShapeDtypeStruct((b, A_GROUPS, A_DIM + ONES_ROWS, nc), BF16)],
        compiler_params=_params(1),
        name="nsa_compress",
    )(kc_chunks, vc_chunks, w1ka, w1kb, w1va, w1vb, plo, phi, w2k_t, w2v_t, kc_gain, cos_c, sin_c)


def _stack_heads(q, tq):
    qs = jnp.concatenate([q[r * A_DIM:(r + 1) * A_DIM, :] for r in range(A_REP)], axis=1)
    return jnp.concatenate([qs, jnp.zeros_like(qs)], axis=0)


def _nsa_cmp_kernel(q_ref, k_ref, v_ref, ov_ref, o_ref, sb_ref, *, tq):
    i = pl.program_id(2)
    n = A_REP * tq
    nc = k_ref.shape[2]
    nb = ov_ref.shape[0]
    qp = _stack_heads(q_ref[...], tq)
    s = _dot(k_ref[0, 0], qp)
    cmp_end = lax.broadcasted_iota(jnp.int32, (nc, n), 0) * CMP_STRIDE + (CMP_LEN - 1)
    tok = i * tq + (lax.broadcasted_iota(jnp.int32, (nc, n), 1) & (tq - 1))
    s = jnp.where(cmp_end <= tok, s, NEG)
    m = jnp.max(s, axis=0, keepdims=True)
    p = jnp.exp2(s - jnp.where(m > 0.5 * NEG, m, 0.0)).astype(BF16)
    ocl = _dot(v_ref[0, 0], p)
    l = ocl[A_DIM:A_DIM + 1]
    inv = jnp.where(l > 0.0, 1.0 / l, 0.0)
    oc = ocl[:A_DIM] * inv
    for r in range(A_REP):
        o_ref[r * A_DIM:(r + 1) * A_DIM, :] = oc[:, r * tq:(r + 1) * tq]
    imp4 = _dot(ov_ref[...], p) * inv
    imp = imp4[:, 0:tq]
    for r in range(1, A_REP):
        imp = imp + imp4[:, r * tq:(r + 1) * tq]

    blk = lax.broadcasted_iota(jnp.int32, (nb, tq), 0)
    t = i * tq + lax.broadcasted_iota(jnp.int32, (nb, tq), 1)
    forced = (blk == t // SLC_LEN) | (blk == 0)
    v0 = jnp.where(forced, FORCE, jnp.where(blk * SLC_LEN <= t, imp, NEG))
    topk = min(SLC_TOPK, nb)

    v = jnp.where(blk == t // SLC_LEN, 2.0 * FORCE, v0)
    for _ in range(topk):
        v = jnp.where(v == jnp.max(v, axis=0, keepdims=True), -jnp.inf, v)
    taken = v == -jnp.inf
    sb_ref[0] = jnp.where(taken, 0.0, NEG)
    n_taken = jnp.sum(jnp.where(taken, 1.0, 0.0), axis=0, keepdims=True)

    @pl.when(jnp.max(jnp.abs(n_taken - float(topk))) > 0.0)
    def _():
        blk_f = blk.astype(F32)
        w = v0
        sel = jnp.zeros((nb, tq), F32)
        for _ in range(topk):
            mx = jnp.max(w, axis=0, keepdims=True)
            first = jnp.min(jnp.where(w == mx, blk_f, float(nb)), axis=0, keepdims=True)
            hit = blk_f == first
            sel = jnp.where(hit, 1.0, sel)
            w = jnp.where(hit, -jnp.inf, w)
        sb_ref[0] = jnp.where(sel > 0.0, 0.0, NEG)


def _nsa_cmp(q_t, kcmp, vcmp_t, ov_t, batch, seq, tq):
    nq = seq // tq
    nc = kcmp.shape[2]
    nb = ov_t.shape[0]
    t = q_t.shape[1]
    gr = A_REP * A_DIM
    return pl.pallas_call(
        functools.partial(_nsa_cmp_kernel, tq=tq),
        grid=(batch, A_GROUPS, nq),
        in_specs=[pl.BlockSpec((gr, tq), lambda b, g, i: (g, b * nq + i)),
                  pl.BlockSpec((1, 1, nc, 128), lambda b, g, i: (b, g, 0, 0)),
                  pl.BlockSpec((1, 1, A_DIM + ONES_ROWS, nc), lambda b, g, i: (b, g, 0, 0)),
                  pl.BlockSpec((nb, nc), lambda b, g, i: (0, 0))],
        out_specs=[pl.BlockSpec((gr, tq), lambda b, g, i: (g, b * nq + i)),
                   pl.BlockSpec((1, nb, tq), lambda b, g, i: (g, 0, b * nq + i))],
        out_shape=[jax.ShapeDtypeStruct((A_HEADS * A_DIM, t), F32),
                   jax.ShapeDtypeStruct((A_GROUPS, nb, t), F32)],
        compiler_params=_params(3),
        name="nsa_cmp",
    )(q_t, kcmp, vcmp_t, ov_t)


ONES_ROWS = 16
FLASH_UNROLL = 4


def _with_ones(v):
    return jnp.concatenate([v, jnp.ones((ONES_ROWS, v.shape[1]), F32)], axis=0).astype(BF16)


def _flash_update(s, v_t, m_scr, acc_scr):
    m_prev = m_scr[...]
    m_new = jnp.maximum(m_prev, jnp.max(s, axis=0, keepdims=True))
    alpha = jnp.exp2(m_prev - m_new)
    p = jnp.exp2(s - m_new)
    acc_scr[...] = alpha * acc_scr[...] + _dot(v_t, p.astype(BF16))
    m_scr[...] = m_new


def _flash_update_bounded(s, v_t, m_scr, acc_scr):
    acc_scr[...] += _dot(v_t, jnp.exp2(s - m_scr[...]).astype(BF16))


def _flash_result(acc_scr, dv):
    acc = acc_scr[...]
    return acc[:dv] * (1.0 / acc[dv:dv + 1])


BOUND_LIMIT = 56.0


def _logit_bound(q, k_norm_max):
    qf = q.astype(F32)
    return jnp.sqrt(jnp.sum(qf * qf, axis=0, keepdims=True)) * (k_norm_max * 1.02)


def _flash_bounded_or_online(bound, run, m_scr, acc_scr):
    acc_scr[...] = jnp.zeros(acc_scr.shape, F32)
    small = jnp.max(bound) <= BOUND_LIMIT

    @pl.when(small)
    def _():
        m_scr[...] = bound
        run(_flash_update_bounded)

    @pl.when(jnp.logical_not(small))
    def _():
        m_scr[...] = jnp.full(m_scr.shape, -jnp.inf, F32)
        run(_flash_update)


def _flash_causal(scores, values, mask, n_full, n_masked, s_scr, m_scr, acc_scr, update):
    unroll = s_scr.shape[0]
    s_scr[0] = scores(0)

    def trip(t, carry):
        j = unroll * t
        for u in range(unroll):
            s_scr[(u + 1) % unroll] = scores(j + u + 1)
            update(s_scr[u], values(j + u), m_scr, acc_scr)
        return carry

    lax.fori_loop(0, n_full // unroll, trip, 0)
    first = (n_full // unroll) * unroll
    for rest in range(unroll):

        @pl.when(n_full - first == rest)
        def _(rest=rest):
            for u in range(rest + n_masked):
                if u + 1 < rest + n_masked:
                    s_scr[(u + 1) % unroll] = scores(first + u + 1)
                s = s_scr[u % unroll]
                update(s if u < rest else mask(s, first + u), values(first + u), m_scr, acc_scr)


def _nsa_sel_kernel(kmax_ref, q_ref, k_ref, v_ref, sb_ref, kw_ref, vw_ref, o_ref, ow_ref, qa_scr, s_scr, m_scr, acc_scr,
                    *, tq, tk):
    b, g, i = pl.program_id(0), pl.program_id(1), pl.program_id(2)
    n = A_REP * tq
    q = q_ref[...]
    qs = jnp.concatenate([q[r * A_DIM:(r + 1) * A_DIM, :] for r in range(A_REP)], axis=1)
    sb = sb_ref[0].astype(BF16)
    qa_scr[...] = jnp.concatenate([qs, jnp.zeros_like(qs), jnp.concatenate([sb] * A_REP, axis=1)], axis=0)
    n_batch = pl.num_programs(0)

    def scores(j):
        return _dot(k_ref[0, pl.ds(pl.multiple_of(j * tk, tk), tk), :], qa_scr[...])

    def causal(s, j):
        kpos = j * tk + lax.broadcasted_iota(jnp.int32, (tk, n), 0)
        tok = i * tq + (lax.broadcasted_iota(jnp.int32, (tk, n), 1) & (tq - 1))
        return jnp.where(kpos <= tok, s, NEG)

    _flash_bounded_or_online(
        _logit_bound(qs, kmax_ref[g * n_batch + b]),
        lambda update: _flash_causal(scores, lambda j: v_ref[0, j], causal, (i * tq) // tk, max(1, tq // tk),
                                     s_scr, m_scr, acc_scr, update),
        m_scr, acc_scr)
    o = _flash_result(acc_scr, A_DIM)
    for r in range(A_REP):
        o_ref[r * A_DIM:(r + 1) * A_DIM, :] = o[:, r * tq:(r + 1) * tq]

    n_back = WINDOW // tq

    def window_tile(update, c):
        kt = i - n_back + c
        s = _dot(kw_ref[0, pl.ds(pl.multiple_of(kt * tq, tq), tq), :], qa_scr[0:128, :])
        if c in (0, n_back):
            kpos = kt * tq + lax.broadcasted_iota(jnp.int32, (tq, n), 0)
            tok = i * tq + (lax.broadcasted_iota(jnp.int32, (tq, n), 1) & (tq - 1))
            s = jnp.where(kpos > tok - WINDOW if c == 0 else kpos <= tok, s, NEG)
        update(s, vw_ref[0, kt], m_scr, acc_scr)

    def window(update):
        @pl.when(i >= n_back)
        def _():
            for c in range(n_back + 1):
                window_tile(update, c)

        @pl.when(i < n_back)
        def _():
            for c in range(n_back + 1):
                pl.when(i - n_back + c >= 0)(functools.partial(window_tile, update, c))

    _flash_bounded_or_online(_logit_bound(qs, kmax_ref[(A_GROUPS + g) * n_batch + b]), window, m_scr, acc_scr)
    o = _flash_result(acc_scr, A_DIM)
    for r in range(A_REP):
        ow_ref[r * A_DIM:(r + 1) * A_DIM, :] = o[:, r * tq:(r + 1) * tq]


def _nsa_sel(k_norm_max, q_t, k_aug, v_slc_tiles, selb, k_win, v_win_tiles, batch, seq, tq, tk):
    nq, nk = seq // tq, seq // tk
    nb = selb.shape[1]
    t = q_t.shape[1]
    gr = A_REP * A_DIM
    n = A_REP * tq
    kw = k_aug.shape[2]
    out = jax.ShapeDtypeStruct((A_HEADS * A_DIM, t), F32)
    grid_spec = pltpu.PrefetchScalarGridSpec(
        num_scalar_prefetch=1,
        grid=(batch, A_GROUPS, nq),
        in_specs=[pl.BlockSpec((gr, tq), lambda b, g, i, km: (g, b * nq + i)),
                  pl.BlockSpec((1, seq, kw), lambda b, g, i, km: (g, b, 0)),
                  pl.BlockSpec((1, nk, A_DIM + ONES_ROWS, tk), lambda b, g, i, km: (g, b, 0, 0)),
                  pl.BlockSpec((1, nb, tq), lambda b, g, i, km: (g, 0, b * nq + i)),
                  pl.BlockSpec((1, seq, 128), lambda b, g, i, km: (g, b, 0)),
                  pl.BlockSpec((1, nq, A_DIM + ONES_ROWS, tq), lambda b, g, i, km: (g, b, 0, 0))],
        out_specs=[pl.BlockSpec((gr, tq), lambda b, g, i, km: (g, b * nq + i)),
                   pl.BlockSpec((gr, tq), lambda b, g, i, km: (g, b * nq + i))],
        scratch_shapes=[pltpu.VMEM((kw, n), BF16), pltpu.VMEM((FLASH_UNROLL, tk, n), F32),
                        pltpu.VMEM((1, n), F32), pltpu.VMEM((A_DIM + ONES_ROWS, n), F32)],
    )
    return pl.pallas_call(
        functools.partial(_nsa_sel_kernel, tq=tq, tk=tk),
        grid_spec=grid_spec,
        out_shape=[out, out],
        compiler_params=_params(3),
        name="nsa_sel",
    )(k_norm_max, q_t, k_aug, v_slc_tiles, selb, k_win, v_win_tiles)


def _mla_prep_kernel(cq_ref, ckv_ref, kr_ref, cos_ref, sin_ref, qlg_ref, kvlg_ref, qg_ref, kg_ref, wuq_ref, wukv_ref,
                     qo_ref, ko_ref, vo_ref, kn_ref):
    cos, sin = cos_ref[...], sin_ref[...]
    tt = cos.shape[1]
    q_all = _dot(wuq_ref[...], (_rms_rows(cq_ref[...]) * qlg_ref[...]).astype(BF16))
    kv_all = _dot(wukv_ref[...], (_rms_rows(ckv_ref[...]) * kvlg_ref[...]).astype(BF16))
    kr = kr_ref[...]
    pad = jnp.zeros((128 - B_QK, tt), F32)
    for h in range(B_HEADS):
        y = _rms_rows(q_all[h * B_QK:(h + 1) * B_QK]) * qg_ref[...]
        y = _rope_rows(y, cos, sin, B_NOPE, B_ROPE // 2) * (B_QK ** -0.5 * LOG2E)
        qo_ref[h] = jnp.concatenate([y, pad], axis=0).astype(BF16)
        base = h * (B_NOPE + B_V)
        k = jnp.concatenate([kv_all[base:base + B_NOPE], kr], axis=0)
        y = _rope_rows(_rms_rows(k) * kg_ref[...], cos, sin, B_NOPE, B_ROPE // 2)
        kn_ref[h:h + 1, :] = jnp.sum(y * y, axis=0, keepdims=True)
        ko_ref[h] = jnp.concatenate([y, pad], axis=0).T.astype(BF16)
        vo_ref[h, 0] = _with_ones(kv_all[base + B_NOPE:base + B_NOPE + B_V])


def _mla_attn_kernel(kmax_ref, q_ref, k_ref, v_ref, o_ref, s_scr, m_scr, acc_scr, *, tq, tk):
    b, h, i = pl.program_id(0), pl.program_id(1), pl.program_id(2)

    def scores(j):
        return _dot(k_ref[0, pl.ds(pl.multiple_of(j * tk, tk), tk), :], q_ref[0])

    def causal(s, j):
        kpos = j * tk + lax.broadcasted_iota(jnp.int32, (tk, tq), 0)
        tok = i * tq + lax.broadcasted_iota(jnp.int32, (tk, tq), 1)
        return jnp.where(kpos <= tok, s, NEG)

    _flash_bounded_or_online(
        _logit_bound(q_ref[0], kmax_ref[h * pl.num_programs(0) + b]),
        lambda update: _flash_causal(scores, lambda j: v_ref[0, j], causal, (i * tq) // tk, max(1, tq // tk),
                                     s_scr, m_scr, acc_scr, update),
        m_scr, acc_scr)
    o_ref[...] = _flash_result(acc_scr, B_V)


def _mla_attn(k_norm_max, q_m, k_m, v_m_tiles, batch, seq, tq, tk):
    nq, nk = seq // tq, seq // tk
    t = q_m.shape[2]
    grid_spec = pltpu.PrefetchScalarGridSpec(
        num_scalar_prefetch=1,
        grid=(batch, B_HEADS, nq),
        in_specs=[pl.BlockSpec((1, 128, tq), lambda b, h, i, km: (h, 0, b * nq + i)),
                  pl.BlockSpec((1, seq, 128), lambda b, h, i, km: (h, b, 0)),
                  pl.BlockSpec((1, nk, B_V + ONES_ROWS, tk), lambda b, h, i, km: (h, b, 0, 0))],
        out_specs=pl.BlockSpec((B_V, tq), lambda b, h, i, km: (h, b * nq + i)),
        scratch_shapes=[pltpu.VMEM((FLASH_UNROLL, tk, tq), F32),
                        pltpu.VMEM((1, tq), F32), pltpu.VMEM((B_V + ONES_ROWS, tq), F32)],
    )
    return pl.pallas_call(
        functools.partial(_mla_attn_kernel, tq=tq, tk=tk),
        grid_spec=grid_spec,
        out_shape=jax.ShapeDtypeStruct((B_HEADS * B_V, t), F32),
        compiler_params=_params(3),
        name="mla_attn",
    )(k_norm_max, q_m, k_m, v_m_tiles)


def _out_proj_kernel(oc_ref, os_ref, ow_ref, gt_ref, ob_ref, x_ref, ga_ref, gb_ref, w_ref, g2_ref, h_ref, hn_ref):
    heads = []
    for h in range(A_HEADS):
        rows = slice(h * A_DIM, (h + 1) * A_DIM)
        heads.append(gt_ref[3 * h:3 * h + 1, :] * oc_ref[rows, :] + gt_ref[3 * h + 1:3 * h + 2, :] * os_ref[rows, :]
                     + gt_ref[3 * h + 2:3 * h + 3, :] * ow_ref[rows, :])
    oa = _rms_rows(jnp.concatenate(heads, axis=0)) * ga_ref[...]
    ob = _rms_rows(ob_ref[...]) * gb_ref[...]
    cat = jnp.concatenate([oa, ob], axis=0).astype(BF16)
    hid = x_ref[...].T + _dot(w_ref[...], cat)
    h_ref[...] = hid
    hn_ref[...] = (_rms_rows(hid) * g2_ref[...]).astype(BF16)


def _out_proj(oc_t, os_t, ow_t, gates_t, ob_t, x2, gain_a, gain_b, w_out_t, gain2, tt):
    t, d = x2.shape
    aw = oc_t.shape[0]
    bw = ob_t.shape[0]
    tok = lambda rows: pl.BlockSpec((rows, tt), lambda i: (0, i))
    full = lambda a: pl.BlockSpec(a.shape, lambda i: (0,) * a.ndim)
    return pl.pallas_call(
        _out_proj_kernel,
        grid=(t // tt,),
        in_specs=[tok(aw), tok(aw), tok(aw), tok(GATE_ROWS), tok(bw), pl.BlockSpec((tt, d), lambda i: (i, 0)),
                  full(gain_a), full(gain_b), full(w_out_t), full(gain2)],
        out_specs=[tok(d), tok(d)],
        out_shape=[jax.ShapeDtypeStruct((d, t), F32), jax.ShapeDtypeStruct((d, t), BF16)],
        compiler_params=_params(1),
        name="out_proj",
    )(oc_t, os_t, ow_t, gates_t, ob_t, x2, gain_a, gain_b, w_out_t, gain2)


def _top_ranked(s):
    n, rest = s.shape[0], s.shape[1:]
    row = lax.broadcasted_iota(jnp.int32, s.shape, 0).astype(F32)
    slot = lax.broadcasted_iota(jnp.int32, (P_TOPK,) + rest, 0)

    def body(a, carry):
        v, rank, vals = carry
        mx = jnp.max(v, axis=0, keepdims=True)
        first = jnp.min(jnp.where(v == mx, row, float(n)), axis=0, keepdims=True)
        hit = row == first
        rank = jnp.where(hit, jnp.asarray(a, F32), rank)
        v = jnp.where(hit, -jnp.inf, v)
        vals = jnp.where(slot == a, mx, vals)
        return v, rank, vals

    _, rank, vals = lax.fori_loop(0, P_TOPK, body,
                                  (s, jnp.full(s.shape, float(P_TOPK), F32), jnp.zeros((P_TOPK,) + rest, F32)))
    return rank, vals


def _pair_counts(v1, v2):
    k = v1.shape[0]
    slot = lax.broadcasted_iota(jnp.int32, v1.shape, 0).astype(F32)
    top = v1[0:1] + v2[0:1]

    def body(_, carry):
        count, front, z = carry
        mx = jnp.max(front, axis=0, keepdims=True)
        a_star = jnp.min(jnp.where(front == mx, slot, float(k)), axis=0, keepdims=True)
        hit = slot == a_star
        count = count + jnp.where(hit, 1.0, 0.0)
        nxt = jnp.sum(jnp.where(hit, count, 0.0), axis=0, keepdims=True)
        v2_nxt = jnp.sum(jnp.where(slot == nxt, v2, 0.0), axis=0, keepdims=True)
        front = jnp.where(hit, jnp.where(nxt < float(k), v1 + v2_nxt, -jnp.inf), front)
        return count, front, z + jnp.exp(mx - top)

    count, _, z = lax.fori_loop(0, k, body, (jnp.zeros(v1.shape, F32), v1 + v2[0:1], jnp.zeros(top.shape, F32)))
    return count, z


REMOVED = -2.0 ** 126
LANES = 128


def _top_ranked_pair_fast(s1, s2):
    n, tp = s1.shape
    slot = lax.broadcasted_iota(jnp.int32, (P_TOPK, tp), 0)

    def body(a, carry):
        code = REMOVED * (1.0 + jnp.asarray(a, F32) * (1.0 / 32.0))
        out = []
        for key, vals in (carry[0:2], carry[2:4]):
            mx = jnp.max(key, axis=0, keepdims=True)
            key = jnp.where(key == mx, code, key)
            out += [key, jnp.where(slot == a, mx, vals)]
        return tuple(out)

    zeros = jnp.zeros((P_TOPK, tp), F32)
    k1, t1, k2, t2 = lax.fori_loop(0, P_TOPK, body, (s1, zeros, s2, zeros))
    res, bad = [], jnp.zeros((1, tp), F32)
    for key, vals in ((k1, t1), (k2, t2)):
        removed = key <= REMOVED
        rank = jnp.where(removed, (key * (1.0 / REMOVED) - 1.0) * 32.0, float(P_TOPK))
        n_removed = jnp.sum(jnp.where(removed, 1.0, 0.0), axis=0, keepdims=True)
        bad = jnp.maximum(bad, jnp.abs(n_removed - float(P_TOPK)))
        res += [rank, vals]
    return res[0], res[1], res[2], res[3], bad


def _peer_route_kernel(hn_ref, wq_ref, keys_ref, cut_ref, g1_ref, r2_ref, g2_ref, q_scr, s_scr, rank_scr, vals_scr):
    tp = hn_ref.shape[1]
    half = P_KEY_DIM // 2
    q_scr[...] = _dot(wq_ref[...], hn_ref[...])
    for h in range(P_HEADS):
        q = _rms_rows(q_scr[h * P_KEY_DIM:(h + 1) * P_KEY_DIM, :]).astype(BF16)
        s_scr[0, h] = _dot(keys_ref[h, 0], q[:half])
        s_scr[1, h] = _dot(keys_ref[h, 1], q[half:])

    def first_level(rank_pair, flagged):
        for h in range(P_HEADS):
            for lt in range(tp // LANES):
                lanes = slice(lt * LANES, (lt + 1) * LANES)
                rank1, vals1, rank2, vals2, bad = rank_pair(s_scr[0, h, :, lanes], s_scr[1, h, :, lanes])
                flagged = jnp.maximum(flagged, bad)
                rank_scr[0, h, :, lanes] = rank1
                rank_scr[1, h, :, lanes] = rank2
                for a in range(P_TOPK):
                    vals_scr[0, a, h:h + 1, lanes] = vals1[a:a + 1]
                    vals_scr[1, a, h:h + 1, lanes] = vals2[a:a + 1]
        return flagged

    flagged = first_level(_top_ranked_pair_fast, jnp.zeros((1, LANES), F32))

    @pl.when(jnp.max(flagged) > 0.0)
    def _():
        first_level(lambda s1, s2: (*_top_ranked(s1), *_top_ranked(s2), jnp.zeros((1, LANES), F32)), flagged)

    count, z = _pair_counts(vals_scr[0], vals_scr[1])
    for h in range(P_HEADS):
        rank1 = rank_scr[0, h]
        cut = jnp.zeros_like(rank1)
        for a in range(P_TOPK):
            cut = jnp.where(rank1 == float(a), count[a, h:h + 1, :], cut)
        cut_ref[h] = cut.astype(BF16)
        g1_ref[h] = jnp.exp(s_scr[0, h] - vals_scr[0, 0, h:h + 1, :]).astype(BF16)
        r2_ref[h] = rank_scr[1, h].astype(BF16)
        g2_ref[h] = (jnp.exp(s_scr[1, h] - vals_scr[1, 0, h:h + 1, :]) * (1.0 / z[0, h:h + 1, :])).astype(BF16)


def _peer_route(hn_t, wq_t, sub_keys, tp):
    d, t = hn_t.shape
    halfs = jax.ShapeDtypeStruct((P_HEADS, N_KEYS, t), BF16)
    ospec = pl.BlockSpec((P_HEADS, N_KEYS, tp), lambda i: (0, 0, i))
    return pl.pallas_call(
        _peer_route_kernel,
        grid=(t // tp,),
        in_specs=[pl.BlockSpec((d, tp), lambda i: (0, i)),
                  pl.BlockSpec(wq_t.shape, lambda i: (0, 0)),
                  pl.BlockSpec(sub_keys.shape, lambda i: (0, 0, 0, 0))],
        out_specs=[ospec, ospec, ospec, ospec],
        out_shape=[halfs, halfs, halfs, halfs],
        scratch_shapes=[pltpu.VMEM((P_HEADS * P_KEY_DIM, tp), F32), pltpu.VMEM((2, P_HEADS, N_KEYS, tp), F32),
                        pltpu.VMEM((2, P_HEADS, N_KEYS, tp), F32), pltpu.VMEM((2, P_TOPK, P_HEADS, tp), F32)],
        compiler_params=_params(1),
        name="peer_route",
    )(hn_t, wq_t, sub_keys)


def _peer_ffn_kernel(hn_ref, h_ref, u_ref, v_ref, cut_ref, g1_ref, r2_ref, g2_ref, o_ref, acc_scr, *, te):
    e = pl.program_id(1)

    @pl.when(e == 0)
    def _():
        acc_scr[...] = jnp.zeros(acc_scr.shape, F32)

    hn = hn_ref[...]
    tt = hn.shape[1]

    def rows_bf16(ref, h, ii):
        return jnp.broadcast_to(ref[h, ii:ii + 1, :], (N_KEYS, tt))

    chunk = N_KEYS
    weights = []
    for c in range(te // chunk):
        a = _dot(u_ref[c * chunk:(c + 1) * chunk, :], hn)
        for k in range(chunk // N_KEYS):
            ii = c * (chunk // N_KEYS) + k
            gate = None
            for h in range(P_HEADS):
                chosen = r2_ref[h] < rows_bf16(cut_ref, h, ii)
                term = jnp.where(chosen, g2_ref[h], jnp.zeros((N_KEYS, tt), BF16)) * rows_bf16(g1_ref, h, ii)
                gate = term if gate is None else gate + term
            weights.append(gate * _gelu_tanh(a[k * N_KEYS:(k + 1) * N_KEYS, :]).astype(BF16))
    acc_scr[...] += _dot(v_ref[...], jnp.concatenate(weights, axis=0))

    @pl.when(e == pl.num_programs(1) - 1)
    def _():
        o_ref[...] = (h_ref[...] + acc_scr[...]).T


def _peer_ffn(hn_t, h_t, u_bf, v_t_bf, cut_k, g1_k, r2, g2, tt, te):
    d, t = hn_t.shape
    n_exp = u_bf.shape[0]
    kpe = te // N_KEYS
    return pl.pallas_call(
        functools.partial(_peer_ffn_kernel, te=te),
        grid=(t // tt, n_exp // te),
        in_specs=[pl.BlockSpec((d, tt), lambda i, e: (0, i)),
                  pl.BlockSpec((d, tt), lambda i, e: (0, i)),
                  pl.BlockSpec((te, d), lambda i, e: (e, 0)),
                  pl.BlockSpec((d, te), lambda i, e: (0, e)),
                  pl.BlockSpec((P_HEADS, kpe, tt), lambda i, e: (0, e, i)),
                  pl.BlockSpec((P_HEADS, kpe, tt), lambda i, e: (0, e, i)),
                  pl.BlockSpec((P_HEADS, N_KEYS, tt), lambda i, e: (0, 0, i)),
                  pl.BlockSpec((P_HEADS, N_KEYS, tt), lambda i, e: (0, 0, i))],
        out_specs=pl.BlockSpec((tt, d), lambda i, e: (i, 0)),
        out_shape=jax.ShapeDtypeStruct((t, d), F32),
        scratch_shapes=[pltpu.VMEM((d, tt), F32)],
        compiler_params=_params(2),
        name="peer_ffn",
    )(hn_t, h_t, u_bf, v_t_bf, cut_k, g1_k, r2, g2)


def _rope_tables_t(pos_flat, rot_dim):
    inv_freq = ROPE_THETA ** (-jnp.arange(0, rot_dim, 2, dtype=F32) / rot_dim)
    ang = pos_flat.astype(F32)[None, :] * inv_freq[:, None]
    return jnp.cos(ang), jnp.sin(ang)


def _expand_cmp_w1(w1):
    w = w1.reshape(CMP_LEN, A_DIM, CMP_HIDDEN)
    out = []
    for part in (w[:CMP_STRIDE], w[CMP_STRIDE:]):
        z = jnp.zeros_like(part)
        both = jnp.stack([jnp.concatenate([part, z], axis=1), jnp.concatenate([z, part], axis=1)])
        out.append(both.reshape(A_GROUPS, CMP_STRIDE * A_GROUPS * A_DIM, CMP_HIDDEN).transpose(0, 2, 1).astype(BF16))
    return out


TOKEN_TILE = 512
NSA_Q_TILE = 256
NSA_SEL_K_TILE = 512
MLA_Q_TILE = 1024
MLA_K_TILE = 512
PEER_ROUTE_TILE = 256
PEER_EXPERT_TILE = 2048


def _col(v):
    return v.reshape(-1, 1).astype(F32)


def _mixers(x, positions, norm1_gain, w_in, nsa_q_gain, nsa_kc_gain, nsa_ks_gain, nsa_kw_gain,
            cmp_pos, cmp_k_w1, cmp_k_w2, cmp_v_w1, cmp_v_w2,
            mla_q_lora_gain, mla_w_uq, mla_kv_lora_gain, mla_w_ukv, mla_q_gain, mla_k_gain):
    batch, seq, d = x.shape
    t = batch * seq
    tt = TOKEN_TILE
    tq_nsa = NSA_Q_TILE
    tk_sel = NSA_SEL_K_TILE
    tq_mla, tk_mla = MLA_Q_TILE, MLA_K_TILE
    assert tk_sel == tt and tk_mla == tt
    assert d == D_MODEL and seq % 512 == 0 and seq // SLC_LEN >= SLC_TOPK and WINDOW % tq_nsa == 0
    col = _col
    x2 = x.reshape(t, d)
    pos = positions.reshape(t)

    w_in_t = w_in.T
    gate_lo = sum((512, 128, 128, 128, 128, 128, 128))
    gate_hi = gate_lo + 3 * A_HEADS
    w_in_t = jnp.concatenate([w_in_t[:gate_lo], w_in_t[gate_hi:], w_in_t[gate_lo:gate_hi],
                              jnp.zeros((PROJ_ROWS - w_in_t.shape[0], d), F32)], axis=0).astype(BF16)
    cos_a, sin_a = _rope_tables_t(pos, A_ROPE)
    cos_b, sin_b = _rope_tables_t(pos, B_ROPE)
    q_t, kc_tm, vc_tm, k_slc, v_slc_t, k_win, v_win_t, gates_t, q_m, k_m, v_m_t, kn_a, kn_b = _in_proj(
        x2, norm1_gain.reshape(1, d), w_in_t, cos_a, sin_a, col(nsa_q_gain), col(nsa_ks_gain), col(nsa_kw_gain),
        cos_b, sin_b, col(mla_q_lora_gain), col(mla_kv_lora_gain), col(mla_q_gain), col(mla_k_gain),
        mla_w_uq.T.astype(BF16), mla_w_ukv.T.astype(BF16), tt, seq, tq_nsa)

    nc = seq // CMP_STRIDE
    chunk_w = CMP_STRIDE * A_GROUPS * A_DIM
    w1ka, w1kb = _expand_cmp_w1(cmp_k_w1)
    w1va, w1vb = _expand_cmp_w1(cmp_v_w1)
    pos_rows = lambda p: jnp.broadcast_to(p[:, None, :], (CMP_STRIDE, A_GROUPS, A_DIM)).reshape(1, chunk_w)
    cmp_end = jnp.minimum(jnp.arange(nc) * CMP_STRIDE + CMP_LEN - 1, seq - 1)
    cos_c, sin_c = _rope_tables_t(positions[:, cmp_end].reshape(-1), A_ROPE)
    to_b = lambda a: a.reshape(A_ROPE // 2, batch, nc).transpose(1, 0, 2)
    kcmp, vcmp_t = _compress(kc_tm.reshape(batch, nc, chunk_w), vc_tm.reshape(batch, nc, chunk_w),
                             w1ka, w1kb, w1va, w1vb, pos_rows(cmp_pos[:CMP_STRIDE]), pos_rows(cmp_pos[CMP_STRIDE:]),
                             cmp_k_w2.T.astype(BF16), cmp_v_w2.T.astype(BF16), col(nsa_kc_gain), to_b(cos_c), to_b(sin_c))

    n_cmp = (seq - CMP_LEN) // CMP_STRIDE + 1
    nb = seq // SLC_LEN
    c_start = np.arange(nc)[None, :] * CMP_STRIDE
    s_start = np.arange(nb)[:, None] * SLC_LEN
    ov = (c_start < s_start + SLC_LEN) & (c_start + CMP_LEN - 1 >= s_start) & (np.arange(nc)[None, :] < n_cmp)
    ov_t = jnp.asarray(ov.astype(np.float32)).astype(BF16)

    oc_t, selb = _nsa_cmp(q_t, kcmp, vcmp_t, ov_t, batch, seq, tq_nsa)
    norm_max = lambda kn: jnp.sqrt(jnp.max(kn.reshape(KEY_NORM_ROWS, batch, seq), axis=2)).reshape(-1)
    os_t, ow_t = _nsa_sel(norm_max(kn_a), q_t, k_slc, v_slc_t, selb, k_win, v_win_t, batch, seq, tq_nsa, tk_sel)

    ob_t = _mla_attn(norm_max(kn_b), q_m, k_m, v_m_t, batch, seq, tq_mla, tk_mla)
    return oc_t, os_t, ow_t, gates_t, ob_t


def _peer(hn_t, h_t, peer_w_q, peer_sub_keys, peer_u, peer_v):
    cut, g1, r2, g2 = _peer_route(hn_t, peer_w_q.T.astype(BF16), peer_sub_keys.astype(BF16), PEER_ROUTE_TILE)
    return _peer_ffn(hn_t, h_t, peer_u.astype(BF16), peer_v.T.astype(BF16),
                     cut, g1, r2, g2, TOKEN_TILE, PEER_EXPERT_TILE)


def _layer(x, positions, norm1_gain, w_in, nsa_q_gain, nsa_kc_gain, nsa_ks_gain, nsa_kw_gain,
           cmp_pos, cmp_k_w1, cmp_k_w2, cmp_v_w1, cmp_v_w2,
           mla_q_lora_gain, mla_w_uq, mla_kv_lora_gain, mla_w_ukv, mla_q_gain, mla_k_gain,
           out_gain_a, out_gain_b, w_out, norm2_gain, peer_w_q, peer_sub_keys, peer_u, peer_v):
    batch, seq, d = x.shape
    oc_t, os_t, ow_t, gates_t, ob_t = _mixers(
        x, positions, norm1_gain, w_in, nsa_q_gain, nsa_kc_gain, nsa_ks_gain, nsa_kw_gain,
        cmp_pos, cmp_k_w1, cmp_k_w2, cmp_v_w1, cmp_v_w2,
        mla_q_lora_gain, mla_w_uq, mla_kv_lora_gain, mla_w_ukv, mla_q_gain, mla_k_gain)
    h_t, hn_t = _out_proj(oc_t, os_t, ow_t, gates_t, ob_t, x.reshape(batch * seq, d), _col(out_gain_a), _col(out_gain_b),
                          w_out.T.astype(BF16), _col(norm2_gain), TOKEN_TILE)
    return _peer(hn_t, h_t, peer_w_q, peer_sub_keys, peer_u, peer_v).reshape(batch, seq, d)


def kernel(x, positions, norm1_gain, w_in, nsa_q_gain, nsa_kc_gain, nsa_ks_gain, nsa_kw_gain, cmp_pos, cmp_k_w1, cmp_k_w2, cmp_v_w1, cmp_v_w2, mla_q_lora_gain, mla_w_uq, mla_kv_lora_gain, mla_w_ukv, mla_q_gain, mla_k_gain, out_gain_a, out_gain_b, w_out, norm2_gain, peer_w_q, peer_sub_keys, peer_u, peer_v):
    h = x
    for l in range(norm1_gain.shape[0]):
        h = _layer(h, positions, norm1_gain[l], w_in[l], nsa_q_gain[l], nsa_kc_gain[l], nsa_ks_gain[l], nsa_kw_gain[l],
                   cmp_pos[l], cmp_k_w1[l], cmp_k_w2[l], cmp_v_w1[l], cmp_v_w2[l],
                   mla_q_lora_gain[l], mla_w_uq[l], mla_kv_lora_gain[l], mla_w_ukv[l], mla_q_gain[l], mla_k_gain[l],
                   out_gain_a[l], out_gain_b[l], w_out[l], norm2_gain[l], peer_w_q[l], peer_sub_keys[l],
                   peer_u[l], peer_v[l])
    return h
```

```python
import functools

import jax
import jax.numpy as jnp
import numpy as np
from jax import lax
from jax.experimental import pallas as pl
from jax.experimental.pallas import tpu as pltpu

F32, BF16 = jnp.float32, jnp.bfloat16
EPS = 1e-6
NEG = -1e30
FORCE = 1e9
ROPE_THETA = 500000.0
LOG2E = 1.4426950408889634

D_MODEL = 1024
A_HEADS, A_GROUPS, A_DIM = 8, 2, 64
A_REP = A_HEADS // A_GROUPS
A_ROPE = A_DIM // 4
CMP_LEN, CMP_STRIDE, CMP_HIDDEN = 32, 16, 256
SLC_LEN, SLC_TOPK, WINDOW = 64, 16, 512
B_HEADS, Q_LORA, KV_LORA, B_NOPE, B_ROPE, B_V = 8, 256, 128, 64, 32, 64
B_QK = B_NOPE + B_ROPE
P_HEADS, N_KEYS, P_KEY_DIM, P_TOPK = 8, 128, 256, 16
N_EXPERTS = N_KEYS * N_KEYS

ROW_Q, ROW_KC, ROW_VC, ROW_KS, ROW_VS, ROW_KW, ROW_VW = 0, 512, 640, 768, 896, 1024, 1152
ROW_CQ, ROW_CKV, ROW_KR, ROW_GATE, PROJ_ROWS = 1280, 1536, 1664, 1696, 1728
GATE_ROWS = 32
KEY_NORM_ROWS = 8

VMEM_LIMIT = 56 * 1024 * 1024
NT_DIMS = (((1,), (1,)), ((), ()))


def _params(n_axes):
    return pltpu.CompilerParams(dimension_semantics=("arbitrary",) * n_axes, vmem_limit_bytes=VMEM_LIMIT)


def _dot(a, b):
    return jnp.dot(a, b, preferred_element_type=F32)


def _dot_nt(a, b):
    return lax.dot_general(a, b, NT_DIMS, preferred_element_type=F32)


def _row_sumsq(x):
    sq = x * x
    hi = sq.astype(BF16)
    lo = (sq - hi.astype(F32)).astype(BF16)
    ones = jnp.ones((8, x.shape[1]), BF16)
    return (_dot_nt(ones, hi) + _dot_nt(ones, lo))[0:1, :]


def _rms_rows(x):
    ss = jnp.sum(x * x, axis=0, keepdims=True)
    return x * lax.rsqrt(ss * (1.0 / x.shape[0]) + EPS)


def _rope_rows(y, cos, sin, off, half):
    x1, x2 = y[off:off + half], y[off + half:off + 2 * half]
    parts = [y[:off]] if off else []
    parts += [x1 * cos - x2 * sin, x2 * cos + x1 * sin]
    if off + 2 * half < y.shape[0]:
        parts.append(y[off + 2 * half:])
    return jnp.concatenate(parts, axis=0)


def _gelu_tanh(x):
    k = -2.0 * 0.7978845608028654 * LOG2E
    return x / (1.0 + jnp.exp2(x * (k + (k * 0.044715) * (x * x))))


def _in_proj_kernel(x_ref, g_ref, w_ref, cos_a_ref, sin_a_ref, qg_ref, ksg_ref, kwg_ref,
                    cos_b_ref, sin_b_ref, qlg_ref, kvlg_ref, mqg_ref, mkg_ref, wuq_ref, wukv_ref, *out_refs, seq, tw):
    x = x_ref[...]
    xg = (x * g_ref[...]).astype(BF16)
    p = _dot_nt(w_ref[...], xg)
    p = p * lax.rsqrt(_row_sumsq(x) * (1.0 / x.shape[1]) + EPS)
    gw = A_GROUPS * A_DIM
    rows = lambda start, n: p[start:start + n]
    _nsa_prep_kernel(rows(ROW_Q, A_HEADS * A_DIM), rows(ROW_KC, gw), rows(ROW_VC, gw), rows(ROW_KS, gw), rows(ROW_VS, gw),
                     rows(ROW_KW, gw), rows(ROW_VW, gw), rows(ROW_GATE, GATE_ROWS), cos_a_ref, sin_a_ref,
                     qg_ref, ksg_ref, kwg_ref, *out_refs[:8], out_refs[11], seq=seq, tw=tw)
    _mla_prep_kernel(rows(ROW_CQ, Q_LORA), rows(ROW_CKV, KV_LORA), rows(ROW_KR, B_ROPE), cos_b_ref, sin_b_ref,
                     qlg_ref, kvlg_ref, mqg_ref, mkg_ref, wuq_ref, wukv_ref, *out_refs[8:11], out_refs[12])


def _in_proj(x2, gain, w_t, cos_a, sin_a, q_gain, ks_gain, kw_gain, cos_b, sin_b, q_lora_gain, kv_lora_gain,
             mq_gain, mk_gain, wuq_t, wukv_t, tt, seq, tw):
    t, d = x2.shape
    gw = A_GROUPS * A_DIM
    kw = 128 + seq // SLC_LEN
    va, vb = A_DIM + ONES_ROWS, B_V + ONES_ROWS
    full = lambda a: pl.BlockSpec(a.shape, lambda i: (0,) * a.ndim)
    lanes = lambda n: pl.BlockSpec((n, tt), lambda i: (0, i))
    return pl.pallas_call(
        functools.partial(_in_proj_kernel, seq=seq, tw=tw),
        grid=(t // tt,),
        in_specs=[pl.BlockSpec((tt, d), lambda i: (i, 0)), full(gain), full(w_t),
                  lanes(A_ROPE // 2), lanes(A_ROPE // 2), full(q_gain), full(ks_gain), full(kw_gain),
                  lanes(B_ROPE // 2), lanes(B_ROPE // 2), full(q_lora_gain), full(kv_lora_gain), full(mq_gain),
                  full(mk_gain), full(wuq_t), full(wukv_t)],
        out_specs=[lanes(A_HEADS * A_DIM),
                   pl.BlockSpec((tt, gw), lambda i: (i, 0)),
                   pl.BlockSpec((tt, gw), lambda i: (i, 0)),
                   pl.BlockSpec((A_GROUPS, tt, kw), lambda i: (0, i, 0)),
                   pl.BlockSpec((A_GROUPS, 1, va, tt), lambda i: (0, i, 0, 0)),
                   pl.BlockSpec((A_GROUPS, tt, 128), lambda i: (0, i, 0)),
                   pl.BlockSpec((A_GROUPS, tt // tw, va, tw), lambda i: (0, i, 0, 0)),
                   lanes(GATE_ROWS),
                   pl.BlockSpec((B_HEADS, 128, tt), lambda i: (0, 0, i)),
                   pl.BlockSpec((B_HEADS, tt, 128), lambda i: (0, i, 0)),
                   pl.BlockSpec((B_HEADS, 1, vb, tt), lambda i: (0, i, 0, 0)),
                   lanes(KEY_NORM_ROWS), lanes(KEY_NORM_ROWS)],
        out_shape=[jax.ShapeDtypeStruct((A_HEADS * A_DIM, t), BF16),
                   jax.ShapeDtypeStruct((t, gw), F32),
                   jax.ShapeDtypeStruct((t, gw), F32),
                   jax.ShapeDtypeStruct((A_GROUPS, t, kw), BF16),
                   jax.ShapeDtypeStruct((A_GROUPS, t // tt, va, tt), BF16),
                   jax.ShapeDtypeStruct((A_GROUPS, t, 128), BF16),
                   jax.ShapeDtypeStruct((A_GROUPS, t // tw, va, tw), BF16),
                   jax.ShapeDtypeStruct((GATE_ROWS, t), F32),
                   jax.ShapeDtypeStruct((B_HEADS, 128, t), BF16),
                   jax.ShapeDtypeStruct((B_HEADS, t, 128), BF16),
                   jax.ShapeDtypeStruct((B_HEADS, t // tt, vb, tt), BF16),
                   jax.ShapeDtypeStruct((KEY_NORM_ROWS, t), F32),
                   jax.ShapeDtypeStruct((KEY_NORM_ROWS, t), F32)],
        compiler_params=_params(1),
        name="in_proj",
    )(x2, gain, w_t, cos_a, sin_a, q_gain, ks_gain, kw_gain, cos_b, sin_b, q_lora_gain, kv_lora_gain,
      mq_gain, mk_gain, wuq_t, wukv_t)


def _nsa_prep_kernel(q_ref, kc_ref, vc_ref, ks_ref, vs_ref, kw_ref, vw_ref, gt_ref, cos_ref, sin_ref,
                     qg_ref, ksg_ref, kwg_ref,
                     qo_ref, kco_ref, vco_ref, kso_ref, vso_ref, kwo_ref, vwo_ref, gto_ref, kn_ref, *, seq, tw):
    cos, sin = cos_ref[...], sin_ref[...]
    tt = cos.shape[1]
    nb = seq // SLC_LEN
    for h in range(A_HEADS):
        y = _rms_rows(q_ref[h * A_DIM:(h + 1) * A_DIM, :]) * qg_ref[...]
        y = _rope_rows(y, cos, sin, 0, A_ROPE // 2) * (A_DIM ** -0.5 * LOG2E)
        qo_ref[h * A_DIM:(h + 1) * A_DIM, :] = y.astype(BF16)
    kco_ref[...] = kc_ref[...].T
    vco_ref[...] = vc_ref[...].T
    zeros = jnp.zeros((A_DIM, tt), F32)
    tok = pl.program_id(0) * tt + lax.broadcasted_iota(jnp.int32, (tt, nb), 0)
    block_hot = jnp.where(lax.broadcasted_iota(jnp.int32, (tt, nb), 1) == (tok % seq) // SLC_LEN, 1.0, 0.0)
    kn_ref[...] = jnp.zeros(kn_ref.shape, F32)
    for g in range(A_GROUPS):
        for branch, (src, gain, dst) in enumerate(((ks_ref, ksg_ref, kso_ref), (kw_ref, kwg_ref, kwo_ref))):
            y = _rms_rows(src[g * A_DIM:(g + 1) * A_DIM, :]) * gain[...]
            y = _rope_rows(y, cos, sin, 0, A_ROPE // 2)
            row = branch * A_GROUPS + g
            kn_ref[row:row + 1, :] = jnp.sum(y * y, axis=0, keepdims=True)
            k_tm = jnp.concatenate([y, zeros], axis=0).T
            if dst is kso_ref:
                k_tm = jnp.concatenate([k_tm, block_hot], axis=1)
            dst[g] = k_tm.astype(BF16)
        vso_ref[g, 0] = _with_ones(vs_ref[g * A_DIM:(g + 1) * A_DIM, :])
        v_win = _with_ones(vw_ref[g * A_DIM:(g + 1) * A_DIM, :])
        for c in range(tt // tw):
            vwo_ref[g, c] = v_win[:, c * tw:(c + 1) * tw]
    gto_ref[...] = 1.0 / (1.0 + jnp.exp(-gt_ref[...]))


def _compress_kernel(kc_ref, vc_ref, w1ka_ref, w1kb_ref, w1va_ref, w1vb_ref, plo_ref, phi_ref,
                     w2k_ref, w2v_ref, kg_ref, cos_ref, sin_ref, ko_ref, vo_ref):
    nc = kc_ref.shape[1]
    zeros = jnp.zeros((A_DIM, nc), F32)
    for src, w1a, w1b, w2, is_k in ((kc_ref, w1ka_ref, w1kb_ref, w2k_ref, True),
                                    (vc_ref, w1va_ref, w1vb_ref, w2v_ref, False)):
        x = src[0]
        xlo = (x + plo_ref[...]).astype(BF16)
        xhi = (x + phi_ref[...]).astype(BF16)
        for g in range(A_GROUPS):
            first = _dot_nt(w1a[g], xlo)
            second = _dot_nt(w1b[g], xhi)
            hid = _gelu_tanh(first + pltpu.roll(second, nc - 1, axis=1)).astype(BF16)
            c = _dot(w2[...], hid)
            if is_k:
                y = _rope_rows(_rms_rows(c) * kg_ref[...], cos_ref[0], sin_ref[0], 0, A_ROPE // 2)
                ko_ref[0, g] = jnp.concatenate([y, zeros], axis=0).T.astype(BF16)
            else:
                vo_ref[0, g] = _with_ones(c)


def _compress(kc_chunks, vc_chunks, w1ka, w1kb, w1va, w1vb, plo, phi, w2k_t, w2v_t, kc_gain, cos_c, sin_c):
    b, nc, cw = kc_chunks.shape
    full = lambda a: pl.BlockSpec(a.shape, lambda i: (0,) * a.ndim)
    return pl.pallas_call(
        _compress_kernel,
        grid=(b,),
        in_specs=[pl.BlockSpec((1, nc, cw), lambda i: (i, 0, 0)), pl.BlockSpec((1, nc, cw), lambda i: (i, 0, 0)),
                  full(w1ka), full(w1kb), full(w1va), full(w1vb), full(plo), full(phi), full(w2k_t), full(w2v_t),
                  full(kc_gain),
                  pl.BlockSpec((1, A_ROPE // 2, nc), lambda i: (i, 0, 0)),
                  pl.BlockSpec((1, A_ROPE // 2, nc), lambda i: (i, 0, 0))],
        out_specs=[pl.BlockSpec((1, A_GROUPS, nc, 128), lambda i: (i, 0, 0, 0)),
                   pl.BlockSpec((1, A_GROUPS, A_DIM + ONES_ROWS, nc), lambda i: (i, 0, 0, 0))],
        out_shape=[jax.ShapeDtypeStruct((b, A_GROUPS, nc, 128), BF16),
                   jax.ShapeDtypeStruct((b, A_GROUPS, A_DIM + ONES_ROWS, nc), BF16)],
        compiler_params=_params(1),
        name="nsa_compress",
    )(kc_chunks, vc_chunks, w1ka, w1kb, w1va, w1vb, plo, phi, w2k_t, w2v_t, kc_gain, cos_c, sin_c)


def _stack_heads(q, tq):
    qs = jnp.concatenate([q[r * A_DIM:(r + 1) * A_DIM, :] for r in range(A_REP)], axis=1)
    return jnp.concatenate([qs, jnp.zeros_like(qs)], axis=0)


def _nsa_cmp_kernel(q_ref, k_ref, v_ref, ov_ref, o_ref, sb_ref, *, tq):
    i = pl.program_id(2)
    n = A_REP * tq
    nc = k_ref.shape[2]
    nb = ov_ref.shape[0]
    qp = _stack_heads(q_ref[...], tq)
    s = _dot(k_ref[0, 0], qp)
    cmp_end = lax.broadcasted_iota(jnp.int32, (nc, n), 0) * CMP_STRIDE + (CMP_LEN - 1)
    tok = i * tq + (lax.broadcasted_iota(jnp.int32, (nc, n), 1) & (tq - 1))
    s = jnp.where(cmp_end <= tok, s, NEG)
    m = jnp.max(s, axis=0, keepdims=True)
    p = jnp.exp2(s - jnp.where(m > 0.5 * NEG, m, 0.0)).astype(BF16)
    ocl = _dot(v_ref[0, 0], p)
    l = ocl[A_DIM:A_DIM + 1]
    inv = jnp.where(l > 0.0, 1.0 / l, 0.0)
    oc = ocl[:A_DIM] * inv
    for r in range(A_REP):
        o_ref[r * A_DIM:(r + 1) * A_DIM, :] = oc[:, r * tq:(r + 1) * tq]
    imp4 = _dot(ov_ref[...], p) * inv
    imp = imp4[:, 0:tq]
    for r in range(1, A_REP):
        imp = imp + imp4[:, r * tq:(r + 1) * tq]

    blk = lax.broadcasted_iota(jnp.int32, (nb, tq), 0)
    t = i * tq + lax.broadcasted_iota(jnp.int32, (nb, tq), 1)
    forced = (blk == t // SLC_LEN) | (blk == 0)
    v0 = jnp.where(forced, FORCE, jnp.where(blk * SLC_LEN <= t, imp, NEG))
    topk = min(SLC_TOPK, nb)

    v = jnp.where(blk == t // SLC_LEN, 2.0 * FORCE, v0)
    for _ in range(topk):
        v = jnp.where(v == jnp.max(v, axis=0, keepdims=True), -jnp.inf, v)
    taken = v == -jnp.inf
    sb_ref[0] = jnp.where(taken, 0.0, NEG)
    n_taken = jnp.sum(jnp.where(taken, 1.0, 0.0), axis=0, keepdims=True)

    @pl.when(jnp.max(jnp.abs(n_taken - float(topk))) > 0.0)
    def _():
        blk_f = blk.astype(F32)
        w = v0
        sel = jnp.zeros((nb, tq), F32)
        for _ in range(topk):
            mx = jnp.max(w, axis=0, keepdims=True)
            first = jnp.min(jnp.where(w == mx, blk_f, float(nb)), axis=0, keepdims=True)
            hit = blk_f == first
            sel = jnp.where(hit, 1.0, sel)
            w = jnp.where(hit, -jnp.inf, w)
        sb_ref[0] = jnp.where(sel > 0.0, 0.0, NEG)


def _nsa_cmp(q_t, kcmp, vcmp_t, ov_t, batch, seq, tq):
    nq = seq // tq
    nc = kcmp.shape[2]
    nb = ov_t.shape[0]
    t = q_t.shape[1]
    gr = A_REP * A_DIM
    return pl.pallas_call(
        functools.partial(_nsa_cmp_kernel, tq=tq),
        grid=(batch, A_GROUPS, nq),
        in_specs=[pl.BlockSpec((gr, tq), lambda b, g, i: (g, b * nq + i)),
                  pl.BlockSpec((1, 1, nc, 128), lambda b, g, i: (b, g, 0, 0)),
                  pl.BlockSpec((1, 1, A_DIM + ONES_ROWS, nc), lambda b, g, i: (b, g, 0, 0)),
                  pl.BlockSpec((nb, nc), lambda b, g, i: (0, 0))],
        out_specs=[pl.BlockSpec((gr, tq), lambda b, g, i: (g, b * nq + i)),
                   pl.BlockSpec((1, nb, tq), lambda b, g, i: (g, 0, b * nq + i))],
        out_shape=[jax.ShapeDtypeStruct((A_HEADS * A_DIM, t), F32),
                   jax.ShapeDtypeStruct((A_GROUPS, nb, t), F32)],
        compiler_params=_params(3),
        name="nsa_cmp",
    )(q_t, kcmp, vcmp_t, ov_t)


ONES_ROWS = 16
FLASH_UNROLL = 4


def _with_ones(v):
    return jnp.concatenate([v, jnp.ones((ONES_ROWS, v.shape[1]), F32)], axis=0).astype(BF16)


def _flash_update(s, v_t, m_scr, acc_scr):
    m_prev = m_scr[...]
    m_new = jnp.maximum(m_prev, jnp.max(s, axis=0, keepdims=True))
    alpha = jnp.exp2(m_prev - m_new)
    p = jnp.exp2(s - m_new)
    acc_scr[...] = alpha * acc_scr[...] + _dot(v_t, p.astype(BF16))
    m_scr[...] = m_new


def _flash_update_bounded(s, v_t, m_scr, acc_scr):
    acc_scr[...] += _dot(v_t, jnp.exp2(s - m_scr[...]).astype(BF16))


def _flash_result(acc_scr, dv):
    acc = acc_scr[...]
    return acc[:dv] * (1.0 / acc[dv:dv + 1])


BOUND_LIMIT = 56.0


def _logit_bound(q, k_norm_max):
    qf = q.astype(F32)
    return jnp.sqrt(jnp.sum(qf * qf, axis=0, keepdims=True)) * (k_norm_max * 1.02)


def _flash_bounded_or_online(bound, run, m_scr, acc_scr):
    acc_scr[...] = jnp.zeros(acc_scr.shape, F32)
    small = jnp.max(bound) <= BOUND_LIMIT

    @pl.when(small)
    def _():
        m_scr[...] = bound
        run(_flash_update_bounded)

    @pl.when(jnp.logical_not(small))
    def _():
        m_scr[...] = jnp.full(m_scr.shape, -jnp.inf, F32)
        run(_flash_update)


def _flash_causal(scores, values, mask, n_full, n_masked, s_scr, m_scr, acc_scr, update):
    unroll = s_scr.shape[0]
    s_scr[0] = scores(0)

    def trip(t, carry):
        j = unroll * t
        for u in range(unroll):
            s_scr[(u + 1) % unroll] = scores(j + u + 1)
            update(s_scr[u], values(j + u), m_scr, acc_scr)
        return carry

    lax.fori_loop(0, n_full // unroll, trip, 0)
    first = (n_full // unroll) * unroll
    for rest in range(unroll):

        @pl.when(n_full - first == rest)
        def _(rest=rest):
            for u in range(rest + n_masked):
                if u + 1 < rest + n_masked:
                    s_scr[(u + 1) % unroll] = scores(first + u + 1)
                s = s_scr[u % unroll]
                update(s if u < rest else mask(s, first + u), values(first + u), m_scr, acc_scr)


def _nsa_sel_kernel(kmax_ref, q_ref, k_ref, v_ref, sb_ref, kw_ref, vw_ref, o_ref, ow_ref, qa_scr, s_scr, m_scr, acc_scr,
                    *, tq, tk):
    b, g, i = pl.program_id(0), pl.program_id(1), pl.program_id(2)
    n = A_REP * tq
    q = q_ref[...]
    qs = jnp.concatenate([q[r * A_DIM:(r + 1) * A_DIM, :] for r in range(A_REP)], axis=1)
    sb = sb_ref[0].astype(BF16)
    qa_scr[...] = jnp.concatenate([qs, jnp.zeros_like(qs), jnp.concatenate([sb] * A_REP, axis=1)], axis=0)
    n_batch = pl.num_programs(0)

    def scores(j):
        return _dot(k_ref[0, pl.ds(pl.multiple_of(j * tk, tk), tk), :], qa_scr[...])

    def causal(s, j):
        kpos = j * tk + lax.broadcasted_iota(jnp.int32, (tk, n), 0)
        tok = i * tq + (lax.broadcasted_iota(jnp.int32, (tk, n), 1) & (tq - 1))
        return jnp.where(kpos <= tok, s, NEG)

    _flash_bounded_or_online(
        _logit_bound(qs, kmax_ref[g * n_batch + b]),
        lambda update: _flash_causal(scores, lambda j: v_ref[0, j], causal, (i * tq) // tk, max(1, tq // tk),
                                     s_scr, m_scr, acc_scr, update),
        m_scr, acc_scr)
    o = _flash_result(acc_scr, A_DIM)
    for r in range(A_REP):
        o_ref[r * A_DIM:(r + 1) * A_DIM, :] = o[:, r * tq:(r + 1) * tq]

    n_back = WINDOW // tq

    def window_tile(update, c):
        kt = i - n_back + c
        s = _dot(kw_ref[0, pl.ds(pl.multiple_of(kt * tq, tq), tq), :], qa_scr[0:128, :])
        if c in (0, n_back):
            kpos = kt * tq + lax.broadcasted_iota(jnp.int32, (tq, n), 0)
            tok = i * tq + (lax.broadcasted_iota(jnp.int32, (tq, n), 1) & (tq - 1))
            s = jnp.where(kpos > tok - WINDOW if c == 0 else kpos <= tok, s, NEG)
        update(s, vw_ref[0, kt], m_scr, acc_scr)

    def window(update):
        @pl.when(i >= n_back)
        def _():
            for c in range(n_back + 1):
                window_tile(update, c)

        @pl.when(i < n_back)
        def _():
            for c in range(n_back + 1):
                pl.when(i - n_back + c >= 0)(functools.partial(window_tile, update, c))

    _flash_bounded_or_online(_logit_bound(qs, kmax_ref[(A_GROUPS + g) * n_batch + b]), window, m_scr, acc_scr)
    o = _flash_result(acc_scr, A_DIM)
    for r in range(A_REP):
        ow_ref[r * A_DIM:(r + 1) * A_DIM, :] = o[:, r * tq:(r + 1) * tq]


def _nsa_sel(k_norm_max, q_t, k_aug, v_slc_tiles, selb, k_win, v_win_tiles, batch, seq, tq, tk):
    nq, nk = seq // tq, seq // tk
    nb = selb.shape[1]
    t = q_t.shape[1]
    gr = A_REP * A_DIM
    n = A_REP * tq
    kw = k_aug.shape[2]
    out = jax.ShapeDtypeStruct((A_HEADS * A_DIM, t), F32)
    grid_spec = pltpu.PrefetchScalarGridSpec(
        num_scalar_prefetch=1,
        grid=(batch, A_GROUPS, nq),
        in_specs=[pl.BlockSpec((gr, tq), lambda b, g, i, km: (g, b * nq + i)),
                  pl.BlockSpec((1, seq, kw), lambda b, g, i, km: (g, b, 0)),
                  pl.BlockSpec((1, nk, A_DIM + ONES_ROWS, tk), lambda b, g, i, km: (g, b, 0, 0)),
                  pl.BlockSpec((1, nb, tq), lambda b, g, i, km: (g, 0, b * nq + i)),
                  pl.BlockSpec((1, seq, 128), lambda b, g, i, km: (g, b, 0)),
                  pl.BlockSpec((1, nq, A_DIM + ONES_ROWS, tq), lambda b, g, i, km: (g, b, 0, 0))],
        out_specs=[pl.BlockSpec((gr, tq), lambda b, g, i, km: (g, b * nq + i)),
                   pl.BlockSpec((gr, tq), lambda b, g, i, km: (g, b * nq + i))],
        scratch_shapes=[pltpu.VMEM((kw, n), BF16), pltpu.VMEM((FLASH_UNROLL, tk, n), F32),
                        pltpu.VMEM((1, n), F32), pltpu.VMEM((A_DIM + ONES_ROWS, n), F32)],
    )
    return pl.pallas_call(
        functools.partial(_nsa_sel_kernel, tq=tq, tk=tk),
        grid_spec=grid_spec,
        out_shape=[out, out],
        compiler_params=_params(3),
        name="nsa_sel",
    )(k_norm_max, q_t, k_aug, v_slc_tiles, selb, k_win, v_win_tiles)


def _mla_prep_kernel(cq_ref, ckv_ref, kr_ref, cos_ref, sin_ref, qlg_ref, kvlg_ref, qg_ref, kg_ref, wuq_ref, wukv_ref,
                     qo_ref, ko_ref, vo_ref, kn_ref):
    cos, sin = cos_ref[...], sin_ref[...]
    tt = cos.shape[1]
    q_all = _dot(wuq_ref[...], (_rms_rows(cq_ref[...]) * qlg_ref[...]).astype(BF16))
    kv_all = _dot(wukv_ref[...], (_rms_rows(ckv_ref[...]) * kvlg_ref[...]).astype(BF16))
    kr = kr_ref[...]
    pad = jnp.zeros((128 - B_QK, tt), F32)
    for h in range(B_HEADS):
        y = _rms_rows(q_all[h * B_QK:(h + 1) * B_QK]) * qg_ref[...]
        y = _rope_rows(y, cos, sin, B_NOPE, B_ROPE // 2) * (B_QK ** -0.5 * LOG2E)
        qo_ref[h] = jnp.concatenate([y, pad], axis=0).astype(BF16)
        base = h * (B_NOPE + B_V)
        k = jnp.concatenate([kv_all[base:base + B_NOPE], kr], axis=0)
        y = _rope_rows(_rms_rows(k) * kg_ref[...], cos, sin, B_NOPE, B_ROPE // 2)
        kn_ref[h:h + 1, :] = jnp.sum(y * y, axis=0, keepdims=True)
        ko_ref[h] = jnp.concatenate([y, pad], axis=0).T.astype(BF16)
        vo_ref[h, 0] = _with_ones(kv_all[base + B_NOPE:base + B_NOPE + B_V])


def _mla_attn_kernel(kmax_ref, q_ref, k_ref, v_ref, o_ref, s_scr, m_scr, acc_scr, *, tq, tk):
    b, h, i = pl.program_id(0), pl.program_id(1), pl.program_id(2)

    def scores(j):
        return _dot(k_ref[0, pl.ds(pl.multiple_of(j * tk, tk), tk), :], q_ref[0])

    def causal(s, j):
        kpos = j * tk + lax.broadcasted_iota(jnp.int32, (tk, tq), 0)
        tok = i * tq + lax.broadcasted_iota(jnp.int32, (tk, tq), 1)
        return jnp.where(kpos <= tok, s, NEG)

    _flash_bounded_or_online(
        _logit_bound(q_ref[0], kmax_ref[h * pl.num_programs(0) + b]),
        lambda update: _flash_causal(scores, lambda j: v_ref[0, j], causal, (i * tq) // tk, max(1, tq // tk),
                                     s_scr, m_scr, acc_scr, update),
        m_scr, acc_scr)
    o_ref[...] = _flash_result(acc_scr, B_V)


def _mla_attn(k_norm_max, q_m, k_m, v_m_tiles, batch, seq, tq, tk):
    nq, nk = seq // tq, seq // tk
    t = q_m.shape[2]
    grid_spec = pltpu.PrefetchScalarGridSpec(
        num_scalar_prefetch=1,
        grid=(batch, B_HEADS, nq),
        in_specs=[pl.BlockSpec((1, 128, tq), lambda b, h, i, km: (h, 0, b * nq + i)),
                  pl.BlockSpec((1, seq, 128), lambda b, h, i, km: (h, b, 0)),
                  pl.BlockSpec((1, nk, B_V + ONES_ROWS, tk), lambda b, h, i, km: (h, b, 0, 0))],
        out_specs=pl.BlockSpec((B_V, tq), lambda b, h, i, km: (h, b * nq + i)),
        scratch_shapes=[pltpu.VMEM((FLASH_UNROLL, tk, tq), F32),
                        pltpu.VMEM((1, tq), F32), pltpu.VMEM((B_V + ONES_ROWS, tq), F32)],
    )
    return pl.pallas_call(
        functools.partial(_mla_attn_kernel, tq=tq, tk=tk),
        grid_spec=grid_spec,
        out_shape=jax.ShapeDtypeStruct((B_HEADS * B_V, t), F32),
        compiler_params=_params(3),
        name="mla_attn",
    )(k_norm_max, q_m, k_m, v_m_tiles)


def _out_proj_kernel(oc_ref, os_ref, ow_ref, gt_ref, ob_ref, x_ref, ga_ref, gb_ref, w_ref, g2_ref, h_ref, hn_ref):
    heads = []
    for h in range(A_HEADS):
        rows = slice(h * A_DIM, (h + 1) * A_DIM)
        heads.append(gt_ref[3 * h:3 * h + 1, :] * oc_ref[rows, :] + gt_ref[3 * h + 1:3 * h + 2, :] * os_ref[rows, :]
                     + gt_ref[3 * h + 2:3 * h + 3, :] * ow_ref[rows, :])
    oa = _rms_rows(jnp.concatenate(heads, axis=0)) * ga_ref[...]
    ob = _rms_rows(ob_ref[...]) * gb_ref[...]
    cat = jnp.concatenate([oa, ob], axis=0).astype(BF16)
    hid = x_ref[...].T + _dot(w_ref[...], cat)
    h_ref[...] = hid
    hn_ref[...] = (_rms_rows(hid) * g2_ref[...]).astype(BF16)


def _out_proj(oc_t, os_t, ow_t, gates_t, ob_t, x2, gain_a, gain_b, w_out_t, gain2, tt):
    t, d = x2.shape
    aw = oc_t.shape[0]
    bw = ob_t.shape[0]
    tok = lambda rows: pl.BlockSpec((rows, tt), lambda i: (0, i))
    full = lambda a: pl.BlockSpec(a.shape, lambda i: (0,) * a.ndim)
    return pl.pallas_call(
        _out_proj_kernel,
        grid=(t // tt,),
        in_specs=[tok(aw), tok(aw), tok(aw), tok(GATE_ROWS), tok(bw), pl.BlockSpec((tt, d), lambda i: (i, 0)),
                  full(gain_a), full(gain_b), full(w_out_t), full(gain2)],
        out_specs=[tok(d), tok(d)],
        out_shape=[jax.ShapeDtypeStruct((d, t), F32), jax.ShapeDtypeStruct((d, t), BF16)],
        compiler_params=_params(1),
        name="out_proj",
    )(oc_t, os_t, ow_t, gates_t, ob_t, x2, gain_a, gain_b, w_out_t, gain2)


def _top_ranked(s):
    n, rest = s.shape[0], s.shape[1:]
    row = lax.broadcasted_iota(jnp.int32, s.shape, 0).astype(F32)
    slot = lax.broadcasted_iota(jnp.int32, (P_TOPK,) + rest, 0)

    def body(a, carry):
        v, rank, vals = carry
        mx = jnp.max(v, axis=0, keepdims=True)
        first = jnp.min(jnp.where(v == mx, row, float(n)), axis=0, keepdims=True)
        hit = row == first
        rank = jnp.where(hit, jnp.asarray(a, F32), rank)
        v = jnp.where(hit, -jnp.inf, v)
        vals = jnp.where(slot == a, mx, vals)
        return v, rank, vals

    _, rank, vals = lax.fori_loop(0, P_TOPK, body,
                                  (s, jnp.full(s.shape, float(P_TOPK), F32), jnp.zeros((P_TOPK,) + rest, F32)))
    return rank, vals


def _pair_counts(v1, v2):
    k = v1.shape[0]
    slot = lax.broadcasted_iota(jnp.int32, v1.shape, 0).astype(F32)
    top = v1[0:1] + v2[0:1]

    def body(_, carry):
        count, front, z = carry
        mx = jnp.max(front, axis=0, keepdims=True)
        a_star = jnp.min(jnp.where(front == mx, slot, float(k)), axis=0, keepdims=True)
        hit = slot == a_star
        count = count + jnp.where(hit, 1.0, 0.0)
        nxt = jnp.sum(jnp.where(hit, count, 0.0), axis=0, keepdims=True)
        v2_nxt = jnp.sum(jnp.where(slot == nxt, v2, 0.0), axis=0, keepdims=True)
        front = jnp.where(hit, jnp.where(nxt < float(k), v1 + v2_nxt, -jnp.inf), front)
        return count, front, z + jnp.exp(mx - top)

    count, _, z = lax.fori_loop(0, k, body, (jnp.zeros(v1.shape, F32), v1 + v2[0:1], jnp.zeros(top.shape, F32)))
    return count, z


REMOVED = -2.0 ** 126
LANES = 128


def _top_ranked_pair_fast(s1, s2):
    n, tp = s1.shape
    slot = lax.broadcasted_iota(jnp.int32, (P_TOPK, tp), 0)

    def body(a, carry):
        code = REMOVED * (1.0 + jnp.asarray(a, F32) * (1.0 / 32.0))
        out = []
        for key, vals in (carry[0:2], carry[2:4]):
            mx = jnp.max(key, axis=0, keepdims=True)
            key = jnp.where(key == mx, code, key)
            out += [key, jnp.where(slot == a, mx, vals)]
        return tuple(out)

    zeros = jnp.zeros((P_TOPK, tp), F32)
    k1, t1, k2, t2 = lax.fori_loop(0, P_TOPK, body, (s1, zeros, s2, zeros))
    res, bad = [], jnp.zeros((1, tp), F32)
    for key, vals in ((k1, t1), (k2, t2)):
        removed = key <= REMOVED
        rank = jnp.where(removed, (key * (1.0 / REMOVED) - 1.0) * 32.0, float(P_TOPK))
        n_removed = jnp.sum(jnp.where(removed, 1.0, 0.0), axis=0, keepdims=True)
        bad = jnp.maximum(bad, jnp.abs(n_removed - float(P_TOPK)))
        res += [rank, vals]
    return res[0], res[1], res[2], res[3], bad


def _peer_route_kernel(hn_ref, wq_ref, keys_ref, cut_ref, g1_ref, r2_ref, g2_ref, q_scr, s_scr, rank_scr, vals_scr):
    tp = hn_ref.shape[1]
    half = P_KEY_DIM // 2
    q_scr[...] = _dot(wq_ref[...], hn_ref[...])
    for h in range(P_HEADS):
        q = _rms_rows(q_scr[h * P_KEY_DIM:(h + 1) * P_KEY_DIM, :]).astype(BF16)
        s_scr[0, h] = _dot(keys_ref[h, 0], q[:half])
        s_scr[1, h] = _dot(keys_ref[h, 1], q[half:])

    def first_level(rank_pair, flagged):
        for h in range(P_HEADS):
            for lt in range(tp // LANES):
                lanes = slice(lt * LANES, (lt + 1) * LANES)
                rank1, vals1, rank2, vals2, bad = rank_pair(s_scr[0, h, :, lanes], s_scr[1, h, :, lanes])
                flagged = jnp.maximum(flagged, bad)
                rank_scr[0, h, :, lanes] = rank1
                rank_scr[1, h, :, lanes] = rank2
                for a in range(P_TOPK):
                    vals_scr[0, a, h:h + 1, lanes] = vals1[a:a + 1]
                    vals_scr[1, a, h:h + 1, lanes] = vals2[a:a + 1]
        return flagged

    flagged = first_level(_top_ranked_pair_fast, jnp.zeros((1, LANES), F32))

    @pl.when(jnp.max(flagged) > 0.0)
    def _():
        first_level(lambda s1, s2: (*_top_ranked(s1), *_top_ranked(s2), jnp.zeros((1, LANES), F32)), flagged)

    count, z = _pair_counts(vals_scr[0], vals_scr[1])
    for h in range(P_HEADS):
        rank1 = rank_scr[0, h]
        cut = jnp.zeros_like(rank1)
        for a in range(P_TOPK):
            cut = jnp.where(rank1 == float(a), count[a, h:h + 1, :], cut)
        cut_ref[h] = cut.astype(BF16)
        g1_ref[h] = jnp.exp(s_scr[0, h] - vals_scr[0, 0, h:h + 1, :]).astype(BF16)
        r2_ref[h] = rank_scr[1, h].astype(BF16)
        g2_ref[h] = (jnp.exp(s_scr[1, h] - vals_scr[1, 0, h:h + 1, :]) * (1.0 / z[0, h:h + 1, :])).astype(BF16)


def _peer_route(hn_t, wq_t, sub_keys, tp):
    d, t = hn_t.shape
    halfs = jax.ShapeDtypeStruct((P_HEADS, N_KEYS, t), BF16)
    ospec = pl.BlockSpec((P_HEADS, N_KEYS, tp), lambda i: (0, 0, i))
    return pl.pallas_call(
        _peer_route_kernel,
        grid=(t // tp,),
        in_specs=[pl.BlockSpec((d, tp), lambda i: (0, i)),
                  pl.BlockSpec(wq_t.shape, lambda i: (0, 0)),
                  pl.BlockSpec(sub_keys.shape, lambda i: (0, 0, 0, 0))],
        out_specs=[ospec, ospec, ospec, ospec],
        out_shape=[halfs, halfs, halfs, halfs],
        scratch_shapes=[pltpu.VMEM((P_HEADS * P_KEY_DIM, tp), F32), pltpu.VMEM((2, P_HEADS, N_KEYS, tp), F32),
                        pltpu.VMEM((2, P_HEADS, N_KEYS, tp), F32), pltpu.VMEM((2, P_TOPK, P_HEADS, tp), F32)],
        compiler_params=_params(1),
        name="peer_route",
    )(hn_t, wq_t, sub_keys)


def _peer_ffn_kernel(hn_ref, h_ref, u_ref, v_ref, cut_ref, g1_ref, r2_ref, g2_ref, o_ref, acc_scr, *, te):
    e = pl.program_id(1)

    @pl.when(e == 0)
    def _():
        acc_scr[...] = jnp.zeros(acc_scr.shape, F32)

    hn = hn_ref[...]
    tt = hn.shape[1]

    def rows_bf16(ref, h, ii):
        return jnp.broadcast_to(ref[h, ii:ii + 1, :], (N_KEYS, tt))

    chunk = N_KEYS
    weights = []
    for c in range(te // chunk):
        a = _dot(u_ref[c * chunk:(c + 1) * chunk, :], hn)
        for k in range(chunk // N_KEYS):
            ii = c * (chunk // N_KEYS) + k
            gate = None
            for h in range(P_HEADS):
                chosen = r2_ref[h] < rows_bf16(cut_ref, h, ii)
                term = jnp.where(chosen, g2_ref[h], jnp.zeros((N_KEYS, tt), BF16)) * rows_bf16(g1_ref, h, ii)
                gate = term if gate is None else gate + term
            weights.append(gate * _gelu_tanh(a[k * N_KEYS:(k + 1) * N_KEYS, :]).astype(BF16))
    acc_scr[...] += _dot(v_ref[...], jnp.concatenate(weights, axis=0))

    @pl.when(e == pl.num_programs(1) - 1)
    def _():
        o_ref[...] = (h_ref[...] + acc_scr[...]).T


def _peer_ffn(hn_t, h_t, u_bf, v_t_bf, cut_k, g1_k, r2, g2, tt, te):
    d, t = hn_t.shape
    n_exp = u_bf.shape[0]
    kpe = te // N_KEYS
    return pl.pallas_call(
        functools.partial(_peer_ffn_kernel, te=te),
        grid=(t // tt, n_exp // te),
        in_specs=[pl.BlockSpec((d, tt), lambda i, e: (0, i)),
                  pl.BlockSpec((d, tt), lambda i, e: (0, i)),
                  pl.BlockSpec((te, d), lambda i, e: (e, 0)),
                  pl.BlockSpec((d, te), lambda i, e: (0, e)),
                  pl.BlockSpec((P_HEADS, kpe, tt), lambda i, e: (0, e, i)),
                  pl.BlockSpec((P_HEADS, kpe, tt), lambda i, e: (0, e, i)),
                  pl.BlockSpec((P_HEADS, N_KEYS, tt), lambda i, e: (0, 0, i)),
                  pl.BlockSpec((P_HEADS, N_KEYS, tt), lambda i, e: (0, 0, i))],
        out_specs=pl.BlockSpec((tt, d), lambda i, e: (i, 0)),
        out_shape=jax.ShapeDtypeStruct((t, d), F32),
        scratch_shapes=[pltpu.VMEM((d, tt), F32)],
        compiler_params=_params(2),
        name="peer_ffn",
    )(hn_t, h_t, u_bf, v_t_bf, cut_k, g1_k, r2, g2)


def _rope_tables_t(pos_flat, rot_dim):
    inv_freq = ROPE_THETA ** (-jnp.arange(0, rot_dim, 2, dtype=F32) / rot_dim)
    ang = pos_flat.astype(F32)[None, :] * inv_freq[:, None]
    return jnp.cos(ang), jnp.sin(ang)


def _expand_cmp_w1(w1):
    w = w1.reshape(CMP_LEN, A_DIM, CMP_HIDDEN)
    out = []
    for part in (w[:CMP_STRIDE], w[CMP_STRIDE:]):
        z = jnp.zeros_like(part)
        both = jnp.stack([jnp.concatenate([part, z], axis=1), jnp.concatenate([z, part], axis=1)])
        out.append(both.reshape(A_GROUPS, CMP_STRIDE * A_GROUPS * A_DIM, CMP_HIDDEN).transpose(0, 2, 1).astype(BF16))
    return out


TOKEN_TILE = 512
NSA_Q_TILE = 256
NSA_CMP_Q_TILE = 512
NSA_SEL_K_TILE = 512
MLA_Q_TILE = 1024
MLA_K_TILE = 512
PEER_ROUTE_TILE = 256
PEER_EXPERT_TILE = 2048


def _col(v):
    return v.reshape(-1, 1).astype(F32)


def _mixers(x, positions, norm1_gain, w_in, nsa_q_gain, nsa_kc_gain, nsa_ks_gain, nsa_kw_gain,
            cmp_pos, cmp_k_w1, cmp_k_w2, cmp_v_w1, cmp_v_w2,
            mla_q_lora_gain, mla_w_uq, mla_kv_lora_gain, mla_w_ukv, mla_q_gain, mla_k_gain):
    batch, seq, d = x.shape
    t = batch * seq
    tt = TOKEN_TILE
    tq_nsa = NSA_Q_TILE
    tk_sel = NSA_SEL_K_TILE
    tq_mla, tk_mla = MLA_Q_TILE, MLA_K_TILE
    assert tk_sel == tt and tk_mla == tt
    assert d == D_MODEL and seq % 512 == 0 and seq // SLC_LEN >= SLC_TOPK and WINDOW % tq_nsa == 0
    col = _col
    x2 = x.reshape(t, d)
    pos = positions.reshape(t)

    w_in_t = w_in.T
    gate_lo = sum((512, 128, 128, 128, 128, 128, 128))
    gate_hi = gate_lo + 3 * A_HEADS
    w_in_t = jnp.concatenate([w_in_t[:gate_lo], w_in_t[gate_hi:], w_in_t[gate_lo:gate_hi],
                              jnp.zeros((PROJ_ROWS - w_in_t.shape[0], d), F32)], axis=0).astype(BF16)
    cos_a, sin_a = _rope_tables_t(pos, A_ROPE)
    cos_b, sin_b = _rope_tables_t(pos, B_ROPE)
    q_t, kc_tm, vc_tm, k_slc, v_slc_t, k_win, v_win_t, gates_t, q_m, k_m, v_m_t, kn_a, kn_b = _in_proj(
        x2, norm1_gain.reshape(1, d), w_in_t, cos_a, sin_a, col(nsa_q_gain), col(nsa_ks_gain), col(nsa_kw_gain),
        cos_b, sin_b, col(mla_q_lora_gain), col(mla_kv_lora_gain), col(mla_q_gain), col(mla_k_gain),
        mla_w_uq.T.astype(BF16), mla_w_ukv.T.astype(BF16), tt, seq, tq_nsa)

    nc = seq // CMP_STRIDE
    chunk_w = CMP_STRIDE * A_GROUPS * A_DIM
    w1ka, w1kb = _expand_cmp_w1(cmp_k_w1)
    w1va, w1vb = _expand_cmp_w1(cmp_v_w1)
    pos_rows = lambda p: jnp.broadcast_to(p[:, None, :], (CMP_STRIDE, A_GROUPS, A_DIM)).reshape(1, chunk_w)
    cmp_end = jnp.minimum(jnp.arange(nc) * CMP_STRIDE + CMP_LEN - 1, seq - 1)
    cos_c, sin_c = _rope_tables_t(positions[:, cmp_end].reshape(-1), A_ROPE)
    to_b = lambda a: a.reshape(A_ROPE // 2, batch, nc).transpose(1, 0, 2)
    kcmp, vcmp_t = _compress(kc_tm.reshape(batch, nc, chunk_w), vc_tm.reshape(batch, nc, chunk_w),
                             w1ka, w1kb, w1va, w1vb, pos_rows(cmp_pos[:CMP_STRIDE]), pos_rows(cmp_pos[CMP_STRIDE:]),
                             cmp_k_w2.T.astype(BF16), cmp_v_w2.T.astype(BF16), col(nsa_kc_gain), to_b(cos_c), to_b(sin_c))

    n_cmp = (seq - CMP_LEN) // CMP_STRIDE + 1
    nb = seq // SLC_LEN
    c_start = np.arange(nc)[None, :] * CMP_STRIDE
    s_start = np.arange(nb)[:, None] * SLC_LEN
    ov = (c_start < s_start + SLC_LEN) & (c_start + CMP_LEN - 1 >= s_start) & (np.arange(nc)[None, :] < n_cmp)
    ov_t = jnp.asarray(ov.astype(np.float32)).astype(BF16)

    oc_t, selb = _nsa_cmp(q_t, kcmp, vcmp_t, ov_t, batch, seq, NSA_CMP_Q_TILE)
    norm_max = lambda kn: jnp.sqrt(jnp.max(kn.reshape(KEY_NORM_ROWS, batch, seq), axis=2)).reshape(-1)
    os_t, ow_t = _nsa_sel(norm_max(kn_a), q_t, k_slc, v_slc_t, selb, k_win, v_win_t, batch, seq, tq_nsa, tk_sel)

    ob_t = _mla_attn(norm_max(kn_b), q_m, k_m, v_m_t, batch, seq, tq_mla, tk_mla)
    return oc_t, os_t, ow_t, gates_t, ob_t


def _peer(hn_t, h_t, peer_w_q, peer_sub_keys, peer_u, peer_v):
    cut, g1, r2, g2 = _peer_route(hn_t, peer_w_q.T.astype(BF16), peer_sub_keys.astype(BF16), PEER_ROUTE_TILE)
    return _peer_ffn(hn_t, h_t, peer_u.astype(BF16), peer_v.T.astype(BF16),
                     cut, g1, r2, g2, TOKEN_TILE, PEER_EXPERT_TILE)


def _layer(x, positions, norm1_gain, w_in, nsa_q_gain, nsa_kc_gain, nsa_ks_gain, nsa_kw_gain,
           cmp_pos, cmp_k_w1, cmp_k_w2, cmp_v_w1, cmp_v_w2,
           mla_q_lora_gain, mla_w_uq, mla_kv_lora_gain, mla_w_ukv, mla_q_gain, mla_k_gain,
           out_gain_a, out_gain_b, w_out, norm2_gain, peer_w_q, peer_sub_keys, peer_u, peer_v):
    batch, seq, d = x.shape
    oc_t, os_t, ow_t, gates_t, ob_t = _mixers(
        x, positions, norm1_gain, w_in, nsa_q_gain, nsa_kc_gain, nsa_ks_gain, nsa_kw_gain,
        cmp_pos, cmp_k_w1, cmp_k_w2, cmp_v_w1, cmp_v_w2,
        mla_q_lora_gain, mla_w_uq, mla_kv_lora_gain, mla_w_ukv, mla_q_gain, mla_k_gain)
    h_t, hn_t = _out_proj(oc_t, os_t, ow_t, gates_t, ob_t, x.reshape(batch * seq, d), _col(out_gain_a), _col(out_gain_b),
                          w_out.T.astype(BF16), _col(norm2_gain), TOKEN_TILE)
    return _peer(hn_t, h_t, peer_w_q, peer_sub_keys, peer_u, peer_v).reshape(batch, seq, d)


def kernel(x, positions, norm1_gain, w_in, nsa_q_gain, nsa_kc_gain, nsa_ks_gain, nsa_kw_gain, cmp_pos, cmp_k_w1, cmp_k_w2, cmp_v_w1, cmp_v_w2, mla_q_lora_gain, mla_w_uq, mla_kv_lora_gain, mla_w_ukv, mla_q_gain, mla_k_gain, out_gain_a, out_gain_b, w_out, norm2_gain, peer_w_q, peer_sub_keys, peer_u, peer_v):
    h = x
    for l in range(norm1_gain.shape[0]):
        h = _layer(h, positions, norm1_gain[l], w_in[l], nsa_q_gain[l], nsa_kc_gain[l], nsa_ks_gain[l], nsa_kw_gain[l],
                   cmp_pos[l], cmp_k_w1[l], cmp_k_w2[l], cmp_v_w1[l], cmp_v_w2[l],
                   mla_q_lora_gain[l], mla_w_uq[l], mla_kv_lora_gain[l], mla_w_ukv[l], mla_q_gain[l], mla_k_gain[l],
                   out_gain_a[l], out_gain_b[l], w_out[l], norm2_gain[l], peer_w_q[l], peer_sub_keys[l],
                   peer_u[l], peer_v[l])
    return h
```

```python
import functools

import jax
import jax.numpy as jnp
import numpy as np
from jax import lax
from jax.experimental import pallas as pl
from jax.experimental.pallas import tpu as pltpu

F32, BF16 = jnp.float32, jnp.bfloat16
EPS = 1e-6
NEG = -1e30
FORCE = 1e9
ROPE_THETA = 500000.0
LOG2E = 1.4426950408889634

D_MODEL = 1024
A_HEADS, A_GROUPS, A_DIM = 8, 2, 64
A_REP = A_HEADS // A_GROUPS
A_ROPE = A_DIM // 4
CMP_LEN, CMP_STRIDE, CMP_HIDDEN = 32, 16, 256
SLC_LEN, SLC_TOPK, WINDOW = 64, 16, 512
B_HEADS, Q_LORA, KV_LORA, B_NOPE, B_ROPE, B_V = 8, 256, 128, 64, 32, 64
B_QK = B_NOPE + B_ROPE
P_HEADS, N_KEYS, P_KEY_DIM, P_TOPK = 8, 128, 256, 16
N_EXPERTS = N_KEYS * N_KEYS

ROW_Q, ROW_KC, ROW_VC, ROW_KS, ROW_VS, ROW_KW, ROW_VW = 0, 512, 640, 768, 896, 1024, 1152
ROW_CQ, ROW_CKV, ROW_KR, ROW_GATE, PROJ_ROWS = 1280, 1536, 1664, 1696, 1728
GATE_ROWS = 32
KEY_NORM_ROWS = 8

VMEM_LIMIT = 56 * 1024 * 1024
NT_DIMS = (((1,), (1,)), ((), ()))


def _params(n_axes):
    return pltpu.CompilerParams(dimension_semantics=("arbitrary",) * n_axes, vmem_limit_bytes=VMEM_LIMIT)


def _dot(a, b):
    return jnp.dot(a, b, preferred_element_type=F32)


def _dot_nt(a, b):
    return lax.dot_general(a, b, NT_DIMS, preferred_element_type=F32)


def _row_sumsq(x):
    sq = x * x
    hi = sq.astype(BF16)
    lo = (sq - hi.astype(F32)).astype(BF16)
    ones = jnp.ones((8, x.shape[1]), BF16)
    return (_dot_nt(ones, hi) + _dot_nt(ones, lo))[0:1, :]


def _rms_rows(x):
    ss = jnp.sum(x * x, axis=0, keepdims=True)
    return x * lax.rsqrt(ss * (1.0 / x.shape[0]) + EPS)


def _rope_rows(y, cos, sin, off, half):
    x1, x2 = y[off:off + half], y[off + half:off + 2 * half]
    parts = [y[:off]] if off else []
    parts += [x1 * cos - x2 * sin, x2 * cos + x1 * sin]
    if off + 2 * half < y.shape[0]:
        parts.append(y[off + 2 * half:])
    return jnp.concatenate(parts, axis=0)


def _gelu_tanh(x):
    k = -2.0 * 0.7978845608028654 * LOG2E
    return x / (1.0 + jnp.exp2(x * (k + (k * 0.044715) * (x * x))))


def _in_proj_kernel(x_ref, g_ref, w_ref, cos_a_ref, sin_a_ref, qg_ref, ksg_ref, kwg_ref,
                    cos_b_ref, sin_b_ref, qlg_ref, kvlg_ref, mqg_ref, mkg_ref, wuq_ref, wukv_ref, *out_refs, seq, tw):
    x = x_ref[...]
    xg = (x * g_ref[...]).astype(BF16)
    p = _dot_nt(w_ref[...], xg)
    p = p * lax.rsqrt(_row_sumsq(x) * (1.0 / x.shape[1]) + EPS)
    gw = A_GROUPS * A_DIM
    rows = lambda start, n: p[start:start + n]
    _nsa_prep_kernel(rows(ROW_Q, A_HEADS * A_DIM), rows(ROW_KC, gw), rows(ROW_VC, gw), rows(ROW_KS, gw), rows(ROW_VS, gw),
                     rows(ROW_KW, gw), rows(ROW_VW, gw), rows(ROW_GATE, GATE_ROWS), cos_a_ref, sin_a_ref,
                     qg_ref, ksg_ref, kwg_ref, *out_refs[:8], out_refs[11], seq=seq, tw=tw)
    _mla_prep_kernel(rows(ROW_CQ, Q_LORA), rows(ROW_CKV, KV_LORA), rows(ROW_KR, B_ROPE), cos_b_ref, sin_b_ref,
                     qlg_ref, kvlg_ref, mqg_ref, mkg_ref, wuq_ref, wukv_ref, *out_refs[8:11], out_refs[12])


def _in_proj(x2, gain, w_t, cos_a, sin_a, q_gain, ks_gain, kw_gain, cos_b, sin_b, q_lora_gain, kv_lora_gain,
             mq_gain, mk_gain, wuq_t, wukv_t, tt, seq, tw):
    t, d = x2.shape
    gw = A_GROUPS * A_DIM
    kw = 128 + seq // SLC_LEN
    va, vb = A_DIM + ONES_ROWS, B_V + ONES_ROWS
    full = lambda a: pl.BlockSpec(a.shape, lambda i: (0,) * a.ndim)
    lanes = lambda n: pl.BlockSpec((n, tt), lambda i: (0, i))
    return pl.pallas_call(
        functools.partial(_in_proj_kernel, seq=seq, tw=tw),
        grid=(t // tt,),
        in_specs=[pl.BlockSpec((tt, d), lambda i: (i, 0)), full(gain), full(w_t),
                  lanes(A_ROPE // 2), lanes(A_ROPE // 2), full(q_gain), full(ks_gain), full(kw_gain),
                  lanes(B_ROPE // 2), lanes(B_ROPE // 2), full(q_lora_gain), full(kv_lora_gain), full(mq_gain),
                  full(mk_gain), full(wuq_t), full(wukv_t)],
        out_specs=[lanes(A_HEADS * A_DIM),
                   pl.BlockSpec((tt, gw), lambda i: (i, 0)),
                   pl.BlockSpec((tt, gw), lambda i: (i, 0)),
                   pl.BlockSpec((A_GROUPS, tt, kw), lambda i: (0, i, 0)),
                   pl.BlockSpec((A_GROUPS, 1, va, tt), lambda i: (0, i, 0, 0)),
                   pl.BlockSpec((A_GROUPS, tt, 128), lambda i: (0, i, 0)),
                   pl.BlockSpec((A_GROUPS, tt // tw, va, tw), lambda i: (0, i, 0, 0)),
                   lanes(GATE_ROWS),
                   pl.BlockSpec((B_HEADS, 128, tt), lambda i: (0, 0, i)),
                   pl.BlockSpec((B_HEADS, tt, 128), lambda i: (0, i, 0)),
                   pl.BlockSpec((B_HEADS, 1, vb, tt), lambda i: (0, i, 0, 0)),
                   lanes(KEY_NORM_ROWS), lanes(KEY_NORM_ROWS)],
        out_shape=[jax.ShapeDtypeStruct((A_HEADS * A_DIM, t), BF16),
                   jax.ShapeDtypeStruct((t, gw), F32),
                   jax.ShapeDtypeStruct((t, gw), F32),
                   jax.ShapeDtypeStruct((A_GROUPS, t, kw), BF16),
                   jax.ShapeDtypeStruct((A_GROUPS, t // tt, va, tt), BF16),
                   jax.ShapeDtypeStruct((A_GROUPS, t, 128), BF16),
                   jax.ShapeDtypeStruct((A_GROUPS, t // tw, va, tw), BF16),
                   jax.ShapeDtypeStruct((GATE_ROWS, t), F32),
                   jax.ShapeDtypeStruct((B_HEADS, 128, t), BF16),
                   jax.ShapeDtypeStruct((B_HEADS, t, 128), BF16),
                   jax.ShapeDtypeStruct((B_HEADS, t // tt, vb, tt), BF16),
                   jax.ShapeDtypeStruct((KEY_NORM_ROWS, t), F32),
                   jax.ShapeDtypeStruct((KEY_NORM_ROWS, t), F32)],
        compiler_params=_params(1),
        name="in_proj",
    )(x2, gain, w_t, cos_a, sin_a, q_gain, ks_gain, kw_gain, cos_b, sin_b, q_lora_gain, kv_lora_gain,
      mq_gain, mk_gain, wuq_t, wukv_t)


def _nsa_prep_kernel(q_ref, kc_ref, vc_ref, ks_ref, vs_ref, kw_ref, vw_ref, gt_ref, cos_ref, sin_ref,
                     qg_ref, ksg_ref, kwg_ref,
                     qo_ref, kco_ref, vco_ref, kso_ref, vso_ref, kwo_ref, vwo_ref, gto_ref, kn_ref, *, seq, tw):
    cos, sin = cos_ref[...], sin_ref[...]
    tt = cos.shape[1]
    nb = seq // SLC_LEN
    for h in range(A_HEADS):
        y = _rms_rows(q_ref[h * A_DIM:(h + 1) * A_DIM, :]) * qg_ref[...]
        y = _rope_rows(y, cos, sin, 0, A_ROPE // 2) * (A_DIM ** -0.5 * LOG2E)
        qo_ref[h * A_DIM:(h + 1) * A_DIM, :] = y.astype(BF16)
    kco_ref[...] = kc_ref[...].T
    vco_ref[...] = vc_ref[...].T
    zeros = jnp.zeros((A_DIM, tt), F32)
    tok = pl.program_id(0) * tt + lax.broadcasted_iota(jnp.int32, (tt, nb), 0)
    block_hot = jnp.where(lax.broadcasted_iota(jnp.int32, (tt, nb), 1) == (tok % seq) // SLC_LEN, 1.0, 0.0)
    kn_ref[...] = jnp.zeros(kn_ref.shape, F32)
    for g in range(A_GROUPS):
        for branch, (src, gain, dst) in enumerate(((ks_ref, ksg_ref, kso_ref), (kw_ref, kwg_ref, kwo_ref))):
            y = _rms_rows(src[g * A_DIM:(g + 1) * A_DIM, :]) * gain[...]
            y = _rope_rows(y, cos, sin, 0, A_ROPE // 2)
            row = branch * A_GROUPS + g
            kn_ref[row:row + 1, :] = jnp.sum(y * y, axis=0, keepdims=True)
            k_tm = jnp.concatenate([y, zeros], axis=0).T
            if dst is kso_ref:
                k_tm = jnp.concatenate([k_tm, block_hot], axis=1)
            dst[g] = k_tm.astype(BF16)
        vso_ref[g, 0] = _with_ones(vs_ref[g * A_DIM:(g + 1) * A_DIM, :])
        v_win = _with_ones(vw_ref[g * A_DIM:(g + 1) * A_DIM, :])
        for c in range(tt // tw):
            vwo_ref[g, c] = v_win[:, c * tw:(c + 1) * tw]
    gto_ref[...] = 1.0 / (1.0 + jnp.exp(-gt_ref[...]))


def _compress_kernel(kc_ref, vc_ref, w1ka_ref, w1kb_ref, w1va_ref, w1vb_ref, plo_ref, phi_ref,
                     w2k_ref, w2v_ref, kg_ref, cos_ref, sin_ref, ko_ref, vo_ref):
    nc = kc_ref.shape[1]
    zeros = jnp.zeros((A_DIM, nc), F32)
    for src, w1a, w1b, w2, is_k in ((kc_ref, w1ka_ref, w1kb_ref, w2k_ref, True),
                                    (vc_ref, w1va_ref, w1vb_ref, w2v_ref, False)):
        x = src[0]
        xlo = (x + plo_ref[...]).astype(BF16)
        xhi = (x + phi_ref[...]).astype(BF16)
        for g in range(A_GROUPS):
            first = _dot_nt(w1a[g], xlo)
            second = _dot_nt(w1b[g], xhi)
            hid = _gelu_tanh(first + pltpu.roll(second, nc - 1, axis=1)).astype(BF16)
            c = _dot(w2[...], hid)
            if is_k:
                y = _rope_rows(_rms_rows(c) * kg_ref[...], cos_ref[0], sin_ref[0], 0, A_ROPE // 2)
                ko_ref[0, g] = jnp.concatenate([y, zeros], axis=0).T.astype(BF16)
            else:
                vo_ref[0, g] = _with_ones(c)


def _compress(kc_chunks, vc_chunks, w1ka, w1kb, w1va, w1vb, plo, phi, w2k_t, w2v_t, kc_gain, cos_c, sin_c):
    b, nc, cw = kc_chunks.shape
    full = lambda a: pl.BlockSpec(a.shape, lambda i: (0,) * a.ndim)
    return pl.pallas_call(
        _compress_kernel,
        grid=(b,),
        in_specs=[pl.BlockSpec((1, nc, cw), lambda i: (i, 0, 0)), pl.BlockSpec((1, nc, cw), lambda i: (i, 0, 0)),
                  full(w1ka), full(w1kb), full(w1va), full(w1vb), full(plo), full(phi), full(w2k_t), full(w2v_t),
                  full(kc_gain),
                  pl.BlockSpec((1, A_ROPE // 2, nc), lambda i: (i, 0, 0)),
                  pl.BlockSpec((1, A_ROPE // 2, nc), lambda i: (i, 0, 0))],
        out_specs=[pl.BlockSpec((1, A_GROUPS, nc, 128), lambda i: (i, 0, 0, 0)),
                   pl.BlockSpec((1, A_GROUPS, A_DIM + ONES_ROWS, nc), lambda i: (i, 0, 0, 0))],
        out_shape=[jax.ShapeDtypeStruct((b, A_GROUPS, nc, 128), BF16),
                   jax.ShapeDtypeStruct((b, A_GROUPS, A_DIM + ONES_ROWS, nc), BF16)],
        compiler_params=_params(1),
        name="nsa_compress",
    )(kc_chunks, vc_chunks, w1ka, w1kb, w1va, w1vb, plo, phi, w2k_t, w2v_t, kc_gain, cos_c, sin_c)


def _stack_heads(q, tq):
    qs = jnp.concatenate([q[r * A_DIM:(r + 1) * A_DIM, :] for r in range(A_REP)], axis=1)
    return jnp.concatenate([qs, jnp.zeros_like(qs)], axis=0)


def _nsa_cmp_kernel(q_ref, k_ref, v_ref, ov_ref, o_ref, sb_ref, *, tq):
    i = pl.program_id(2)
    n = A_REP * tq
    nc = k_ref.shape[2]
    nb = ov_ref.shape[0]
    qp = _stack_heads(q_ref[...], tq)
    s = _dot(k_ref[0, 0], qp)
    cmp_end = lax.broadcasted_iota(jnp.int32, (nc, n), 0) * CMP_STRIDE + (CMP_LEN - 1)
    tok = i * tq + (lax.broadcasted_iota(jnp.int32, (nc, n), 1) & (tq - 1))
    s = jnp.where(cmp_end <= tok, s, NEG)
    m = jnp.max(s, axis=0, keepdims=True)
    p = jnp.exp2(s - jnp.where(m > 0.5 * NEG, m, 0.0)).astype(BF16)
    ocl = _dot(v_ref[0, 0], p)
    l = ocl[A_DIM:A_DIM + 1]
    inv = jnp.where(l > 0.0, 1.0 / l, 0.0)
    oc = ocl[:A_DIM] * inv
    for r in range(A_REP):
        o_ref[r * A_DIM:(r + 1) * A_DIM, :] = oc[:, r * tq:(r + 1) * tq]
    imp4 = _dot(ov_ref[...], p) * inv
    imp = imp4[:, 0:tq]
    for r in range(1, A_REP):
        imp = imp + imp4[:, r * tq:(r + 1) * tq]

    blk = lax.broadcasted_iota(jnp.int32, (nb, tq), 0)
    t = i * tq + lax.broadcasted_iota(jnp.int32, (nb, tq), 1)
    forced = (blk == t // SLC_LEN) | (blk == 0)
    v0 = jnp.where(forced, FORCE, jnp.where(blk * SLC_LEN <= t, imp, NEG))
    topk = min(SLC_TOPK, nb)

    v = jnp.where(blk == t // SLC_LEN, 2.0 * FORCE, v0)
    for _ in range(topk):
        v = jnp.where(v == jnp.max(v, axis=0, keepdims=True), -jnp.inf, v)
    taken = v == -jnp.inf
    sb_ref[0] = jnp.where(taken, 0.0, NEG)
    n_taken = jnp.sum(jnp.where(taken, 1.0, 0.0), axis=0, keepdims=True)

    @pl.when(jnp.max(jnp.abs(n_taken - float(topk))) > 0.0)
    def _():
        blk_f = blk.astype(F32)
        w = v0
        sel = jnp.zeros((nb, tq), F32)
        for _ in range(topk):
            mx = jnp.max(w, axis=0, keepdims=True)
            first = jnp.min(jnp.where(w == mx, blk_f, float(nb)), axis=0, keepdims=True)
            hit = blk_f == first
            sel = jnp.where(hit, 1.0, sel)
            w = jnp.where(hit, -jnp.inf, w)
        sb_ref[0] = jnp.where(sel > 0.0, 0.0, NEG)


def _nsa_cmp(q_t, kcmp, vcmp_t, ov_t, batch, seq, tq):
    nq = seq // tq
    nc = kcmp.shape[2]
    nb = ov_t.shape[0]
    t = q_t.shape[1]
    gr = A_REP * A_DIM
    return pl.pallas_call(
        functools.partial(_nsa_cmp_kernel, tq=tq),
        grid=(batch, A_GROUPS, nq),
        in_specs=[pl.BlockSpec((gr, tq), lambda b, g, i: (g, b * nq + i)),
                  pl.BlockSpec((1, 1, nc, 128), lambda b, g, i: (b, g, 0, 0)),
                  pl.BlockSpec((1, 1, A_DIM + ONES_ROWS, nc), lambda b, g, i: (b, g, 0, 0)),
                  pl.BlockSpec((nb, nc), lambda b, g, i: (0, 0))],
        out_specs=[pl.BlockSpec((gr, tq), lambda b, g, i: (g, b * nq + i)),
                   pl.BlockSpec((1, nb, tq), lambda b, g, i: (g, 0, b * nq + i))],
        out_shape=[jax.ShapeDtypeStruct((A_HEADS * A_DIM, t), F32),
                   jax.ShapeDtypeStruct((A_GROUPS, nb, t), F32)],
        compiler_params=_params(3),
        name="nsa_cmp",
    )(q_t, kcmp, vcmp_t, ov_t)


ONES_ROWS = 16
FLASH_UNROLL = 4


def _with_ones(v):
    return jnp.concatenate([v, jnp.ones((ONES_ROWS, v.shape[1]), F32)], axis=0).astype(BF16)


def _flash_update(s, v_t, m_scr, acc_scr):
    m_prev = m_scr[...]
    m_new = jnp.maximum(m_prev, jnp.max(s, axis=0, keepdims=True))
    alpha = jnp.exp2(m_prev - m_new)
    p = jnp.exp2(s - m_new)
    acc_scr[...] = alpha * acc_scr[...] + _dot(v_t, p.astype(BF16))
    m_scr[...] = m_new


def _flash_update_bounded(s, v_t, m_scr, acc_scr):
    acc_scr[...] += _dot(v_t, jnp.exp2(s - m_scr[...]).astype(BF16))


def _flash_result(acc_scr, dv):
    acc = acc_scr[...]
    return acc[:dv] * (1.0 / acc[dv:dv + 1])


BOUND_LIMIT = 56.0


def _logit_bound(q, k_norm_max):
    qf = q.astype(F32)
    return jnp.sqrt(jnp.sum(qf * qf, axis=0, keepdims=True)) * (k_norm_max * 1.02)


def _flash_bounded_or_online(bound, run, m_scr, acc_scr):
    acc_scr[...] = jnp.zeros(acc_scr.shape, F32)
    small = jnp.max(bound) <= BOUND_LIMIT

    @pl.when(small)
    def _():
        m_scr[...] = bound
        run(_flash_update_bounded)

    @pl.when(jnp.logical_not(small))
    def _():
        m_scr[...] = jnp.full(m_scr.shape, -jnp.inf, F32)
        run(_flash_update)


def _flash_causal(scores, values, mask, n_full, n_masked, s_scr, m_scr, acc_scr, update):
    unroll = s_scr.shape[0]
    s_scr[0] = scores(0)

    def trip(t, carry):
        j = unroll * t
        for u in range(unroll):
            s_scr[(u + 1) % unroll] = scores(j + u + 1)
            update(s_scr[u], values(j + u), m_scr, acc_scr)
        return carry

    lax.fori_loop(0, n_full // unroll, trip, 0)
    first = (n_full // unroll) * unroll
    for rest in range(unroll):

        @pl.when(n_full - first == rest)
        def _(rest=rest):
            for u in range(rest + n_masked):
                if u + 1 < rest + n_masked:
                    s_scr[(u + 1) % unroll] = scores(first + u + 1)
                s = s_scr[u % unroll]
                update(s if u < rest else mask(s, first + u), values(first + u), m_scr, acc_scr)


def _nsa_sel_kernel(kmax_ref, q_ref, k_ref, v_ref, sb_ref, kw_ref, vw_ref, o_ref, ow_ref, qa_scr, s_scr, m_scr, acc_scr,
                    *, tq, tk):
    b, g, i = pl.program_id(0), pl.program_id(1), pl.program_id(2)
    n = A_REP * tq
    q = q_ref[...]
    qs = jnp.concatenate([q[r * A_DIM:(r + 1) * A_DIM, :] for r in range(A_REP)], axis=1)
    sb = sb_ref[0].astype(BF16)
    qa_scr[...] = jnp.concatenate([qs, jnp.zeros_like(qs), jnp.concatenate([sb] * A_REP, axis=1)], axis=0)
    n_batch = pl.num_programs(0)

    def scores(j):
        return _dot(k_ref[0, pl.ds(pl.multiple_of(j * tk, tk), tk), :], qa_scr[...])

    def causal(s, j):
        kpos = j * tk + lax.broadcasted_iota(jnp.int32, (tk, n), 0)
        tok = i * tq + (lax.broadcasted_iota(jnp.int32, (tk, n), 1) & (tq - 1))
        return jnp.where(kpos <= tok, s, NEG)

    _flash_bounded_or_online(
        _logit_bound(qs, kmax_ref[g * n_batch + b]),
        lambda update: _flash_causal(scores, lambda j: v_ref[0, j], causal, (i * tq) // tk, max(1, tq // tk),
                                     s_scr, m_scr, acc_scr, update),
        m_scr, acc_scr)
    o = _flash_result(acc_scr, A_DIM)
    for r in range(A_REP):
        o_ref[r * A_DIM:(r + 1) * A_DIM, :] = o[:, r * tq:(r + 1) * tq]

    n_back = WINDOW // tq

    def window_tile(update, c):
        kt = i - n_back + c
        s = _dot(kw_ref[0, pl.ds(pl.multiple_of(kt * tq, tq), tq), :], qa_scr[0:128, :])
        if c in (0, n_back):
            kpos = kt * tq + lax.broadcasted_iota(jnp.int32, (tq, n), 0)
            tok = i * tq + (lax.broadcasted_iota(jnp.int32, (tq, n), 1) & (tq - 1))
            s = jnp.where(kpos > tok - WINDOW if c == 0 else kpos <= tok, s, NEG)
        update(s, vw_ref[0, kt], m_scr, acc_scr)

    def window(update):
        @pl.when(i >= n_back)
        def _():
            for c in range(n_back + 1):
                window_tile(update, c)

        @pl.when(i < n_back)
        def _():
            for c in range(n_back + 1):
                pl.when(i - n_back + c >= 0)(functools.partial(window_tile, update, c))

    _flash_bounded_or_online(_logit_bound(qs, kmax_ref[(A_GROUPS + g) * n_batch + b]), window, m_scr, acc_scr)
    o = _flash_result(acc_scr, A_DIM)
    for r in range(A_REP):
        ow_ref[r * A_DIM:(r + 1) * A_DIM, :] = o[:, r * tq:(r + 1) * tq]


def _nsa_sel(k_norm_max, q_t, k_aug, v_slc_tiles, selb, k_win, v_win_tiles, batch, seq, tq, tk):
    nq, nk = seq // tq, seq // tk
    nb = selb.shape[1]
    t = q_t.shape[1]
    gr = A_REP * A_DIM
    n = A_REP * tq
    kw = k_aug.shape[2]
    out = jax.ShapeDtypeStruct((A_HEADS * A_DIM, t), F32)
    grid_spec = pltpu.PrefetchScalarGridSpec(
        num_scalar_prefetch=1,
        grid=(batch, A_GROUPS, nq),
        in_specs=[pl.BlockSpec((gr, tq), lambda b, g, i, km: (g, b * nq + i)),
                  pl.BlockSpec((1, seq, kw), lambda b, g, i, km: (g, b, 0)),
                  pl.BlockSpec((1, nk, A_DIM + ONES_ROWS, tk), lambda b, g, i, km: (g, b, 0, 0)),
                  pl.BlockSpec((1, nb, tq), lambda b, g, i, km: (g, 0, b * nq + i)),
                  pl.BlockSpec((1, seq, 128), lambda b, g, i, km: (g, b, 0)),
                  pl.BlockSpec((1, nq, A_DIM + ONES_ROWS, tq), lambda b, g, i, km: (g, b, 0, 0))],
        out_specs=[pl.BlockSpec((gr, tq), lambda b, g, i, km: (g, b * nq + i)),
                   pl.BlockSpec((gr, tq), lambda b, g, i, km: (g, b * nq + i))],
        scratch_shapes=[pltpu.VMEM((kw, n), BF16), pltpu.VMEM((FLASH_UNROLL, tk, n), F32),
                        pltpu.VMEM((1, n), F32), pltpu.VMEM((A_DIM + ONES_ROWS, n), F32)],
    )
    return pl.pallas_call(
        functools.partial(_nsa_sel_kernel, tq=tq, tk=tk),
        grid_spec=grid_spec,
        out_shape=[out, out],
        compiler_params=_params(3),
        name="nsa_sel",
    )(k_norm_max, q_t, k_aug, v_slc_tiles, selb, k_win, v_win_tiles)


def _mla_prep_kernel(cq_ref, ckv_ref, kr_ref, cos_ref, sin_ref, qlg_ref, kvlg_ref, qg_ref, kg_ref, wuq_ref, wukv_ref,
                     qo_ref, ko_ref, vo_ref, kn_ref):
    cos, sin = cos_ref[...], sin_ref[...]
    tt = cos.shape[1]
    q_all = _dot(wuq_ref[...], (_rms_rows(cq_ref[...]) * qlg_ref[...]).astype(BF16))
    kv_all = _dot(wukv_ref[...], (_rms_rows(ckv_ref[...]) * kvlg_ref[...]).astype(BF16))
    kr = kr_ref[...]
    pad = jnp.zeros((128 - B_QK, tt), F32)
    for h in range(B_HEADS):
        y = _rms_rows(q_all[h * B_QK:(h + 1) * B_QK]) * qg_ref[...]
        y = _rope_rows(y, cos, sin, B_NOPE, B_ROPE // 2) * (B_QK ** -0.5 * LOG2E)
        qo_ref[h] = jnp.concatenate([y, pad], axis=0).astype(BF16)
        base = h * (B_NOPE + B_V)
        k = jnp.concatenate([kv_all[base:base + B_NOPE], kr], axis=0)
        y = _rope_rows(_rms_rows(k) * kg_ref[...], cos, sin, B_NOPE, B_ROPE // 2)
        kn_ref[h:h + 1, :] = jnp.sum(y * y, axis=0, keepdims=True)
        ko_ref[h] = jnp.concatenate([y, pad], axis=0).T.astype(BF16)
        vo_ref[h, 0] = _with_ones(kv_all[base + B_NOPE:base + B_NOPE + B_V])


def _mla_attn_kernel(kmax_ref, q_ref, k_ref, v_ref, o_ref, s_scr, m_scr, acc_scr, *, tq, tk):
    b, h, i = pl.program_id(0), pl.program_id(1), pl.program_id(2)

    def scores(j):
        return _dot(k_ref[0, pl.ds(pl.multiple_of(j * tk, tk), tk), :], q_ref[0])

    def causal(s, j):
        kpos = j * tk + lax.broadcasted_iota(jnp.int32, (tk, tq), 0)
        tok = i * tq + lax.broadcasted_iota(jnp.int32, (tk, tq), 1)
        return jnp.where(kpos <= tok, s, NEG)

    _flash_bounded_or_online(
        _logit_bound(q_ref[0], kmax_ref[h * pl.num_programs(0) + b]),
        lambda update: _flash_causal(scores, lambda j: v_ref[0, j], causal, (i * tq) // tk, max(1, tq // tk),
                                     s_scr, m_scr, acc_scr, update),
        m_scr, acc_scr)
    o_ref[...] = _flash_result(acc_scr, B_V)


def _mla_attn(k_norm_max, q_m, k_m, v_m_tiles, batch, seq, tq, tk):
    nq, nk = seq // tq, seq // tk
    t = q_m.shape[2]
    grid_spec = pltpu.PrefetchScalarGridSpec(
        num_scalar_prefetch=1,
        grid=(batch, B_HEADS, nq),
        in_specs=[pl.BlockSpec((1, 128, tq), lambda b, h, i, km: (h, 0, b * nq + i)),
                  pl.BlockSpec((1, seq, 128), lambda b, h, i, km: (h, b, 0)),
                  pl.BlockSpec((1, nk, B_V + ONES_ROWS, tk), lambda b, h, i, km: (h, b, 0, 0))],
        out_specs=pl.BlockSpec((B_V, tq), lambda b, h, i, km: (h, b * nq + i)),
        scratch_shapes=[pltpu.VMEM((FLASH_UNROLL, tk, tq), F32),
                        pltpu.VMEM((1, tq), F32), pltpu.VMEM((B_V + ONES_ROWS, tq), F32)],
    )
    return pl.pallas_call(
        functools.partial(_mla_attn_kernel, tq=tq, tk=tk),
        grid_spec=grid_spec,
        out_shape=jax.ShapeDtypeStruct((B_HEADS * B_V, t), F32),
        compiler_params=_params(3),
        name="mla_attn",
    )(k_norm_max, q_m, k_m, v_m_tiles)


def _out_proj_kernel(oc_ref, os_ref, ow_ref, gt_ref, ob_ref, x_ref, ga_ref, gb_ref, w_ref, g2_ref, h_ref, hn_ref):
    heads = []
    for h in range(A_HEADS):
        rows = slice(h * A_DIM, (h + 1) * A_DIM)
        heads.append(gt_ref[3 * h:3 * h + 1, :] * oc_ref[rows, :] + gt_ref[3 * h + 1:3 * h + 2, :] * os_ref[rows, :]
                     + gt_ref[3 * h + 2:3 * h + 3, :] * ow_ref[rows, :])
    oa = _rms_rows(jnp.concatenate(heads, axis=0)) * ga_ref[...]
    ob = _rms_rows(ob_ref[...]) * gb_ref[...]
    cat = jnp.concatenate([oa, ob], axis=0).astype(BF16)
    hid = x_ref[...].T + _dot(w_ref[...], cat)
    h_ref[...] = hid
    hn_ref[...] = (_rms_rows(hid) * g2_ref[...]).astype(BF16)


def _out_proj(oc_t, os_t, ow_t, gates_t, ob_t, x2, gain_a, gain_b, w_out_t, gain2, tt):
    t, d = x2.shape
    aw = oc_t.shape[0]
    bw = ob_t.shape[0]
    tok = lambda rows: pl.BlockSpec((rows, tt), lambda i: (0, i))
    full = lambda a: pl.BlockSpec(a.shape, lambda i: (0,) * a.ndim)
    return pl.pallas_call(
        _out_proj_kernel,
        grid=(t // tt,),
        in_specs=[tok(aw), tok(aw), tok(aw), tok(GATE_ROWS), tok(bw), pl.BlockSpec((tt, d), lambda i: (i, 0)),
                  full(gain_a), full(gain_b), full(w_out_t), full(gain2)],
        out_specs=[tok(d), tok(d)],
        out_shape=[jax.ShapeDtypeStruct((d, t), F32), jax.ShapeDtypeStruct((d, t), BF16)],
        compiler_params=_params(1),
        name="out_proj",
    )(oc_t, os_t, ow_t, gates_t, ob_t, x2, gain_a, gain_b, w_out_t, gain2)


def _top_ranked(s):
    n, rest = s.shape[0], s.shape[1:]
    row = lax.broadcasted_iota(jnp.int32, s.shape, 0).astype(F32)
    slot = lax.broadcasted_iota(jnp.int32, (P_TOPK,) + rest, 0)

    def body(a, carry):
        v, rank, vals = carry
        mx = jnp.max(v, axis=0, keepdims=True)
        first = jnp.min(jnp.where(v == mx, row, float(n)), axis=0, keepdims=True)
        hit = row == first
        rank = jnp.where(hit, jnp.asarray(a, F32), rank)
        v = jnp.where(hit, -jnp.inf, v)
        vals = jnp.where(slot == a, mx, vals)
        return v, rank, vals

    _, rank, vals = lax.fori_loop(0, P_TOPK, body,
                                  (s, jnp.full(s.shape, float(P_TOPK), F32), jnp.zeros((P_TOPK,) + rest, F32)))
    return rank, vals


def _pair_counts(v1, v2):
    k = v1.shape[0]
    slot = lax.broadcasted_iota(jnp.int32, v1.shape, 0).astype(F32)
    top = v1[0:1] + v2[0:1]

    def body(_, carry):
        count, front, z = carry
        mx = jnp.max(front, axis=0, keepdims=True)
        a_star = jnp.min(jnp.where(front == mx, slot, float(k)), axis=0, keepdims=True)
        hit = slot == a_star
        count = count + jnp.where(hit, 1.0, 0.0)
        nxt = jnp.sum(jnp.where(hit, count, 0.0), axis=0, keepdims=True)
        v2_nxt = jnp.sum(jnp.where(slot == nxt, v2, 0.0), axis=0, keepdims=True)
        front = jnp.where(hit, jnp.where(nxt < float(k), v1 + v2_nxt, -jnp.inf), front)
        return count, front, z + jnp.exp(mx - top)

    count, _, z = lax.fori_loop(0, k, body, (jnp.zeros(v1.shape, F32), v1 + v2[0:1], jnp.zeros(top.shape, F32)))
    return count, z


REMOVED = -2.0 ** 126
LANES = 128


def _top_ranked_pair_fast(s1, s2):
    n, tp = s1.shape
    slot = lax.broadcasted_iota(jnp.int32, (P_TOPK, tp), 0)

    def body(a, carry):
        code = REMOVED * (1.0 + jnp.asarray(a, F32) * (1.0 / 32.0))
        out = []
        for key, vals in (carry[0:2], carry[2:4]):
            mx = jnp.max(key, axis=0, keepdims=True)
            key = jnp.where(key == mx, code, key)
            out += [key, jnp.where(slot == a, mx, vals)]
        return tuple(out)

    zeros = jnp.zeros((P_TOPK, tp), F32)
    k1, t1, k2, t2 = lax.fori_loop(0, P_TOPK, body, (s1, zeros, s2, zeros))
    res, bad = [], jnp.zeros((1, tp), F32)
    for key, vals in ((k1, t1), (k2, t2)):
        removed = key <= REMOVED
        rank = jnp.where(removed, (key * (1.0 / REMOVED) - 1.0) * 32.0, float(P_TOPK))
        n_removed = jnp.sum(jnp.where(removed, 1.0, 0.0), axis=0, keepdims=True)
        bad = jnp.maximum(bad, jnp.abs(n_removed - float(P_TOPK)))
        res += [rank, vals]
    return res[0], res[1], res[2], res[3], bad


def _peer_route_kernel(hn_ref, wq_ref, keys_ref, cut_ref, g1_ref, r2_ref, g2_ref, q_scr, s_scr, rank_scr, vals_scr):
    tp = hn_ref.shape[1]
    half = P_KEY_DIM // 2
    q_scr[...] = _dot(wq_ref[...], hn_ref[...])
    for h in range(P_HEADS):
        q = _rms_rows(q_scr[h * P_KEY_DIM:(h + 1) * P_KEY_DIM, :]).astype(BF16)
        s_scr[0, h] = _dot(keys_ref[h, 0], q[:half])
        s_scr[1, h] = _dot(keys_ref[h, 1], q[half:])

    def first_level(rank_pair, flagged):
        for h in range(P_HEADS):
            for lt in range(tp // LANES):
                lanes = slice(lt * LANES, (lt + 1) * LANES)
                rank1, vals1, rank2, vals2, bad = rank_pair(s_scr[0, h, :, lanes], s_scr[1, h, :, lanes])
                flagged = jnp.maximum(flagged, bad)
                rank_scr[0, h, :, lanes] = rank1
                rank_scr[1, h, :, lanes] = rank2
                for a in range(P_TOPK):
                    vals_scr[0, a, h:h + 1, lanes] = vals1[a:a + 1]
                    vals_scr[1, a, h:h + 1, lanes] = vals2[a:a + 1]
        return flagged

    flagged = first_level(_top_ranked_pair_fast, jnp.zeros((1, LANES), F32))

    @pl.when(jnp.max(flagged) > 0.0)
    def _():
        first_level(lambda s1, s2: (*_top_ranked(s1), *_top_ranked(s2), jnp.zeros((1, LANES), F32)), flagged)

    count, z = _pair_counts(vals_scr[0], vals_scr[1])
    for h in range(P_HEADS):
        rank1 = rank_scr[0, h]
        cut = jnp.zeros_like(rank1)
        for a in range(P_TOPK):
            cut = jnp.where(rank1 == float(a), count[a, h:h + 1, :], cut)
        cut_ref[h] = cut.astype(BF16)
        g1_ref[h] = jnp.exp(s_scr[0, h] - vals_scr[0, 0, h:h + 1, :]).astype(BF16)
        r2_ref[h] = rank_scr[1, h].astype(BF16)
        g2_ref[h] = (jnp.exp(s_scr[1, h] - vals_scr[1, 0, h:h + 1, :]) * (1.0 / z[0, h:h + 1, :])).astype(BF16)


def _peer_route(hn_t, wq_t, sub_keys, tp):
    d, t = hn_t.shape
    halfs = jax.ShapeDtypeStruct((P_HEADS, N_KEYS, t), BF16)
    ospec = pl.BlockSpec((P_HEADS, N_KEYS, tp), lambda i: (0, 0, i))
    return pl.pallas_call(
        _peer_route_kernel,
        grid=(t // tp,),
        in_specs=[pl.BlockSpec((d, tp), lambda i: (0, i)),
                  pl.BlockSpec(wq_t.shape, lambda i: (0, 0)),
                  pl.BlockSpec(sub_keys.shape, lambda i: (0, 0, 0, 0))],
        out_specs=[ospec, ospec, ospec, ospec],
        out_shape=[halfs, halfs, halfs, halfs],
        scratch_shapes=[pltpu.VMEM((P_HEADS * P_KEY_DIM, tp), F32), pltpu.VMEM((2, P_HEADS, N_KEYS, tp), F32),
                        pltpu.VMEM((2, P_HEADS, N_KEYS, tp), F32), pltpu.VMEM((2, P_TOPK, P_HEADS, tp), F32)],
        compiler_params=_params(1),
        name="peer_route",
    )(hn_t, wq_t, sub_keys)


def _peer_ffn_kernel(hn_ref, h_ref, u_ref, v_ref, cut_ref, g1_ref, r2_ref, g2_ref, o_ref, acc_scr, *, te):
    e = pl.program_id(1)

    @pl.when(e == 0)
    def _():
        acc_scr[...] = jnp.zeros(acc_scr.shape, F32)

    hn = hn_ref[...]
    tt = hn.shape[1]

    def rows_bf16(ref, h, ii):
        return jnp.broadcast_to(ref[h, ii:ii + 1, :], (N_KEYS, tt))

    chunk = N_KEYS
    weights = []
    for c in range(te // chunk):
        a = _dot(u_ref[c * chunk:(c + 1) * chunk, :], hn)
        for k in range(chunk // N_KEYS):
            ii = c * (chunk // N_KEYS) + k
            gate = None
            for h in range(P_HEADS):
                chosen = r2_ref[h] < rows_bf16(cut_ref, h, ii)
                term = jnp.where(chosen, g2_ref[h], jnp.zeros((N_KEYS, tt), BF16)) * rows_bf16(g1_ref, h, ii)
                gate = term if gate is None else gate + term
            weights.append(gate * _gelu_tanh(a[k * N_KEYS:(k + 1) * N_KEYS, :]).astype(BF16))
    acc_scr[...] += _dot(v_ref[...], jnp.concatenate(weights, axis=0))

    @pl.when(e == pl.num_programs(1) - 1)
    def _():
        o_ref[...] = (h_ref[...] + acc_scr[...]).T


def _peer_ffn(hn_t, h_t, u_bf, v_t_bf, cut_k, g1_k, r2, g2, tt, te):
    d, t = hn_t.shape
    n_exp = u_bf.shape[0]
    kpe = te // N_KEYS
    return pl.pallas_call(
        functools.partial(_peer_ffn_kernel, te=te),
        grid=(t // tt, n_exp // te),
        in_specs=[pl.BlockSpec((d, tt), lambda i, e: (0, i)),
                  pl.BlockSpec((d, tt), lambda i, e: (0, i)),
                  pl.BlockSpec((te, d), lambda i, e: (e, 0)),
                  pl.BlockSpec((d, te), lambda i, e: (0, e)),
                  pl.BlockSpec((P_HEADS, kpe, tt), lambda i, e: (0, e, i)),
                  pl.BlockSpec((P_HEADS, kpe, tt), lambda i, e: (0, e, i)),
                  pl.BlockSpec((P_HEADS, N_KEYS, tt), lambda i, e: (0, 0, i)),
                  pl.BlockSpec((P_HEADS, N_KEYS, tt), lambda i, e: (0, 0, i))],
        out_specs=pl.BlockSpec((tt, d), lambda i, e: (i, 0)),
        out_shape=jax.ShapeDtypeStruct((t, d), F32),
        scratch_shapes=[pltpu.VMEM((d, tt), F32)],
        compiler_params=_params(2),
        name="peer_ffn",
    )(hn_t, h_t, u_bf, v_t_bf, cut_k, g1_k, r2, g2)


def _rope_tables_t(pos_flat, rot_dim):
    inv_freq = ROPE_THETA ** (-jnp.arange(0, rot_dim, 2, dtype=F32) / rot_dim)
    ang = pos_flat.astype(F32)[None, :] * inv_freq[:, None]
    return jnp.cos(ang), jnp.sin(ang)


def _expand_cmp_w1(w1):
    w = w1.reshape(CMP_LEN, A_DIM, CMP_HIDDEN)
    out = []
    for part in (w[:CMP_STRIDE], w[CMP_STRIDE:]):
        z = jnp.zeros_like(part)
        both = jnp.stack([jnp.concatenate([part, z], axis=1), jnp.concatenate([z, part], axis=1)])
        out.append(both.reshape(A_GROUPS, CMP_STRIDE * A_GROUPS * A_DIM, CMP_HIDDEN).transpose(0, 2, 1).astype(BF16))
    return out


TOKEN_TILE = 512
NSA_Q_TILE = 256
NSA_CMP_Q_TILE = 512
NSA_SEL_K_TILE = 512
MLA_Q_TILE = 1024
MLA_K_TILE = 512
PEER_ROUTE_TILE = 256
PEER_EXPERT_TILE = 2048


def _col(v):
    return v.reshape(-1, 1).astype(F32)


def _mixers(x, positions, norm1_gain, w_in, nsa_q_gain, nsa_kc_gain, nsa_ks_gain, nsa_kw_gain,
            cmp_pos, cmp_k_w1, cmp_k_w2, cmp_v_w1, cmp_v_w2,
            mla_q_lora_gain, mla_w_uq, mla_kv_lora_gain, mla_w_ukv, mla_q_gain, mla_k_gain):
    batch, seq, d = x.shape
    t = batch * seq
    tt = TOKEN_TILE
    tq_nsa = NSA_Q_TILE
    tk_sel = NSA_SEL_K_TILE
    tq_mla, tk_mla = MLA_Q_TILE, MLA_K_TILE
    assert tk_sel == tt and tk_mla == tt
    assert d == D_MODEL and seq // SLC_LEN >= SLC_TOPK and WINDOW % tq_nsa == 0
    assert all(seq % tile == 0 for tile in (tt, tq_nsa, tq_mla, NSA_CMP_Q_TILE))
    col = _col
    x2 = x.reshape(t, d)
    pos = positions.reshape(t)

    w_in_t = w_in.T
    gate_lo = sum((512, 128, 128, 128, 128, 128, 128))
    gate_hi = gate_lo + 3 * A_HEADS
    w_in_t = jnp.concatenate([w_in_t[:gate_lo], w_in_t[gate_hi:], w_in_t[gate_lo:gate_hi],
                              jnp.zeros((PROJ_ROWS - w_in_t.shape[0], d), F32)], axis=0).astype(BF16)
    cos_a, sin_a = _rope_tables_t(pos, A_ROPE)
    cos_b, sin_b = _rope_tables_t(pos, B_ROPE)
    q_t, kc_tm, vc_tm, k_slc, v_slc_t, k_win, v_win_t, gates_t, q_m, k_m, v_m_t, kn_a, kn_b = _in_proj(
        x2, norm1_gain.reshape(1, d), w_in_t, cos_a, sin_a, col(nsa_q_gain), col(nsa_ks_gain), col(nsa_kw_gain),
        cos_b, sin_b, col(mla_q_lora_gain), col(mla_kv_lora_gain), col(mla_q_gain), col(mla_k_gain),
        mla_w_uq.T.astype(BF16), mla_w_ukv.T.astype(BF16), tt, seq, tq_nsa)

    nc = seq // CMP_STRIDE
    chunk_w = CMP_STRIDE * A_GROUPS * A_DIM
    w1ka, w1kb = _expand_cmp_w1(cmp_k_w1)
    w1va, w1vb = _expand_cmp_w1(cmp_v_w1)
    pos_rows = lambda p: jnp.broadcast_to(p[:, None, :], (CMP_STRIDE, A_GROUPS, A_DIM)).reshape(1, chunk_w)
    cmp_end = jnp.minimum(jnp.arange(nc) * CMP_STRIDE + CMP_LEN - 1, seq - 1)
    cos_c, sin_c = _rope_tables_t(positions[:, cmp_end].reshape(-1), A_ROPE)
    to_b = lambda a: a.reshape(A_ROPE // 2, batch, nc).transpose(1, 0, 2)
    kcmp, vcmp_t = _compress(kc_tm.reshape(batch, nc, chunk_w), vc_tm.reshape(batch, nc, chunk_w),
                             w1ka, w1kb, w1va, w1vb, pos_rows(cmp_pos[:CMP_STRIDE]), pos_rows(cmp_pos[CMP_STRIDE:]),
                             cmp_k_w2.T.astype(BF16), cmp_v_w2.T.astype(BF16), col(nsa_kc_gain), to_b(cos_c), to_b(sin_c))

    n_cmp = (seq - CMP_LEN) // CMP_STRIDE + 1
    nb = seq // SLC_LEN
    c_start = np.arange(nc)[None, :] * CMP_STRIDE
    s_start = np.arange(nb)[:, None] * SLC_LEN
    ov = (c_start < s_start + SLC_LEN) & (c_start + CMP_LEN - 1 >= s_start) & (np.arange(nc)[None, :] < n_cmp)
    ov_t = jnp.asarray(ov.astype(np.float32)).astype(BF16)

    oc_t, selb = _nsa_cmp(q_t, kcmp, vcmp_t, ov_t, batch, seq, NSA_CMP_Q_TILE)
    norm_max = lambda kn: jnp.sqrt(jnp.max(kn.reshape(KEY_NORM_ROWS, batch, seq), axis=2)).reshape(-1)
    os_t, ow_t = _nsa_sel(norm_max(kn_a), q_t, k_slc, v_slc_t, selb, k_win, v_win_t, batch, seq, tq_nsa, tk_sel)

    ob_t = _mla_attn(norm_max(kn_b), q_m, k_m, v_m_t, batch, seq, tq_mla, tk_mla)
    return oc_t, os_t, ow_t, gates_t, ob_t


def _peer(hn_t, h_t, peer_w_q, peer_sub_keys, peer_u, peer_v):
    cut, g1, r2, g2 = _peer_route(hn_t, peer_w_q.T.astype(BF16), peer_sub_keys.astype(BF16), PEER_ROUTE_TILE)
    return _peer_ffn(hn_t, h_t, peer_u.astype(BF16), peer_v.T.astype(BF16),
                     cut, g1, r2, g2, TOKEN_TILE, PEER_EXPERT_TILE)


def _layer(x, positions, norm1_gain, w_in, nsa_q_gain, nsa_kc_gain, nsa_ks_gain, nsa_kw_gain,
           cmp_pos, cmp_k_w1, cmp_k_w2, cmp_v_w1, cmp_v_w2,
           mla_q_lora_gain, mla_w_uq, mla_kv_lora_gain, mla_w_ukv, mla_q_gain, mla_k_gain,
           out_gain_a, out_gain_b, w_out, norm2_gain, peer_w_q, peer_sub_keys, peer_u, peer_v):
    batch, seq, d = x.shape
    oc_t, os_t, ow_t, gates_t, ob_t = _mixers(
        x, positions, norm1_gain, w_in, nsa_q_gain, nsa_kc_gain, nsa_ks_gain, nsa_kw_gain,
        cmp_pos, cmp_k_w1, cmp_k_w2, cmp_v_w1, cmp_v_w2,
        mla_q_lora_gain, mla_w_uq, mla_kv_lora_gain, mla_w_ukv, mla_q_gain, mla_k_gain)
    h_t, hn_t = _out_proj(oc_t, os_t, ow_t, gates_t, ob_t, x.reshape(batch * seq, d), _col(out_gain_a), _col(out_gain_b),
                          w_out.T.astype(BF16), _col(norm2_gain), TOKEN_TILE)
    return _peer(hn_t, h_t, peer_w_q, peer_sub_keys, peer_u, peer_v).reshape(batch, seq, d)


def kernel(x, positions, norm1_gain, w_in, nsa_q_gain, nsa_kc_gain, nsa_ks_gain, nsa_kw_gain, cmp_pos, cmp_k_w1, cmp_k_w2, cmp_v_w1, cmp_v_w2, mla_q_lora_gain, mla_w_uq, mla_kv_lora_gain, mla_w_ukv, mla_q_gain, mla_k_gain, out_gain_a, out_gain_b, w_out, norm2_gain, peer_w_q, peer_sub_keys, peer_u, peer_v):
    h = x
    for l in range(norm1_gain.shape[0]):
        h = _layer(h, positions, norm1_gain[l], w_in[l], nsa_q_gain[l], nsa_kc_gain[l], nsa_ks_gain[l], nsa_kw_gain[l],
                   cmp_pos[l], cmp_k_w1[l], cmp_k_w2[l], cmp_v_w1[l], cmp_v_w2[l],
                   mla_q_lora_gain[l], mla_w_uq[l], mla_kv_lora_gain[l], mla_w_ukv[l], mla_q_gain[l], mla_k_gain[l],
                   out_gain_a[l], out_gain_b[l], w_out[l], norm2_gain[l], peer_w_q[l], peer_sub_keys[l],
                   peer_u[l], peer_v[l])
    return h
```

```python
import functools

import jax
import jax.numpy as jnp
import numpy as np
from jax import lax
from jax.experimental import pallas as pl
from jax.experimental.pallas import tpu as pltpu

F32, BF16 = jnp.float32, jnp.bfloat16
EPS = 1e-6
NEG = -1e30
FORCE = 1e9
ROPE_THETA = 500000.0
LOG2E = 1.4426950408889634

D_MODEL = 1024
A_HEADS, A_GROUPS, A_DIM = 8, 2, 64
A_REP = A_HEADS // A_GROUPS
A_ROPE = A_DIM // 4
CMP_LEN, CMP_STRIDE, CMP_HIDDEN = 32, 16, 256
SLC_LEN, SLC_TOPK, WINDOW = 64, 16, 512
B_HEADS, Q_LORA, KV_LORA, B_NOPE, B_ROPE, B_V = 8, 256, 128, 64, 32, 64
B_QK = B_NOPE + B_ROPE
P_HEADS, N_KEYS, P_KEY_DIM, P_TOPK = 8, 128, 256, 16
N_EXPERTS = N_KEYS * N_KEYS

ROW_Q, ROW_KC, ROW_VC, ROW_KS, ROW_VS, ROW_KW, ROW_VW = 0, 512, 640, 768, 896, 1024, 1152
ROW_CQ, ROW_CKV, ROW_KR, ROW_GATE, PROJ_ROWS = 1280, 1536, 1664, 1696, 1728
GATE_ROWS = 32
KEY_NORM_ROWS = 8

VMEM_LIMIT = 56 * 1024 * 1024
NT_DIMS = (((1,), (1,)), ((), ()))


def _params(n_axes):
    return pltpu.CompilerParams(dimension_semantics=("arbitrary",) * n_axes, vmem_limit_bytes=VMEM_LIMIT)


def _dot(a, b):
    return jnp.dot(a, b, preferred_element_type=F32)


def _dot_nt(a, b):
    return lax.dot_general(a, b, NT_DIMS, preferred_element_type=F32)


def _row_sumsq(x):
    sq = x * x
    hi = sq.astype(BF16)
    lo = (sq - hi.astype(F32)).astype(BF16)
    ones = jnp.ones((8, x.shape[1]), BF16)
    return (_dot_nt(ones, hi) + _dot_nt(ones, lo))[0:1, :]


def _rms_rows(x):
    ss = jnp.sum(x * x, axis=0, keepdims=True)
    return x * lax.rsqrt(ss * (1.0 / x.shape[0]) + EPS)


def _rope_rows(y, cos, sin, off, half):
    x1, x2 = y[off:off + half], y[off + half:off + 2 * half]
    parts = [y[:off]] if off else []
    parts += [x1 * cos - x2 * sin, x2 * cos + x1 * sin]
    if off + 2 * half < y.shape[0]:
        parts.append(y[off + 2 * half:])
    return jnp.concatenate(parts, axis=0)


def _gelu_tanh(x):
    k = -2.0 * 0.7978845608028654 * LOG2E
    return x / (1.0 + jnp.exp2(x * (k + (k * 0.044715) * (x * x))))


def _in_proj_kernel(x_ref, g_ref, w_ref, cos_a_ref, sin_a_ref, qg_ref, ksg_ref, kwg_ref,
                    cos_b_ref, sin_b_ref, qlg_ref, kvlg_ref, mqg_ref, mkg_ref, wuq_ref, wukv_ref, *out_refs, seq, tw):
    x = x_ref[...]
    xg = (x * g_ref[...]).astype(BF16)
    p = _dot_nt(w_ref[...], xg)
    p = p * lax.rsqrt(_row_sumsq(x) * (1.0 / x.shape[1]) + EPS)
    gw = A_GROUPS * A_DIM
    rows = lambda start, n: p[start:start + n]
    _nsa_prep_kernel(rows(ROW_Q, A_HEADS * A_DIM), rows(ROW_KC, gw), rows(ROW_VC, gw), rows(ROW_KS, gw), rows(ROW_VS, gw),
                     rows(ROW_KW, gw), rows(ROW_VW, gw), rows(ROW_GATE, GATE_ROWS), cos_a_ref, sin_a_ref,
                     qg_ref, ksg_ref, kwg_ref, *out_refs[:8], out_refs[11], seq=seq, tw=tw)
    _mla_prep_kernel(rows(ROW_CQ, Q_LORA), rows(ROW_CKV, KV_LORA), rows(ROW_KR, B_ROPE), cos_b_ref, sin_b_ref,
                     qlg_ref, kvlg_ref, mqg_ref, mkg_ref, wuq_ref, wukv_ref, *out_refs[8:11], out_refs[12])


def _in_proj(x2, gain, w_t, cos_a, sin_a, q_gain, ks_gain, kw_gain, cos_b, sin_b, q_lora_gain, kv_lora_gain,
             mq_gain, mk_gain, wuq_t, wukv_t, tt, seq, tw):
    t, d = x2.shape
    gw = A_GROUPS * A_DIM
    kw = 128 + seq // SLC_LEN
    va, vb = A_DIM + ONES_ROWS, B_V + ONES_ROWS
    full = lambda a: pl.BlockSpec(a.shape, lambda i: (0,) * a.ndim)
    lanes = lambda n: pl.BlockSpec((n, tt), lambda i: (0, i))
    return pl.pallas_call(
        functools.partial(_in_proj_kernel, seq=seq, tw=tw),
        grid=(t // tt,),
        in_specs=[pl.BlockSpec((tt, d), lambda i: (i, 0)), full(gain), full(w_t),
                  lanes(A_ROPE // 2), lanes(A_ROPE // 2), full(q_gain), full(ks_gain), full(kw_gain),
                  lanes(B_ROPE // 2), lanes(B_ROPE // 2), full(q_lora_gain), full(kv_lora_gain), full(mq_gain),
                  full(mk_gain), full(wuq_t), full(wukv_t)],
        out_specs=[lanes(A_HEADS * A_DIM),
                   pl.BlockSpec((tt, gw), lambda i: (i, 0)),
                   pl.BlockSpec((tt, gw), lambda i: (i, 0)),
                   pl.BlockSpec((A_GROUPS, tt, kw), lambda i: (0, i, 0)),
                   pl.BlockSpec((A_GROUPS, 1, va, tt), lambda i: (0, i, 0, 0)),
                   pl.BlockSpec((A_GROUPS, tt, 128), lambda i: (0, i, 0)),
                   pl.BlockSpec((A_GROUPS, tt // tw, va, tw), lambda i: (0, i, 0, 0)),
                   lanes(GATE_ROWS),
                   pl.BlockSpec((B_HEADS, 128, tt), lambda i: (0, 0, i)),
                   pl.BlockSpec((B_HEADS, tt, 128), lambda i: (0, i, 0)),
                   pl.BlockSpec((B_HEADS, 1, vb, tt), lambda i: (0, i, 0, 0)),
                   lanes(KEY_NORM_ROWS), lanes(KEY_NORM_ROWS)],
        out_shape=[jax.ShapeDtypeStruct((A_HEADS * A_DIM, t), BF16),
                   jax.ShapeDtypeStruct((t, gw), F32),
                   jax.ShapeDtypeStruct((t, gw), F32),
                   jax.ShapeDtypeStruct((A_GROUPS, t, kw), BF16),
                   jax.ShapeDtypeStruct((A_GROUPS, t // tt, va, tt), BF16),
                   jax.ShapeDtypeStruct((A_GROUPS, t, 128), BF16),
                   jax.ShapeDtypeStruct((A_GROUPS, t // tw, va, tw), BF16),
                   jax.ShapeDtypeStruct((GATE_ROWS, t), F32),
                   jax.ShapeDtypeStruct((B_HEADS, 128, t), BF16),
                   jax.ShapeDtypeStruct((B_HEADS, t, 128), BF16),
                   jax.ShapeDtypeStruct((B_HEADS, t // tt, vb, tt), BF16),
                   jax.ShapeDtypeStruct((KEY_NORM_ROWS, t), F32),
                   jax.ShapeDtypeStruct((KEY_NORM_ROWS, t), F32)],
        compiler_params=_params(1),
        name="in_proj",
    )(x2, gain, w_t, cos_a, sin_a, q_gain, ks_gain, kw_gain, cos_b, sin_b, q_lora_gain, kv_lora_gain,
      mq_gain, mk_gain, wuq_t, wukv_t)


def _nsa_prep_kernel(q_ref, kc_ref, vc_ref, ks_ref, vs_ref, kw_ref, vw_ref, gt_ref, cos_ref, sin_ref,
                     qg_ref, ksg_ref, kwg_ref,
                     qo_ref, kco_ref, vco_ref, kso_ref, vso_ref, kwo_ref, vwo_ref, gto_ref, kn_ref, *, seq, tw):
    cos, sin = cos_ref[...], sin_ref[...]
    tt = cos.shape[1]
    nb = seq // SLC_LEN
    for h in range(A_HEADS):
        y = _rms_rows(q_ref[h * A_DIM:(h + 1) * A_DIM, :]) * qg_ref[...]
        y = _rope_rows(y, cos, sin, 0, A_ROPE // 2) * (A_DIM ** -0.5 * LOG2E)
        qo_ref[h * A_DIM:(h + 1) * A_DIM, :] = y.astype(BF16)
    kco_ref[...] = kc_ref[...].T
    vco_ref[...] = vc_ref[...].T
    zeros = jnp.zeros((A_DIM, tt), F32)
    tok = pl.program_id(0) * tt + lax.broadcasted_iota(jnp.int32, (tt, nb), 0)
    block_hot = jnp.where(lax.broadcasted_iota(jnp.int32, (tt, nb), 1) == (tok % seq) // SLC_LEN, 1.0, 0.0)
    kn_ref[...] = jnp.zeros(kn_ref.shape, F32)
    for g in range(A_GROUPS):
        for branch, (src, gain, dst) in enumerate(((ks_ref, ksg_ref, kso_ref), (kw_ref, kwg_ref, kwo_ref))):
            y = _rms_rows(src[g * A_DIM:(g + 1) * A_DIM, :]) * gain[...]
            y = _rope_rows(y, cos, sin, 0, A_ROPE // 2)
            row = branch * A_GROUPS + g
            kn_ref[row:row + 1, :] = jnp.sum(y * y, axis=0, keepdims=True)
            k_tm = jnp.concatenate([y, zeros], axis=0).T
            if dst is kso_ref:
                k_tm = jnp.concatenate([k_tm, block_hot], axis=1)
            dst[g] = k_tm.astype(BF16)
        vso_ref[g, 0] = _with_ones(vs_ref[g * A_DIM:(g + 1) * A_DIM, :])
        v_win = _with_ones(vw_ref[g * A_DIM:(g + 1) * A_DIM, :])
        for c in range(tt // tw):
            vwo_ref[g, c] = v_win[:, c * tw:(c + 1) * tw]
    gto_ref[...] = 1.0 / (1.0 + jnp.exp(-gt_ref[...]))


def _compress_kernel(kc_ref, vc_ref, w1ka_ref, w1kb_ref, w1va_ref, w1vb_ref, plo_ref, phi_ref,
                     w2k_ref, w2v_ref, kg_ref, cos_ref, sin_ref, ko_ref, vo_ref):
    nc = kc_ref.shape[1]
    zeros = jnp.zeros((A_DIM, nc), F32)
    for src, w1a, w1b, w2, is_k in ((kc_ref, w1ka_ref, w1kb_ref, w2k_ref, True),
                                    (vc_ref, w1va_ref, w1vb_ref, w2v_ref, False)):
        x = src[0]
        xlo = (x + plo_ref[...]).astype(BF16)
        xhi = (x + phi_ref[...]).astype(BF16)
        for g in range(A_GROUPS):
            first = _dot_nt(w1a[g], xlo)
            second = _dot_nt(w1b[g], xhi)
            hid = _gelu_tanh(first + pltpu.roll(second, nc - 1, axis=1)).astype(BF16)
            c = _dot(w2[...], hid)
            if is_k:
                y = _rope_rows(_rms_rows(c) * kg_ref[...], cos_ref[0], sin_ref[0], 0, A_ROPE // 2)
                ko_ref[0, g] = jnp.concatenate([y, zeros], axis=0).T.astype(BF16)
            else:
                vo_ref[0, g] = _with_ones(c)


def _compress(kc_chunks, vc_chunks, w1ka, w1kb, w1va, w1vb, plo, phi, w2k_t, w2v_t, kc_gain, cos_c, sin_c):
    b, nc, cw = kc_chunks.shape
    full = lambda a: pl.BlockSpec(a.shape, lambda i: (0,) * a.ndim)
    return pl.pallas_call(
        _compress_kernel,
        grid=(b,),
        in_specs=[pl.BlockSpec((1, nc, cw), lambda i: (i, 0, 0)), pl.BlockSpec((1, nc, cw), lambda i: (i, 0, 0)),
                  full(w1ka), full(w1kb), full(w1va), full(w1vb), full(plo), full(phi), full(w2k_t), full(w2v_t),
                  full(kc_gain),
                  pl.BlockSpec((1, A_ROPE // 2, nc), lambda i: (i, 0, 0)),
                  pl.BlockSpec((1, A_ROPE // 2, nc), lambda i: (i, 0, 0))],
        out_specs=[pl.BlockSpec((1, A_GROUPS, nc, 128), lambda i: (i, 0, 0, 0)),
                   pl.BlockSpec((1, A_GROUPS, A_DIM + ONES_ROWS, nc), lambda i: (i, 0, 0, 0))],
        out_shape=[jax.ShapeDtypeStruct((b, A_GROUPS, nc, 128), BF16),
                   jax.ShapeDtypeStruct((b, A_GROUPS, A_DIM + ONES_ROWS, nc), BF16)],
        compiler_params=_params(1),
        name="nsa_compress",
    )(kc_chunks, vc_chunks, w1ka, w1kb, w1va, w1vb, plo, phi, w2k_t, w2v_t, kc_gain, cos_c, sin_c)


def _stack_heads(q, tq):
    qs = jnp.concatenate([q[r * A_DIM:(r + 1) * A_DIM, :] for r in range(A_REP)], axis=1)
    return jnp.concatenate([qs, jnp.zeros_like(qs)], axis=0)


def _nsa_cmp_kernel(q_ref, k_ref, v_ref, ov_ref, o_ref, sb_ref, *, tq):
    i = pl.program_id(2)
    n = A_REP * tq
    nc = k_ref.shape[2]
    nb = ov_ref.shape[0]
    qp = _stack_heads(q_ref[...], tq)
    s = _dot(k_ref[0, 0], qp)
    cmp_end = lax.broadcasted_iota(jnp.int32, (nc, n), 0) * CMP_STRIDE + (CMP_LEN - 1)
    tok = i * tq + (lax.broadcasted_iota(jnp.int32, (nc, n), 1) & (tq - 1))
    s = jnp.where(cmp_end <= tok, s, NEG)
    m = jnp.max(s, axis=0, keepdims=True)
    p = jnp.exp2(s - jnp.where(m > 0.5 * NEG, m, 0.0)).astype(BF16)
    ocl = _dot(v_ref[0, 0], p)
    l = ocl[A_DIM:A_DIM + 1]
    inv = jnp.where(l > 0.0, 1.0 / l, 0.0)
    oc = ocl[:A_DIM] * inv
    for r in range(A_REP):
        o_ref[r * A_DIM:(r + 1) * A_DIM, :] = oc[:, r * tq:(r + 1) * tq]
    imp4 = _dot(ov_ref[...], p) * inv
    imp = imp4[:, 0:tq]
    for r in range(1, A_REP):
        imp = imp + imp4[:, r * tq:(r + 1) * tq]

    blk = lax.broadcasted_iota(jnp.int32, (nb, tq), 0)
    t = i * tq + lax.broadcasted_iota(jnp.int32, (nb, tq), 1)
    forced = (blk == t // SLC_LEN) | (blk == 0)
    v0 = jnp.where(forced, FORCE, jnp.where(blk * SLC_LEN <= t, imp, NEG))
    topk = min(SLC_TOPK, nb)

    v = jnp.where(blk == t // SLC_LEN, 2.0 * FORCE, v0)
    for _ in range(topk):
        v = jnp.where(v == jnp.max(v, axis=0, keepdims=True), -jnp.inf, v)
    taken = v == -jnp.inf
    sb_ref[0] = jnp.where(taken, 0.0, NEG)
    n_taken = jnp.sum(jnp.where(taken, 1.0, 0.0), axis=0, keepdims=True)

    @pl.when(jnp.max(jnp.abs(n_taken - float(topk))) > 0.0)
    def _():
        blk_f = blk.astype(F32)
        w = v0
        sel = jnp.zeros((nb, tq), F32)
        for _ in range(topk):
            mx = jnp.max(w, axis=0, keepdims=True)
            first = jnp.min(jnp.where(w == mx, blk_f, float(nb)), axis=0, keepdims=True)
            hit = blk_f == first
            sel = jnp.where(hit, 1.0, sel)
            w = jnp.where(hit, -jnp.inf, w)
        sb_ref[0] = jnp.where(sel > 0.0, 0.0, NEG)


def _nsa_cmp(q_t, kcmp, vcmp_t, ov_t, batch, seq, tq):
    nq = seq // tq
    nc = kcmp.shape[2]
    nb = ov_t.shape[0]
    t = q_t.shape[1]
    gr = A_REP * A_DIM
    return pl.pallas_call(
        functools.partial(_nsa_cmp_kernel, tq=tq),
        grid=(batch, A_GROUPS, nq),
        in_specs=[pl.BlockSpec((gr, tq), lambda b, g, i: (g, b * nq + i)),
                  pl.BlockSpec((1, 1, nc, 128), lambda b, g, i: (b, g, 0, 0)),
                  pl.BlockSpec((1, 1, A_DIM + ONES_ROWS, nc), lambda b, g, i: (b, g, 0, 0)),
                  pl.BlockSpec((nb, nc), lambda b, g, i: (0, 0))],
        out_specs=[pl.BlockSpec((gr, tq), lambda b, g, i: (g, b * nq + i)),
                   pl.BlockSpec((1, nb, tq), lambda b, g, i: (g, 0, b * nq + i))],
        out_shape=[jax.ShapeDtypeStruct((A_HEADS * A_DIM, t), F32),
                   jax.ShapeDtypeStruct((A_GROUPS, nb, t), F32)],
        compiler_params=_params(3),
        name="nsa_cmp",
    )(q_t, kcmp, vcmp_t, ov_t)


ONES_ROWS = 16
FLASH_UNROLL = 4


def _with_ones(v):
    return jnp.concatenate([v, jnp.ones((ONES_ROWS, v.shape[1]), F32)], axis=0).astype(BF16)


def _flash_update(s, v_t, m_scr, acc_scr):
    m_prev = m_scr[...]
    m_new = jnp.maximum(m_prev, jnp.max(s, axis=0, keepdims=True))
    alpha = jnp.exp2(m_prev - m_new)
    p = jnp.exp2(s - m_new)
    acc_scr[...] = alpha * acc_scr[...] + _dot(v_t, p.astype(BF16))
    m_scr[...] = m_new


def _flash_update_bounded(s, v_t, m_scr, acc_scr):
    acc_scr[...] += _dot(v_t, jnp.exp2(s - m_scr[...]).astype(BF16))


def _flash_result(acc_scr, dv):
    acc = acc_scr[...]
    return acc[:dv] * (1.0 / acc[dv:dv + 1])


BOUND_LIMIT = 56.0


def _logit_bound(q, k_norm_max):
    qf = q.astype(F32)
    return jnp.sqrt(jnp.sum(qf * qf, axis=0, keepdims=True)) * (k_norm_max * 1.02)


def _flash_bounded_or_online(bound, run, m_scr, acc_scr):
    acc_scr[...] = jnp.zeros(acc_scr.shape, F32)
    small = jnp.max(bound) <= BOUND_LIMIT

    @pl.when(small)
    def _():
        m_scr[...] = bound
        run(_flash_update_bounded)

    @pl.when(jnp.logical_not(small))
    def _():
        m_scr[...] = jnp.full(m_scr.shape, -jnp.inf, F32)
        run(_flash_update)


def _flash_causal(scores, values, mask, n_full, n_masked, s_scr, m_scr, acc_scr, update):
    unroll = s_scr.shape[0]
    s_scr[0] = scores(0)

    def trip(t, carry):
        j = unroll * t
        for u in range(unroll):
            s_scr[(u + 1) % unroll] = scores(j + u + 1)
            update(s_scr[u], values(j + u), m_scr, acc_scr)
        return carry

    lax.fori_loop(0, n_full // unroll, trip, 0)
    first = (n_full // unroll) * unroll
    for rest in range(unroll):

        @pl.when(n_full - first == rest)
        def _(rest=rest):
            for u in range(rest + n_masked):
                if u + 1 < rest + n_masked:
                    s_scr[(u + 1) % unroll] = scores(first + u + 1)
                s = s_scr[u % unroll]
                update(s if u < rest else mask(s, first + u), values(first + u), m_scr, acc_scr)


def _nsa_sel_kernel(kmax_ref, q_ref, k_ref, v_ref, sb_ref, kw_ref, vw_ref, o_ref, ow_ref, qa_scr, s_scr, m_scr, acc_scr,
                    *, tq, tk):
    b, g, i = pl.program_id(0), pl.program_id(1), pl.program_id(2)
    n = A_REP * tq
    q = q_ref[...]
    qs = jnp.concatenate([q[r * A_DIM:(r + 1) * A_DIM, :] for r in range(A_REP)], axis=1)
    sb = sb_ref[0].astype(BF16)
    qa_scr[...] = jnp.concatenate([qs, jnp.zeros_like(qs), jnp.concatenate([sb] * A_REP, axis=1)], axis=0)
    n_batch = pl.num_programs(0)

    def scores(j):
        return _dot(k_ref[0, pl.ds(pl.multiple_of(j * tk, tk), tk), :], qa_scr[...])

    def causal(s, j):
        kpos = j * tk + lax.broadcasted_iota(jnp.int32, (tk, n), 0)
        tok = i * tq + (lax.broadcasted_iota(jnp.int32, (tk, n), 1) & (tq - 1))
        return jnp.where(kpos <= tok, s, NEG)

    _flash_bounded_or_online(
        _logit_bound(qs, kmax_ref[g * n_batch + b]),
        lambda update: _flash_causal(scores, lambda j: v_ref[0, j], causal, (i * tq) // tk, max(1, tq // tk),
                                     s_scr, m_scr, acc_scr, update),
        m_scr, acc_scr)
    o = _flash_result(acc_scr, A_DIM)
    for r in range(A_REP):
        o_ref[r * A_DIM:(r + 1) * A_DIM, :] = o[:, r * tq:(r + 1) * tq]

    n_back = WINDOW // tq

    def window_tile(update, c):
        kt = i - n_back + c
        s = _dot(kw_ref[0, pl.ds(pl.multiple_of(kt * tq, tq), tq), :], qa_scr[0:128, :])
        if c in (0, n_back):
            kpos = kt * tq + lax.broadcasted_iota(jnp.int32, (tq, n), 0)
            tok = i * tq + (lax.broadcasted_iota(jnp.int32, (tq, n), 1) & (tq - 1))
            s = jnp.where(kpos > tok - WINDOW if c == 0 else kpos <= tok, s, NEG)
        update(s, vw_ref[0, kt], m_scr, acc_scr)

    def window(update):
        @pl.when(i >= n_back)
        def _():
            for c in range(n_back + 1):
                window_tile(update, c)

        @pl.when(i < n_back)
        def _():
            for c in range(n_back + 1):
                pl.when(i - n_back + c >= 0)(functools.partial(window_tile, update, c))

    _flash_bounded_or_online(_logit_bound(qs, kmax_ref[(A_GROUPS + g) * n_batch + b]), window, m_scr, acc_scr)
    o = _flash_result(acc_scr, A_DIM)
    for r in range(A_REP):
        ow_ref[r * A_DIM:(r + 1) * A_DIM, :] = o[:, r * tq:(r + 1) * tq]


def _nsa_sel(k_norm_max, q_t, k_aug, v_slc_tiles, selb, k_win, v_win_tiles, batch, seq, tq, tk):
    nq, nk = seq // tq, seq // tk
    nb = selb.shape[1]
    t = q_t.shape[1]
    gr = A_REP * A_DIM
    n = A_REP * tq
    kw = k_aug.shape[2]
    out = jax.ShapeDtypeStruct((A_HEADS * A_DIM, t), F32)
    grid_spec = pltpu.PrefetchScalarGridSpec(
        num_scalar_prefetch=1,
        grid=(batch, A_GROUPS, nq),
        in_specs=[pl.BlockSpec((gr, tq), lambda b, g, i, km: (g, b * nq + i)),
                  pl.BlockSpec((1, seq, kw), lambda b, g, i, km: (g, b, 0)),
                  pl.BlockSpec((1, nk, A_DIM + ONES_ROWS, tk), lambda b, g, i, km: (g, b, 0, 0)),
                  pl.BlockSpec((1, nb, tq), lambda b, g, i, km: (g, 0, b * nq + i)),
                  pl.BlockSpec((1, seq, 128), lambda b, g, i, km: (g, b, 0)),
                  pl.BlockSpec((1, nq, A_DIM + ONES_ROWS, tq), lambda b, g, i, km: (g, b, 0, 0))],
        out_specs=[pl.BlockSpec((gr, tq), lambda b, g, i, km: (g, b * nq + i)),
                   pl.BlockSpec((gr, tq), lambda b, g, i, km: (g, b * nq + i))],
        scratch_shapes=[pltpu.VMEM((kw, n), BF16), pltpu.VMEM((FLASH_UNROLL, tk, n), F32),
                        pltpu.VMEM((1, n), F32), pltpu.VMEM((A_DIM + ONES_ROWS, n), F32)],
    )
    return pl.pallas_call(
        functools.partial(_nsa_sel_kernel, tq=tq, tk=tk),
        grid_spec=grid_spec,
        out_shape=[out, out],
        compiler_params=_params(3),
        name="nsa_sel",
    )(k_norm_max, q_t, k_aug, v_slc_tiles, selb, k_win, v_win_tiles)


def _mla_prep_kernel(cq_ref, ckv_ref, kr_ref, cos_ref, sin_ref, qlg_ref, kvlg_ref, qg_ref, kg_ref, wuq_ref, wukv_ref,
                     qo_ref, ko_ref, vo_ref, kn_ref):
    cos, sin = cos_ref[...], sin_ref[...]
    tt = cos.shape[1]
    q_all = _dot(wuq_ref[...], (_rms_rows(cq_ref[...]) * qlg_ref[...]).astype(BF16))
    kv_all = _dot(wukv_ref[...], (_rms_rows(ckv_ref[...]) * kvlg_ref[...]).astype(BF16))
    kr = kr_ref[...]
    pad = jnp.zeros((128 - B_QK, tt), F32)
    for h in range(B_HEADS):
        y = _rms_rows(q_all[h * B_QK:(h + 1) * B_QK]) * qg_ref[...]
        y = _rope_rows(y, cos, sin, B_NOPE, B_ROPE // 2) * (B_QK ** -0.5 * LOG2E)
        qo_ref[h] = jnp.concatenate([y, pad], axis=0).astype(BF16)
        base = h * (B_NOPE + B_V)
        k = jnp.concatenate([kv_all[base:base + B_NOPE], kr], axis=0)
        y = _rope_rows(_rms_rows(k) * kg_ref[...], cos, sin, B_NOPE, B_ROPE // 2)
        kn_ref[h:h + 1, :] = jnp.sum(y * y, axis=0, keepdims=True)
        ko_ref[h] = jnp.concatenate([y, pad], axis=0).T.astype(BF16)
        vo_ref[h, 0] = _with_ones(kv_all[base + B_NOPE:base + B_NOPE + B_V])


def _mla_attn_kernel(kmax_ref, q_ref, k_ref, v_ref, o_ref, s_scr, m_scr, acc_scr, *, tq, tk):
    b, h, i = pl.program_id(0), pl.program_id(1), pl.program_id(2)

    def scores(j):
        return _dot(k_ref[0, pl.ds(pl.multiple_of(j * tk, tk), tk), :], q_ref[0])

    def causal(s, j):
        kpos = j * tk + lax.broadcasted_iota(jnp.int32, (tk, tq), 0)
        tok = i * tq + lax.broadcasted_iota(jnp.int32, (tk, tq), 1)
        return jnp.where(kpos <= tok, s, NEG)

    _flash_bounded_or_online(
        _logit_bound(q_ref[0], kmax_ref[h * pl.num_programs(0) + b]),
        lambda update: _flash_causal(scores, lambda j: v_ref[0, j], causal, (i * tq) // tk, max(1, tq // tk),
                                     s_scr, m_scr, acc_scr, update),
        m_scr, acc_scr)
    o_ref[...] = _flash_result(acc_scr, B_V)


def _mla_attn(k_norm_max, q_m, k_m, v_m_tiles, batch, seq, tq, tk):
    nq, nk = seq // tq, seq // tk
    t = q_m.shape[2]
    grid_spec = pltpu.PrefetchScalarGridSpec(
        num_scalar_prefetch=1,
        grid=(batch, B_HEADS, nq),
        in_specs=[pl.BlockSpec((1, 128, tq), lambda b, h, i, km: (h, 0, b * nq + i)),
                  pl.BlockSpec((1, seq, 128), lambda b, h, i, km: (h, b, 0)),
                  pl.BlockSpec((1, nk, B_V + ONES_ROWS, tk), lambda b, h, i, km: (h, b, 0, 0))],
        out_specs=pl.BlockSpec((B_V, tq), lambda b, h, i, km: (h, b * nq + i)),
        scratch_shapes=[pltpu.VMEM((FLASH_UNROLL, tk, tq), F32),
                        pltpu.VMEM((1, tq), F32), pltpu.VMEM((B_V + ONES_ROWS, tq), F32)],
    )
    return pl.pallas_call(
        functools.partial(_mla_attn_kernel, tq=tq, tk=tk),
        grid_spec=grid_spec,
        out_shape=jax.ShapeDtypeStruct((B_HEADS * B_V, t), F32),
        compiler_params=_params(3),
        name="mla_attn",
    )(k_norm_max, q_m, k_m, v_m_tiles)


def _out_proj_kernel(oc_ref, os_ref, ow_ref, gt_ref, ob_ref, x_ref, ga_ref, gb_ref, w_ref, g2_ref, h_ref, hn_ref):
    heads = []
    for h in range(A_HEADS):
        rows = slice(h * A_DIM, (h + 1) * A_DIM)
        heads.append(gt_ref[3 * h:3 * h + 1, :] * oc_ref[rows, :] + gt_ref[3 * h + 1:3 * h + 2, :] * os_ref[rows, :]
                     + gt_ref[3 * h + 2:3 * h + 3, :] * ow_ref[rows, :])
    oa = _rms_rows(jnp.concatenate(heads, axis=0)) * ga_ref[...]
    ob = _rms_rows(ob_ref[...]) * gb_ref[...]
    cat = jnp.concatenate([oa, ob], axis=0).astype(BF16)
    hid = x_ref[...].T + _dot(w_ref[...], cat)
    h_ref[...] = hid
    hn_ref[...] = (_rms_rows(hid) * g2_ref[...]).astype(BF16)


def _out_proj(oc_t, os_t, ow_t, gates_t, ob_t, x2, gain_a, gain_b, w_out_t, gain2, tt):
    t, d = x2.shape
    aw = oc_t.shape[0]
    bw = ob_t.shape[0]
    tok = lambda rows: pl.BlockSpec((rows, tt), lambda i: (0, i))
    full = lambda a: pl.BlockSpec(a.shape, lambda i: (0,) * a.ndim)
    return pl.pallas_call(
        _out_proj_kernel,
        grid=(t // tt,),
        in_specs=[tok(aw), tok(aw), tok(aw), tok(GATE_ROWS), tok(bw), pl.BlockSpec((tt, d), lambda i: (i, 0)),
                  full(gain_a), full(gain_b), full(w_out_t), full(gain2)],
        out_specs=[tok(d), tok(d)],
        out_shape=[jax.ShapeDtypeStruct((d, t), F32), jax.ShapeDtypeStruct((d, t), BF16)],
        compiler_params=_params(1),
        name="out_proj",
    )(oc_t, os_t, ow_t, gates_t, ob_t, x2, gain_a, gain_b, w_out_t, gain2)


def _top_ranked(s):
    n, rest = s.shape[0], s.shape[1:]
    row = lax.broadcasted_iota(jnp.int32, s.shape, 0).astype(F32)
    slot = lax.broadcasted_iota(jnp.int32, (P_TOPK,) + rest, 0)

    def body(a, carry):
        v, rank, vals = carry
        mx = jnp.max(v, axis=0, keepdims=True)
        first = jnp.min(jnp.where(v == mx, row, float(n)), axis=0, keepdims=True)
        hit = row == first
        rank = jnp.where(hit, jnp.asarray(a, F32), rank)
        v = jnp.where(hit, -jnp.inf, v)
        vals = jnp.where(slot == a, mx, vals)
        return v, rank, vals

    _, rank, vals = lax.fori_loop(0, P_TOPK, body,
                                  (s, jnp.full(s.shape, float(P_TOPK), F32), jnp.zeros((P_TOPK,) + rest, F32)))
    return rank, vals


def _pair_counts(v1, v2):
    k = v1.shape[0]
    slot = lax.broadcasted_iota(jnp.int32, v1.shape, 0).astype(F32)
    top = v1[0:1] + v2[0:1]

    def body(_, carry):
        count, front, z = carry
        mx = jnp.max(front, axis=0, keepdims=True)
        a_star = jnp.min(jnp.where(front == mx, slot, float(k)), axis=0, keepdims=True)
        hit = slot == a_star
        count = count + jnp.where(hit, 1.0, 0.0)
        nxt = jnp.sum(jnp.where(hit, count, 0.0), axis=0, keepdims=True)
        v2_nxt = jnp.sum(jnp.where(slot == nxt, v2, 0.0), axis=0, keepdims=True)
        front = jnp.where(hit, jnp.where(nxt < float(k), v1 + v2_nxt, -jnp.inf), front)
        return count, front, z + jnp.exp(mx - top)

    count, _, z = lax.fori_loop(0, k, body, (jnp.zeros(v1.shape, F32), v1 + v2[0:1], jnp.zeros(top.shape, F32)),
                                unroll=4)
    return count, z


REMOVED = -2.0 ** 126
LANES = 128


def _top_ranked_pair_fast(s1, s2):
    n, tp = s1.shape
    slot = lax.broadcasted_iota(jnp.int32, (P_TOPK, tp), 0)

    def body(a, carry):
        code = REMOVED * (1.0 + jnp.asarray(a, F32) * (1.0 / 32.0))
        out = []
        for key, vals in (carry[0:2], carry[2:4]):
            mx = jnp.max(key, axis=0, keepdims=True)
            key = jnp.where(key == mx, code, key)
            out += [key, jnp.where(slot == a, mx, vals)]
        return tuple(out)

    zeros = jnp.zeros((P_TOPK, tp), F32)
    k1, t1, k2, t2 = lax.fori_loop(0, P_TOPK, body, (s1, zeros, s2, zeros), unroll=4)
    res, bad = [], jnp.zeros((1, tp), F32)
    for key, vals in ((k1, t1), (k2, t2)):
        removed = key <= REMOVED
        rank = jnp.where(removed, (key * (1.0 / REMOVED) - 1.0) * 32.0, float(P_TOPK))
        n_removed = jnp.sum(jnp.where(removed, 1.0, 0.0), axis=0, keepdims=True)
        bad = jnp.maximum(bad, jnp.abs(n_removed - float(P_TOPK)))
        res += [rank, vals]
    return res[0], res[1], res[2], res[3], bad


def _peer_route_kernel(hn_ref, wq_ref, keys_ref, cut_ref, g1_ref, r2_ref, g2_ref, q_scr, s_scr, rank_scr, vals_scr):
    tp = hn_ref.shape[1]
    half = P_KEY_DIM // 2
    q_scr[...] = _dot(wq_ref[...], hn_ref[...])
    for h in range(P_HEADS):
        q = _rms_rows(q_scr[h * P_KEY_DIM:(h + 1) * P_KEY_DIM, :]).astype(BF16)
        s_scr[0, h] = _dot(keys_ref[h, 0], q[:half])
        s_scr[1, h] = _dot(keys_ref[h, 1], q[half:])

    def first_level(rank_pair, flagged):
        for h in range(P_HEADS):
            for lt in range(tp // LANES):
                lanes = slice(lt * LANES, (lt + 1) * LANES)
                rank1, vals1, rank2, vals2, bad = rank_pair(s_scr[0, h, :, lanes], s_scr[1, h, :, lanes])
                flagged = jnp.maximum(flagged, bad)
                rank_scr[0, h, :, lanes] = rank1
                rank_scr[1, h, :, lanes] = rank2
                for a in range(P_TOPK):
                    vals_scr[0, a, h:h + 1, lanes] = vals1[a:a + 1]
                    vals_scr[1, a, h:h + 1, lanes] = vals2[a:a + 1]
        return flagged

    flagged = first_level(_top_ranked_pair_fast, jnp.zeros((1, LANES), F32))

    @pl.when(jnp.max(flagged) > 0.0)
    def _():
        first_level(lambda s1, s2: (*_top_ranked(s1), *_top_ranked(s2), jnp.zeros((1, LANES), F32)), flagged)

    count, z = _pair_counts(vals_scr[0], vals_scr[1])
    for h in range(P_HEADS):
        rank1 = rank_scr[0, h]
        cut = jnp.zeros_like(rank1)
        for a in range(P_TOPK):
            cut = jnp.where(rank1 == float(a), count[a, h:h + 1, :], cut)
        cut_ref[h] = cut.astype(BF16)
        g1_ref[h] = jnp.exp(s_scr[0, h] - vals_scr[0, 0, h:h + 1, :]).astype(BF16)
        r2_ref[h] = rank_scr[1, h].astype(BF16)
        g2_ref[h] = (jnp.exp(s_scr[1, h] - vals_scr[1, 0, h:h + 1, :]) * (1.0 / z[0, h:h + 1, :])).astype(BF16)


def _peer_route(hn_t, wq_t, sub_keys, tp):
    d, t = hn_t.shape
    halfs = jax.ShapeDtypeStruct((P_HEADS, N_KEYS, t), BF16)
    ospec = pl.BlockSpec((P_HEADS, N_KEYS, tp), lambda i: (0, 0, i))
    return pl.pallas_call(
        _peer_route_kernel,
        grid=(t // tp,),
        in_specs=[pl.BlockSpec((d, tp), lambda i: (0, i)),
                  pl.BlockSpec(wq_t.shape, lambda i: (0, 0)),
                  pl.BlockSpec(sub_keys.shape, lambda i: (0, 0, 0, 0))],
        out_specs=[ospec, ospec, ospec, ospec],
        out_shape=[halfs, halfs, halfs, halfs],
        scratch_shapes=[pltpu.VMEM((P_HEADS * P_KEY_DIM, tp), F32), pltpu.VMEM((2, P_HEADS, N_KEYS, tp), F32),
                        pltpu.VMEM((2, P_HEADS, N_KEYS, tp), F32), pltpu.VMEM((2, P_TOPK, P_HEADS, tp), F32)],
        compiler_params=_params(1),
        name="peer_route",
    )(hn_t, wq_t, sub_keys)


def _peer_ffn_kernel(hn_ref, h_ref, u_ref, v_ref, cut_ref, g1_ref, r2_ref, g2_ref, o_ref, acc_scr, *, te):
    e = pl.program_id(1)

    @pl.when(e == 0)
    def _():
        acc_scr[...] = jnp.zeros(acc_scr.shape, F32)

    hn = hn_ref[...]
    tt = hn.shape[1]

    def rows_bf16(ref, h, ii):
        return jnp.broadcast_to(ref[h, ii:ii + 1, :], (N_KEYS, tt))

    chunk = N_KEYS
    weights = []
    for c in range(te // chunk):
        a = _dot(u_ref[c * chunk:(c + 1) * chunk, :], hn)
        for k in range(chunk // N_KEYS):
            ii = c * (chunk // N_KEYS) + k
            gate = None
            for h in range(P_HEADS):
                chosen = r2_ref[h] < rows_bf16(cut_ref, h, ii)
                term = jnp.where(chosen, g2_ref[h], jnp.zeros((N_KEYS, tt), BF16)) * rows_bf16(g1_ref, h, ii)
                gate = term if gate is None else gate + term
            weights.append(gate * _gelu_tanh(a[k * N_KEYS:(k + 1) * N_KEYS, :]).astype(BF16))
    acc_scr[...] += _dot(v_ref[...], jnp.concatenate(weights, axis=0))

    @pl.when(e == pl.num_programs(1) - 1)
    def _():
        o_ref[...] = (h_ref[...] + acc_scr[...]).T


def _peer_ffn(hn_t, h_t, u_bf, v_t_bf, cut_k, g1_k, r2, g2, tt, te):
    d, t = hn_t.shape
    n_exp = u_bf.shape[0]
    kpe = te // N_KEYS
    return pl.pallas_call(
        functools.partial(_peer_ffn_kernel, te=te),
        grid=(t // tt, n_exp // te),
        in_specs=[pl.BlockSpec((d, tt), lambda i, e: (0, i)),
                  pl.BlockSpec((d, tt), lambda i, e: (0, i)),
                  pl.BlockSpec((te, d), lambda i, e: (e, 0)),
                  pl.BlockSpec((d, te), lambda i, e: (0, e)),
                  pl.BlockSpec((P_HEADS, kpe, tt), lambda i, e: (0, e, i)),
                  pl.BlockSpec((P_HEADS, kpe, tt), lambda i, e: (0, e, i)),
                  pl.BlockSpec((P_HEADS, N_KEYS, tt), lambda i, e: (0, 0, i)),
                  pl.BlockSpec((P_HEADS, N_KEYS, tt), lambda i, e: (0, 0, i))],
        out_specs=pl.BlockSpec((tt, d), lambda i, e: (i, 0)),
        out_shape=jax.ShapeDtypeStruct((t, d), F32),
        scratch_shapes=[pltpu.VMEM((d, tt), F32)],
        compiler_params=_params(2),
        name="peer_ffn",
    )(hn_t, h_t, u_bf, v_t_bf, cut_k, g1_k, r2, g2)


def _rope_tables_t(pos_flat, rot_dim):
    inv_freq = ROPE_THETA ** (-jnp.arange(0, rot_dim, 2, dtype=F32) / rot_dim)
    ang = pos_flat.astype(F32)[None, :] * inv_freq[:, None]
    return jnp.cos(ang), jnp.sin(ang)


def _expand_cmp_w1(w1):
    w = w1.reshape(CMP_LEN, A_DIM, CMP_HIDDEN)
    out = []
    for part in (w[:CMP_STRIDE], w[CMP_STRIDE:]):
        z = jnp.zeros_like(part)
        both = jnp.stack([jnp.concatenate([part, z], axis=1), jnp.concatenate([z, part], axis=1)])
        out.append(both.reshape(A_GROUPS, CMP_STRIDE * A_GROUPS * A_DIM, CMP_HIDDEN).transpose(0, 2, 1).astype(BF16))
    return out


TOKEN_TILE = 512
NSA_Q_TILE = 256
NSA_CMP_Q_TILE = 512
NSA_SEL_K_TILE = 512
MLA_Q_TILE = 1024
MLA_K_TILE = 512
PEER_ROUTE_TILE = 256
PEER_EXPERT_TILE = 2048


def _col(v):
    return v.reshape(-1, 1).astype(F32)


def _mixers(x, positions, norm1_gain, w_in, nsa_q_gain, nsa_kc_gain, nsa_ks_gain, nsa_kw_gain,
            cmp_pos, cmp_k_w1, cmp_k_w2, cmp_v_w1, cmp_v_w2,
            mla_q_lora_gain, mla_w_uq, mla_kv_lora_gain, mla_w_ukv, mla_q_gain, mla_k_gain):
    batch, seq, d = x.shape
    t = batch * seq
    tt = TOKEN_TILE
    tq_nsa = NSA_Q_TILE
    tk_sel = NSA_SEL_K_TILE
    tq_mla, tk_mla = MLA_Q_TILE, MLA_K_TILE
    assert tk_sel == tt and tk_mla == tt
    assert d == D_MODEL and seq // SLC_LEN >= SLC_TOPK and WINDOW % tq_nsa == 0
    assert all(seq % tile == 0 for tile in (tt, tq_nsa, tq_mla, NSA_CMP_Q_TILE))
    col = _col
    x2 = x.reshape(t, d)
    pos = positions.reshape(t)

    w_in_t = w_in.T
    gate_lo = sum((512, 128, 128, 128, 128, 128, 128))
    gate_hi = gate_lo + 3 * A_HEADS
    w_in_t = jnp.concatenate([w_in_t[:gate_lo], w_in_t[gate_hi:], w_in_t[gate_lo:gate_hi],
                              jnp.zeros((PROJ_ROWS - w_in_t.shape[0], d), F32)], axis=0).astype(BF16)
    cos_a, sin_a = _rope_tables_t(pos, A_ROPE)
    cos_b, sin_b = _rope_tables_t(pos, B_ROPE)
    q_t, kc_tm, vc_tm, k_slc, v_slc_t, k_win, v_win_t, gates_t, q_m, k_m, v_m_t, kn_a, kn_b = _in_proj(
        x2, norm1_gain.reshape(1, d), w_in_t, cos_a, sin_a, col(nsa_q_gain), col(nsa_ks_gain), col(nsa_kw_gain),
        cos_b, sin_b, col(mla_q_lora_gain), col(mla_kv_lora_gain), col(mla_q_gain), col(mla_k_gain),
        mla_w_uq.T.astype(BF16), mla_w_ukv.T.astype(BF16), tt, seq, tq_nsa)

    nc = seq // CMP_STRIDE
    chunk_w = CMP_STRIDE * A_GROUPS * A_DIM
    w1ka, w1kb = _expand_cmp_w1(cmp_k_w1)
    w1va, w1vb = _expand_cmp_w1(cmp_v_w1)
    pos_rows = lambda p: jnp.broadcast_to(p[:, None, :], (CMP_STRIDE, A_GROUPS, A_DIM)).reshape(1, chunk_w)
    cmp_end = jnp.minimum(jnp.arange(nc) * CMP_STRIDE + CMP_LEN - 1, seq - 1)
    cos_c, sin_c = _rope_tables_t(positions[:, cmp_end].reshape(-1), A_ROPE)
    to_b = lambda a: a.reshape(A_ROPE // 2, batch, nc).transpose(1, 0, 2)
    kcmp, vcmp_t = _compress(kc_tm.reshape(batch, nc, chunk_w), vc_tm.reshape(batch, nc, chunk_w),
                             w1ka, w1kb, w1va, w1vb, pos_rows(cmp_pos[:CMP_STRIDE]), pos_rows(cmp_pos[CMP_STRIDE:]),
                             cmp_k_w2.T.astype(BF16), cmp_v_w2.T.astype(BF16), col(nsa_kc_gain), to_b(cos_c), to_b(sin_c))

    n_cmp = (seq - CMP_LEN) // CMP_STRIDE + 1
    nb = seq // SLC_LEN
    c_start = np.arange(nc)[None, :] * CMP_STRIDE
    s_start = np.arange(nb)[:, None] * SLC_LEN
    ov = (c_start < s_start + SLC_LEN) & (c_start + CMP_LEN - 1 >= s_start) & (np.arange(nc)[None, :] < n_cmp)
    ov_t = jnp.asarray(ov.astype(np.float32)).astype(BF16)

    oc_t, selb = _nsa_cmp(q_t, kcmp, vcmp_t, ov_t, batch, seq, NSA_CMP_Q_TILE)
    norm_max = lambda kn: jnp.sqrt(jnp.max(kn.reshape(KEY_NORM_ROWS, batch, seq), axis=2)).reshape(-1)
    os_t, ow_t = _nsa_sel(norm_max(kn_a), q_t, k_slc, v_slc_t, selb, k_win, v_win_t, batch, seq, tq_nsa, tk_sel)

    ob_t = _mla_attn(norm_max(kn_b), q_m, k_m, v_m_t, batch, seq, tq_mla, tk_mla)
    return oc_t, os_t, ow_t, gates_t, ob_t


def _peer(hn_t, h_t, peer_w_q, peer_sub_keys, peer_u, peer_v):
    cut, g1, r2, g2 = _peer_route(hn_t, peer_w_q.T.astype(BF16), peer_sub_keys.astype(BF16), PEER_ROUTE_TILE)
    return _peer_ffn(hn_t, h_t, peer_u.astype(BF16), peer_v.T.astype(BF16),
                     cut, g1, r2, g2, TOKEN_TILE, PEER_EXPERT_TILE)


def _layer(x, positions, norm1_gain, w_in, nsa_q_gain, nsa_kc_gain, nsa_ks_gain, nsa_kw_gain,
           cmp_pos, cmp_k_w1, cmp_k_w2, cmp_v_w1, cmp_v_w2,
           mla_q_lora_gain, mla_w_uq, mla_kv_lora_gain, mla_w_ukv, mla_q_gain, mla_k_gain,
           out_gain_a, out_gain_b, w_out, norm2_gain, peer_w_q, peer_sub_keys, peer_u, peer_v):
    batch, seq, d = x.shape
    oc_t, os_t, ow_t, gates_t, ob_t = _mixers(
        x, positions, norm1_gain, w_in, nsa_q_gain, nsa_kc_gain, nsa_ks_gain, nsa_kw_gain,
        cmp_pos, cmp_k_w1, cmp_k_w2, cmp_v_w1, cmp_v_w2,
        mla_q_lora_gain, mla_w_uq, mla_kv_lora_gain, mla_w_ukv, mla_q_gain, mla_k_gain)
    h_t, hn_t = _out_proj(oc_t, os_t, ow_t, gates_t, ob_t, x.reshape(batch * seq, d), _col(out_gain_a), _col(out_gain_b),
                          w_out.T.astype(BF16), _col(norm2_gain), TOKEN_TILE)
    return _peer(hn_t, h_t, peer_w_q, peer_sub_keys, peer_u, peer_v).reshape(batch, seq, d)


def kernel(x, positions, norm1_gain, w_in, nsa_q_gain, nsa_kc_gain, nsa_ks_gain, nsa_kw_gain, cmp_pos, cmp_k_w1, cmp_k_w2, cmp_v_w1, cmp_v_w2, mla_q_lora_gain, mla_w_uq, mla_kv_lora_gain, mla_w_ukv, mla_q_gain, mla_k_gain, out_gain_a, out_gain_b, w_out, norm2_gain, peer_w_q, peer_sub_keys, peer_u, peer_v):
    h = x
    for l in range(norm1_gain.shape[0]):
        h = _layer(h, positions, norm1_gain[l], w_in[l], nsa_q_gain[l], nsa_kc_gain[l], nsa_ks_gain[l], nsa_kw_gain[l],
                   cmp_pos[l], cmp_k_w1[l], cmp_k_w2[l], cmp_v_w1[l], cmp_v_w2[l],
                   mla_q_lora_gain[l], mla_w_uq[l], mla_kv_lora_gain[l], mla_w_ukv[l], mla_q_gain[l], mla_k_gain[l],
                   out_gain_a[l], out_gain_b[l], w_out[l], norm2_gain[l], peer_w_q[l], peer_sub_keys[l],
                   peer_u[l], peer_v[l])
    return h
```
